```python
import jax, jax.numpy as jnp
from jax import lax
import numpy as np

D_MODEL = 1024
BATCH = 8
SEQ = 8192
DEPTH = 2

MEM_LEN = 256
D_FF = 2816
N_EVEN = (DEPTH + 1) // 2
N_ODD = DEPTH // 2

CONV_A_CH = 512
CONV_A_WIDTH = 31
SWA_HEADS = 8
SWA_KV_HEADS = 2
SWA_GROUP = SWA_HEADS // SWA_KV_HEADS
HEAD_DIM = 64
WINDOW = 128
BLOCK = 128
EVEN_IN = 2 * CONV_A_CH + (SWA_HEADS + 2 * SWA_KV_HEADS) * HEAD_DIM
EVEN_MIX = CONV_A_CH + SWA_HEADS * HEAD_DIM
SC_CH = 1024
SC_WIDTH = 3
XA_HEADS = 4
XA_HEAD_DIM = D_MODEL // XA_HEADS

RMS_EPS = 1e-6
LN_EPS = 1e-5

kernel_name = "hybrid_conformer_swa_shortconv_macaron"


def rmsnorm(x, g):
    x32 = x.astype(jnp.float32)
    y = x32 * lax.rsqrt(jnp.mean(x32 * x32, axis=-1, keepdims=True) + RMS_EPS)
    return y.astype(x.dtype) * g


def layernorm(x, g, b):
    x32 = x.astype(jnp.float32)
    mu = jnp.mean(x32, axis=-1, keepdims=True)
    var = jnp.mean(jnp.square(x32 - mu), axis=-1, keepdims=True)
    y = (x32 - mu) * lax.rsqrt(var + LN_EPS)
    return y.astype(x.dtype) * g + b


def swiglu(u, w_gu, w_down):
    gu = u @ w_gu
    return (jax.nn.silu(gu[..., :D_FF]) * gu[..., D_FF:]) @ w_down


def causal_depthwise_conv(x, w):
    k_width, ch = w.shape
    return lax.conv_general_dilated(
        x, w[:, None, :].astype(x.dtype), window_strides=(1,), padding=[(k_width - 1, 0)],
        dimension_numbers=("NWC", "WIO", "NWC"), feature_group_count=ch)


def alibi_slopes(n_heads):
    return 2.0 ** (-8.0 * jnp.arange(1, n_heads + 1, dtype=jnp.float32) / n_heads)


def conformer_conv(a_val, a_gate, conv_w, conv_b, ln_g, ln_b):
    a = a_val * jax.nn.sigmoid(a_gate)
    a = causal_depthwise_conv(a, conv_w) + conv_b
    return jax.nn.silu(layernorm(a, ln_g, ln_b))


def sliding_window_gqa(q, k, v, sinks):
    bsz, seq = q.shape[:2]
    nb = seq // BLOCK
    qb = q.reshape(bsz, nb, BLOCK, SWA_KV_HEADS, SWA_GROUP, HEAD_DIM)
    kb = k.reshape(bsz, nb, BLOCK, SWA_KV_HEADS, HEAD_DIM)
    vb = v.reshape(bsz, nb, BLOCK, SWA_KV_HEADS, HEAD_DIM)
    pad = ((0, 0), (1, 0), (0, 0), (0, 0), (0, 0))
    kk = jnp.concatenate([jnp.pad(kb, pad)[:, :-1], kb], axis=2)
    vv = jnp.concatenate([jnp.pad(vb, pad)[:, :-1], vb], axis=2)
    scores = jnp.einsum("bnqkgd,bnskd->bnkgqs", qb, kk).astype(jnp.float32) * (HEAD_DIM ** -0.5)
    dist = jnp.arange(BLOCK)[:, None] + BLOCK - jnp.arange(2 * BLOCK)[None, :]
    key_pos = jnp.arange(nb)[:, None] * BLOCK - BLOCK + jnp.arange(2 * BLOCK)[None, :]
    valid = ((dist >= 0) & (dist < WINDOW))[None] & (key_pos >= 0)[:, None, :]
    slopes = alibi_slopes(SWA_HEADS).reshape(SWA_KV_HEADS, SWA_GROUP)
    scores = scores - slopes[:, :, None, None] * dist.astype(jnp.float32)
    scores = jnp.where(valid[None, :, None, None], scores, -jnp.inf)
    sink = jnp.broadcast_to(
        sinks.astype(jnp.float32).reshape(SWA_KV_HEADS, SWA_GROUP)[None, None, :, :, None, None],
        scores.shape[:-1] + (1,))
    probs = jax.nn.softmax(jnp.concatenate([scores, sink], axis=-1), axis=-1)[..., :-1]
    out = jnp.einsum("bnkgqs,bnskd->bnqkgd", probs.astype(vv.dtype), vv)
    return out.reshape(bsz, seq, SWA_HEADS * HEAD_DIM)


def even_mixer(u, w_in, conv_w, conv_b, ln_g, ln_b, sinks, w_out):
    bsz, seq, _ = u.shape
    z = u @ w_in
    o0 = CONV_A_CH
    o1 = o0 + CONV_A_CH
    o2 = o1 + SWA_HEADS * HEAD_DIM
    o3 = o2 + SWA_KV_HEADS * HEAD_DIM
    a = conformer_conv(z[..., :o0], z[..., o0:o1], conv_w, conv_b, ln_g, ln_b)
    q = z[..., o1:o2].reshape(bsz, seq, SWA_KV_HEADS, SWA_GROUP, HEAD_DIM)
    k = z[..., o2:o3].reshape(bsz, seq, SWA_KV_HEADS, HEAD_DIM)
    v = z[..., o3:].reshape(bsz, seq, SWA_KV_HEADS, HEAD_DIM)
    o = sliding_window_gqa(q, k, v, sinks)
    return jnp.concatenate([a, o], axis=-1) @ w_out


def odd_mixer(u, w_in, conv_w, w_out):
    z = u @ w_in
    gate_b = z[..., :SC_CH]
    gate_c = z[..., SC_CH:2 * SC_CH]
    val = z[..., 2 * SC_CH:]
    y = gate_b * causal_depthwise_conv(gate_c * val, conv_w)
    return y @ w_out


def memory_cross_attention(u, m, wq, wkv, wo):
    bsz, seq, _ = u.shape
    mlen = m.shape[1]
    q = (u @ wq).reshape(bsz, seq, XA_HEADS, XA_HEAD_DIM)
    kv = m @ wkv
    k = kv[..., :D_MODEL].reshape(bsz, mlen, XA_HEADS, XA_HEAD_DIM)
    v = kv[..., D_MODEL:].reshape(bsz, mlen, XA_HEADS, XA_HEAD_DIM)
    s = jnp.einsum("bqhd,bkhd->bhqk", q, k).astype(jnp.float32) * (XA_HEAD_DIM ** -0.5)
    p = jax.nn.softmax(s, axis=-1).astype(v.dtype)
    o = jnp.einsum("bhqk,bkhd->bqhd", p, v).reshape(bsz, seq, D_MODEL)
    return o @ wo


def _normal(key, shape, fan_in):
    return jax.random.normal(key, shape, jnp.float32) * (fan_in ** -0.5)


def _gain(key, shape):
    return 1.0 + 0.05 * jax.random.normal(key, shape, jnp.float32)


def _fwd_setup_inputs(seed: int = 0) -> dict:
    key = jax.random.key(seed)
    ks = jax.random.split(key, 32)
    D, F = D_MODEL, D_FF
    return {
        "x": jax.random.normal(ks[0], (BATCH, SEQ, D), jnp.float32),
        "mem": jax.random.normal(ks[1], (BATCH, MEM_LEN, D), jnp.float32),
        "ffn1_norm": _gain(ks[2], (DEPTH, D)),
        "ffn1_w_gu": _normal(ks[3], (DEPTH, D, 2 * F), D),
        "ffn1_w_down": _normal(ks[4], (DEPTH, F, D), F),
        "mix_norm": _gain(ks[5], (DEPTH, D)),
        "even_w_in": _normal(ks[6], (N_EVEN, D, EVEN_IN), D),
        "conv_a_w": _normal(ks[7], (N_EVEN, CONV_A_WIDTH, CONV_A_CH), CONV_A_WIDTH),
        "conv_a_b": 0.02 * jax.random.normal(ks[8], (N_EVEN, CONV_A_CH), jnp.float32),
        "conv_a_ln_g": _gain(ks[9], (N_EVEN, CONV_A_CH)),
        "conv_a_ln_b": 0.02 * jax.random.normal(ks[10], (N_EVEN, CONV_A_CH), jnp.float32),
        "swa_sinks": 0.5 * jax.random.normal(ks[11], (N_EVEN, SWA_HEADS), jnp.float32),
        "even_w_out": _normal(ks[12], (N_EVEN, EVEN_MIX, D), EVEN_MIX),
        "odd_w_in": _normal(ks[13], (N_ODD, D, 3 * SC_CH), D),
        "sc_conv_w": _normal(ks[14], (N_ODD, SC_WIDTH, SC_CH), SC_WIDTH),
        "odd_w_out": _normal(ks[15], (N_ODD, SC_CH, D), SC_CH),
        "xa_norm": _gain(ks[16], (DEPTH, D)),
        "xa_mem_norm": _gain(ks[17], (DEPTH, D)),
        "xa_wq": _normal(ks[18], (DEPTH, D, D), D),
        "xa_wkv": _normal(ks[19], (DEPTH, D, 2 * D), D),
        "xa_wo": _normal(ks[20], (DEPTH, D, D), D),
        "ffn2_norm": _gain(ks[21], (DEPTH, D)),
        "ffn2_w_gu": _normal(ks[22], (DEPTH, D, 2 * F), D),
        "ffn2_w_down": _normal(ks[23], (DEPTH, F, D), F),
        "final_norm": _gain(ks[24], (D,)),
    }


def _fwd_reference(x, mem, ffn1_norm, ffn1_w_gu, ffn1_w_down, mix_norm, even_w_in, conv_a_w, conv_a_b,
              conv_a_ln_g, conv_a_ln_b, swa_sinks, even_w_out, odd_w_in, sc_conv_w, odd_w_out,
              xa_norm, xa_mem_norm, xa_wq, xa_wkv, xa_wo, ffn2_norm, ffn2_w_gu, ffn2_w_down, final_norm):
    h = x
    for i in range(DEPTH):
        h = h + 0.5 * swiglu(rmsnorm(h, ffn1_norm[i]), ffn1_w_gu[i], ffn1_w_down[i])
        u = rmsnorm(h, mix_norm[i])
        j = i // 2
        if i % 2 == 0:
            h = h + even_mixer(u, even_w_in[j], conv_a_w[j], conv_a_b[j], conv_a_ln_g[j],
                               conv_a_ln_b[j], swa_sinks[j], even_w_out[j])
        else:
            h = h + odd_mixer(u, odd_w_in[j], sc_conv_w[j], odd_w_out[j])
        h = h + memory_cross_attention(rmsnorm(h, xa_norm[i]), rmsnorm(mem, xa_mem_norm[i]),
                                       xa_wq[i], xa_wkv[i], xa_wo[i])
        h = h + 0.5 * swiglu(rmsnorm(h, ffn2_norm[i]), ffn2_w_gu[i], ffn2_w_down[i])
    return rmsnorm(h, final_norm)


import jax as _jax
import jax.numpy as _jnp

TWIN_FORMAT = 'train_step'
FWD_PARAMS = ['x', 'mem', 'ffn1_norm', 'ffn1_w_gu', 'ffn1_w_down', 'mix_norm', 'even_w_in', 'conv_a_w', 'conv_a_b', 'conv_a_ln_g', 'conv_a_ln_b', 'swa_sinks', 'even_w_out', 'odd_w_in', 'sc_conv_w', 'odd_w_out', 'xa_norm', 'xa_mem_norm', 'xa_wq', 'xa_wkv', 'xa_wo', 'ffn2_norm', 'ffn2_w_gu', 'ffn2_w_down', 'final_norm']
TWIN_WEIGHTS = ['ffn1_norm', 'ffn1_w_gu', 'ffn1_w_down', 'mix_norm', 'even_w_in', 'conv_a_w', 'conv_a_b', 'conv_a_ln_g', 'conv_a_ln_b', 'swa_sinks', 'even_w_out', 'odd_w_in', 'sc_conv_w', 'odd_w_out', 'xa_norm', 'xa_mem_norm', 'xa_wq', 'xa_wkv', 'xa_wo', 'ffn2_norm', 'ffn2_w_gu', 'ffn2_w_down', 'final_norm']
TWIN_DIFF_INPUT = 'x'
TWIN_INPUTS = ['x', 'mem', 'ffn1_norm', 'ffn1_w_gu', 'ffn1_w_down', 'mix_norm', 'even_w_in', 'conv_a_w', 'conv_a_b', 'conv_a_ln_g', 'conv_a_ln_b', 'swa_sinks', 'even_w_out', 'odd_w_in', 'sc_conv_w', 'odd_w_out', 'xa_norm', 'xa_mem_norm', 'xa_wq', 'xa_wkv', 'xa_wo', 'ffn2_norm', 'ffn2_w_gu', 'ffn2_w_down', 'final_norm', 'loss_target', 'm_ffn1_norm', 'm_ffn1_w_gu', 'm_ffn1_w_down', 'm_mix_norm', 'm_even_w_in', 'm_conv_a_w', 'm_conv_a_b', 'm_conv_a_ln_g', 'm_conv_a_ln_b', 'm_swa_sinks', 'm_even_w_out', 'm_odd_w_in', 'm_sc_conv_w', 'm_odd_w_out', 'm_xa_norm', 'm_xa_mem_norm', 'm_xa_wq', 'm_xa_wkv', 'm_xa_wo', 'm_ffn2_norm', 'm_ffn2_w_gu', 'm_ffn2_w_down', 'm_final_norm', 'v_ffn1_norm', 'v_ffn1_w_gu', 'v_ffn1_w_down', 'v_mix_norm', 'v_even_w_in', 'v_conv_a_w', 'v_conv_a_b', 'v_conv_a_ln_g', 'v_conv_a_ln_b', 'v_swa_sinks', 'v_even_w_out', 'v_odd_w_in', 'v_sc_conv_w', 'v_odd_w_out', 'v_xa_norm', 'v_xa_mem_norm', 'v_xa_wq', 'v_xa_wkv', 'v_xa_wo', 'v_ffn2_norm', 'v_ffn2_w_gu', 'v_ffn2_w_down', 'v_final_norm']
TWIN_OUTPUTS = ['loss', 'grad_x', 'grad_ffn1_norm', 'grad_ffn1_w_gu', 'grad_ffn1_w_down', 'grad_mix_norm', 'grad_even_w_in', 'grad_conv_a_w', 'grad_conv_a_b', 'grad_conv_a_ln_g', 'grad_conv_a_ln_b', 'grad_swa_sinks', 'grad_even_w_out', 'grad_odd_w_in', 'grad_sc_conv_w', 'grad_odd_w_out', 'grad_xa_norm', 'grad_xa_mem_norm', 'grad_xa_wq', 'grad_xa_wkv', 'grad_xa_wo', 'grad_ffn2_norm', 'grad_ffn2_w_gu', 'grad_ffn2_w_down', 'grad_final_norm', 'delta_ffn1_norm', 'delta_ffn1_w_gu', 'delta_ffn1_w_down', 'delta_mix_norm', 'delta_even_w_in', 'delta_conv_a_w', 'delta_conv_a_b', 'delta_conv_a_ln_g', 'delta_conv_a_ln_b', 'delta_swa_sinks', 'delta_even_w_out', 'delta_odd_w_in', 'delta_sc_conv_w', 'delta_odd_w_out', 'delta_xa_norm', 'delta_xa_mem_norm', 'delta_xa_wq', 'delta_xa_wkv', 'delta_xa_wo', 'delta_ffn2_norm', 'delta_ffn2_w_gu', 'delta_ffn2_w_down', 'delta_final_norm', 'new_m_ffn1_norm', 'new_m_ffn1_w_gu', 'new_m_ffn1_w_down', 'new_m_mix_norm', 'new_m_even_w_in', 'new_m_conv_a_w', 'new_m_conv_a_b', 'new_m_conv_a_ln_g', 'new_m_conv_a_ln_b', 'new_m_swa_sinks', 'new_m_even_w_out', 'new_m_odd_w_in', 'new_m_sc_conv_w', 'new_m_odd_w_out', 'new_m_xa_norm', 'new_m_xa_mem_norm', 'new_m_xa_wq', 'new_m_xa_wkv', 'new_m_xa_wo', 'new_m_ffn2_norm', 'new_m_ffn2_w_gu', 'new_m_ffn2_w_down', 'new_m_final_norm', 'new_v_ffn1_norm', 'new_v_ffn1_w_gu', 'new_v_ffn1_w_down', 'new_v_mix_norm', 'new_v_even_w_in', 'new_v_conv_a_w', 'new_v_conv_a_b', 'new_v_conv_a_ln_g', 'new_v_conv_a_ln_b', 'new_v_swa_sinks', 'new_v_even_w_out', 'new_v_odd_w_in', 'new_v_sc_conv_w', 'new_v_odd_w_out', 'new_v_xa_norm', 'new_v_xa_mem_norm', 'new_v_xa_wq', 'new_v_xa_wkv', 'new_v_xa_wo', 'new_v_ffn2_norm', 'new_v_ffn2_w_gu', 'new_v_ffn2_w_down', 'new_v_final_norm']
TWIN_LEAF_KINDS = {'loss': 'loss', 'grad_x': 'grad_x', 'grad_ffn1_norm': 'grad_w', 'grad_ffn1_w_gu': 'grad_w', 'grad_ffn1_w_down': 'grad_w', 'grad_mix_norm': 'grad_w', 'grad_even_w_in': 'grad_w', 'grad_conv_a_w': 'grad_w', 'grad_conv_a_b': 'grad_w', 'grad_conv_a_ln_g': 'grad_w', 'grad_conv_a_ln_b': 'grad_w', 'grad_swa_sinks': 'grad_w', 'grad_even_w_out': 'grad_w', 'grad_odd_w_in': 'grad_w', 'grad_sc_conv_w': 'grad_w', 'grad_odd_w_out': 'grad_w', 'grad_xa_norm': 'grad_w', 'grad_xa_mem_norm': 'grad_w', 'grad_xa_wq': 'grad_w', 'grad_xa_wkv': 'grad_w', 'grad_xa_wo': 'grad_w', 'grad_ffn2_norm': 'grad_w', 'grad_ffn2_w_gu': 'grad_w', 'grad_ffn2_w_down': 'grad_w', 'grad_final_norm': 'grad_w', 'delta_ffn1_norm': 'delta_w', 'delta_ffn1_w_gu': 'delta_w', 'delta_ffn1_w_down': 'delta_w', 'delta_mix_norm': 'delta_w', 'delta_even_w_in': 'delta_w', 'delta_conv_a_w': 'delta_w', 'delta_conv_a_b': 'delta_w', 'delta_conv_a_ln_g': 'delta_w', 'delta_conv_a_ln_b': 'delta_w', 'delta_swa_sinks': 'delta_w', 'delta_even_w_out': 'delta_w', 'delta_odd_w_in': 'delta_w', 'delta_sc_conv_w': 'delta_w', 'delta_odd_w_out': 'delta_w', 'delta_xa_norm': 'delta_w', 'delta_xa_mem_norm': 'delta_w', 'delta_xa_wq': 'delta_w', 'delta_xa_wkv': 'delta_w', 'delta_xa_wo': 'delta_w', 'delta_ffn2_norm': 'delta_w', 'delta_ffn2_w_gu': 'delta_w', 'delta_ffn2_w_down': 'delta_w', 'delta_final_norm': 'delta_w', 'new_m_ffn1_norm': 'new_m', 'new_m_ffn1_w_gu': 'new_m', 'new_m_ffn1_w_down': 'new_m', 'new_m_mix_norm': 'new_m', 'new_m_even_w_in': 'new_m', 'new_m_conv_a_w': 'new_m', 'new_m_conv_a_b': 'new_m', 'new_m_conv_a_ln_g': 'new_m', 'new_m_conv_a_ln_b': 'new_m', 'new_m_swa_sinks': 'new_m', 'new_m_even_w_out': 'new_m', 'new_m_odd_w_in': 'new_m', 'new_m_sc_conv_w': 'new_m', 'new_m_odd_w_out': 'new_m', 'new_m_xa_norm': 'new_m', 'new_m_xa_mem_norm': 'new_m', 'new_m_xa_wq': 'new_m', 'new_m_xa_wkv': 'new_m', 'new_m_xa_wo': 'new_m', 'new_m_ffn2_norm': 'new_m', 'new_m_ffn2_w_gu': 'new_m', 'new_m_ffn2_w_down': 'new_m', 'new_m_final_norm': 'new_m', 'new_v_ffn1_norm': 'new_v', 'new_v_ffn1_w_gu': 'new_v', 'new_v_ffn1_w_down': 'new_v', 'new_v_mix_norm': 'new_v', 'new_v_even_w_in': 'new_v', 'new_v_conv_a_w': 'new_v', 'new_v_conv_a_b': 'new_v', 'new_v_conv_a_ln_g': 'new_v', 'new_v_conv_a_ln_b': 'new_v', 'new_v_swa_sinks': 'new_v', 'new_v_even_w_out': 'new_v', 'new_v_odd_w_in': 'new_v', 'new_v_sc_conv_w': 'new_v', 'new_v_odd_w_out': 'new_v', 'new_v_xa_norm': 'new_v', 'new_v_xa_mem_norm': 'new_v', 'new_v_xa_wq': 'new_v', 'new_v_xa_wkv': 'new_v', 'new_v_xa_wo': 'new_v', 'new_v_ffn2_norm': 'new_v', 'new_v_ffn2_w_gu': 'new_v', 'new_v_ffn2_w_down': 'new_v', 'new_v_final_norm': 'new_v'}


def _forward(args):
    return _fwd_reference(*[args[k] for k in FWD_PARAMS])


def _output_shape():
    def fwd():
        inp = _fwd_setup_inputs(0)
        return _fwd_reference(*[inp[k] for k in FWD_PARAMS])
    out = _jax.eval_shape(fwd)
    return out.shape, out.dtype

N_MICROBATCH = 1
ADAM_LR = 0.001
ADAM_B1 = 0.9
ADAM_B2 = 0.999
ADAM_EPS = 1e-08
ADAM_WD = 0.01
ADAM_STEP = 10
PER_EXAMPLE_BATCH_AXIS = {'x': 0, 'mem': 0, 'loss_target': 0}
SHARED_INPUTS = []
_WEIGHT_DTYPES = {'ffn1_norm': _jnp.float32, 'ffn1_w_gu': _jnp.float32, 'ffn1_w_down': _jnp.float32, 'mix_norm': _jnp.float32, 'even_w_in': _jnp.float32, 'conv_a_w': _jnp.float32, 'conv_a_b': _jnp.float32, 'conv_a_ln_g': _jnp.float32, 'conv_a_ln_b': _jnp.float32, 'swa_sinks': _jnp.float32, 'even_w_out': _jnp.float32, 'odd_w_in': _jnp.float32, 'sc_conv_w': _jnp.float32, 'odd_w_out': _jnp.float32, 'xa_norm': _jnp.float32, 'xa_mem_norm': _jnp.float32, 'xa_wq': _jnp.float32, 'xa_wkv': _jnp.float32, 'xa_wo': _jnp.float32, 'ffn2_norm': _jnp.float32, 'ffn2_w_gu': _jnp.float32, 'ffn2_w_down': _jnp.float32, 'final_norm': _jnp.float32}
MOMENT_SCALE = {'ffn1_norm': 1.406210e-01, 'ffn1_w_gu': 5.974912e-02, 'ffn1_w_down': 9.757169e-02, 'mix_norm': 2.567209e-01, 'even_w_in': 1.373719e-01, 'conv_a_w': 2.016905e-01, 'conv_a_b': 4.406785e-01, 'conv_a_ln_g': 2.444456e-01, 'conv_a_ln_b': 2.125485e-01, 'swa_sinks': 1.813650e-01, 'even_w_out': 1.523071e-01, 'odd_w_in': 1.621011e-01, 'sc_conv_w': 1.681170e-01, 'odd_w_out': 1.691046e-01, 'xa_norm': 2.871482e-02, 'xa_mem_norm': 3.984389e-02, 'xa_wq': 2.723669e-02, 'xa_wkv': 2.731302e-02, 'xa_wo': 2.749321e-02, 'ffn2_norm': 1.019525e-01, 'ffn2_w_gu': 4.415773e-02, 'ffn2_w_down': 7.240141e-02, 'final_norm': 6.417233e+01}


def _to_microbatches(a, axis):
    t = _jnp.moveaxis(a, axis, 0)
    t = t.reshape((N_MICROBATCH, t.shape[0] // N_MICROBATCH) + t.shape[1:])
    return _jnp.moveaxis(t, 1, axis + 1)


def setup_inputs(seed: int = 0) -> dict:
    inp = _fwd_setup_inputs(seed)
    key = _jax.random.fold_in(_jax.random.key(seed), 7919)
    shape, _ = _output_shape()
    out = dict(inp)
    out["loss_target"] = _jax.random.normal(_jax.random.fold_in(key, 0), shape, _jnp.float32)
    for i, name in enumerate(TWIN_WEIGHTS):
        w = inp[name].astype(_jnp.float32)
        if MOMENT_SCALE is None:
            s = _jnp.sqrt(_jnp.mean(_jnp.square(w)) + 1e-30)
        else:
            s = MOMENT_SCALE[name]
        km, kv = _jax.random.split(_jax.random.fold_in(key, i + 1))
        out[name] = w
        out["m_" + name] = s * _jax.random.normal(km, w.shape, _jnp.float32)
        out["v_" + name] = (s * s) * _jax.random.uniform(kv, w.shape, _jnp.float32, 0.5, 1.5)
    if N_MICROBATCH > 1:
        for name, axis in PER_EXAMPLE_BATCH_AXIS.items():
            out[name] = _to_microbatches(out[name], axis)
    return {'x': out['x'], 'mem': out['mem'], 'ffn1_norm': out['ffn1_norm'], 'ffn1_w_gu': out['ffn1_w_gu'], 'ffn1_w_down': out['ffn1_w_down'], 'mix_norm': out['mix_norm'], 'even_w_in': out['even_w_in'], 'conv_a_w': out['conv_a_w'], 'conv_a_b': out['conv_a_b'], 'conv_a_ln_g': out['conv_a_ln_g'], 'conv_a_ln_b': out['conv_a_ln_b'], 'swa_sinks': out['swa_sinks'], 'even_w_out': out['even_w_out'], 'odd_w_in': out['odd_w_in'], 'sc_conv_w': out['sc_conv_w'], 'odd_w_out': out['odd_w_out'], 'xa_norm': out['xa_norm'], 'xa_mem_norm': out['xa_mem_norm'], 'xa_wq': out['xa_wq'], 'xa_wkv': out['xa_wkv'], 'xa_wo': out['xa_wo'], 'ffn2_norm': out['ffn2_norm'], 'ffn2_w_gu': out['ffn2_w_gu'], 'ffn2_w_down': out['ffn2_w_down'], 'final_norm': out['final_norm'], 'loss_target': out['loss_target'], 'm_ffn1_norm': out['m_ffn1_norm'], 'm_ffn1_w_gu': out['m_ffn1_w_gu'], 'm_ffn1_w_down': out['m_ffn1_w_down'], 'm_mix_norm': out['m_mix_norm'], 'm_even_w_in': out['m_even_w_in'], 'm_conv_a_w': out['m_conv_a_w'], 'm_conv_a_b': out['m_conv_a_b'], 'm_conv_a_ln_g': out['m_conv_a_ln_g'], 'm_conv_a_ln_b': out['m_conv_a_ln_b'], 'm_swa_sinks': out['m_swa_sinks'], 'm_even_w_out': out['m_even_w_out'], 'm_odd_w_in': out['m_odd_w_in'], 'm_sc_conv_w': out['m_sc_conv_w'], 'm_odd_w_out': out['m_odd_w_out'], 'm_xa_norm': out['m_xa_norm'], 'm_xa_mem_norm': out['m_xa_mem_norm'], 'm_xa_wq': out['m_xa_wq'], 'm_xa_wkv': out['m_xa_wkv'], 'm_xa_wo': out['m_xa_wo'], 'm_ffn2_norm': out['m_ffn2_norm'], 'm_ffn2_w_gu': out['m_ffn2_w_gu'], 'm_ffn2_w_down': out['m_ffn2_w_down'], 'm_final_norm': out['m_final_norm'], 'v_ffn1_norm': out['v_ffn1_norm'], 'v_ffn1_w_gu': out['v_ffn1_w_gu'], 'v_ffn1_w_down': out['v_ffn1_w_down'], 'v_mix_norm': out['v_mix_norm'], 'v_even_w_in': out['v_even_w_in'], 'v_conv_a_w': out['v_conv_a_w'], 'v_conv_a_b': out['v_conv_a_b'], 'v_conv_a_ln_g': out['v_conv_a_ln_g'], 'v_conv_a_ln_b': out['v_conv_a_ln_b'], 'v_swa_sinks': out['v_swa_sinks'], 'v_even_w_out': out['v_even_w_out'], 'v_odd_w_in': out['v_odd_w_in'], 'v_sc_conv_w': out['v_sc_conv_w'], 'v_odd_w_out': out['v_odd_w_out'], 'v_xa_norm': out['v_xa_norm'], 'v_xa_mem_norm': out['v_xa_mem_norm'], 'v_xa_wq': out['v_xa_wq'], 'v_xa_wkv': out['v_xa_wkv'], 'v_xa_wo': out['v_xa_wo'], 'v_ffn2_norm': out['v_ffn2_norm'], 'v_ffn2_w_gu': out['v_ffn2_w_gu'], 'v_ffn2_w_down': out['v_ffn2_w_down'], 'v_final_norm': out['v_final_norm']}


def _loss(weights, diff, rest, loss_target):
    with _jax.named_scope("forward"):
        args = {**rest, TWIN_DIFF_INPUT: diff, **{k: w.astype(_WEIGHT_DTYPES[k]) for k, w in weights.items()}}
        y = _forward(args)
    with _jax.named_scope("loss_head"):
        err = _jnp.square(y.astype(_jnp.float32) - loss_target)
        return 0.5 * _jnp.sum(_jnp.mean(err, axis=-1)) if err.ndim else 0.5 * err


def _adamw(w, g, m, v):
    m = ADAM_B1 * m + (1.0 - ADAM_B1) * g
    v = ADAM_B2 * v + (1.0 - ADAM_B2) * _jnp.square(g)
    m_hat = m / (1.0 - ADAM_B1 ** ADAM_STEP)
    v_hat = v / (1.0 - ADAM_B2 ** ADAM_STEP)
    delta = -ADAM_LR * (m_hat / (_jnp.sqrt(v_hat) + ADAM_EPS) + ADAM_WD * w)
    return delta, m, v


def reference(x, mem, ffn1_norm, ffn1_w_gu, ffn1_w_down, mix_norm, even_w_in, conv_a_w, conv_a_b, conv_a_ln_g, conv_a_ln_b, swa_sinks, even_w_out, odd_w_in, sc_conv_w, odd_w_out, xa_norm, xa_mem_norm, xa_wq, xa_wkv, xa_wo, ffn2_norm, ffn2_w_gu, ffn2_w_down, final_norm, loss_target, m_ffn1_norm, m_ffn1_w_gu, m_ffn1_w_down, m_mix_norm, m_even_w_in, m_conv_a_w, m_conv_a_b, m_conv_a_ln_g, m_conv_a_ln_b, m_swa_sinks, m_even_w_out, m_odd_w_in, m_sc_conv_w, m_odd_w_out, m_xa_norm, m_xa_mem_norm, m_xa_wq, m_xa_wkv, m_xa_wo, m_ffn2_norm, m_ffn2_w_gu, m_ffn2_w_down, m_final_norm, v_ffn1_norm, v_ffn1_w_gu, v_ffn1_w_down, v_mix_norm, v_even_w_in, v_conv_a_w, v_conv_a_b, v_conv_a_ln_g, v_conv_a_ln_b, v_swa_sinks, v_even_w_out, v_odd_w_in, v_sc_conv_w, v_odd_w_out, v_xa_norm, v_xa_mem_norm, v_xa_wq, v_xa_wkv, v_xa_wo, v_ffn2_norm, v_ffn2_w_gu, v_ffn2_w_down, v_final_norm):
    given = dict(x=x, mem=mem, ffn1_norm=ffn1_norm, ffn1_w_gu=ffn1_w_gu, ffn1_w_down=ffn1_w_down, mix_norm=mix_norm, even_w_in=even_w_in, conv_a_w=conv_a_w, conv_a_b=conv_a_b, conv_a_ln_g=conv_a_ln_g, conv_a_ln_b=conv_a_ln_b, swa_sinks=swa_sinks, even_w_out=even_w_out, odd_w_in=odd_w_in, sc_conv_w=sc_conv_w, odd_w_out=odd_w_out, xa_norm=xa_norm, xa_mem_norm=xa_mem_norm, xa_wq=xa_wq, xa_wkv=xa_wkv, xa_wo=xa_wo, ffn2_norm=ffn2_norm, ffn2_w_gu=ffn2_w_gu, ffn2_w_down=ffn2_w_down, final_norm=final_norm, loss_target=loss_target, m_ffn1_norm=m_ffn1_norm, m_ffn1_w_gu=m_ffn1_w_gu, m_ffn1_w_down=m_ffn1_w_down, m_mix_norm=m_mix_norm, m_even_w_in=m_even_w_in, m_conv_a_w=m_conv_a_w, m_conv_a_b=m_conv_a_b, m_conv_a_ln_g=m_conv_a_ln_g, m_conv_a_ln_b=m_conv_a_ln_b, m_swa_sinks=m_swa_sinks, m_even_w_out=m_even_w_out, m_odd_w_in=m_odd_w_in, m_sc_conv_w=m_sc_conv_w, m_odd_w_out=m_odd_w_out, m_xa_norm=m_xa_norm, m_xa_mem_norm=m_xa_mem_norm, m_xa_wq=m_xa_wq, m_xa_wkv=m_xa_wkv, m_xa_wo=m_xa_wo, m_ffn2_norm=m_ffn2_norm, m_ffn2_w_gu=m_ffn2_w_gu, m_ffn2_w_down=m_ffn2_w_down, m_final_norm=m_final_norm, v_ffn1_norm=v_ffn1_norm, v_ffn1_w_gu=v_ffn1_w_gu, v_ffn1_w_down=v_ffn1_w_down, v_mix_norm=v_mix_norm, v_even_w_in=v_even_w_in, v_conv_a_w=v_conv_a_w, v_conv_a_b=v_conv_a_b, v_conv_a_ln_g=v_conv_a_ln_g, v_conv_a_ln_b=v_conv_a_ln_b, v_swa_sinks=v_swa_sinks, v_even_w_out=v_even_w_out, v_odd_w_in=v_odd_w_in, v_sc_conv_w=v_sc_conv_w, v_odd_w_out=v_odd_w_out, v_xa_norm=v_xa_norm, v_xa_mem_norm=v_xa_mem_norm, v_xa_wq=v_xa_wq, v_xa_wkv=v_xa_wkv, v_xa_wo=v_xa_wo, v_ffn2_norm=v_ffn2_norm, v_ffn2_w_gu=v_ffn2_w_gu, v_ffn2_w_down=v_ffn2_w_down, v_final_norm=v_final_norm)
    weights = {n: given[n] for n in TWIN_WEIGHTS}
    shared = {n: given[n] for n in SHARED_INPUTS}
    per_example = {n: given[n] for n in ['x', 'mem']}
    grad_fn = _jax.value_and_grad(_loss, argnums=(0, 1))

    def one_microbatch(ex, loss_target):
        ex = dict(ex)
        diff = ex.pop(TWIN_DIFF_INPUT)
        return grad_fn(weights, diff, {**shared, **ex}, loss_target)

    if N_MICROBATCH == 1:
        loss, (grad_w, grad_x) = one_microbatch(per_example, given["loss_target"])
    else:
        def body(carry, xs):
            loss_sum, grad_sum = carry
            l_k, (gw_k, gx_k) = one_microbatch(xs[0], xs[1])
            with _jax.named_scope("update"):
                return (loss_sum + l_k, _jax.tree.map(_jnp.add, grad_sum, gw_k)), gx_k

        init = (_jnp.zeros((), _jnp.float32), _jax.tree.map(_jnp.zeros_like, weights))
        (loss, grad_w), grad_x = _jax.lax.scan(body, init, (per_example, given["loss_target"]))
    with _jax.named_scope("update"):
        delta_w, new_m, new_v = {}, {}, {}
        for n in TWIN_WEIGHTS:
            delta_w[n], new_m[n], new_v[n] = _adamw(weights[n], grad_w[n], given["m_" + n], given["v_" + n])
    return (loss, grad_x, *[grad_w[n] for n in TWIN_WEIGHTS], *[delta_w[n] for n in TWIN_WEIGHTS],
            *[new_m[n] for n in TWIN_WEIGHTS], *[new_v[n] for n in TWIN_WEIGHTS])
```

```python
import functools

import jax
import jax.numpy as jnp
from jax import lax
from jax.experimental import pallas as pl
from jax.experimental.pallas import tpu as pltpu

F32 = jnp.float32
BF16 = jnp.bfloat16

D_MODEL = 1024
D_FF = 2816
CONV_A_CH = 512
CONV_A_WIDTH = 31
SWA_HEADS = 8
SWA_KV_HEADS = 2
SWA_GROUP = SWA_HEADS // SWA_KV_HEADS
HEAD_DIM = 64
WINDOW = 128
SC_CH = 1024
XA_HEADS = 4
XA_HEAD_DIM = D_MODEL // XA_HEADS
RMS_EPS = 1e-6
LN_EPS = 1e-5
ADAM_LR = 0.001
ADAM_B1 = 0.9
ADAM_B2 = 0.999
ADAM_EPS = 1e-08
ADAM_WD = 0.01
ADAM_STEP = 10
N_DEV = 8

V7X_VMEM_BYTES = 64 * 1024 * 1024
VMEM_LIMIT = V7X_VMEM_BYTES - 8 * 1024 * 1024
CONV_HALO = 32
SC_HALO = 8
NEG_BIG = -1e30


def _params(n_axes):
    return pltpu.CompilerParams(dimension_semantics=("arbitrary",) * n_axes, vmem_limit_bytes=VMEM_LIMIT)


def _tile(n, pref):
    t = min(n, pref)
    assert n % t == 0, (n, pref)
    return t


def _dot(a, b):
    return jnp.dot(a, b, preferred_element_type=F32)


def _dot_nt(a, b):
    return lax.dot_general(a, b, (((1,), (1,)), ((), ())), preferred_element_type=F32)


def _dot_tn(a, b):
    return lax.dot_general(a, b, (((0,), (0,)), ((), ())), preferred_element_type=F32)


def _sigmoid(x):
    return 1.0 / (1.0 + jnp.exp(-x))


def norm_matmul(h, g, w, out_dtype, name, tm=512, tn=None):
    T, K = h.shape
    N = w.shape[1]
    tm = _tile(T, tm)
    tn = N if tn is None else tn

    def body(h_ref, g_ref, w_ref, z_ref, u_ref):
        @pl.when(pl.program_id(1) == 0)
        def _():
            x = h_ref[...]
            r = lax.rsqrt(jnp.mean(x * x, axis=-1, keepdims=True) + RMS_EPS)
            u_ref[...] = ((x * r) * g_ref[...]).astype(BF16)

        z_ref[...] = _dot(u_ref[...], w_ref[...]).astype(z_ref.dtype)

    return pl.pallas_call(
        body, name=name, grid=(T // tm, N // tn),
        in_specs=[pl.BlockSpec((tm, K), lambda i, j: (i, 0)),
                  pl.BlockSpec((1, K), lambda i, j: (0, 0)),
                  pl.BlockSpec((K, tn), lambda i, j: (0, j))],
        out_specs=[pl.BlockSpec((tm, tn), lambda i, j: (i, j)),
                   pl.BlockSpec((tm, K), lambda i, j: (i, 0))],
        out_shape=[jax.ShapeDtypeStruct((T, N), out_dtype), jax.ShapeDtypeStruct((T, K), BF16)],
        compiler_params=_params(2),
    )(h, g, w)


def matmul_residual(a, w, res, name, tm=512):
    T, K = a.shape
    N = w.shape[1]
    tm = _tile(T, tm)

    def body(a_ref, w_ref, r_ref, o_ref):
        o_ref[...] = r_ref[...] + _dot(a_ref[...], w_ref[...])

    return pl.pallas_call(
        body, name=name, grid=(T // tm,),
        in_specs=[pl.BlockSpec((tm, K), lambda i: (i, 0)),
                  pl.BlockSpec((K, N), lambda i: (0, 0)),
                  pl.BlockSpec((tm, N), lambda i: (i, 0))],
        out_specs=pl.BlockSpec((tm, N), lambda i: (i, 0)),
        out_shape=jax.ShapeDtypeStruct((T, N), F32),
        compiler_params=_params(1),
    )(a, w, res)


def matmul_nt(dy, w, out_dtype, name, tm=512):
    T, N = dy.shape
    K = w.shape[0]
    tm = _tile(T, tm)

    def body(dy_ref, w_ref, o_ref):
        o_ref[...] = _dot_nt(dy_ref[...].astype(BF16), w_ref[...]).astype(o_ref.dtype)

    return pl.pallas_call(
        body, name=name, grid=(T // tm,),
        in_specs=[pl.BlockSpec((tm, N), lambda i: (i, 0)),
                  pl.BlockSpec((K, N), lambda i: (0, 0))],
        out_specs=pl.BlockSpec((tm, K), lambda i: (i, 0)),
        out_shape=jax.ShapeDtypeStruct((T, K), out_dtype),
        compiler_params=_params(1),
    )(dy, w)


def matmul_nt_norm_bwd(dz, w, h, g, dh_in, name, tm=256):
    T, N = dz.shape
    K = w.shape[0]
    tm = _tile(T, tm)

    def body(dz_ref, w_ref, h_ref, g_ref, dhin_ref, dh_ref, dg_ref):
        @pl.when(pl.program_id(0) == 0)
        def _():
            dg_ref[...] = jnp.zeros_like(dg_ref)

        du = _dot_nt(dz_ref[...], w_ref[...])
        x = h_ref[...]
        r = lax.rsqrt(jnp.mean(x * x, axis=-1, keepdims=True) + RMS_EPS)
        xh = x * r
        dg_ref[...] += jnp.sum(du * xh, axis=0, keepdims=True)
        dxh = du * g_ref[...]
        dh_ref[...] = dhin_ref[...] + r * (dxh - xh * jnp.mean(dxh * xh, axis=-1, keepdims=True))

    return pl.pallas_call(
        body, name=name, grid=(T // tm,),
        in_specs=[pl.BlockSpec((tm, N), lambda i: (i, 0)),
                  pl.BlockSpec((K, N), lambda i: (0, 0)),
                  pl.BlockSpec((tm, K), lambda i: (i, 0)),
                  pl.BlockSpec((1, K), lambda i: (0, 0)),
                  pl.BlockSpec((tm, K), lambda i: (i, 0))],
        out_specs=[pl.BlockSpec((tm, K), lambda i: (i, 0)),
                   pl.BlockSpec((1, K), lambda i: (0, 0))],
        out_shape=[jax.ShapeDtypeStruct((T, K), F32), jax.ShapeDtypeStruct((1, K), F32)],
        compiler_params=_params(1),
    )(dz, w, h, g, dh_in)


def matmul_tn(x, dy, name, scale=1.0, tk=None, tn=None, tt=512):
    T, K = x.shape
    N = dy.shape[1]
    tk = K if tk is None else tk
    tn = N if tn is None else tn
    tt = _tile(T, tt)
    nt = T // tt

    def body(x_ref, dy_ref, o_ref, acc_ref):
        t = pl.program_id(2)

        @pl.when(t == 0)
        def _():
            acc_ref[...] = jnp.zeros_like(acc_ref)

        acc_ref[...] += _dot_tn(x_ref[...].astype(BF16), dy_ref[...].astype(BF16))

        @pl.when(t == nt - 1)
        def _():
            o_ref[...] = (acc_ref[...] * scale).astype(o_ref.dtype)

    return pl.pallas_call(
        body, name=name, grid=(K // tk, N // tn, nt),
        in_specs=[pl.BlockSpec((tt, tk), lambda a, b, t: (t, a)),
                  pl.BlockSpec((tt, tn), lambda a, b, t: (t, b))],
        out_specs=pl.BlockSpec((tk, tn), lambda a, b, t: (a, b)),
        out_shape=jax.ShapeDtypeStruct((K, N), BF16),
        scratch_shapes=[pltpu.VMEM((tk, tn), F32)],
        compiler_params=_params(3),
    )(x, dy)


def ffn_down(gu, wd, res, name, tm=512):
    T = gu.shape[0]
    F = gu.shape[1] // 2
    N = wd.shape[1]
    tm = _tile(T, tm)

    def body(g_ref, up_ref, w_ref, r_ref, o_ref, a_ref):
        g = g_ref[...].astype(F32)
        a = (g * _sigmoid(g)) * up_ref[...].astype(F32)
        a_ref[...] = a.astype(BF16)
        o_ref[...] = r_ref[...] + 0.5 * _dot(a_ref[...], w_ref[...])

    return pl.pallas_call(
        body, name=name, grid=(T // tm,),
        in_specs=[pl.BlockSpec((tm, F), lambda i: (i, 0)),
                  pl.BlockSpec((tm, F), lambda i: (i, 1)),
                  pl.BlockSpec((F, N), lambda i: (0, 0)),
                  pl.BlockSpec((tm, N), lambda i: (i, 0))],
        out_specs=[pl.BlockSpec((tm, N), lambda i: (i, 0)),
                   pl.BlockSpec((tm, F), lambda i: (i, 0))],
        out_shape=[jax.ShapeDtypeStruct((T, N), F32), jax.ShapeDtypeStruct((T, F), BF16)],
        compiler_params=_params(1),
    )(gu, gu, wd, res)


def ffn_down_bwd(dy, wd, gu, name, tm=512):
    T, N = dy.shape
    F = wd.shape[0]
    tm = _tile(T, tm)

    def body(dy_ref, w_ref, g_ref, up_ref, o_ref):
        da = 0.5 * _dot_nt(dy_ref[...].astype(BF16), w_ref[...])
        g = g_ref[...].astype(F32)
        up = up_ref[...].astype(F32)
        s = _sigmoid(g)
        o_ref[:, :F] = (da * up * (s * (1.0 + g * (1.0 - s)))).astype(BF16)
        o_ref[:, F:] = (da * (g * s)).astype(BF16)

    return pl.pallas_call(
        body, name=name, grid=(T // tm,),
        in_specs=[pl.BlockSpec((tm, N), lambda i: (i, 0)),
                  pl.BlockSpec((F, N), lambda i: (0, 0)),
                  pl.BlockSpec((tm, F), lambda i: (i, 0)),
                  pl.BlockSpec((tm, F), lambda i: (i, 1))],
        out_specs=pl.BlockSpec((tm, 2 * F), lambda i: (i, 0)),
        out_shape=jax.ShapeDtypeStruct((T, 2 * F), BF16),
        compiler_params=_params(1),
    )(dy, wd, gu, gu)


def ffn_forward(h, g, w_gu, w_down, name):
    gu, u = norm_matmul(h, g, w_gu, BF16, name + "_gu", tn=D_FF)
    h_out, a = ffn_down(gu, w_down, h, name + "_down")
    return h_out, (h, u, gu, a)


def ffn_backward(dy, saved, g, w_gu, w_down, name):
    h, u, gu, a = saved
    dgu = ffn_down_bwd(dy, w_down, gu, name + "_ddown")
    d_w_down = matmul_tn(a, dy, name + "_dwd", scale=0.5, tk=D_FF // 2)
    d_w_gu = matmul_tn(u, dgu, name + "_dwgu", tn=D_FF)
    dh, dg = matmul_nt_norm_bwd(dgu, w_gu, h, g, dy, name + "_dx")
    return dh, dg, d_w_gu, d_w_down


def conformer_conv_fwd(z, cw, cb, lg, lb, name, tm=512):
    T = z.shape[0]
    C = CONV_A_CH
    tm = _tile(T, tm)
    hb = tm // CONV_HALO

    def body(v_ref, gt_ref, pv_ref, pg_ref, cw_ref, cb_ref, lg_ref, lb_ref, o_ref, xs_ref):
        i = pl.program_id(0)
        prev = pv_ref[...] * _sigmoid(pg_ref[...])
        xs_ref[0:CONV_HALO, :] = jnp.where(i > 0, prev, 0.0)
        xs_ref[CONV_HALO:, :] = v_ref[...] * _sigmoid(gt_ref[...])
        acc = jnp.zeros((tm, C), F32) + cb_ref[...]
        for k in range(CONV_A_WIDTH):
            acc = acc + cw_ref[k:k + 1, :] * xs_ref[pl.ds(CONV_HALO - (CONV_A_WIDTH - 1) + k, tm), :]
        mu = jnp.mean(acc, axis=-1, keepdims=True)
        xc = acc - mu
        var = jnp.mean(xc * xc, axis=-1, keepdims=True)
        y = (xc * lax.rsqrt(var + LN_EPS)) * lg_ref[...] + lb_ref[...]
        o_ref[...] = (y * _sigmoid(y)).astype(BF16)

    return pl.pallas_call(
        body, name=name, grid=(T // tm,),
        in_specs=[pl.BlockSpec((tm, C), lambda i: (i, 0)),
                  pl.BlockSpec((tm, C), lambda i: (i, 1)),
                  pl.BlockSpec((CONV_HALO, C), lambda i: (jnp.maximum(i * hb - 1, 0), 0)),
                  pl.BlockSpec((CONV_HALO, C), lambda i: (jnp.maximum(i * hb - 1, 0), 1)),
                  pl.BlockSpec((32, C), lambda i: (0, 0)),
                  pl.BlockSpec((1, C), lambda i: (0, 0)),
                  pl.BlockSpec((1, C), lambda i: (0, 0)),
                  pl.BlockSpec((1, C), lambda i: (0, 0))],
        out_specs=pl.BlockSpec((tm, C), lambda i: (i, 0)),
        out_shape=jax.ShapeDtypeStruct((T, C), BF16),
        scratch_shapes=[pltpu.VMEM((tm + CONV_HALO, C), F32)],
        compiler_params=_params(1),
    )(z, z, z, z, cw, cb, lg, lb)


def conformer_conv_bwd(z, dm, cw, cb, lg, lb, name, tm=512):
    T = z.shape[0]
    C = CONV_A_CH
    tm = _tile(T, tm)
    hb = tm // CONV_HALO
    n_tiles = T // tm
    last_halo = T // CONV_HALO - 1
    R = tm + CONV_HALO
    KW = CONV_A_WIDTH

    def body(v_ref, gt_ref, pv_ref, pg_ref, nv_ref, ng_ref, do_ref, ndo_ref, cw_ref, cb_ref, lg_ref, lb_ref,
             dz_ref, dcw_ref, dcb_ref, dlg_ref, dlb_ref, xs_ref, ds_ref):
        i = pl.program_id(0)

        @pl.when(i == 0)
        def _():
            dcw_ref[...] = jnp.zeros_like(dcw_ref)
            dcb_ref[...] = jnp.zeros_like(dcb_ref)
            dlg_ref[...] = jnp.zeros_like(dlg_ref)
            dlb_ref[...] = jnp.zeros_like(dlb_ref)

        val = v_ref[...]
        sg = _sigmoid(gt_ref[...])
        prev = pv_ref[...] * _sigmoid(pg_ref[...])
        xs_ref[0:CONV_HALO, :] = jnp.where(i > 0, prev, 0.0)
        xs_ref[CONV_HALO:CONV_HALO + tm, :] = val * sg
        xs_ref[CONV_HALO + tm:, :] = nv_ref[...] * _sigmoid(ng_ref[...])

        acc = jnp.zeros((R, C), F32) + cb_ref[...]
        for k in range(KW):
            acc = acc + cw_ref[k:k + 1, :] * xs_ref[pl.ds(CONV_HALO - (KW - 1) + k, R), :]
        mu = jnp.mean(acc, axis=-1, keepdims=True)
        xc = acc - mu
        rstd = lax.rsqrt(jnp.mean(xc * xc, axis=-1, keepdims=True) + LN_EPS)
        xh = xc * rstd
        y = xh * lg_ref[...] + lb_ref[...]
        s = _sigmoid(y)
        dout = jnp.concatenate([do_ref[...], jnp.where(i < n_tiles - 1, ndo_ref[...], 0.0)], axis=0)
        dy = dout * (s * (1.0 + y * (1.0 - s)))
        dxh = dy * lg_ref[...]
        dconv = rstd * (dxh - jnp.mean(dxh, axis=-1, keepdims=True) - xh * jnp.mean(dxh * xh, axis=-1, keepdims=True))
        ds_ref[...] = dconv

        dy_m = dy[:tm]
        dlg_ref[...] += jnp.sum(dy_m * xh[:tm], axis=0, keepdims=True)
        dlb_ref[...] += jnp.sum(dy_m, axis=0, keepdims=True)
        dc_m = dconv[:tm]
        dcb_ref[...] += jnp.sum(dc_m, axis=0, keepdims=True)
        dglu = jnp.zeros((tm, C), F32)
        for k in range(KW):
            dcw_ref[k:k + 1, :] += jnp.sum(dc_m * xs_ref[pl.ds(CONV_HALO - (KW - 1) + k, tm), :], axis=0, keepdims=True)
            dglu = dglu + cw_ref[k:k + 1, :] * ds_ref[pl.ds(KW - 1 - k, tm), :]
        dz_ref[:, :C] = (dglu * sg).astype(BF16)
        dz_ref[:, C:] = (dglu * val * sg * (1.0 - sg)).astype(BF16)

    prev_map = lambda i: jnp.maximum(i * hb - 1, 0)
    next_map = lambda i: jnp.minimum((i + 1) * hb, last_halo)
    return pl.pallas_call(
        body, name=name, grid=(n_tiles,),
        in_specs=[pl.BlockSpec((tm, C), lambda i: (i, 0)),
                  pl.BlockSpec((tm, C), lambda i: (i, 1)),
                  pl.BlockSpec((CONV_HALO, C), lambda i: (prev_map(i), 0)),
                  pl.BlockSpec((CONV_HALO, C), lambda i: (prev_map(i), 1)),
                  pl.BlockSpec((CONV_HALO, C), lambda i: (next_map(i), 0)),
                  pl.BlockSpec((CONV_HALO, C), lambda i: (next_map(i), 1)),
                  pl.BlockSpec((tm, C), lambda i: (i, 0)),
                  pl.BlockSpec((CONV_HALO, C), lambda i: (next_map(i), 0)),
                  pl.BlockSpec((32, C), lambda i: (0, 0)),
                  pl.BlockSpec((1, C), lambda i: (0, 0)),
                  pl.BlockSpec((1, C), lambda i: (0, 0)),
                  pl.BlockSpec((1, C), lambda i: (0, 0))],
        out_specs=[pl.BlockSpec((tm, 2 * C), lambda i: (i, 0)),
                   pl.BlockSpec((32, C), lambda i: (0, 0)),
                   pl.BlockSpec((1, C), lambda i: (0, 0)),
                   pl.BlockSpec((1, C), lambda i: (0, 0)),
                   pl.BlockSpec((1, C), lambda i: (0, 0))],
        out_shape=[jax.ShapeDtypeStruct((T, 2 * C), BF16),
                   jax.ShapeDtypeStruct((32, C), F32),
                   jax.ShapeDtypeStruct((1, C), F32),
                   jax.ShapeDtypeStruct((1, C), F32),
                   jax.ShapeDtypeStruct((1, C), F32)],
        scratch_shapes=[pltpu.VMEM((tm + 2 * CONV_HALO, C), F32), pltpu.VMEM((R, C), F32)],
        compiler_params=_params(1),
    )(z, z, z, z, z, z, dm, dm, cw, cb, lg, lb)


def _swa_scores(q_h, kk_h, slope, bias_dist, valid, sink):
    s = _dot_nt(q_h, kk_h) * (HEAD_DIM ** -0.5) - slope * bias_dist
    s = jnp.where(valid, s, NEG_BIG)
    m = jnp.maximum(jnp.max(s, axis=-1, keepdims=True), sink)
    p = jnp.exp(s - m)
    e_sink = jnp.exp(sink - m)
    inv = 1.0 / (jnp.sum(p, axis=-1, keepdims=True) + e_sink)
    return p * inv, e_sink * inv


def _swa_mask(r0):
    qi = lax.broadcasted_iota(jnp.int32, (WINDOW, 2 * WINDOW), 0)
    kj = lax.broadcasted_iota(jnp.int32, (WINDOW, 2 * WINDOW), 1)
    dist = qi + WINDOW - kj
    valid = (dist >= 0) & (dist < WINDOW) & (r0 - WINDOW + kj >= 0)
    return dist.astype(F32), valid


def swa_fwd(z, kpad, vpad, sinks, name, tq=512):
    T = z.shape[0]
    tq = _tile(T, tq)
    HQ = SWA_HEADS * HEAD_DIM

    def body(sink_ref, q_ref, k_ref, v_ref, o_ref):
        i = pl.program_id(0)
        for sub in range(tq // WINDOW):
            r0 = pl.multiple_of(i * tq + sub * WINDOW, WINDOW)
            kk = k_ref[pl.ds(r0, 2 * WINDOW), :]
            vv = v_ref[pl.ds(r0, 2 * WINDOW), :]
            qb = q_ref[sub * WINDOW:(sub + 1) * WINDOW, :].astype(BF16)
            dist, valid = _swa_mask(r0)
            outs = []
            for h in range(SWA_HEADS):
                kh = h // SWA_GROUP
                ks = slice(kh * HEAD_DIM, (kh + 1) * HEAD_DIM)
                pn, _ = _swa_scores(qb[:, h * HEAD_DIM:(h + 1) * HEAD_DIM], kk[:, ks], 2.0 ** (-(h + 1)), dist, valid,
                                    sink_ref[h])
                outs.append(_dot(pn.astype(BF16), vv[:, ks]))
            o_ref[sub * WINDOW:(sub + 1) * WINDOW, :] = jnp.concatenate(outs, axis=-1).astype(BF16)

    return pl.pallas_call(
        body, name=name, grid=(T // tq,),
        in_specs=[pl.BlockSpec(memory_space=pltpu.SMEM),
                  pl.BlockSpec((tq, HQ), lambda i: (i, 2)),
                  pl.BlockSpec((T + WINDOW, 2 * HEAD_DIM), lambda i: (0, 0)),
                  pl.BlockSpec((T + WINDOW, 2 * HEAD_DIM), lambda i: (0, 0))],
        out_specs=pl.BlockSpec((tq, HQ), lambda i: (i, 0)),
        out_shape=jax.ShapeDtypeStruct((T, HQ), BF16),
        compiler_params=_params(1),
    )(sinks, z, kpad, vpad)


def swa_bwd(z, kpad, vpad, sinks, dm, name, tq=512):
    T = z.shape[0]
    tq = _tile(T, tq)
    HQ = SWA_HEADS * HEAD_DIM
    scale = HEAD_DIM ** -0.5

    def body(sink_ref, q_ref, k_ref, v_ref, do_ref, dq_ref, dk_ref, dv_ref, dsink_ref):
        i = pl.program_id(0)

        @pl.when(i == 0)
        def _():
            dk_ref[...] = jnp.zeros_like(dk_ref)
            dv_ref[...] = jnp.zeros_like(dv_ref)
            dsink_ref[...] = jnp.zeros_like(dsink_ref)

        for sub in range(tq // WINDOW):
            r0 = pl.multiple_of(i * tq + sub * WINDOW, WINDOW)
            kk = k_ref[pl.ds(r0, 2 * WINDOW), :]
            vv = v_ref[pl.ds(r0, 2 * WINDOW), :]
            rows = slice(sub * WINDOW, (sub + 1) * WINDOW)
            qb = q_ref[rows, :].astype(BF16)
            dob = do_ref[rows, :].astype(BF16)
            dist, valid = _swa_mask(r0)
            dqs, dks, dvs = [], [], []
            for kh in range(SWA_KV_HEADS):
                ks = slice(kh * HEAD_DIM, (kh + 1) * HEAD_DIM)
                dk_acc = jnp.zeros((2 * WINDOW, HEAD_DIM), F32)
                dv_acc = jnp.zeros((2 * WINDOW, HEAD_DIM), F32)
                for g in range(SWA_GROUP):
                    h = kh * SWA_GROUP + g
                    hs = slice(h * HEAD_DIM, (h + 1) * HEAD_DIM)
                    pn, p_sink = _swa_scores(qb[:, hs], kk[:, ks], 2.0 ** (-(h + 1)), dist, valid, sink_ref[h])
                    dp = _dot_nt(dob[:, hs], vv[:, ks])
                    delta = jnp.sum(pn * dp, axis=-1, keepdims=True)
                    ds = (pn * (dp - delta)).astype(BF16)
                    dqs.append(_dot(ds, kk[:, ks]) * scale)
                    dk_acc = dk_acc + _dot_tn(ds, qb[:, hs]) * scale
                    dv_acc = dv_acc + _dot_tn(pn.astype(BF16), dob[:, hs])
                    dsink_ref[h:h + 1, :] += jnp.zeros((1, 128), F32) - jnp.sum(p_sink * delta)
                dks.append(dk_acc)
                dvs.append(dv_acc)
            dq_ref[rows, :] = jnp.concatenate(dqs, axis=-1).astype(BF16)
            dk_ref[pl.ds(r0, 2 * WINDOW), :] += jnp.concatenate(dks, axis=-1)
            dv_ref[pl.ds(r0, 2 * WINDOW), :] += jnp.concatenate(dvs, axis=-1)

    kv_spec = pl.BlockSpec((T + WINDOW, 2 * HEAD_DIM), lambda i: (0, 0))
    return pl.pallas_call(
        body, name=name, grid=(T // tq,),
        in_specs=[pl.BlockSpec(memory_space=pltpu.SMEM),
                  pl.BlockSpec((tq, HQ), lambda i: (i, 2)),
                  kv_spec, kv_spec,
                  pl.BlockSpec((tq, HQ), lambda i: (i, 1))],
        out_specs=[pl.BlockSpec((tq, HQ), lambda i: (i, 0)),
                   kv_spec, kv_spec,
                   pl.BlockSpec((SWA_HEADS, 128), lambda i: (0, 0))],
        out_shape=[jax.ShapeDtypeStruct((T, HQ), BF16),
                   jax.ShapeDtypeStruct((T + WINDOW, 2 * HEAD_DIM), F32),
                   jax.ShapeDtypeStruct((T + WINDOW, 2 * HEAD_DIM), F32),
                   jax.ShapeDtypeStruct((SWA_HEADS, 128), F32)],
        compiler_params=_params(1),
    )(sinks, z, kpad, vpad, dm)


def short_conv_fwd(z, w, name, tm=512):
    T = z.shape[0]
    C = SC_CH
    tm = _tile(T, tm)
    hb = tm // SC_HALO

    def body(b_ref, c_ref, v_ref, pc_ref, pv_ref, w_ref, o_ref, xs_ref):
        i = pl.program_id(0)
        xs_ref[0:SC_HALO, :] = jnp.where(i > 0, pc_ref[...] * pv_ref[...], 0.0)
        xs_ref[SC_HALO:, :] = c_ref[...] * v_ref[...]
        conv = jnp.zeros((tm, C), F32)
        for k in range(3):
            conv = conv + w_ref[k:k + 1, :] * xs_ref[pl.ds(SC_HALO - 2 + k, tm), :]
        o_ref[...] = (b_ref[...] * conv).astype(BF16)

    prev_map = lambda i: jnp.maximum(i * hb - 1, 0)
    return pl.pallas_call(
        body, name=name, grid=(T // tm,),
        in_specs=[pl.BlockSpec((tm, C), lambda i: (i, 0)),
                  pl.BlockSpec((tm, C), lambda i: (i, 1)),
                  pl.BlockSpec((tm, C), lambda i: (i, 2)),
                  pl.BlockSpec((SC_HALO, C), lambda i: (prev_map(i), 1)),
                  pl.BlockSpec((SC_HALO, C), lambda i: (prev_map(i), 2)),
                  pl.BlockSpec((8, C), lambda i: (0, 0))],
        out_specs=pl.BlockSpec((tm, C), lambda i: (i, 0)),
        out_shape=jax.ShapeDtypeStruct((T, C), BF16),
        scratch_shapes=[pltpu.VMEM((tm + SC_HALO, C), F32)],
        compiler_params=_params(1),
    )(z, z, z, z, z, w)


def short_conv_bwd(z, dm, w, name, tm=512):
    T = z.shape[0]
    C = SC_CH
    tm = _tile(T, tm)
    hb = tm // SC_HALO
    n_tiles = T // tm
    last_halo = T // SC_HALO - 1
    R = tm + SC_HALO

    def body(b_ref, c_ref, v_ref, pc_ref, pv_ref, nb_ref, do_ref, ndo_ref, w_ref, dz_ref, dw_ref, xs_ref, ds_ref):
        i = pl.program_id(0)

        @pl.when(i == 0)
        def _():
            dw_ref[...] = jnp.zeros_like(dw_ref)

        c = c_ref[...]
        val = v_ref[...]
        dout = do_ref[...]
        xs_ref[0:SC_HALO, :] = jnp.where(i > 0, pc_ref[...] * pv_ref[...], 0.0)
        xs_ref[SC_HALO:, :] = c * val
        dconv = dout * b_ref[...]
        ds_ref[0:tm, :] = dconv
        ds_ref[tm:, :] = jnp.where(i < n_tiles - 1, ndo_ref[...] * nb_ref[...], 0.0)
        conv = jnp.zeros((tm, C), F32)
        dcv = jnp.zeros((tm, C), F32)
        for k in range(3):
            xk = xs_ref[pl.ds(SC_HALO - 2 + k, tm), :]
            conv = conv + w_ref[k:k + 1, :] * xk
            dw_ref[k:k + 1, :] += jnp.sum(dconv * xk, axis=0, keepdims=True)
            dcv = dcv + w_ref[k:k + 1, :] * ds_ref[pl.ds(2 - k, tm), :]
        dz_ref[:, 0:C] = (dout * conv).astype(BF16)
        dz_ref[:, C:2 * C] = (dcv * val).astype(BF16)
        dz_ref[:, 2 * C:] = (dcv * c).astype(BF16)

    prev_map = lambda i: jnp.maximum(i * hb - 1, 0)
    next_map = lambda i: jnp.minimum((i + 1) * hb, last_halo)
    return pl.pallas_call(
        body, name=name, grid=(n_tiles,),
        in_specs=[pl.BlockSpec((tm, C), lambda i: (i, 0)),
                  pl.BlockSpec((tm, C), lambda i: (i, 1)),
                  pl.BlockSpec((tm, C), lambda i: (i, 2)),
                  pl.BlockSpec((SC_HALO, C), lambda i: (prev_map(i), 1)),
                  pl.BlockSpec((SC_HALO, C), lambda i: (prev_map(i), 2)),
                  pl.BlockSpec((SC_HALO, C), lambda i: (next_map(i), 0)),
                  pl.BlockSpec((tm, C), lambda i: (i, 0)),
                  pl.BlockSpec((SC_HALO, C), lambda i: (next_map(i), 0)),
                  pl.BlockSpec((8, C), lambda i: (0, 0))],
        out_specs=[pl.BlockSpec((tm, 3 * C), lambda i: (i, 0)),
                   pl.BlockSpec((8, C), lambda i: (0, 0))],
        out_shape=[jax.ShapeDtypeStruct((T, 3 * C), BF16), jax.ShapeDtypeStruct((8, C), F32)],
        scratch_shapes=[pltpu.VMEM((tm + SC_HALO, C), F32), pltpu.VMEM((R, C), F32)],
        compiler_params=_params(1),
    )(z, z, z, z, z, z, dm, dm, w)


def _xa_probs(q_h, k_h):
    s = _dot_nt(q_h, k_h) * (XA_HEAD_DIM ** -0.5)
    p = jnp.exp(s - jnp.max(s, axis=-1, keepdims=True))
    return p * (1.0 / jnp.sum(p, axis=-1, keepdims=True))


def xattn_fwd(q, kv, name, tm=512):
    T = q.shape[0]
    M = kv.shape[0]
    tm = _tile(T, tm)

    def body(q_ref, k_ref, v_ref, o_ref):
        for h in range(XA_HEADS):
            hs = slice(h * XA_HEAD_DIM, (h + 1) * XA_HEAD_DIM)
            p = _xa_probs(q_ref[:, hs], k_ref[:, hs])
            o_ref[:, hs] = _dot(p.astype(BF16), v_ref[:, hs]).astype(BF16)

    return pl.pallas_call(
        body, name=name, grid=(T // tm,),
        in_specs=[pl.BlockSpec((tm, D_MODEL), lambda i: (i, 0)),
                  pl.BlockSpec((M, D_MODEL), lambda i: (0, 0)),
                  pl.BlockSpec((M, D_MODEL), lambda i: (0, 1))],
        out_specs=pl.BlockSpec((tm, D_MODEL), lambda i: (i, 0)),
        out_shape=jax.ShapeDtypeStruct((T, D_MODEL), BF16),
        compiler_params=_params(1),
    )(q, kv, kv)


def xattn_bwd(q, kv, do, name, tm=512):
    T = q.shape[0]
    M = kv.shape[0]
    tm = _tile(T, tm)
    scale = XA_HEAD_DIM ** -0.5

    def body(q_ref, k_ref, v_ref, do_ref, dq_ref, dkv_ref):
        @pl.when(pl.program_id(0) == 0)
        def _():
            dkv_ref[...] = jnp.zeros_like(dkv_ref)

        for h in range(XA_HEADS):
            hs = slice(h * XA_HEAD_DIM, (h + 1) * XA_HEAD_DIM)
            vs = slice(D_MODEL + h * XA_HEAD_DIM, D_MODEL + (h + 1) * XA_HEAD_DIM)
            q_h = q_ref[:, hs]
            do_h = do_ref[:, hs]
            p = _xa_probs(q_h, k_ref[:, hs])
            dp = _dot_nt(do_h, v_ref[:, hs])
            ds = (p * (dp - jnp.sum(p * dp, axis=-1, keepdims=True))).astype(BF16)
            dq_ref[:, hs] = (_dot(ds, k_ref[:, hs]) * scale).astype(BF16)
            dkv_ref[:, hs] += _dot_tn(ds, q_h) * scale
            dkv_ref[:, vs] += _dot_tn(p.astype(BF16), do_h)

    return pl.pallas_call(
        body, name=name, grid=(T // tm,),
        in_specs=[pl.BlockSpec((tm, D_MODEL), lambda i: (i, 0)),
                  pl.BlockSpec((M, D_MODEL), lambda i: (0, 0)),
                  pl.BlockSpec((M, D_MODEL), lambda i: (0, 1)),
                  pl.BlockSpec((tm, D_MODEL), lambda i: (i, 0))],
        out_specs=[pl.BlockSpec((tm, D_MODEL), lambda i: (i, 0)),
                   pl.BlockSpec((M, 2 * D_MODEL), lambda i: (0, 0))],
        out_shape=[jax.ShapeDtypeStruct((T, D_MODEL), BF16), jax.ShapeDtypeStruct((M, 2 * D_MODEL), F32)],
        compiler_params=_params(1),
    )(q, kv, kv, do)


def final_loss(h, g, target, name, tm=512):
    T, K = h.shape
    tm = _tile(T, tm)

    def body(h_ref, g_ref, t_ref, dh_ref, dg_ref, loss_ref):
        @pl.when(pl.program_id(0) == 0)
        def _():
            dg_ref[...] = jnp.zeros_like(dg_ref)
            loss_ref[...] = jnp.zeros_like(loss_ref)

        x = h_ref[...]
        r = lax.rsqrt(jnp.mean(x * x, axis=-1, keepdims=True) + RMS_EPS)
        xh = x * r
        e = xh * g_ref[...] - t_ref[...]
        loss_ref[...] += jnp.zeros((1, 128), F32) + 0.5 * jnp.sum(jnp.mean(e * e, axis=-1, keepdims=True))
        dy = e * (1.0 / K)
        dg_ref[...] += jnp.sum(dy * xh, axis=0, keepdims=True)
        dxh = dy * g_ref[...]
        dh_ref[...] = r * (dxh - xh * jnp.mean(dxh * xh, axis=-1, keepdims=True))

    return pl.pallas_call(
        body, name=name, grid=(T // tm,),
        in_specs=[pl.BlockSpec((tm, K), lambda i: (i, 0)),
                  pl.BlockSpec((1, K), lambda i: (0, 0)),
                  pl.BlockSpec((tm, K), lambda i: (i, 0))],
        out_specs=[pl.BlockSpec((tm, K), lambda i: (i, 0)),
                   pl.BlockSpec((1, K), lambda i: (0, 0)),
                   pl.BlockSpec((1, 128), lambda i: (0, 0))],
        out_shape=[jax.ShapeDtypeStruct((T, K), F32), jax.ShapeDtypeStruct((1, K), F32),
                   jax.ShapeDtypeStruct((1, 128), F32)],
        compiler_params=_params(1),
    )(h, g, target)


def _row(v):
    return v.reshape(1, -1)


def _pad_rows(a, rows):
    return jnp.pad(a, ((0, rows - a.shape[0]), (0, 0)))


def local_step(x, mem, target, W, P):
    T = x.shape[0]
    cw = _pad_rows(P["conv_a_w"], 32)
    scw = _pad_rows(P["sc_conv_w"], 8)
    cb, lg, lb = _row(P["conv_a_b"]), _row(P["conv_a_ln_g"]), _row(P["conv_a_ln_b"])
    sinks = P["swa_sinks"]

    saved = []
    h = x
    for i in range(2):
        L = f"l{i}"
        h, s_ffn1 = ffn_forward(h, P["ffn1_norm"][i:i + 1], W["ffn1_w_gu"][i], W["ffn1_w_down"][i], L + "_ffn1")
        h1 = h
        if i == 0:
            z, u2 = norm_matmul(h1, P["mix_norm"][i:i + 1], W["even_w_in"], F32, L + "_mix_in")
            a = conformer_conv_fwd(z, cw, cb, lg, lb, L + "_conv")
            kpad = jnp.pad(z[:, 1536:1664].astype(BF16), ((WINDOW, 0), (0, 0)))
            vpad = jnp.pad(z[:, 1664:1792].astype(BF16), ((WINDOW, 0), (0, 0)))
            o = swa_fwd(z, kpad, vpad, sinks, L + "_swa")
            m = jnp.concatenate([a, o], axis=-1)
            h = matmul_residual(m, W["even_w_out"], h1, L + "_mix_out")
            s_mix = (h1, u2, z, m, kpad, vpad)
        else:
            z, u2 = norm_matmul(h1, P["mix_norm"][i:i + 1], W["odd_w_in"], F32, L + "_mix_in")
            m = short_conv_fwd(z, scw, L + "_sconv")
            h = matmul_residual(m, W["odd_w_out"], h1, L + "_mix_out")
            s_mix = (h1, u2, z, m)
        h2 = h
        kv, umem = norm_matmul(mem, P["xa_mem_norm"][i:i + 1], W["xa_wkv"][i], BF16, L + "_xa_kv", tm=256)
        q, u3 = norm_matmul(h2, P["xa_norm"][i:i + 1], W["xa_wq"][i], BF16, L + "_xa_q")
        o = xattn_fwd(q, kv, L + "_xa")
        h = matmul_residual(o, W["xa_wo"][i], h2, L + "_xa_out")
        s_xa = (h2, u3, q, o, kv, umem)
        h, s_ffn2 = ffn_forward(h, P["ffn2_norm"][i:i + 1], W["ffn2_w_gu"][i], W["ffn2_w_down"][i], L + "_ffn2")
        saved.append((s_ffn1, s_mix, s_xa, s_ffn2))

    dh, d_final, loss = final_loss(h, _row(P["final_norm"]), target, "final_loss")

    dW = {k: [None, None] for k in ("ffn1_w_gu", "ffn1_w_down", "ffn2_w_gu", "ffn2_w_down", "xa_wq", "xa_wkv", "xa_wo")}
    dP = {k: [None, None] for k in ("ffn1_norm", "mix_norm", "xa_norm", "xa_mem_norm", "ffn2_norm")}
    dP["final_norm"] = d_final.reshape(-1)
    for i in (1, 0):
        L = f"l{i}b"
        s_ffn1, s_mix, s_xa, s_ffn2 = saved[i]
        dh, dP["ffn2_norm"][i], dW["ffn2_w_gu"][i], dW["ffn2_w_down"][i] = ffn_backward(
            dh, s_ffn2, P["ffn2_norm"][i:i + 1], W["ffn2_w_gu"][i], W["ffn2_w_down"][i], L + "_ffn2")
        h2, u3, q, o, kv, umem = s_xa
        dW["xa_wo"][i] = matmul_tn(o, dh, L + "_xa_dwo")
        do = matmul_nt(dh, W["xa_wo"][i], BF16, L + "_xa_do")
        dq, dkv = xattn_bwd(q, kv, do, L + "_xa")
        dW["xa_wq"][i] = matmul_tn(u3, dq, L + "_xa_dwq")
        dW["xa_wkv"][i] = matmul_tn(umem, dkv, L + "_xa_dwkv", tn=1024)
        dkv_b = dkv.astype(BF16)
        _, dP["xa_mem_norm"][i] = matmul_nt_norm_bwd(dkv_b, W["xa_wkv"][i], mem, P["xa_mem_norm"][i:i + 1],
                                                     jnp.zeros_like(mem), L + "_xa_dmem")
        dh, dP["xa_norm"][i] = matmul_nt_norm_bwd(dq, W["xa_wq"][i], h2, P["xa_norm"][i:i + 1], dh, L + "_xa_dx")
        if i == 0:
            h1, u2, z, m, kpad, vpad = s_mix
            dW["even_w_out"] = matmul_tn(m, dh, L + "_mix_dwo")
            dm = matmul_nt(dh, W["even_w_out"], F32, L + "_mix_dm")
            dz_conv, dcw, dcb, dlg, dlb = conformer_conv_bwd(z, dm, cw, cb, lg, lb, L + "_conv")
            dq_s, dkp, dvp, dsk = swa_bwd(z, kpad, vpad, sinks, dm, L + "_swa")
            dz = jnp.concatenate([dz_conv, dq_s, dkp[WINDOW:].astype(BF16), dvp[WINDOW:].astype(BF16)], axis=-1)
            dW["even_w_in"] = matmul_tn(u2, dz, L + "_mix_dwi", tn=896)
            dh, dP["mix_norm"][i] = matmul_nt_norm_bwd(dz, W["even_w_in"], h1, P["mix_norm"][i:i + 1], dh, L + "_mix_dx")
            dP["conv_a_w"] = dcw[:CONV_A_WIDTH]
            dP["conv_a_b"], dP["conv_a_ln_g"], dP["conv_a_ln_b"] = dcb.reshape(-1), dlg.reshape(-1), dlb.reshape(-1)
            dP["swa_sinks"] = dsk[:, 0]
        else:
            h1, u2, z, m = s_mix
            dW["odd_w_out"] = matmul_tn(m, dh, L + "_mix_dwo")
            dm = matmul_nt(dh, W["odd_w_out"], F32, L + "_mix_dm")
            dz, dscw = short_conv_bwd(z, dm, scw, L + "_sconv")
            dW["odd_w_in"] = matmul_tn(u2, dz, L + "_mix_dwi", tn=1024)
            dh, dP["mix_norm"][i] = matmul_nt_norm_bwd(dz, W["odd_w_in"], h1, P["mix_norm"][i:i + 1], dh, L + "_mix_dx")
            dP["sc_conv_w"] = dscw[:3]
        dh, dP["ffn1_norm"][i], dW["ffn1_w_gu"][i], dW["ffn1_w_down"][i] = ffn_backward(
            dh, s_ffn1, P["ffn1_norm"][i:i + 1], W["ffn1_w_gu"][i], W["ffn1_w_down"][i], L + "_ffn1")
    for k in ("ffn1_norm", "mix_norm", "xa_norm", "xa_mem_norm", "ffn2_norm"):
        dP[k] = jnp.concatenate(dP[k], axis=0)
    return loss, dh, dW, dP


def _mesh_pos():
    return lax.axis_index("x"), lax.axis_index("y"), lax.axis_index("c")


def _flat_index(px, py, pc):
    return 4 * px + 2 * py + pc


def all_gather(blob, name):
    R, C = blob.shape

    def body(x_ref, out_ref, send_sems, recv_sems, local_sem):
        x, y, c = _mesh_pos()
        me, sibling = (x, y, c), (x, y, 1 - c)
        chips = [(1 - x, y), (x, 1 - y), (1 - x, 1 - y)]

        def slot(px, py, pc):
            return out_ref.at[_flat_index(px, py, pc)]

        def copy(k, block, to, src=None):
            return pltpu.make_async_remote_copy(
                src_ref=slot(*block) if src is None else src, dst_ref=slot(*block),
                send_sem=send_sems.at[k], recv_sem=recv_sems.at[k],
                device_id=to, device_id_type=pl.DeviceIdType.MESH)

        mine = pltpu.make_async_copy(x_ref, slot(*me), local_sem)
        mine.start()
        first = [copy(0, me, sibling, src=x_ref)]
        first += [copy(1 + j, me, (*chip, c), src=x_ref) for j, chip in enumerate(chips)]
        for cp in first:
            cp.start()
        passed = [copy(4 + j, (*chip, c), sibling) for j, chip in enumerate(chips)]
        for j, chip in enumerate(chips):
            copy(1 + j, (*chip, c), me).wait_recv()
            passed[j].start()
        copy(0, sibling, me).wait_recv()
        for j, chip in enumerate(chips):
            copy(4 + j, (*chip, 1 - c), me).wait_recv()
        for cp in first + passed:
            cp.wait_send()
        mine.wait()

    return pl.pallas_call(
        body, name=name,
        out_shape=jax.ShapeDtypeStruct((N_DEV, R, C), blob.dtype),
        in_specs=[pl.BlockSpec(memory_space=pl.ANY)],
        out_specs=pl.BlockSpec(memory_space=pl.ANY),
        scratch_shapes=[pltpu.SemaphoreType.DMA((7,)), pltpu.SemaphoreType.DMA((7,)), pltpu.SemaphoreType.DMA],
    )(blob)


def scatter_exchange(g, name):
    _, R, C = g.shape

    def body(g_ref, out_ref, send_sems, recv_sems, local_sem):
        x, y, c = _mesh_pos()
        me_idx = _flat_index(x, y, c)
        mine = pltpu.make_async_copy(g_ref.at[me_idx], out_ref.at[me_idx], local_sem)
        mine.start()
        sends, peers = [], []
        for k in range(1, N_DEV):
            px = 1 - x if k & 4 else x
            py = 1 - y if k & 2 else y
            pc = 1 - c if k & 1 else c
            peer_idx = _flat_index(px, py, pc)
            cp = pltpu.make_async_remote_copy(
                src_ref=g_ref.at[peer_idx], dst_ref=out_ref.at[me_idx],
                send_sem=send_sems.at[k - 1], recv_sem=recv_sems.at[k - 1],
                device_id=(px, py, pc), device_id_type=pl.DeviceIdType.MESH)
            cp.start()
            sends.append(cp)
            peers.append((peer_idx, (px, py, pc)))
        for k in range(1, N_DEV):
            peer_idx, peer = peers[k - 1]
            pltpu.make_async_remote_copy(
                src_ref=g_ref.at[me_idx], dst_ref=out_ref.at[peer_idx],
                send_sem=send_sems.at[k - 1], recv_sem=recv_sems.at[k - 1],
                device_id=peer, device_id_type=pl.DeviceIdType.MESH).wait_recv()
        for cp in sends:
            cp.wait_send()
        mine.wait()

    return pl.pallas_call(
        body, name=name,
        out_shape=jax.ShapeDtypeStruct(g.shape, g.dtype),
        in_specs=[pl.BlockSpec(memory_space=pl.ANY)],
        out_specs=pl.BlockSpec(memory_space=pl.ANY),
        scratch_shapes=[pltpu.SemaphoreType.DMA((7,)), pltpu.SemaphoreType.DMA((7,)), pltpu.SemaphoreType.DMA],
    )(g)


def ordered_sum(parts, name, tr=256):
    n, R, C = parts.shape
    tr = _tile(R, tr)

    def body(p_ref, o_ref):
        acc = p_ref[0].astype(F32)
        for j in range(1, n):
            acc = acc + p_ref[j].astype(F32)
        o_ref[...] = acc

    return pl.pallas_call(
        body, name=name, grid=(R // tr,),
        in_specs=[pl.BlockSpec((n, tr, C), lambda i: (0, i, 0))],
        out_specs=pl.BlockSpec((tr, C), lambda i: (i, 0)),
        out_shape=jax.ShapeDtypeStruct((R, C), F32),
        compiler_params=_params(1),
    )(parts)


def adamw(w, g, m, v, name, tr=256):
    R, C = w.shape
    tr = next((t for t in range(tr, 7, -8) if R % t == 0), R)
    c1 = 1.0 - ADAM_B1 ** ADAM_STEP
    c2 = 1.0 - ADAM_B2 ** ADAM_STEP

    def body(w_ref, g_ref, m_ref, v_ref, d_ref, mo_ref, vo_ref):
        grad = g_ref[...]
        m2 = ADAM_B1 * m_ref[...] + (1.0 - ADAM_B1) * grad
        v2 = ADAM_B2 * v_ref[...] + (1.0 - ADAM_B2) * (grad * grad)
        mo_ref[...] = m2
        vo_ref[...] = v2
        d_ref[...] = -ADAM_LR * ((m2 / c1) / (jnp.sqrt(v2 / c2) + ADAM_EPS) + ADAM_WD * w_ref[...])

    spec = pl.BlockSpec((tr, C), lambda i: (i, 0))
    return pl.pallas_call(
        body, name=name, grid=(R // tr,),
        in_specs=[spec] * 4, out_specs=[spec] * 3,
        out_shape=[jax.ShapeDtypeStruct((R, C), F32)] * 3,
        compiler_params=_params(1),
    )(w, g, m, v)


WEIGHT_NAMES = ("ffn1_norm", "ffn1_w_gu", "ffn1_w_down", "mix_norm", "even_w_in", "conv_a_w", "conv_a_b", "conv_a_ln_g",
                "conv_a_ln_b", "swa_sinks", "even_w_out", "odd_w_in", "sc_conv_w", "odd_w_out", "xa_norm", "xa_mem_norm",
                "xa_wq", "xa_wkv", "xa_wo", "ffn2_norm", "ffn2_w_gu", "ffn2_w_down", "final_norm")
BIG = (("ffn1_w_gu", 1), ("ffn1_w_down", 0), ("even_w_in", 1), ("even_w_out", 0), ("odd_w_in", 1), ("odd_w_out", 0),
       ("xa_wq", 0), ("xa_wkv", 1), ("xa_wo", 0), ("ffn2_w_gu", 1), ("ffn2_w_down", 0))
BLOB_COLS = 1024
BLOB_ROW_TILE = 256
SMALL_ROWS = (("ffn1_norm", 0, 2), ("mix_norm", 2, 2), ("xa_norm", 4, 2), ("xa_mem_norm", 6, 2), ("ffn2_norm", 8, 2),
              ("final_norm", 10, 1))
ROW_CONV_B_LNG = 11
ROW_LNB_SINKS_LOSS = 12
LOSS_COL = 512 + SWA_HEADS
ROW_SC_CONV = 13
ROW_CONV_W = 16
SMALL_BLOB_ROWS = 32
SMALL_ADAM_ROWS = 16


def _unpack_full(gathered, off, shard_shape, axis):
    L, k, n = shard_shape
    rows = L * k * n // BLOB_COLS
    blk = gathered[:, off:off + rows, :].reshape(N_DEV, L, k, n)
    if axis == 1:
        return jnp.transpose(blk, (1, 2, 0, 3)).reshape(L, k, N_DEV * n), rows
    return jnp.transpose(blk, (1, 0, 2, 3)).reshape(L, N_DEV * k, n), rows


def _pack_grad(dw, shard_shape, axis):
    L, k, n = shard_shape
    if axis == 1:
        blk = jnp.transpose(dw.reshape(L, k, N_DEV, n), (2, 0, 1, 3))
    else:
        blk = jnp.transpose(dw.reshape(L, N_DEV, k, n), (1, 0, 2, 3))
    return blk.reshape(N_DEV, L * k * n // BLOB_COLS, BLOB_COLS)


def _small_blob(v):
    rows = [v[n].reshape(-1, D_MODEL) for n, _, _ in SMALL_ROWS]
    rows.append(jnp.concatenate([v["conv_a_b"].reshape(-1), v["conv_a_ln_g"].reshape(-1)]).reshape(1, D_MODEL))
    tail = jnp.zeros((D_MODEL - 512 - SWA_HEADS,), F32)
    if "loss" in v:
        tail = tail.at[0].set(v["loss"])
    rows.append(jnp.concatenate([v["conv_a_ln_b"].reshape(-1), v["swa_sinks"].reshape(-1), tail]).reshape(1, D_MODEL))
    rows.append(jnp.zeros((SMALL_ADAM_ROWS - ROW_SC_CONV, D_MODEL), F32))
    return jnp.concatenate(rows, axis=0)


def _small_unblob(b, shapes):
    out = {n: b[r:r + k].reshape(shapes[n]) for n, r, k in SMALL_ROWS}
    out["conv_a_b"] = b[ROW_CONV_B_LNG, :512].reshape(shapes["conv_a_b"])
    out["conv_a_ln_g"] = b[ROW_CONV_B_LNG, 512:].reshape(shapes["conv_a_ln_g"])
    out["conv_a_ln_b"] = b[ROW_LNB_SINKS_LOSS, :512].reshape(shapes["conv_a_ln_b"])
    out["swa_sinks"] = b[ROW_LNB_SINKS_LOSS, 512:512 + SWA_HEADS].reshape(shapes["swa_sinks"])
    return out


def kernel(x, mem, ffn1_norm, ffn1_w_gu, ffn1_w_down, mix_norm, even_w_in, conv_a_w, conv_a_b, conv_a_ln_g, conv_a_ln_b, swa_sinks, even_w_out, odd_w_in, sc_conv_w, odd_w_out, xa_norm, xa_mem_norm, xa_wq, xa_wkv, xa_wo, ffn2_norm, ffn2_w_gu, ffn2_w_down, final_norm, loss_target, m_ffn1_norm, m_ffn1_w_gu, m_ffn1_w_down, m_mix_norm, m_even_w_in, m_conv_a_w, m_conv_a_b, m_conv_a_ln_g, m_conv_a_ln_b, m_swa_sinks, m_even_w_out, m_odd_w_in, m_sc_conv_w, m_odd_w_out, m_xa_norm, m_xa_mem_norm, m_xa_wq, m_xa_wkv, m_xa_wo, m_ffn2_norm, m_ffn2_w_gu, m_ffn2_w_down, m_final_norm, v_ffn1_norm, v_ffn1_w_gu, v_ffn1_w_down, v_mix_norm, v_even_w_in, v_conv_a_w, v_conv_a_b, v_conv_a_ln_g, v_conv_a_ln_b, v_swa_sinks, v_even_w_out, v_odd_w_in, v_sc_conv_w, v_odd_w_out, v_xa_norm, v_xa_mem_norm, v_xa_wq, v_xa_wkv, v_xa_wo, v_ffn2_norm, v_ffn2_w_gu, v_ffn2_w_down, v_final_norm):
    w = dict(ffn1_norm=ffn1_norm, ffn1_w_gu=ffn1_w_gu, ffn1_w_down=ffn1_w_down, mix_norm=mix_norm, even_w_in=even_w_in,
             conv_a_w=conv_a_w, conv_a_b=conv_a_b, conv_a_ln_g=conv_a_ln_g, conv_a_ln_b=conv_a_ln_b, swa_sinks=swa_sinks,
             even_w_out=even_w_out, odd_w_in=odd_w_in, sc_conv_w=sc_conv_w, odd_w_out=odd_w_out, xa_norm=xa_norm,
             xa_mem_norm=xa_mem_norm, xa_wq=xa_wq, xa_wkv=xa_wkv, xa_wo=xa_wo, ffn2_norm=ffn2_norm, ffn2_w_gu=ffn2_w_gu,
             ffn2_w_down=ffn2_w_down, final_norm=final_norm)
    m = dict(ffn1_norm=m_ffn1_norm, ffn1_w_gu=m_ffn1_w_gu, ffn1_w_down=m_ffn1_w_down, mix_norm=m_mix_norm,
             even_w_in=m_even_w_in, conv_a_w=m_conv_a_w, conv_a_b=m_conv_a_b, conv_a_ln_g=m_conv_a_ln_g,
             conv_a_ln_b=m_conv_a_ln_b, swa_sinks=m_swa_sinks, even_w_out=m_even_w_out, odd_w_in=m_odd_w_in,
             sc_conv_w=m_sc_conv_w, odd_w_out=m_odd_w_out, xa_norm=m_xa_norm, xa_mem_norm=m_xa_mem_norm, xa_wq=m_xa_wq,
             xa_wkv=m_xa_wkv, xa_wo=m_xa_wo, ffn2_norm=m_ffn2_norm, ffn2_w_gu=m_ffn2_w_gu, ffn2_w_down=m_ffn2_w_down,
             final_norm=m_final_norm)
    v = dict(ffn1_norm=v_ffn1_norm, ffn1_w_gu=v_ffn1_w_gu, ffn1_w_down=v_ffn1_w_down, mix_norm=v_mix_norm,
             even_w_in=v_even_w_in, conv_a_w=v_conv_a_w, conv_a_b=v_conv_a_b, conv_a_ln_g=v_conv_a_ln_g,
             conv_a_ln_b=v_conv_a_ln_b, swa_sinks=v_swa_sinks, even_w_out=v_even_w_out, odd_w_in=v_odd_w_in,
             sc_conv_w=v_sc_conv_w, odd_w_out=v_odd_w_out, xa_norm=v_xa_norm, xa_mem_norm=v_xa_mem_norm, xa_wq=v_xa_wq,
             xa_wkv=v_xa_wkv, xa_wo=v_xa_wo, ffn2_norm=v_ffn2_norm, ffn2_w_gu=v_ffn2_w_gu, ffn2_w_down=v_ffn2_w_down,
             final_norm=v_final_norm)
    me = _flat_index(*_mesh_pos())

    blob = jnp.concatenate([w[n].astype(BF16).reshape(-1, BLOB_COLS) for n, _ in BIG], axis=0)
    blob_pad = -blob.shape[0] % BLOB_ROW_TILE
    blob = jnp.pad(blob, ((0, blob_pad), (0, 0)))
    gathered = all_gather(blob, "gather_weights")
    W, off = {}, 0
    for n, axis in BIG:
        full, rows = _unpack_full(gathered, off, w[n].shape, axis)
        off += rows
        W[n] = full[0] if full.shape[0] == 1 else [full[0], full[1]]
    conv_blob = jnp.concatenate([w["conv_a_w"].reshape(-1), w["sc_conv_w"].reshape(-1),
                                 jnp.zeros((8 * 1024 - 31 * 64 - 3 * 128,), F32)]).reshape(8, 1024)
    conv_all = all_gather(conv_blob, "gather_conv_weights").reshape(N_DEV, 8 * 1024)
    conv_a_full = jnp.transpose(conv_all[:, :31 * 64].reshape(N_DEV, 31, 64), (1, 0, 2)).reshape(31, 512)
    sc_full = jnp.transpose(conv_all[:, 31 * 64:31 * 64 + 3 * 128].reshape(N_DEV, 3, 128), (1, 0, 2)).reshape(3, 1024)

    P = dict(ffn1_norm=ffn1_norm, mix_norm=mix_norm, xa_norm=xa_norm, xa_mem_norm=xa_mem_norm, ffn2_norm=ffn2_norm,
             final_norm=final_norm, conv_a_w=conv_a_full, conv_a_b=conv_a_b[0], conv_a_ln_g=conv_a_ln_g[0],
             conv_a_ln_b=conv_a_ln_b[0], swa_sinks=swa_sinks[0], sc_conv_w=sc_full)

    loss_part, grad_x, dW, dP = local_step(x[0], mem[0], loss_target[0], W, P)

    packed = []
    for n, axis in BIG:
        dw = dW[n]
        dw = jnp.stack(dw, axis=0) if isinstance(dw, list) else dw[None]
        packed.append(_pack_grad(dw, w[n].shape, axis))
    packed.append(jnp.zeros((N_DEV, blob_pad, BLOB_COLS), BF16))
    received = scatter_exchange(jnp.concatenate(packed, axis=1), "scatter_grads")
    g_rows = ordered_sum(received, "sum_grads", tr=BLOB_ROW_TILE)
    grads, off = {}, 0
    for n, _ in BIG:
        rows = w[n].size // BLOB_COLS
        grads[n] = g_rows[off:off + rows].reshape(w[n].shape)
        off += rows

    dP = dict(dP, loss=loss_part[0, 0])
    small = jnp.concatenate([
        _small_blob(dP)[:ROW_SC_CONV], dP["sc_conv_w"],
        jnp.concatenate([dP["conv_a_w"].reshape(-1), jnp.zeros((512,), F32)]).reshape(16, D_MODEL)], axis=0)
    small_sum = ordered_sum(all_gather(small, "gather_small_grads"), "sum_small_grads", tr=SMALL_BLOB_ROWS)
    loss = small_sum[ROW_LNB_SINKS_LOSS, LOSS_COL]
    grads.update(_small_unblob(small_sum, {n: w[n].shape for n in WEIGHT_NAMES}))
    sc_g = small_sum[ROW_SC_CONV:ROW_SC_CONV + 3]
    grads["sc_conv_w"] = lax.dynamic_slice(sc_g, (0, me * 128), (3, 128)).reshape(w["sc_conv_w"].shape)
    cw_g = small_sum[ROW_CONV_W:].reshape(-1)[:31 * 512].reshape(31, 512)
    grads["conv_a_w"] = lax.dynamic_slice(cw_g, (0, me * 64), (31, 64)).reshape(w["conv_a_w"].shape)

    delta, new_m, new_v = {}, {}, {}
    for n, _ in BIG + (("conv_a_w", 0), ("sc_conv_w", 0)):
        shp = w[n].shape
        two_d = (shp[0] * shp[1], shp[2])
        d_, m_, v_ = adamw(w[n].reshape(two_d), grads[n].reshape(two_d), m[n].reshape(two_d), v[n].reshape(two_d),
                           "adamw_" + n)
        delta[n], new_m[n], new_v[n] = d_.reshape(shp), m_.reshape(shp), v_.reshape(shp)
    d_, m_, v_ = adamw(_small_blob(w), small_sum[:SMALL_ADAM_ROWS], _small_blob(m), _small_blob(v), "adamw_small",
                       tr=SMALL_ADAM_ROWS)
    shapes = {n: w[n].shape for n in WEIGHT_NAMES}
    delta.update(_small_unblob(d_, shapes))
    new_m.update(_small_unblob(m_, shapes))
    new_v.update(_small_unblob(v_, shapes))

    return (loss, grad_x[None], *[grads[n] for n in WEIGHT_NAMES], *[delta[n] for n in WEIGHT_NAMES],
            *[new_m[n] for n in WEIGHT_NAMES], *[new_v[n] for n in WEIGHT_NAMES])
```

```python
import functools

import jax
import jax.numpy as jnp
from jax import lax
from jax.experimental import pallas as pl
from jax.experimental.pallas import tpu as pltpu

F32 = jnp.float32
BF16 = jnp.bfloat16

D_MODEL = 1024
D_FF = 2816
CONV_A_CH = 512
CONV_A_WIDTH = 31
SWA_HEADS = 8
SWA_KV_HEADS = 2
SWA_GROUP = SWA_HEADS // SWA_KV_HEADS
HEAD_DIM = 64
WINDOW = 128
SC_CH = 1024
XA_HEADS = 4
XA_HEAD_DIM = D_MODEL // XA_HEADS
RMS_EPS = 1e-6
LN_EPS = 1e-5
ADAM_LR = 0.001
ADAM_B1 = 0.9
ADAM_B2 = 0.999
ADAM_EPS = 1e-08
ADAM_WD = 0.01
ADAM_STEP = 10
N_DEV = 8

V7X_VMEM_BYTES = 64 * 1024 * 1024
VMEM_LIMIT = V7X_VMEM_BYTES - 8 * 1024 * 1024
CONV_HALO = 32
SC_HALO = 8
NEG_BIG = -1e30

SPLIT_AXIS = dict(ffn1_w_gu=1, ffn1_w_down=0, even_w_in=1, even_w_out=0, odd_w_in=1, odd_w_out=0, xa_wq=0, xa_wkv=1, xa_wo=0,
                  ffn2_w_gu=1, ffn2_w_down=0)
STAGE_KEYS = dict(
    A=(("ffn1_w_gu", 0), ("ffn1_w_down", 0)),
    B=(("even_w_in", 0), ("even_w_out", 0), ("xa_wq", 0), ("xa_wkv", 0), ("xa_wo", 0)),
    C=(("ffn2_w_gu", 0), ("ffn2_w_down", 0)),
    D=(("ffn1_w_gu", 1), ("ffn1_w_down", 1), ("odd_w_in", 0), ("odd_w_out", 0), ("xa_wq", 1), ("xa_wkv", 1), ("xa_wo", 1),
       ("ffn2_w_gu", 1), ("ffn2_w_down", 1)))


def _params(n_axes):
    return pltpu.CompilerParams(dimension_semantics=("arbitrary",) * n_axes, vmem_limit_bytes=VMEM_LIMIT)


def _tile(n, pref):
    t = min(n, pref)
    assert n % t == 0, (n, pref)
    return t


def _dot(a, b):
    return jnp.dot(a, b, preferred_element_type=F32)


def _dot_nt(a, b):
    return lax.dot_general(a, b, (((1,), (1,)), ((), ())), preferred_element_type=F32)


def _dot_tn(a, b):
    return lax.dot_general(a, b, (((0,), (0,)), ((), ())), preferred_element_type=F32)


def _sigmoid(x):
    return 1.0 / (1.0 + jnp.exp(-x))


ANY_SPEC = pl.BlockSpec(memory_space=pl.ANY)


def _with_dep(body, n_in, dep):
    if dep is None:
        return body, [], []
    return (lambda *refs: body(*refs[:n_in], *refs[n_in + 1:])), [ANY_SPEC], [dep]


def norm_matmul(h, g, w, out_dtype, name, tm=512, tn=None, dep=None):
    T, K = h.shape
    N = w.shape[1]
    tm = _tile(T, tm)
    tn = N if tn is None else tn

    def kern(h_ref, g_ref, w_ref, z_ref, u_ref):
        @pl.when(pl.program_id(1) == 0)
        def _():
            x = h_ref[...]
            r = lax.rsqrt(jnp.mean(x * x, axis=-1, keepdims=True) + RMS_EPS)
            u_ref[...] = ((x * r) * g_ref[...]).astype(BF16)

        z_ref[...] = _dot(u_ref[...], w_ref[...]).astype(z_ref.dtype)

    body, dep_spec, dep_arg = _with_dep(kern, 3, dep)
    return pl.pallas_call(
        body, name=name, grid=(T // tm, N // tn),
        in_specs=[pl.BlockSpec((tm, K), lambda i, j: (i, 0)),
                  pl.BlockSpec((1, K), lambda i, j: (0, 0)),
                  pl.BlockSpec((K, tn), lambda i, j: (0, j))] + dep_spec,
        out_specs=[pl.BlockSpec((tm, tn), lambda i, j: (i, j)),
                   pl.BlockSpec((tm, K), lambda i, j: (i, 0))],
        out_shape=[jax.ShapeDtypeStruct((T, N), out_dtype), jax.ShapeDtypeStruct((T, K), BF16)],
        compiler_params=_params(2),
    )(h, g, w, *dep_arg)


def matmul_residual(a, w, res, name, tm=512):
    T, K = a.shape
    N = w.shape[1]
    tm = _tile(T, tm)

    def body(a_ref, w_ref, r_ref, o_ref):
        o_ref[...] = r_ref[...] + _dot(a_ref[...], w_ref[...])

    return pl.pallas_call(
        body, name=name, grid=(T // tm,),
        in_specs=[pl.BlockSpec((tm, K), lambda i: (i, 0)),
                  pl.BlockSpec((K, N), lambda i: (0, 0)),
                  pl.BlockSpec((tm, N), lambda i: (i, 0))],
        out_specs=pl.BlockSpec((tm, N), lambda i: (i, 0)),
        out_shape=jax.ShapeDtypeStruct((T, N), F32),
        compiler_params=_params(1),
    )(a, w, res)


def matmul_nt(dy, w, out_dtype, name, tm=512):
    T, N = dy.shape
    K = w.shape[0]
    tm = _tile(T, tm)

    def body(dy_ref, w_ref, o_ref):
        o_ref[...] = _dot_nt(dy_ref[...].astype(BF16), w_ref[...]).astype(o_ref.dtype)

    return pl.pallas_call(
        body, name=name, grid=(T // tm,),
        in_specs=[pl.BlockSpec((tm, N), lambda i: (i, 0)),
                  pl.BlockSpec((K, N), lambda i: (0, 0))],
        out_specs=pl.BlockSpec((tm, K), lambda i: (i, 0)),
        out_shape=jax.ShapeDtypeStruct((T, K), out_dtype),
        compiler_params=_params(1),
    )(dy, w)


def matmul_nt_norm_bwd(dz, w, h, g, dh_in, name, tm=256):
    T, N = dz.shape
    K = w.shape[0]
    tm = _tile(T, tm)

    def body(dz_ref, w_ref, h_ref, g_ref, dhin_ref, dh_ref, dg_ref):
        @pl.when(pl.program_id(0) == 0)
        def _():
            dg_ref[...] = jnp.zeros_like(dg_ref)

        du = _dot_nt(dz_ref[...], w_ref[...])
        x = h_ref[...]
        r = lax.rsqrt(jnp.mean(x * x, axis=-1, keepdims=True) + RMS_EPS)
        xh = x * r
        dg_ref[...] += jnp.sum(du * xh, axis=0, keepdims=True)
        dxh = du * g_ref[...]
        dh_ref[...] = dhin_ref[...] + r * (dxh - xh * jnp.mean(dxh * xh, axis=-1, keepdims=True))

    return pl.pallas_call(
        body, name=name, grid=(T // tm,),
        in_specs=[pl.BlockSpec((tm, N), lambda i: (i, 0)),
                  pl.BlockSpec((K, N), lambda i: (0, 0)),
                  pl.BlockSpec((tm, K), lambda i: (i, 0)),
                  pl.BlockSpec((1, K), lambda i: (0, 0)),
                  pl.BlockSpec((tm, K), lambda i: (i, 0))],
        out_specs=[pl.BlockSpec((tm, K), lambda i: (i, 0)),
                   pl.BlockSpec((1, K), lambda i: (0, 0))],
        out_shape=[jax.ShapeDtypeStruct((T, K), F32), jax.ShapeDtypeStruct((1, K), F32)],
        compiler_params=_params(1),
    )(dz, w, h, g, dh_in)


def matmul_tn(x, dy, name, scale=1.0, tk=None, tn=None, tt=512):
    T, K = x.shape
    N = dy.shape[1]
    tk = K if tk is None else tk
    tn = N if tn is None else tn
    tt = _tile(T, tt)
    nt = T // tt

    def body(x_ref, dy_ref, o_ref, acc_ref):
        t = pl.program_id(2)

        @pl.when(t == 0)
        def _():
            acc_ref[...] = jnp.zeros_like(acc_ref)

        acc_ref[...] += _dot_tn(x_ref[...].astype(BF16), dy_ref[...].astype(BF16))

        @pl.when(t == nt - 1)
        def _():
            o_ref[...] = (acc_ref[...] * scale).astype(o_ref.dtype)

    return pl.pallas_call(
        body, name=name, grid=(K // tk, N // tn, nt),
        in_specs=[pl.BlockSpec((tt, tk), lambda a, b, t: (t, a)),
                  pl.BlockSpec((tt, tn), lambda a, b, t: (t, b))],
        out_specs=pl.BlockSpec((tk, tn), lambda a, b, t: (a, b)),
        out_shape=jax.ShapeDtypeStruct((K, N), BF16),
        scratch_shapes=[pltpu.VMEM((tk, tn), F32)],
        compiler_params=_params(3),
    )(x, dy)


def ffn_down(gu, wd, res, name, tm=512):
    T = gu.shape[0]
    F = gu.shape[1] // 2
    N = wd.shape[1]
    tm = _tile(T, tm)

    def body(g_ref, up_ref, w_ref, r_ref, o_ref, a_ref):
        g = g_ref[...].astype(F32)
        a = (g * _sigmoid(g)) * up_ref[...].astype(F32)
        a_ref[...] = a.astype(BF16)
        o_ref[...] = r_ref[...] + 0.5 * _dot(a_ref[...], w_ref[...])

    return pl.pallas_call(
        body, name=name, grid=(T // tm,),
        in_specs=[pl.BlockSpec((tm, F), lambda i: (i, 0)),
                  pl.BlockSpec((tm, F), lambda i: (i, 1)),
                  pl.BlockSpec((F, N), lambda i: (0, 0)),
                  pl.BlockSpec((tm, N), lambda i: (i, 0))],
        out_specs=[pl.BlockSpec((tm, N), lambda i: (i, 0)),
                   pl.BlockSpec((tm, F), lambda i: (i, 0))],
        out_shape=[jax.ShapeDtypeStruct((T, N), F32), jax.ShapeDtypeStruct((T, F), BF16)],
        compiler_params=_params(1),
    )(gu, gu, wd, res)


def ffn_down_bwd(dy, wd, gu, name, tm=512, dep=None):
    T, N = dy.shape
    F = wd.shape[0]
    tm = _tile(T, tm)

    def kern(dy_ref, w_ref, g_ref, up_ref, o_ref):
        da = 0.5 * _dot_nt(dy_ref[...].astype(BF16), w_ref[...])
        g = g_ref[...].astype(F32)
        up = up_ref[...].astype(F32)
        s = _sigmoid(g)
        o_ref[:, :F] = (da * up * (s * (1.0 + g * (1.0 - s)))).astype(BF16)
        o_ref[:, F:] = (da * (g * s)).astype(BF16)

    body, dep_spec, dep_arg = _with_dep(kern, 4, dep)
    return pl.pallas_call(
        body, name=name, grid=(T // tm,),
        in_specs=[pl.BlockSpec((tm, N), lambda i: (i, 0)),
                  pl.BlockSpec((F, N), lambda i: (0, 0)),
                  pl.BlockSpec((tm, F), lambda i: (i, 0)),
                  pl.BlockSpec((tm, F), lambda i: (i, 1))] + dep_spec,
        out_specs=pl.BlockSpec((tm, 2 * F), lambda i: (i, 0)),
        out_shape=jax.ShapeDtypeStruct((T, 2 * F), BF16),
        compiler_params=_params(1),
    )(dy, wd, gu, gu, *dep_arg)


def ffn_forward(h, g, w_gu, w_down, name, dep=None):
    gu, u = norm_matmul(h, g, w_gu, BF16, name + "_gu", tn=D_FF, dep=dep)
    h_out, a = ffn_down(gu, w_down, h, name + "_down")
    return h_out, (h, u, gu, a)


def ffn_backward(dy, saved, g, w_gu, w_down, name, dep=None):
    h, u, gu, a = saved
    dgu = ffn_down_bwd(dy, w_down, gu, name + "_ddown", dep=dep)
    d_w_down = matmul_tn(a, dy, name + "_dwd", scale=0.5, tk=D_FF // 2)
    d_w_gu = matmul_tn(u, dgu, name + "_dwgu", tn=D_FF)
    dh, dg = matmul_nt_norm_bwd(dgu, w_gu, h, g, dy, name + "_dx")
    return dh, dg, d_w_gu, d_w_down


def conformer_conv_fwd(z, cw, cb, lg, lb, name, tm=512):
    T = z.shape[0]
    C = CONV_A_CH
    tm = _tile(T, tm)
    hb = tm // CONV_HALO

    def body(v_ref, gt_ref, pv_ref, pg_ref, cw_ref, cb_ref, lg_ref, lb_ref, o_ref, xs_ref):
        i = pl.program_id(0)
        prev = pv_ref[...] * _sigmoid(pg_ref[...])
        xs_ref[0:CONV_HALO, :] = jnp.where(i > 0, prev, 0.0)
        xs_ref[CONV_HALO:, :] = v_ref[...] * _sigmoid(gt_ref[...])
        acc = jnp.zeros((tm, C), F32) + cb_ref[...]
        for k in range(CONV_A_WIDTH):
            acc = acc + cw_ref[k:k + 1, :] * xs_ref[pl.ds(CONV_HALO - (CONV_A_WIDTH - 1) + k, tm), :]
        mu = jnp.mean(acc, axis=-1, keepdims=True)
        xc = acc - mu
        var = jnp.mean(xc * xc, axis=-1, keepdims=True)
        y = (xc * lax.rsqrt(var + LN_EPS)) * lg_ref[...] + lb_ref[...]
        o_ref[...] = (y * _sigmoid(y)).astype(BF16)

    return pl.pallas_call(
        body, name=name, grid=(T // tm,),
        in_specs=[pl.BlockSpec((tm, C), lambda i: (i, 0)),
                  pl.BlockSpec((tm, C), lambda i: (i, 1)),
                  pl.BlockSpec((CONV_HALO, C), lambda i: (jnp.maximum(i * hb - 1, 0), 0)),
                  pl.BlockSpec((CONV_HALO, C), lambda i: (jnp.maximum(i * hb - 1, 0), 1)),
                  pl.BlockSpec((32, C), lambda i: (0, 0)),
                  pl.BlockSpec((1, C), lambda i: (0, 0)),
                  pl.BlockSpec((1, C), lambda i: (0, 0)),
                  pl.BlockSpec((1, C), lambda i: (0, 0))],
        out_specs=pl.BlockSpec((tm, C), lambda i: (i, 0)),
        out_shape=jax.ShapeDtypeStruct((T, C), BF16),
        scratch_shapes=[pltpu.VMEM((tm + CONV_HALO, C), F32)],
        compiler_params=_params(1),
    )(z, z, z, z, cw, cb, lg, lb)


def conformer_conv_bwd(z, dm, cw, cb, lg, lb, name, tm=512):
    T = z.shape[0]
    C = CONV_A_CH
    tm = _tile(T, tm)
    hb = tm // CONV_HALO
    n_tiles = T // tm
    last_halo = T // CONV_HALO - 1
    R = tm + CONV_HALO
    KW = CONV_A_WIDTH

    def body(v_ref, gt_ref, pv_ref, pg_ref, nv_ref, ng_ref, do_ref, ndo_ref, cw_ref, cb_ref, lg_ref, lb_ref,
             dz_ref, dcw_ref, dcb_ref, dlg_ref, dlb_ref, xs_ref, ds_ref):
        i = pl.program_id(0)

        @pl.when(i == 0)
        def _():
            dcw_ref[...] = jnp.zeros_like(dcw_ref)
            dcb_ref[...] = jnp.zeros_like(dcb_ref)
            dlg_ref[...] = jnp.zeros_like(dlg_ref)
            dlb_ref[...] = jnp.zeros_like(dlb_ref)

        val = v_ref[...]
        sg = _sigmoid(gt_ref[...])
        prev = pv_ref[...] * _sigmoid(pg_ref[...])
        xs_ref[0:CONV_HALO, :] = jnp.where(i > 0, prev, 0.0)
        xs_ref[CONV_HALO:CONV_HALO + tm, :] = val * sg
        xs_ref[CONV_HALO + tm:, :] = nv_ref[...] * _sigmoid(ng_ref[...])

        acc = jnp.zeros((R, C), F32) + cb_ref[...]
        for k in range(KW):
            acc = acc + cw_ref[k:k + 1, :] * xs_ref[pl.ds(CONV_HALO - (KW - 1) + k, R), :]
        mu = jnp.mean(acc, axis=-1, keepdims=True)
        xc = acc - mu
        rstd = lax.rsqrt(jnp.mean(xc * xc, axis=-1, keepdims=True) + LN_EPS)
        xh = xc * rstd
        y = xh * lg_ref[...] + lb_ref[...]
        s = _sigmoid(y)
        dout = jnp.concatenate([do_ref[...], jnp.where(i < n_tiles - 1, ndo_ref[...], 0.0)], axis=0)
        dy = dout * (s * (1.0 + y * (1.0 - s)))
        dxh = dy * lg_ref[...]
        dconv = rstd * (dxh - jnp.mean(dxh, axis=-1, keepdims=True) - xh * jnp.mean(dxh * xh, axis=-1, keepdims=True))
        ds_ref[...] = dconv

        dy_m = dy[:tm]
        dlg_ref[...] += jnp.sum(dy_m * xh[:tm], axis=0, keepdims=True)
        dlb_ref[...] += jnp.sum(dy_m, axis=0, keepdims=True)
        dc_m = dconv[:tm]
        dcb_ref[...] += jnp.sum(dc_m, axis=0, keepdims=True)
        dglu = jnp.zeros((tm, C), F32)
        for k in range(KW):
            dcw_ref[k:k + 1, :] += jnp.sum(dc_m * xs_ref[pl.ds(CONV_HALO - (KW - 1) + k, tm), :], axis=0, keepdims=True)
            dglu = dglu + cw_ref[k:k + 1, :] * ds_ref[pl.ds(KW - 1 - k, tm), :]
        dz_ref[:, :C] = (dglu * sg).astype(BF16)
        dz_ref[:, C:] = (dglu * val * sg * (1.0 - sg)).astype(BF16)

    prev_map = lambda i: jnp.maximum(i * hb - 1, 0)
    next_map = lambda i: jnp.minimum((i + 1) * hb, last_halo)
    return pl.pallas_call(
        body, name=name, grid=(n_tiles,),
        in_specs=[pl.BlockSpec((tm, C), lambda i: (i, 0)),
                  pl.BlockSpec((tm, C), lambda i: (i, 1)),
                  pl.BlockSpec((CONV_HALO, C), lambda i: (prev_map(i), 0)),
                  pl.BlockSpec((CONV_HALO, C), lambda i: (prev_map(i), 1)),
                  pl.BlockSpec((CONV_HALO, C), lambda i: (next_map(i), 0)),
                  pl.BlockSpec((CONV_HALO, C), lambda i: (next_map(i), 1)),
                  pl.BlockSpec((tm, C), lambda i: (i, 0)),
                  pl.BlockSpec((CONV_HALO, C), lambda i: (next_map(i), 0)),
                  pl.BlockSpec((32, C), lambda i: (0, 0)),
                  pl.BlockSpec((1, C), lambda i: (0, 0)),
                  pl.BlockSpec((1, C), lambda i: (0, 0)),
                  pl.BlockSpec((1, C), lambda i: (0, 0))],
        out_specs=[pl.BlockSpec((tm, 2 * C), lambda i: (i, 0)),
                   pl.BlockSpec((32, C), lambda i: (0, 0)),
                   pl.BlockSpec((1, C), lambda i: (0, 0)),
                   pl.BlockSpec((1, C), lambda i: (0, 0)),
                   pl.BlockSpec((1, C), lambda i: (0, 0))],
        out_shape=[jax.ShapeDtypeStruct((T, 2 * C), BF16),
                   jax.ShapeDtypeStruct((32, C), F32),
                   jax.ShapeDtypeStruct((1, C), F32),
                   jax.ShapeDtypeStruct((1, C), F32),
                   jax.ShapeDtypeStruct((1, C), F32)],
        scratch_shapes=[pltpu.VMEM((tm + 2 * CONV_HALO, C), F32), pltpu.VMEM((R, C), F32)],
        compiler_params=_params(1),
    )(z, z, z, z, z, z, dm, dm, cw, cb, lg, lb)


def _swa_scores(q_h, kk_h, slope, bias_dist, valid, sink):
    s = _dot_nt(q_h, kk_h) * (HEAD_DIM ** -0.5) - slope * bias_dist
    s = jnp.where(valid, s, NEG_BIG)
    m = jnp.maximum(jnp.max(s, axis=-1, keepdims=True), sink)
    p = jnp.exp(s - m)
    e_sink = jnp.exp(sink - m)
    inv = 1.0 / (jnp.sum(p, axis=-1, keepdims=True) + e_sink)
    return p * inv, e_sink * inv


def _swa_mask(r0):
    qi = lax.broadcasted_iota(jnp.int32, (WINDOW, 2 * WINDOW), 0)
    kj = lax.broadcasted_iota(jnp.int32, (WINDOW, 2 * WINDOW), 1)
    dist = qi + WINDOW - kj
    valid = (dist >= 0) & (dist < WINDOW) & (r0 - WINDOW + kj >= 0)
    return dist.astype(F32), valid


def swa_fwd(z, kpad, vpad, sinks, name, tq=512):
    T = z.shape[0]
    tq = _tile(T, tq)
    HQ = SWA_HEADS * HEAD_DIM

    def body(sink_ref, q_ref, k_ref, v_ref, o_ref):
        i = pl.program_id(0)
        for sub in range(tq // WINDOW):
            r0 = pl.multiple_of(i * tq + sub * WINDOW, WINDOW)
            kk = k_ref[pl.ds(r0, 2 * WINDOW), :]
            vv = v_ref[pl.ds(r0, 2 * WINDOW), :]
            qb = q_ref[sub * WINDOW:(sub + 1) * WINDOW, :].astype(BF16)
            dist, valid = _swa_mask(r0)
            outs = []
            for h in range(SWA_HEADS):
                kh = h // SWA_GROUP
                ks = slice(kh * HEAD_DIM, (kh + 1) * HEAD_DIM)
                pn, _ = _swa_scores(qb[:, h * HEAD_DIM:(h + 1) * HEAD_DIM], kk[:, ks], 2.0 ** (-(h + 1)), dist, valid,
                                    sink_ref[h])
                outs.append(_dot(pn.astype(BF16), vv[:, ks]))
            o_ref[sub * WINDOW:(sub + 1) * WINDOW, :] = jnp.concatenate(outs, axis=-1).astype(BF16)

    return pl.pallas_call(
        body, name=name, grid=(T // tq,),
        in_specs=[pl.BlockSpec(memory_space=pltpu.SMEM),
                  pl.BlockSpec((tq, HQ), lambda i: (i, 2)),
                  pl.BlockSpec((T + WINDOW, 2 * HEAD_DIM), lambda i: (0, 0)),
                  pl.BlockSpec((T + WINDOW, 2 * HEAD_DIM), lambda i: (0, 0))],
        out_specs=pl.BlockSpec((tq, HQ), lambda i: (i, 0)),
        out_shape=jax.ShapeDtypeStruct((T, HQ), BF16),
        compiler_params=_params(1),
    )(sinks, z, kpad, vpad)


def swa_bwd(z, kpad, vpad, sinks, dm, name, tq=512):
    T = z.shape[0]
    tq = _tile(T, tq)
    HQ = SWA_HEADS * HEAD_DIM
    scale = HEAD_DIM ** -0.5

    def body(sink_ref, q_ref, k_ref, v_ref, do_ref, dq_ref, dk_ref, dv_ref, dsink_ref):
        i = pl.program_id(0)

        @pl.when(i == 0)
        def _():
            dk_ref[...] = jnp.zeros_like(dk_ref)
            dv_ref[...] = jnp.zeros_like(dv_ref)
            dsink_ref[...] = jnp.zeros_like(dsink_ref)

        for sub in range(tq // WINDOW):
            r0 = pl.multiple_of(i * tq + sub * WINDOW, WINDOW)
            kk = k_ref[pl.ds(r0, 2 * WINDOW), :]
            vv = v_ref[pl.ds(r0, 2 * WINDOW), :]
            rows = slice(sub * WINDOW, (sub + 1) * WINDOW)
            qb = q_ref[rows, :].astype(BF16)
            dob = do_ref[rows, :].astype(BF16)
            dist, valid = _swa_mask(r0)
            dqs, dks, dvs = [], [], []
            for kh in range(SWA_KV_HEADS):
                ks = slice(kh * HEAD_DIM, (kh + 1) * HEAD_DIM)
                dk_acc = jnp.zeros((2 * WINDOW, HEAD_DIM), F32)
                dv_acc = jnp.zeros((2 * WINDOW, HEAD_DIM), F32)
                for g in range(SWA_GROUP):
                    h = kh * SWA_GROUP + g
                    hs = slice(h * HEAD_DIM, (h + 1) * HEAD_DIM)
                    pn, p_sink = _swa_scores(qb[:, hs], kk[:, ks], 2.0 ** (-(h + 1)), dist, valid, sink_ref[h])
                    dp = _dot_nt(dob[:, hs], vv[:, ks])
                    delta = jnp.sum(pn * dp, axis=-1, keepdims=True)
                    ds = (pn * (dp - delta)).astype(BF16)
                    dqs.append(_dot(ds, kk[:, ks]) * scale)
                    dk_acc = dk_acc + _dot_tn(ds, qb[:, hs]) * scale
                    dv_acc = dv_acc + _dot_tn(pn.astype(BF16), dob[:, hs])
                    dsink_ref[h:h + 1, :] += jnp.zeros((1, 128), F32) - jnp.sum(p_sink * delta)
                dks.append(dk_acc)
                dvs.append(dv_acc)
            dq_ref[rows, :] = jnp.concatenate(dqs, axis=-1).astype(BF16)
            dk_ref[pl.ds(r0, 2 * WINDOW), :] += jnp.concatenate(dks, axis=-1)
            dv_ref[pl.ds(r0, 2 * WINDOW), :] += jnp.concatenate(dvs, axis=-1)

    kv_spec = pl.BlockSpec((T + WINDOW, 2 * HEAD_DIM), lambda i: (0, 0))
    return pl.pallas_call(
        body, name=name, grid=(T // tq,),
        in_specs=[pl.BlockSpec(memory_space=pltpu.SMEM),
                  pl.BlockSpec((tq, HQ), lambda i: (i, 2)),
                  kv_spec, kv_spec,
                  pl.BlockSpec((tq, HQ), lambda i: (i, 1))],
        out_specs=[pl.BlockSpec((tq, HQ), lambda i: (i, 0)),
                   kv_spec, kv_spec,
                   pl.BlockSpec((SWA_HEADS, 128), lambda i: (0, 0))],
        out_shape=[jax.ShapeDtypeStruct((T, HQ), BF16),
                   jax.ShapeDtypeStruct((T + WINDOW, 2 * HEAD_DIM), F32),
                   jax.ShapeDtypeStruct((T + WINDOW, 2 * HEAD_DIM), F32),
                   jax.ShapeDtypeStruct((SWA_HEADS, 128), F32)],
        compiler_params=_params(1),
    )(sinks, z, kpad, vpad, dm)


def short_conv_fwd(z, w, name, tm=512):
    T = z.shape[0]
    C = SC_CH
    tm = _tile(T, tm)
    hb = tm // SC_HALO

    def body(b_ref, c_ref, v_ref, pc_ref, pv_ref, w_ref, o_ref, xs_ref):
        i = pl.program_id(0)
        xs_ref[0:SC_HALO, :] = jnp.where(i > 0, pc_ref[...] * pv_ref[...], 0.0)
        xs_ref[SC_HALO:, :] = c_ref[...] * v_ref[...]
        conv = jnp.zeros((tm, C), F32)
        for k in range(3):
            conv = conv + w_ref[k:k + 1, :] * xs_ref[pl.ds(SC_HALO - 2 + k, tm), :]
        o_ref[...] = (b_ref[...] * conv).astype(BF16)

    prev_map = lambda i: jnp.maximum(i * hb - 1, 0)
    return pl.pallas_call(
        body, name=name, grid=(T // tm,),
        in_specs=[pl.BlockSpec((tm, C), lambda i: (i, 0)),
                  pl.BlockSpec((tm, C), lambda i: (i, 1)),
                  pl.BlockSpec((tm, C), lambda i: (i, 2)),
                  pl.BlockSpec((SC_HALO, C), lambda i: (prev_map(i), 1)),
                  pl.BlockSpec((SC_HALO, C), lambda i: (prev_map(i), 2)),
                  pl.BlockSpec((8, C), lambda i: (0, 0))],
        out_specs=pl.BlockSpec((tm, C), lambda i: (i, 0)),
        out_shape=jax.ShapeDtypeStruct((T, C), BF16),
        scratch_shapes=[pltpu.VMEM((tm + SC_HALO, C), F32)],
        compiler_params=_params(1),
    )(z, z, z, z, z, w)


def short_conv_bwd(z, dm, w, name, tm=512):
    T = z.shape[0]
    C = SC_CH
    tm = _tile(T, tm)
    hb = tm // SC_HALO
    n_tiles = T // tm
    last_halo = T // SC_HALO - 1
    R = tm + SC_HALO

    def body(b_ref, c_ref, v_ref, pc_ref, pv_ref, nb_ref, do_ref, ndo_ref, w_ref, dz_ref, dw_ref, xs_ref, ds_ref):
        i = pl.program_id(0)

        @pl.when(i == 0)
        def _():
            dw_ref[...] = jnp.zeros_like(dw_ref)

        c = c_ref[...]
        val = v_ref[...]
        dout = do_ref[...]
        xs_ref[0:SC_HALO, :] = jnp.where(i > 0, pc_ref[...] * pv_ref[...], 0.0)
        xs_ref[SC_HALO:, :] = c * val
        dconv = dout * b_ref[...]
        ds_ref[0:tm, :] = dconv
        ds_ref[tm:, :] = jnp.where(i < n_tiles - 1, ndo_ref[...] * nb_ref[...], 0.0)
        conv = jnp.zeros((tm, C), F32)
        dcv = jnp.zeros((tm, C), F32)
        for k in range(3):
            xk = xs_ref[pl.ds(SC_HALO - 2 + k, tm), :]
            conv = conv + w_ref[k:k + 1, :] * xk
            dw_ref[k:k + 1, :] += jnp.sum(dconv * xk, axis=0, keepdims=True)
            dcv = dcv + w_ref[k:k + 1, :] * ds_ref[pl.ds(2 - k, tm), :]
        dz_ref[:, 0:C] = (dout * conv).astype(BF16)
        dz_ref[:, C:2 * C] = (dcv * val).astype(BF16)
        dz_ref[:, 2 * C:] = (dcv * c).astype(BF16)

    prev_map = lambda i: jnp.maximum(i * hb - 1, 0)
    next_map = lambda i: jnp.minimum((i + 1) * hb, last_halo)
    return pl.pallas_call(
        body, name=name, grid=(n_tiles,),
        in_specs=[pl.BlockSpec((tm, C), lambda i: (i, 0)),
                  pl.BlockSpec((tm, C), lambda i: (i, 1)),
                  pl.BlockSpec((tm, C), lambda i: (i, 2)),
                  pl.BlockSpec((SC_HALO, C), lambda i: (prev_map(i), 1)),
                  pl.BlockSpec((SC_HALO, C), lambda i: (prev_map(i), 2)),
                  pl.BlockSpec((SC_HALO, C), lambda i: (next_map(i), 0)),
                  pl.BlockSpec((tm, C), lambda i: (i, 0)),
                  pl.BlockSpec((SC_HALO, C), lambda i: (next_map(i), 0)),
                  pl.BlockSpec((8, C), lambda i: (0, 0))],
        out_specs=[pl.BlockSpec((tm, 3 * C), lambda i: (i, 0)),
                   pl.BlockSpec((8, C), lambda i: (0, 0))],
        out_shape=[jax.ShapeDtypeStruct((T, 3 * C), BF16), jax.ShapeDtypeStruct((8, C), F32)],
        scratch_shapes=[pltpu.VMEM((tm + SC_HALO, C), F32), pltpu.VMEM((R, C), F32)],
        compiler_params=_params(1),
    )(z, z, z, z, z, z, dm, dm, w)


def _xa_probs(q_h, k_h):
    s = _dot_nt(q_h, k_h) * (XA_HEAD_DIM ** -0.5)
    p = jnp.exp(s - jnp.max(s, axis=-1, keepdims=True))
    return p * (1.0 / jnp.sum(p, axis=-1, keepdims=True))


def xattn_fwd(q, kv, name, tm=512):
    T = q.shape[0]
    M = kv.shape[0]
    tm = _tile(T, tm)

    def body(q_ref, k_ref, v_ref, o_ref):
        for h in range(XA_HEADS):
            hs = slice(h * XA_HEAD_DIM, (h + 1) * XA_HEAD_DIM)
            p = _xa_probs(q_ref[:, hs], k_ref[:, hs])
            o_ref[:, hs] = _dot(p.astype(BF16), v_ref[:, hs]).astype(BF16)

    return pl.pallas_call(
        body, name=name, grid=(T // tm,),
        in_specs=[pl.BlockSpec((tm, D_MODEL), lambda i: (i, 0)),
                  pl.BlockSpec((M, D_MODEL), lambda i: (0, 0)),
                  pl.BlockSpec((M, D_MODEL), lambda i: (0, 1))],
        out_specs=pl.BlockSpec((tm, D_MODEL), lambda i: (i, 0)),
        out_shape=jax.ShapeDtypeStruct((T, D_MODEL), BF16),
        compiler_params=_params(1),
    )(q, kv, kv)


def xattn_bwd(q, kv, do, name, tm=512):
    T = q.shape[0]
    M = kv.shape[0]
    tm = _tile(T, tm)
    scale = XA_HEAD_DIM ** -0.5

    def body(q_ref, k_ref, v_ref, do_ref, dq_ref, dkv_ref):
        @pl.when(pl.program_id(0) == 0)
        def _():
            dkv_ref[...] = jnp.zeros_like(dkv_ref)

        for h in range(XA_HEADS):
            hs = slice(h * XA_HEAD_DIM, (h + 1) * XA_HEAD_DIM)
            vs = slice(D_MODEL + h * XA_HEAD_DIM, D_MODEL + (h + 1) * XA_HEAD_DIM)
            q_h = q_ref[:, hs]
            do_h = do_ref[:, hs]
            p = _xa_probs(q_h, k_ref[:, hs])
            dp = _dot_nt(do_h, v_ref[:, hs])
            ds = (p * (dp - jnp.sum(p * dp, axis=-1, keepdims=True))).astype(BF16)
            dq_ref[:, hs] = (_dot(ds, k_ref[:, hs]) * scale).astype(BF16)
            dkv_ref[:, hs] += _dot_tn(ds, q_h) * scale
            dkv_ref[:, vs] += _dot_tn(p.astype(BF16), do_h)

    return pl.pallas_call(
        body, name=name, grid=(T // tm,),
        in_specs=[pl.BlockSpec((tm, D_MODEL), lambda i: (i, 0)),
                  pl.BlockSpec((M, D_MODEL), lambda i: (0, 0)),
                  pl.BlockSpec((M, D_MODEL), lambda i: (0, 1)),
                  pl.BlockSpec((tm, D_MODEL), lambda i: (i, 0))],
        out_specs=[pl.BlockSpec((tm, D_MODEL), lambda i: (i, 0)),
                   pl.BlockSpec((M, 2 * D_MODEL), lambda i: (0, 0))],
        out_shape=[jax.ShapeDtypeStruct((T, D_MODEL), BF16), jax.ShapeDtypeStruct((M, 2 * D_MODEL), F32)],
        compiler_params=_params(1),
    )(q, kv, kv, do)


def final_loss(h, g, target, name, tm=512):
    T, K = h.shape
    tm = _tile(T, tm)

    def body(h_ref, g_ref, t_ref, dh_ref, dg_ref, loss_ref):
        @pl.when(pl.program_id(0) == 0)
        def _():
            dg_ref[...] = jnp.zeros_like(dg_ref)
            loss_ref[...] = jnp.zeros_like(loss_ref)

        x = h_ref[...]
        r = lax.rsqrt(jnp.mean(x * x, axis=-1, keepdims=True) + RMS_EPS)
        xh = x * r
        e = xh * g_ref[...] - t_ref[...]
        loss_ref[...] += jnp.zeros((1, 128), F32) + 0.5 * jnp.sum(jnp.mean(e * e, axis=-1, keepdims=True))
        dy = e * (1.0 / K)
        dg_ref[...] += jnp.sum(dy * xh, axis=0, keepdims=True)
        dxh = dy * g_ref[...]
        dh_ref[...] = r * (dxh - xh * jnp.mean(dxh * xh, axis=-1, keepdims=True))

    return pl.pallas_call(
        body, name=name, grid=(T // tm,),
        in_specs=[pl.BlockSpec((tm, K), lambda i: (i, 0)),
                  pl.BlockSpec((1, K), lambda i: (0, 0)),
                  pl.BlockSpec((tm, K), lambda i: (i, 0))],
        out_specs=[pl.BlockSpec((tm, K), lambda i: (i, 0)),
                   pl.BlockSpec((1, K), lambda i: (0, 0)),
                   pl.BlockSpec((1, 128), lambda i: (0, 0))],
        out_shape=[jax.ShapeDtypeStruct((T, K), F32), jax.ShapeDtypeStruct((1, K), F32),
                   jax.ShapeDtypeStruct((1, 128), F32)],
        compiler_params=_params(1),
    )(h, g, target)


def _row(v):
    return v.reshape(1, -1)


def _pad_rows(a, rows):
    return jnp.pad(a, ((0, rows - a.shape[0]), (0, 0)))


def local_step(x, mem, target, P, get_weights, put_grads):
    cw = _pad_rows(P["conv_a_w"], 32)
    scw = _pad_rows(P["sc_conv_w"], 8)
    cb, lg, lb = _row(P["conv_a_b"]), _row(P["conv_a_ln_g"]), _row(P["conv_a_ln_b"])
    sinks = P["swa_sinks"]

    class _Layered:
        def __init__(self, store, name=None):
            self.store, self.name = store, name

        def __getitem__(self, key):
            if self.name is None:
                return self.store[(key, 0)] if key in ("even_w_in", "even_w_out", "odd_w_in", "odd_w_out") \
                    else _Layered(self.store, key)
            return self.store[(self.name, key)]

    store = {}
    W = _Layered(store)
    saved = []
    h = x
    for i in range(2):
        L = f"l{i}"
        new, dep = get_weights("A" if i == 0 else "D", h)
        store.update(new)
        h, s_ffn1 = ffn_forward(h, P["ffn1_norm"][i:i + 1], W["ffn1_w_gu"][i], W["ffn1_w_down"][i], L + "_ffn1", dep=dep)
        h1 = h
        if i == 0:
            new, _ = get_weights("B", h)
            store.update(new)
            z, u2 = norm_matmul(h1, P["mix_norm"][i:i + 1], W["even_w_in"], F32, L + "_mix_in")
            a = conformer_conv_fwd(z, cw, cb, lg, lb, L + "_conv")
            kpad = jnp.pad(z[:, 1536:1664].astype(BF16), ((WINDOW, 0), (0, 0)))
            vpad = jnp.pad(z[:, 1664:1792].astype(BF16), ((WINDOW, 0), (0, 0)))
            o = swa_fwd(z, kpad, vpad, sinks, L + "_swa")
            m = jnp.concatenate([a, o], axis=-1)
            h = matmul_residual(m, W["even_w_out"], h1, L + "_mix_out")
            s_mix = (h1, u2, z, m, kpad, vpad)
        else:
            z, u2 = norm_matmul(h1, P["mix_norm"][i:i + 1], W["odd_w_in"], F32, L + "_mix_in")
            m = short_conv_fwd(z, scw, L + "_sconv")
            h = matmul_residual(m, W["odd_w_out"], h1, L + "_mix_out")
            s_mix = (h1, u2, z, m)
        h2 = h
        kv, umem = norm_matmul(mem, P["xa_mem_norm"][i:i + 1], W["xa_wkv"][i], BF16, L + "_xa_kv", tm=256)
        q, u3 = norm_matmul(h2, P["xa_norm"][i:i + 1], W["xa_wq"][i], BF16, L + "_xa_q")
        o = xattn_fwd(q, kv, L + "_xa")
        h = matmul_residual(o, W["xa_wo"][i], h2, L + "_xa_out")
        s_xa = (h2, u3, q, o, kv, umem)
        if i == 0:
            new, _ = get_weights("C", h)
            store.update(new)
        h, s_ffn2 = ffn_forward(h, P["ffn2_norm"][i:i + 1], W["ffn2_w_gu"][i], W["ffn2_w_down"][i], L + "_ffn2")
        saved.append((s_ffn1, s_mix, s_xa, s_ffn2))

    dh, d_final, loss = final_loss(h, _row(P["final_norm"]), target, "final_loss")

    names = ("ffn1_w_gu", "ffn1_w_down", "ffn2_w_gu", "ffn2_w_down", "xa_wq", "xa_wkv", "xa_wo", "even_w_in", "even_w_out",
             "odd_w_in", "odd_w_out")
    dW = {k: [None, None] for k in names}
    dP = {k: [None, None] for k in ("ffn1_norm", "mix_norm", "xa_norm", "xa_mem_norm", "ffn2_norm")}
    dP["final_norm"] = d_final.reshape(-1)
    dep = None
    for i in (1, 0):
        L = f"l{i}b"
        s_ffn1, s_mix, s_xa, s_ffn2 = saved[i]
        dh, dP["ffn2_norm"][i], dW["ffn2_w_gu"][i], dW["ffn2_w_down"][i] = ffn_backward(
            dh, s_ffn2, P["ffn2_norm"][i:i + 1], W["ffn2_w_gu"][i], W["ffn2_w_down"][i], L + "_ffn2", dep=dep)
        h2, u3, q, o, kv, umem = s_xa
        dW["xa_wo"][i] = matmul_tn(o, dh, L + "_xa_dwo")
        do = matmul_nt(dh, W["xa_wo"][i], BF16, L + "_xa_do")
        dq, dkv = xattn_bwd(q, kv, do, L + "_xa")
        dW["xa_wq"][i] = matmul_tn(u3, dq, L + "_xa_dwq")
        dW["xa_wkv"][i] = matmul_tn(umem, dkv, L + "_xa_dwkv", tn=1024)
        dkv_b = dkv.astype(BF16)
        _, dP["xa_mem_norm"][i] = matmul_nt_norm_bwd(dkv_b, W["xa_wkv"][i], mem, P["xa_mem_norm"][i:i + 1],
                                                     jnp.zeros_like(mem), L + "_xa_dmem")
        dh, dP["xa_norm"][i] = matmul_nt_norm_bwd(dq, W["xa_wq"][i], h2, P["xa_norm"][i:i + 1], dh, L + "_xa_dx")
        if i == 0:
            h1, u2, z, m, kpad, vpad = s_mix
            dW["even_w_out"][0] = matmul_tn(m, dh, L + "_mix_dwo")
            dm = matmul_nt(dh, W["even_w_out"], F32, L + "_mix_dm")
            dz_conv, dcw, dcb, dlg, dlb = conformer_conv_bwd(z, dm, cw, cb, lg, lb, L + "_conv")
            dq_s, dkp, dvp, dsk = swa_bwd(z, kpad, vpad, sinks, dm, L + "_swa")
            dz = jnp.concatenate([dz_conv, dq_s, dkp[WINDOW:].astype(BF16), dvp[WINDOW:].astype(BF16)], axis=-1)
            dW["even_w_in"][0] = matmul_tn(u2, dz, L + "_mix_dwi", tn=896)
            dh, dP["mix_norm"][i] = matmul_nt_norm_bwd(dz, W["even_w_in"], h1, P["mix_norm"][i:i + 1], dh, L + "_mix_dx")
            dP["conv_a_w"] = dcw[:CONV_A_WIDTH]
            dP["conv_a_b"], dP["conv_a_ln_g"], dP["conv_a_ln_b"] = dcb.reshape(-1), dlg.reshape(-1), dlb.reshape(-1)
            dP["swa_sinks"] = dsk[:, 0]
        else:
            h1, u2, z, m = s_mix
            dW["odd_w_out"][0] = matmul_tn(m, dh, L + "_mix_dwo")
            dm = matmul_nt(dh, W["odd_w_out"], F32, L + "_mix_dm")
            dz, dscw = short_conv_bwd(z, dm, scw, L + "_sconv")
            dW["odd_w_in"][0] = matmul_tn(u2, dz, L + "_mix_dwi", tn=1024)
            dh, dP["mix_norm"][i] = matmul_nt_norm_bwd(dz, W["odd_w_in"], h1, P["mix_norm"][i:i + 1], dh, L + "_mix_dx")
            dP["sc_conv_w"] = dscw[:3]
        if i == 0:
            dep = put_grads("BC", {k: dW[k[0]][k[1]] for k in STAGE_KEYS["B"] + STAGE_KEYS["C"]})
        dh, dP["ffn1_norm"][i], dW["ffn1_w_gu"][i], dW["ffn1_w_down"][i] = ffn_backward(
            dh, s_ffn1, P["ffn1_norm"][i:i + 1], W["ffn1_w_gu"][i], W["ffn1_w_down"][i], L + "_ffn1", dep=dep)
        stage = "D" if i == 1 else "A"
        dep = put_grads(stage, {k: dW[k[0]][k[1]] for k in STAGE_KEYS[stage]})
    for k in ("ffn1_norm", "mix_norm", "xa_norm", "xa_mem_norm", "ffn2_norm"):
        dP[k] = jnp.concatenate(dP[k], axis=0)
    return loss, dh, dP


def _mesh_pos():
    return lax.axis_index("x"), lax.axis_index("y"), lax.axis_index("c")


def _flat_index(px, py, pc):
    return 4 * px + 2 * py + pc


def all_gather(blob, name):
    R, C = blob.shape

    def body(x_ref, out_ref, send_sems, recv_sems, local_sem):
        x, y, c = _mesh_pos()
        me, sibling = (x, y, c), (x, y, 1 - c)
        chips = [(1 - x, y), (x, 1 - y), (1 - x, 1 - y)]

        def slot(px, py, pc):
            return out_ref.at[_flat_index(px, py, pc)]

        def copy(k, block, to, src=None):
            return pltpu.make_async_remote_copy(
                src_ref=slot(*block) if src is None else src, dst_ref=slot(*block),
                send_sem=send_sems.at[k], recv_sem=recv_sems.at[k],
                device_id=to, device_id_type=pl.DeviceIdType.MESH)

        mine = pltpu.make_async_copy(x_ref, slot(*me), local_sem)
        mine.start()
        first = [copy(0, me, sibling, src=x_ref)]
        first += [copy(1 + j, me, (*chip, c), src=x_ref) for j, chip in enumerate(chips)]
        for cp in first:
            cp.start()
        passed = [copy(4 + j, (*chip, c), sibling) for j, chip in enumerate(chips)]
        for j, chip in enumerate(chips):
            copy(1 + j, (*chip, c), me).wait_recv()
            passed[j].start()
        copy(0, sibling, me).wait_recv()
        for j, chip in enumerate(chips):
            copy(4 + j, (*chip, 1 - c), me).wait_recv()
        for cp in first + passed:
            cp.wait_send()
        mine.wait()

    return pl.pallas_call(
        body, name=name,
        out_shape=jax.ShapeDtypeStruct((N_DEV, R, C), blob.dtype),
        in_specs=[pl.BlockSpec(memory_space=pl.ANY)],
        out_specs=pl.BlockSpec(memory_space=pl.ANY),
        scratch_shapes=[pltpu.SemaphoreType.DMA((7,)), pltpu.SemaphoreType.DMA((7,)), pltpu.SemaphoreType.DMA],
    )(blob)


def scatter_exchange(g, name):
    _, R, C = g.shape

    def body(g_ref, out_ref, send_sems, recv_sems, local_sem):
        x, y, c = _mesh_pos()
        me_idx = _flat_index(x, y, c)
        mine = pltpu.make_async_copy(g_ref.at[me_idx], out_ref.at[me_idx], local_sem)
        mine.start()
        sends, peers = [], []
        for k in range(1, N_DEV):
            px = 1 - x if k & 4 else x
            py = 1 - y if k & 2 else y
            pc = 1 - c if k & 1 else c
            peer_idx = _flat_index(px, py, pc)
            cp = pltpu.make_async_remote_copy(
                src_ref=g_ref.at[peer_idx], dst_ref=out_ref.at[me_idx],
                send_sem=send_sems.at[k - 1], recv_sem=recv_sems.at[k - 1],
                device_id=(px, py, pc), device_id_type=pl.DeviceIdType.MESH)
            cp.start()
            sends.append(cp)
            peers.append((peer_idx, (px, py, pc)))
        for k in range(1, N_DEV):
            peer_idx, peer = peers[k - 1]
            pltpu.make_async_remote_copy(
                src_ref=g_ref.at[me_idx], dst_ref=out_ref.at[peer_idx],
                send_sem=send_sems.at[k - 1], recv_sem=recv_sems.at[k - 1],
                device_id=peer, device_id_type=pl.DeviceIdType.MESH).wait_recv()
        for cp in sends:
            cp.wait_send()
        mine.wait()

    return pl.pallas_call(
        body, name=name,
        out_shape=jax.ShapeDtypeStruct(g.shape, g.dtype),
        in_specs=[pl.BlockSpec(memory_space=pl.ANY)],
        out_specs=pl.BlockSpec(memory_space=pl.ANY),
        scratch_shapes=[pltpu.SemaphoreType.DMA((7,)), pltpu.SemaphoreType.DMA((7,)), pltpu.SemaphoreType.DMA],
    )(g)


HBM_SPEC = pl.BlockSpec(memory_space=pltpu.HBM)
SEM_SPEC = pl.BlockSpec(memory_space=pltpu.SEMAPHORE)
DATAFLOW_EFFECT = pltpu.SideEffectType.DATAFLOW_SIDE_EFFECTING


def _peers(x, y, c):
    out = []
    for k in range(1, N_DEV):
        pos = (1 - x if k & 4 else x, 1 - y if k & 2 else y, 1 - c if k & 1 else c)
        out.append((_flat_index(*pos), pos))
    return out


def place_own(src, scatter, name):
    R, C = src.shape[-2:]

    def body(src_ref, land_ref, sem):
        me = _flat_index(*_mesh_pos())
        cp = pltpu.make_async_copy(src_ref.at[me] if scatter else src_ref, land_ref.at[me], sem)
        cp.start()
        cp.wait()

    return pl.pallas_call(
        body, name=name, out_shape=jax.ShapeDtypeStruct((N_DEV, R, C), src.dtype),
        in_specs=[ANY_SPEC], out_specs=ANY_SPEC, scratch_shapes=[pltpu.SemaphoreType.DMA],
    )(src)


def _exchange_copy(src_ref, land_ref, send_sems, recv_sems, j, me, peer_idx, peer, scatter):
    return pltpu.make_async_remote_copy(
        src_ref=src_ref.at[peer_idx] if scatter else src_ref, dst_ref=land_ref.at[me],
        send_sem=send_sems.at[j], recv_sem=recv_sems.at[j], device_id=peer, device_id_type=pl.DeviceIdType.MESH)


def exchange_start(srcs, lands, scatter, after, name):
    n = len(srcs)

    def body(*refs):
        src_refs, land_refs = refs[:n], refs[n:2 * n]
        outs = refs[2 * n + 1:]
        send_sems, recv_sems, token = outs[:n], outs[n:2 * n], outs[4 * n]
        x, y, c = _mesh_pos()
        me = _flat_index(x, y, c)
        for g in range(n):
            for j, (peer_idx, peer) in enumerate(_peers(x, y, c)):
                _exchange_copy(src_refs[g], land_refs[g], send_sems[g], recv_sems[g], j, me, peer_idx, peer, scatter).start()
        token[...] = jnp.zeros_like(token)

    hbm = lambda a: pltpu.with_memory_space_constraint(a, pltpu.HBM)
    res = pl.pallas_call(
        body, name=name,
        out_shape=(*[pltpu.SemaphoreType.DMA((N_DEV - 1,))] * (2 * n),
                   *[pltpu.HBM(a.shape, a.dtype) for a in srcs], *[pltpu.HBM(a.shape, a.dtype) for a in lands],
                   jax.ShapeDtypeStruct((8, 128), F32)),
        in_specs=[HBM_SPEC] * (2 * n) + [ANY_SPEC],
        out_specs=(*[SEM_SPEC] * (2 * n), *[HBM_SPEC] * (2 * n), pl.BlockSpec(memory_space=pltpu.VMEM)),
        input_output_aliases={i: 2 * n + i for i in range(2 * n)},
        compiler_params=pltpu.CompilerParams(has_side_effects=DATAFLOW_EFFECT),
    )(*[hbm(a) for a in srcs], *[hbm(a) for a in lands], after)
    handles = [(res[g], res[n + g], res[2 * n + g], res[3 * n + g]) for g in range(n)]
    return handles, res[4 * n]


def exchange_wait(handle, scatter, after, name):
    send_sem, recv_sem, src_thru, land_thru = handle

    def body(src_ref, land_ref, send_sems, recv_sems, after_ref, src_dead, got_ref):
        x, y, c = _mesh_pos()
        me = _flat_index(x, y, c)
        for j, (peer_idx, peer) in enumerate(_peers(x, y, c)):
            mine = _exchange_copy(src_ref, land_ref, send_sems, recv_sems, j, me, peer_idx, peer, scatter)
            mine.wait_send()
            theirs = pltpu.make_async_remote_copy(
                src_ref=src_ref.at[me] if scatter else src_ref, dst_ref=land_ref.at[peer_idx],
                send_sem=send_sems.at[j], recv_sem=recv_sems.at[j], device_id=peer, device_id_type=pl.DeviceIdType.MESH)
            theirs.wait_recv()

    return pl.pallas_call(
        body, name=name,
        out_shape=(pltpu.HBM(src_thru.shape, src_thru.dtype), pltpu.HBM(land_thru.shape, land_thru.dtype)),
        in_specs=(HBM_SPEC, HBM_SPEC, SEM_SPEC, SEM_SPEC, ANY_SPEC), out_specs=(HBM_SPEC, HBM_SPEC),
        input_output_aliases={0: 0, 1: 1},
        compiler_params=pltpu.CompilerParams(has_side_effects=DATAFLOW_EFFECT),
    )(src_thru, land_thru, send_sem, recv_sem, after)[1]


def ordered_sum(parts, name, tr=512):
    n, R, C = parts.shape
    tr = next((t for t in range(min(tr, R), 15, -16) if R % t == 0), R)

    def body(p_ref, o_ref):
        acc = p_ref[0].astype(F32)
        for j in range(1, n):
            acc = acc + p_ref[j].astype(F32)
        o_ref[...] = acc

    return pl.pallas_call(
        body, name=name, grid=(R // tr,),
        in_specs=[pl.BlockSpec((n, tr, C), lambda i: (0, i, 0))],
        out_specs=pl.BlockSpec((tr, C), lambda i: (i, 0)),
        out_shape=jax.ShapeDtypeStruct((R, C), F32),
        compiler_params=_params(1),
    )(parts)


def adamw(w, g, m, v, name, tr=256):
    R, C = w.shape
    tr = next((t for t in range(tr, 7, -8) if R % t == 0), R)
    c1 = 1.0 - ADAM_B1 ** ADAM_STEP
    c2 = 1.0 - ADAM_B2 ** ADAM_STEP

    def body(w_ref, g_ref, m_ref, v_ref, d_ref, mo_ref, vo_ref):
        grad = g_ref[...]
        m2 = ADAM_B1 * m_ref[...] + (1.0 - ADAM_B1) * grad
        v2 = ADAM_B2 * v_ref[...] + (1.0 - ADAM_B2) * (grad * grad)
        mo_ref[...] = m2
        vo_ref[...] = v2
        d_ref[...] = -ADAM_LR * ((m2 / c1) / (jnp.sqrt(v2 / c2) + ADAM_EPS) + ADAM_WD * w_ref[...])

    spec = pl.BlockSpec((tr, C), lambda i: (i, 0))
    return pl.pallas_call(
        body, name=name, grid=(R // tr,),
        in_specs=[spec] * 4, out_specs=[spec] * 3,
        out_shape=[jax.ShapeDtypeStruct((R, C), F32)] * 3,
        compiler_params=_params(1),
    )(w, g, m, v)


WEIGHT_NAMES = ("ffn1_norm", "ffn1_w_gu", "ffn1_w_down", "mix_norm", "even_w_in", "conv_a_w", "conv_a_b", "conv_a_ln_g",
                "conv_a_ln_b", "swa_sinks", "even_w_out", "odd_w_in", "sc_conv_w", "odd_w_out", "xa_norm", "xa_mem_norm",
                "xa_wq", "xa_wkv", "xa_wo", "ffn2_norm", "ffn2_w_gu", "ffn2_w_down", "final_norm")
BLOB_COLS = 1024
SMALL_ROWS = (("ffn1_norm", 0, 2), ("mix_norm", 2, 2), ("xa_norm", 4, 2), ("xa_mem_norm", 6, 2), ("ffn2_norm", 8, 2),
              ("final_norm", 10, 1))
ROW_CONV_B_LNG = 11
ROW_LNB_SINKS_LOSS = 12
LOSS_COL = 512 + SWA_HEADS
ROW_SC_CONV = 13
ROW_CONV_W = 16
SMALL_BLOB_ROWS = 32
SMALL_ADAM_ROWS = 16


def _small_blob(v):
    rows = [v[n].reshape(-1, D_MODEL) for n, _, _ in SMALL_ROWS]
    rows.append(jnp.concatenate([v["conv_a_b"].reshape(-1), v["conv_a_ln_g"].reshape(-1)]).reshape(1, D_MODEL))
    tail = jnp.zeros((D_MODEL - 512 - SWA_HEADS,), F32)
    if "loss" in v:
        tail = tail.at[0].set(v["loss"])
    rows.append(jnp.concatenate([v["conv_a_ln_b"].reshape(-1), v["swa_sinks"].reshape(-1), tail]).reshape(1, D_MODEL))
    rows.append(jnp.zeros((SMALL_ADAM_ROWS - ROW_SC_CONV, D_MODEL), F32))
    return jnp.concatenate(rows, axis=0)


def _small_unblob(b, shapes):
    out = {n: b[r:r + k].reshape(shapes[n]) for n, r, k in SMALL_ROWS}
    out["conv_a_b"] = b[ROW_CONV_B_LNG, :512].reshape(shapes["conv_a_b"])
    out["conv_a_ln_g"] = b[ROW_CONV_B_LNG, 512:].reshape(shapes["conv_a_ln_g"])
    out["conv_a_ln_b"] = b[ROW_LNB_SINKS_LOSS, :512].reshape(shapes["conv_a_ln_b"])
    out["swa_sinks"] = b[ROW_LNB_SINKS_LOSS, 512:512 + SWA_HEADS].reshape(shapes["swa_sinks"])
    return out


def kernel(x, mem, ffn1_norm, ffn1_w_gu, ffn1_w_down, mix_norm, even_w_in, conv_a_w, conv_a_b, conv_a_ln_g, conv_a_ln_b, swa_sinks, even_w_out, odd_w_in, sc_conv_w, odd_w_out, xa_norm, xa_mem_norm, xa_wq, xa_wkv, xa_wo, ffn2_norm, ffn2_w_gu, ffn2_w_down, final_norm, loss_target, m_ffn1_norm, m_ffn1_w_gu, m_ffn1_w_down, m_mix_norm, m_even_w_in, m_conv_a_w, m_conv_a_b, m_conv_a_ln_g, m_conv_a_ln_b, m_swa_sinks, m_even_w_out, m_odd_w_in, m_sc_conv_w, m_odd_w_out, m_xa_norm, m_xa_mem_norm, m_xa_wq, m_xa_wkv, m_xa_wo, m_ffn2_norm, m_ffn2_w_gu, m_ffn2_w_down, m_final_norm, v_ffn1_norm, v_ffn1_w_gu, v_ffn1_w_down, v_mix_norm, v_even_w_in, v_conv_a_w, v_conv_a_b, v_conv_a_ln_g, v_conv_a_ln_b, v_swa_sinks, v_even_w_out, v_odd_w_in, v_sc_conv_w, v_odd_w_out, v_xa_norm, v_xa_mem_norm, v_xa_wq, v_xa_wkv, v_xa_wo, v_ffn2_norm, v_ffn2_w_gu, v_ffn2_w_down, v_final_norm):
    w = dict(ffn1_norm=ffn1_norm, ffn1_w_gu=ffn1_w_gu, ffn1_w_down=ffn1_w_down, mix_norm=mix_norm, even_w_in=even_w_in,
             conv_a_w=conv_a_w, conv_a_b=conv_a_b, conv_a_ln_g=conv_a_ln_g, conv_a_ln_b=conv_a_ln_b, swa_sinks=swa_sinks,
             even_w_out=even_w_out, odd_w_in=odd_w_in, sc_conv_w=sc_conv_w, odd_w_out=odd_w_out, xa_norm=xa_norm,
             xa_mem_norm=xa_mem_norm, xa_wq=xa_wq, xa_wkv=xa_wkv, xa_wo=xa_wo, ffn2_norm=ffn2_norm, ffn2_w_gu=ffn2_w_gu,
             ffn2_w_down=ffn2_w_down, final_norm=final_norm)
    m = dict(ffn1_norm=m_ffn1_norm, ffn1_w_gu=m_ffn1_w_gu, ffn1_w_down=m_ffn1_w_down, mix_norm=m_mix_norm,
             even_w_in=m_even_w_in, conv_a_w=m_conv_a_w, conv_a_b=m_conv_a_b, conv_a_ln_g=m_conv_a_ln_g,
             conv_a_ln_b=m_conv_a_ln_b, swa_sinks=m_swa_sinks, even_w_out=m_even_w_out, odd_w_in=m_odd_w_in,
             sc_conv_w=m_sc_conv_w, odd_w_out=m_odd_w_out, xa_norm=m_xa_norm, xa_mem_norm=m_xa_mem_norm, xa_wq=m_xa_wq,
             xa_wkv=m_xa_wkv, xa_wo=m_xa_wo, ffn2_norm=m_ffn2_norm, ffn2_w_gu=m_ffn2_w_gu, ffn2_w_down=m_ffn2_w_down,
             final_norm=m_final_norm)
    v = dict(ffn1_norm=v_ffn1_norm, ffn1_w_gu=v_ffn1_w_gu, ffn1_w_down=v_ffn1_w_down, mix_norm=v_mix_norm,
             even_w_in=v_even_w_in, conv_a_w=v_conv_a_w, conv_a_b=v_conv_a_b, conv_a_ln_g=v_conv_a_ln_g,
             conv_a_ln_b=v_conv_a_ln_b, swa_sinks=v_swa_sinks, even_w_out=v_even_w_out, odd_w_in=v_odd_w_in,
             sc_conv_w=v_sc_conv_w, odd_w_out=v_odd_w_out, xa_norm=v_xa_norm, xa_mem_norm=v_xa_mem_norm, xa_wq=v_xa_wq,
             xa_wkv=v_xa_wkv, xa_wo=v_xa_wo, ffn2_norm=v_ffn2_norm, ffn2_w_gu=v_ffn2_w_gu, ffn2_w_down=v_ffn2_w_down,
             final_norm=v_final_norm)
    me = _flat_index(*_mesh_pos())

    conv_blob = jnp.concatenate([w["conv_a_w"].reshape(-1), w["sc_conv_w"].reshape(-1),
                                 jnp.zeros((8 * 1024 - 31 * 64 - 3 * 128,), F32)]).reshape(8, 1024)
    conv_all = all_gather(conv_blob, "gather_conv_weights").reshape(N_DEV, 8 * 1024)
    conv_a_full = jnp.transpose(conv_all[:, :31 * 64].reshape(N_DEV, 31, 64), (1, 0, 2)).reshape(31, 512)
    sc_full = jnp.transpose(conv_all[:, 31 * 64:31 * 64 + 3 * 128].reshape(N_DEV, 3, 128), (1, 0, 2)).reshape(3, 1024)

    def stage_blob(keys):
        return jnp.concatenate([w[n][l].astype(BF16).reshape(-1, BLOB_COLS) for n, l in keys], axis=0)

    def stage_rows(keys):
        return [(n, l, w[n].shape[1] * w[n].shape[2] // BLOB_COLS) for n, l in keys]

    def unpack_weights(gathered, keys):
        out, off = {}, 0
        for n, l, rows in stage_rows(keys):
            k, cols = w[n].shape[1:]
            blk = gathered[:, off:off + rows, :].reshape(N_DEV, k, cols)
            out[(n, l)] = (jnp.transpose(blk, (1, 0, 2)).reshape(k, N_DEV * cols) if SPLIT_AXIS[n] == 1
                           else blk.reshape(N_DEV * k, cols))
            off += rows
        return out

    gathered_a = all_gather(stage_blob(STAGE_KEYS["A"]), "gather_weights_a")
    later = ("B", "C", "D")
    blobs = [stage_blob(STAGE_KEYS[s]) for s in later]
    lands = [place_own(b, False, "place_weights_" + s.lower()) for s, b in zip(later, blobs)]
    weight_handles, weight_token = exchange_start(blobs, lands, False, gathered_a, "gather_start")

    def get_weights(stage, after):
        if stage == "A":
            return unpack_weights(gathered_a, STAGE_KEYS["A"]), weight_token
        land = exchange_wait(weight_handles[later.index(stage)], False, after, "gather_wait_" + stage.lower())
        return unpack_weights(land, STAGE_KEYS[stage]), None

    grad_handles = {}

    def put_grads(stage, dws):
        parts = []
        for (n, l), dw in dws.items():
            k, cols = w[n].shape[1:]
            blk = (jnp.transpose(dw.reshape(k, N_DEV, cols), (1, 0, 2)) if SPLIT_AXIS[n] == 1
                   else dw.reshape(N_DEV, k, cols))
            parts.append(blk.reshape(N_DEV, k * cols // BLOB_COLS, BLOB_COLS))
        packed = jnp.concatenate(parts, axis=1)
        land = place_own(packed, True, "place_grads_" + stage.lower())
        (handle,), token = exchange_start([packed], [land], True, packed, "scatter_start_" + stage.lower())
        grad_handles[stage] = (handle, tuple(dws))
        return token

    P = dict(ffn1_norm=ffn1_norm, mix_norm=mix_norm, xa_norm=xa_norm, xa_mem_norm=xa_mem_norm, ffn2_norm=ffn2_norm,
             final_norm=final_norm, conv_a_w=conv_a_full, conv_a_b=conv_a_b[0], conv_a_ln_g=conv_a_ln_g[0],
             conv_a_ln_b=conv_a_ln_b[0], swa_sinks=swa_sinks[0], sc_conv_w=sc_full)

    loss_part, grad_x, dP = local_step(x[0], mem[0], loss_target[0], P, get_weights, put_grads)

    def finish_grads(stage, after):
        handle, keys = grad_handles[stage]
        land = exchange_wait(handle, True, after, "scatter_wait_" + stage.lower())
        rows_f32 = ordered_sum(land, "sum_grads_" + stage.lower())
        out, off = {}, 0
        for n, l, rows in stage_rows(keys):
            out[(n, l)] = rows_f32[off:off + rows].reshape(w[n].shape[1:])
            off += rows
        return out

    layer_grads = {**finish_grads("D", grad_x), **finish_grads("BC", grad_x)}

    dP = dict(dP, loss=loss_part[0, 0])
    small = jnp.concatenate([
        _small_blob(dP)[:ROW_SC_CONV], dP["sc_conv_w"],
        jnp.concatenate([dP["conv_a_w"].reshape(-1), jnp.zeros((512,), F32)]).reshape(16, D_MODEL)], axis=0)
    small_sum = ordered_sum(all_gather(small, "gather_small_grads"), "sum_small_grads", tr=SMALL_BLOB_ROWS)
    loss = small_sum[ROW_LNB_SINKS_LOSS, LOSS_COL]
    grads = _small_unblob(small_sum, {n: w[n].shape for n in WEIGHT_NAMES})
    sc_g = small_sum[ROW_SC_CONV:ROW_SC_CONV + 3]
    grads["sc_conv_w"] = lax.dynamic_slice(sc_g, (0, me * 128), (3, 128)).reshape(w["sc_conv_w"].shape)
    cw_g = small_sum[ROW_CONV_W:].reshape(-1)[:31 * 512].reshape(31, 512)
    grads["conv_a_w"] = lax.dynamic_slice(cw_g, (0, me * 64), (31, 64)).reshape(w["conv_a_w"].shape)

    delta, new_m, new_v = {}, {}, {}

    def update(n):
        shp = w[n].shape
        two_d = (shp[0] * shp[1], shp[2])
        d_, m_, v_ = adamw(w[n].reshape(two_d), grads[n].reshape(two_d), m[n].reshape(two_d), v[n].reshape(two_d),
                           "adamw_" + n)
        delta[n], new_m[n], new_v[n] = d_.reshape(shp), m_.reshape(shp), v_.reshape(shp)

    first_stage = tuple(n for n, _ in STAGE_KEYS["A"])
    for n in SPLIT_AXIS:
        if n not in first_stage:
            grads[n] = jnp.stack([layer_grads[(n, l)] for l in range(w[n].shape[0])], axis=0)
            update(n)
    update("conv_a_w")
    update("sc_conv_w")
    layer_grads.update(finish_grads("A", delta["ffn2_w_gu"]))
    for n in first_stage:
        grads[n] = jnp.stack([layer_grads[(n, l)] for l in range(w[n].shape[0])], axis=0)
        update(n)
    d_, m_, v_ = adamw(_small_blob(w), small_sum[:SMALL_ADAM_ROWS], _small_blob(m), _small_blob(v), "adamw_small",
                       tr=SMALL_ADAM_ROWS)
    shapes = {n: w[n].shape for n in WEIGHT_NAMES}
    delta.update(_small_unblob(d_, shapes))
    new_m.update(_small_unblob(m_, shapes))
    new_v.update(_small_unblob(v_, shapes))

    return (loss, grad_x[None], *[grads[n] for n in WEIGHT_NAMES], *[delta[n] for n in WEIGHT_NAMES],
            *[new_m[n] for n in WEIGHT_NAMES], *[new_v[n] for n in WEIGHT_NAMES])
```

```python
import functools

import jax
import jax.numpy as jnp
from jax import lax
from jax.experimental import pallas as pl
from jax.experimental.pallas import tpu as pltpu

F32 = jnp.float32
BF16 = jnp.bfloat16

D_MODEL = 1024
D_FF = 2816
CONV_A_CH = 512
CONV_A_WIDTH = 31
SWA_HEADS = 8
SWA_KV_HEADS = 2
SWA_GROUP = SWA_HEADS // SWA_KV_HEADS
HEAD_DIM = 64
WINDOW = 128
SC_CH = 1024
XA_HEADS = 4
XA_HEAD_DIM = D_MODEL // XA_HEADS
RMS_EPS = 1e-6
LN_EPS = 1e-5
ADAM_LR = 0.001
ADAM_B1 = 0.9
ADAM_B2 = 0.999
ADAM_EPS = 1e-08
ADAM_WD = 0.01
ADAM_STEP = 10
N_DEV = 8

V7X_VMEM_BYTES = 64 * 1024 * 1024
VMEM_LIMIT = V7X_VMEM_BYTES - 8 * 1024 * 1024
CONV_HALO = 32
SC_HALO = 8
NEG_BIG = -1e30

SPLIT_AXIS = dict(ffn1_w_gu=1, ffn1_w_down=0, even_w_in=1, even_w_out=0, odd_w_in=1, odd_w_out=0, xa_wq=0, xa_wkv=1, xa_wo=0,
                  ffn2_w_gu=1, ffn2_w_down=0)
STAGE_KEYS = dict(
    A=(("ffn1_w_gu", 0), ("ffn1_w_down", 0)),
    B=(("even_w_in", 0), ("even_w_out", 0), ("xa_wq", 0), ("xa_wkv", 0), ("xa_wo", 0)),
    C=(("ffn2_w_gu", 0), ("ffn2_w_down", 0)),
    D=(("ffn1_w_gu", 1), ("ffn1_w_down", 1), ("odd_w_in", 0), ("odd_w_out", 0), ("xa_wq", 1), ("xa_wkv", 1), ("xa_wo", 1),
       ("ffn2_w_gu", 1), ("ffn2_w_down", 1)))


def _params(n_axes):
    return pltpu.CompilerParams(dimension_semantics=("arbitrary",) * n_axes, vmem_limit_bytes=VMEM_LIMIT)


def _tile(n, pref):
    t = min(n, pref)
    assert n % t == 0, (n, pref)
    return t


def _dot(a, b):
    return jnp.dot(a, b, preferred_element_type=F32)


def _dot_nt(a, b):
    return lax.dot_general(a, b, (((1,), (1,)), ((), ())), preferred_element_type=F32)


def _dot_tn(a, b):
    return lax.dot_general(a, b, (((0,), (0,)), ((), ())), preferred_element_type=F32)


def _sigmoid(x):
    return 1.0 / (1.0 + jnp.exp(-x))


ANY_SPEC = pl.BlockSpec(memory_space=pl.ANY)


def _with_dep(body, n_in, dep):
    if dep is None:
        return body, [], []
    return (lambda *refs: body(*refs[:n_in], *refs[n_in + 1:])), [ANY_SPEC], [dep]


def norm_matmul(h, g, w, out_dtype, name, tm=512, tn=None, dep=None, transposed=True):
    T, K = h.shape
    N = w.shape[0] if transposed else w.shape[1]
    tm = _tile(T, tm)
    tn = N if tn is None else tn

    def kern(h_ref, g_ref, w_ref, z_ref, u_ref):
        @pl.when(pl.program_id(1) == 0)
        def _():
            x = h_ref[...]
            r = lax.rsqrt(jnp.mean(x * x, axis=-1, keepdims=True) + RMS_EPS)
            u_ref[...] = ((x * r) * g_ref[...]).astype(BF16)

        mm = _dot_nt if transposed else _dot
        z_ref[...] = mm(u_ref[...], w_ref[...]).astype(z_ref.dtype)

    body, dep_spec, dep_arg = _with_dep(kern, 3, dep)
    w_spec = pl.BlockSpec((tn, K), lambda i, j: (j, 0)) if transposed else pl.BlockSpec((K, tn), lambda i, j: (0, j))
    return pl.pallas_call(
        body, name=name, grid=(T // tm, N // tn),
        in_specs=[pl.BlockSpec((tm, K), lambda i, j: (i, 0)),
                  pl.BlockSpec((1, K), lambda i, j: (0, 0)),
                  w_spec] + dep_spec,
        out_specs=[pl.BlockSpec((tm, tn), lambda i, j: (i, j)),
                   pl.BlockSpec((tm, K), lambda i, j: (i, 0))],
        out_shape=[jax.ShapeDtypeStruct((T, N), out_dtype), jax.ShapeDtypeStruct((T, K), BF16)],
        compiler_params=_params(2),
    )(h, g, w, *dep_arg)


def matmul_residual(a, w, res, name, tm=512):
    T, K = a.shape
    N = w.shape[1]
    tm = _tile(T, tm)

    def body(a_ref, w_ref, r_ref, o_ref):
        o_ref[...] = r_ref[...] + _dot(a_ref[...], w_ref[...])

    return pl.pallas_call(
        body, name=name, grid=(T // tm,),
        in_specs=[pl.BlockSpec((tm, K), lambda i: (i, 0)),
                  pl.BlockSpec((K, N), lambda i: (0, 0)),
                  pl.BlockSpec((tm, N), lambda i: (i, 0))],
        out_specs=pl.BlockSpec((tm, N), lambda i: (i, 0)),
        out_shape=jax.ShapeDtypeStruct((T, N), F32),
        compiler_params=_params(1),
    )(a, w, res)


def matmul_nt(dy, w, out_dtype, name, tm=512):
    T, N = dy.shape
    K = w.shape[0]
    tm = _tile(T, tm)

    def body(dy_ref, w_ref, o_ref):
        o_ref[...] = _dot_nt(dy_ref[...].astype(BF16), w_ref[...]).astype(o_ref.dtype)

    return pl.pallas_call(
        body, name=name, grid=(T // tm,),
        in_specs=[pl.BlockSpec((tm, N), lambda i: (i, 0)),
                  pl.BlockSpec((K, N), lambda i: (0, 0))],
        out_specs=pl.BlockSpec((tm, K), lambda i: (i, 0)),
        out_shape=jax.ShapeDtypeStruct((T, K), out_dtype),
        compiler_params=_params(1),
    )(dy, w)


def matmul_norm_bwd(dz, w, h, g, dh_in, name, tm=256, transposed=True, dep=None):
    T, N = dz.shape
    K = h.shape[1]
    tm = _tile(T, tm)

    def kern(dz_ref, w_ref, h_ref, g_ref, dhin_ref, dh_ref, dg_ref):
        @pl.when(pl.program_id(0) == 0)
        def _():
            dg_ref[...] = jnp.zeros_like(dg_ref)

        mm = _dot if transposed else _dot_nt
        du = mm(dz_ref[...], w_ref[...])
        x = h_ref[...]
        r = lax.rsqrt(jnp.mean(x * x, axis=-1, keepdims=True) + RMS_EPS)
        xh = x * r
        dg_ref[...] += jnp.sum(du * xh, axis=0, keepdims=True)
        dxh = du * g_ref[...]
        dh_ref[...] = dhin_ref[...] + r * (dxh - xh * jnp.mean(dxh * xh, axis=-1, keepdims=True))

    body, dep_spec, dep_arg = _with_dep(kern, 5, dep)
    return pl.pallas_call(
        body, name=name, grid=(T // tm,),
        in_specs=[pl.BlockSpec((tm, N), lambda i: (i, 0)),
                  pl.BlockSpec(w.shape, lambda i: (0, 0)),
                  pl.BlockSpec((tm, K), lambda i: (i, 0)),
                  pl.BlockSpec((1, K), lambda i: (0, 0)),
                  pl.BlockSpec((tm, K), lambda i: (i, 0))] + dep_spec,
        out_specs=[pl.BlockSpec((tm, K), lambda i: (i, 0)),
                   pl.BlockSpec((1, K), lambda i: (0, 0))],
        out_shape=[jax.ShapeDtypeStruct((T, K), F32), jax.ShapeDtypeStruct((1, K), F32)],
        compiler_params=_params(1),
    )(dz, w, h, g, dh_in, *dep_arg)


def matmul_tn(x, dy, name, scale=1.0, tk=None, tn=None, tt=512):
    T, K = x.shape
    N = dy.shape[1]
    tk = K if tk is None else tk
    tn = N if tn is None else tn
    tt = _tile(T, tt)
    nt = T // tt

    def body(x_ref, dy_ref, o_ref, acc_ref):
        t = pl.program_id(2)

        @pl.when(t == 0)
        def _():
            acc_ref[...] = jnp.zeros_like(acc_ref)

        acc_ref[...] += _dot_tn(x_ref[...].astype(BF16), dy_ref[...].astype(BF16))

        @pl.when(t == nt - 1)
        def _():
            o_ref[...] = (acc_ref[...] * scale).astype(o_ref.dtype)

    return pl.pallas_call(
        body, name=name, grid=(K // tk, N // tn, nt),
        in_specs=[pl.BlockSpec((tt, tk), lambda a, b, t: (t, a)),
                  pl.BlockSpec((tt, tn), lambda a, b, t: (t, b))],
        out_specs=pl.BlockSpec((tk, tn), lambda a, b, t: (a, b)),
        out_shape=jax.ShapeDtypeStruct((K, N), BF16),
        scratch_shapes=[pltpu.VMEM((tk, tn), F32)],
        compiler_params=_params(3),
    )(x, dy)


def ffn_down(gu, wd, res, name, tm=512):
    T = gu.shape[0]
    F = gu.shape[1] // 2
    N = wd.shape[1]
    tm = _tile(T, tm)

    def body(g_ref, up_ref, w_ref, r_ref, o_ref, a_ref):
        g = g_ref[...].astype(F32)
        a = (g * _sigmoid(g)) * up_ref[...].astype(F32)
        a_ref[...] = a.astype(BF16)
        o_ref[...] = r_ref[...] + 0.5 * _dot(a_ref[...], w_ref[...])

    return pl.pallas_call(
        body, name=name, grid=(T // tm,),
        in_specs=[pl.BlockSpec((tm, F), lambda i: (i, 0)),
                  pl.BlockSpec((tm, F), lambda i: (i, 1)),
                  pl.BlockSpec((F, N), lambda i: (0, 0)),
                  pl.BlockSpec((tm, N), lambda i: (i, 0))],
        out_specs=[pl.BlockSpec((tm, N), lambda i: (i, 0)),
                   pl.BlockSpec((tm, F), lambda i: (i, 0))],
        out_shape=[jax.ShapeDtypeStruct((T, N), F32), jax.ShapeDtypeStruct((T, F), BF16)],
        compiler_params=_params(1),
    )(gu, gu, wd, res)


def ffn_down_bwd(dy, wd, gu, name, tm=512, dep=None):
    T, N = dy.shape
    F = wd.shape[0]
    tm = _tile(T, tm)

    def kern(dy_ref, w_ref, g_ref, up_ref, o_ref):
        da = 0.5 * _dot_nt(dy_ref[...].astype(BF16), w_ref[...])
        g = g_ref[...].astype(F32)
        up = up_ref[...].astype(F32)
        s = _sigmoid(g)
        o_ref[:, :F] = (da * up * (s * (1.0 + g * (1.0 - s)))).astype(BF16)
        o_ref[:, F:] = (da * (g * s)).astype(BF16)

    body, dep_spec, dep_arg = _with_dep(kern, 4, dep)
    return pl.pallas_call(
        body, name=name, grid=(T // tm,),
        in_specs=[pl.BlockSpec((tm, N), lambda i: (i, 0)),
                  pl.BlockSpec((F, N), lambda i: (0, 0)),
                  pl.BlockSpec((tm, F), lambda i: (i, 0)),
                  pl.BlockSpec((tm, F), lambda i: (i, 1))] + dep_spec,
        out_specs=pl.BlockSpec((tm, 2 * F), lambda i: (i, 0)),
        out_shape=jax.ShapeDtypeStruct((T, 2 * F), BF16),
        compiler_params=_params(1),
    )(dy, wd, gu, gu, *dep_arg)


def ffn_forward(h, g, w_gu, w_down, name, dep=None):
    gu, u = norm_matmul(h, g, w_gu, BF16, name + "_gu", tn=D_FF, dep=dep)
    h_out, a = ffn_down(gu, w_down, h, name + "_down")
    return h_out, (h, u, gu, a)


def ffn_backward(dy, saved, g, w_gu, w_down, name, dep=None, emit=None):
    h, u, gu, a = saved
    dgu = ffn_down_bwd(dy, w_down, gu, name + "_ddown", dep=dep)
    d_w_down = matmul_tn(a, dy, name + "_dwd", scale=0.5, tk=D_FF // 2)
    d_w_gu = matmul_tn(dgu, u, name + "_dwgu", tk=D_FF)
    dh, dg = matmul_norm_bwd(dgu, w_gu, h, g, dy, name + "_dx", dep=emit(d_w_gu, d_w_down))
    return dh, dg


def conformer_conv_fwd(z, cw, cb, lg, lb, name, tm=512):
    T = z.shape[0]
    C = CONV_A_CH
    tm = _tile(T, tm)
    hb = tm // CONV_HALO

    def body(v_ref, gt_ref, pv_ref, pg_ref, cw_ref, cb_ref, lg_ref, lb_ref, o_ref, xs_ref):
        i = pl.program_id(0)
        prev = pv_ref[...] * _sigmoid(pg_ref[...])
        xs_ref[0:CONV_HALO, :] = jnp.where(i > 0, prev, 0.0)
        xs_ref[CONV_HALO:, :] = v_ref[...] * _sigmoid(gt_ref[...])
        acc = jnp.zeros((tm, C), F32) + cb_ref[...]
        for k in range(CONV_A_WIDTH):
            acc = acc + cw_ref[k:k + 1, :] * xs_ref[pl.ds(CONV_HALO - (CONV_A_WIDTH - 1) + k, tm), :]
        mu = jnp.mean(acc, axis=-1, keepdims=True)
        xc = acc - mu
        var = jnp.mean(xc * xc, axis=-1, keepdims=True)
        y = (xc * lax.rsqrt(var + LN_EPS)) * lg_ref[...] + lb_ref[...]
        o_ref[...] = (y * _sigmoid(y)).astype(BF16)

    return pl.pallas_call(
        body, name=name, grid=(T // tm,),
        in_specs=[pl.BlockSpec((tm, C), lambda i: (i, 0)),
                  pl.BlockSpec((tm, C), lambda i: (i, 1)),
                  pl.BlockSpec((CONV_HALO, C), lambda i: (jnp.maximum(i * hb - 1, 0), 0)),
                  pl.BlockSpec((CONV_HALO, C), lambda i: (jnp.maximum(i * hb - 1, 0), 1)),
                  pl.BlockSpec((32, C), lambda i: (0, 0)),
                  pl.BlockSpec((1, C), lambda i: (0, 0)),
                  pl.BlockSpec((1, C), lambda i: (0, 0)),
                  pl.BlockSpec((1, C), lambda i: (0, 0))],
        out_specs=pl.BlockSpec((tm, C), lambda i: (i, 0)),
        out_shape=jax.ShapeDtypeStruct((T, C), BF16),
        scratch_shapes=[pltpu.VMEM((tm + CONV_HALO, C), F32)],
        compiler_params=_params(1),
    )(z, z, z, z, cw, cb, lg, lb)


def conformer_conv_bwd(z, dm, cw, cb, lg, lb, name, tm=512):
    T = z.shape[0]
    C = CONV_A_CH
    tm = _tile(T, tm)
    hb = tm // CONV_HALO
    n_tiles = T // tm
    last_halo = T // CONV_HALO - 1
    R = tm + CONV_HALO
    KW = CONV_A_WIDTH

    def body(v_ref, gt_ref, pv_ref, pg_ref, nv_ref, ng_ref, do_ref, ndo_ref, cw_ref, cb_ref, lg_ref, lb_ref,
             dz_ref, dcw_ref, dcb_ref, dlg_ref, dlb_ref, xs_ref, ds_ref):
        i = pl.program_id(0)

        @pl.when(i == 0)
        def _():
            dcw_ref[...] = jnp.zeros_like(dcw_ref)
            dcb_ref[...] = jnp.zeros_like(dcb_ref)
            dlg_ref[...] = jnp.zeros_like(dlg_ref)
            dlb_ref[...] = jnp.zeros_like(dlb_ref)

        val = v_ref[...]
        sg = _sigmoid(gt_ref[...])
        prev = pv_ref[...] * _sigmoid(pg_ref[...])
        xs_ref[0:CONV_HALO, :] = jnp.where(i > 0, prev, 0.0)
        xs_ref[CONV_HALO:CONV_HALO + tm, :] = val * sg
        xs_ref[CONV_HALO + tm:, :] = nv_ref[...] * _sigmoid(ng_ref[...])

        acc = jnp.zeros((R, C), F32) + cb_ref[...]
        for k in range(KW):
            acc = acc + cw_ref[k:k + 1, :] * xs_ref[pl.ds(CONV_HALO - (KW - 1) + k, R), :]
        mu = jnp.mean(acc, axis=-1, keepdims=True)
        xc = acc - mu
        rstd = lax.rsqrt(jnp.mean(xc * xc, axis=-1, keepdims=True) + LN_EPS)
        xh = xc * rstd
        y = xh * lg_ref[...] + lb_ref[...]
        s = _sigmoid(y)
        dout = jnp.concatenate([do_ref[...], jnp.where(i < n_tiles - 1, ndo_ref[...], 0.0)], axis=0)
        dy = dout * (s * (1.0 + y * (1.0 - s)))
        dxh = dy * lg_ref[...]
        dconv = rstd * (dxh - jnp.mean(dxh, axis=-1, keepdims=True) - xh * jnp.mean(dxh * xh, axis=-1, keepdims=True))
        ds_ref[...] = dconv

        dy_m = dy[:tm]
        dlg_ref[...] += jnp.sum(dy_m * xh[:tm], axis=0, keepdims=True)
        dlb_ref[...] += jnp.sum(dy_m, axis=0, keepdims=True)
        dc_m = dconv[:tm]
        dcb_ref[...] += jnp.sum(dc_m, axis=0, keepdims=True)
        dglu = jnp.zeros((tm, C), F32)
        for k in range(KW):
            dcw_ref[k:k + 1, :] += jnp.sum(dc_m * xs_ref[pl.ds(CONV_HALO - (KW - 1) + k, tm), :], axis=0, keepdims=True)
            dglu = dglu + cw_ref[k:k + 1, :] * ds_ref[pl.ds(KW - 1 - k, tm), :]
        dz_ref[:, :C] = (dglu * sg).astype(BF16)
        dz_ref[:, C:] = (dglu * val * sg * (1.0 - sg)).astype(BF16)

    prev_map = lambda i: jnp.maximum(i * hb - 1, 0)
    next_map = lambda i: jnp.minimum((i + 1) * hb, last_halo)
    return pl.pallas_call(
        body, name=name, grid=(n_tiles,),
        in_specs=[pl.BlockSpec((tm, C), lambda i: (i, 0)),
                  pl.BlockSpec((tm, C), lambda i: (i, 1)),
                  pl.BlockSpec((CONV_HALO, C), lambda i: (prev_map(i), 0)),
                  pl.BlockSpec((CONV_HALO, C), lambda i: (prev_map(i), 1)),
                  pl.BlockSpec((CONV_HALO, C), lambda i: (next_map(i), 0)),
                  pl.BlockSpec((CONV_HALO, C), lambda i: (next_map(i), 1)),
                  pl.BlockSpec((tm, C), lambda i: (i, 0)),
                  pl.BlockSpec((CONV_HALO, C), lambda i: (next_map(i), 0)),
                  pl.BlockSpec((32, C), lambda i: (0, 0)),
                  pl.BlockSpec((1, C), lambda i: (0, 0)),
                  pl.BlockSpec((1, C), lambda i: (0, 0)),
                  pl.BlockSpec((1, C), lambda i: (0, 0))],
        out_specs=[pl.BlockSpec((tm, 2 * C), lambda i: (i, 0)),
                   pl.BlockSpec((32, C), lambda i: (0, 0)),
                   pl.BlockSpec((1, C), lambda i: (0, 0)),
                   pl.BlockSpec((1, C), lambda i: (0, 0)),
                   pl.BlockSpec((1, C), lambda i: (0, 0))],
        out_shape=[jax.ShapeDtypeStruct((T, 2 * C), BF16),
                   jax.ShapeDtypeStruct((32, C), F32),
                   jax.ShapeDtypeStruct((1, C), F32),
                   jax.ShapeDtypeStruct((1, C), F32),
                   jax.ShapeDtypeStruct((1, C), F32)],
        scratch_shapes=[pltpu.VMEM((tm + 2 * CONV_HALO, C), F32), pltpu.VMEM((R, C), F32)],
        compiler_params=_params(1),
    )(z, z, z, z, z, z, dm, dm, cw, cb, lg, lb)


def _swa_scores(q_h, kk_h, slope, bias_dist, valid, sink):
    s = _dot_nt(q_h, kk_h) * (HEAD_DIM ** -0.5) - slope * bias_dist
    s = jnp.where(valid, s, NEG_BIG)
    m = jnp.maximum(jnp.max(s, axis=-1, keepdims=True), sink)
    p = jnp.exp(s - m)
    e_sink = jnp.exp(sink - m)
    inv = 1.0 / (jnp.sum(p, axis=-1, keepdims=True) + e_sink)
    return p * inv, e_sink * inv


def _swa_mask(r0):
    qi = lax.broadcasted_iota(jnp.int32, (WINDOW, 2 * WINDOW), 0)
    kj = lax.broadcasted_iota(jnp.int32, (WINDOW, 2 * WINDOW), 1)
    dist = qi + WINDOW - kj
    valid = (dist >= 0) & (dist < WINDOW) & (r0 - WINDOW + kj >= 0)
    return dist.astype(F32), valid


def swa_fwd(z, kpad, vpad, sinks, name, tq=512):
    T = z.shape[0]
    tq = _tile(T, tq)
    HQ = SWA_HEADS * HEAD_DIM

    def body(sink_ref, q_ref, k_ref, v_ref, o_ref):
        i = pl.program_id(0)
        for sub in range(tq // WINDOW):
            r0 = pl.multiple_of(i * tq + sub * WINDOW, WINDOW)
            kk = k_ref[pl.ds(r0, 2 * WINDOW), :]
            vv = v_ref[pl.ds(r0, 2 * WINDOW), :]
            qb = q_ref[sub * WINDOW:(sub + 1) * WINDOW, :].astype(BF16)
            dist, valid = _swa_mask(r0)
            outs = []
            for h in range(SWA_HEADS):
                kh = h // SWA_GROUP
                ks = slice(kh * HEAD_DIM, (kh + 1) * HEAD_DIM)
                pn, _ = _swa_scores(qb[:, h * HEAD_DIM:(h + 1) * HEAD_DIM], kk[:, ks], 2.0 ** (-(h + 1)), dist, valid,
                                    sink_ref[h])
                outs.append(_dot(pn.astype(BF16), vv[:, ks]))
            o_ref[sub * WINDOW:(sub + 1) * WINDOW, :] = jnp.concatenate(outs, axis=-1).astype(BF16)

    return pl.pallas_call(
        body, name=name, grid=(T // tq,),
        in_specs=[pl.BlockSpec(memory_space=pltpu.SMEM),
                  pl.BlockSpec((tq, HQ), lambda i: (i, 2)),
                  pl.BlockSpec((T + WINDOW, 2 * HEAD_DIM), lambda i: (0, 0)),
                  pl.BlockSpec((T + WINDOW, 2 * HEAD_DIM), lambda i: (0, 0))],
        out_specs=pl.BlockSpec((tq, HQ), lambda i: (i, 0)),
        out_shape=jax.ShapeDtypeStruct((T, HQ), BF16),
        compiler_params=_params(1),
    )(sinks, z, kpad, vpad)


def swa_bwd(z, kpad, vpad, sinks, dm, name, tq=512):
    T = z.shape[0]
    tq = _tile(T, tq)
    HQ = SWA_HEADS * HEAD_DIM
    scale = HEAD_DIM ** -0.5

    def body(sink_ref, q_ref, k_ref, v_ref, do_ref, dq_ref, dk_ref, dv_ref, dsink_ref):
        i = pl.program_id(0)

        @pl.when(i == 0)
        def _():
            dk_ref[...] = jnp.zeros_like(dk_ref)
            dv_ref[...] = jnp.zeros_like(dv_ref)
            dsink_ref[...] = jnp.zeros_like(dsink_ref)

        for sub in range(tq // WINDOW):
            r0 = pl.multiple_of(i * tq + sub * WINDOW, WINDOW)
            kk = k_ref[pl.ds(r0, 2 * WINDOW), :]
            vv = v_ref[pl.ds(r0, 2 * WINDOW), :]
            rows = slice(sub * WINDOW, (sub + 1) * WINDOW)
            qb = q_ref[rows, :].astype(BF16)
            dob = do_ref[rows, :].astype(BF16)
            dist, valid = _swa_mask(r0)
            dqs, dks, dvs = [], [], []
            for kh in range(SWA_KV_HEADS):
                ks = slice(kh * HEAD_DIM, (kh + 1) * HEAD_DIM)
                dk_acc = jnp.zeros((2 * WINDOW, HEAD_DIM), F32)
                dv_acc = jnp.zeros((2 * WINDOW, HEAD_DIM), F32)
                for g in range(SWA_GROUP):
                    h = kh * SWA_GROUP + g
                    hs = slice(h * HEAD_DIM, (h + 1) * HEAD_DIM)
                    pn, p_sink = _swa_scores(qb[:, hs], kk[:, ks], 2.0 ** (-(h + 1)), dist, valid, sink_ref[h])
                    dp = _dot_nt(dob[:, hs], vv[:, ks])
                    delta = jnp.sum(pn * dp, axis=-1, keepdims=True)
                    ds = (pn * (dp - delta)).astype(BF16)
                    dqs.append(_dot(ds, kk[:, ks]) * scale)
                    dk_acc = dk_acc + _dot_tn(ds, qb[:, hs]) * scale
                    dv_acc = dv_acc + _dot_tn(pn.astype(BF16), dob[:, hs])
                    dsink_ref[h:h + 1, :] += jnp.zeros((1, 128), F32) - jnp.sum(p_sink * delta)
                dks.append(dk_acc)
                dvs.append(dv_acc)
            dq_ref[rows, :] = jnp.concatenate(dqs, axis=-1).astype(BF16)
            dk_ref[pl.ds(r0, 2 * WINDOW), :] += jnp.concatenate(dks, axis=-1)
            dv_ref[pl.ds(r0, 2 * WINDOW), :] += jnp.concatenate(dvs, axis=-1)

    kv_spec = pl.BlockSpec((T + WINDOW, 2 * HEAD_DIM), lambda i: (0, 0))
    return pl.pallas_call(
        body, name=name, grid=(T // tq,),
        in_specs=[pl.BlockSpec(memory_space=pltpu.SMEM),
                  pl.BlockSpec((tq, HQ), lambda i: (i, 2)),
                  kv_spec, kv_spec,
                  pl.BlockSpec((tq, HQ), lambda i: (i, 1))],
        out_specs=[pl.BlockSpec((tq, HQ), lambda i: (i, 0)),
                   kv_spec, kv_spec,
                   pl.BlockSpec((SWA_HEADS, 128), lambda i: (0, 0))],
        out_shape=[jax.ShapeDtypeStruct((T, HQ), BF16),
                   jax.ShapeDtypeStruct((T + WINDOW, 2 * HEAD_DIM), F32),
                   jax.ShapeDtypeStruct((T + WINDOW, 2 * HEAD_DIM), F32),
                   jax.ShapeDtypeStruct((SWA_HEADS, 128), F32)],
        compiler_params=_params(1),
    )(sinks, z, kpad, vpad, dm)


def short_conv_fwd(z, w, name, tm=512):
    T = z.shape[0]
    C = SC_CH
    tm = _tile(T, tm)
    hb = tm // SC_HALO

    def body(b_ref, c_ref, v_ref, pc_ref, pv_ref, w_ref, o_ref, xs_ref):
        i = pl.program_id(0)
        xs_ref[0:SC_HALO, :] = jnp.where(i > 0, pc_ref[...] * pv_ref[...], 0.0)
        xs_ref[SC_HALO:, :] = c_ref[...] * v_ref[...]
        conv = jnp.zeros((tm, C), F32)
        for k in range(3):
            conv = conv + w_ref[k:k + 1, :] * xs_ref[pl.ds(SC_HALO - 2 + k, tm), :]
        o_ref[...] = (b_ref[...] * conv).astype(BF16)

    prev_map = lambda i: jnp.maximum(i * hb - 1, 0)
    return pl.pallas_call(
        body, name=name, grid=(T // tm,),
        in_specs=[pl.BlockSpec((tm, C), lambda i: (i, 0)),
                  pl.BlockSpec((tm, C), lambda i: (i, 1)),
                  pl.BlockSpec((tm, C), lambda i: (i, 2)),
                  pl.BlockSpec((SC_HALO, C), lambda i: (prev_map(i), 1)),
                  pl.BlockSpec((SC_HALO, C), lambda i: (prev_map(i), 2)),
                  pl.BlockSpec((8, C), lambda i: (0, 0))],
        out_specs=pl.BlockSpec((tm, C), lambda i: (i, 0)),
        out_shape=jax.ShapeDtypeStruct((T, C), BF16),
        scratch_shapes=[pltpu.VMEM((tm + SC_HALO, C), F32)],
        compiler_params=_params(1),
    )(z, z, z, z, z, w)


def short_conv_bwd(z, dm, w, name, tm=512):
    T = z.shape[0]
    C = SC_CH
    tm = _tile(T, tm)
    hb = tm // SC_HALO
    n_tiles = T // tm
    last_halo = T // SC_HALO - 1
    R = tm + SC_HALO

    def body(b_ref, c_ref, v_ref, pc_ref, pv_ref, nb_ref, do_ref, ndo_ref, w_ref, dz_ref, dw_ref, xs_ref, ds_ref):
        i = pl.program_id(0)

        @pl.when(i == 0)
        def _():
            dw_ref[...] = jnp.zeros_like(dw_ref)

        c = c_ref[...]
        val = v_ref[...]
        dout = do_ref[...]
        xs_ref[0:SC_HALO, :] = jnp.where(i > 0, pc_ref[...] * pv_ref[...], 0.0)
        xs_ref[SC_HALO:, :] = c * val
        dconv = dout * b_ref[...]
        ds_ref[0:tm, :] = dconv
        ds_ref[tm:, :] = jnp.where(i < n_tiles - 1, ndo_ref[...] * nb_ref[...], 0.0)
        conv = jnp.zeros((tm, C), F32)
        dcv = jnp.zeros((tm, C), F32)
        for k in range(3):
            xk = xs_ref[pl.ds(SC_HALO - 2 + k, tm), :]
            conv = conv + w_ref[k:k + 1, :] * xk
            dw_ref[k:k + 1, :] += jnp.sum(dconv * xk, axis=0, keepdims=True)
            dcv = dcv + w_ref[k:k + 1, :] * ds_ref[pl.ds(2 - k, tm), :]
        dz_ref[:, 0:C] = (dout * conv).astype(BF16)
        dz_ref[:, C:2 * C] = (dcv * val).astype(BF16)
        dz_ref[:, 2 * C:] = (dcv * c).astype(BF16)

    prev_map = lambda i: jnp.maximum(i * hb - 1, 0)
    next_map = lambda i: jnp.minimum((i + 1) * hb, last_halo)
    return pl.pallas_call(
        body, name=name, grid=(n_tiles,),
        in_specs=[pl.BlockSpec((tm, C), lambda i: (i, 0)),
                  pl.BlockSpec((tm, C), lambda i: (i, 1)),
                  pl.BlockSpec((tm, C), lambda i: (i, 2)),
                  pl.BlockSpec((SC_HALO, C), lambda i: (prev_map(i), 1)),
                  pl.BlockSpec((SC_HALO, C), lambda i: (prev_map(i), 2)),
                  pl.BlockSpec((SC_HALO, C), lambda i: (next_map(i), 0)),
                  pl.BlockSpec((tm, C), lambda i: (i, 0)),
                  pl.BlockSpec((SC_HALO, C), lambda i: (next_map(i), 0)),
                  pl.BlockSpec((8, C), lambda i: (0, 0))],
        out_specs=[pl.BlockSpec((tm, 3 * C), lambda i: (i, 0)),
                   pl.BlockSpec((8, C), lambda i: (0, 0))],
        out_shape=[jax.ShapeDtypeStruct((T, 3 * C), BF16), jax.ShapeDtypeStruct((8, C), F32)],
        scratch_shapes=[pltpu.VMEM((tm + SC_HALO, C), F32), pltpu.VMEM((R, C), F32)],
        compiler_params=_params(1),
    )(z, z, z, z, z, z, dm, dm, w)


def _xa_probs(q_h, k_h):
    s = _dot_nt(q_h, k_h) * (XA_HEAD_DIM ** -0.5)
    p = jnp.exp(s - jnp.max(s, axis=-1, keepdims=True))
    return p * (1.0 / jnp.sum(p, axis=-1, keepdims=True))


def xattn_fwd(q, kv, name, tm=512):
    T = q.shape[0]
    M = kv.shape[0]
    tm = _tile(T, tm)

    def body(q_ref, k_ref, v_ref, o_ref):
        for h in range(XA_HEADS):
            hs = slice(h * XA_HEAD_DIM, (h + 1) * XA_HEAD_DIM)
            p = _xa_probs(q_ref[:, hs], k_ref[:, hs])
            o_ref[:, hs] = _dot(p.astype(BF16), v_ref[:, hs]).astype(BF16)

    return pl.pallas_call(
        body, name=name, grid=(T // tm,),
        in_specs=[pl.BlockSpec((tm, D_MODEL), lambda i: (i, 0)),
                  pl.BlockSpec((M, D_MODEL), lambda i: (0, 0)),
                  pl.BlockSpec((M, D_MODEL), lambda i: (0, 1))],
        out_specs=pl.BlockSpec((tm, D_MODEL), lambda i: (i, 0)),
        out_shape=jax.ShapeDtypeStruct((T, D_MODEL), BF16),
        compiler_params=_params(1),
    )(q, kv, kv)


def xattn_bwd(q, kv, do, name, tm=512):
    T = q.shape[0]
    M = kv.shape[0]
    tm = _tile(T, tm)
    scale = XA_HEAD_DIM ** -0.5

    def body(q_ref, k_ref, v_ref, do_ref, dq_ref, dkv_ref):
        @pl.when(pl.program_id(0) == 0)
        def _():
            dkv_ref[...] = jnp.zeros_like(dkv_ref)

        for h in range(XA_HEADS):
            hs = slice(h * XA_HEAD_DIM, (h + 1) * XA_HEAD_DIM)
            vs = slice(D_MODEL + h * XA_HEAD_DIM, D_MODEL + (h + 1) * XA_HEAD_DIM)
            q_h = q_ref[:, hs]
            do_h = do_ref[:, hs]
            p = _xa_probs(q_h, k_ref[:, hs])
            dp = _dot_nt(do_h, v_ref[:, hs])
            ds = (p * (dp - jnp.sum(p * dp, axis=-1, keepdims=True))).astype(BF16)
            dq_ref[:, hs] = (_dot(ds, k_ref[:, hs]) * scale).astype(BF16)
            dkv_ref[:, hs] += _dot_tn(ds, q_h) * scale
            dkv_ref[:, vs] += _dot_tn(p.astype(BF16), do_h)

    return pl.pallas_call(
        body, name=name, grid=(T // tm,),
        in_specs=[pl.BlockSpec((tm, D_MODEL), lambda i: (i, 0)),
                  pl.BlockSpec((M, D_MODEL), lambda i: (0, 0)),
                  pl.BlockSpec((M, D_MODEL), lambda i: (0, 1)),
                  pl.BlockSpec((tm, D_MODEL), lambda i: (i, 0))],
        out_specs=[pl.BlockSpec((tm, D_MODEL), lambda i: (i, 0)),
                   pl.BlockSpec((M, 2 * D_MODEL), lambda i: (0, 0))],
        out_shape=[jax.ShapeDtypeStruct((T, D_MODEL), BF16), jax.ShapeDtypeStruct((M, 2 * D_MODEL), F32)],
        compiler_params=_params(1),
    )(q, kv, kv, do)


def final_loss(h, g, target, name, tm=512):
    T, K = h.shape
    tm = _tile(T, tm)

    def body(h_ref, g_ref, t_ref, dh_ref, dg_ref, loss_ref):
        @pl.when(pl.program_id(0) == 0)
        def _():
            dg_ref[...] = jnp.zeros_like(dg_ref)
            loss_ref[...] = jnp.zeros_like(loss_ref)

        x = h_ref[...]
        r = lax.rsqrt(jnp.mean(x * x, axis=-1, keepdims=True) + RMS_EPS)
        xh = x * r
        e = xh * g_ref[...] - t_ref[...]
        loss_ref[...] += jnp.zeros((1, 128), F32) + 0.5 * jnp.sum(jnp.mean(e * e, axis=-1, keepdims=True))
        dy = e * (1.0 / K)
        dg_ref[...] += jnp.sum(dy * xh, axis=0, keepdims=True)
        dxh = dy * g_ref[...]
        dh_ref[...] = r * (dxh - xh * jnp.mean(dxh * xh, axis=-1, keepdims=True))

    return pl.pallas_call(
        body, name=name, grid=(T // tm,),
        in_specs=[pl.BlockSpec((tm, K), lambda i: (i, 0)),
                  pl.BlockSpec((1, K), lambda i: (0, 0)),
                  pl.BlockSpec((tm, K), lambda i: (i, 0))],
        out_specs=[pl.BlockSpec((tm, K), lambda i: (i, 0)),
                   pl.BlockSpec((1, K), lambda i: (0, 0)),
                   pl.BlockSpec((1, 128), lambda i: (0, 0))],
        out_shape=[jax.ShapeDtypeStruct((T, K), F32), jax.ShapeDtypeStruct((1, K), F32),
                   jax.ShapeDtypeStruct((1, 128), F32)],
        compiler_params=_params(1),
    )(h, g, target)


def _row(v):
    return v.reshape(1, -1)


def _pad_rows(a, rows):
    return jnp.pad(a, ((0, rows - a.shape[0]), (0, 0)))


def local_step(x, mem, target, P, get_weights, put_grads):
    cw = _pad_rows(P["conv_a_w"], 32)
    scw = _pad_rows(P["sc_conv_w"], 8)
    cb, lg, lb = _row(P["conv_a_b"]), _row(P["conv_a_ln_g"]), _row(P["conv_a_ln_b"])
    sinks = P["swa_sinks"]

    class _Layered:
        def __init__(self, store, name=None):
            self.store, self.name = store, name

        def __getitem__(self, key):
            if self.name is None:
                return self.store[(key, 0)] if key in ("even_w_in", "even_w_out", "odd_w_in", "odd_w_out") \
                    else _Layered(self.store, key)
            return self.store[(self.name, key)]

    store = {}
    W = _Layered(store)
    saved = []
    h = x
    for i in range(2):
        L = f"l{i}"
        new, dep = get_weights("A" if i == 0 else "D", h)
        store.update(new)
        h, s_ffn1 = ffn_forward(h, P["ffn1_norm"][i:i + 1], W["ffn1_w_gu"][i], W["ffn1_w_down"][i], L + "_ffn1", dep=dep)
        h1 = h
        if i == 0:
            new, _ = get_weights("B", h)
            store.update(new)
            z, u2 = norm_matmul(h1, P["mix_norm"][i:i + 1], W["even_w_in"], F32, L + "_mix_in")
            a = conformer_conv_fwd(z, cw, cb, lg, lb, L + "_conv")
            kpad = jnp.pad(z[:, 1536:1664].astype(BF16), ((WINDOW, 0), (0, 0)))
            vpad = jnp.pad(z[:, 1664:1792].astype(BF16), ((WINDOW, 0), (0, 0)))
            o = swa_fwd(z, kpad, vpad, sinks, L + "_swa")
            m = jnp.concatenate([a, o], axis=-1)
            h = matmul_residual(m, W["even_w_out"], h1, L + "_mix_out")
            s_mix = (h1, u2, z, m, kpad, vpad)
        else:
            z, u2 = norm_matmul(h1, P["mix_norm"][i:i + 1], W["odd_w_in"], F32, L + "_mix_in")
            m = short_conv_fwd(z, scw, L + "_sconv")
            h = matmul_residual(m, W["odd_w_out"], h1, L + "_mix_out")
            s_mix = (h1, u2, z, m)
        h2 = h
        kv, umem = norm_matmul(mem, P["xa_mem_norm"][i:i + 1], W["xa_wkv"][i], BF16, L + "_xa_kv", tm=256)
        q, u3 = norm_matmul(h2, P["xa_norm"][i:i + 1], W["xa_wq"][i], BF16, L + "_xa_q", transposed=False)
        o = xattn_fwd(q, kv, L + "_xa")
        h = matmul_residual(o, W["xa_wo"][i], h2, L + "_xa_out")
        s_xa = (h2, u3, q, o, kv, umem)
        if i == 0:
            new, _ = get_weights("C", h)
            store.update(new)
        h, s_ffn2 = ffn_forward(h, P["ffn2_norm"][i:i + 1], W["ffn2_w_gu"][i], W["ffn2_w_down"][i], L + "_ffn2")
        saved.append((s_ffn1, s_mix, s_xa, s_ffn2))

    dh, d_final, loss = final_loss(h, _row(P["final_norm"]), target, "final_loss")

    names = ("ffn1_w_gu", "ffn1_w_down", "ffn2_w_gu", "ffn2_w_down", "xa_wq", "xa_wkv", "xa_wo", "even_w_in", "even_w_out",
             "odd_w_in", "odd_w_out")
    dW = {k: [None, None] for k in names}
    dP = {k: [None, None] for k in ("ffn1_norm", "mix_norm", "xa_norm", "xa_mem_norm", "ffn2_norm")}
    dP["final_norm"] = d_final.reshape(-1)
    for i in (1, 0):
        L = f"l{i}b"
        s_ffn1, s_mix, s_xa, s_ffn2 = saved[i]

        def keep_ffn2(d_w_gu, d_w_down, i=i):
            dW["ffn2_w_gu"][i], dW["ffn2_w_down"][i] = d_w_gu, d_w_down

        def send_ffn1(d_w_gu, d_w_down, i=i):
            dW["ffn1_w_gu"][i], dW["ffn1_w_down"][i] = d_w_gu, d_w_down
            stage = "D" if i == 1 else "A"
            return put_grads(stage, {k: dW[k[0]][k[1]] for k in STAGE_KEYS[stage]})

        dh, dP["ffn2_norm"][i] = ffn_backward(
            dh, s_ffn2, P["ffn2_norm"][i:i + 1], W["ffn2_w_gu"][i], W["ffn2_w_down"][i], L + "_ffn2", emit=keep_ffn2)
        h2, u3, q, o, kv, umem = s_xa
        dW["xa_wo"][i] = matmul_tn(o, dh, L + "_xa_dwo")
        do = matmul_nt(dh, W["xa_wo"][i], BF16, L + "_xa_do")
        dq, dkv = xattn_bwd(q, kv, do, L + "_xa")
        dW["xa_wq"][i] = matmul_tn(u3, dq, L + "_xa_dwq")
        dW["xa_wkv"][i] = matmul_tn(dkv, umem, L + "_xa_dwkv", tk=1024)
        dkv_b = dkv.astype(BF16)
        _, dP["xa_mem_norm"][i] = matmul_norm_bwd(dkv_b, W["xa_wkv"][i], mem, P["xa_mem_norm"][i:i + 1],
                                                  jnp.zeros_like(mem), L + "_xa_dmem")
        dh, dP["xa_norm"][i] = matmul_norm_bwd(dq, W["xa_wq"][i], h2, P["xa_norm"][i:i + 1], dh, L + "_xa_dx",
                                               transposed=False)
        if i == 0:
            h1, u2, z, m, kpad, vpad = s_mix
            dW["even_w_out"][0] = matmul_tn(m, dh, L + "_mix_dwo")
            dm = matmul_nt(dh, W["even_w_out"], F32, L + "_mix_dm")
            dz_conv, dcw, dcb, dlg, dlb = conformer_conv_bwd(z, dm, cw, cb, lg, lb, L + "_conv")
            dq_s, dkp, dvp, dsk = swa_bwd(z, kpad, vpad, sinks, dm, L + "_swa")
            dz = jnp.concatenate([dz_conv, dq_s, dkp[WINDOW:].astype(BF16), dvp[WINDOW:].astype(BF16)], axis=-1)
            dW["even_w_in"][0] = matmul_tn(dz, u2, L + "_mix_dwi", tk=896)
            dh, dP["mix_norm"][i] = matmul_norm_bwd(dz, W["even_w_in"], h1, P["mix_norm"][i:i + 1], dh, L + "_mix_dx")
            dP["conv_a_w"] = dcw[:CONV_A_WIDTH]
            dP["conv_a_b"], dP["conv_a_ln_g"], dP["conv_a_ln_b"] = dcb.reshape(-1), dlg.reshape(-1), dlb.reshape(-1)
            dP["swa_sinks"] = dsk[:, 0]
        else:
            h1, u2, z, m = s_mix
            dW["odd_w_out"][0] = matmul_tn(m, dh, L + "_mix_dwo")
            dm = matmul_nt(dh, W["odd_w_out"], F32, L + "_mix_dm")
            dz, dscw = short_conv_bwd(z, dm, scw, L + "_sconv")
            dW["odd_w_in"][0] = matmul_tn(dz, u2, L + "_mix_dwi", tk=1024)
            dh, dP["mix_norm"][i] = matmul_norm_bwd(dz, W["odd_w_in"], h1, P["mix_norm"][i:i + 1], dh, L + "_mix_dx")
            dP["sc_conv_w"] = dscw[:3]
        dep = put_grads("BC", {k: dW[k[0]][k[1]] for k in STAGE_KEYS["B"] + STAGE_KEYS["C"]}) if i == 0 else None
        dh, dP["ffn1_norm"][i] = ffn_backward(
            dh, s_ffn1, P["ffn1_norm"][i:i + 1], W["ffn1_w_gu"][i], W["ffn1_w_down"][i], L + "_ffn1", dep=dep,
            emit=send_ffn1)
    for k in ("ffn1_norm", "mix_norm", "xa_norm", "xa_mem_norm", "ffn2_norm"):
        dP[k] = jnp.concatenate(dP[k], axis=0)
    return loss, dh, dP


def _mesh_pos():
    return lax.axis_index("x"), lax.axis_index("y"), lax.axis_index("c")


def _flat_index(px, py, pc):
    return 4 * px + 2 * py + pc


def all_gather(blob, name):
    R, C = blob.shape

    def body(x_ref, out_ref, send_sems, recv_sems, local_sem):
        x, y, c = _mesh_pos()
        me, sibling = (x, y, c), (x, y, 1 - c)
        chips = [(1 - x, y), (x, 1 - y), (1 - x, 1 - y)]

        def slot(px, py, pc):
            return out_ref.at[_flat_index(px, py, pc)]

        def copy(k, block, to, src=None):
            return pltpu.make_async_remote_copy(
                src_ref=slot(*block) if src is None else src, dst_ref=slot(*block),
                send_sem=send_sems.at[k], recv_sem=recv_sems.at[k],
                device_id=to, device_id_type=pl.DeviceIdType.MESH)

        mine = pltpu.make_async_copy(x_ref, slot(*me), local_sem)
        mine.start()
        first = [copy(0, me, sibling, src=x_ref)]
        first += [copy(1 + j, me, (*chip, c), src=x_ref) for j, chip in enumerate(chips)]
        for cp in first:
            cp.start()
        passed = [copy(4 + j, (*chip, c), sibling) for j, chip in enumerate(chips)]
        for j, chip in enumerate(chips):
            copy(1 + j, (*chip, c), me).wait_recv()
            passed[j].start()
        copy(0, sibling, me).wait_recv()
        for j, chip in enumerate(chips):
            copy(4 + j, (*chip, 1 - c), me).wait_recv()
        for cp in first + passed:
            cp.wait_send()
        mine.wait()

    return pl.pallas_call(
        body, name=name,
        out_shape=jax.ShapeDtypeStruct((N_DEV, R, C), blob.dtype),
        in_specs=[pl.BlockSpec(memory_space=pl.ANY)],
        out_specs=pl.BlockSpec(memory_space=pl.ANY),
        scratch_shapes=[pltpu.SemaphoreType.DMA((7,)), pltpu.SemaphoreType.DMA((7,)), pltpu.SemaphoreType.DMA],
    )(blob)


def scatter_exchange(g, name):
    _, R, C = g.shape

    def body(g_ref, out_ref, send_sems, recv_sems, local_sem):
        x, y, c = _mesh_pos()
        me_idx = _flat_index(x, y, c)
        mine = pltpu.make_async_copy(g_ref.at[me_idx], out_ref.at[me_idx], local_sem)
        mine.start()
        sends, peers = [], []
        for k in range(1, N_DEV):
            px = 1 - x if k & 4 else x
            py = 1 - y if k & 2 else y
            pc = 1 - c if k & 1 else c
            peer_idx = _flat_index(px, py, pc)
            cp = pltpu.make_async_remote_copy(
                src_ref=g_ref.at[peer_idx], dst_ref=out_ref.at[me_idx],
                send_sem=send_sems.at[k - 1], recv_sem=recv_sems.at[k - 1],
                device_id=(px, py, pc), device_id_type=pl.DeviceIdType.MESH)
            cp.start()
            sends.append(cp)
            peers.append((peer_idx, (px, py, pc)))
        for k in range(1, N_DEV):
            peer_idx, peer = peers[k - 1]
            pltpu.make_async_remote_copy(
                src_ref=g_ref.at[me_idx], dst_ref=out_ref.at[peer_idx],
                send_sem=send_sems.at[k - 1], recv_sem=recv_sems.at[k - 1],
                device_id=peer, device_id_type=pl.DeviceIdType.MESH).wait_recv()
        for cp in sends:
            cp.wait_send()
        mine.wait()

    return pl.pallas_call(
        body, name=name,
        out_shape=jax.ShapeDtypeStruct(g.shape, g.dtype),
        in_specs=[pl.BlockSpec(memory_space=pl.ANY)],
        out_specs=pl.BlockSpec(memory_space=pl.ANY),
        scratch_shapes=[pltpu.SemaphoreType.DMA((7,)), pltpu.SemaphoreType.DMA((7,)), pltpu.SemaphoreType.DMA],
    )(g)


HBM_SPEC = pl.BlockSpec(memory_space=pltpu.HBM)
SEM_SPEC = pl.BlockSpec(memory_space=pltpu.SEMAPHORE)
DATAFLOW_EFFECT = pltpu.SideEffectType.DATAFLOW_SIDE_EFFECTING


def _peers(x, y, c):
    out = []
    for k in range(1, N_DEV):
        pos = (1 - x if k & 4 else x, 1 - y if k & 2 else y, 1 - c if k & 1 else c)
        out.append((_flat_index(*pos), pos))
    return out


def _exchange_copy(src_ref, land_ref, send_sems, recv_sems, j, me, peer_idx, peer, scatter):
    return pltpu.make_async_remote_copy(
        src_ref=src_ref.at[peer_idx] if scatter else src_ref, dst_ref=land_ref.at[me],
        send_sem=send_sems.at[j], recv_sem=recv_sems.at[j], device_id=peer, device_id_type=pl.DeviceIdType.MESH)


def exchange_start(srcs, lands, scatter, after, name):
    n = len(srcs)

    def body(*refs):
        src_refs, land_refs = refs[:n], refs[n:2 * n]
        outs = refs[2 * n + 1:]
        send_sems, recv_sems, token = outs[:n], outs[n:2 * n], outs[4 * n]
        x, y, c = _mesh_pos()
        me = _flat_index(x, y, c)
        for g in range(n):
            for j, (peer_idx, peer) in enumerate(_peers(x, y, c)):
                _exchange_copy(src_refs[g], land_refs[g], send_sems[g], recv_sems[g], j, me, peer_idx, peer, scatter).start()
        token[...] = jnp.zeros_like(token)

    hbm = lambda a: pltpu.with_memory_space_constraint(a, pltpu.HBM)
    res = pl.pallas_call(
        body, name=name,
        out_shape=(*[pltpu.SemaphoreType.DMA((N_DEV - 1,))] * (2 * n),
                   *[pltpu.HBM(a.shape, a.dtype) for a in srcs], *[pltpu.HBM(a.shape, a.dtype) for a in lands],
                   jax.ShapeDtypeStruct((8, 128), F32)),
        in_specs=[HBM_SPEC] * (2 * n) + [ANY_SPEC],
        out_specs=(*[SEM_SPEC] * (2 * n), *[HBM_SPEC] * (2 * n), pl.BlockSpec(memory_space=pltpu.VMEM)),
        input_output_aliases={i: 2 * n + i for i in range(2 * n)},
        compiler_params=pltpu.CompilerParams(has_side_effects=DATAFLOW_EFFECT),
    )(*[hbm(a) for a in srcs], *[hbm(a) for a in lands], after)
    handles = [(res[g], res[n + g], res[2 * n + g], res[3 * n + g]) for g in range(n)]
    return handles, res[4 * n]


def exchange_wait(handle, scatter, after, name):
    send_sem, recv_sem, src_thru, land_thru = handle

    def body(src_ref, land_ref, send_sems, recv_sems, after_ref, src_dead, got_ref):
        x, y, c = _mesh_pos()
        me = _flat_index(x, y, c)
        for j, (peer_idx, peer) in enumerate(_peers(x, y, c)):
            mine = _exchange_copy(src_ref, land_ref, send_sems, recv_sems, j, me, peer_idx, peer, scatter)
            mine.wait_send()
            theirs = pltpu.make_async_remote_copy(
                src_ref=src_ref.at[me] if scatter else src_ref, dst_ref=land_ref.at[peer_idx],
                send_sem=send_sems.at[j], recv_sem=recv_sems.at[j], device_id=peer, device_id_type=pl.DeviceIdType.MESH)
            theirs.wait_recv()

    return pl.pallas_call(
        body, name=name,
        out_shape=(pltpu.HBM(src_thru.shape, src_thru.dtype), pltpu.HBM(land_thru.shape, land_thru.dtype)),
        in_specs=(HBM_SPEC, HBM_SPEC, SEM_SPEC, SEM_SPEC, ANY_SPEC), out_specs=(HBM_SPEC, HBM_SPEC),
        input_output_aliases={0: 0, 1: 1},
        compiler_params=pltpu.CompilerParams(has_side_effects=DATAFLOW_EFFECT),
    )(src_thru, land_thru, send_sem, recv_sem, after)


def ordered_sum(parts, name, tr=512):
    n, R, C = parts.shape
    tr = next((t for t in range(min(tr, R), 15, -16) if R % t == 0), R)

    def body(p_ref, o_ref):
        acc = p_ref[0].astype(F32)
        for j in range(1, n):
            acc = acc + p_ref[j].astype(F32)
        o_ref[...] = acc

    return pl.pallas_call(
        body, name=name, grid=(R // tr,),
        in_specs=[pl.BlockSpec((n, tr, C), lambda i: (0, i, 0))],
        out_specs=pl.BlockSpec((tr, C), lambda i: (i, 0)),
        out_shape=jax.ShapeDtypeStruct((R, C), F32),
        compiler_params=_params(1),
    )(parts)


def adamw(w, g, m, v, name, tr=256):
    R, C = w.shape
    tr = next((t for t in range(tr, 7, -8) if R % t == 0), R)
    c1 = 1.0 - ADAM_B1 ** ADAM_STEP
    c2 = 1.0 - ADAM_B2 ** ADAM_STEP

    def body(w_ref, g_ref, m_ref, v_ref, d_ref, mo_ref, vo_ref):
        grad = g_ref[...]
        m2 = ADAM_B1 * m_ref[...] + (1.0 - ADAM_B1) * grad
        v2 = ADAM_B2 * v_ref[...] + (1.0 - ADAM_B2) * (grad * grad)
        mo_ref[...] = m2
        vo_ref[...] = v2
        d_ref[...] = -ADAM_LR * ((m2 / c1) / (jnp.sqrt(v2 / c2) + ADAM_EPS) + ADAM_WD * w_ref[...])

    spec = pl.BlockSpec((tr, C), lambda i: (i, 0))
    return pl.pallas_call(
        body, name=name, grid=(R // tr,),
        in_specs=[spec] * 4, out_specs=[spec] * 3,
        out_shape=[jax.ShapeDtypeStruct((R, C), F32)] * 3,
        compiler_params=_params(1),
    )(w, g, m, v)


WEIGHT_NAMES = ("ffn1_norm", "ffn1_w_gu", "ffn1_w_down", "mix_norm", "even_w_in", "conv_a_w", "conv_a_b", "conv_a_ln_g",
                "conv_a_ln_b", "swa_sinks", "even_w_out", "odd_w_in", "sc_conv_w", "odd_w_out", "xa_norm", "xa_mem_norm",
                "xa_wq", "xa_wkv", "xa_wo", "ffn2_norm", "ffn2_w_gu", "ffn2_w_down", "final_norm")
BLOB_COLS = 1024
SMALL_ROWS = (("ffn1_norm", 0, 2), ("mix_norm", 2, 2), ("xa_norm", 4, 2), ("xa_mem_norm", 6, 2), ("ffn2_norm", 8, 2),
              ("final_norm", 10, 1))
ROW_CONV_B_LNG = 11
ROW_LNB_SINKS_LOSS = 12
LOSS_COL = 512 + SWA_HEADS
ROW_SC_CONV = 13
ROW_CONV_W = 16
SMALL_BLOB_ROWS = 32
SMALL_ADAM_ROWS = 16


def _small_blob(v):
    rows = [v[n].reshape(-1, D_MODEL) for n, _, _ in SMALL_ROWS]
    rows.append(jnp.concatenate([v["conv_a_b"].reshape(-1), v["conv_a_ln_g"].reshape(-1)]).reshape(1, D_MODEL))
    tail = jnp.zeros((D_MODEL - 512 - SWA_HEADS,), F32)
    if "loss" in v:
        tail = tail.at[0].set(v["loss"])
    rows.append(jnp.concatenate([v["conv_a_ln_b"].reshape(-1), v["swa_sinks"].reshape(-1), tail]).reshape(1, D_MODEL))
    rows.append(jnp.zeros((SMALL_ADAM_ROWS - ROW_SC_CONV, D_MODEL), F32))
    return jnp.concatenate(rows, axis=0)


def _small_unblob(b, shapes):
    out = {n: b[r:r + k].reshape(shapes[n]) for n, r, k in SMALL_ROWS}
    out["conv_a_b"] = b[ROW_CONV_B_LNG, :512].reshape(shapes["conv_a_b"])
    out["conv_a_ln_g"] = b[ROW_CONV_B_LNG, 512:].reshape(shapes["conv_a_ln_g"])
    out["conv_a_ln_b"] = b[ROW_LNB_SINKS_LOSS, :512].reshape(shapes["conv_a_ln_b"])
    out["swa_sinks"] = b[ROW_LNB_SINKS_LOSS, 512:512 + SWA_HEADS].reshape(shapes["swa_sinks"])
    return out


def kernel(x, mem, ffn1_norm, ffn1_w_gu, ffn1_w_down, mix_norm, even_w_in, conv_a_w, conv_a_b, conv_a_ln_g, conv_a_ln_b, swa_sinks, even_w_out, odd_w_in, sc_conv_w, odd_w_out, xa_norm, xa_mem_norm, xa_wq, xa_wkv, xa_wo, ffn2_norm, ffn2_w_gu, ffn2_w_down, final_norm, loss_target, m_ffn1_norm, m_ffn1_w_gu, m_ffn1_w_down, m_mix_norm, m_even_w_in, m_conv_a_w, m_conv_a_b, m_conv_a_ln_g, m_conv_a_ln_b, m_swa_sinks, m_even_w_out, m_odd_w_in, m_sc_conv_w, m_odd_w_out, m_xa_norm, m_xa_mem_norm, m_xa_wq, m_xa_wkv, m_xa_wo, m_ffn2_norm, m_ffn2_w_gu, m_ffn2_w_down, m_final_norm, v_ffn1_norm, v_ffn1_w_gu, v_ffn1_w_down, v_mix_norm, v_even_w_in, v_conv_a_w, v_conv_a_b, v_conv_a_ln_g, v_conv_a_ln_b, v_swa_sinks, v_even_w_out, v_odd_w_in, v_sc_conv_w, v_odd_w_out, v_xa_norm, v_xa_mem_norm, v_xa_wq, v_xa_wkv, v_xa_wo, v_ffn2_norm, v_ffn2_w_gu, v_ffn2_w_down, v_final_norm):
    w = dict(ffn1_norm=ffn1_norm, ffn1_w_gu=ffn1_w_gu, ffn1_w_down=ffn1_w_down, mix_norm=mix_norm, even_w_in=even_w_in,
             conv_a_w=conv_a_w, conv_a_b=conv_a_b, conv_a_ln_g=conv_a_ln_g, conv_a_ln_b=conv_a_ln_b, swa_sinks=swa_sinks,
             even_w_out=even_w_out, odd_w_in=odd_w_in, sc_conv_w=sc_conv_w, odd_w_out=odd_w_out, xa_norm=xa_norm,
             xa_mem_norm=xa_mem_norm, xa_wq=xa_wq, xa_wkv=xa_wkv, xa_wo=xa_wo, ffn2_norm=ffn2_norm, ffn2_w_gu=ffn2_w_gu,
             ffn2_w_down=ffn2_w_down, final_norm=final_norm)
    m = dict(ffn1_norm=m_ffn1_norm, ffn1_w_gu=m_ffn1_w_gu, ffn1_w_down=m_ffn1_w_down, mix_norm=m_mix_norm,
             even_w_in=m_even_w_in, conv_a_w=m_conv_a_w, conv_a_b=m_conv_a_b, conv_a_ln_g=m_conv_a_ln_g,
             conv_a_ln_b=m_conv_a_ln_b, swa_sinks=m_swa_sinks, even_w_out=m_even_w_out, odd_w_in=m_odd_w_in,
             sc_conv_w=m_sc_conv_w, odd_w_out=m_odd_w_out, xa_norm=m_xa_norm, xa_mem_norm=m_xa_mem_norm, xa_wq=m_xa_wq,
             xa_wkv=m_xa_wkv, xa_wo=m_xa_wo, ffn2_norm=m_ffn2_norm, ffn2_w_gu=m_ffn2_w_gu, ffn2_w_down=m_ffn2_w_down,
             final_norm=m_final_norm)
    v = dict(ffn1_norm=v_ffn1_norm, ffn1_w_gu=v_ffn1_w_gu, ffn1_w_down=v_ffn1_w_down, mix_norm=v_mix_norm,
             even_w_in=v_even_w_in, conv_a_w=v_conv_a_w, conv_a_b=v_conv_a_b, conv_a_ln_g=v_conv_a_ln_g,
             conv_a_ln_b=v_conv_a_ln_b, swa_sinks=v_swa_sinks, even_w_out=v_even_w_out, odd_w_in=v_odd_w_in,
             sc_conv_w=v_sc_conv_w, odd_w_out=v_odd_w_out, xa_norm=v_xa_norm, xa_mem_norm=v_xa_mem_norm, xa_wq=v_xa_wq,
             xa_wkv=v_xa_wkv, xa_wo=v_xa_wo, ffn2_norm=v_ffn2_norm, ffn2_w_gu=v_ffn2_w_gu, ffn2_w_down=v_ffn2_w_down,
             final_norm=v_final_norm)
    me = _flat_index(*_mesh_pos())

    conv_blob = jnp.concatenate([w["conv_a_w"].reshape(-1), w["sc_conv_w"].reshape(-1),
                                 jnp.zeros((8 * 1024 - 31 * 64 - 3 * 128,), F32)]).reshape(8, 1024)
    conv_all = all_gather(conv_blob, "gather_conv_weights").reshape(N_DEV, 8 * 1024)
    conv_a_full = jnp.transpose(conv_all[:, :31 * 64].reshape(N_DEV, 31, 64), (1, 0, 2)).reshape(31, 512)
    sc_full = jnp.transpose(conv_all[:, 31 * 64:31 * 64 + 3 * 128].reshape(N_DEV, 3, 128), (1, 0, 2)).reshape(3, 1024)

    def stage_blob(keys):
        shards = [w[n][l].T if SPLIT_AXIS[n] == 1 else w[n][l] for n, l in keys]
        return jnp.concatenate([s.astype(BF16) for s in shards], axis=0)

    def stage_rows(keys):
        return [(n, l, w[n].shape[2] if SPLIT_AXIS[n] == 1 else w[n].shape[1]) for n, l in keys]

    def unpack_weights(gathered, keys):
        out, off = {}, 0
        for n, l, rows in stage_rows(keys):
            out[(n, l)] = gathered[:, off:off + rows, :].reshape(N_DEV * rows, BLOB_COLS)
            off += rows
        return out

    def with_own(land, own):
        return lax.dynamic_update_slice(land, own[None], (me, 0, 0))

    gathered_a = all_gather(stage_blob(STAGE_KEYS["A"]), "gather_weights_a")
    later = ("B", "C", "D")
    blobs = [stage_blob(STAGE_KEYS[s]) for s in later]
    lands = [lax.empty((N_DEV,) + b.shape, BF16) for b in blobs]
    weight_handles, weight_token = exchange_start(blobs, lands, False, gathered_a, "gather_start")

    def get_weights(stage, after):
        if stage == "A":
            return unpack_weights(gathered_a, STAGE_KEYS["A"]), weight_token
        own, land = exchange_wait(weight_handles[later.index(stage)], False, after, "gather_wait_" + stage.lower())
        return unpack_weights(with_own(land, own), STAGE_KEYS[stage]), None

    grad_handles = {}

    def put_grads(stage, dws):
        packed = jnp.concatenate([dw.reshape(N_DEV, -1, BLOB_COLS) for dw in dws.values()], axis=1)
        land = lax.empty(packed.shape, BF16)
        (handle,), token = exchange_start([packed], [land], True, packed, "scatter_start_" + stage.lower())
        grad_handles[stage] = (handle, tuple(dws))
        return token

    P = dict(ffn1_norm=ffn1_norm, mix_norm=mix_norm, xa_norm=xa_norm, xa_mem_norm=xa_mem_norm, ffn2_norm=ffn2_norm,
             final_norm=final_norm, conv_a_w=conv_a_full, conv_a_b=conv_a_b[0], conv_a_ln_g=conv_a_ln_g[0],
             conv_a_ln_b=conv_a_ln_b[0], swa_sinks=swa_sinks[0], sc_conv_w=sc_full)

    loss_part, grad_x, dP = local_step(x[0], mem[0], loss_target[0], P, get_weights, put_grads)

    def finish_grads(stage, after):
        handle, keys = grad_handles[stage]
        packed, land = exchange_wait(handle, True, after, "scatter_wait_" + stage.lower())
        own = lax.dynamic_slice(packed, (me, 0, 0), (1,) + packed.shape[1:])[0]
        rows_f32 = ordered_sum(with_own(land, own), "sum_grads_" + stage.lower())
        out, off = {}, 0
        for n, l, rows in stage_rows(keys):
            part = rows_f32[off:off + rows]
            out[(n, l)] = part.T if SPLIT_AXIS[n] == 1 else part
            off += rows
        return out

    layer_grads = {**finish_grads("D", grad_x), **finish_grads("BC", grad_x)}

    dP = dict(dP, loss=loss_part[0, 0])
    small = jnp.concatenate([
        _small_blob(dP)[:ROW_SC_CONV], dP["sc_conv_w"],
        jnp.concatenate([dP["conv_a_w"].reshape(-1), jnp.zeros((512,), F32)]).reshape(16, D_MODEL)], axis=0)
    small_sum = ordered_sum(all_gather(small, "gather_small_grads"), "sum_small_grads", tr=SMALL_BLOB_ROWS)
    loss = small_sum[ROW_LNB_SINKS_LOSS, LOSS_COL]
    grads = _small_unblob(small_sum, {n: w[n].shape for n in WEIGHT_NAMES})
    sc_g = small_sum[ROW_SC_CONV:ROW_SC_CONV + 3]
    grads["sc_conv_w"] = lax.dynamic_slice(sc_g, (0, me * 128), (3, 128)).reshape(w["sc_conv_w"].shape)
    cw_g = small_sum[ROW_CONV_W:].reshape(-1)[:31 * 512].reshape(31, 512)
    grads["conv_a_w"] = lax.dynamic_slice(cw_g, (0, me * 64), (31, 64)).reshape(w["conv_a_w"].shape)

    delta, new_m, new_v = {}, {}, {}

    def update(n):
        shp = w[n].shape
        two_d = (shp[0] * shp[1], shp[2])
        d_, m_, v_ = adamw(w[n].reshape(two_d), grads[n].reshape(two_d), m[n].reshape(two_d), v[n].reshape(two_d),
                           "adamw_" + n)
        delta[n], new_m[n], new_v[n] = d_.reshape(shp), m_.reshape(shp), v_.reshape(shp)

    first_stage = tuple(n for n, _ in STAGE_KEYS["A"])
    for n in SPLIT_AXIS:
        if n not in first_stage:
            grads[n] = jnp.stack([layer_grads[(n, l)] for l in range(w[n].shape[0])], axis=0)
            update(n)
    update("conv_a_w")
    update("sc_conv_w")
    layer_grads.update(finish_grads("A", delta["ffn2_w_gu"]))
    for n in first_stage:
        grads[n] = jnp.stack([layer_grads[(n, l)] for l in range(w[n].shape[0])], axis=0)
        update(n)
    d_, m_, v_ = adamw(_small_blob(w), small_sum[:SMALL_ADAM_ROWS], _small_blob(m), _small_blob(v), "adamw_small",
                       tr=SMALL_ADAM_ROWS)
    shapes = {n: w[n].shape for n in WEIGHT_NAMES}
    delta.update(_small_unblob(d_, shapes))
    new_m.update(_small_unblob(m_, shapes))
    new_v.update(_small_unblob(v_, shapes))

    return (loss, grad_x[None], *[grads[n] for n in WEIGHT_NAMES], *[delta[n] for n in WEIGHT_NAMES],
            *[new_m[n] for n in WEIGHT_NAMES], *[new_v[n] for n in WEIGHT_NAMES])
```

```python
import functools

import jax
import jax.numpy as jnp
from jax import lax
from jax.experimental import pallas as pl
from jax.experimental.pallas import tpu as pltpu

F32 = jnp.float32
BF16 = jnp.bfloat16

D_MODEL = 1024
D_FF = 2816
CONV_A_CH = 512
CONV_A_WIDTH = 31
SWA_HEADS = 8
SWA_KV_HEADS = 2
SWA_GROUP = SWA_HEADS // SWA_KV_HEADS
HEAD_DIM = 64
WINDOW = 128
SC_CH = 1024
XA_HEADS = 4
XA_HEAD_DIM = D_MODEL // XA_HEADS
RMS_EPS = 1e-6
LN_EPS = 1e-5
ADAM_LR = 0.001
ADAM_B1 = 0.9
ADAM_B2 = 0.999
ADAM_EPS = 1e-08
ADAM_WD = 0.01
ADAM_STEP = 10
N_DEV = 8

V7X_VMEM_BYTES = 64 * 1024 * 1024
VMEM_LIMIT = V7X_VMEM_BYTES - 8 * 1024 * 1024
CONV_HALO = 32
SC_HALO = 8
NEG_BIG = -1e30

SPLIT_AXIS = dict(ffn1_w_gu=1, ffn1_w_down=0, even_w_in=1, even_w_out=0, odd_w_in=1, odd_w_out=0, xa_wq=0, xa_wkv=1, xa_wo=0,
                  ffn2_w_gu=1, ffn2_w_down=0)
STAGE_KEYS = dict(
    A=(("ffn1_w_gu", 0), ("ffn1_w_down", 0)),
    B=(("even_w_in", 0), ("even_w_out", 0), ("xa_wq", 0), ("xa_wkv", 0), ("xa_wo", 0)),
    C=(("ffn2_w_gu", 0), ("ffn2_w_down", 0)),
    D=(("ffn1_w_gu", 1), ("ffn1_w_down", 1), ("odd_w_in", 0), ("odd_w_out", 0), ("xa_wq", 1), ("xa_wkv", 1), ("xa_wo", 1),
       ("ffn2_w_gu", 1), ("ffn2_w_down", 1)))


def _params(n_axes):
    return pltpu.CompilerParams(dimension_semantics=("arbitrary",) * n_axes, vmem_limit_bytes=VMEM_LIMIT)


def _tile(n, pref):
    t = min(n, pref)
    assert n % t == 0, (n, pref)
    return t


def _dot(a, b):
    return jnp.dot(a, b, preferred_element_type=F32)


def _dot_nt(a, b):
    return lax.dot_general(a, b, (((1,), (1,)), ((), ())), preferred_element_type=F32)


def _dot_tn(a, b):
    return lax.dot_general(a, b, (((0,), (0,)), ((), ())), preferred_element_type=F32)


def _sigmoid(x):
    return 1.0 / (1.0 + jnp.exp(-x))


ANY_SPEC = pl.BlockSpec(memory_space=pl.ANY)


def _with_dep(body, n_in, dep):
    if dep is None:
        return body, [], []
    return (lambda *refs: body(*refs[:n_in], *refs[n_in + 1:])), [ANY_SPEC], [dep]


def norm_matmul(h, g, w, out_dtype, name, tm=512, tn=None, dep=None, transposed=True):
    T, K = h.shape
    N = w.shape[0] if transposed else w.shape[1]
    tm = _tile(T, tm)
    tn = N if tn is None else tn

    def kern(h_ref, g_ref, w_ref, z_ref, u_ref):
        @pl.when(pl.program_id(1) == 0)
        def _():
            x = h_ref[...]
            r = lax.rsqrt(jnp.mean(x * x, axis=-1, keepdims=True) + RMS_EPS)
            u_ref[...] = ((x * r) * g_ref[...]).astype(BF16)

        mm = _dot_nt if transposed else _dot
        z_ref[...] = mm(u_ref[...], w_ref[...]).astype(z_ref.dtype)

    body, dep_spec, dep_arg = _with_dep(kern, 3, dep)
    w_spec = pl.BlockSpec((tn, K), lambda i, j: (j, 0)) if transposed else pl.BlockSpec((K, tn), lambda i, j: (0, j))
    return pl.pallas_call(
        body, name=name, grid=(T // tm, N // tn),
        in_specs=[pl.BlockSpec((tm, K), lambda i, j: (i, 0)),
                  pl.BlockSpec((1, K), lambda i, j: (0, 0)),
                  w_spec] + dep_spec,
        out_specs=[pl.BlockSpec((tm, tn), lambda i, j: (i, j)),
                   pl.BlockSpec((tm, K), lambda i, j: (i, 0))],
        out_shape=[jax.ShapeDtypeStruct((T, N), out_dtype), jax.ShapeDtypeStruct((T, K), BF16)],
        compiler_params=_params(2),
    )(h, g, w, *dep_arg)


def matmul_residual(a, w, res, name, tm=512):
    T, K = a.shape
    N = w.shape[1]
    tm = _tile(T, tm)

    def body(a_ref, w_ref, r_ref, o_ref):
        o_ref[...] = r_ref[...] + _dot(a_ref[...], w_ref[...])

    return pl.pallas_call(
        body, name=name, grid=(T // tm,),
        in_specs=[pl.BlockSpec((tm, K), lambda i: (i, 0)),
                  pl.BlockSpec((K, N), lambda i: (0, 0)),
                  pl.BlockSpec((tm, N), lambda i: (i, 0))],
        out_specs=pl.BlockSpec((tm, N), lambda i: (i, 0)),
        out_shape=jax.ShapeDtypeStruct((T, N), F32),
        compiler_params=_params(1),
    )(a, w, res)


def matmul_nt(dy, w, out_dtype, name, tm=512):
    T, N = dy.shape
    K = w.shape[0]
    tm = _tile(T, tm)

    def body(dy_ref, w_ref, o_ref):
        o_ref[...] = _dot_nt(dy_ref[...].astype(BF16), w_ref[...]).astype(o_ref.dtype)

    return pl.pallas_call(
        body, name=name, grid=(T // tm,),
        in_specs=[pl.BlockSpec((tm, N), lambda i: (i, 0)),
                  pl.BlockSpec((K, N), lambda i: (0, 0))],
        out_specs=pl.BlockSpec((tm, K), lambda i: (i, 0)),
        out_shape=jax.ShapeDtypeStruct((T, K), out_dtype),
        compiler_params=_params(1),
    )(dy, w)


def matmul_norm_bwd(dz, w, h, g, dh_in, name, tm=256, transposed=True, dep=None):
    T, N = dz.shape
    K = h.shape[1]
    tm = _tile(T, tm)

    def kern(dz_ref, w_ref, h_ref, g_ref, dhin_ref, dh_ref, dg_ref):
        @pl.when(pl.program_id(0) == 0)
        def _():
            dg_ref[...] = jnp.zeros_like(dg_ref)

        mm = _dot if transposed else _dot_nt
        du = mm(dz_ref[...], w_ref[...])
        x = h_ref[...]
        r = lax.rsqrt(jnp.mean(x * x, axis=-1, keepdims=True) + RMS_EPS)
        xh = x * r
        dg_ref[...] += jnp.sum(du * xh, axis=0, keepdims=True)
        dxh = du * g_ref[...]
        dh_ref[...] = dhin_ref[...] + r * (dxh - xh * jnp.mean(dxh * xh, axis=-1, keepdims=True))

    body, dep_spec, dep_arg = _with_dep(kern, 5, dep)
    return pl.pallas_call(
        body, name=name, grid=(T // tm,),
        in_specs=[pl.BlockSpec((tm, N), lambda i: (i, 0)),
                  pl.BlockSpec(w.shape, lambda i: (0, 0)),
                  pl.BlockSpec((tm, K), lambda i: (i, 0)),
                  pl.BlockSpec((1, K), lambda i: (0, 0)),
                  pl.BlockSpec((tm, K), lambda i: (i, 0))] + dep_spec,
        out_specs=[pl.BlockSpec((tm, K), lambda i: (i, 0)),
                   pl.BlockSpec((1, K), lambda i: (0, 0))],
        out_shape=[jax.ShapeDtypeStruct((T, K), F32), jax.ShapeDtypeStruct((1, K), F32)],
        compiler_params=_params(1),
    )(dz, w, h, g, dh_in, *dep_arg)


def matmul_tn(x, dy, name, scale=1.0, tk=None, tn=None, tt=512):
    T, K = x.shape
    N = dy.shape[1]
    tk = K if tk is None else tk
    tn = N if tn is None else tn
    tt = _tile(T, tt)
    nt = T // tt

    def body(x_ref, dy_ref, o_ref, acc_ref):
        t = pl.program_id(2)

        @pl.when(t == 0)
        def _():
            acc_ref[...] = jnp.zeros_like(acc_ref)

        acc_ref[...] += _dot_tn(x_ref[...].astype(BF16), dy_ref[...].astype(BF16))

        @pl.when(t == nt - 1)
        def _():
            o_ref[...] = (acc_ref[...] * scale).astype(o_ref.dtype)

    return pl.pallas_call(
        body, name=name, grid=(K // tk, N // tn, nt),
        in_specs=[pl.BlockSpec((tt, tk), lambda a, b, t: (t, a)),
                  pl.BlockSpec((tt, tn), lambda a, b, t: (t, b))],
        out_specs=pl.BlockSpec((tk, tn), lambda a, b, t: (a, b)),
        out_shape=jax.ShapeDtypeStruct((K, N), BF16),
        scratch_shapes=[pltpu.VMEM((tk, tn), F32)],
        compiler_params=_params(3),
    )(x, dy)


def ffn_down(gu, wd, res, name, tm=512):
    T = gu.shape[0]
    F = gu.shape[1] // 2
    N = wd.shape[1]
    tm = _tile(T, tm)

    def body(g_ref, up_ref, w_ref, r_ref, o_ref, a_ref):
        g = g_ref[...].astype(F32)
        a = (g * _sigmoid(g)) * up_ref[...].astype(F32)
        a_ref[...] = a.astype(BF16)
        o_ref[...] = r_ref[...] + 0.5 * _dot(a_ref[...], w_ref[...])

    return pl.pallas_call(
        body, name=name, grid=(T // tm,),
        in_specs=[pl.BlockSpec((tm, F), lambda i: (i, 0)),
                  pl.BlockSpec((tm, F), lambda i: (i, 1)),
                  pl.BlockSpec((F, N), lambda i: (0, 0)),
                  pl.BlockSpec((tm, N), lambda i: (i, 0))],
        out_specs=[pl.BlockSpec((tm, N), lambda i: (i, 0)),
                   pl.BlockSpec((tm, F), lambda i: (i, 0))],
        out_shape=[jax.ShapeDtypeStruct((T, N), F32), jax.ShapeDtypeStruct((T, F), BF16)],
        compiler_params=_params(1),
    )(gu, gu, wd, res)


def ffn_down_bwd(dy, wd, gu, name, tm=512, dep=None):
    T, N = dy.shape
    F = wd.shape[0]
    tm = _tile(T, tm)

    def kern(dy_ref, w_ref, g_ref, up_ref, o_ref):
        da = 0.5 * _dot_nt(dy_ref[...].astype(BF16), w_ref[...])
        g = g_ref[...].astype(F32)
        up = up_ref[...].astype(F32)
        s = _sigmoid(g)
        o_ref[:, :F] = (da * up * (s * (1.0 + g * (1.0 - s)))).astype(BF16)
        o_ref[:, F:] = (da * (g * s)).astype(BF16)

    body, dep_spec, dep_arg = _with_dep(kern, 4, dep)
    return pl.pallas_call(
        body, name=name, grid=(T // tm,),
        in_specs=[pl.BlockSpec((tm, N), lambda i: (i, 0)),
                  pl.BlockSpec((F, N), lambda i: (0, 0)),
                  pl.BlockSpec((tm, F), lambda i: (i, 0)),
                  pl.BlockSpec((tm, F), lambda i: (i, 1))] + dep_spec,
        out_specs=pl.BlockSpec((tm, 2 * F), lambda i: (i, 0)),
        out_shape=jax.ShapeDtypeStruct((T, 2 * F), BF16),
        compiler_params=_params(1),
    )(dy, wd, gu, gu, *dep_arg)


def ffn_forward(h, g, w_gu, w_down, name, dep=None):
    gu, u = norm_matmul(h, g, w_gu, BF16, name + "_gu", tn=D_FF, dep=dep)
    h_out, a = ffn_down(gu, w_down, h, name + "_down")
    return h_out, (h, u, gu, a)


def ffn_backward(dy, saved, g, w_gu, w_down, name, dep=None, emit=None):
    h, u, gu, a = saved
    dgu = ffn_down_bwd(dy, w_down, gu, name + "_ddown", dep=dep)
    d_w_down = matmul_tn(a, dy, name + "_dwd", scale=0.5, tk=D_FF // 2)
    d_w_gu = matmul_tn(dgu, u, name + "_dwgu", tk=D_FF)
    dh, dg = matmul_norm_bwd(dgu, w_gu, h, g, dy, name + "_dx", dep=emit(d_w_gu, d_w_down))
    return dh, dg


def conformer_conv_fwd(z, cw, cb, lg, lb, name, tm=512):
    T = z.shape[0]
    C = CONV_A_CH
    tm = _tile(T, tm)
    hb = tm // CONV_HALO

    def body(v_ref, gt_ref, pv_ref, pg_ref, cw_ref, cb_ref, lg_ref, lb_ref, o_ref, xs_ref):
        i = pl.program_id(0)
        prev = pv_ref[...] * _sigmoid(pg_ref[...])
        xs_ref[0:CONV_HALO, :] = jnp.where(i > 0, prev, 0.0)
        xs_ref[CONV_HALO:, :] = v_ref[...] * _sigmoid(gt_ref[...])
        acc = jnp.zeros((tm, C), F32) + cb_ref[...]
        for k in range(CONV_A_WIDTH):
            acc = acc + cw_ref[k:k + 1, :] * xs_ref[pl.ds(CONV_HALO - (CONV_A_WIDTH - 1) + k, tm), :]
        mu = jnp.mean(acc, axis=-1, keepdims=True)
        xc = acc - mu
        var = jnp.mean(xc * xc, axis=-1, keepdims=True)
        y = (xc * lax.rsqrt(var + LN_EPS)) * lg_ref[...] + lb_ref[...]
        o_ref[...] = (y * _sigmoid(y)).astype(BF16)

    return pl.pallas_call(
        body, name=name, grid=(T // tm,),
        in_specs=[pl.BlockSpec((tm, C), lambda i: (i, 0)),
                  pl.BlockSpec((tm, C), lambda i: (i, 1)),
                  pl.BlockSpec((CONV_HALO, C), lambda i: (jnp.maximum(i * hb - 1, 0), 0)),
                  pl.BlockSpec((CONV_HALO, C), lambda i: (jnp.maximum(i * hb - 1, 0), 1)),
                  pl.BlockSpec((32, C), lambda i: (0, 0)),
                  pl.BlockSpec((1, C), lambda i: (0, 0)),
                  pl.BlockSpec((1, C), lambda i: (0, 0)),
                  pl.BlockSpec((1, C), lambda i: (0, 0))],
        out_specs=pl.BlockSpec((tm, C), lambda i: (i, 0)),
        out_shape=jax.ShapeDtypeStruct((T, C), BF16),
        scratch_shapes=[pltpu.VMEM((tm + CONV_HALO, C), F32)],
        compiler_params=_params(1),
    )(z, z, z, z, cw, cb, lg, lb)


def conformer_conv_bwd(z, dm, cw, cb, lg, lb, name, tm=512):
    T = z.shape[0]
    C = CONV_A_CH
    tm = _tile(T, tm)
    hb = tm // CONV_HALO
    n_tiles = T // tm
    last_halo = T // CONV_HALO - 1
    R = tm + CONV_HALO
    KW = CONV_A_WIDTH

    def body(v_ref, gt_ref, pv_ref, pg_ref, nv_ref, ng_ref, do_ref, ndo_ref, cw_ref, cb_ref, lg_ref, lb_ref,
             dz_ref, dcw_ref, dcb_ref, dlg_ref, dlb_ref, xs_ref, ds_ref):
        i = pl.program_id(0)

        @pl.when(i == 0)
        def _():
            dcw_ref[...] = jnp.zeros_like(dcw_ref)
            dcb_ref[...] = jnp.zeros_like(dcb_ref)
            dlg_ref[...] = jnp.zeros_like(dlg_ref)
            dlb_ref[...] = jnp.zeros_like(dlb_ref)

        val = v_ref[...]
        sg = _sigmoid(gt_ref[...])
        prev = pv_ref[...] * _sigmoid(pg_ref[...])
        xs_ref[0:CONV_HALO, :] = jnp.where(i > 0, prev, 0.0)
        xs_ref[CONV_HALO:CONV_HALO + tm, :] = val * sg
        xs_ref[CONV_HALO + tm:, :] = nv_ref[...] * _sigmoid(ng_ref[...])

        acc = jnp.zeros((R, C), F32) + cb_ref[...]
        for k in range(KW):
            acc = acc + cw_ref[k:k + 1, :] * xs_ref[pl.ds(CONV_HALO - (KW - 1) + k, R), :]
        mu = jnp.mean(acc, axis=-1, keepdims=True)
        xc = acc - mu
        rstd = lax.rsqrt(jnp.mean(xc * xc, axis=-1, keepdims=True) + LN_EPS)
        xh = xc * rstd
        y = xh * lg_ref[...] + lb_ref[...]
        s = _sigmoid(y)
        dout = jnp.concatenate([do_ref[...], jnp.where(i < n_tiles - 1, ndo_ref[...], 0.0)], axis=0)
        dy = dout * (s * (1.0 + y * (1.0 - s)))
        dxh = dy * lg_ref[...]
        dconv = rstd * (dxh - jnp.mean(dxh, axis=-1, keepdims=True) - xh * jnp.mean(dxh * xh, axis=-1, keepdims=True))
        ds_ref[...] = dconv

        dy_m = dy[:tm]
        dlg_ref[...] += jnp.sum(dy_m * xh[:tm], axis=0, keepdims=True)
        dlb_ref[...] += jnp.sum(dy_m, axis=0, keepdims=True)
        dc_m = dconv[:tm]
        dcb_ref[...] += jnp.sum(dc_m, axis=0, keepdims=True)
        dglu = jnp.zeros((tm, C), F32)
        for k in range(KW):
            dcw_ref[k:k + 1, :] += jnp.sum(dc_m * xs_ref[pl.ds(CONV_HALO - (KW - 1) + k, tm), :], axis=0, keepdims=True)
            dglu = dglu + cw_ref[k:k + 1, :] * ds_ref[pl.ds(KW - 1 - k, tm), :]
        dz_ref[:, :C] = (dglu * sg).astype(BF16)
        dz_ref[:, C:] = (dglu * val * sg * (1.0 - sg)).astype(BF16)

    prev_map = lambda i: jnp.maximum(i * hb - 1, 0)
    next_map = lambda i: jnp.minimum((i + 1) * hb, last_halo)
    return pl.pallas_call(
        body, name=name, grid=(n_tiles,),
        in_specs=[pl.BlockSpec((tm, C), lambda i: (i, 0)),
                  pl.BlockSpec((tm, C), lambda i: (i, 1)),
                  pl.BlockSpec((CONV_HALO, C), lambda i: (prev_map(i), 0)),
                  pl.BlockSpec((CONV_HALO, C), lambda i: (prev_map(i), 1)),
                  pl.BlockSpec((CONV_HALO, C), lambda i: (next_map(i), 0)),
                  pl.BlockSpec((CONV_HALO, C), lambda i: (next_map(i), 1)),
                  pl.BlockSpec((tm, C), lambda i: (i, 0)),
                  pl.BlockSpec((CONV_HALO, C), lambda i: (next_map(i), 0)),
                  pl.BlockSpec((32, C), lambda i: (0, 0)),
                  pl.BlockSpec((1, C), lambda i: (0, 0)),
                  pl.BlockSpec((1, C), lambda i: (0, 0)),
                  pl.BlockSpec((1, C), lambda i: (0, 0))],
        out_specs=[pl.BlockSpec((tm, 2 * C), lambda i: (i, 0)),
                   pl.BlockSpec((32, C), lambda i: (0, 0)),
                   pl.BlockSpec((1, C), lambda i: (0, 0)),
                   pl.BlockSpec((1, C), lambda i: (0, 0)),
                   pl.BlockSpec((1, C), lambda i: (0, 0))],
        out_shape=[jax.ShapeDtypeStruct((T, 2 * C), BF16),
                   jax.ShapeDtypeStruct((32, C), F32),
                   jax.ShapeDtypeStruct((1, C), F32),
                   jax.ShapeDtypeStruct((1, C), F32),
                   jax.ShapeDtypeStruct((1, C), F32)],
        scratch_shapes=[pltpu.VMEM((tm + 2 * CONV_HALO, C), F32), pltpu.VMEM((R, C), F32)],
        compiler_params=_params(1),
    )(z, z, z, z, z, z, dm, dm, cw, cb, lg, lb)


def _swa_scores(q_h, kk_h, slope, bias_dist, valid, sink):
    s = _dot_nt(q_h, kk_h) * (HEAD_DIM ** -0.5) - slope * bias_dist
    s = jnp.where(valid, s, NEG_BIG)
    m = jnp.maximum(jnp.max(s, axis=-1, keepdims=True), sink)
    p = jnp.exp(s - m)
    e_sink = jnp.exp(sink - m)
    inv = 1.0 / (jnp.sum(p, axis=-1, keepdims=True) + e_sink)
    return p * inv, e_sink * inv


def _swa_mask(r0):
    qi = lax.broadcasted_iota(jnp.int32, (WINDOW, 2 * WINDOW), 0)
    kj = lax.broadcasted_iota(jnp.int32, (WINDOW, 2 * WINDOW), 1)
    dist = qi + WINDOW - kj
    valid = (dist >= 0) & (dist < WINDOW) & (r0 - WINDOW + kj >= 0)
    return dist.astype(F32), valid


def swa_fwd(z, kpad, vpad, sinks, name, tq=512):
    T = z.shape[0]
    tq = _tile(T, tq)
    HQ = SWA_HEADS * HEAD_DIM

    def body(sink_ref, q_ref, k_ref, v_ref, o_ref):
        i = pl.program_id(0)
        for sub in range(tq // WINDOW):
            r0 = pl.multiple_of(i * tq + sub * WINDOW, WINDOW)
            kk = k_ref[pl.ds(r0, 2 * WINDOW), :]
            vv = v_ref[pl.ds(r0, 2 * WINDOW), :]
            qb = q_ref[sub * WINDOW:(sub + 1) * WINDOW, :].astype(BF16)
            dist, valid = _swa_mask(r0)
            outs = []
            for h in range(SWA_HEADS):
                kh = h // SWA_GROUP
                ks = slice(kh * HEAD_DIM, (kh + 1) * HEAD_DIM)
                pn, _ = _swa_scores(qb[:, h * HEAD_DIM:(h + 1) * HEAD_DIM], kk[:, ks], 2.0 ** (-(h + 1)), dist, valid,
                                    sink_ref[h])
                outs.append(_dot(pn.astype(BF16), vv[:, ks]))
            o_ref[sub * WINDOW:(sub + 1) * WINDOW, :] = jnp.concatenate(outs, axis=-1).astype(BF16)

    return pl.pallas_call(
        body, name=name, grid=(T // tq,),
        in_specs=[pl.BlockSpec(memory_space=pltpu.SMEM),
                  pl.BlockSpec((tq, HQ), lambda i: (i, 2)),
                  pl.BlockSpec((T + WINDOW, 2 * HEAD_DIM), lambda i: (0, 0)),
                  pl.BlockSpec((T + WINDOW, 2 * HEAD_DIM), lambda i: (0, 0))],
        out_specs=pl.BlockSpec((tq, HQ), lambda i: (i, 0)),
        out_shape=jax.ShapeDtypeStruct((T, HQ), BF16),
        compiler_params=_params(1),
    )(sinks, z, kpad, vpad)


def swa_bwd(z, kpad, vpad, sinks, dm, name, tq=512):
    T = z.shape[0]
    tq = _tile(T, tq)
    HQ = SWA_HEADS * HEAD_DIM
    scale = HEAD_DIM ** -0.5

    def body(sink_ref, q_ref, k_ref, v_ref, do_ref, dq_ref, dk_ref, dv_ref, dsink_ref):
        i = pl.program_id(0)

        @pl.when(i == 0)
        def _():
            dk_ref[...] = jnp.zeros_like(dk_ref)
            dv_ref[...] = jnp.zeros_like(dv_ref)
            dsink_ref[...] = jnp.zeros_like(dsink_ref)

        for sub in range(tq // WINDOW):
            r0 = pl.multiple_of(i * tq + sub * WINDOW, WINDOW)
            kk = k_ref[pl.ds(r0, 2 * WINDOW), :]
            vv = v_ref[pl.ds(r0, 2 * WINDOW), :]
            rows = slice(sub * WINDOW, (sub + 1) * WINDOW)
            qb = q_ref[rows, :].astype(BF16)
            dob = do_ref[rows, :].astype(BF16)
            dist, valid = _swa_mask(r0)
            dqs, dks, dvs = [], [], []
            for kh in range(SWA_KV_HEADS):
                ks = slice(kh * HEAD_DIM, (kh + 1) * HEAD_DIM)
                dk_acc = jnp.zeros((2 * WINDOW, HEAD_DIM), F32)
                dv_acc = jnp.zeros((2 * WINDOW, HEAD_DIM), F32)
                for g in range(SWA_GROUP):
                    h = kh * SWA_GROUP + g
                    hs = slice(h * HEAD_DIM, (h + 1) * HEAD_DIM)
                    pn, p_sink = _swa_scores(qb[:, hs], kk[:, ks], 2.0 ** (-(h + 1)), dist, valid, sink_ref[h])
                    dp = _dot_nt(dob[:, hs], vv[:, ks])
                    delta = jnp.sum(pn * dp, axis=-1, keepdims=True)
                    ds = (pn * (dp - delta)).astype(BF16)
                    dqs.append(_dot(ds, kk[:, ks]) * scale)
                    dk_acc = dk_acc + _dot_tn(ds, qb[:, hs]) * scale
                    dv_acc = dv_acc + _dot_tn(pn.astype(BF16), dob[:, hs])
                    dsink_ref[h:h + 1, :] += jnp.zeros((1, 128), F32) - jnp.sum(p_sink * delta)
                dks.append(dk_acc)
                dvs.append(dv_acc)
            dq_ref[rows, :] = jnp.concatenate(dqs, axis=-1).astype(BF16)
            dk_ref[pl.ds(r0, 2 * WINDOW), :] += jnp.concatenate(dks, axis=-1)
            dv_ref[pl.ds(r0, 2 * WINDOW), :] += jnp.concatenate(dvs, axis=-1)

    kv_spec = pl.BlockSpec((T + WINDOW, 2 * HEAD_DIM), lambda i: (0, 0))
    return pl.pallas_call(
        body, name=name, grid=(T // tq,),
        in_specs=[pl.BlockSpec(memory_space=pltpu.SMEM),
                  pl.BlockSpec((tq, HQ), lambda i: (i, 2)),
                  kv_spec, kv_spec,
                  pl.BlockSpec((tq, HQ), lambda i: (i, 1))],
        out_specs=[pl.BlockSpec((tq, HQ), lambda i: (i, 0)),
                   kv_spec, kv_spec,
                   pl.BlockSpec((SWA_HEADS, 128), lambda i: (0, 0))],
        out_shape=[jax.ShapeDtypeStruct((T, HQ), BF16),
                   jax.ShapeDtypeStruct((T + WINDOW, 2 * HEAD_DIM), F32),
                   jax.ShapeDtypeStruct((T + WINDOW, 2 * HEAD_DIM), F32),
                   jax.ShapeDtypeStruct((SWA_HEADS, 128), F32)],
        compiler_params=_params(1),
    )(sinks, z, kpad, vpad, dm)


def short_conv_fwd(z, w, name, tm=512):
    T = z.shape[0]
    C = SC_CH
    tm = _tile(T, tm)
    hb = tm // SC_HALO

    def body(b_ref, c_ref, v_ref, pc_ref, pv_ref, w_ref, o_ref, xs_ref):
        i = pl.program_id(0)
        xs_ref[0:SC_HALO, :] = jnp.where(i > 0, pc_ref[...] * pv_ref[...], 0.0)
        xs_ref[SC_HALO:, :] = c_ref[...] * v_ref[...]
        conv = jnp.zeros((tm, C), F32)
        for k in range(3):
            conv = conv + w_ref[k:k + 1, :] * xs_ref[pl.ds(SC_HALO - 2 + k, tm), :]
        o_ref[...] = (b_ref[...] * conv).astype(BF16)

    prev_map = lambda i: jnp.maximum(i * hb - 1, 0)
    return pl.pallas_call(
        body, name=name, grid=(T // tm,),
        in_specs=[pl.BlockSpec((tm, C), lambda i: (i, 0)),
                  pl.BlockSpec((tm, C), lambda i: (i, 1)),
                  pl.BlockSpec((tm, C), lambda i: (i, 2)),
                  pl.BlockSpec((SC_HALO, C), lambda i: (prev_map(i), 1)),
                  pl.BlockSpec((SC_HALO, C), lambda i: (prev_map(i), 2)),
                  pl.BlockSpec((8, C), lambda i: (0, 0))],
        out_specs=pl.BlockSpec((tm, C), lambda i: (i, 0)),
        out_shape=jax.ShapeDtypeStruct((T, C), BF16),
        scratch_shapes=[pltpu.VMEM((tm + SC_HALO, C), F32)],
        compiler_params=_params(1),
    )(z, z, z, z, z, w)


def short_conv_bwd(z, dm, w, name, tm=512):
    T = z.shape[0]
    C = SC_CH
    tm = _tile(T, tm)
    hb = tm // SC_HALO
    n_tiles = T // tm
    last_halo = T // SC_HALO - 1
    R = tm + SC_HALO

    def body(b_ref, c_ref, v_ref, pc_ref, pv_ref, nb_ref, do_ref, ndo_ref, w_ref, dz_ref, dw_ref, xs_ref, ds_ref):
        i = pl.program_id(0)

        @pl.when(i == 0)
        def _():
            dw_ref[...] = jnp.zeros_like(dw_ref)

        c = c_ref[...]
        val = v_ref[...]
        dout = do_ref[...]
        xs_ref[0:SC_HALO, :] = jnp.where(i > 0, pc_ref[...] * pv_ref[...], 0.0)
        xs_ref[SC_HALO:, :] = c * val
        dconv = dout * b_ref[...]
        ds_ref[0:tm, :] = dconv
        ds_ref[tm:, :] = jnp.where(i < n_tiles - 1, ndo_ref[...] * nb_ref[...], 0.0)
        conv = jnp.zeros((tm, C), F32)
        dcv = jnp.zeros((tm, C), F32)
        for k in range(3):
            xk = xs_ref[pl.ds(SC_HALO - 2 + k, tm), :]
            conv = conv + w_ref[k:k + 1, :] * xk
            dw_ref[k:k + 1, :] += jnp.sum(dconv * xk, axis=0, keepdims=True)
            dcv = dcv + w_ref[k:k + 1, :] * ds_ref[pl.ds(2 - k, tm), :]
        dz_ref[:, 0:C] = (dout * conv).astype(BF16)
        dz_ref[:, C:2 * C] = (dcv * val).astype(BF16)
        dz_ref[:, 2 * C:] = (dcv * c).astype(BF16)

    prev_map = lambda i: jnp.maximum(i * hb - 1, 0)
    next_map = lambda i: jnp.minimum((i + 1) * hb, last_halo)
    return pl.pallas_call(
        body, name=name, grid=(n_tiles,),
        in_specs=[pl.BlockSpec((tm, C), lambda i: (i, 0)),
                  pl.BlockSpec((tm, C), lambda i: (i, 1)),
                  pl.BlockSpec((tm, C), lambda i: (i, 2)),
                  pl.BlockSpec((SC_HALO, C), lambda i: (prev_map(i), 1)),
                  pl.BlockSpec((SC_HALO, C), lambda i: (prev_map(i), 2)),
                  pl.BlockSpec((SC_HALO, C), lambda i: (next_map(i), 0)),
                  pl.BlockSpec((tm, C), lambda i: (i, 0)),
                  pl.BlockSpec((SC_HALO, C), lambda i: (next_map(i), 0)),
                  pl.BlockSpec((8, C), lambda i: (0, 0))],
        out_specs=[pl.BlockSpec((tm, 3 * C), lambda i: (i, 0)),
                   pl.BlockSpec((8, C), lambda i: (0, 0))],
        out_shape=[jax.ShapeDtypeStruct((T, 3 * C), BF16), jax.ShapeDtypeStruct((8, C), F32)],
        scratch_shapes=[pltpu.VMEM((tm + SC_HALO, C), F32), pltpu.VMEM((R, C), F32)],
        compiler_params=_params(1),
    )(z, z, z, z, z, z, dm, dm, w)


def _xa_probs(q_h, k_h):
    s = _dot_nt(q_h, k_h) * (XA_HEAD_DIM ** -0.5)
    p = jnp.exp(s - jnp.max(s, axis=-1, keepdims=True))
    return p * (1.0 / jnp.sum(p, axis=-1, keepdims=True))


def xattn_fwd(q, kv, name, tm=512):
    T = q.shape[0]
    M = kv.shape[0]
    tm = _tile(T, tm)

    def body(q_ref, k_ref, v_ref, o_ref):
        for h in range(XA_HEADS):
            hs = slice(h * XA_HEAD_DIM, (h + 1) * XA_HEAD_DIM)
            p = _xa_probs(q_ref[:, hs], k_ref[:, hs])
            o_ref[:, hs] = _dot(p.astype(BF16), v_ref[:, hs]).astype(BF16)

    return pl.pallas_call(
        body, name=name, grid=(T // tm,),
        in_specs=[pl.BlockSpec((tm, D_MODEL), lambda i: (i, 0)),
                  pl.BlockSpec((M, D_MODEL), lambda i: (0, 0)),
                  pl.BlockSpec((M, D_MODEL), lambda i: (0, 1))],
        out_specs=pl.BlockSpec((tm, D_MODEL), lambda i: (i, 0)),
        out_shape=jax.ShapeDtypeStruct((T, D_MODEL), BF16),
        compiler_params=_params(1),
    )(q, kv, kv)


def xattn_bwd(q, kv, do, name, tm=512):
    T = q.shape[0]
    M = kv.shape[0]
    tm = _tile(T, tm)
    scale = XA_HEAD_DIM ** -0.5

    def body(q_ref, k_ref, v_ref, do_ref, dq_ref, dkv_ref):
        @pl.when(pl.program_id(0) == 0)
        def _():
            dkv_ref[...] = jnp.zeros_like(dkv_ref)

        for h in range(XA_HEADS):
            hs = slice(h * XA_HEAD_DIM, (h + 1) * XA_HEAD_DIM)
            vs = slice(D_MODEL + h * XA_HEAD_DIM, D_MODEL + (h + 1) * XA_HEAD_DIM)
            q_h = q_ref[:, hs]
            do_h = do_ref[:, hs]
            p = _xa_probs(q_h, k_ref[:, hs])
            dp = _dot_nt(do_h, v_ref[:, hs])
            ds = (p * (dp - jnp.sum(p * dp, axis=-1, keepdims=True))).astype(BF16)
            dq_ref[:, hs] = (_dot(ds, k_ref[:, hs]) * scale).astype(BF16)
            dkv_ref[:, hs] += _dot_tn(ds, q_h) * scale
            dkv_ref[:, vs] += _dot_tn(p.astype(BF16), do_h)

    return pl.pallas_call(
        body, name=name, grid=(T // tm,),
        in_specs=[pl.BlockSpec((tm, D_MODEL), lambda i: (i, 0)),
                  pl.BlockSpec((M, D_MODEL), lambda i: (0, 0)),
                  pl.BlockSpec((M, D_MODEL), lambda i: (0, 1)),
                  pl.BlockSpec((tm, D_MODEL), lambda i: (i, 0))],
        out_specs=[pl.BlockSpec((tm, D_MODEL), lambda i: (i, 0)),
                   pl.BlockSpec((M, 2 * D_MODEL), lambda i: (0, 0))],
        out_shape=[jax.ShapeDtypeStruct((T, D_MODEL), BF16), jax.ShapeDtypeStruct((M, 2 * D_MODEL), F32)],
        compiler_params=_params(1),
    )(q, kv, kv, do)


def final_loss(h, g, target, name, tm=512):
    T, K = h.shape
    tm = _tile(T, tm)

    def body(h_ref, g_ref, t_ref, dh_ref, dg_ref, loss_ref):
        @pl.when(pl.program_id(0) == 0)
        def _():
            dg_ref[...] = jnp.zeros_like(dg_ref)
            loss_ref[...] = jnp.zeros_like(loss_ref)

        x = h_ref[...]
        r = lax.rsqrt(jnp.mean(x * x, axis=-1, keepdims=True) + RMS_EPS)
        xh = x * r
        e = xh * g_ref[...] - t_ref[...]
        loss_ref[...] += jnp.zeros((1, 128), F32) + 0.5 * jnp.sum(jnp.mean(e * e, axis=-1, keepdims=True))
        dy = e * (1.0 / K)
        dg_ref[...] += jnp.sum(dy * xh, axis=0, keepdims=True)
        dxh = dy * g_ref[...]
        dh_ref[...] = r * (dxh - xh * jnp.mean(dxh * xh, axis=-1, keepdims=True))

    return pl.pallas_call(
        body, name=name, grid=(T // tm,),
        in_specs=[pl.BlockSpec((tm, K), lambda i: (i, 0)),
                  pl.BlockSpec((1, K), lambda i: (0, 0)),
                  pl.BlockSpec((tm, K), lambda i: (i, 0))],
        out_specs=[pl.BlockSpec((tm, K), lambda i: (i, 0)),
                   pl.BlockSpec((1, K), lambda i: (0, 0)),
                   pl.BlockSpec((1, 128), lambda i: (0, 0))],
        out_shape=[jax.ShapeDtypeStruct((T, K), F32), jax.ShapeDtypeStruct((1, K), F32),
                   jax.ShapeDtypeStruct((1, 128), F32)],
        compiler_params=_params(1),
    )(h, g, target)


def _row(v):
    return v.reshape(1, -1)


def _pad_rows(a, rows):
    return jnp.pad(a, ((0, rows - a.shape[0]), (0, 0)))


def local_step(x, mem, target, P, get_weights, put_grads):
    cw = _pad_rows(P["conv_a_w"], 32)
    scw = _pad_rows(P["sc_conv_w"], 8)
    cb, lg, lb = _row(P["conv_a_b"]), _row(P["conv_a_ln_g"]), _row(P["conv_a_ln_b"])
    sinks = P["swa_sinks"]

    class _Layered:
        def __init__(self, store, name=None):
            self.store, self.name = store, name

        def __getitem__(self, key):
            if self.name is None:
                return self.store[(key, 0)] if key in ("even_w_in", "even_w_out", "odd_w_in", "odd_w_out") \
                    else _Layered(self.store, key)
            return self.store[(self.name, key)]

    store = {}
    W = _Layered(store)
    saved = []
    h = x
    for i in range(2):
        L = f"l{i}"
        new, dep = get_weights("A" if i == 0 else "D", h)
        store.update(new)
        h, s_ffn1 = ffn_forward(h, P["ffn1_norm"][i:i + 1], W["ffn1_w_gu"][i], W["ffn1_w_down"][i], L + "_ffn1", dep=dep)
        h1 = h
        if i == 0:
            new, _ = get_weights("B", h)
            store.update(new)
            z, u2 = norm_matmul(h1, P["mix_norm"][i:i + 1], W["even_w_in"], F32, L + "_mix_in")
            a = conformer_conv_fwd(z, cw, cb, lg, lb, L + "_conv")
            kpad = jnp.pad(z[:, 1536:1664].astype(BF16), ((WINDOW, 0), (0, 0)))
            vpad = jnp.pad(z[:, 1664:1792].astype(BF16), ((WINDOW, 0), (0, 0)))
            o = swa_fwd(z, kpad, vpad, sinks, L + "_swa")
            m = jnp.concatenate([a, o], axis=-1)
            h = matmul_residual(m, W["even_w_out"], h1, L + "_mix_out")
            s_mix = (h1, u2, z, m, kpad, vpad)
        else:
            z, u2 = norm_matmul(h1, P["mix_norm"][i:i + 1], W["odd_w_in"], F32, L + "_mix_in")
            m = short_conv_fwd(z, scw, L + "_sconv")
            h = matmul_residual(m, W["odd_w_out"], h1, L + "_mix_out")
            s_mix = (h1, u2, z, m)
        h2 = h
        kv, umem = norm_matmul(mem, P["xa_mem_norm"][i:i + 1], W["xa_wkv"][i], BF16, L + "_xa_kv", tm=256)
        q, u3 = norm_matmul(h2, P["xa_norm"][i:i + 1], W["xa_wq"][i], BF16, L + "_xa_q", transposed=False)
        o = xattn_fwd(q, kv, L + "_xa")
        h = matmul_residual(o, W["xa_wo"][i], h2, L + "_xa_out")
        s_xa = (h2, u3, q, o, kv, umem)
        if i == 0:
            new, _ = get_weights("C", h)
            store.update(new)
        h, s_ffn2 = ffn_forward(h, P["ffn2_norm"][i:i + 1], W["ffn2_w_gu"][i], W["ffn2_w_down"][i], L + "_ffn2")
        saved.append((s_ffn1, s_mix, s_xa, s_ffn2))

    dh, d_final, loss = final_loss(h, _row(P["final_norm"]), target, "final_loss")

    names = ("ffn1_w_gu", "ffn1_w_down", "ffn2_w_gu", "ffn2_w_down", "xa_wq", "xa_wkv", "xa_wo", "even_w_in", "even_w_out",
             "odd_w_in", "odd_w_out")
    dW = {k: [None, None] for k in names}
    dP = {k: [None, None] for k in ("ffn1_norm", "mix_norm", "xa_norm", "xa_mem_norm", "ffn2_norm")}
    dP["final_norm"] = d_final.reshape(-1)
    for i in (1, 0):
        L = f"l{i}b"
        s_ffn1, s_mix, s_xa, s_ffn2 = saved[i]

        def keep_ffn2(d_w_gu, d_w_down, i=i):
            dW["ffn2_w_gu"][i], dW["ffn2_w_down"][i] = d_w_gu, d_w_down

        def send_ffn1(d_w_gu, d_w_down, i=i):
            dW["ffn1_w_gu"][i], dW["ffn1_w_down"][i] = d_w_gu, d_w_down
            stage = "D" if i == 1 else "A"
            return put_grads(stage, {k: dW[k[0]][k[1]] for k in STAGE_KEYS[stage]})

        dh, dP["ffn2_norm"][i] = ffn_backward(
            dh, s_ffn2, P["ffn2_norm"][i:i + 1], W["ffn2_w_gu"][i], W["ffn2_w_down"][i], L + "_ffn2", emit=keep_ffn2)
        h2, u3, q, o, kv, umem = s_xa
        dW["xa_wo"][i] = matmul_tn(o, dh, L + "_xa_dwo")
        do = matmul_nt(dh, W["xa_wo"][i], BF16, L + "_xa_do")
        dq, dkv = xattn_bwd(q, kv, do, L + "_xa")
        dW["xa_wq"][i] = matmul_tn(u3, dq, L + "_xa_dwq")
        dW["xa_wkv"][i] = matmul_tn(dkv, umem, L + "_xa_dwkv", tk=1024)
        dkv_b = dkv.astype(BF16)
        _, dP["xa_mem_norm"][i] = matmul_norm_bwd(dkv_b, W["xa_wkv"][i], mem, P["xa_mem_norm"][i:i + 1],
                                                  jnp.zeros_like(mem), L + "_xa_dmem")
        dh, dP["xa_norm"][i] = matmul_norm_bwd(dq, W["xa_wq"][i], h2, P["xa_norm"][i:i + 1], dh, L + "_xa_dx",
                                               transposed=False)
        if i == 0:
            h1, u2, z, m, kpad, vpad = s_mix
            dW["even_w_out"][0] = matmul_tn(m, dh, L + "_mix_dwo")
            dm = matmul_nt(dh, W["even_w_out"], F32, L + "_mix_dm")
            dz_conv, dcw, dcb, dlg, dlb = conformer_conv_bwd(z, dm, cw, cb, lg, lb, L + "_conv")
            dq_s, dkp, dvp, dsk = swa_bwd(z, kpad, vpad, sinks, dm, L + "_swa")
            dz = jnp.concatenate([dz_conv, dq_s, dkp[WINDOW:].astype(BF16), dvp[WINDOW:].astype(BF16)], axis=-1)
            dW["even_w_in"][0] = matmul_tn(dz, u2, L + "_mix_dwi", tk=896)
            dh, dP["mix_norm"][i] = matmul_norm_bwd(dz, W["even_w_in"], h1, P["mix_norm"][i:i + 1], dh, L + "_mix_dx")
            dP["conv_a_w"] = dcw[:CONV_A_WIDTH]
            dP["conv_a_b"], dP["conv_a_ln_g"], dP["conv_a_ln_b"] = dcb.reshape(-1), dlg.reshape(-1), dlb.reshape(-1)
            dP["swa_sinks"] = dsk[:, 0]
        else:
            h1, u2, z, m = s_mix
            dW["odd_w_out"][0] = matmul_tn(m, dh, L + "_mix_dwo")
            dm = matmul_nt(dh, W["odd_w_out"], F32, L + "_mix_dm")
            dz, dscw = short_conv_bwd(z, dm, scw, L + "_sconv")
            dW["odd_w_in"][0] = matmul_tn(dz, u2, L + "_mix_dwi", tk=1024)
            dh, dP["mix_norm"][i] = matmul_norm_bwd(dz, W["odd_w_in"], h1, P["mix_norm"][i:i + 1], dh, L + "_mix_dx")
            dP["sc_conv_w"] = dscw[:3]
        dep = put_grads("BC", {k: dW[k[0]][k[1]] for k in STAGE_KEYS["B"] + STAGE_KEYS["C"]}) if i == 0 else None
        dh, dP["ffn1_norm"][i] = ffn_backward(
            dh, s_ffn1, P["ffn1_norm"][i:i + 1], W["ffn1_w_gu"][i], W["ffn1_w_down"][i], L + "_ffn1", dep=dep,
            emit=send_ffn1)
    for k in ("ffn1_norm", "mix_norm", "xa_norm", "xa_mem_norm", "ffn2_norm"):
        dP[k] = jnp.concatenate(dP[k], axis=0)
    return loss, dh, dP


def _mesh_pos():
    return lax.axis_index("x"), lax.axis_index("y"), lax.axis_index("c")


def _flat_index(px, py, pc):
    return 4 * px + 2 * py + pc


def all_gather(blob, name):
    R, C = blob.shape

    def body(x_ref, out_ref, send_sems, recv_sems, local_sem):
        x, y, c = _mesh_pos()
        me, sibling = (x, y, c), (x, y, 1 - c)
        chips = [(1 - x, y), (x, 1 - y), (1 - x, 1 - y)]

        def slot(px, py, pc):
            return out_ref.at[_flat_index(px, py, pc)]

        def copy(k, block, to, src=None):
            return pltpu.make_async_remote_copy(
                src_ref=slot(*block) if src is None else src, dst_ref=slot(*block),
                send_sem=send_sems.at[k], recv_sem=recv_sems.at[k],
                device_id=to, device_id_type=pl.DeviceIdType.MESH)

        mine = pltpu.make_async_copy(x_ref, slot(*me), local_sem)
        mine.start()
        first = [copy(0, me, sibling, src=x_ref)]
        first += [copy(1 + j, me, (*chip, c), src=x_ref) for j, chip in enumerate(chips)]
        for cp in first:
            cp.start()
        passed = [copy(4 + j, (*chip, c), sibling) for j, chip in enumerate(chips)]
        for j, chip in enumerate(chips):
            copy(1 + j, (*chip, c), me).wait_recv()
            passed[j].start()
        copy(0, sibling, me).wait_recv()
        for j, chip in enumerate(chips):
            copy(4 + j, (*chip, 1 - c), me).wait_recv()
        for cp in first + passed:
            cp.wait_send()
        mine.wait()

    return pl.pallas_call(
        body, name=name,
        out_shape=jax.ShapeDtypeStruct((N_DEV, R, C), blob.dtype),
        in_specs=[pl.BlockSpec(memory_space=pl.ANY)],
        out_specs=pl.BlockSpec(memory_space=pl.ANY),
        scratch_shapes=[pltpu.SemaphoreType.DMA((7,)), pltpu.SemaphoreType.DMA((7,)), pltpu.SemaphoreType.DMA],
    )(blob)


def scatter_exchange(g, name):
    _, R, C = g.shape

    def body(g_ref, out_ref, send_sems, recv_sems, local_sem):
        x, y, c = _mesh_pos()
        me_idx = _flat_index(x, y, c)
        mine = pltpu.make_async_copy(g_ref.at[me_idx], out_ref.at[me_idx], local_sem)
        mine.start()
        sends, peers = [], []
        for k in range(1, N_DEV):
            px = 1 - x if k & 4 else x
            py = 1 - y if k & 2 else y
            pc = 1 - c if k & 1 else c
            peer_idx = _flat_index(px, py, pc)
            cp = pltpu.make_async_remote_copy(
                src_ref=g_ref.at[peer_idx], dst_ref=out_ref.at[me_idx],
                send_sem=send_sems.at[k - 1], recv_sem=recv_sems.at[k - 1],
                device_id=(px, py, pc), device_id_type=pl.DeviceIdType.MESH)
            cp.start()
            sends.append(cp)
            peers.append((peer_idx, (px, py, pc)))
        for k in range(1, N_DEV):
            peer_idx, peer = peers[k - 1]
            pltpu.make_async_remote_copy(
                src_ref=g_ref.at[me_idx], dst_ref=out_ref.at[peer_idx],
                send_sem=send_sems.at[k - 1], recv_sem=recv_sems.at[k - 1],
                device_id=peer, device_id_type=pl.DeviceIdType.MESH).wait_recv()
        for cp in sends:
            cp.wait_send()
        mine.wait()

    return pl.pallas_call(
        body, name=name,
        out_shape=jax.ShapeDtypeStruct(g.shape, g.dtype),
        in_specs=[pl.BlockSpec(memory_space=pl.ANY)],
        out_specs=pl.BlockSpec(memory_space=pl.ANY),
        scratch_shapes=[pltpu.SemaphoreType.DMA((7,)), pltpu.SemaphoreType.DMA((7,)), pltpu.SemaphoreType.DMA],
    )(g)


HBM_SPEC = pl.BlockSpec(memory_space=pltpu.HBM)
SEM_SPEC = pl.BlockSpec(memory_space=pltpu.SEMAPHORE)
DATAFLOW_EFFECT = pltpu.SideEffectType.DATAFLOW_SIDE_EFFECTING


def _peers(x, y, c):
    out = []
    for k in range(1, N_DEV):
        pos = (1 - x if k & 4 else x, 1 - y if k & 2 else y, 1 - c if k & 1 else c)
        out.append((_flat_index(*pos), pos))
    return out


def _exchange_copy(src_ref, land_ref, send_sems, recv_sems, j, me, peer_idx, peer, scatter):
    return pltpu.make_async_remote_copy(
        src_ref=src_ref.at[peer_idx] if scatter else src_ref, dst_ref=land_ref.at[me],
        send_sem=send_sems.at[j], recv_sem=recv_sems.at[j], device_id=peer, device_id_type=pl.DeviceIdType.MESH)


def exchange_start(srcs, lands, scatter, after, name):
    n = len(srcs)
    n_after = len(after)

    def body(*refs):
        src_refs, land_refs = refs[:n], refs[n:2 * n]
        outs = refs[2 * n + n_after:]
        send_sems, recv_sems, token = outs[:n], outs[n:2 * n], outs[4 * n]
        x, y, c = _mesh_pos()
        me = _flat_index(x, y, c)
        for g in range(n):
            for j, (peer_idx, peer) in enumerate(_peers(x, y, c)):
                _exchange_copy(src_refs[g], land_refs[g], send_sems[g], recv_sems[g], j, me, peer_idx, peer, scatter).start()
        token[...] = jnp.zeros_like(token)

    hbm = lambda a: pltpu.with_memory_space_constraint(a, pltpu.HBM)
    res = pl.pallas_call(
        body, name=name,
        out_shape=(*[pltpu.SemaphoreType.DMA((N_DEV - 1,))] * (2 * n),
                   *[pltpu.HBM(a.shape, a.dtype) for a in srcs], *[pltpu.HBM(a.shape, a.dtype) for a in lands],
                   jax.ShapeDtypeStruct((8, 128), F32)),
        in_specs=[HBM_SPEC] * (2 * n) + [ANY_SPEC] * n_after,
        out_specs=(*[SEM_SPEC] * (2 * n), *[HBM_SPEC] * (2 * n), pl.BlockSpec(memory_space=pltpu.VMEM)),
        input_output_aliases={i: 2 * n + i for i in range(2 * n)},
        compiler_params=pltpu.CompilerParams(has_side_effects=DATAFLOW_EFFECT),
    )(*[hbm(a) for a in srcs], *[hbm(a) for a in lands], *after)
    handles = [(res[g], res[n + g], res[2 * n + g], res[3 * n + g]) for g in range(n)]
    return handles, res[4 * n]


def exchange_wait(handle, scatter, after, name):
    send_sem, recv_sem, src_thru, land_thru = handle

    def body(src_ref, land_ref, send_sems, recv_sems, after_ref, src_dead, got_ref):
        x, y, c = _mesh_pos()
        me = _flat_index(x, y, c)
        for j, (peer_idx, peer) in enumerate(_peers(x, y, c)):
            mine = _exchange_copy(src_ref, land_ref, send_sems, recv_sems, j, me, peer_idx, peer, scatter)
            mine.wait_send()
            theirs = pltpu.make_async_remote_copy(
                src_ref=src_ref.at[me] if scatter else src_ref, dst_ref=land_ref.at[peer_idx],
                send_sem=send_sems.at[j], recv_sem=recv_sems.at[j], device_id=peer, device_id_type=pl.DeviceIdType.MESH)
            theirs.wait_recv()

    return pl.pallas_call(
        body, name=name,
        out_shape=(pltpu.HBM(src_thru.shape, src_thru.dtype), pltpu.HBM(land_thru.shape, land_thru.dtype)),
        in_specs=(HBM_SPEC, HBM_SPEC, SEM_SPEC, SEM_SPEC, ANY_SPEC), out_specs=(HBM_SPEC, HBM_SPEC),
        input_output_aliases={0: 0, 1: 1},
        compiler_params=pltpu.CompilerParams(has_side_effects=DATAFLOW_EFFECT),
    )(src_thru, land_thru, send_sem, recv_sem, after)


def ordered_sum(parts, name, tr=512):
    n, R, C = parts.shape
    tr = next((t for t in range(min(tr, R), 15, -16) if R % t == 0), R)

    def body(p_ref, o_ref):
        acc = p_ref[0].astype(F32)
        for j in range(1, n):
            acc = acc + p_ref[j].astype(F32)
        o_ref[...] = acc

    return pl.pallas_call(
        body, name=name, grid=(R // tr,),
        in_specs=[pl.BlockSpec((n, tr, C), lambda i: (0, i, 0))],
        out_specs=pl.BlockSpec((tr, C), lambda i: (i, 0)),
        out_shape=jax.ShapeDtypeStruct((R, C), F32),
        compiler_params=_params(1),
    )(parts)


def adamw(w, g, m, v, name, tr=256):
    R, C = w.shape
    tr = next((t for t in range(tr, 7, -8) if R % t == 0), R)
    c1 = 1.0 - ADAM_B1 ** ADAM_STEP
    c2 = 1.0 - ADAM_B2 ** ADAM_STEP

    def body(w_ref, g_ref, m_ref, v_ref, d_ref, mo_ref, vo_ref):
        grad = g_ref[...]
        m2 = ADAM_B1 * m_ref[...] + (1.0 - ADAM_B1) * grad
        v2 = ADAM_B2 * v_ref[...] + (1.0 - ADAM_B2) * (grad * grad)
        mo_ref[...] = m2
        vo_ref[...] = v2
        d_ref[...] = -ADAM_LR * ((m2 / c1) / (jnp.sqrt(v2 / c2) + ADAM_EPS) + ADAM_WD * w_ref[...])

    spec = pl.BlockSpec((tr, C), lambda i: (i, 0))
    return pl.pallas_call(
        body, name=name, grid=(R // tr,),
        in_specs=[spec] * 4, out_specs=[spec] * 3,
        out_shape=[jax.ShapeDtypeStruct((R, C), F32)] * 3,
        compiler_params=_params(1),
    )(w, g, m, v)


WEIGHT_NAMES = ("ffn1_norm", "ffn1_w_gu", "ffn1_w_down", "mix_norm", "even_w_in", "conv_a_w", "conv_a_b", "conv_a_ln_g",
                "conv_a_ln_b", "swa_sinks", "even_w_out", "odd_w_in", "sc_conv_w", "odd_w_out", "xa_norm", "xa_mem_norm",
                "xa_wq", "xa_wkv", "xa_wo", "ffn2_norm", "ffn2_w_gu", "ffn2_w_down", "final_norm")
BLOB_COLS = 1024
SMALL_ROWS = (("ffn1_norm", 0, 2), ("mix_norm", 2, 2), ("xa_norm", 4, 2), ("xa_mem_norm", 6, 2), ("ffn2_norm", 8, 2),
              ("final_norm", 10, 1))
ROW_CONV_B_LNG = 11
ROW_LNB_SINKS_LOSS = 12
LOSS_COL = 512 + SWA_HEADS
ROW_SC_CONV = 13
ROW_CONV_W = 16
SMALL_BLOB_ROWS = 32
SMALL_ADAM_ROWS = 16


def _small_blob(v):
    rows = [v[n].reshape(-1, D_MODEL) for n, _, _ in SMALL_ROWS]
    rows.append(jnp.concatenate([v["conv_a_b"].reshape(-1), v["conv_a_ln_g"].reshape(-1)]).reshape(1, D_MODEL))
    tail = jnp.zeros((D_MODEL - 512 - SWA_HEADS,), F32)
    if "loss" in v:
        tail = tail.at[0].set(v["loss"])
    rows.append(jnp.concatenate([v["conv_a_ln_b"].reshape(-1), v["swa_sinks"].reshape(-1), tail]).reshape(1, D_MODEL))
    rows.append(jnp.zeros((SMALL_ADAM_ROWS - ROW_SC_CONV, D_MODEL), F32))
    return jnp.concatenate(rows, axis=0)


def _small_unblob(b, shapes):
    out = {n: b[r:r + k].reshape(shapes[n]) for n, r, k in SMALL_ROWS}
    out["conv_a_b"] = b[ROW_CONV_B_LNG, :512].reshape(shapes["conv_a_b"])
    out["conv_a_ln_g"] = b[ROW_CONV_B_LNG, 512:].reshape(shapes["conv_a_ln_g"])
    out["conv_a_ln_b"] = b[ROW_LNB_SINKS_LOSS, :512].reshape(shapes["conv_a_ln_b"])
    out["swa_sinks"] = b[ROW_LNB_SINKS_LOSS, 512:512 + SWA_HEADS].reshape(shapes["swa_sinks"])
    return out


def kernel(x, mem, ffn1_norm, ffn1_w_gu, ffn1_w_down, mix_norm, even_w_in, conv_a_w, conv_a_b, conv_a_ln_g, conv_a_ln_b, swa_sinks, even_w_out, odd_w_in, sc_conv_w, odd_w_out, xa_norm, xa_mem_norm, xa_wq, xa_wkv, xa_wo, ffn2_norm, ffn2_w_gu, ffn2_w_down, final_norm, loss_target, m_ffn1_norm, m_ffn1_w_gu, m_ffn1_w_down, m_mix_norm, m_even_w_in, m_conv_a_w, m_conv_a_b, m_conv_a_ln_g, m_conv_a_ln_b, m_swa_sinks, m_even_w_out, m_odd_w_in, m_sc_conv_w, m_odd_w_out, m_xa_norm, m_xa_mem_norm, m_xa_wq, m_xa_wkv, m_xa_wo, m_ffn2_norm, m_ffn2_w_gu, m_ffn2_w_down, m_final_norm, v_ffn1_norm, v_ffn1_w_gu, v_ffn1_w_down, v_mix_norm, v_even_w_in, v_conv_a_w, v_conv_a_b, v_conv_a_ln_g, v_conv_a_ln_b, v_swa_sinks, v_even_w_out, v_odd_w_in, v_sc_conv_w, v_odd_w_out, v_xa_norm, v_xa_mem_norm, v_xa_wq, v_xa_wkv, v_xa_wo, v_ffn2_norm, v_ffn2_w_gu, v_ffn2_w_down, v_final_norm):
    w = dict(ffn1_norm=ffn1_norm, ffn1_w_gu=ffn1_w_gu, ffn1_w_down=ffn1_w_down, mix_norm=mix_norm, even_w_in=even_w_in,
             conv_a_w=conv_a_w, conv_a_b=conv_a_b, conv_a_ln_g=conv_a_ln_g, conv_a_ln_b=conv_a_ln_b, swa_sinks=swa_sinks,
             even_w_out=even_w_out, odd_w_in=odd_w_in, sc_conv_w=sc_conv_w, odd_w_out=odd_w_out, xa_norm=xa_norm,
             xa_mem_norm=xa_mem_norm, xa_wq=xa_wq, xa_wkv=xa_wkv, xa_wo=xa_wo, ffn2_norm=ffn2_norm, ffn2_w_gu=ffn2_w_gu,
             ffn2_w_down=ffn2_w_down, final_norm=final_norm)
    m = dict(ffn1_norm=m_ffn1_norm, ffn1_w_gu=m_ffn1_w_gu, ffn1_w_down=m_ffn1_w_down, mix_norm=m_mix_norm,
             even_w_in=m_even_w_in, conv_a_w=m_conv_a_w, conv_a_b=m_conv_a_b, conv_a_ln_g=m_conv_a_ln_g,
             conv_a_ln_b=m_conv_a_ln_b, swa_sinks=m_swa_sinks, even_w_out=m_even_w_out, odd_w_in=m_odd_w_in,
             sc_conv_w=m_sc_conv_w, odd_w_out=m_odd_w_out, xa_norm=m_xa_norm, xa_mem_norm=m_xa_mem_norm, xa_wq=m_xa_wq,
             xa_wkv=m_xa_wkv, xa_wo=m_xa_wo, ffn2_norm=m_ffn2_norm, ffn2_w_gu=m_ffn2_w_gu, ffn2_w_down=m_ffn2_w_down,
             final_norm=m_final_norm)
    v = dict(ffn1_norm=v_ffn1_norm, ffn1_w_gu=v_ffn1_w_gu, ffn1_w_down=v_ffn1_w_down, mix_norm=v_mix_norm,
             even_w_in=v_even_w_in, conv_a_w=v_conv_a_w, conv_a_b=v_conv_a_b, conv_a_ln_g=v_conv_a_ln_g,
             conv_a_ln_b=v_conv_a_ln_b, swa_sinks=v_swa_sinks, even_w_out=v_even_w_out, odd_w_in=v_odd_w_in,
             sc_conv_w=v_sc_conv_w, odd_w_out=v_odd_w_out, xa_norm=v_xa_norm, xa_mem_norm=v_xa_mem_norm, xa_wq=v_xa_wq,
             xa_wkv=v_xa_wkv, xa_wo=v_xa_wo, ffn2_norm=v_ffn2_norm, ffn2_w_gu=v_ffn2_w_gu, ffn2_w_down=v_ffn2_w_down,
             final_norm=v_final_norm)
    me = _flat_index(*_mesh_pos())

    conv_blob = jnp.concatenate([w["conv_a_w"].reshape(-1), w["sc_conv_w"].reshape(-1),
                                 jnp.zeros((8 * 1024 - 31 * 64 - 3 * 128,), F32)]).reshape(8, 1024)
    conv_all = all_gather(conv_blob, "gather_conv_weights").reshape(N_DEV, 8 * 1024)
    conv_a_full = jnp.transpose(conv_all[:, :31 * 64].reshape(N_DEV, 31, 64), (1, 0, 2)).reshape(31, 512)
    sc_full = jnp.transpose(conv_all[:, 31 * 64:31 * 64 + 3 * 128].reshape(N_DEV, 3, 128), (1, 0, 2)).reshape(3, 1024)

    def stage_blob(keys):
        shards = [w[n][l].T if SPLIT_AXIS[n] == 1 else w[n][l] for n, l in keys]
        return jnp.concatenate([s.astype(BF16) for s in shards], axis=0)

    def stage_rows(keys):
        return [(n, l, w[n].shape[2] if SPLIT_AXIS[n] == 1 else w[n].shape[1]) for n, l in keys]

    def unpack_weights(gathered, keys):
        out, off = {}, 0
        for n, l, rows in stage_rows(keys):
            out[(n, l)] = gathered[:, off:off + rows, :].reshape(N_DEV * rows, BLOB_COLS)
            off += rows
        return out

    def with_own(land, own):
        return lax.dynamic_update_slice(land, own[None], (me, 0, 0))

    gathered_a = all_gather(stage_blob(STAGE_KEYS["A"]), "gather_weights_a")
    later = ("B", "C", "D")
    blobs = [stage_blob(STAGE_KEYS[s]) for s in later]
    lands = [lax.empty((N_DEV,) + b.shape, BF16) for b in blobs]
    weight_handles, weight_token = exchange_start(blobs, lands, False, [gathered_a, conv_all], "gather_start")

    def get_weights(stage, after):
        if stage == "A":
            return unpack_weights(gathered_a, STAGE_KEYS["A"]), weight_token
        own, land = exchange_wait(weight_handles[later.index(stage)], False, after, "gather_wait_" + stage.lower())
        return unpack_weights(with_own(land, own), STAGE_KEYS[stage]), None

    grad_handles = {}

    def put_grads(stage, dws):
        packed = jnp.concatenate([dw.reshape(N_DEV, -1, BLOB_COLS) for dw in dws.values()], axis=1)
        land = lax.empty(packed.shape, BF16)
        (handle,), token = exchange_start([packed], [land], True, [], "scatter_start_" + stage.lower())
        grad_handles[stage] = (handle, tuple(dws))
        return token

    P = dict(ffn1_norm=ffn1_norm, mix_norm=mix_norm, xa_norm=xa_norm, xa_mem_norm=xa_mem_norm, ffn2_norm=ffn2_norm,
             final_norm=final_norm, conv_a_w=conv_a_full, conv_a_b=conv_a_b[0], conv_a_ln_g=conv_a_ln_g[0],
             conv_a_ln_b=conv_a_ln_b[0], swa_sinks=swa_sinks[0], sc_conv_w=sc_full)

    loss_part, grad_x, dP = local_step(x[0], mem[0], loss_target[0], P, get_weights, put_grads)

    def finish_grads(stage, after):
        handle, keys = grad_handles[stage]
        packed, land = exchange_wait(handle, True, after, "scatter_wait_" + stage.lower())
        own = lax.dynamic_slice(packed, (me, 0, 0), (1,) + packed.shape[1:])[0]
        rows_f32 = ordered_sum(with_own(land, own), "sum_grads_" + stage.lower())
        out, off = {}, 0
        for n, l, rows in stage_rows(keys):
            part = rows_f32[off:off + rows]
            out[(n, l)] = part.T if SPLIT_AXIS[n] == 1 else part
            off += rows
        return out

    layer_grads = {**finish_grads("D", grad_x), **finish_grads("BC", grad_x)}

    dP = dict(dP, loss=loss_part[0, 0])
    small = jnp.concatenate([
        _small_blob(dP)[:ROW_SC_CONV], dP["sc_conv_w"],
        jnp.concatenate([dP["conv_a_w"].reshape(-1), jnp.zeros((512,), F32)]).reshape(16, D_MODEL)], axis=0)
    small_sum = ordered_sum(all_gather(small, "gather_small_grads"), "sum_small_grads", tr=SMALL_BLOB_ROWS)
    loss = small_sum[ROW_LNB_SINKS_LOSS, LOSS_COL]
    grads = _small_unblob(small_sum, {n: w[n].shape for n in WEIGHT_NAMES})
    sc_g = small_sum[ROW_SC_CONV:ROW_SC_CONV + 3]
    grads["sc_conv_w"] = lax.dynamic_slice(sc_g, (0, me * 128), (3, 128)).reshape(w["sc_conv_w"].shape)
    cw_g = small_sum[ROW_CONV_W:].reshape(-1)[:31 * 512].reshape(31, 512)
    grads["conv_a_w"] = lax.dynamic_slice(cw_g, (0, me * 64), (31, 64)).reshape(w["conv_a_w"].shape)

    delta, new_m, new_v = {}, {}, {}

    def update(n):
        shp = w[n].shape
        two_d = (shp[0] * shp[1], shp[2])
        d_, m_, v_ = adamw(w[n].reshape(two_d), grads[n].reshape(two_d), m[n].reshape(two_d), v[n].reshape(two_d),
                           "adamw_" + n)
        delta[n], new_m[n], new_v[n] = d_.reshape(shp), m_.reshape(shp), v_.reshape(shp)

    first_stage = tuple(n for n, _ in STAGE_KEYS["A"])
    for n in SPLIT_AXIS:
        if n not in first_stage:
            grads[n] = jnp.stack([layer_grads[(n, l)] for l in range(w[n].shape[0])], axis=0)
            update(n)
    update("conv_a_w")
    update("sc_conv_w")
    layer_grads.update(finish_grads("A", delta["ffn2_w_gu"]))
    for n in first_stage:
        grads[n] = jnp.stack([layer_grads[(n, l)] for l in range(w[n].shape[0])], axis=0)
        update(n)
    d_, m_, v_ = adamw(_small_blob(w), small_sum[:SMALL_ADAM_ROWS], _small_blob(m), _small_blob(v), "adamw_small",
                       tr=SMALL_ADAM_ROWS)
    shapes = {n: w[n].shape for n in WEIGHT_NAMES}
    delta.update(_small_unblob(d_, shapes))
    new_m.update(_small_unblob(m_, shapes))
    new_v.update(_small_unblob(v_, shapes))

    return (loss, grad_x[None], *[grads[n] for n in WEIGHT_NAMES], *[delta[n] for n in WEIGHT_NAMES],
            *[new_m[n] for n in WEIGHT_NAMES], *[new_v[n] for n in WEIGHT_NAMES])
```

```python
import functools

import jax
import jax.numpy as jnp
from jax import lax
from jax.experimental import pallas as pl
from jax.experimental.pallas import tpu as pltpu

F32 = jnp.float32
BF16 = jnp.bfloat16

D_MODEL = 1024
D_FF = 2816
CONV_A_CH = 512
CONV_A_WIDTH = 31
SWA_HEADS = 8
SWA_KV_HEADS = 2
SWA_GROUP = SWA_HEADS // SWA_KV_HEADS
HEAD_DIM = 64
WINDOW = 128
SC_CH = 1024
XA_HEADS = 4
XA_HEAD_DIM = D_MODEL // XA_HEADS
RMS_EPS = 1e-6
LN_EPS = 1e-5
ADAM_LR = 0.001
ADAM_B1 = 0.9
ADAM_B2 = 0.999
ADAM_EPS = 1e-08
ADAM_WD = 0.01
ADAM_STEP = 10
N_DEV = 8

V7X_VMEM_BYTES = 64 * 1024 * 1024
VMEM_LIMIT = V7X_VMEM_BYTES - 8 * 1024 * 1024
CONV_HALO = 32
SC_HALO = 8
NEG_BIG = -1e30

SPLIT_AXIS = dict(ffn1_w_gu=1, ffn1_w_down=0, even_w_in=1, even_w_out=0, odd_w_in=1, odd_w_out=0, xa_wq=0, xa_wkv=1, xa_wo=0,
                  ffn2_w_gu=1, ffn2_w_down=0)
STAGE_KEYS = dict(
    A=(("ffn1_w_gu", 0), ("ffn1_w_down", 0)),
    B=(("even_w_in", 0), ("even_w_out", 0), ("xa_wq", 0), ("xa_wkv", 0), ("xa_wo", 0)),
    C=(("ffn2_w_gu", 0), ("ffn2_w_down", 0)),
    D=(("ffn1_w_gu", 1), ("ffn1_w_down", 1), ("odd_w_in", 0), ("odd_w_out", 0), ("xa_wq", 1), ("xa_wkv", 1), ("xa_wo", 1),
       ("ffn2_w_gu", 1), ("ffn2_w_down", 1)))


def _params(n_axes):
    return pltpu.CompilerParams(dimension_semantics=("arbitrary",) * n_axes, vmem_limit_bytes=VMEM_LIMIT)


def _tile(n, pref):
    t = min(n, pref)
    assert n % t == 0, (n, pref)
    return t


def _dot(a, b):
    return jnp.dot(a, b, preferred_element_type=F32)


def _dot_nt(a, b):
    return lax.dot_general(a, b, (((1,), (1,)), ((), ())), preferred_element_type=F32)


def _dot_tn(a, b):
    return lax.dot_general(a, b, (((0,), (0,)), ((), ())), preferred_element_type=F32)


def _sigmoid(x):
    return 1.0 / (1.0 + jnp.exp(-x))


ANY_SPEC = pl.BlockSpec(memory_space=pl.ANY)


def _with_dep(body, n_in, dep):
    if dep is None:
        return body, [], []
    return (lambda *refs: body(*refs[:n_in], *refs[n_in + 1:])), [ANY_SPEC], [dep]


ROW_CHUNK = 128


def _row_chunks(tm):
    ch = min(ROW_CHUNK, tm)
    return [slice(s, s + ch) for s in range(0, tm, ch)]


def norm_matmul(h, g, w, out_dtype, name, tm=512, dep=None, transposed=True):
    T, K = h.shape
    N = w.shape[0] if transposed else w.shape[1]
    tm = _tile(T, tm)
    mm = _dot_nt if transposed else _dot

    def kern(h_ref, g_ref, w_ref, z_ref, u_ref):
        for rows in _row_chunks(tm):
            x = h_ref[rows, :]
            r = lax.rsqrt(jnp.mean(x * x, axis=-1, keepdims=True) + RMS_EPS)
            u = ((x * r) * g_ref[...]).astype(BF16)
            u_ref[rows, :] = u
            z_ref[rows, :] = mm(u, w_ref[...]).astype(z_ref.dtype)

    body, dep_spec, dep_arg = _with_dep(kern, 3, dep)
    return pl.pallas_call(
        body, name=name, grid=(T // tm,),
        in_specs=[pl.BlockSpec((tm, K), lambda i: (i, 0)),
                  pl.BlockSpec((1, K), lambda i: (0, 0)),
                  pl.BlockSpec(w.shape, lambda i: (0, 0))] + dep_spec,
        out_specs=[pl.BlockSpec((tm, N), lambda i: (i, 0)),
                   pl.BlockSpec((tm, K), lambda i: (i, 0))],
        out_shape=[jax.ShapeDtypeStruct((T, N), out_dtype), jax.ShapeDtypeStruct((T, K), BF16)],
        compiler_params=_params(1),
    )(h, g, w, *dep_arg)


def matmul_residual(a, w, res, name, tm=512):
    T, K = a.shape
    N = w.shape[1]
    tm = _tile(T, tm)

    def body(a_ref, w_ref, r_ref, o_ref):
        o_ref[...] = r_ref[...] + _dot(a_ref[...], w_ref[...])

    return pl.pallas_call(
        body, name=name, grid=(T // tm,),
        in_specs=[pl.BlockSpec((tm, K), lambda i: (i, 0)),
                  pl.BlockSpec((K, N), lambda i: (0, 0)),
                  pl.BlockSpec((tm, N), lambda i: (i, 0))],
        out_specs=pl.BlockSpec((tm, N), lambda i: (i, 0)),
        out_shape=jax.ShapeDtypeStruct((T, N), F32),
        compiler_params=_params(1),
    )(a, w, res)


def matmul_nt(dy, w, out_dtype, name, tm=512):
    T, N = dy.shape
    K = w.shape[0]
    tm = _tile(T, tm)

    def body(dy_ref, w_ref, o_ref):
        o_ref[...] = _dot_nt(dy_ref[...].astype(BF16), w_ref[...]).astype(o_ref.dtype)

    return pl.pallas_call(
        body, name=name, grid=(T // tm,),
        in_specs=[pl.BlockSpec((tm, N), lambda i: (i, 0)),
                  pl.BlockSpec((K, N), lambda i: (0, 0))],
        out_specs=pl.BlockSpec((tm, K), lambda i: (i, 0)),
        out_shape=jax.ShapeDtypeStruct((T, K), out_dtype),
        compiler_params=_params(1),
    )(dy, w)


def matmul_norm_bwd(dz, w, h, g, dh_in, name, tm=256, transposed=True, dep=None):
    T, N = dz.shape
    K = h.shape[1]
    tm = _tile(T, tm)

    def kern(dz_ref, w_ref, h_ref, g_ref, dhin_ref, dh_ref, dg_ref):
        @pl.when(pl.program_id(0) == 0)
        def _():
            dg_ref[...] = jnp.zeros_like(dg_ref)

        mm = _dot if transposed else _dot_nt
        dg = jnp.zeros((1, K), F32)
        for rows in _row_chunks(tm):
            du = mm(dz_ref[rows, :], w_ref[...])
            x = h_ref[rows, :]
            r = lax.rsqrt(jnp.mean(x * x, axis=-1, keepdims=True) + RMS_EPS)
            xh = x * r
            dg = dg + jnp.sum(du * xh, axis=0, keepdims=True)
            dxh = du * g_ref[...]
            dh_ref[rows, :] = dhin_ref[rows, :] + r * (dxh - xh * jnp.mean(dxh * xh, axis=-1, keepdims=True))
        dg_ref[...] += dg

    body, dep_spec, dep_arg = _with_dep(kern, 5, dep)
    return pl.pallas_call(
        body, name=name, grid=(T // tm,),
        in_specs=[pl.BlockSpec((tm, N), lambda i: (i, 0)),
                  pl.BlockSpec(w.shape, lambda i: (0, 0)),
                  pl.BlockSpec((tm, K), lambda i: (i, 0)),
                  pl.BlockSpec((1, K), lambda i: (0, 0)),
                  pl.BlockSpec((tm, K), lambda i: (i, 0))] + dep_spec,
        out_specs=[pl.BlockSpec((tm, K), lambda i: (i, 0)),
                   pl.BlockSpec((1, K), lambda i: (0, 0))],
        out_shape=[jax.ShapeDtypeStruct((T, K), F32), jax.ShapeDtypeStruct((1, K), F32)],
        compiler_params=_params(1),
    )(dz, w, h, g, dh_in, *dep_arg)


def matmul_tn(x, dy, name, scale=1.0, tk=None, tn=None, tt=512):
    T, K = x.shape
    N = dy.shape[1]
    tk = K if tk is None else tk
    tn = N if tn is None else tn
    tt = _tile(T, tt)
    nt = T // tt

    def body(x_ref, dy_ref, o_ref, acc_ref):
        t = pl.program_id(2)

        @pl.when(t == 0)
        def _():
            acc_ref[...] = jnp.zeros_like(acc_ref)

        acc_ref[...] += _dot_tn(x_ref[...].astype(BF16), dy_ref[...].astype(BF16))

        @pl.when(t == nt - 1)
        def _():
            o_ref[...] = (acc_ref[...] * scale).astype(o_ref.dtype)

    return pl.pallas_call(
        body, name=name, grid=(K // tk, N // tn, nt),
        in_specs=[pl.BlockSpec((tt, tk), lambda a, b, t: (t, a)),
                  pl.BlockSpec((tt, tn), lambda a, b, t: (t, b))],
        out_specs=pl.BlockSpec((tk, tn), lambda a, b, t: (a, b)),
        out_shape=jax.ShapeDtypeStruct((K, N), BF16),
        scratch_shapes=[pltpu.VMEM((tk, tn), F32)],
        compiler_params=_params(3),
    )(x, dy)


def ffn_down(gu, wd, res, name, tm=512):
    T = gu.shape[0]
    F = gu.shape[1] // 2
    N = wd.shape[1]
    tm = _tile(T, tm)

    def body(g_ref, up_ref, w_ref, r_ref, o_ref, a_ref):
        for rows in _row_chunks(tm):
            g = g_ref[rows, :].astype(F32)
            a = ((g * _sigmoid(g)) * up_ref[rows, :].astype(F32)).astype(BF16)
            a_ref[rows, :] = a
            o_ref[rows, :] = r_ref[rows, :] + 0.5 * _dot(a, w_ref[...])

    return pl.pallas_call(
        body, name=name, grid=(T // tm,),
        in_specs=[pl.BlockSpec((tm, F), lambda i: (i, 0)),
                  pl.BlockSpec((tm, F), lambda i: (i, 1)),
                  pl.BlockSpec((F, N), lambda i: (0, 0)),
                  pl.BlockSpec((tm, N), lambda i: (i, 0))],
        out_specs=[pl.BlockSpec((tm, N), lambda i: (i, 0)),
                   pl.BlockSpec((tm, F), lambda i: (i, 0))],
        out_shape=[jax.ShapeDtypeStruct((T, N), F32), jax.ShapeDtypeStruct((T, F), BF16)],
        compiler_params=_params(1),
    )(gu, gu, wd, res)


def ffn_down_bwd(dy, wd, gu, name, tm=512, dep=None):
    T, N = dy.shape
    F = wd.shape[0]
    tm = _tile(T, tm)

    def kern(dy_ref, w_ref, g_ref, up_ref, o_ref):
        for rows in _row_chunks(tm):
            da = 0.5 * _dot_nt(dy_ref[rows, :].astype(BF16), w_ref[...])
            g = g_ref[rows, :].astype(F32)
            up = up_ref[rows, :].astype(F32)
            s = _sigmoid(g)
            o_ref[rows, :F] = (da * up * (s * (1.0 + g * (1.0 - s)))).astype(BF16)
            o_ref[rows, F:] = (da * (g * s)).astype(BF16)

    body, dep_spec, dep_arg = _with_dep(kern, 4, dep)
    return pl.pallas_call(
        body, name=name, grid=(T // tm,),
        in_specs=[pl.BlockSpec((tm, N), lambda i: (i, 0)),
                  pl.BlockSpec((F, N), lambda i: (0, 0)),
                  pl.BlockSpec((tm, F), lambda i: (i, 0)),
                  pl.BlockSpec((tm, F), lambda i: (i, 1))] + dep_spec,
        out_specs=pl.BlockSpec((tm, 2 * F), lambda i: (i, 0)),
        out_shape=jax.ShapeDtypeStruct((T, 2 * F), BF16),
        compiler_params=_params(1),
    )(dy, wd, gu, gu, *dep_arg)


def ffn_forward(h, g, w_gu, w_down, name, dep=None):
    gu, u = norm_matmul(h, g, w_gu, BF16, name + "_gu", dep=dep)
    h_out, a = ffn_down(gu, w_down, h, name + "_down")
    return h_out, (h, u, gu, a)


def ffn_backward(dy, saved, g, w_gu, w_down, name, dep=None, emit=None):
    h, u, gu, a = saved
    dgu = ffn_down_bwd(dy, w_down, gu, name + "_ddown", dep=dep)
    d_w_down = matmul_tn(a, dy, name + "_dwd", scale=0.5, tk=D_FF // 2)
    d_w_gu = matmul_tn(dgu, u, name + "_dwgu", tk=D_FF)
    dh, dg = matmul_norm_bwd(dgu, w_gu, h, g, dy, name + "_dx", dep=emit(d_w_gu, d_w_down))
    return dh, dg


def _shifted_taps(ref, phase_ref, n_rows):
    for r in range(1, 8):
        phase_ref[r - 1] = ref[pl.ds(r, n_rows - 8), :]

    def taps(o, n):
        q, r = divmod(o, 8)
        return ref[8 * q:8 * q + n, :] if r == 0 else phase_ref[r - 1, 8 * q:8 * q + n, :]

    return taps


def conformer_conv_fwd(z, cw, cb, lg, lb, name, tm=512):
    T = z.shape[0]
    C = CONV_A_CH
    tm = _tile(T, tm)
    hb = tm // CONV_HALO

    def body(v_ref, gt_ref, pv_ref, pg_ref, cw_ref, cb_ref, lg_ref, lb_ref, o_ref, xs_ref, xph_ref):
        i = pl.program_id(0)
        prev = pv_ref[...] * _sigmoid(pg_ref[...])
        xs_ref[0:CONV_HALO, :] = jnp.where(i > 0, prev, 0.0)
        xs_ref[CONV_HALO:, :] = v_ref[...] * _sigmoid(gt_ref[...])
        taps = _shifted_taps(xs_ref, xph_ref, tm + CONV_HALO)
        acc = jnp.zeros((tm, C), F32) + cb_ref[...]
        for k in range(CONV_A_WIDTH):
            acc = acc + cw_ref[k:k + 1, :] * taps(CONV_HALO - (CONV_A_WIDTH - 1) + k, tm)
        mu = jnp.mean(acc, axis=-1, keepdims=True)
        xc = acc - mu
        var = jnp.mean(xc * xc, axis=-1, keepdims=True)
        y = (xc * lax.rsqrt(var + LN_EPS)) * lg_ref[...] + lb_ref[...]
        o_ref[...] = (y * _sigmoid(y)).astype(BF16)

    return pl.pallas_call(
        body, name=name, grid=(T // tm,),
        in_specs=[pl.BlockSpec((tm, C), lambda i: (i, 0)),
                  pl.BlockSpec((tm, C), lambda i: (i, 1)),
                  pl.BlockSpec((CONV_HALO, C), lambda i: (jnp.maximum(i * hb - 1, 0), 0)),
                  pl.BlockSpec((CONV_HALO, C), lambda i: (jnp.maximum(i * hb - 1, 0), 1)),
                  pl.BlockSpec((32, C), lambda i: (0, 0)),
                  pl.BlockSpec((1, C), lambda i: (0, 0)),
                  pl.BlockSpec((1, C), lambda i: (0, 0)),
                  pl.BlockSpec((1, C), lambda i: (0, 0))],
        out_specs=pl.BlockSpec((tm, C), lambda i: (i, 0)),
        out_shape=jax.ShapeDtypeStruct((T, C), BF16),
        scratch_shapes=[pltpu.VMEM((tm + CONV_HALO, C), F32), pltpu.VMEM((7, tm + CONV_HALO - 8, C), F32)],
        compiler_params=_params(1),
    )(z, z, z, z, cw, cb, lg, lb)


def conformer_conv_bwd(z, dm, cw, cb, lg, lb, name, tm=512):
    T = z.shape[0]
    C = CONV_A_CH
    tm = _tile(T, tm)
    hb = tm // CONV_HALO
    n_tiles = T // tm
    last_halo = T // CONV_HALO - 1
    R = tm + CONV_HALO
    KW = CONV_A_WIDTH

    def body(v_ref, gt_ref, pv_ref, pg_ref, nv_ref, ng_ref, do_ref, ndo_ref, cw_ref, cb_ref, lg_ref, lb_ref,
             dz_ref, dcw_ref, dcb_ref, dlg_ref, dlb_ref, xs_ref, ds_ref, xph_ref, dph_ref):
        i = pl.program_id(0)

        @pl.when(i == 0)
        def _():
            dcw_ref[...] = jnp.zeros_like(dcw_ref)
            dcb_ref[...] = jnp.zeros_like(dcb_ref)
            dlg_ref[...] = jnp.zeros_like(dlg_ref)
            dlb_ref[...] = jnp.zeros_like(dlb_ref)

        val = v_ref[...]
        sg = _sigmoid(gt_ref[...])
        prev = pv_ref[...] * _sigmoid(pg_ref[...])
        xs_ref[0:CONV_HALO, :] = jnp.where(i > 0, prev, 0.0)
        xs_ref[CONV_HALO:CONV_HALO + tm, :] = val * sg
        xs_ref[CONV_HALO + tm:, :] = nv_ref[...] * _sigmoid(ng_ref[...])

        x_taps = _shifted_taps(xs_ref, xph_ref, tm + 2 * CONV_HALO)
        acc = jnp.zeros((R, C), F32) + cb_ref[...]
        for k in range(KW):
            acc = acc + cw_ref[k:k + 1, :] * x_taps(CONV_HALO - (KW - 1) + k, R)
        mu = jnp.mean(acc, axis=-1, keepdims=True)
        xc = acc - mu
        rstd = lax.rsqrt(jnp.mean(xc * xc, axis=-1, keepdims=True) + LN_EPS)
        xh = xc * rstd
        y = xh * lg_ref[...] + lb_ref[...]
        s = _sigmoid(y)
        dout = jnp.concatenate([do_ref[...], jnp.where(i < n_tiles - 1, ndo_ref[...], 0.0)], axis=0)
        dy = dout * (s * (1.0 + y * (1.0 - s)))
        dxh = dy * lg_ref[...]
        dconv = rstd * (dxh - jnp.mean(dxh, axis=-1, keepdims=True) - xh * jnp.mean(dxh * xh, axis=-1, keepdims=True))
        ds_ref[...] = dconv

        dy_m = dy[:tm]
        dlg_ref[...] += jnp.sum(dy_m * xh[:tm], axis=0, keepdims=True)
        dlb_ref[...] += jnp.sum(dy_m, axis=0, keepdims=True)
        dc_m = dconv[:tm]
        dcb_ref[...] += jnp.sum(dc_m, axis=0, keepdims=True)
        d_taps = _shifted_taps(ds_ref, dph_ref, R)
        dglu = jnp.zeros((tm, C), F32)
        for k in range(KW):
            dcw_ref[k:k + 1, :] += jnp.sum(dc_m * x_taps(CONV_HALO - (KW - 1) + k, tm), axis=0, keepdims=True)
            dglu = dglu + cw_ref[k:k + 1, :] * d_taps(KW - 1 - k, tm)
        dz_ref[:, :C] = (dglu * sg).astype(BF16)
        dz_ref[:, C:] = (dglu * val * sg * (1.0 - sg)).astype(BF16)

    prev_map = lambda i: jnp.maximum(i * hb - 1, 0)
    next_map = lambda i: jnp.minimum((i + 1) * hb, last_halo)
    return pl.pallas_call(
        body, name=name, grid=(n_tiles,),
        in_specs=[pl.BlockSpec((tm, C), lambda i: (i, 0)),
                  pl.BlockSpec((tm, C), lambda i: (i, 1)),
                  pl.BlockSpec((CONV_HALO, C), lambda i: (prev_map(i), 0)),
                  pl.BlockSpec((CONV_HALO, C), lambda i: (prev_map(i), 1)),
                  pl.BlockSpec((CONV_HALO, C), lambda i: (next_map(i), 0)),
                  pl.BlockSpec((CONV_HALO, C), lambda i: (next_map(i), 1)),
                  pl.BlockSpec((tm, C), lambda i: (i, 0)),
                  pl.BlockSpec((CONV_HALO, C), lambda i: (next_map(i), 0)),
                  pl.BlockSpec((32, C), lambda i: (0, 0)),
                  pl.BlockSpec((1, C), lambda i: (0, 0)),
                  pl.BlockSpec((1, C), lambda i: (0, 0)),
                  pl.BlockSpec((1, C), lambda i: (0, 0))],
        out_specs=[pl.BlockSpec((tm, 2 * C), lambda i: (i, 0)),
                   pl.BlockSpec((32, C), lambda i: (0, 0)),
                   pl.BlockSpec((1, C), lambda i: (0, 0)),
                   pl.BlockSpec((1, C), lambda i: (0, 0)),
                   pl.BlockSpec((1, C), lambda i: (0, 0))],
        out_shape=[jax.ShapeDtypeStruct((T, 2 * C), BF16),
                   jax.ShapeDtypeStruct((32, C), F32),
                   jax.ShapeDtypeStruct((1, C), F32),
                   jax.ShapeDtypeStruct((1, C), F32),
                   jax.ShapeDtypeStruct((1, C), F32)],
        scratch_shapes=[pltpu.VMEM((tm + 2 * CONV_HALO, C), F32), pltpu.VMEM((R, C), F32),
                        pltpu.VMEM((7, tm + 2 * CONV_HALO - 8, C), F32), pltpu.VMEM((7, R - 8, C), F32)],
        compiler_params=_params(1),
    )(z, z, z, z, z, z, dm, dm, cw, cb, lg, lb)


def _swa_scores(q_h, kk_h, slope, bias_dist, valid, sink):
    s = _dot_nt(q_h, kk_h) * (HEAD_DIM ** -0.5) - slope * bias_dist
    s = jnp.where(valid, s, NEG_BIG)
    m = jnp.maximum(jnp.max(s, axis=-1, keepdims=True), sink)
    p = jnp.exp(s - m)
    e_sink = jnp.exp(sink - m)
    inv = 1.0 / (jnp.sum(p, axis=-1, keepdims=True) + e_sink)
    return p * inv, e_sink * inv


def _swa_mask(r0):
    qi = lax.broadcasted_iota(jnp.int32, (WINDOW, 2 * WINDOW), 0)
    kj = lax.broadcasted_iota(jnp.int32, (WINDOW, 2 * WINDOW), 1)
    dist = qi + WINDOW - kj
    valid = (dist >= 0) & (dist < WINDOW) & (r0 - WINDOW + kj >= 0)
    return dist.astype(F32), valid


def swa_fwd(z, kpad, vpad, sinks, name, tq=512):
    T = z.shape[0]
    tq = _tile(T, tq)
    HQ = SWA_HEADS * HEAD_DIM

    def body(sink_ref, q_ref, k_ref, v_ref, o_ref):
        i = pl.program_id(0)
        for sub in range(tq // WINDOW):
            r0 = pl.multiple_of(i * tq + sub * WINDOW, WINDOW)
            kk = k_ref[pl.ds(r0, 2 * WINDOW), :]
            vv = v_ref[pl.ds(r0, 2 * WINDOW), :]
            qb = q_ref[sub * WINDOW:(sub + 1) * WINDOW, :].astype(BF16)
            dist, valid = _swa_mask(r0)
            outs = []
            for h in range(SWA_HEADS):
                kh = h // SWA_GROUP
                ks = slice(kh * HEAD_DIM, (kh + 1) * HEAD_DIM)
                pn, _ = _swa_scores(qb[:, h * HEAD_DIM:(h + 1) * HEAD_DIM], kk[:, ks], 2.0 ** (-(h + 1)), dist, valid,
                                    sink_ref[h])
                outs.append(_dot(pn.astype(BF16), vv[:, ks]))
            o_ref[sub * WINDOW:(sub + 1) * WINDOW, :] = jnp.concatenate(outs, axis=-1).astype(BF16)

    return pl.pallas_call(
        body, name=name, grid=(T // tq,),
        in_specs=[pl.BlockSpec(memory_space=pltpu.SMEM),
                  pl.BlockSpec((tq, HQ), lambda i: (i, 2)),
                  pl.BlockSpec((T + WINDOW, 2 * HEAD_DIM), lambda i: (0, 0)),
                  pl.BlockSpec((T + WINDOW, 2 * HEAD_DIM), lambda i: (0, 0))],
        out_specs=pl.BlockSpec((tq, HQ), lambda i: (i, 0)),
        out_shape=jax.ShapeDtypeStruct((T, HQ), BF16),
        compiler_params=_params(1),
    )(sinks, z, kpad, vpad)


def swa_bwd(z, kpad, vpad, sinks, dm, name, tq=512):
    T = z.shape[0]
    tq = _tile(T, tq)
    HQ = SWA_HEADS * HEAD_DIM
    scale = HEAD_DIM ** -0.5

    def body(sink_ref, q_ref, k_ref, v_ref, do_ref, dq_ref, dk_ref, dv_ref, dsink_ref):
        i = pl.program_id(0)

        @pl.when(i == 0)
        def _():
            dk_ref[...] = jnp.zeros_like(dk_ref)
            dv_ref[...] = jnp.zeros_like(dv_ref)
            dsink_ref[...] = jnp.zeros_like(dsink_ref)

        for sub in range(tq // WINDOW):
            r0 = pl.multiple_of(i * tq + sub * WINDOW, WINDOW)
            kk = k_ref[pl.ds(r0, 2 * WINDOW), :]
            vv = v_ref[pl.ds(r0, 2 * WINDOW), :]
            rows = slice(sub * WINDOW, (sub + 1) * WINDOW)
            qb = q_ref[rows, :].astype(BF16)
            dob = do_ref[rows, :].astype(BF16)
            dist, valid = _swa_mask(r0)
            dqs, dks, dvs = [], [], []
            for kh in range(SWA_KV_HEADS):
                ks = slice(kh * HEAD_DIM, (kh + 1) * HEAD_DIM)
                dk_acc = jnp.zeros((2 * WINDOW, HEAD_DIM), F32)
                dv_acc = jnp.zeros((2 * WINDOW, HEAD_DIM), F32)
                for g in range(SWA_GROUP):
                    h = kh * SWA_GROUP + g
                    hs = slice(h * HEAD_DIM, (h + 1) * HEAD_DIM)
                    pn, p_sink = _swa_scores(qb[:, hs], kk[:, ks], 2.0 ** (-(h + 1)), dist, valid, sink_ref[h])
                    dp = _dot_nt(dob[:, hs], vv[:, ks])
                    delta = jnp.sum(pn * dp, axis=-1, keepdims=True)
                    ds = (pn * (dp - delta)).astype(BF16)
                    dqs.append(_dot(ds, kk[:, ks]) * scale)
                    dk_acc = dk_acc + _dot_tn(ds, qb[:, hs]) * scale
                    dv_acc = dv_acc + _dot_tn(pn.astype(BF16), dob[:, hs])
                    dsink_ref[h:h + 1, :] += jnp.zeros((1, 128), F32) - jnp.sum(p_sink * delta)
                dks.append(dk_acc)
                dvs.append(dv_acc)
            dq_ref[rows, :] = jnp.concatenate(dqs, axis=-1).astype(BF16)
            dk_ref[pl.ds(r0, 2 * WINDOW), :] += jnp.concatenate(dks, axis=-1)
            dv_ref[pl.ds(r0, 2 * WINDOW), :] += jnp.concatenate(dvs, axis=-1)

    kv_spec = pl.BlockSpec((T + WINDOW, 2 * HEAD_DIM), lambda i: (0, 0))
    return pl.pallas_call(
        body, name=name, grid=(T // tq,),
        in_specs=[pl.BlockSpec(memory_space=pltpu.SMEM),
                  pl.BlockSpec((tq, HQ), lambda i: (i, 2)),
                  kv_spec, kv_spec,
                  pl.BlockSpec((tq, HQ), lambda i: (i, 1))],
        out_specs=[pl.BlockSpec((tq, HQ), lambda i: (i, 0)),
                   kv_spec, kv_spec,
                   pl.BlockSpec((SWA_HEADS, 128), lambda i: (0, 0))],
        out_shape=[jax.ShapeDtypeStruct((T, HQ), BF16),
                   jax.ShapeDtypeStruct((T + WINDOW, 2 * HEAD_DIM), F32),
                   jax.ShapeDtypeStruct((T + WINDOW, 2 * HEAD_DIM), F32),
                   jax.ShapeDtypeStruct((SWA_HEADS, 128), F32)],
        compiler_params=_params(1),
    )(sinks, z, kpad, vpad, dm)


def short_conv_fwd(z, w, name, tm=512):
    T = z.shape[0]
    C = SC_CH
    tm = _tile(T, tm)
    hb = tm // SC_HALO

    def body(b_ref, c_ref, v_ref, pc_ref, pv_ref, w_ref, o_ref, xs_ref):
        i = pl.program_id(0)
        xs_ref[0:SC_HALO, :] = jnp.where(i > 0, pc_ref[...] * pv_ref[...], 0.0)
        xs_ref[SC_HALO:, :] = c_ref[...] * v_ref[...]
        conv = jnp.zeros((tm, C), F32)
        for k in range(3):
            conv = conv + w_ref[k:k + 1, :] * xs_ref[pl.ds(SC_HALO - 2 + k, tm), :]
        o_ref[...] = (b_ref[...] * conv).astype(BF16)

    prev_map = lambda i: jnp.maximum(i * hb - 1, 0)
    return pl.pallas_call(
        body, name=name, grid=(T // tm,),
        in_specs=[pl.BlockSpec((tm, C), lambda i: (i, 0)),
                  pl.BlockSpec((tm, C), lambda i: (i, 1)),
                  pl.BlockSpec((tm, C), lambda i: (i, 2)),
                  pl.BlockSpec((SC_HALO, C), lambda i: (prev_map(i), 1)),
                  pl.BlockSpec((SC_HALO, C), lambda i: (prev_map(i), 2)),
                  pl.BlockSpec((8, C), lambda i: (0, 0))],
        out_specs=pl.BlockSpec((tm, C), lambda i: (i, 0)),
        out_shape=jax.ShapeDtypeStruct((T, C), BF16),
        scratch_shapes=[pltpu.VMEM((tm + SC_HALO, C), F32)],
        compiler_params=_params(1),
    )(z, z, z, z, z, w)


def short_conv_bwd(z, dm, w, name, tm=512):
    T = z.shape[0]
    C = SC_CH
    tm = _tile(T, tm)
    hb = tm // SC_HALO
    n_tiles = T // tm
    last_halo = T // SC_HALO - 1
    R = tm + SC_HALO

    def body(b_ref, c_ref, v_ref, pc_ref, pv_ref, nb_ref, do_ref, ndo_ref, w_ref, dz_ref, dw_ref, xs_ref, ds_ref):
        i = pl.program_id(0)

        @pl.when(i == 0)
        def _():
            dw_ref[...] = jnp.zeros_like(dw_ref)

        c = c_ref[...]
        val = v_ref[...]
        dout = do_ref[...]
        xs_ref[0:SC_HALO, :] = jnp.where(i > 0, pc_ref[...] * pv_ref[...], 0.0)
        xs_ref[SC_HALO:, :] = c * val
        dconv = dout * b_ref[...]
        ds_ref[0:tm, :] = dconv
        ds_ref[tm:, :] = jnp.where(i < n_tiles - 1, ndo_ref[...] * nb_ref[...], 0.0)
        conv = jnp.zeros((tm, C), F32)
        dcv = jnp.zeros((tm, C), F32)
        for k in range(3):
            xk = xs_ref[pl.ds(SC_HALO - 2 + k, tm), :]
            conv = conv + w_ref[k:k + 1, :] * xk
            dw_ref[k:k + 1, :] += jnp.sum(dconv * xk, axis=0, keepdims=True)
            dcv = dcv + w_ref[k:k + 1, :] * ds_ref[pl.ds(2 - k, tm), :]
        dz_ref[:, 0:C] = (dout * conv).astype(BF16)
        dz_ref[:, C:2 * C] = (dcv * val).astype(BF16)
        dz_ref[:, 2 * C:] = (dcv * c).astype(BF16)

    prev_map = lambda i: jnp.maximum(i * hb - 1, 0)
    next_map = lambda i: jnp.minimum((i + 1) * hb, last_halo)
    return pl.pallas_call(
        body, name=name, grid=(n_tiles,),
        in_specs=[pl.BlockSpec((tm, C), lambda i: (i, 0)),
                  pl.BlockSpec((tm, C), lambda i: (i, 1)),
                  pl.BlockSpec((tm, C), lambda i: (i, 2)),
                  pl.BlockSpec((SC_HALO, C), lambda i: (prev_map(i), 1)),
                  pl.BlockSpec((SC_HALO, C), lambda i: (prev_map(i), 2)),
                  pl.BlockSpec((SC_HALO, C), lambda i: (next_map(i), 0)),
                  pl.BlockSpec((tm, C), lambda i: (i, 0)),
                  pl.BlockSpec((SC_HALO, C), lambda i: (next_map(i), 0)),
                  pl.BlockSpec((8, C), lambda i: (0, 0))],
        out_specs=[pl.BlockSpec((tm, 3 * C), lambda i: (i, 0)),
                   pl.BlockSpec((8, C), lambda i: (0, 0))],
        out_shape=[jax.ShapeDtypeStruct((T, 3 * C), BF16), jax.ShapeDtypeStruct((8, C), F32)],
        scratch_shapes=[pltpu.VMEM((tm + SC_HALO, C), F32), pltpu.VMEM((R, C), F32)],
        compiler_params=_params(1),
    )(z, z, z, z, z, z, dm, dm, w)


def _xa_probs(q_h, k_h):
    s = _dot_nt(q_h, k_h) * (XA_HEAD_DIM ** -0.5)
    p = jnp.exp(s - jnp.max(s, axis=-1, keepdims=True))
    return p * (1.0 / jnp.sum(p, axis=-1, keepdims=True))


def xattn_fwd(q, kv, name, tm=512):
    T = q.shape[0]
    M = kv.shape[0]
    tm = _tile(T, tm)

    def body(q_ref, k_ref, v_ref, o_ref):
        for h in range(XA_HEADS):
            hs = slice(h * XA_HEAD_DIM, (h + 1) * XA_HEAD_DIM)
            p = _xa_probs(q_ref[:, hs], k_ref[:, hs])
            o_ref[:, hs] = _dot(p.astype(BF16), v_ref[:, hs]).astype(BF16)

    return pl.pallas_call(
        body, name=name, grid=(T // tm,),
        in_specs=[pl.BlockSpec((tm, D_MODEL), lambda i: (i, 0)),
                  pl.BlockSpec((M, D_MODEL), lambda i: (0, 0)),
                  pl.BlockSpec((M, D_MODEL), lambda i: (0, 1))],
        out_specs=pl.BlockSpec((tm, D_MODEL), lambda i: (i, 0)),
        out_shape=jax.ShapeDtypeStruct((T, D_MODEL), BF16),
        compiler_params=_params(1),
    )(q, kv, kv)


def xattn_bwd(q, kv, do, name, tm=512):
    T = q.shape[0]
    M = kv.shape[0]
    tm = _tile(T, tm)
    scale = XA_HEAD_DIM ** -0.5

    def body(q_ref, k_ref, v_ref, do_ref, dq_ref, dkv_ref):
        @pl.when(pl.program_id(0) == 0)
        def _():
            dkv_ref[...] = jnp.zeros_like(dkv_ref)

        for h in range(XA_HEADS):
            hs = slice(h * XA_HEAD_DIM, (h + 1) * XA_HEAD_DIM)
            vs = slice(D_MODEL + h * XA_HEAD_DIM, D_MODEL + (h + 1) * XA_HEAD_DIM)
            q_h = q_ref[:, hs]
            do_h = do_ref[:, hs]
            p = _xa_probs(q_h, k_ref[:, hs])
            dp = _dot_nt(do_h, v_ref[:, hs])
            ds = (p * (dp - jnp.sum(p * dp, axis=-1, keepdims=True))).astype(BF16)
            dq_ref[:, hs] = (_dot(ds, k_ref[:, hs]) * scale).astype(BF16)
            dkv_ref[:, hs] += _dot_tn(ds, q_h) * scale
            dkv_ref[:, vs] += _dot_tn(p.astype(BF16), do_h)

    return pl.pallas_call(
        body, name=name, grid=(T // tm,),
        in_specs=[pl.BlockSpec((tm, D_MODEL), lambda i: (i, 0)),
                  pl.BlockSpec((M, D_MODEL), lambda i: (0, 0)),
                  pl.BlockSpec((M, D_MODEL), lambda i: (0, 1)),
                  pl.BlockSpec((tm, D_MODEL), lambda i: (i, 0))],
        out_specs=[pl.BlockSpec((tm, D_MODEL), lambda i: (i, 0)),
                   pl.BlockSpec((M, 2 * D_MODEL), lambda i: (0, 0))],
        out_shape=[jax.ShapeDtypeStruct((T, D_MODEL), BF16), jax.ShapeDtypeStruct((M, 2 * D_MODEL), F32)],
        compiler_params=_params(1),
    )(q, kv, kv, do)


def final_loss(h, g, target, name, tm=512):
    T, K = h.shape
    tm = _tile(T, tm)

    def body(h_ref, g_ref, t_ref, dh_ref, dg_ref, loss_ref):
        @pl.when(pl.program_id(0) == 0)
        def _():
            dg_ref[...] = jnp.zeros_like(dg_ref)
            loss_ref[...] = jnp.zeros_like(loss_ref)

        x = h_ref[...]
        r = lax.rsqrt(jnp.mean(x * x, axis=-1, keepdims=True) + RMS_EPS)
        xh = x * r
        e = xh * g_ref[...] - t_ref[...]
        loss_ref[...] += jnp.zeros((1, 128), F32) + 0.5 * jnp.sum(jnp.mean(e * e, axis=-1, keepdims=True))
        dy = e * (1.0 / K)
        dg_ref[...] += jnp.sum(dy * xh, axis=0, keepdims=True)
        dxh = dy * g_ref[...]
        dh_ref[...] = r * (dxh - xh * jnp.mean(dxh * xh, axis=-1, keepdims=True))

    return pl.pallas_call(
        body, name=name, grid=(T // tm,),
        in_specs=[pl.BlockSpec((tm, K), lambda i: (i, 0)),
                  pl.BlockSpec((1, K), lambda i: (0, 0)),
                  pl.BlockSpec((tm, K), lambda i: (i, 0))],
        out_specs=[pl.BlockSpec((tm, K), lambda i: (i, 0)),
                   pl.BlockSpec((1, K), lambda i: (0, 0)),
                   pl.BlockSpec((1, 128), lambda i: (0, 0))],
        out_shape=[jax.ShapeDtypeStruct((T, K), F32), jax.ShapeDtypeStruct((1, K), F32),
                   jax.ShapeDtypeStruct((1, 128), F32)],
        compiler_params=_params(1),
    )(h, g, target)


def _row(v):
    return v.reshape(1, -1)


def _pad_rows(a, rows):
    return jnp.pad(a, ((0, rows - a.shape[0]), (0, 0)))


def local_step(x, mem, target, P, get_weights, put_grads):
    cw = _pad_rows(P["conv_a_w"], 32)
    scw = _pad_rows(P["sc_conv_w"], 8)
    cb, lg, lb = _row(P["conv_a_b"]), _row(P["conv_a_ln_g"]), _row(P["conv_a_ln_b"])
    sinks = P["swa_sinks"]

    class _Layered:
        def __init__(self, store, name=None):
            self.store, self.name = store, name

        def __getitem__(self, key):
            if self.name is None:
                return self.store[(key, 0)] if key in ("even_w_in", "even_w_out", "odd_w_in", "odd_w_out") \
                    else _Layered(self.store, key)
            return self.store[(self.name, key)]

    store = {}
    W = _Layered(store)
    saved = []
    h = x
    for i in range(2):
        L = f"l{i}"
        new, dep = get_weights("A" if i == 0 else "D", h)
        store.update(new)
        h, s_ffn1 = ffn_forward(h, P["ffn1_norm"][i:i + 1], W["ffn1_w_gu"][i], W["ffn1_w_down"][i], L + "_ffn1", dep=dep)
        h1 = h
        if i == 0:
            new, _ = get_weights("B", h)
            store.update(new)
            z, u2 = norm_matmul(h1, P["mix_norm"][i:i + 1], W["even_w_in"], F32, L + "_mix_in")
            a = conformer_conv_fwd(z, cw, cb, lg, lb, L + "_conv")
            kpad = jnp.pad(z[:, 1536:1664].astype(BF16), ((WINDOW, 0), (0, 0)))
            vpad = jnp.pad(z[:, 1664:1792].astype(BF16), ((WINDOW, 0), (0, 0)))
            o = swa_fwd(z, kpad, vpad, sinks, L + "_swa")
            m = jnp.concatenate([a, o], axis=-1)
            h = matmul_residual(m, W["even_w_out"], h1, L + "_mix_out")
            s_mix = (h1, u2, z, m, kpad, vpad)
        else:
            z, u2 = norm_matmul(h1, P["mix_norm"][i:i + 1], W["odd_w_in"], F32, L + "_mix_in")
            m = short_conv_fwd(z, scw, L + "_sconv")
            h = matmul_residual(m, W["odd_w_out"], h1, L + "_mix_out")
            s_mix = (h1, u2, z, m)
        h2 = h
        kv, umem = norm_matmul(mem, P["xa_mem_norm"][i:i + 1], W["xa_wkv"][i], BF16, L + "_xa_kv", tm=256)
        q, u3 = norm_matmul(h2, P["xa_norm"][i:i + 1], W["xa_wq"][i], BF16, L + "_xa_q", transposed=False)
        o = xattn_fwd(q, kv, L + "_xa")
        h = matmul_residual(o, W["xa_wo"][i], h2, L + "_xa_out")
        s_xa = (h2, u3, q, o, kv, umem)
        if i == 0:
            new, _ = get_weights("C", h)
            store.update(new)
        h, s_ffn2 = ffn_forward(h, P["ffn2_norm"][i:i + 1], W["ffn2_w_gu"][i], W["ffn2_w_down"][i], L + "_ffn2")
        saved.append((s_ffn1, s_mix, s_xa, s_ffn2))

    dh, d_final, loss = final_loss(h, _row(P["final_norm"]), target, "final_loss")

    names = ("ffn1_w_gu", "ffn1_w_down", "ffn2_w_gu", "ffn2_w_down", "xa_wq", "xa_wkv", "xa_wo", "even_w_in", "even_w_out",
             "odd_w_in", "odd_w_out")
    dW = {k: [None, None] for k in names}
    dP = {k: [None, None] for k in ("ffn1_norm", "mix_norm", "xa_norm", "xa_mem_norm", "ffn2_norm")}
    dP["final_norm"] = d_final.reshape(-1)
    for i in (1, 0):
        L = f"l{i}b"
        s_ffn1, s_mix, s_xa, s_ffn2 = saved[i]

        def keep_ffn2(d_w_gu, d_w_down, i=i):
            dW["ffn2_w_gu"][i], dW["ffn2_w_down"][i] = d_w_gu, d_w_down

        def send_ffn1(d_w_gu, d_w_down, i=i):
            dW["ffn1_w_gu"][i], dW["ffn1_w_down"][i] = d_w_gu, d_w_down
            stage = "D" if i == 1 else "A"
            return put_grads(stage, {k: dW[k[0]][k[1]] for k in STAGE_KEYS[stage]})

        dh, dP["ffn2_norm"][i] = ffn_backward(
            dh, s_ffn2, P["ffn2_norm"][i:i + 1], W["ffn2_w_gu"][i], W["ffn2_w_down"][i], L + "_ffn2", emit=keep_ffn2)
        h2, u3, q, o, kv, umem = s_xa
        dW["xa_wo"][i] = matmul_tn(o, dh, L + "_xa_dwo")
        do = matmul_nt(dh, W["xa_wo"][i], BF16, L + "_xa_do")
        dq, dkv = xattn_bwd(q, kv, do, L + "_xa")
        dW["xa_wq"][i] = matmul_tn(u3, dq, L + "_xa_dwq")
        dW["xa_wkv"][i] = matmul_tn(dkv, umem, L + "_xa_dwkv", tk=1024)
        dkv_b = dkv.astype(BF16)
        _, dP["xa_mem_norm"][i] = matmul_norm_bwd(dkv_b, W["xa_wkv"][i], mem, P["xa_mem_norm"][i:i + 1],
                                                  jnp.zeros_like(mem), L + "_xa_dmem")
        dh, dP["xa_norm"][i] = matmul_norm_bwd(dq, W["xa_wq"][i], h2, P["xa_norm"][i:i + 1], dh, L + "_xa_dx",
                                               transposed=False)
        if i == 0:
            h1, u2, z, m, kpad, vpad = s_mix
            dW["even_w_out"][0] = matmul_tn(m, dh, L + "_mix_dwo")
            dm = matmul_nt(dh, W["even_w_out"], F32, L + "_mix_dm")
            dz_conv, dcw, dcb, dlg, dlb = conformer_conv_bwd(z, dm, cw, cb, lg, lb, L + "_conv")
            dq_s, dkp, dvp, dsk = swa_bwd(z, kpad, vpad, sinks, dm, L + "_swa")
            dz = jnp.concatenate([dz_conv, dq_s, dkp[WINDOW:].astype(BF16), dvp[WINDOW:].astype(BF16)], axis=-1)
            dW["even_w_in"][0] = matmul_tn(dz, u2, L + "_mix_dwi", tk=896)
            dh, dP["mix_norm"][i] = matmul_norm_bwd(dz, W["even_w_in"], h1, P["mix_norm"][i:i + 1], dh, L + "_mix_dx")
            dP["conv_a_w"] = dcw[:CONV_A_WIDTH]
            dP["conv_a_b"], dP["conv_a_ln_g"], dP["conv_a_ln_b"] = dcb.reshape(-1), dlg.reshape(-1), dlb.reshape(-1)
            dP["swa_sinks"] = dsk[:, 0]
        else:
            h1, u2, z, m = s_mix
            dW["odd_w_out"][0] = matmul_tn(m, dh, L + "_mix_dwo")
            dm = matmul_nt(dh, W["odd_w_out"], F32, L + "_mix_dm")
            dz, dscw = short_conv_bwd(z, dm, scw, L + "_sconv")
            dW["odd_w_in"][0] = matmul_tn(dz, u2, L + "_mix_dwi", tk=1024)
            dh, dP["mix_norm"][i] = matmul_norm_bwd(dz, W["odd_w_in"], h1, P["mix_norm"][i:i + 1], dh, L + "_mix_dx")
            dP["sc_conv_w"] = dscw[:3]
        dep = put_grads("BC", {k: dW[k[0]][k[1]] for k in STAGE_KEYS["B"] + STAGE_KEYS["C"]}) if i == 0 else None
        dh, dP["ffn1_norm"][i] = ffn_backward(
            dh, s_ffn1, P["ffn1_norm"][i:i + 1], W["ffn1_w_gu"][i], W["ffn1_w_down"][i], L + "_ffn1", dep=dep,
            emit=send_ffn1)
    for k in ("ffn1_norm", "mix_norm", "xa_norm", "xa_mem_norm", "ffn2_norm"):
        dP[k] = jnp.concatenate(dP[k], axis=0)
    return loss, dh, dP


def _mesh_pos():
    return lax.axis_index("x"), lax.axis_index("y"), lax.axis_index("c")


def _flat_index(px, py, pc):
    return 4 * px + 2 * py + pc


def all_gather(blob, name):
    R, C = blob.shape

    def body(x_ref, out_ref, send_sems, recv_sems, local_sem):
        x, y, c = _mesh_pos()
        me, sibling = (x, y, c), (x, y, 1 - c)
        chips = [(1 - x, y), (x, 1 - y), (1 - x, 1 - y)]

        def slot(px, py, pc):
            return out_ref.at[_flat_index(px, py, pc)]

        def copy(k, block, to, src=None):
            return pltpu.make_async_remote_copy(
                src_ref=slot(*block) if src is None else src, dst_ref=slot(*block),
                send_sem=send_sems.at[k], recv_sem=recv_sems.at[k],
                device_id=to, device_id_type=pl.DeviceIdType.MESH)

        mine = pltpu.make_async_copy(x_ref, slot(*me), local_sem)
        mine.start()
        first = [copy(0, me, sibling, src=x_ref)]
        first += [copy(1 + j, me, (*chip, c), src=x_ref) for j, chip in enumerate(chips)]
        for cp in first:
            cp.start()
        passed = [copy(4 + j, (*chip, c), sibling) for j, chip in enumerate(chips)]
        for j, chip in enumerate(chips):
            copy(1 + j, (*chip, c), me).wait_recv()
            passed[j].start()
        copy(0, sibling, me).wait_recv()
        for j, chip in enumerate(chips):
            copy(4 + j, (*chip, 1 - c), me).wait_recv()
        for cp in first + passed:
            cp.wait_send()
        mine.wait()

    return pl.pallas_call(
        body, name=name,
        out_shape=jax.ShapeDtypeStruct((N_DEV, R, C), blob.dtype),
        in_specs=[pl.BlockSpec(memory_space=pl.ANY)],
        out_specs=pl.BlockSpec(memory_space=pl.ANY),
        scratch_shapes=[pltpu.SemaphoreType.DMA((7,)), pltpu.SemaphoreType.DMA((7,)), pltpu.SemaphoreType.DMA],
    )(blob)


def scatter_exchange(g, name):
    _, R, C = g.shape

    def body(g_ref, out_ref, send_sems, recv_sems, local_sem):
        x, y, c = _mesh_pos()
        me_idx = _flat_index(x, y, c)
        mine = pltpu.make_async_copy(g_ref.at[me_idx], out_ref.at[me_idx], local_sem)
        mine.start()
        sends, peers = [], []
        for k in range(1, N_DEV):
            px = 1 - x if k & 4 else x
            py = 1 - y if k & 2 else y
            pc = 1 - c if k & 1 else c
            peer_idx = _flat_index(px, py, pc)
            cp = pltpu.make_async_remote_copy(
                src_ref=g_ref.at[peer_idx], dst_ref=out_ref.at[me_idx],
                send_sem=send_sems.at[k - 1], recv_sem=recv_sems.at[k - 1],
                device_id=(px, py, pc), device_id_type=pl.DeviceIdType.MESH)
            cp.start()
            sends.append(cp)
            peers.append((peer_idx, (px, py, pc)))
        for k in range(1, N_DEV):
            peer_idx, peer = peers[k - 1]
            pltpu.make_async_remote_copy(
                src_ref=g_ref.at[me_idx], dst_ref=out_ref.at[peer_idx],
                send_sem=send_sems.at[k - 1], recv_sem=recv_sems.at[k - 1],
                device_id=peer, device_id_type=pl.DeviceIdType.MESH).wait_recv()
        for cp in sends:
            cp.wait_send()
        mine.wait()

    return pl.pallas_call(
        body, name=name,
        out_shape=jax.ShapeDtypeStruct(g.shape, g.dtype),
        in_specs=[pl.BlockSpec(memory_space=pl.ANY)],
        out_specs=pl.BlockSpec(memory_space=pl.ANY),
        scratch_shapes=[pltpu.SemaphoreType.DMA((7,)), pltpu.SemaphoreType.DMA((7,)), pltpu.SemaphoreType.DMA],
    )(g)


HBM_SPEC = pl.BlockSpec(memory_space=pltpu.HBM)
SEM_SPEC = pl.BlockSpec(memory_space=pltpu.SEMAPHORE)
DATAFLOW_EFFECT = pltpu.SideEffectType.DATAFLOW_SIDE_EFFECTING


def _peers(x, y, c):
    out = []
    for k in range(1, N_DEV):
        pos = (1 - x if k & 4 else x, 1 - y if k & 2 else y, 1 - c if k & 1 else c)
        out.append((_flat_index(*pos), pos))
    return out


def _exchange_copy(src_ref, land_ref, send_sems, recv_sems, j, me, peer_idx, peer, scatter):
    return pltpu.make_async_remote_copy(
        src_ref=src_ref.at[peer_idx] if scatter else src_ref, dst_ref=land_ref.at[me],
        send_sem=send_sems.at[j], recv_sem=recv_sems.at[j], device_id=peer, device_id_type=pl.DeviceIdType.MESH)


def exchange_start(srcs, lands, scatter, after, name):
    n = len(srcs)
    n_after = len(after)

    def body(*refs):
        src_refs, land_refs = refs[:n], refs[n:2 * n]
        outs = refs[2 * n + n_after:]
        send_sems, recv_sems, token = outs[:n], outs[n:2 * n], outs[4 * n]
        x, y, c = _mesh_pos()
        me = _flat_index(x, y, c)
        for g in range(n):
            for j, (peer_idx, peer) in enumerate(_peers(x, y, c)):
                _exchange_copy(src_refs[g], land_refs[g], send_sems[g], recv_sems[g], j, me, peer_idx, peer, scatter).start()
        token[...] = jnp.zeros_like(token)

    hbm = lambda a: pltpu.with_memory_space_constraint(a, pltpu.HBM)
    res = pl.pallas_call(
        body, name=name,
        out_shape=(*[pltpu.SemaphoreType.DMA((N_DEV - 1,))] * (2 * n),
                   *[pltpu.HBM(a.shape, a.dtype) for a in srcs], *[pltpu.HBM(a.shape, a.dtype) for a in lands],
                   jax.ShapeDtypeStruct((8, 128), F32)),
        in_specs=[HBM_SPEC] * (2 * n) + [ANY_SPEC] * n_after,
        out_specs=(*[SEM_SPEC] * (2 * n), *[HBM_SPEC] * (2 * n), pl.BlockSpec(memory_space=pltpu.VMEM)),
        input_output_aliases={i: 2 * n + i for i in range(2 * n)},
        compiler_params=pltpu.CompilerParams(has_side_effects=DATAFLOW_EFFECT),
    )(*[hbm(a) for a in srcs], *[hbm(a) for a in lands], *after)
    handles = [(res[g], res[n + g], res[2 * n + g], res[3 * n + g]) for g in range(n)]
    return handles, res[4 * n]


def exchange_wait(handle, scatter, after, name):
    send_sem, recv_sem, src_thru, land_thru = handle

    def body(src_ref, land_ref, send_sems, recv_sems, after_ref, src_dead, got_ref):
        x, y, c = _mesh_pos()
        me = _flat_index(x, y, c)
        for j, (peer_idx, peer) in enumerate(_peers(x, y, c)):
            mine = _exchange_copy(src_ref, land_ref, send_sems, recv_sems, j, me, peer_idx, peer, scatter)
            mine.wait_send()
            theirs = pltpu.make_async_remote_copy(
                src_ref=src_ref.at[me] if scatter else src_ref, dst_ref=land_ref.at[peer_idx],
                send_sem=send_sems.at[j], recv_sem=recv_sems.at[j], device_id=peer, device_id_type=pl.DeviceIdType.MESH)
            theirs.wait_recv()

    return pl.pallas_call(
        body, name=name,
        out_shape=(pltpu.HBM(src_thru.shape, src_thru.dtype), pltpu.HBM(land_thru.shape, land_thru.dtype)),
        in_specs=(HBM_SPEC, HBM_SPEC, SEM_SPEC, SEM_SPEC, ANY_SPEC), out_specs=(HBM_SPEC, HBM_SPEC),
        input_output_aliases={0: 0, 1: 1},
        compiler_params=pltpu.CompilerParams(has_side_effects=DATAFLOW_EFFECT),
    )(src_thru, land_thru, send_sem, recv_sem, after)


def ordered_sum(parts, name, tr=512):
    n, R, C = parts.shape
    tr = next((t for t in range(min(tr, R), 15, -16) if R % t == 0), R)

    def body(p_ref, o_ref):
        acc = p_ref[0].astype(F32)
        for j in range(1, n):
            acc = acc + p_ref[j].astype(F32)
        o_ref[...] = acc

    return pl.pallas_call(
        body, name=name, grid=(R // tr,),
        in_specs=[pl.BlockSpec((n, tr, C), lambda i: (0, i, 0))],
        out_specs=pl.BlockSpec((tr, C), lambda i: (i, 0)),
        out_shape=jax.ShapeDtypeStruct((R, C), F32),
        compiler_params=_params(1),
    )(parts)


def adamw(w, g, m, v, name, tr=256):
    R, C = w.shape
    tr = next((t for t in range(tr, 7, -8) if R % t == 0), R)
    c1 = 1.0 - ADAM_B1 ** ADAM_STEP
    c2 = 1.0 - ADAM_B2 ** ADAM_STEP

    def body(w_ref, g_ref, m_ref, v_ref, d_ref, mo_ref, vo_ref):
        grad = g_ref[...]
        m2 = ADAM_B1 * m_ref[...] + (1.0 - ADAM_B1) * grad
        v2 = ADAM_B2 * v_ref[...] + (1.0 - ADAM_B2) * (grad * grad)
        mo_ref[...] = m2
        vo_ref[...] = v2
        d_ref[...] = -ADAM_LR * ((m2 / c1) / (jnp.sqrt(v2 / c2) + ADAM_EPS) + ADAM_WD * w_ref[...])

    spec = pl.BlockSpec((tr, C), lambda i: (i, 0))
    return pl.pallas_call(
        body, name=name, grid=(R // tr,),
        in_specs=[spec] * 4, out_specs=[spec] * 3,
        out_shape=[jax.ShapeDtypeStruct((R, C), F32)] * 3,
        compiler_params=_params(1),
    )(w, g, m, v)


WEIGHT_NAMES = ("ffn1_norm", "ffn1_w_gu", "ffn1_w_down", "mix_norm", "even_w_in", "conv_a_w", "conv_a_b", "conv_a_ln_g",
                "conv_a_ln_b", "swa_sinks", "even_w_out", "odd_w_in", "sc_conv_w", "odd_w_out", "xa_norm", "xa_mem_norm",
                "xa_wq", "xa_wkv", "xa_wo", "ffn2_norm", "ffn2_w_gu", "ffn2_w_down", "final_norm")
BLOB_COLS = 1024
SMALL_ROWS = (("ffn1_norm", 0, 2), ("mix_norm", 2, 2), ("xa_norm", 4, 2), ("xa_mem_norm", 6, 2), ("ffn2_norm", 8, 2),
              ("final_norm", 10, 1))
ROW_CONV_B_LNG = 11
ROW_LNB_SINKS_LOSS = 12
LOSS_COL = 512 + SWA_HEADS
ROW_SC_CONV = 13
ROW_CONV_W = 16
SMALL_BLOB_ROWS = 32
SMALL_ADAM_ROWS = 16


def _small_blob(v):
    rows = [v[n].reshape(-1, D_MODEL) for n, _, _ in SMALL_ROWS]
    rows.append(jnp.concatenate([v["conv_a_b"].reshape(-1), v["conv_a_ln_g"].reshape(-1)]).reshape(1, D_MODEL))
    tail = jnp.zeros((D_MODEL - 512 - SWA_HEADS,), F32)
    if "loss" in v:
        tail = tail.at[0].set(v["loss"])
    rows.append(jnp.concatenate([v["conv_a_ln_b"].reshape(-1), v["swa_sinks"].reshape(-1), tail]).reshape(1, D_MODEL))
    rows.append(jnp.zeros((SMALL_ADAM_ROWS - ROW_SC_CONV, D_MODEL), F32))
    return jnp.concatenate(rows, axis=0)


def _small_unblob(b, shapes):
    out = {n: b[r:r + k].reshape(shapes[n]) for n, r, k in SMALL_ROWS}
    out["conv_a_b"] = b[ROW_CONV_B_LNG, :512].reshape(shapes["conv_a_b"])
    out["conv_a_ln_g"] = b[ROW_CONV_B_LNG, 512:].reshape(shapes["conv_a_ln_g"])
    out["conv_a_ln_b"] = b[ROW_LNB_SINKS_LOSS, :512].reshape(shapes["conv_a_ln_b"])
    out["swa_sinks"] = b[ROW_LNB_SINKS_LOSS, 512:512 + SWA_HEADS].reshape(shapes["swa_sinks"])
    return out


def kernel(x, mem, ffn1_norm, ffn1_w_gu, ffn1_w_down, mix_norm, even_w_in, conv_a_w, conv_a_b, conv_a_ln_g, conv_a_ln_b, swa_sinks, even_w_out, odd_w_in, sc_conv_w, odd_w_out, xa_norm, xa_mem_norm, xa_wq, xa_wkv, xa_wo, ffn2_norm, ffn2_w_gu, ffn2_w_down, final_norm, loss_target, m_ffn1_norm, m_ffn1_w_gu, m_ffn1_w_down, m_mix_norm, m_even_w_in, m_conv_a_w, m_conv_a_b, m_conv_a_ln_g, m_conv_a_ln_b, m_swa_sinks, m_even_w_out, m_odd_w_in, m_sc_conv_w, m_odd_w_out, m_xa_norm, m_xa_mem_norm, m_xa_wq, m_xa_wkv, m_xa_wo, m_ffn2_norm, m_ffn2_w_gu, m_ffn2_w_down, m_final_norm, v_ffn1_norm, v_ffn1_w_gu, v_ffn1_w_down, v_mix_norm, v_even_w_in, v_conv_a_w, v_conv_a_b, v_conv_a_ln_g, v_conv_a_ln_b, v_swa_sinks, v_even_w_out, v_odd_w_in, v_sc_conv_w, v_odd_w_out, v_xa_norm, v_xa_mem_norm, v_xa_wq, v_xa_wkv, v_xa_wo, v_ffn2_norm, v_ffn2_w_gu, v_ffn2_w_down, v_final_norm):
    w = dict(ffn1_norm=ffn1_norm, ffn1_w_gu=ffn1_w_gu, ffn1_w_down=ffn1_w_down, mix_norm=mix_norm, even_w_in=even_w_in,
             conv_a_w=conv_a_w, conv_a_b=conv_a_b, conv_a_ln_g=conv_a_ln_g, conv_a_ln_b=conv_a_ln_b, swa_sinks=swa_sinks,
             even_w_out=even_w_out, odd_w_in=odd_w_in, sc_conv_w=sc_conv_w, odd_w_out=odd_w_out, xa_norm=xa_norm,
             xa_mem_norm=xa_mem_norm, xa_wq=xa_wq, xa_wkv=xa_wkv, xa_wo=xa_wo, ffn2_norm=ffn2_norm, ffn2_w_gu=ffn2_w_gu,
             ffn2_w_down=ffn2_w_down, final_norm=final_norm)
    m = dict(ffn1_norm=m_ffn1_norm, ffn1_w_gu=m_ffn1_w_gu, ffn1_w_down=m_ffn1_w_down, mix_norm=m_mix_norm,
             even_w_in=m_even_w_in, conv_a_w=m_conv_a_w, conv_a_b=m_conv_a_b, conv_a_ln_g=m_conv_a_ln_g,
             conv_a_ln_b=m_conv_a_ln_b, swa_sinks=m_swa_sinks, even_w_out=m_even_w_out, odd_w_in=m_odd_w_in,
             sc_conv_w=m_sc_conv_w, odd_w_out=m_odd_w_out, xa_norm=m_xa_norm, xa_mem_norm=m_xa_mem_norm, xa_wq=m_xa_wq,
             xa_wkv=m_xa_wkv, xa_wo=m_xa_wo, ffn2_norm=m_ffn2_norm, ffn2_w_gu=m_ffn2_w_gu, ffn2_w_down=m_ffn2_w_down,
             final_norm=m_final_norm)
    v = dict(ffn1_norm=v_ffn1_norm, ffn1_w_gu=v_ffn1_w_gu, ffn1_w_down=v_ffn1_w_down, mix_norm=v_mix_norm,
             even_w_in=v_even_w_in, conv_a_w=v_conv_a_w, conv_a_b=v_conv_a_b, conv_a_ln_g=v_conv_a_ln_g,
             conv_a_ln_b=v_conv_a_ln_b, swa_sinks=v_swa_sinks, even_w_out=v_even_w_out, odd_w_in=v_odd_w_in,
             sc_conv_w=v_sc_conv_w, odd_w_out=v_odd_w_out, xa_norm=v_xa_norm, xa_mem_norm=v_xa_mem_norm, xa_wq=v_xa_wq,
             xa_wkv=v_xa_wkv, xa_wo=v_xa_wo, ffn2_norm=v_ffn2_norm, ffn2_w_gu=v_ffn2_w_gu, ffn2_w_down=v_ffn2_w_down,
             final_norm=v_final_norm)
    me = _flat_index(*_mesh_pos())

    conv_blob = jnp.concatenate([w["conv_a_w"].reshape(-1), w["sc_conv_w"].reshape(-1),
                                 jnp.zeros((8 * 1024 - 31 * 64 - 3 * 128,), F32)]).reshape(8, 1024)
    conv_all = all_gather(conv_blob, "gather_conv_weights").reshape(N_DEV, 8 * 1024)
    conv_a_full = jnp.transpose(conv_all[:, :31 * 64].reshape(N_DEV, 31, 64), (1, 0, 2)).reshape(31, 512)
    sc_full = jnp.transpose(conv_all[:, 31 * 64:31 * 64 + 3 * 128].reshape(N_DEV, 3, 128), (1, 0, 2)).reshape(3, 1024)

    def stage_blob(keys):
        shards = [w[n][l].T if SPLIT_AXIS[n] == 1 else w[n][l] for n, l in keys]
        return jnp.concatenate([s.astype(BF16) for s in shards], axis=0)

    def stage_rows(keys):
        return [(n, l, w[n].shape[2] if SPLIT_AXIS[n] == 1 else w[n].shape[1]) for n, l in keys]

    def unpack_weights(gathered, keys):
        out, off = {}, 0
        for n, l, rows in stage_rows(keys):
            out[(n, l)] = gathered[:, off:off + rows, :].reshape(N_DEV * rows, BLOB_COLS)
            off += rows
        return out

    def with_own(land, own):
        return lax.dynamic_update_slice(land, own[None], (me, 0, 0))

    gathered_a = all_gather(stage_blob(STAGE_KEYS["A"]), "gather_weights_a")
    later = ("B", "C", "D")
    blobs = [stage_blob(STAGE_KEYS[s]) for s in later]
    lands = [lax.empty((N_DEV,) + b.shape, BF16) for b in blobs]
    weight_handles, weight_token = exchange_start(blobs, lands, False, [gathered_a, conv_all], "gather_start")

    def get_weights(stage, after):
        if stage == "A":
            return unpack_weights(gathered_a, STAGE_KEYS["A"]), weight_token
        own, land = exchange_wait(weight_handles[later.index(stage)], False, after, "gather_wait_" + stage.lower())
        return unpack_weights(with_own(land, own), STAGE_KEYS[stage]), None

    grad_handles = {}

    def put_grads(stage, dws):
        packed = jnp.concatenate([dw.reshape(N_DEV, -1, BLOB_COLS) for dw in dws.values()], axis=1)
        land = lax.empty(packed.shape, BF16)
        (handle,), token = exchange_start([packed], [land], True, [], "scatter_start_" + stage.lower())
        grad_handles[stage] = (handle, tuple(dws))
        return token

    P = dict(ffn1_norm=ffn1_norm, mix_norm=mix_norm, xa_norm=xa_norm, xa_mem_norm=xa_mem_norm, ffn2_norm=ffn2_norm,
             final_norm=final_norm, conv_a_w=conv_a_full, conv_a_b=conv_a_b[0], conv_a_ln_g=conv_a_ln_g[0],
             conv_a_ln_b=conv_a_ln_b[0], swa_sinks=swa_sinks[0], sc_conv_w=sc_full)

    loss_part, grad_x, dP = local_step(x[0], mem[0], loss_target[0], P, get_weights, put_grads)

    def finish_grads(stage, after):
        handle, keys = grad_handles[stage]
        packed, land = exchange_wait(handle, True, after, "scatter_wait_" + stage.lower())
        own = lax.dynamic_slice(packed, (me, 0, 0), (1,) + packed.shape[1:])[0]
        rows_f32 = ordered_sum(with_own(land, own), "sum_grads_" + stage.lower())
        out, off = {}, 0
        for n, l, rows in stage_rows(keys):
            part = rows_f32[off:off + rows]
            out[(n, l)] = part.T if SPLIT_AXIS[n] == 1 else part
            off += rows
        return out

    layer_grads = {**finish_grads("D", grad_x), **finish_grads("BC", grad_x)}

    dP = dict(dP, loss=loss_part[0, 0])
    small = jnp.concatenate([
        _small_blob(dP)[:ROW_SC_CONV], dP["sc_conv_w"],
        jnp.concatenate([dP["conv_a_w"].reshape(-1), jnp.zeros((512,), F32)]).reshape(16, D_MODEL)], axis=0)
    small_sum = ordered_sum(all_gather(small, "gather_small_grads"), "sum_small_grads", tr=SMALL_BLOB_ROWS)
    loss = small_sum[ROW_LNB_SINKS_LOSS, LOSS_COL]
    grads = _small_unblob(small_sum, {n: w[n].shape for n in WEIGHT_NAMES})
    sc_g = small_sum[ROW_SC_CONV:ROW_SC_CONV + 3]
    grads["sc_conv_w"] = lax.dynamic_slice(sc_g, (0, me * 128), (3, 128)).reshape(w["sc_conv_w"].shape)
    cw_g = small_sum[ROW_CONV_W:].reshape(-1)[:31 * 512].reshape(31, 512)
    grads["conv_a_w"] = lax.dynamic_slice(cw_g, (0, me * 64), (31, 64)).reshape(w["conv_a_w"].shape)

    delta, new_m, new_v = {}, {}, {}

    def update(n):
        shp = w[n].shape
        two_d = (shp[0] * shp[1], shp[2])
        d_, m_, v_ = adamw(w[n].reshape(two_d), grads[n].reshape(two_d), m[n].reshape(two_d), v[n].reshape(two_d),
                           "adamw_" + n)
        delta[n], new_m[n], new_v[n] = d_.reshape(shp), m_.reshape(shp), v_.reshape(shp)

    first_stage = tuple(n for n, _ in STAGE_KEYS["A"])
    for n in SPLIT_AXIS:
        if n not in first_stage:
            grads[n] = jnp.stack([layer_grads[(n, l)] for l in range(w[n].shape[0])], axis=0)
            update(n)
    update("conv_a_w")
    update("sc_conv_w")
    layer_grads.update(finish_grads("A", delta["ffn2_w_gu"]))
    for n in first_stage:
        grads[n] = jnp.stack([layer_grads[(n, l)] for l in range(w[n].shape[0])], axis=0)
        update(n)
    d_, m_, v_ = adamw(_small_blob(w), small_sum[:SMALL_ADAM_ROWS], _small_blob(m), _small_blob(v), "adamw_small",
                       tr=SMALL_ADAM_ROWS)
    shapes = {n: w[n].shape for n in WEIGHT_NAMES}
    delta.update(_small_unblob(d_, shapes))
    new_m.update(_small_unblob(m_, shapes))
    new_v.update(_small_unblob(v_, shapes))

    return (loss, grad_x[None], *[grads[n] for n in WEIGHT_NAMES], *[delta[n] for n in WEIGHT_NAMES],
            *[new_m[n] for n in WEIGHT_NAMES], *[new_v[n] for n in WEIGHT_NAMES])
```

```python
import functools

import jax
import jax.numpy as jnp
from jax import lax
from jax.experimental import pallas as pl
from jax.experimental.pallas import tpu as pltpu

F32 = jnp.float32
BF16 = jnp.bfloat16

D_MODEL = 1024
D_FF = 2816
CONV_A_CH = 512
CONV_A_WIDTH = 31
SWA_HEADS = 8
SWA_KV_HEADS = 2
SWA_GROUP = SWA_HEADS // SWA_KV_HEADS
HEAD_DIM = 64
WINDOW = 128
SC_CH = 1024
XA_HEADS = 4
XA_HEAD_DIM = D_MODEL // XA_HEADS
RMS_EPS = 1e-6
LN_EPS = 1e-5
ADAM_LR = 0.001
ADAM_B1 = 0.9
ADAM_B2 = 0.999
ADAM_EPS = 1e-08
ADAM_WD = 0.01
ADAM_STEP = 10
N_DEV = 8

V7X_VMEM_BYTES = 64 * 1024 * 1024
VMEM_LIMIT = V7X_VMEM_BYTES - 8 * 1024 * 1024
CONV_HALO = 32
SC_HALO = 8
NEG_BIG = -1e30

SPLIT_AXIS = dict(ffn1_w_gu=1, ffn1_w_down=0, even_w_in=1, even_w_out=0, odd_w_in=1, odd_w_out=0, xa_wq=0, xa_wkv=1, xa_wo=0,
                  ffn2_w_gu=1, ffn2_w_down=0)
STAGE_KEYS = dict(
    A=(("ffn1_w_gu", 0), ("ffn1_w_down", 0)),
    B=(("even_w_in", 0), ("even_w_out", 0), ("xa_wq", 0), ("xa_wkv", 0), ("xa_wo", 0)),
    C=(("ffn2_w_gu", 0), ("ffn2_w_down", 0)),
    D=(("ffn1_w_gu", 1), ("ffn1_w_down", 1), ("odd_w_in", 0), ("odd_w_out", 0), ("xa_wq", 1), ("xa_wkv", 1), ("xa_wo", 1),
       ("ffn2_w_gu", 1), ("ffn2_w_down", 1)))


def _params(n_axes):
    return pltpu.CompilerParams(dimension_semantics=("arbitrary",) * n_axes, vmem_limit_bytes=VMEM_LIMIT)


def _tile(n, pref):
    t = min(n, pref)
    assert n % t == 0, (n, pref)
    return t


def _dot(a, b):
    return jnp.dot(a, b, preferred_element_type=F32)


def _dot_nt(a, b):
    return lax.dot_general(a, b, (((1,), (1,)), ((), ())), preferred_element_type=F32)


def _dot_tn(a, b):
    return lax.dot_general(a, b, (((0,), (0,)), ((), ())), preferred_element_type=F32)


def _sigmoid(x):
    return 0.5 * jnp.tanh(0.5 * x) + 0.5


ANY_SPEC = pl.BlockSpec(memory_space=pl.ANY)


def _with_dep(body, n_in, dep):
    if dep is None:
        return body, [], []
    return (lambda *refs: body(*refs[:n_in], *refs[n_in + 1:])), [ANY_SPEC], [dep]


def rmsnorm(h, g, name, tm=1024, dep=None):
    T, K = h.shape
    tm = _tile(T, tm)

    def kern(h_ref, g_ref, u_ref):
        x = h_ref[...]
        r = lax.rsqrt(jnp.mean(x * x, axis=-1, keepdims=True) + RMS_EPS)
        u_ref[...] = ((x * r) * g_ref[...]).astype(BF16)

    body, dep_spec, dep_arg = _with_dep(kern, 2, dep)
    return pl.pallas_call(
        body, name=name, grid=(T // tm,),
        in_specs=[pl.BlockSpec((tm, K), lambda i: (i, 0)), pl.BlockSpec((1, K), lambda i: (0, 0))] + dep_spec,
        out_specs=pl.BlockSpec((tm, K), lambda i: (i, 0)),
        out_shape=jax.ShapeDtypeStruct((T, K), BF16),
        compiler_params=_params(1),
    )(h, g, *dep_arg)


def matmul(a, w, out_dtype, name, tn, tm=2048, transposed=True):
    T, K = a.shape
    N = w.shape[0] if transposed else w.shape[1]
    tm = _tile(T, tm)
    mm = _dot_nt if transposed else _dot

    def body(a_ref, w_ref, z_ref):
        z_ref[...] = mm(a_ref[...], w_ref[...]).astype(z_ref.dtype)

    w_spec = pl.BlockSpec((tn, K), lambda i, j: (j, 0)) if transposed else pl.BlockSpec((K, tn), lambda i, j: (0, j))
    return pl.pallas_call(
        body, name=name, grid=(T // tm, N // tn),
        in_specs=[pl.BlockSpec((tm, K), lambda i, j: (i, 0)), w_spec],
        out_specs=pl.BlockSpec((tm, tn), lambda i, j: (i, j)),
        out_shape=jax.ShapeDtypeStruct((T, N), out_dtype),
        compiler_params=_params(2),
    )(a, w)


def norm_matmul(h, g, w, out_dtype, name, tn, dep=None, transposed=True):
    u = rmsnorm(h, g, name + "_norm", dep=dep)
    return matmul(u, w, out_dtype, name, tn, transposed=transposed), u


def matmul_residual(a, w, res, name, tm=1024):
    T, K = a.shape
    N = w.shape[1]
    tm = _tile(T, tm)

    def body(a_ref, w_ref, r_ref, o_ref):
        o_ref[...] = r_ref[...] + _dot(a_ref[...], w_ref[...])

    return pl.pallas_call(
        body, name=name, grid=(T // tm,),
        in_specs=[pl.BlockSpec((tm, K), lambda i: (i, 0)),
                  pl.BlockSpec((K, N), lambda i: (0, 0)),
                  pl.BlockSpec((tm, N), lambda i: (i, 0))],
        out_specs=pl.BlockSpec((tm, N), lambda i: (i, 0)),
        out_shape=jax.ShapeDtypeStruct((T, N), F32),
        compiler_params=_params(1),
    )(a, w, res)


def matmul_nt(dy, w, out_dtype, name, tm=1024):
    T, N = dy.shape
    K = w.shape[0]
    tm = _tile(T, tm)

    def body(dy_ref, w_ref, o_ref):
        o_ref[...] = _dot_nt(dy_ref[...].astype(BF16), w_ref[...]).astype(o_ref.dtype)

    return pl.pallas_call(
        body, name=name, grid=(T // tm,),
        in_specs=[pl.BlockSpec((tm, N), lambda i: (i, 0)),
                  pl.BlockSpec((K, N), lambda i: (0, 0))],
        out_specs=pl.BlockSpec((tm, K), lambda i: (i, 0)),
        out_shape=jax.ShapeDtypeStruct((T, K), out_dtype),
        compiler_params=_params(1),
    )(dy, w)


def matmul_norm_bwd(dz, w, h, g, dh_in, name, tm=512, transposed=True, dep=None):
    T, N = dz.shape
    K = h.shape[1]
    tm = _tile(T, tm)

    def kern(dz_ref, w_ref, h_ref, g_ref, dhin_ref, dh_ref, dg_ref):
        @pl.when(pl.program_id(0) == 0)
        def _():
            dg_ref[...] = jnp.zeros_like(dg_ref)

        mm = _dot if transposed else _dot_nt
        du = mm(dz_ref[...], w_ref[...])
        x = h_ref[...]
        r = lax.rsqrt(jnp.mean(x * x, axis=-1, keepdims=True) + RMS_EPS)
        xh = x * r
        dg_ref[...] += jnp.sum(du * xh, axis=0, keepdims=True)
        dxh = du * g_ref[...]
        dh_ref[...] = dhin_ref[...] + r * (dxh - xh * jnp.mean(dxh * xh, axis=-1, keepdims=True))

    body, dep_spec, dep_arg = _with_dep(kern, 5, dep)
    return pl.pallas_call(
        body, name=name, grid=(T // tm,),
        in_specs=[pl.BlockSpec((tm, N), lambda i: (i, 0)),
                  pl.BlockSpec(w.shape, lambda i: (0, 0)),
                  pl.BlockSpec((tm, K), lambda i: (i, 0)),
                  pl.BlockSpec((1, K), lambda i: (0, 0)),
                  pl.BlockSpec((tm, K), lambda i: (i, 0))] + dep_spec,
        out_specs=[pl.BlockSpec((tm, K), lambda i: (i, 0)),
                   pl.BlockSpec((1, K), lambda i: (0, 0))],
        out_shape=[jax.ShapeDtypeStruct((T, K), F32), jax.ShapeDtypeStruct((1, K), F32)],
        compiler_params=_params(1),
    )(dz, w, h, g, dh_in, *dep_arg)


def matmul_tn(x, dy, name, scale=1.0, tk=None, tn=None, tt=512):
    T, K = x.shape
    N = dy.shape[1]
    tk = K if tk is None else tk
    tn = N if tn is None else tn
    tt = _tile(T, tt)
    nt = T // tt

    def body(x_ref, dy_ref, o_ref, acc_ref):
        t = pl.program_id(2)

        @pl.when(t == 0)
        def _():
            acc_ref[...] = jnp.zeros_like(acc_ref)

        acc_ref[...] += _dot_tn(x_ref[...].astype(BF16), dy_ref[...].astype(BF16))

        @pl.when(t == nt - 1)
        def _():
            o_ref[...] = (acc_ref[...] * scale).astype(o_ref.dtype)

    return pl.pallas_call(
        body, name=name, grid=(K // tk, N // tn, nt),
        in_specs=[pl.BlockSpec((tt, tk), lambda a, b, t: (t, a)),
                  pl.BlockSpec((tt, tn), lambda a, b, t: (t, b))],
        out_specs=pl.BlockSpec((tk, tn), lambda a, b, t: (a, b)),
        out_shape=jax.ShapeDtypeStruct((K, N), BF16),
        scratch_shapes=[pltpu.VMEM((tk, tn), F32)],
        compiler_params=_params(3),
    )(x, dy)


def ffn_down(gu, wd, res, name, tm=512):
    T = gu.shape[0]
    F = gu.shape[1] // 2
    N = wd.shape[1]
    tm = _tile(T, tm)

    def body(g_ref, up_ref, w_ref, r_ref, o_ref, a_ref):
        g = g_ref[...].astype(F32)
        a_ref[...] = ((g * _sigmoid(g)) * up_ref[...].astype(F32)).astype(BF16)
        o_ref[...] = r_ref[...] + 0.5 * _dot(a_ref[...], w_ref[...])

    return pl.pallas_call(
        body, name=name, grid=(T // tm,),
        in_specs=[pl.BlockSpec((tm, F), lambda i: (i, 0)),
                  pl.BlockSpec((tm, F), lambda i: (i, 1)),
                  pl.BlockSpec((F, N), lambda i: (0, 0)),
                  pl.BlockSpec((tm, N), lambda i: (i, 0))],
        out_specs=[pl.BlockSpec((tm, N), lambda i: (i, 0)),
                   pl.BlockSpec((tm, F), lambda i: (i, 0))],
        out_shape=[jax.ShapeDtypeStruct((T, N), F32), jax.ShapeDtypeStruct((T, F), BF16)],
        compiler_params=_params(1),
    )(gu, gu, wd, res)


def ffn_down_bwd(dy, wd, gu, name, tm=512, dep=None):
    T, N = dy.shape
    F = wd.shape[0]
    tm = _tile(T, tm)

    def kern(dy_ref, w_ref, g_ref, up_ref, o_ref):
        da = 0.5 * _dot_nt(dy_ref[...].astype(BF16), w_ref[...])
        g = g_ref[...].astype(F32)
        up = up_ref[...].astype(F32)
        s = _sigmoid(g)
        o_ref[:, :F] = (da * up * (s * (1.0 + g * (1.0 - s)))).astype(BF16)
        o_ref[:, F:] = (da * (g * s)).astype(BF16)

    body, dep_spec, dep_arg = _with_dep(kern, 4, dep)
    return pl.pallas_call(
        body, name=name, grid=(T // tm,),
        in_specs=[pl.BlockSpec((tm, N), lambda i: (i, 0)),
                  pl.BlockSpec((F, N), lambda i: (0, 0)),
                  pl.BlockSpec((tm, F), lambda i: (i, 0)),
                  pl.BlockSpec((tm, F), lambda i: (i, 1))] + dep_spec,
        out_specs=pl.BlockSpec((tm, 2 * F), lambda i: (i, 0)),
        out_shape=jax.ShapeDtypeStruct((T, 2 * F), BF16),
        compiler_params=_params(1),
    )(dy, wd, gu, gu, *dep_arg)


def ffn_forward(h, g, w_gu, w_down, name, dep=None):
    gu, u = norm_matmul(h, g, w_gu, BF16, name + "_gu", D_FF // 2, dep=dep)
    h_out, a = ffn_down(gu, w_down, h, name + "_down")
    return h_out, (h, u, gu, a)


def ffn_backward(dy, saved, g, w_gu, w_down, name, dep=None, emit=None):
    h, u, gu, a = saved
    dgu = ffn_down_bwd(dy, w_down, gu, name + "_ddown", dep=dep)
    d_w_down = matmul_tn(a, dy, name + "_dwd", scale=0.5, tk=D_FF // 2)
    d_w_gu = matmul_tn(dgu, u, name + "_dwgu", tk=D_FF)
    dh, dg = matmul_norm_bwd(dgu, w_gu, h, g, dy, name + "_dx", dep=emit(d_w_gu, d_w_down))
    return dh, dg


def _shifted_taps(ref, phase_ref, n_rows):
    for r in range(1, 8):
        phase_ref[r - 1] = ref[pl.ds(r, n_rows - 8), :]

    def taps(o, n):
        q, r = divmod(o, 8)
        return ref[8 * q:8 * q + n, :] if r == 0 else phase_ref[r - 1, 8 * q:8 * q + n, :]

    return taps


def conformer_conv_fwd(z, cw, cb, lg, lb, name, tm=512):
    T = z.shape[0]
    C = CONV_A_CH
    tm = _tile(T, tm)
    hb = tm // CONV_HALO

    def body(v_ref, gt_ref, pv_ref, pg_ref, cw_ref, cb_ref, lg_ref, lb_ref, o_ref, xs_ref, xph_ref):
        i = pl.program_id(0)
        prev = pv_ref[...] * _sigmoid(pg_ref[...])
        xs_ref[0:CONV_HALO, :] = jnp.where(i > 0, prev, 0.0)
        xs_ref[CONV_HALO:, :] = v_ref[...] * _sigmoid(gt_ref[...])
        taps = _shifted_taps(xs_ref, xph_ref, tm + CONV_HALO)
        acc = jnp.zeros((tm, C), F32) + cb_ref[...]
        for k in range(CONV_A_WIDTH):
            acc = acc + cw_ref[k:k + 1, :] * taps(CONV_HALO - (CONV_A_WIDTH - 1) + k, tm)
        mu = jnp.mean(acc, axis=-1, keepdims=True)
        xc = acc - mu
        var = jnp.mean(xc * xc, axis=-1, keepdims=True)
        y = (xc * lax.rsqrt(var + LN_EPS)) * lg_ref[...] + lb_ref[...]
        o_ref[...] = (y * _sigmoid(y)).astype(BF16)

    return pl.pallas_call(
        body, name=name, grid=(T // tm,),
        in_specs=[pl.BlockSpec((tm, C), lambda i: (i, 0)),
                  pl.BlockSpec((tm, C), lambda i: (i, 1)),
                  pl.BlockSpec((CONV_HALO, C), lambda i: (jnp.maximum(i * hb - 1, 0), 0)),
                  pl.BlockSpec((CONV_HALO, C), lambda i: (jnp.maximum(i * hb - 1, 0), 1)),
                  pl.BlockSpec((32, C), lambda i: (0, 0)),
                  pl.BlockSpec((1, C), lambda i: (0, 0)),
                  pl.BlockSpec((1, C), lambda i: (0, 0)),
                  pl.BlockSpec((1, C), lambda i: (0, 0))],
        out_specs=pl.BlockSpec((tm, C), lambda i: (i, 0)),
        out_shape=jax.ShapeDtypeStruct((T, C), BF16),
        scratch_shapes=[pltpu.VMEM((tm + CONV_HALO, C), F32), pltpu.VMEM((7, tm + CONV_HALO - 8, C), F32)],
        compiler_params=_params(1),
    )(z, z, z, z, cw, cb, lg, lb)


def conformer_conv_bwd(z, dm, cw, cb, lg, lb, name, tm=512):
    T = z.shape[0]
    C = CONV_A_CH
    tm = _tile(T, tm)
    hb = tm // CONV_HALO
    n_tiles = T // tm
    last_halo = T // CONV_HALO - 1
    R = tm + CONV_HALO
    KW = CONV_A_WIDTH

    def body(v_ref, gt_ref, pv_ref, pg_ref, nv_ref, ng_ref, do_ref, ndo_ref, cw_ref, cb_ref, lg_ref, lb_ref,
             dz_ref, dcw_ref, dcb_ref, dlg_ref, dlb_ref, xs_ref, ds_ref, xph_ref, dph_ref):
        i = pl.program_id(0)

        @pl.when(i == 0)
        def _():
            dcw_ref[...] = jnp.zeros_like(dcw_ref)
            dcb_ref[...] = jnp.zeros_like(dcb_ref)
            dlg_ref[...] = jnp.zeros_like(dlg_ref)
            dlb_ref[...] = jnp.zeros_like(dlb_ref)

        val = v_ref[...]
        sg = _sigmoid(gt_ref[...])
        prev = pv_ref[...] * _sigmoid(pg_ref[...])
        xs_ref[0:CONV_HALO, :] = jnp.where(i > 0, prev, 0.0)
        xs_ref[CONV_HALO:CONV_HALO + tm, :] = val * sg
        xs_ref[CONV_HALO + tm:, :] = nv_ref[...] * _sigmoid(ng_ref[...])

        x_taps = _shifted_taps(xs_ref, xph_ref, tm + 2 * CONV_HALO)
        acc = jnp.zeros((R, C), F32) + cb_ref[...]
        for k in range(KW):
            acc = acc + cw_ref[k:k + 1, :] * x_taps(CONV_HALO - (KW - 1) + k, R)
        mu = jnp.mean(acc, axis=-1, keepdims=True)
        xc = acc - mu
        rstd = lax.rsqrt(jnp.mean(xc * xc, axis=-1, keepdims=True) + LN_EPS)
        xh = xc * rstd
        y = xh * lg_ref[...] + lb_ref[...]
        s = _sigmoid(y)
        dout = jnp.concatenate([do_ref[...], jnp.where(i < n_tiles - 1, ndo_ref[...], 0.0)], axis=0)
        dy = dout * (s * (1.0 + y * (1.0 - s)))
        dxh = dy * lg_ref[...]
        dconv = rstd * (dxh - jnp.mean(dxh, axis=-1, keepdims=True) - xh * jnp.mean(dxh * xh, axis=-1, keepdims=True))
        ds_ref[...] = dconv

        dy_m = dy[:tm]
        dlg_ref[...] += jnp.sum(dy_m * xh[:tm], axis=0, keepdims=True)
        dlb_ref[...] += jnp.sum(dy_m, axis=0, keepdims=True)
        dc_m = dconv[:tm]
        dcb_ref[...] += jnp.sum(dc_m, axis=0, keepdims=True)
        d_taps = _shifted_taps(ds_ref, dph_ref, R)
        dglu = jnp.zeros((tm, C), F32)
        for k in range(KW):
            dcw_ref[k:k + 1, :] += jnp.sum(dc_m * x_taps(CONV_HALO - (KW - 1) + k, tm), axis=0, keepdims=True)
            dglu = dglu + cw_ref[k:k + 1, :] * d_taps(KW - 1 - k, tm)
        dz_ref[:, :C] = (dglu * sg).astype(BF16)
        dz_ref[:, C:] = (dglu * val * sg * (1.0 - sg)).astype(BF16)

    prev_map = lambda i: jnp.maximum(i * hb - 1, 0)
    next_map = lambda i: jnp.minimum((i + 1) * hb, last_halo)
    return pl.pallas_call(
        body, name=name, grid=(n_tiles,),
        in_specs=[pl.BlockSpec((tm, C), lambda i: (i, 0)),
                  pl.BlockSpec((tm, C), lambda i: (i, 1)),
                  pl.BlockSpec((CONV_HALO, C), lambda i: (prev_map(i), 0)),
                  pl.BlockSpec((CONV_HALO, C), lambda i: (prev_map(i), 1)),
                  pl.BlockSpec((CONV_HALO, C), lambda i: (next_map(i), 0)),
                  pl.BlockSpec((CONV_HALO, C), lambda i: (next_map(i), 1)),
                  pl.BlockSpec((tm, C), lambda i: (i, 0)),
                  pl.BlockSpec((CONV_HALO, C), lambda i: (next_map(i), 0)),
                  pl.BlockSpec((32, C), lambda i: (0, 0)),
                  pl.BlockSpec((1, C), lambda i: (0, 0)),
                  pl.BlockSpec((1, C), lambda i: (0, 0)),
                  pl.BlockSpec((1, C), lambda i: (0, 0))],
        out_specs=[pl.BlockSpec((tm, 2 * C), lambda i: (i, 0)),
                   pl.BlockSpec((32, C), lambda i: (0, 0)),
                   pl.BlockSpec((1, C), lambda i: (0, 0)),
                   pl.BlockSpec((1, C), lambda i: (0, 0)),
                   pl.BlockSpec((1, C), lambda i: (0, 0))],
        out_shape=[jax.ShapeDtypeStruct((T, 2 * C), BF16),
                   jax.ShapeDtypeStruct((32, C), F32),
                   jax.ShapeDtypeStruct((1, C), F32),
                   jax.ShapeDtypeStruct((1, C), F32),
                   jax.ShapeDtypeStruct((1, C), F32)],
        scratch_shapes=[pltpu.VMEM((tm + 2 * CONV_HALO, C), F32), pltpu.VMEM((R, C), F32),
                        pltpu.VMEM((7, tm + 2 * CONV_HALO - 8, C), F32), pltpu.VMEM((7, R - 8, C), F32)],
        compiler_params=_params(1),
    )(z, z, z, z, z, z, dm, dm, cw, cb, lg, lb)


def _swa_scores(q_h, kk_h, slope, bias_dist, valid, sink):
    s = _dot_nt(q_h, kk_h) * (HEAD_DIM ** -0.5) - slope * bias_dist
    s = jnp.where(valid, s, NEG_BIG)
    m = jnp.maximum(jnp.max(s, axis=-1, keepdims=True), sink)
    p = jnp.exp(s - m)
    e_sink = jnp.exp(sink - m)
    inv = 1.0 / (jnp.sum(p, axis=-1, keepdims=True) + e_sink)
    return p * inv, e_sink * inv


def _swa_mask(r0):
    qi = lax.broadcasted_iota(jnp.int32, (WINDOW, 2 * WINDOW), 0)
    kj = lax.broadcasted_iota(jnp.int32, (WINDOW, 2 * WINDOW), 1)
    dist = qi + WINDOW - kj
    valid = (dist >= 0) & (dist < WINDOW) & (r0 - WINDOW + kj >= 0)
    return dist.astype(F32), valid


def swa_fwd(z, kpad, vpad, sinks, name, tq=512):
    T = z.shape[0]
    tq = _tile(T, tq)
    HQ = SWA_HEADS * HEAD_DIM

    def body(sink_ref, q_ref, k_ref, v_ref, o_ref):
        i = pl.program_id(0)
        for sub in range(tq // WINDOW):
            r0 = pl.multiple_of(i * tq + sub * WINDOW, WINDOW)
            kk = k_ref[pl.ds(r0, 2 * WINDOW), :]
            vv = v_ref[pl.ds(r0, 2 * WINDOW), :]
            qb = q_ref[sub * WINDOW:(sub + 1) * WINDOW, :].astype(BF16)
            dist, valid = _swa_mask(r0)
            outs = []
            for h in range(SWA_HEADS):
                kh = h // SWA_GROUP
                ks = slice(kh * HEAD_DIM, (kh + 1) * HEAD_DIM)
                pn, _ = _swa_scores(qb[:, h * HEAD_DIM:(h + 1) * HEAD_DIM], kk[:, ks], 2.0 ** (-(h + 1)), dist, valid,
                                    sink_ref[h])
                outs.append(_dot(pn.astype(BF16), vv[:, ks]))
            o_ref[sub * WINDOW:(sub + 1) * WINDOW, :] = jnp.concatenate(outs, axis=-1).astype(BF16)

    return pl.pallas_call(
        body, name=name, grid=(T // tq,),
        in_specs=[pl.BlockSpec(memory_space=pltpu.SMEM),
                  pl.BlockSpec((tq, HQ), lambda i: (i, 2)),
                  pl.BlockSpec((T + WINDOW, 2 * HEAD_DIM), lambda i: (0, 0)),
                  pl.BlockSpec((T + WINDOW, 2 * HEAD_DIM), lambda i: (0, 0))],
        out_specs=pl.BlockSpec((tq, HQ), lambda i: (i, 0)),
        out_shape=jax.ShapeDtypeStruct((T, HQ), BF16),
        compiler_params=_params(1),
    )(sinks, z, kpad, vpad)


def swa_bwd(z, kpad, vpad, sinks, dm, name, tq=512):
    T = z.shape[0]
    tq = _tile(T, tq)
    HQ = SWA_HEADS * HEAD_DIM
    scale = HEAD_DIM ** -0.5

    def body(sink_ref, q_ref, k_ref, v_ref, do_ref, dq_ref, dk_ref, dv_ref, dsink_ref):
        i = pl.program_id(0)

        @pl.when(i == 0)
        def _():
            dk_ref[...] = jnp.zeros_like(dk_ref)
            dv_ref[...] = jnp.zeros_like(dv_ref)
            dsink_ref[...] = jnp.zeros_like(dsink_ref)

        for sub in range(tq // WINDOW):
            r0 = pl.multiple_of(i * tq + sub * WINDOW, WINDOW)
            kk = k_ref[pl.ds(r0, 2 * WINDOW), :]
            vv = v_ref[pl.ds(r0, 2 * WINDOW), :]
            rows = slice(sub * WINDOW, (sub + 1) * WINDOW)
            qb = q_ref[rows, :].astype(BF16)
            dob = do_ref[rows, :].astype(BF16)
            dist, valid = _swa_mask(r0)
            dqs, dks, dvs = [], [], []
            for kh in range(SWA_KV_HEADS):
                ks = slice(kh * HEAD_DIM, (kh + 1) * HEAD_DIM)
                dk_acc = jnp.zeros((2 * WINDOW, HEAD_DIM), F32)
                dv_acc = jnp.zeros((2 * WINDOW, HEAD_DIM), F32)
                for g in range(SWA_GROUP):
                    h = kh * SWA_GROUP + g
                    hs = slice(h * HEAD_DIM, (h + 1) * HEAD_DIM)
                    pn, p_sink = _swa_scores(qb[:, hs], kk[:, ks], 2.0 ** (-(h + 1)), dist, valid, sink_ref[h])
                    dp = _dot_nt(dob[:, hs], vv[:, ks])
                    delta = jnp.sum(pn * dp, axis=-1, keepdims=True)
                    ds = (pn * (dp - delta)).astype(BF16)
                    dqs.append(_dot(ds, kk[:, ks]) * scale)
                    dk_acc = dk_acc + _dot_tn(ds, qb[:, hs]) * scale
                    dv_acc = dv_acc + _dot_tn(pn.astype(BF16), dob[:, hs])
                    dsink_ref[h:h + 1, :] += jnp.zeros((1, 128), F32) - jnp.sum(p_sink * delta)
                dks.append(dk_acc)
                dvs.append(dv_acc)
            dq_ref[rows, :] = jnp.concatenate(dqs, axis=-1).astype(BF16)
            dk_ref[pl.ds(r0, 2 * WINDOW), :] += jnp.concatenate(dks, axis=-1)
            dv_ref[pl.ds(r0, 2 * WINDOW), :] += jnp.concatenate(dvs, axis=-1)

    kv_spec = pl.BlockSpec((T + WINDOW, 2 * HEAD_DIM), lambda i: (0, 0))
    return pl.pallas_call(
        body, name=name, grid=(T // tq,),
        in_specs=[pl.BlockSpec(memory_space=pltpu.SMEM),
                  pl.BlockSpec((tq, HQ), lambda i: (i, 2)),
                  kv_spec, kv_spec,
                  pl.BlockSpec((tq, HQ), lambda i: (i, 1))],
        out_specs=[pl.BlockSpec((tq, HQ), lambda i: (i, 0)),
                   kv_spec, kv_spec,
                   pl.BlockSpec((SWA_HEADS, 128), lambda i: (0, 0))],
        out_shape=[jax.ShapeDtypeStruct((T, HQ), BF16),
                   jax.ShapeDtypeStruct((T + WINDOW, 2 * HEAD_DIM), F32),
                   jax.ShapeDtypeStruct((T + WINDOW, 2 * HEAD_DIM), F32),
                   jax.ShapeDtypeStruct((SWA_HEADS, 128), F32)],
        compiler_params=_params(1),
    )(sinks, z, kpad, vpad, dm)


def short_conv_fwd(z, w, name, tm=512):
    T = z.shape[0]
    C = SC_CH
    tm = _tile(T, tm)
    hb = tm // SC_HALO

    def body(b_ref, c_ref, v_ref, pc_ref, pv_ref, w_ref, o_ref, xs_ref):
        i = pl.program_id(0)
        xs_ref[0:SC_HALO, :] = jnp.where(i > 0, pc_ref[...] * pv_ref[...], 0.0)
        xs_ref[SC_HALO:, :] = c_ref[...] * v_ref[...]
        conv = jnp.zeros((tm, C), F32)
        for k in range(3):
            conv = conv + w_ref[k:k + 1, :] * xs_ref[pl.ds(SC_HALO - 2 + k, tm), :]
        o_ref[...] = (b_ref[...] * conv).astype(BF16)

    prev_map = lambda i: jnp.maximum(i * hb - 1, 0)
    return pl.pallas_call(
        body, name=name, grid=(T // tm,),
        in_specs=[pl.BlockSpec((tm, C), lambda i: (i, 0)),
                  pl.BlockSpec((tm, C), lambda i: (i, 1)),
                  pl.BlockSpec((tm, C), lambda i: (i, 2)),
                  pl.BlockSpec((SC_HALO, C), lambda i: (prev_map(i), 1)),
                  pl.BlockSpec((SC_HALO, C), lambda i: (prev_map(i), 2)),
                  pl.BlockSpec((8, C), lambda i: (0, 0))],
        out_specs=pl.BlockSpec((tm, C), lambda i: (i, 0)),
        out_shape=jax.ShapeDtypeStruct((T, C), BF16),
        scratch_shapes=[pltpu.VMEM((tm + SC_HALO, C), F32)],
        compiler_params=_params(1),
    )(z, z, z, z, z, w)


def short_conv_bwd(z, dm, w, name, tm=512):
    T = z.shape[0]
    C = SC_CH
    tm = _tile(T, tm)
    hb = tm // SC_HALO
    n_tiles = T // tm
    last_halo = T // SC_HALO - 1
    R = tm + SC_HALO

    def body(b_ref, c_ref, v_ref, pc_ref, pv_ref, nb_ref, do_ref, ndo_ref, w_ref, dz_ref, dw_ref, xs_ref, ds_ref):
        i = pl.program_id(0)

        @pl.when(i == 0)
        def _():
            dw_ref[...] = jnp.zeros_like(dw_ref)

        c = c_ref[...]
        val = v_ref[...]
        dout = do_ref[...]
        xs_ref[0:SC_HALO, :] = jnp.where(i > 0, pc_ref[...] * pv_ref[...], 0.0)
        xs_ref[SC_HALO:, :] = c * val
        dconv = dout * b_ref[...]
        ds_ref[0:tm, :] = dconv
        ds_ref[tm:, :] = jnp.where(i < n_tiles - 1, ndo_ref[...] * nb_ref[...], 0.0)
        conv = jnp.zeros((tm, C), F32)
        dcv = jnp.zeros((tm, C), F32)
        for k in range(3):
            xk = xs_ref[pl.ds(SC_HALO - 2 + k, tm), :]
            conv = conv + w_ref[k:k + 1, :] * xk
            dw_ref[k:k + 1, :] += jnp.sum(dconv * xk, axis=0, keepdims=True)
            dcv = dcv + w_ref[k:k + 1, :] * ds_ref[pl.ds(2 - k, tm), :]
        dz_ref[:, 0:C] = (dout * conv).astype(BF16)
        dz_ref[:, C:2 * C] = (dcv * val).astype(BF16)
        dz_ref[:, 2 * C:] = (dcv * c).astype(BF16)

    prev_map = lambda i: jnp.maximum(i * hb - 1, 0)
    next_map = lambda i: jnp.minimum((i + 1) * hb, last_halo)
    return pl.pallas_call(
        body, name=name, grid=(n_tiles,),
        in_specs=[pl.BlockSpec((tm, C), lambda i: (i, 0)),
                  pl.BlockSpec((tm, C), lambda i: (i, 1)),
                  pl.BlockSpec((tm, C), lambda i: (i, 2)),
                  pl.BlockSpec((SC_HALO, C), lambda i: (prev_map(i), 1)),
                  pl.BlockSpec((SC_HALO, C), lambda i: (prev_map(i), 2)),
                  pl.BlockSpec((SC_HALO, C), lambda i: (next_map(i), 0)),
                  pl.BlockSpec((tm, C), lambda i: (i, 0)),
                  pl.BlockSpec((SC_HALO, C), lambda i: (next_map(i), 0)),
                  pl.BlockSpec((8, C), lambda i: (0, 0))],
        out_specs=[pl.BlockSpec((tm, 3 * C), lambda i: (i, 0)),
                   pl.BlockSpec((8, C), lambda i: (0, 0))],
        out_shape=[jax.ShapeDtypeStruct((T, 3 * C), BF16), jax.ShapeDtypeStruct((8, C), F32)],
        scratch_shapes=[pltpu.VMEM((tm + SC_HALO, C), F32), pltpu.VMEM((R, C), F32)],
        compiler_params=_params(1),
    )(z, z, z, z, z, z, dm, dm, w)


def _xa_probs(q_h, k_h):
    s = _dot_nt(q_h, k_h) * (XA_HEAD_DIM ** -0.5)
    p = jnp.exp(s - jnp.max(s, axis=-1, keepdims=True))
    return p * (1.0 / jnp.sum(p, axis=-1, keepdims=True))


def xattn_fwd(q, kv, name, tm=512):
    T = q.shape[0]
    M = kv.shape[0]
    tm = _tile(T, tm)

    def body(q_ref, k_ref, v_ref, o_ref):
        for h in range(XA_HEADS):
            hs = slice(h * XA_HEAD_DIM, (h + 1) * XA_HEAD_DIM)
            p = _xa_probs(q_ref[:, hs], k_ref[:, hs])
            o_ref[:, hs] = _dot(p.astype(BF16), v_ref[:, hs]).astype(BF16)

    return pl.pallas_call(
        body, name=name, grid=(T // tm,),
        in_specs=[pl.BlockSpec((tm, D_MODEL), lambda i: (i, 0)),
                  pl.BlockSpec((M, D_MODEL), lambda i: (0, 0)),
                  pl.BlockSpec((M, D_MODEL), lambda i: (0, 1))],
        out_specs=pl.BlockSpec((tm, D_MODEL), lambda i: (i, 0)),
        out_shape=jax.ShapeDtypeStruct((T, D_MODEL), BF16),
        compiler_params=_params(1),
    )(q, kv, kv)


def xattn_bwd(q, kv, do, name, tm=512):
    T = q.shape[0]
    M = kv.shape[0]
    tm = _tile(T, tm)
    scale = XA_HEAD_DIM ** -0.5

    def body(q_ref, k_ref, v_ref, do_ref, dq_ref, dkv_ref):
        @pl.when(pl.program_id(0) == 0)
        def _():
            dkv_ref[...] = jnp.zeros_like(dkv_ref)

        for h in range(XA_HEADS):
            hs = slice(h * XA_HEAD_DIM, (h + 1) * XA_HEAD_DIM)
            vs = slice(D_MODEL + h * XA_HEAD_DIM, D_MODEL + (h + 1) * XA_HEAD_DIM)
            q_h = q_ref[:, hs]
            do_h = do_ref[:, hs]
            p = _xa_probs(q_h, k_ref[:, hs])
            dp = _dot_nt(do_h, v_ref[:, hs])
            ds = (p * (dp - jnp.sum(p * dp, axis=-1, keepdims=True))).astype(BF16)
            dq_ref[:, hs] = (_dot(ds, k_ref[:, hs]) * scale).astype(BF16)
            dkv_ref[:, hs] += _dot_tn(ds, q_h) * scale
            dkv_ref[:, vs] += _dot_tn(p.astype(BF16), do_h)

    return pl.pallas_call(
        body, name=name, grid=(T // tm,),
        in_specs=[pl.BlockSpec((tm, D_MODEL), lambda i: (i, 0)),
                  pl.BlockSpec((M, D_MODEL), lambda i: (0, 0)),
                  pl.BlockSpec((M, D_MODEL), lambda i: (0, 1)),
                  pl.BlockSpec((tm, D_MODEL), lambda i: (i, 0))],
        out_specs=[pl.BlockSpec((tm, D_MODEL), lambda i: (i, 0)),
                   pl.BlockSpec((M, 2 * D_MODEL), lambda i: (0, 0))],
        out_shape=[jax.ShapeDtypeStruct((T, D_MODEL), BF16), jax.ShapeDtypeStruct((M, 2 * D_MODEL), F32)],
        compiler_params=_params(1),
    )(q, kv, kv, do)


def final_loss(h, g, target, name, tm=512):
    T, K = h.shape
    tm = _tile(T, tm)

    def body(h_ref, g_ref, t_ref, dh_ref, dg_ref, loss_ref):
        @pl.when(pl.program_id(0) == 0)
        def _():
            dg_ref[...] = jnp.zeros_like(dg_ref)
            loss_ref[...] = jnp.zeros_like(loss_ref)

        x = h_ref[...]
        r = lax.rsqrt(jnp.mean(x * x, axis=-1, keepdims=True) + RMS_EPS)
        xh = x * r
        e = xh * g_ref[...] - t_ref[...]
        loss_ref[...] += jnp.zeros((1, 128), F32) + 0.5 * jnp.sum(jnp.mean(e * e, axis=-1, keepdims=True))
        dy = e * (1.0 / K)
        dg_ref[...] += jnp.sum(dy * xh, axis=0, keepdims=True)
        dxh = dy * g_ref[...]
        dh_ref[...] = r * (dxh - xh * jnp.mean(dxh * xh, axis=-1, keepdims=True))

    return pl.pallas_call(
        body, name=name, grid=(T // tm,),
        in_specs=[pl.BlockSpec((tm, K), lambda i: (i, 0)),
                  pl.BlockSpec((1, K), lambda i: (0, 0)),
                  pl.BlockSpec((tm, K), lambda i: (i, 0))],
        out_specs=[pl.BlockSpec((tm, K), lambda i: (i, 0)),
                   pl.BlockSpec((1, K), lambda i: (0, 0)),
                   pl.BlockSpec((1, 128), lambda i: (0, 0))],
        out_shape=[jax.ShapeDtypeStruct((T, K), F32), jax.ShapeDtypeStruct((1, K), F32),
                   jax.ShapeDtypeStruct((1, 128), F32)],
        compiler_params=_params(1),
    )(h, g, target)


def _row(v):
    return v.reshape(1, -1)


def _pad_rows(a, rows):
    return jnp.pad(a, ((0, rows - a.shape[0]), (0, 0)))


def local_step(x, mem, target, P, get_weights, put_grads):
    cw = _pad_rows(P["conv_a_w"], 32)
    scw = _pad_rows(P["sc_conv_w"], 8)
    cb, lg, lb = _row(P["conv_a_b"]), _row(P["conv_a_ln_g"]), _row(P["conv_a_ln_b"])
    sinks = P["swa_sinks"]

    class _Layered:
        def __init__(self, store, name=None):
            self.store, self.name = store, name

        def __getitem__(self, key):
            if self.name is None:
                return self.store[(key, 0)] if key in ("even_w_in", "even_w_out", "odd_w_in", "odd_w_out") \
                    else _Layered(self.store, key)
            return self.store[(self.name, key)]

    store = {}
    W = _Layered(store)
    saved = []
    h = x
    for i in range(2):
        L = f"l{i}"
        new, dep = get_weights("A" if i == 0 else "D", h)
        store.update(new)
        h, s_ffn1 = ffn_forward(h, P["ffn1_norm"][i:i + 1], W["ffn1_w_gu"][i], W["ffn1_w_down"][i], L + "_ffn1", dep=dep)
        h1 = h
        if i == 0:
            new, _ = get_weights("B", h)
            store.update(new)
            z, u2 = norm_matmul(h1, P["mix_norm"][i:i + 1], W["even_w_in"], F32, L + "_mix_in", 896)
            a = conformer_conv_fwd(z, cw, cb, lg, lb, L + "_conv")
            kpad = jnp.pad(z[:, 1536:1664].astype(BF16), ((WINDOW, 0), (0, 0)))
            vpad = jnp.pad(z[:, 1664:1792].astype(BF16), ((WINDOW, 0), (0, 0)))
            o = swa_fwd(z, kpad, vpad, sinks, L + "_swa")
            m = jnp.concatenate([a, o], axis=-1)
            h = matmul_residual(m, W["even_w_out"], h1, L + "_mix_out")
            s_mix = (h1, u2, z, m, kpad, vpad)
        else:
            z, u2 = norm_matmul(h1, P["mix_norm"][i:i + 1], W["odd_w_in"], F32, L + "_mix_in", 1024)
            m = short_conv_fwd(z, scw, L + "_sconv")
            h = matmul_residual(m, W["odd_w_out"], h1, L + "_mix_out")
            s_mix = (h1, u2, z, m)
        h2 = h
        kv, umem = norm_matmul(mem, P["xa_mem_norm"][i:i + 1], W["xa_wkv"][i], BF16, L + "_xa_kv", 2 * D_MODEL)
        q, u3 = norm_matmul(h2, P["xa_norm"][i:i + 1], W["xa_wq"][i], BF16, L + "_xa_q", D_MODEL, transposed=False)
        o = xattn_fwd(q, kv, L + "_xa")
        h = matmul_residual(o, W["xa_wo"][i], h2, L + "_xa_out")
        s_xa = (h2, u3, q, o, kv, umem)
        if i == 0:
            new, _ = get_weights("C", h)
            store.update(new)
        h, s_ffn2 = ffn_forward(h, P["ffn2_norm"][i:i + 1], W["ffn2_w_gu"][i], W["ffn2_w_down"][i], L + "_ffn2")
        saved.append((s_ffn1, s_mix, s_xa, s_ffn2))

    dh, d_final, loss = final_loss(h, _row(P["final_norm"]), target, "final_loss")

    names = ("ffn1_w_gu", "ffn1_w_down", "ffn2_w_gu", "ffn2_w_down", "xa_wq", "xa_wkv", "xa_wo", "even_w_in", "even_w_out",
             "odd_w_in", "odd_w_out")
    dW = {k: [None, None] for k in names}
    dP = {k: [None, None] for k in ("ffn1_norm", "mix_norm", "xa_norm", "xa_mem_norm", "ffn2_norm")}
    dP["final_norm"] = d_final.reshape(-1)
    for i in (1, 0):
        L = f"l{i}b"
        s_ffn1, s_mix, s_xa, s_ffn2 = saved[i]

        def keep_ffn2(d_w_gu, d_w_down, i=i):
            dW["ffn2_w_gu"][i], dW["ffn2_w_down"][i] = d_w_gu, d_w_down

        def send_ffn1(d_w_gu, d_w_down, i=i):
            dW["ffn1_w_gu"][i], dW["ffn1_w_down"][i] = d_w_gu, d_w_down
            stage = "D" if i == 1 else "A"
            return put_grads(stage, {k: dW[k[0]][k[1]] for k in STAGE_KEYS[stage]})

        dh, dP["ffn2_norm"][i] = ffn_backward(
            dh, s_ffn2, P["ffn2_norm"][i:i + 1], W["ffn2_w_gu"][i], W["ffn2_w_down"][i], L + "_ffn2", emit=keep_ffn2)
        h2, u3, q, o, kv, umem = s_xa
        dW["xa_wo"][i] = matmul_tn(o, dh, L + "_xa_dwo")
        do = matmul_nt(dh, W["xa_wo"][i], BF16, L + "_xa_do")
        dq, dkv = xattn_bwd(q, kv, do, L + "_xa")
        dW["xa_wq"][i] = matmul_tn(u3, dq, L + "_xa_dwq")
        dW["xa_wkv"][i] = matmul_tn(dkv, umem, L + "_xa_dwkv", tk=1024)
        dkv_b = dkv.astype(BF16)
        _, dP["xa_mem_norm"][i] = matmul_norm_bwd(dkv_b, W["xa_wkv"][i], mem, P["xa_mem_norm"][i:i + 1],
                                                  jnp.zeros_like(mem), L + "_xa_dmem")
        dh, dP["xa_norm"][i] = matmul_norm_bwd(dq, W["xa_wq"][i], h2, P["xa_norm"][i:i + 1], dh, L + "_xa_dx",
                                               transposed=False)
        if i == 0:
            h1, u2, z, m, kpad, vpad = s_mix
            dW["even_w_out"][0] = matmul_tn(m, dh, L + "_mix_dwo")
            dm = matmul_nt(dh, W["even_w_out"], F32, L + "_mix_dm")
            dz_conv, dcw, dcb, dlg, dlb = conformer_conv_bwd(z, dm, cw, cb, lg, lb, L + "_conv")
            dq_s, dkp, dvp, dsk = swa_bwd(z, kpad, vpad, sinks, dm, L + "_swa")
            dz = jnp.concatenate([dz_conv, dq_s, dkp[WINDOW:].astype(BF16), dvp[WINDOW:].astype(BF16)], axis=-1)
            dW["even_w_in"][0] = matmul_tn(dz, u2, L + "_mix_dwi", tk=896)
            dh, dP["mix_norm"][i] = matmul_norm_bwd(dz, W["even_w_in"], h1, P["mix_norm"][i:i + 1], dh, L + "_mix_dx")
            dP["conv_a_w"] = dcw[:CONV_A_WIDTH]
            dP["conv_a_b"], dP["conv_a_ln_g"], dP["conv_a_ln_b"] = dcb.reshape(-1), dlg.reshape(-1), dlb.reshape(-1)
            dP["swa_sinks"] = dsk[:, 0]
        else:
            h1, u2, z, m = s_mix
            dW["odd_w_out"][0] = matmul_tn(m, dh, L + "_mix_dwo")
            dm = matmul_nt(dh, W["odd_w_out"], F32, L + "_mix_dm")
            dz, dscw = short_conv_bwd(z, dm, scw, L + "_sconv")
            dW["odd_w_in"][0] = matmul_tn(dz, u2, L + "_mix_dwi", tk=1024)
            dh, dP["mix_norm"][i] = matmul_norm_bwd(dz, W["odd_w_in"], h1, P["mix_norm"][i:i + 1], dh, L + "_mix_dx")
            dP["sc_conv_w"] = dscw[:3]
        dep = put_grads("BC", {k: dW[k[0]][k[1]] for k in STAGE_KEYS["B"] + STAGE_KEYS["C"]}) if i == 0 else None
        dh, dP["ffn1_norm"][i] = ffn_backward(
            dh, s_ffn1, P["ffn1_norm"][i:i + 1], W["ffn1_w_gu"][i], W["ffn1_w_down"][i], L + "_ffn1", dep=dep,
            emit=send_ffn1)
    for k in ("ffn1_norm", "mix_norm", "xa_norm", "xa_mem_norm", "ffn2_norm"):
        dP[k] = jnp.concatenate(dP[k], axis=0)
    return loss, dh, dP


def _mesh_pos():
    return lax.axis_index("x"), lax.axis_index("y"), lax.axis_index("c")


def _flat_index(px, py, pc):
    return 4 * px + 2 * py + pc


def all_gather(blob, name):
    R, C = blob.shape

    def body(x_ref, out_ref, send_sems, recv_sems, local_sem):
        x, y, c = _mesh_pos()
        me, sibling = (x, y, c), (x, y, 1 - c)
        chips = [(1 - x, y), (x, 1 - y), (1 - x, 1 - y)]

        def slot(px, py, pc):
            return out_ref.at[_flat_index(px, py, pc)]

        def copy(k, block, to, src=None):
            return pltpu.make_async_remote_copy(
                src_ref=slot(*block) if src is None else src, dst_ref=slot(*block),
                send_sem=send_sems.at[k], recv_sem=recv_sems.at[k],
                device_id=to, device_id_type=pl.DeviceIdType.MESH)

        mine = pltpu.make_async_copy(x_ref, slot(*me), local_sem)
        mine.start()
        first = [copy(0, me, sibling, src=x_ref)]
        first += [copy(1 + j, me, (*chip, c), src=x_ref) for j, chip in enumerate(chips)]
        for cp in first:
            cp.start()
        passed = [copy(4 + j, (*chip, c), sibling) for j, chip in enumerate(chips)]
        for j, chip in enumerate(chips):
            copy(1 + j, (*chip, c), me).wait_recv()
            passed[j].start()
        copy(0, sibling, me).wait_recv()
        for j, chip in enumerate(chips):
            copy(4 + j, (*chip, 1 - c), me).wait_recv()
        for cp in first + passed:
            cp.wait_send()
        mine.wait()

    return pl.pallas_call(
        body, name=name,
        out_shape=jax.ShapeDtypeStruct((N_DEV, R, C), blob.dtype),
        in_specs=[pl.BlockSpec(memory_space=pl.ANY)],
        out_specs=pl.BlockSpec(memory_space=pl.ANY),
        scratch_shapes=[pltpu.SemaphoreType.DMA((7,)), pltpu.SemaphoreType.DMA((7,)), pltpu.SemaphoreType.DMA],
    )(blob)


def scatter_exchange(g, name):
    _, R, C = g.shape

    def body(g_ref, out_ref, send_sems, recv_sems, local_sem):
        x, y, c = _mesh_pos()
        me_idx = _flat_index(x, y, c)
        mine = pltpu.make_async_copy(g_ref.at[me_idx], out_ref.at[me_idx], local_sem)
        mine.start()
        sends, peers = [], []
        for k in range(1, N_DEV):
            px = 1 - x if k & 4 else x
            py = 1 - y if k & 2 else y
            pc = 1 - c if k & 1 else c
            peer_idx = _flat_index(px, py, pc)
            cp = pltpu.make_async_remote_copy(
                src_ref=g_ref.at[peer_idx], dst_ref=out_ref.at[me_idx],
                send_sem=send_sems.at[k - 1], recv_sem=recv_sems.at[k - 1],
                device_id=(px, py, pc), device_id_type=pl.DeviceIdType.MESH)
            cp.start()
            sends.append(cp)
            peers.append((peer_idx, (px, py, pc)))
        for k in range(1, N_DEV):
            peer_idx, peer = peers[k - 1]
            pltpu.make_async_remote_copy(
                src_ref=g_ref.at[me_idx], dst_ref=out_ref.at[peer_idx],
                send_sem=send_sems.at[k - 1], recv_sem=recv_sems.at[k - 1],
                device_id=peer, device_id_type=pl.DeviceIdType.MESH).wait_recv()
        for cp in sends:
            cp.wait_send()
        mine.wait()

    return pl.pallas_call(
        body, name=name,
        out_shape=jax.ShapeDtypeStruct(g.shape, g.dtype),
        in_specs=[pl.BlockSpec(memory_space=pl.ANY)],
        out_specs=pl.BlockSpec(memory_space=pl.ANY),
        scratch_shapes=[pltpu.SemaphoreType.DMA((7,)), pltpu.SemaphoreType.DMA((7,)), pltpu.SemaphoreType.DMA],
    )(g)


HBM_SPEC = pl.BlockSpec(memory_space=pltpu.HBM)
SEM_SPEC = pl.BlockSpec(memory_space=pltpu.SEMAPHORE)
DATAFLOW_EFFECT = pltpu.SideEffectType.DATAFLOW_SIDE_EFFECTING


def _peers(x, y, c):
    out = []
    for k in range(1, N_DEV):
        pos = (1 - x if k & 4 else x, 1 - y if k & 2 else y, 1 - c if k & 1 else c)
        out.append((_flat_index(*pos), pos))
    return out


def _exchange_copy(src_ref, land_ref, send_sems, recv_sems, j, me, peer_idx, peer, scatter):
    return pltpu.make_async_remote_copy(
        src_ref=src_ref.at[peer_idx] if scatter else src_ref, dst_ref=land_ref.at[me],
        send_sem=send_sems.at[j], recv_sem=recv_sems.at[j], device_id=peer, device_id_type=pl.DeviceIdType.MESH)


def exchange_start(srcs, lands, scatter, after, name):
    n = len(srcs)
    n_after = len(after)

    def body(*refs):
        src_refs, land_refs = refs[:n], refs[n:2 * n]
        outs = refs[2 * n + n_after:]
        send_sems, recv_sems, token = outs[:n], outs[n:2 * n], outs[4 * n]
        x, y, c = _mesh_pos()
        me = _flat_index(x, y, c)
        for g in range(n):
            for j, (peer_idx, peer) in enumerate(_peers(x, y, c)):
                _exchange_copy(src_refs[g], land_refs[g], send_sems[g], recv_sems[g], j, me, peer_idx, peer, scatter).start()
        token[...] = jnp.zeros_like(token)

    hbm = lambda a: pltpu.with_memory_space_constraint(a, pltpu.HBM)
    res = pl.pallas_call(
        body, name=name,
        out_shape=(*[pltpu.SemaphoreType.DMA((N_DEV - 1,))] * (2 * n),
                   *[pltpu.HBM(a.shape, a.dtype) for a in srcs], *[pltpu.HBM(a.shape, a.dtype) for a in lands],
                   jax.ShapeDtypeStruct((8, 128), F32)),
        in_specs=[HBM_SPEC] * (2 * n) + [ANY_SPEC] * n_after,
        out_specs=(*[SEM_SPEC] * (2 * n), *[HBM_SPEC] * (2 * n), pl.BlockSpec(memory_space=pltpu.VMEM)),
        input_output_aliases={i: 2 * n + i for i in range(2 * n)},
        compiler_params=pltpu.CompilerParams(has_side_effects=DATAFLOW_EFFECT),
    )(*[hbm(a) for a in srcs], *[hbm(a) for a in lands], *after)
    handles = [(res[g], res[n + g], res[2 * n + g], res[3 * n + g]) for g in range(n)]
    return handles, res[4 * n]


def exchange_wait(handle, scatter, after, name):
    send_sem, recv_sem, src_thru, land_thru = handle

    def body(src_ref, land_ref, send_sems, recv_sems, after_ref, src_dead, got_ref):
        x, y, c = _mesh_pos()
        me = _flat_index(x, y, c)
        for j, (peer_idx, peer) in enumerate(_peers(x, y, c)):
            mine = _exchange_copy(src_ref, land_ref, send_sems, recv_sems, j, me, peer_idx, peer, scatter)
            mine.wait_send()
            theirs = pltpu.make_async_remote_copy(
                src_ref=src_ref.at[me] if scatter else src_ref, dst_ref=land_ref.at[peer_idx],
                send_sem=send_sems.at[j], recv_sem=recv_sems.at[j], device_id=peer, device_id_type=pl.DeviceIdType.MESH)
            theirs.wait_recv()

    return pl.pallas_call(
        body, name=name,
        out_shape=(pltpu.HBM(src_thru.shape, src_thru.dtype), pltpu.HBM(land_thru.shape, land_thru.dtype)),
        in_specs=(HBM_SPEC, HBM_SPEC, SEM_SPEC, SEM_SPEC, ANY_SPEC), out_specs=(HBM_SPEC, HBM_SPEC),
        input_output_aliases={0: 0, 1: 1},
        compiler_params=pltpu.CompilerParams(has_side_effects=DATAFLOW_EFFECT),
    )(src_thru, land_thru, send_sem, recv_sem, after)


def ordered_sum(parts, name, tr=512):
    n, R, C = parts.shape
    tr = next((t for t in range(min(tr, R), 15, -16) if R % t == 0), R)

    def body(p_ref, o_ref):
        acc = p_ref[0].astype(F32)
        for j in range(1, n):
            acc = acc + p_ref[j].astype(F32)
        o_ref[...] = acc

    return pl.pallas_call(
        body, name=name, grid=(R // tr,),
        in_specs=[pl.BlockSpec((n, tr, C), lambda i: (0, i, 0))],
        out_specs=pl.BlockSpec((tr, C), lambda i: (i, 0)),
        out_shape=jax.ShapeDtypeStruct((R, C), F32),
        compiler_params=_params(1),
    )(parts)


def adamw(w, g, m, v, name, tr=256):
    R, C = w.shape
    tr = next((t for t in range(tr, 7, -8) if R % t == 0), R)
    c1 = 1.0 - ADAM_B1 ** ADAM_STEP
    c2 = 1.0 - ADAM_B2 ** ADAM_STEP

    def body(w_ref, g_ref, m_ref, v_ref, d_ref, mo_ref, vo_ref):
        grad = g_ref[...]
        m2 = ADAM_B1 * m_ref[...] + (1.0 - ADAM_B1) * grad
        v2 = ADAM_B2 * v_ref[...] + (1.0 - ADAM_B2) * (grad * grad)
        mo_ref[...] = m2
        vo_ref[...] = v2
        d_ref[...] = -ADAM_LR * ((m2 / c1) / (jnp.sqrt(v2 / c2) + ADAM_EPS) + ADAM_WD * w_ref[...])

    spec = pl.BlockSpec((tr, C), lambda i: (i, 0))
    return pl.pallas_call(
        body, name=name, grid=(R // tr,),
        in_specs=[spec] * 4, out_specs=[spec] * 3,
        out_shape=[jax.ShapeDtypeStruct((R, C), F32)] * 3,
        compiler_params=_params(1),
    )(w, g, m, v)


WEIGHT_NAMES = ("ffn1_norm", "ffn1_w_gu", "ffn1_w_down", "mix_norm", "even_w_in", "conv_a_w", "conv_a_b", "conv_a_ln_g",
                "conv_a_ln_b", "swa_sinks", "even_w_out", "odd_w_in", "sc_conv_w", "odd_w_out", "xa_norm", "xa_mem_norm",
                "xa_wq", "xa_wkv", "xa_wo", "ffn2_norm", "ffn2_w_gu", "ffn2_w_down", "final_norm")
BLOB_COLS = 1024
SMALL_ROWS = (("ffn1_norm", 0, 2), ("mix_norm", 2, 2), ("xa_norm", 4, 2), ("xa_mem_norm", 6, 2), ("ffn2_norm", 8, 2),
              ("final_norm", 10, 1))
ROW_CONV_B_LNG = 11
ROW_LNB_SINKS_LOSS = 12
LOSS_COL = 512 + SWA_HEADS
ROW_SC_CONV = 13
ROW_CONV_W = 16
SMALL_BLOB_ROWS = 32
SMALL_ADAM_ROWS = 16


def _small_blob(v):
    rows = [v[n].reshape(-1, D_MODEL) for n, _, _ in SMALL_ROWS]
    rows.append(jnp.concatenate([v["conv_a_b"].reshape(-1), v["conv_a_ln_g"].reshape(-1)]).reshape(1, D_MODEL))
    tail = jnp.zeros((D_MODEL - 512 - SWA_HEADS,), F32)
    if "loss" in v:
        tail = tail.at[0].set(v["loss"])
    rows.append(jnp.concatenate([v["conv_a_ln_b"].reshape(-1), v["swa_sinks"].reshape(-1), tail]).reshape(1, D_MODEL))
    rows.append(jnp.zeros((SMALL_ADAM_ROWS - ROW_SC_CONV, D_MODEL), F32))
    return jnp.concatenate(rows, axis=0)


def _small_unblob(b, shapes):
    out = {n: b[r:r + k].reshape(shapes[n]) for n, r, k in SMALL_ROWS}
    out["conv_a_b"] = b[ROW_CONV_B_LNG, :512].reshape(shapes["conv_a_b"])
    out["conv_a_ln_g"] = b[ROW_CONV_B_LNG, 512:].reshape(shapes["conv_a_ln_g"])
    out["conv_a_ln_b"] = b[ROW_LNB_SINKS_LOSS, :512].reshape(shapes["conv_a_ln_b"])
    out["swa_sinks"] = b[ROW_LNB_SINKS_LOSS, 512:512 + SWA_HEADS].reshape(shapes["swa_sinks"])
    return out


def kernel(x, mem, ffn1_norm, ffn1_w_gu, ffn1_w_down, mix_norm, even_w_in, conv_a_w, conv_a_b, conv_a_ln_g, conv_a_ln_b, swa_sinks, even_w_out, odd_w_in, sc_conv_w, odd_w_out, xa_norm, xa_mem_norm, xa_wq, xa_wkv, xa_wo, ffn2_norm, ffn2_w_gu, ffn2_w_down, final_norm, loss_target, m_ffn1_norm, m_ffn1_w_gu, m_ffn1_w_down, m_mix_norm, m_even_w_in, m_conv_a_w, m_conv_a_b, m_conv_a_ln_g, m_conv_a_ln_b, m_swa_sinks, m_even_w_out, m_odd_w_in, m_sc_conv_w, m_odd_w_out, m_xa_norm, m_xa_mem_norm, m_xa_wq, m_xa_wkv, m_xa_wo, m_ffn2_norm, m_ffn2_w_gu, m_ffn2_w_down, m_final_norm, v_ffn1_norm, v_ffn1_w_gu, v_ffn1_w_down, v_mix_norm, v_even_w_in, v_conv_a_w, v_conv_a_b, v_conv_a_ln_g, v_conv_a_ln_b, v_swa_sinks, v_even_w_out, v_odd_w_in, v_sc_conv_w, v_odd_w_out, v_xa_norm, v_xa_mem_norm, v_xa_wq, v_xa_wkv, v_xa_wo, v_ffn2_norm, v_ffn2_w_gu, v_ffn2_w_down, v_final_norm):
    w = dict(ffn1_norm=ffn1_norm, ffn1_w_gu=ffn1_w_gu, ffn1_w_down=ffn1_w_down, mix_norm=mix_norm, even_w_in=even_w_in,
             conv_a_w=conv_a_w, conv_a_b=conv_a_b, conv_a_ln_g=conv_a_ln_g, conv_a_ln_b=conv_a_ln_b, swa_sinks=swa_sinks,
             even_w_out=even_w_out, odd_w_in=odd_w_in, sc_conv_w=sc_conv_w, odd_w_out=odd_w_out, xa_norm=xa_norm,
             xa_mem_norm=xa_mem_norm, xa_wq=xa_wq, xa_wkv=xa_wkv, xa_wo=xa_wo, ffn2_norm=ffn2_norm, ffn2_w_gu=ffn2_w_gu,
             ffn2_w_down=ffn2_w_down, final_norm=final_norm)
    m = dict(ffn1_norm=m_ffn1_norm, ffn1_w_gu=m_ffn1_w_gu, ffn1_w_down=m_ffn1_w_down, mix_norm=m_mix_norm,
             even_w_in=m_even_w_in, conv_a_w=m_conv_a_w, conv_a_b=m_conv_a_b, conv_a_ln_g=m_conv_a_ln_g,
             conv_a_ln_b=m_conv_a_ln_b, swa_sinks=m_swa_sinks, even_w_out=m_even_w_out, odd_w_in=m_odd_w_in,
             sc_conv_w=m_sc_conv_w, odd_w_out=m_odd_w_out, xa_norm=m_xa_norm, xa_mem_norm=m_xa_mem_norm, xa_wq=m_xa_wq,
             xa_wkv=m_xa_wkv, xa_wo=m_xa_wo, ffn2_norm=m_ffn2_norm, ffn2_w_gu=m_ffn2_w_gu, ffn2_w_down=m_ffn2_w_down,
             final_norm=m_final_norm)
    v = dict(ffn1_norm=v_ffn1_norm, ffn1_w_gu=v_ffn1_w_gu, ffn1_w_down=v_ffn1_w_down, mix_norm=v_mix_norm,
             even_w_in=v_even_w_in, conv_a_w=v_conv_a_w, conv_a_b=v_conv_a_b, conv_a_ln_g=v_conv_a_ln_g,
             conv_a_ln_b=v_conv_a_ln_b, swa_sinks=v_swa_sinks, even_w_out=v_even_w_out, odd_w_in=v_odd_w_in,
             sc_conv_w=v_sc_conv_w, odd_w_out=v_odd_w_out, xa_norm=v_xa_norm, xa_mem_norm=v_xa_mem_norm, xa_wq=v_xa_wq,
             xa_wkv=v_xa_wkv, xa_wo=v_xa_wo, ffn2_norm=v_ffn2_norm, ffn2_w_gu=v_ffn2_w_gu, ffn2_w_down=v_ffn2_w_down,
             final_norm=v_final_norm)
    me = _flat_index(*_mesh_pos())

    conv_blob = jnp.concatenate([w["conv_a_w"].reshape(-1), w["sc_conv_w"].reshape(-1),
                                 jnp.zeros((8 * 1024 - 31 * 64 - 3 * 128,), F32)]).reshape(8, 1024)
    conv_all = all_gather(conv_blob, "gather_conv_weights").reshape(N_DEV, 8 * 1024)
    conv_a_full = jnp.transpose(conv_all[:, :31 * 64].reshape(N_DEV, 31, 64), (1, 0, 2)).reshape(31, 512)
    sc_full = jnp.transpose(conv_all[:, 31 * 64:31 * 64 + 3 * 128].reshape(N_DEV, 3, 128), (1, 0, 2)).reshape(3, 1024)

    def stage_blob(keys):
        shards = [w[n][l].T if SPLIT_AXIS[n] == 1 else w[n][l] for n, l in keys]
        return jnp.concatenate([s.astype(BF16) for s in shards], axis=0)

    def stage_rows(keys):
        return [(n, l, w[n].shape[2] if SPLIT_AXIS[n] == 1 else w[n].shape[1]) for n, l in keys]

    def unpack_weights(gathered, keys):
        out, off = {}, 0
        for n, l, rows in stage_rows(keys):
            out[(n, l)] = gathered[:, off:off + rows, :].reshape(N_DEV * rows, BLOB_COLS)
            off += rows
        return out

    def with_own(land, own):
        return lax.dynamic_update_slice(land, own[None], (me, 0, 0))

    gathered_a = all_gather(stage_blob(STAGE_KEYS["A"]), "gather_weights_a")
    later = ("B", "C", "D")
    blobs = [stage_blob(STAGE_KEYS[s]) for s in later]
    lands = [lax.empty((N_DEV,) + b.shape, BF16) for b in blobs]
    weight_handles, weight_token = exchange_start(blobs, lands, False, [gathered_a, conv_all], "gather_start")

    def get_weights(stage, after):
        if stage == "A":
            return unpack_weights(gathered_a, STAGE_KEYS["A"]), weight_token
        own, land = exchange_wait(weight_handles[later.index(stage)], False, after, "gather_wait_" + stage.lower())
        return unpack_weights(with_own(land, own), STAGE_KEYS[stage]), None

    grad_handles = {}

    def put_grads(stage, dws):
        packed = jnp.concatenate([dw.reshape(N_DEV, -1, BLOB_COLS) for dw in dws.values()], axis=1)
        land = lax.empty(packed.shape, BF16)
        (handle,), token = exchange_start([packed], [land], True, [], "scatter_start_" + stage.lower())
        grad_handles[stage] = (handle, tuple(dws))
        return token

    P = dict(ffn1_norm=ffn1_norm, mix_norm=mix_norm, xa_norm=xa_norm, xa_mem_norm=xa_mem_norm, ffn2_norm=ffn2_norm,
             final_norm=final_norm, conv_a_w=conv_a_full, conv_a_b=conv_a_b[0], conv_a_ln_g=conv_a_ln_g[0],
             conv_a_ln_b=conv_a_ln_b[0], swa_sinks=swa_sinks[0], sc_conv_w=sc_full)

    loss_part, grad_x, dP = local_step(x[0], mem[0], loss_target[0], P, get_weights, put_grads)

    def finish_grads(stage, after):
        handle, keys = grad_handles[stage]
        packed, land = exchange_wait(handle, True, after, "scatter_wait_" + stage.lower())
        own = lax.dynamic_slice(packed, (me, 0, 0), (1,) + packed.shape[1:])[0]
        rows_f32 = ordered_sum(with_own(land, own), "sum_grads_" + stage.lower())
        out, off = {}, 0
        for n, l, rows in stage_rows(keys):
            part = rows_f32[off:off + rows]
            out[(n, l)] = part.T if SPLIT_AXIS[n] == 1 else part
            off += rows
        return out

    layer_grads = {**finish_grads("D", grad_x), **finish_grads("BC", grad_x)}

    dP = dict(dP, loss=loss_part[0, 0])
    small = jnp.concatenate([
        _small_blob(dP)[:ROW_SC_CONV], dP["sc_conv_w"],
        jnp.concatenate([dP["conv_a_w"].reshape(-1), jnp.zeros((512,), F32)]).reshape(16, D_MODEL)], axis=0)
    small_sum = ordered_sum(all_gather(small, "gather_small_grads"), "sum_small_grads", tr=SMALL_BLOB_ROWS)
    loss = small_sum[ROW_LNB_SINKS_LOSS, LOSS_COL]
    grads = _small_unblob(small_sum, {n: w[n].shape for n in WEIGHT_NAMES})
    sc_g = small_sum[ROW_SC_CONV:ROW_SC_CONV + 3]
    grads["sc_conv_w"] = lax.dynamic_slice(sc_g, (0, me * 128), (3, 128)).reshape(w["sc_conv_w"].shape)
    cw_g = small_sum[ROW_CONV_W:].reshape(-1)[:31 * 512].reshape(31, 512)
    grads["conv_a_w"] = lax.dynamic_slice(cw_g, (0, me * 64), (31, 64)).reshape(w["conv_a_w"].shape)

    delta, new_m, new_v = {}, {}, {}

    def update(n):
        shp = w[n].shape
        two_d = (shp[0] * shp[1], shp[2])
        d_, m_, v_ = adamw(w[n].reshape(two_d), grads[n].reshape(two_d), m[n].reshape(two_d), v[n].reshape(two_d),
                           "adamw_" + n)
        delta[n], new_m[n], new_v[n] = d_.reshape(shp), m_.reshape(shp), v_.reshape(shp)

    first_stage = tuple(n for n, _ in STAGE_KEYS["A"])
    for n in SPLIT_AXIS:
        if n not in first_stage:
            grads[n] = jnp.stack([layer_grads[(n, l)] for l in range(w[n].shape[0])], axis=0)
            update(n)
    update("conv_a_w")
    update("sc_conv_w")
    layer_grads.update(finish_grads("A", delta["ffn2_w_gu"]))
    for n in first_stage:
        grads[n] = jnp.stack([layer_grads[(n, l)] for l in range(w[n].shape[0])], axis=0)
        update(n)
    d_, m_, v_ = adamw(_small_blob(w), small_sum[:SMALL_ADAM_ROWS], _small_blob(m), _small_blob(v), "adamw_small",
                       tr=SMALL_ADAM_ROWS)
    shapes = {n: w[n].shape for n in WEIGHT_NAMES}
    delta.update(_small_unblob(d_, shapes))
    new_m.update(_small_unblob(m_, shapes))
    new_v.update(_small_unblob(v_, shapes))

    return (loss, grad_x[None], *[grads[n] for n in WEIGHT_NAMES], *[delta[n] for n in WEIGHT_NAMES],
            *[new_m[n] for n in WEIGHT_NAMES], *[new_v[n] for n in WEIGHT_NAMES])
```

```python
import functools

import jax
import jax.numpy as jnp
from jax import lax
from jax.experimental import pallas as pl
from jax.experimental.pallas import tpu as pltpu

F32 = jnp.float32
BF16 = jnp.bfloat16

D_MODEL = 1024
D_FF = 2816
CONV_A_CH = 512
CONV_A_WIDTH = 31
SWA_HEADS = 8
SWA_KV_HEADS = 2
SWA_GROUP = SWA_HEADS // SWA_KV_HEADS
HEAD_DIM = 64
WINDOW = 128
SC_CH = 1024
XA_HEADS = 4
XA_HEAD_DIM = D_MODEL // XA_HEADS
RMS_EPS = 1e-6
LN_EPS = 1e-5
ADAM_LR = 0.001
ADAM_B1 = 0.9
ADAM_B2 = 0.999
ADAM_EPS = 1e-08
ADAM_WD = 0.01
ADAM_STEP = 10
N_DEV = 8

V7X_VMEM_BYTES = 64 * 1024 * 1024
VMEM_LIMIT = V7X_VMEM_BYTES - 8 * 1024 * 1024
CONV_HALO = 32
SC_HALO = 8
NEG_BIG = -1e30

SPLIT_AXIS = dict(ffn1_w_gu=1, ffn1_w_down=0, even_w_in=1, even_w_out=0, odd_w_in=1, odd_w_out=0, xa_wq=0, xa_wkv=1, xa_wo=0,
                  ffn2_w_gu=1, ffn2_w_down=0)
STAGE_KEYS = dict(
    A=(("ffn1_w_gu", 0), ("ffn1_w_down", 0)),
    B=(("even_w_in", 0), ("even_w_out", 0), ("xa_wq", 0), ("xa_wkv", 0), ("xa_wo", 0)),
    C=(("ffn2_w_gu", 0), ("ffn2_w_down", 0)),
    D=(("ffn1_w_gu", 1), ("ffn1_w_down", 1), ("odd_w_in", 0), ("odd_w_out", 0), ("xa_wq", 1), ("xa_wkv", 1), ("xa_wo", 1),
       ("ffn2_w_gu", 1), ("ffn2_w_down", 1)))


def _params(n_axes):
    return pltpu.CompilerParams(dimension_semantics=("arbitrary",) * n_axes, vmem_limit_bytes=VMEM_LIMIT)


def _tile(n, pref):
    t = min(n, pref)
    assert n % t == 0, (n, pref)
    return t


def _dot(a, b):
    return jnp.dot(a, b, preferred_element_type=F32)


def _dot_nt(a, b):
    return lax.dot_general(a, b, (((1,), (1,)), ((), ())), preferred_element_type=F32)


def _dot_tn(a, b):
    return lax.dot_general(a, b, (((0,), (0,)), ((), ())), preferred_element_type=F32)


def _sigmoid(x):
    return 0.5 * jnp.tanh(0.5 * x) + 0.5


ANY_SPEC = pl.BlockSpec(memory_space=pl.ANY)


def _with_dep(body, n_in, dep):
    if dep is None:
        return body, [], []
    return (lambda *refs: body(*refs[:n_in], *refs[n_in + 1:])), [ANY_SPEC], [dep]


def rmsnorm(h, g, name, tm=1024, dep=None):
    T, K = h.shape
    tm = _tile(T, tm)

    def kern(h_ref, g_ref, u_ref):
        x = h_ref[...]
        r = lax.rsqrt(jnp.mean(x * x, axis=-1, keepdims=True) + RMS_EPS)
        u_ref[...] = ((x * r) * g_ref[...]).astype(BF16)

    body, dep_spec, dep_arg = _with_dep(kern, 2, dep)
    return pl.pallas_call(
        body, name=name, grid=(T // tm,),
        in_specs=[pl.BlockSpec((tm, K), lambda i: (i, 0)), pl.BlockSpec((1, K), lambda i: (0, 0))] + dep_spec,
        out_specs=pl.BlockSpec((tm, K), lambda i: (i, 0)),
        out_shape=jax.ShapeDtypeStruct((T, K), BF16),
        compiler_params=_params(1),
    )(h, g, *dep_arg)


def matmul(a, w, out_dtype, name, tn, tm=2048, transposed=True, n_tiles=None, first_tile=0):
    T, K = a.shape
    N = w.shape[0] if transposed else w.shape[1]
    n_tiles = N // tn if n_tiles is None else n_tiles
    tm = _tile(T, tm)
    mm = _dot_nt if transposed else _dot

    def body(a_ref, w_ref, z_ref):
        z_ref[...] = mm(a_ref[...], w_ref[...]).astype(z_ref.dtype)

    w_spec = (pl.BlockSpec((tn, K), lambda i, j: (first_tile + j, 0)) if transposed
              else pl.BlockSpec((K, tn), lambda i, j: (0, first_tile + j)))
    return pl.pallas_call(
        body, name=name, grid=(T // tm, n_tiles),
        in_specs=[pl.BlockSpec((tm, K), lambda i, j: (i, 0)), w_spec],
        out_specs=pl.BlockSpec((tm, tn), lambda i, j: (i, j)),
        out_shape=jax.ShapeDtypeStruct((T, n_tiles * tn), out_dtype),
        compiler_params=_params(2),
    )(a, w)


def norm_matmul(h, g, w, out_dtype, name, tn, dep=None, transposed=True):
    u = rmsnorm(h, g, name + "_norm", dep=dep)
    return matmul(u, w, out_dtype, name, tn, transposed=transposed), u


def matmul_residual(a, w, res, name, tm=1024):
    T, K = a.shape
    N = w.shape[1]
    tm = _tile(T, tm)

    def body(a_ref, w_ref, r_ref, o_ref):
        o_ref[...] = r_ref[...] + _dot(a_ref[...], w_ref[...])

    return pl.pallas_call(
        body, name=name, grid=(T // tm,),
        in_specs=[pl.BlockSpec((tm, K), lambda i: (i, 0)),
                  pl.BlockSpec((K, N), lambda i: (0, 0)),
                  pl.BlockSpec((tm, N), lambda i: (i, 0))],
        out_specs=pl.BlockSpec((tm, N), lambda i: (i, 0)),
        out_shape=jax.ShapeDtypeStruct((T, N), F32),
        compiler_params=_params(1),
    )(a, w, res)


def matmul_nt(dy, w, out_dtype, name, tm=1024):
    T, N = dy.shape
    K = w.shape[0]
    tm = _tile(T, tm)

    def body(dy_ref, w_ref, o_ref):
        o_ref[...] = _dot_nt(dy_ref[...].astype(BF16), w_ref[...]).astype(o_ref.dtype)

    return pl.pallas_call(
        body, name=name, grid=(T // tm,),
        in_specs=[pl.BlockSpec((tm, N), lambda i: (i, 0)),
                  pl.BlockSpec((K, N), lambda i: (0, 0))],
        out_specs=pl.BlockSpec((tm, K), lambda i: (i, 0)),
        out_shape=jax.ShapeDtypeStruct((T, K), out_dtype),
        compiler_params=_params(1),
    )(dy, w)


def matmul_norm_bwd(dz, w, h, g, dh_in, name, tm=512, transposed=True, dep=None):
    T, N = dz.shape
    K = h.shape[1]
    tm = _tile(T, tm)

    def kern(dz_ref, w_ref, h_ref, g_ref, dhin_ref, dh_ref, dg_ref):
        @pl.when(pl.program_id(0) == 0)
        def _():
            dg_ref[...] = jnp.zeros_like(dg_ref)

        mm = _dot if transposed else _dot_nt
        du = mm(dz_ref[...], w_ref[...])
        x = h_ref[...]
        r = lax.rsqrt(jnp.mean(x * x, axis=-1, keepdims=True) + RMS_EPS)
        xh = x * r
        dg_ref[...] += jnp.sum(du * xh, axis=0, keepdims=True)
        dxh = du * g_ref[...]
        dh_ref[...] = dhin_ref[...] + r * (dxh - xh * jnp.mean(dxh * xh, axis=-1, keepdims=True))

    body, dep_spec, dep_arg = _with_dep(kern, 5, dep)
    return pl.pallas_call(
        body, name=name, grid=(T // tm,),
        in_specs=[pl.BlockSpec((tm, N), lambda i: (i, 0)),
                  pl.BlockSpec(w.shape, lambda i: (0, 0)),
                  pl.BlockSpec((tm, K), lambda i: (i, 0)),
                  pl.BlockSpec((1, K), lambda i: (0, 0)),
                  pl.BlockSpec((tm, K), lambda i: (i, 0))] + dep_spec,
        out_specs=[pl.BlockSpec((tm, K), lambda i: (i, 0)),
                   pl.BlockSpec((1, K), lambda i: (0, 0))],
        out_shape=[jax.ShapeDtypeStruct((T, K), F32), jax.ShapeDtypeStruct((1, K), F32)],
        compiler_params=_params(1),
    )(dz, w, h, g, dh_in, *dep_arg)


def matmul_tn(x, dy, name, scale=1.0, tk=None, tn=None, tt=1024):
    T, K = x.shape
    N = dy.shape[1]
    tk = K if tk is None else tk
    tn = N if tn is None else tn
    tt = _tile(T, tt)
    nt = T // tt

    def body(x_ref, dy_ref, o_ref, acc_ref):
        t = pl.program_id(2)

        @pl.when(t == 0)
        def _():
            acc_ref[...] = jnp.zeros_like(acc_ref)

        acc_ref[...] += _dot_tn(x_ref[...].astype(BF16), dy_ref[...].astype(BF16))

        @pl.when(t == nt - 1)
        def _():
            o_ref[...] = (acc_ref[...] * scale).astype(o_ref.dtype)

    return pl.pallas_call(
        body, name=name, grid=(K // tk, N // tn, nt),
        in_specs=[pl.BlockSpec((tt, tk), lambda a, b, t: (t, a)),
                  pl.BlockSpec((tt, tn), lambda a, b, t: (t, b))],
        out_specs=pl.BlockSpec((tk, tn), lambda a, b, t: (a, b)),
        out_shape=jax.ShapeDtypeStruct((K, N), BF16),
        scratch_shapes=[pltpu.VMEM((tk, tn), F32)],
        compiler_params=_params(3),
    )(x, dy)


FFN_COL_CHUNK = 256


def ffn_down(gu, wd, res, name, tm=512):
    T = gu.shape[0]
    F = gu.shape[1] // 2
    N = wd.shape[1]
    tm = _tile(T, tm)

    def body(g_ref, up_ref, w_ref, r_ref, o_ref, a_ref):
        g = g_ref[...].astype(F32)
        a_ref[...] = ((g * _sigmoid(g)) * up_ref[...].astype(F32)).astype(BF16)
        o_ref[...] = r_ref[...] + 0.5 * _dot(a_ref[...], w_ref[...])

    return pl.pallas_call(
        body, name=name, grid=(T // tm,),
        in_specs=[pl.BlockSpec((tm, F), lambda i: (i, 0)),
                  pl.BlockSpec((tm, F), lambda i: (i, 1)),
                  pl.BlockSpec((F, N), lambda i: (0, 0)),
                  pl.BlockSpec((tm, N), lambda i: (i, 0))],
        out_specs=[pl.BlockSpec((tm, N), lambda i: (i, 0)),
                   pl.BlockSpec((tm, F), lambda i: (i, 0))],
        out_shape=[jax.ShapeDtypeStruct((T, N), F32), jax.ShapeDtypeStruct((T, F), BF16)],
        compiler_params=_params(1),
    )(gu, gu, wd, res)


def ffn_down_bwd(dy, wd, gu, name, tm=512, dep=None):
    T, N = dy.shape
    F = wd.shape[0]
    tm = _tile(T, tm)

    def kern(dy_ref, w_ref, g_ref, up_ref, o_ref):
        dyb = dy_ref[...].astype(BF16)
        for c0 in range(0, F, FFN_COL_CHUNK):
            cols = slice(c0, c0 + FFN_COL_CHUNK)
            da = 0.5 * _dot_nt(dyb, w_ref[cols, :])
            g = g_ref[:, cols].astype(F32)
            up = up_ref[:, cols].astype(F32)
            s = _sigmoid(g)
            o_ref[:, cols] = (da * up * (s * (1.0 + g * (1.0 - s)))).astype(BF16)
            o_ref[:, F + c0:F + c0 + FFN_COL_CHUNK] = (da * (g * s)).astype(BF16)

    body, dep_spec, dep_arg = _with_dep(kern, 4, dep)
    return pl.pallas_call(
        body, name=name, grid=(T // tm,),
        in_specs=[pl.BlockSpec((tm, N), lambda i: (i, 0)),
                  pl.BlockSpec((F, N), lambda i: (0, 0)),
                  pl.BlockSpec((tm, F), lambda i: (i, 0)),
                  pl.BlockSpec((tm, F), lambda i: (i, 1))] + dep_spec,
        out_specs=pl.BlockSpec((tm, 2 * F), lambda i: (i, 0)),
        out_shape=jax.ShapeDtypeStruct((T, 2 * F), BF16),
        compiler_params=_params(1),
    )(dy, wd, gu, gu, *dep_arg)


def ffn_forward(h, g, w_gu, w_down, name, dep=None):
    gu, u = norm_matmul(h, g, w_gu, BF16, name + "_gu", D_FF // 2, dep=dep)
    h_out, a = ffn_down(gu, w_down, h, name + "_down")
    return h_out, (h, u, gu, a)


def ffn_backward(dy, saved, g, w_gu, w_down, name, dep=None, emit=None):
    h, u, gu, a = saved
    dgu = ffn_down_bwd(dy, w_down, gu, name + "_ddown", dep=dep)
    d_w_down = matmul_tn(a, dy, name + "_dwd", scale=0.5, tk=D_FF // 2)
    d_w_gu = matmul_tn(dgu, u, name + "_dwgu", tk=D_FF)
    dh, dg = matmul_norm_bwd(dgu, w_gu, h, g, dy, name + "_dx", dep=emit(d_w_gu, d_w_down))
    return dh, dg


def _shifted_taps(ref, phase_ref, n_rows):
    for r in range(1, 8):
        phase_ref[r - 1] = ref[pl.ds(r, n_rows - 8), :]

    def taps(o, n):
        q, r = divmod(o, 8)
        return ref[8 * q:8 * q + n, :] if r == 0 else phase_ref[r - 1, 8 * q:8 * q + n, :]

    return taps


def conformer_conv_fwd(z, cw, cb, lg, lb, name, tm=512):
    T = z.shape[0]
    C = CONV_A_CH
    tm = _tile(T, tm)
    hb = tm // CONV_HALO

    def body(v_ref, gt_ref, pv_ref, pg_ref, cw_ref, cb_ref, lg_ref, lb_ref, o_ref, xs_ref, xph_ref):
        i = pl.program_id(0)
        prev = pv_ref[...] * _sigmoid(pg_ref[...])
        xs_ref[0:CONV_HALO, :] = jnp.where(i > 0, prev, 0.0)
        xs_ref[CONV_HALO:, :] = v_ref[...] * _sigmoid(gt_ref[...])
        taps = _shifted_taps(xs_ref, xph_ref, tm + CONV_HALO)
        acc = jnp.zeros((tm, C), F32) + cb_ref[...]
        for k in range(CONV_A_WIDTH):
            acc = acc + cw_ref[k:k + 1, :] * taps(CONV_HALO - (CONV_A_WIDTH - 1) + k, tm)
        mu = jnp.mean(acc, axis=-1, keepdims=True)
        xc = acc - mu
        var = jnp.mean(xc * xc, axis=-1, keepdims=True)
        y = (xc * lax.rsqrt(var + LN_EPS)) * lg_ref[...] + lb_ref[...]
        o_ref[...] = (y * _sigmoid(y)).astype(BF16)

    return pl.pallas_call(
        body, name=name, grid=(T // tm,),
        in_specs=[pl.BlockSpec((tm, C), lambda i: (i, 0)),
                  pl.BlockSpec((tm, C), lambda i: (i, 1)),
                  pl.BlockSpec((CONV_HALO, C), lambda i: (jnp.maximum(i * hb - 1, 0), 0)),
                  pl.BlockSpec((CONV_HALO, C), lambda i: (jnp.maximum(i * hb - 1, 0), 1)),
                  pl.BlockSpec((32, C), lambda i: (0, 0)),
                  pl.BlockSpec((1, C), lambda i: (0, 0)),
                  pl.BlockSpec((1, C), lambda i: (0, 0)),
                  pl.BlockSpec((1, C), lambda i: (0, 0))],
        out_specs=pl.BlockSpec((tm, C), lambda i: (i, 0)),
        out_shape=jax.ShapeDtypeStruct((T, C), BF16),
        scratch_shapes=[pltpu.VMEM((tm + CONV_HALO, C), F32), pltpu.VMEM((7, tm + CONV_HALO - 8, C), F32)],
        compiler_params=_params(1),
    )(z, z, z, z, cw, cb, lg, lb)


def conformer_conv_bwd(z, dm, cw, cb, lg, lb, name, tm=512):
    T = z.shape[0]
    C = CONV_A_CH
    tm = _tile(T, tm)
    hb = tm // CONV_HALO
    n_tiles = T // tm
    last_halo = T // CONV_HALO - 1
    R = tm + CONV_HALO
    KW = CONV_A_WIDTH

    def body(v_ref, gt_ref, pv_ref, pg_ref, nv_ref, ng_ref, do_ref, ndo_ref, cw_ref, cb_ref, lg_ref, lb_ref,
             dz_ref, dcw_ref, dcb_ref, dlg_ref, dlb_ref, xs_ref, ds_ref, xph_ref, dph_ref):
        i = pl.program_id(0)

        @pl.when(i == 0)
        def _():
            dcw_ref[...] = jnp.zeros_like(dcw_ref)
            dcb_ref[...] = jnp.zeros_like(dcb_ref)
            dlg_ref[...] = jnp.zeros_like(dlg_ref)
            dlb_ref[...] = jnp.zeros_like(dlb_ref)

        val = v_ref[...]
        sg = _sigmoid(gt_ref[...])
        prev = pv_ref[...] * _sigmoid(pg_ref[...])
        xs_ref[0:CONV_HALO, :] = jnp.where(i > 0, prev, 0.0)
        xs_ref[CONV_HALO:CONV_HALO + tm, :] = val * sg
        xs_ref[CONV_HALO + tm:, :] = nv_ref[...] * _sigmoid(ng_ref[...])

        x_taps = _shifted_taps(xs_ref, xph_ref, tm + 2 * CONV_HALO)
        acc = jnp.zeros((R, C), F32) + cb_ref[...]
        for k in range(KW):
            acc = acc + cw_ref[k:k + 1, :] * x_taps(CONV_HALO - (KW - 1) + k, R)
        mu = jnp.mean(acc, axis=-1, keepdims=True)
        xc = acc - mu
        rstd = lax.rsqrt(jnp.mean(xc * xc, axis=-1, keepdims=True) + LN_EPS)
        xh = xc * rstd
        y = xh * lg_ref[...] + lb_ref[...]
        s = _sigmoid(y)
        dout = jnp.concatenate([do_ref[...], jnp.where(i < n_tiles - 1, ndo_ref[...], 0.0)], axis=0)
        dy = dout * (s * (1.0 + y * (1.0 - s)))
        dxh = dy * lg_ref[...]
        dconv = rstd * (dxh - jnp.mean(dxh, axis=-1, keepdims=True) - xh * jnp.mean(dxh * xh, axis=-1, keepdims=True))
        ds_ref[...] = dconv

        dy_m = dy[:tm]
        dlg_ref[...] += jnp.sum(dy_m * xh[:tm], axis=0, keepdims=True)
        dlb_ref[...] += jnp.sum(dy_m, axis=0, keepdims=True)
        dc_m = dconv[:tm]
        dcb_ref[...] += jnp.sum(dc_m, axis=0, keepdims=True)
        d_taps = _shifted_taps(ds_ref, dph_ref, R)
        dglu = jnp.zeros((tm, C), F32)
        for k in range(KW):
            dcw_ref[k:k + 1, :] += jnp.sum(dc_m * x_taps(CONV_HALO - (KW - 1) + k, tm), axis=0, keepdims=True)
            dglu = dglu + cw_ref[k:k + 1, :] * d_taps(KW - 1 - k, tm)
        dz_ref[:, :C] = (dglu * sg).astype(BF16)
        dz_ref[:, C:] = (dglu * val * sg * (1.0 - sg)).astype(BF16)

    prev_map = lambda i: jnp.maximum(i * hb - 1, 0)
    next_map = lambda i: jnp.minimum((i + 1) * hb, last_halo)
    return pl.pallas_call(
        body, name=name, grid=(n_tiles,),
        in_specs=[pl.BlockSpec((tm, C), lambda i: (i, 0)),
                  pl.BlockSpec((tm, C), lambda i: (i, 1)),
                  pl.BlockSpec((CONV_HALO, C), lambda i: (prev_map(i), 0)),
                  pl.BlockSpec((CONV_HALO, C), lambda i: (prev_map(i), 1)),
                  pl.BlockSpec((CONV_HALO, C), lambda i: (next_map(i), 0)),
                  pl.BlockSpec((CONV_HALO, C), lambda i: (next_map(i), 1)),
                  pl.BlockSpec((tm, C), lambda i: (i, 0)),
                  pl.BlockSpec((CONV_HALO, C), lambda i: (next_map(i), 0)),
                  pl.BlockSpec((32, C), lambda i: (0, 0)),
                  pl.BlockSpec((1, C), lambda i: (0, 0)),
                  pl.BlockSpec((1, C), lambda i: (0, 0)),
                  pl.BlockSpec((1, C), lambda i: (0, 0))],
        out_specs=[pl.BlockSpec((tm, 2 * C), lambda i: (i, 0)),
                   pl.BlockSpec((32, C), lambda i: (0, 0)),
                   pl.BlockSpec((1, C), lambda i: (0, 0)),
                   pl.BlockSpec((1, C), lambda i: (0, 0)),
                   pl.BlockSpec((1, C), lambda i: (0, 0))],
        out_shape=[jax.ShapeDtypeStruct((T, 2 * C), BF16),
                   jax.ShapeDtypeStruct((32, C), F32),
                   jax.ShapeDtypeStruct((1, C), F32),
                   jax.ShapeDtypeStruct((1, C), F32),
                   jax.ShapeDtypeStruct((1, C), F32)],
        scratch_shapes=[pltpu.VMEM((tm + 2 * CONV_HALO, C), F32), pltpu.VMEM((R, C), F32),
                        pltpu.VMEM((7, tm + 2 * CONV_HALO - 8, C), F32), pltpu.VMEM((7, R - 8, C), F32)],
        compiler_params=_params(1),
    )(z, z, z, z, z, z, dm, dm, cw, cb, lg, lb)


def _swa_scores(q_h, kk_h, slope, bias_dist, valid, sink):
    s = _dot_nt(q_h, kk_h) * (HEAD_DIM ** -0.5) - slope * bias_dist
    s = jnp.where(valid, s, NEG_BIG)
    m = jnp.maximum(jnp.max(s, axis=-1, keepdims=True), sink)
    p = jnp.exp(s - m)
    e_sink = jnp.exp(sink - m)
    inv = 1.0 / (jnp.sum(p, axis=-1, keepdims=True) + e_sink)
    return p * inv, e_sink * inv


def _swa_mask(r0):
    qi = lax.broadcasted_iota(jnp.int32, (WINDOW, 2 * WINDOW), 0)
    kj = lax.broadcasted_iota(jnp.int32, (WINDOW, 2 * WINDOW), 1)
    dist = qi + WINDOW - kj
    valid = (dist >= 0) & (dist < WINDOW) & (r0 - WINDOW + kj >= 0)
    return dist.astype(F32), valid


def swa_fwd(z, kpad, vpad, sinks, name, tq=512):
    T = z.shape[0]
    tq = _tile(T, tq)
    HQ = SWA_HEADS * HEAD_DIM

    def body(sink_ref, q_ref, k_ref, v_ref, o_ref):
        i = pl.program_id(0)
        for sub in range(tq // WINDOW):
            r0 = pl.multiple_of(i * tq + sub * WINDOW, WINDOW)
            kk = k_ref[pl.ds(r0, 2 * WINDOW), :]
            vv = v_ref[pl.ds(r0, 2 * WINDOW), :]
            qb = q_ref[sub * WINDOW:(sub + 1) * WINDOW, :].astype(BF16)
            dist, valid = _swa_mask(r0)
            outs = []
            for h in range(SWA_HEADS):
                kh = h // SWA_GROUP
                ks = slice(kh * HEAD_DIM, (kh + 1) * HEAD_DIM)
                pn, _ = _swa_scores(qb[:, h * HEAD_DIM:(h + 1) * HEAD_DIM], kk[:, ks], 2.0 ** (-(h + 1)), dist, valid,
                                    sink_ref[h])
                outs.append(_dot(pn.astype(BF16), vv[:, ks]))
            o_ref[sub * WINDOW:(sub + 1) * WINDOW, :] = jnp.concatenate(outs, axis=-1).astype(BF16)

    return pl.pallas_call(
        body, name=name, grid=(T // tq,),
        in_specs=[pl.BlockSpec(memory_space=pltpu.SMEM),
                  pl.BlockSpec((tq, HQ), lambda i: (i, 2)),
                  pl.BlockSpec((T + WINDOW, 2 * HEAD_DIM), lambda i: (0, 0)),
                  pl.BlockSpec((T + WINDOW, 2 * HEAD_DIM), lambda i: (0, 0))],
        out_specs=pl.BlockSpec((tq, HQ), lambda i: (i, 0)),
        out_shape=jax.ShapeDtypeStruct((T, HQ), BF16),
        compiler_params=_params(1),
    )(sinks, z, kpad, vpad)


def swa_bwd(z, kpad, vpad, sinks, dm, name, tq=512):
    T = z.shape[0]
    tq = _tile(T, tq)
    HQ = SWA_HEADS * HEAD_DIM
    scale = HEAD_DIM ** -0.5

    def body(sink_ref, q_ref, k_ref, v_ref, do_ref, dq_ref, dk_ref, dv_ref, dsink_ref):
        i = pl.program_id(0)

        @pl.when(i == 0)
        def _():
            dk_ref[...] = jnp.zeros_like(dk_ref)
            dv_ref[...] = jnp.zeros_like(dv_ref)
            dsink_ref[...] = jnp.zeros_like(dsink_ref)

        for sub in range(tq // WINDOW):
            r0 = pl.multiple_of(i * tq + sub * WINDOW, WINDOW)
            kk = k_ref[pl.ds(r0, 2 * WINDOW), :]
            vv = v_ref[pl.ds(r0, 2 * WINDOW), :]
            rows = slice(sub * WINDOW, (sub + 1) * WINDOW)
            qb = q_ref[rows, :].astype(BF16)
            dob = do_ref[rows, :].astype(BF16)
            dist, valid = _swa_mask(r0)
            dqs, dks, dvs = [], [], []
            for kh in range(SWA_KV_HEADS):
                ks = slice(kh * HEAD_DIM, (kh + 1) * HEAD_DIM)
                dk_acc = jnp.zeros((2 * WINDOW, HEAD_DIM), F32)
                dv_acc = jnp.zeros((2 * WINDOW, HEAD_DIM), F32)
                for g in range(SWA_GROUP):
                    h = kh * SWA_GROUP + g
                    hs = slice(h * HEAD_DIM, (h + 1) * HEAD_DIM)
                    pn, p_sink = _swa_scores(qb[:, hs], kk[:, ks], 2.0 ** (-(h + 1)), dist, valid, sink_ref[h])
                    dp = _dot_nt(dob[:, hs], vv[:, ks])
                    delta = jnp.sum(pn * dp, axis=-1, keepdims=True)
                    ds = (pn * (dp - delta)).astype(BF16)
                    dqs.append(_dot(ds, kk[:, ks]) * scale)
                    dk_acc = dk_acc + _dot_tn(ds, qb[:, hs]) * scale
                    dv_acc = dv_acc + _dot_tn(pn.astype(BF16), dob[:, hs])
                    dsink_ref[h:h + 1, :] += jnp.zeros((1, 128), F32) - jnp.sum(p_sink * delta)
                dks.append(dk_acc)
                dvs.append(dv_acc)
            dq_ref[rows, :] = jnp.concatenate(dqs, axis=-1).astype(BF16)
            dk_ref[pl.ds(r0, 2 * WINDOW), :] += jnp.concatenate(dks, axis=-1)
            dv_ref[pl.ds(r0, 2 * WINDOW), :] += jnp.concatenate(dvs, axis=-1)

    kv_spec = pl.BlockSpec((T + WINDOW, 2 * HEAD_DIM), lambda i: (0, 0))
    return pl.pallas_call(
        body, name=name, grid=(T // tq,),
        in_specs=[pl.BlockSpec(memory_space=pltpu.SMEM),
                  pl.BlockSpec((tq, HQ), lambda i: (i, 2)),
                  kv_spec, kv_spec,
                  pl.BlockSpec((tq, HQ), lambda i: (i, 1))],
        out_specs=[pl.BlockSpec((tq, HQ), lambda i: (i, 0)),
                   kv_spec, kv_spec,
                   pl.BlockSpec((SWA_HEADS, 128), lambda i: (0, 0))],
        out_shape=[jax.ShapeDtypeStruct((T, HQ), BF16),
                   jax.ShapeDtypeStruct((T + WINDOW, 2 * HEAD_DIM), F32),
                   jax.ShapeDtypeStruct((T + WINDOW, 2 * HEAD_DIM), F32),
                   jax.ShapeDtypeStruct((SWA_HEADS, 128), F32)],
        compiler_params=_params(1),
    )(sinks, z, kpad, vpad, dm)


def short_conv_fwd(z, w, name, tm=512):
    T = z.shape[0]
    C = SC_CH
    tm = _tile(T, tm)
    hb = tm // SC_HALO

    def body(b_ref, c_ref, v_ref, pc_ref, pv_ref, w_ref, o_ref, xs_ref):
        i = pl.program_id(0)
        xs_ref[0:SC_HALO, :] = jnp.where(i > 0, pc_ref[...] * pv_ref[...], 0.0)
        xs_ref[SC_HALO:, :] = c_ref[...] * v_ref[...]
        conv = jnp.zeros((tm, C), F32)
        for k in range(3):
            conv = conv + w_ref[k:k + 1, :] * xs_ref[pl.ds(SC_HALO - 2 + k, tm), :]
        o_ref[...] = (b_ref[...] * conv).astype(BF16)

    prev_map = lambda i: jnp.maximum(i * hb - 1, 0)
    return pl.pallas_call(
        body, name=name, grid=(T // tm,),
        in_specs=[pl.BlockSpec((tm, C), lambda i: (i, 0)),
                  pl.BlockSpec((tm, C), lambda i: (i, 1)),
                  pl.BlockSpec((tm, C), lambda i: (i, 2)),
                  pl.BlockSpec((SC_HALO, C), lambda i: (prev_map(i), 1)),
                  pl.BlockSpec((SC_HALO, C), lambda i: (prev_map(i), 2)),
                  pl.BlockSpec((8, C), lambda i: (0, 0))],
        out_specs=pl.BlockSpec((tm, C), lambda i: (i, 0)),
        out_shape=jax.ShapeDtypeStruct((T, C), BF16),
        scratch_shapes=[pltpu.VMEM((tm + SC_HALO, C), F32)],
        compiler_params=_params(1),
    )(z, z, z, z, z, w)


def short_conv_bwd(z, dm, w, name, tm=512):
    T = z.shape[0]
    C = SC_CH
    tm = _tile(T, tm)
    hb = tm // SC_HALO
    n_tiles = T // tm
    last_halo = T // SC_HALO - 1
    R = tm + SC_HALO

    def body(b_ref, c_ref, v_ref, pc_ref, pv_ref, nb_ref, do_ref, ndo_ref, w_ref, dz_ref, dw_ref, xs_ref, ds_ref):
        i = pl.program_id(0)

        @pl.when(i == 0)
        def _():
            dw_ref[...] = jnp.zeros_like(dw_ref)

        c = c_ref[...]
        val = v_ref[...]
        dout = do_ref[...]
        xs_ref[0:SC_HALO, :] = jnp.where(i > 0, pc_ref[...] * pv_ref[...], 0.0)
        xs_ref[SC_HALO:, :] = c * val
        dconv = dout * b_ref[...]
        ds_ref[0:tm, :] = dconv
        ds_ref[tm:, :] = jnp.where(i < n_tiles - 1, ndo_ref[...] * nb_ref[...], 0.0)
        conv = jnp.zeros((tm, C), F32)
        dcv = jnp.zeros((tm, C), F32)
        for k in range(3):
            xk = xs_ref[pl.ds(SC_HALO - 2 + k, tm), :]
            conv = conv + w_ref[k:k + 1, :] * xk
            dw_ref[k:k + 1, :] += jnp.sum(dconv * xk, axis=0, keepdims=True)
            dcv = dcv + w_ref[k:k + 1, :] * ds_ref[pl.ds(2 - k, tm), :]
        dz_ref[:, 0:C] = (dout * conv).astype(BF16)
        dz_ref[:, C:2 * C] = (dcv * val).astype(BF16)
        dz_ref[:, 2 * C:] = (dcv * c).astype(BF16)

    prev_map = lambda i: jnp.maximum(i * hb - 1, 0)
    next_map = lambda i: jnp.minimum((i + 1) * hb, last_halo)
    return pl.pallas_call(
        body, name=name, grid=(n_tiles,),
        in_specs=[pl.BlockSpec((tm, C), lambda i: (i, 0)),
                  pl.BlockSpec((tm, C), lambda i: (i, 1)),
                  pl.BlockSpec((tm, C), lambda i: (i, 2)),
                  pl.BlockSpec((SC_HALO, C), lambda i: (prev_map(i), 1)),
                  pl.BlockSpec((SC_HALO, C), lambda i: (prev_map(i), 2)),
                  pl.BlockSpec((SC_HALO, C), lambda i: (next_map(i), 0)),
                  pl.BlockSpec((tm, C), lambda i: (i, 0)),
                  pl.BlockSpec((SC_HALO, C), lambda i: (next_map(i), 0)),
                  pl.BlockSpec((8, C), lambda i: (0, 0))],
        out_specs=[pl.BlockSpec((tm, 3 * C), lambda i: (i, 0)),
                   pl.BlockSpec((8, C), lambda i: (0, 0))],
        out_shape=[jax.ShapeDtypeStruct((T, 3 * C), BF16), jax.ShapeDtypeStruct((8, C), F32)],
        scratch_shapes=[pltpu.VMEM((tm + SC_HALO, C), F32), pltpu.VMEM((R, C), F32)],
        compiler_params=_params(1),
    )(z, z, z, z, z, z, dm, dm, w)


def _xa_probs(q_h, k_h):
    s = _dot_nt(q_h, k_h) * (XA_HEAD_DIM ** -0.5)
    p = jnp.exp(s - jnp.max(s, axis=-1, keepdims=True))
    return p * (1.0 / jnp.sum(p, axis=-1, keepdims=True))


def xattn_fwd(q, kv, name, tm=512):
    T = q.shape[0]
    M = kv.shape[0]
    tm = _tile(T, tm)

    def body(q_ref, k_ref, v_ref, o_ref):
        for h in range(XA_HEADS):
            hs = slice(h * XA_HEAD_DIM, (h + 1) * XA_HEAD_DIM)
            p = _xa_probs(q_ref[:, hs], k_ref[:, hs])
            o_ref[:, hs] = _dot(p.astype(BF16), v_ref[:, hs]).astype(BF16)

    return pl.pallas_call(
        body, name=name, grid=(T // tm,),
        in_specs=[pl.BlockSpec((tm, D_MODEL), lambda i: (i, 0)),
                  pl.BlockSpec((M, D_MODEL), lambda i: (0, 0)),
                  pl.BlockSpec((M, D_MODEL), lambda i: (0, 1))],
        out_specs=pl.BlockSpec((tm, D_MODEL), lambda i: (i, 0)),
        out_shape=jax.ShapeDtypeStruct((T, D_MODEL), BF16),
        compiler_params=_params(1),
    )(q, kv, kv)


def xattn_bwd(q, kv, do, name, tm=512):
    T = q.shape[0]
    M = kv.shape[0]
    tm = _tile(T, tm)
    scale = XA_HEAD_DIM ** -0.5

    def body(q_ref, k_ref, v_ref, do_ref, dq_ref, dkv_ref):
        @pl.when(pl.program_id(0) == 0)
        def _():
            dkv_ref[...] = jnp.zeros_like(dkv_ref)

        for h in range(XA_HEADS):
            hs = slice(h * XA_HEAD_DIM, (h + 1) * XA_HEAD_DIM)
            vs = slice(D_MODEL + h * XA_HEAD_DIM, D_MODEL + (h + 1) * XA_HEAD_DIM)
            q_h = q_ref[:, hs]
            do_h = do_ref[:, hs]
            p = _xa_probs(q_h, k_ref[:, hs])
            dp = _dot_nt(do_h, v_ref[:, hs])
            ds = (p * (dp - jnp.sum(p * dp, axis=-1, keepdims=True))).astype(BF16)
            dq_ref[:, hs] = (_dot(ds, k_ref[:, hs]) * scale).astype(BF16)
            dkv_ref[:, hs] += _dot_tn(ds, q_h) * scale
            dkv_ref[:, vs] += _dot_tn(p.astype(BF16), do_h)

    return pl.pallas_call(
        body, name=name, grid=(T // tm,),
        in_specs=[pl.BlockSpec((tm, D_MODEL), lambda i: (i, 0)),
                  pl.BlockSpec((M, D_MODEL), lambda i: (0, 0)),
                  pl.BlockSpec((M, D_MODEL), lambda i: (0, 1)),
                  pl.BlockSpec((tm, D_MODEL), lambda i: (i, 0))],
        out_specs=[pl.BlockSpec((tm, D_MODEL), lambda i: (i, 0)),
                   pl.BlockSpec((M, 2 * D_MODEL), lambda i: (0, 0))],
        out_shape=[jax.ShapeDtypeStruct((T, D_MODEL), BF16), jax.ShapeDtypeStruct((M, 2 * D_MODEL), F32)],
        compiler_params=_params(1),
    )(q, kv, kv, do)


def final_loss(h, g, target, name, tm=512):
    T, K = h.shape
    tm = _tile(T, tm)

    def body(h_ref, g_ref, t_ref, dh_ref, dg_ref, loss_ref):
        @pl.when(pl.program_id(0) == 0)
        def _():
            dg_ref[...] = jnp.zeros_like(dg_ref)
            loss_ref[...] = jnp.zeros_like(loss_ref)

        x = h_ref[...]
        r = lax.rsqrt(jnp.mean(x * x, axis=-1, keepdims=True) + RMS_EPS)
        xh = x * r
        e = xh * g_ref[...] - t_ref[...]
        loss_ref[...] += jnp.zeros((1, 128), F32) + 0.5 * jnp.sum(jnp.mean(e * e, axis=-1, keepdims=True))
        dy = e * (1.0 / K)
        dg_ref[...] += jnp.sum(dy * xh, axis=0, keepdims=True)
        dxh = dy * g_ref[...]
        dh_ref[...] = r * (dxh - xh * jnp.mean(dxh * xh, axis=-1, keepdims=True))

    return pl.pallas_call(
        body, name=name, grid=(T // tm,),
        in_specs=[pl.BlockSpec((tm, K), lambda i: (i, 0)),
                  pl.BlockSpec((1, K), lambda i: (0, 0)),
                  pl.BlockSpec((tm, K), lambda i: (i, 0))],
        out_specs=[pl.BlockSpec((tm, K), lambda i: (i, 0)),
                   pl.BlockSpec((1, K), lambda i: (0, 0)),
                   pl.BlockSpec((1, 128), lambda i: (0, 0))],
        out_shape=[jax.ShapeDtypeStruct((T, K), F32), jax.ShapeDtypeStruct((1, K), F32),
                   jax.ShapeDtypeStruct((1, 128), F32)],
        compiler_params=_params(1),
    )(h, g, target)


def _row(v):
    return v.reshape(1, -1)


def _pad_rows(a, rows):
    return jnp.pad(a, ((0, rows - a.shape[0]), (0, 0)))


def local_step(x, mem, target, P, get_weights, put_grads):
    cw = _pad_rows(P["conv_a_w"], 32)
    scw = _pad_rows(P["sc_conv_w"], 8)
    cb, lg, lb = _row(P["conv_a_b"]), _row(P["conv_a_ln_g"]), _row(P["conv_a_ln_b"])
    sinks = P["swa_sinks"]

    class _Layered:
        def __init__(self, store, name=None):
            self.store, self.name = store, name

        def __getitem__(self, key):
            if self.name is None:
                return self.store[(key, 0)] if key in ("even_w_in", "even_w_out", "odd_w_in", "odd_w_out") \
                    else _Layered(self.store, key)
            return self.store[(self.name, key)]

    store = {}
    W = _Layered(store)
    saved = []
    h = x
    for i in range(2):
        L = f"l{i}"
        new, dep = get_weights("A" if i == 0 else "D", h)
        store.update(new)
        h, s_ffn1 = ffn_forward(h, P["ffn1_norm"][i:i + 1], W["ffn1_w_gu"][i], W["ffn1_w_down"][i], L + "_ffn1", dep=dep)
        h1 = h
        if i == 0:
            new, _ = get_weights("B", h)
            store.update(new)
            u2 = rmsnorm(h1, P["mix_norm"][i:i + 1], L + "_mix_in_norm")
            z = matmul(u2, W["even_w_in"], F32, L + "_mix_in", 768, n_tiles=2)
            kv = matmul(u2, W["even_w_in"], BF16, L + "_mix_kv", 256, n_tiles=1, first_tile=6)
            a = conformer_conv_fwd(z, cw, cb, lg, lb, L + "_conv")
            kpad = jnp.pad(kv[:, :2 * HEAD_DIM], ((WINDOW, 0), (0, 0)))
            vpad = jnp.pad(kv[:, 2 * HEAD_DIM:], ((WINDOW, 0), (0, 0)))
            o = swa_fwd(z, kpad, vpad, sinks, L + "_swa")
            m = jnp.concatenate([a, o], axis=-1)
            h = matmul_residual(m, W["even_w_out"], h1, L + "_mix_out")
            s_mix = (h1, u2, z, m, kpad, vpad)
        else:
            z, u2 = norm_matmul(h1, P["mix_norm"][i:i + 1], W["odd_w_in"], F32, L + "_mix_in", 1024)
            m = short_conv_fwd(z, scw, L + "_sconv")
            h = matmul_residual(m, W["odd_w_out"], h1, L + "_mix_out")
            s_mix = (h1, u2, z, m)
        h2 = h
        kv, umem = norm_matmul(mem, P["xa_mem_norm"][i:i + 1], W["xa_wkv"][i], BF16, L + "_xa_kv", 2 * D_MODEL)
        q, u3 = norm_matmul(h2, P["xa_norm"][i:i + 1], W["xa_wq"][i], BF16, L + "_xa_q", D_MODEL, transposed=False)
        o = xattn_fwd(q, kv, L + "_xa")
        h = matmul_residual(o, W["xa_wo"][i], h2, L + "_xa_out")
        s_xa = (h2, u3, q, o, kv, umem)
        if i == 0:
            new, _ = get_weights("C", h)
            store.update(new)
        h, s_ffn2 = ffn_forward(h, P["ffn2_norm"][i:i + 1], W["ffn2_w_gu"][i], W["ffn2_w_down"][i], L + "_ffn2")
        saved.append((s_ffn1, s_mix, s_xa, s_ffn2))

    dh, d_final, loss = final_loss(h, _row(P["final_norm"]), target, "final_loss")

    names = ("ffn1_w_gu", "ffn1_w_down", "ffn2_w_gu", "ffn2_w_down", "xa_wq", "xa_wkv", "xa_wo", "even_w_in", "even_w_out",
             "odd_w_in", "odd_w_out")
    dW = {k: [None, None] for k in names}
    dP = {k: [None, None] for k in ("ffn1_norm", "mix_norm", "xa_norm", "xa_mem_norm", "ffn2_norm")}
    dP["final_norm"] = d_final.reshape(-1)
    for i in (1, 0):
        L = f"l{i}b"
        s_ffn1, s_mix, s_xa, s_ffn2 = saved[i]

        def keep_ffn2(d_w_gu, d_w_down, i=i):
            dW["ffn2_w_gu"][i], dW["ffn2_w_down"][i] = d_w_gu, d_w_down

        def send_ffn1(d_w_gu, d_w_down, i=i):
            dW["ffn1_w_gu"][i], dW["ffn1_w_down"][i] = d_w_gu, d_w_down
            stage = "D" if i == 1 else "A"
            return put_grads(stage, {k: dW[k[0]][k[1]] for k in STAGE_KEYS[stage]})

        dh, dP["ffn2_norm"][i] = ffn_backward(
            dh, s_ffn2, P["ffn2_norm"][i:i + 1], W["ffn2_w_gu"][i], W["ffn2_w_down"][i], L + "_ffn2", emit=keep_ffn2)
        h2, u3, q, o, kv, umem = s_xa
        dW["xa_wo"][i] = matmul_tn(o, dh, L + "_xa_dwo")
        do = matmul_nt(dh, W["xa_wo"][i], BF16, L + "_xa_do")
        dq, dkv = xattn_bwd(q, kv, do, L + "_xa")
        dW["xa_wq"][i] = matmul_tn(u3, dq, L + "_xa_dwq")
        dW["xa_wkv"][i] = matmul_tn(dkv, umem, L + "_xa_dwkv", tk=1024)
        dkv_b = dkv.astype(BF16)
        _, dP["xa_mem_norm"][i] = matmul_norm_bwd(dkv_b, W["xa_wkv"][i], mem, P["xa_mem_norm"][i:i + 1],
                                                  jnp.zeros_like(mem), L + "_xa_dmem")
        dh, dP["xa_norm"][i] = matmul_norm_bwd(dq, W["xa_wq"][i], h2, P["xa_norm"][i:i + 1], dh, L + "_xa_dx",
                                               transposed=False)
        if i == 0:
            h1, u2, z, m, kpad, vpad = s_mix
            dW["even_w_out"][0] = matmul_tn(m, dh, L + "_mix_dwo")
            dm = matmul_nt(dh, W["even_w_out"], F32, L + "_mix_dm")
            dz_conv, dcw, dcb, dlg, dlb = conformer_conv_bwd(z, dm, cw, cb, lg, lb, L + "_conv")
            dq_s, dkp, dvp, dsk = swa_bwd(z, kpad, vpad, sinks, dm, L + "_swa")
            dz = jnp.concatenate([dz_conv, dq_s, dkp[WINDOW:].astype(BF16), dvp[WINDOW:].astype(BF16)], axis=-1)
            dW["even_w_in"][0] = matmul_tn(dz, u2, L + "_mix_dwi", tk=896)
            dh, dP["mix_norm"][i] = matmul_norm_bwd(dz, W["even_w_in"], h1, P["mix_norm"][i:i + 1], dh, L + "_mix_dx")
            dP["conv_a_w"] = dcw[:CONV_A_WIDTH]
            dP["conv_a_b"], dP["conv_a_ln_g"], dP["conv_a_ln_b"] = dcb.reshape(-1), dlg.reshape(-1), dlb.reshape(-1)
            dP["swa_sinks"] = dsk[:, 0]
        else:
            h1, u2, z, m = s_mix
            dW["odd_w_out"][0] = matmul_tn(m, dh, L + "_mix_dwo")
            dm = matmul_nt(dh, W["odd_w_out"], F32, L + "_mix_dm")
            dz, dscw = short_conv_bwd(z, dm, scw, L + "_sconv")
            dW["odd_w_in"][0] = matmul_tn(dz, u2, L + "_mix_dwi", tk=1024)
            dh, dP["mix_norm"][i] = matmul_norm_bwd(dz, W["odd_w_in"], h1, P["mix_norm"][i:i + 1], dh, L + "_mix_dx")
            dP["sc_conv_w"] = dscw[:3]
        dep = put_grads("BC", {k: dW[k[0]][k[1]] for k in STAGE_KEYS["B"] + STAGE_KEYS["C"]}) if i == 0 else None
        dh, dP["ffn1_norm"][i] = ffn_backward(
            dh, s_ffn1, P["ffn1_norm"][i:i + 1], W["ffn1_w_gu"][i], W["ffn1_w_down"][i], L + "_ffn1", dep=dep,
            emit=send_ffn1)
    for k in ("ffn1_norm", "mix_norm", "xa_norm", "xa_mem_norm", "ffn2_norm"):
        dP[k] = jnp.concatenate(dP[k], axis=0)
    return loss, dh, dP


def _mesh_pos():
    return lax.axis_index("x"), lax.axis_index("y"), lax.axis_index("c")


def _flat_index(px, py, pc):
    return 4 * px + 2 * py + pc


def all_gather(blob, name):
    R, C = blob.shape

    def body(x_ref, out_ref, send_sems, recv_sems, local_sem):
        x, y, c = _mesh_pos()
        me, sibling = (x, y, c), (x, y, 1 - c)
        chips = [(1 - x, y), (x, 1 - y), (1 - x, 1 - y)]

        def slot(px, py, pc):
            return out_ref.at[_flat_index(px, py, pc)]

        def copy(k, block, to, src=None):
            return pltpu.make_async_remote_copy(
                src_ref=slot(*block) if src is None else src, dst_ref=slot(*block),
                send_sem=send_sems.at[k], recv_sem=recv_sems.at[k],
                device_id=to, device_id_type=pl.DeviceIdType.MESH)

        mine = pltpu.make_async_copy(x_ref, slot(*me), local_sem)
        mine.start()
        first = [copy(0, me, sibling, src=x_ref)]
        first += [copy(1 + j, me, (*chip, c), src=x_ref) for j, chip in enumerate(chips)]
        for cp in first:
            cp.start()
        passed = [copy(4 + j, (*chip, c), sibling) for j, chip in enumerate(chips)]
        for j, chip in enumerate(chips):
            copy(1 + j, (*chip, c), me).wait_recv()
            passed[j].start()
        copy(0, sibling, me).wait_recv()
        for j, chip in enumerate(chips):
            copy(4 + j, (*chip, 1 - c), me).wait_recv()
        for cp in first + passed:
            cp.wait_send()
        mine.wait()

    return pl.pallas_call(
        body, name=name,
        out_shape=jax.ShapeDtypeStruct((N_DEV, R, C), blob.dtype),
        in_specs=[pl.BlockSpec(memory_space=pl.ANY)],
        out_specs=pl.BlockSpec(memory_space=pl.ANY),
        scratch_shapes=[pltpu.SemaphoreType.DMA((7,)), pltpu.SemaphoreType.DMA((7,)), pltpu.SemaphoreType.DMA],
    )(blob)


def scatter_exchange(g, name):
    _, R, C = g.shape

    def body(g_ref, out_ref, send_sems, recv_sems, local_sem):
        x, y, c = _mesh_pos()
        me_idx = _flat_index(x, y, c)
        mine = pltpu.make_async_copy(g_ref.at[me_idx], out_ref.at[me_idx], local_sem)
        mine.start()
        sends, peers = [], []
        for k in range(1, N_DEV):
            px = 1 - x if k & 4 else x
            py = 1 - y if k & 2 else y
            pc = 1 - c if k & 1 else c
            peer_idx = _flat_index(px, py, pc)
            cp = pltpu.make_async_remote_copy(
                src_ref=g_ref.at[peer_idx], dst_ref=out_ref.at[me_idx],
                send_sem=send_sems.at[k - 1], recv_sem=recv_sems.at[k - 1],
                device_id=(px, py, pc), device_id_type=pl.DeviceIdType.MESH)
            cp.start()
            sends.append(cp)
            peers.append((peer_idx, (px, py, pc)))
        for k in range(1, N_DEV):
            peer_idx, peer = peers[k - 1]
            pltpu.make_async_remote_copy(
                src_ref=g_ref.at[me_idx], dst_ref=out_ref.at[peer_idx],
                send_sem=send_sems.at[k - 1], recv_sem=recv_sems.at[k - 1],
                device_id=peer, device_id_type=pl.DeviceIdType.MESH).wait_recv()
        for cp in sends:
            cp.wait_send()
        mine.wait()

    return pl.pallas_call(
        body, name=name,
        out_shape=jax.ShapeDtypeStruct(g.shape, g.dtype),
        in_specs=[pl.BlockSpec(memory_space=pl.ANY)],
        out_specs=pl.BlockSpec(memory_space=pl.ANY),
        scratch_shapes=[pltpu.SemaphoreType.DMA((7,)), pltpu.SemaphoreType.DMA((7,)), pltpu.SemaphoreType.DMA],
    )(g)


HBM_SPEC = pl.BlockSpec(memory_space=pltpu.HBM)
SEM_SPEC = pl.BlockSpec(memory_space=pltpu.SEMAPHORE)
DATAFLOW_EFFECT = pltpu.SideEffectType.DATAFLOW_SIDE_EFFECTING


def _peers(x, y, c):
    out = []
    for k in range(1, N_DEV):
        pos = (1 - x if k & 4 else x, 1 - y if k & 2 else y, 1 - c if k & 1 else c)
        out.append((_flat_index(*pos), pos))
    return out


def _exchange_copy(src_ref, land_ref, send_sems, recv_sems, j, me, peer_idx, peer, scatter):
    return pltpu.make_async_remote_copy(
        src_ref=src_ref.at[peer_idx] if scatter else src_ref, dst_ref=land_ref.at[me],
        send_sem=send_sems.at[j], recv_sem=recv_sems.at[j], device_id=peer, device_id_type=pl.DeviceIdType.MESH)


def exchange_start(srcs, lands, scatter, after, name):
    n = len(srcs)
    n_after = len(after)

    def body(*refs):
        src_refs, land_refs = refs[:n], refs[n:2 * n]
        outs = refs[2 * n + n_after:]
        send_sems, recv_sems, token = outs[:n], outs[n:2 * n], outs[4 * n]
        x, y, c = _mesh_pos()
        me = _flat_index(x, y, c)
        for g in range(n):
            for j, (peer_idx, peer) in enumerate(_peers(x, y, c)):
                _exchange_copy(src_refs[g], land_refs[g], send_sems[g], recv_sems[g], j, me, peer_idx, peer, scatter).start()
        token[...] = jnp.zeros_like(token)

    hbm = lambda a: pltpu.with_memory_space_constraint(a, pltpu.HBM)
    res = pl.pallas_call(
        body, name=name,
        out_shape=(*[pltpu.SemaphoreType.DMA((N_DEV - 1,))] * (2 * n),
                   *[pltpu.HBM(a.shape, a.dtype) for a in srcs], *[pltpu.HBM(a.shape, a.dtype) for a in lands],
                   jax.ShapeDtypeStruct((8, 128), F32)),
        in_specs=[HBM_SPEC] * (2 * n) + [ANY_SPEC] * n_after,
        out_specs=(*[SEM_SPEC] * (2 * n), *[HBM_SPEC] * (2 * n), pl.BlockSpec(memory_space=pltpu.VMEM)),
        input_output_aliases={i: 2 * n + i for i in range(2 * n)},
        compiler_params=pltpu.CompilerParams(has_side_effects=DATAFLOW_EFFECT),
    )(*[hbm(a) for a in srcs], *[hbm(a) for a in lands], *after)
    handles = [(res[g], res[n + g], res[2 * n + g], res[3 * n + g]) for g in range(n)]
    return handles, res[4 * n]


def exchange_wait(handle, scatter, after, name):
    send_sem, recv_sem, src_thru, land_thru = handle

    def body(src_ref, land_ref, send_sems, recv_sems, after_ref, src_dead, got_ref):
        x, y, c = _mesh_pos()
        me = _flat_index(x, y, c)
        for j, (peer_idx, peer) in enumerate(_peers(x, y, c)):
            mine = _exchange_copy(src_ref, land_ref, send_sems, recv_sems, j, me, peer_idx, peer, scatter)
            mine.wait_send()
            theirs = pltpu.make_async_remote_copy(
                src_ref=src_ref.at[me] if scatter else src_ref, dst_ref=land_ref.at[peer_idx],
                send_sem=send_sems.at[j], recv_sem=recv_sems.at[j], device_id=peer, device_id_type=pl.DeviceIdType.MESH)
            theirs.wait_recv()

    return pl.pallas_call(
        body, name=name,
        out_shape=(pltpu.HBM(src_thru.shape, src_thru.dtype), pltpu.HBM(land_thru.shape, land_thru.dtype)),
        in_specs=(HBM_SPEC, HBM_SPEC, SEM_SPEC, SEM_SPEC, ANY_SPEC), out_specs=(HBM_SPEC, HBM_SPEC),
        input_output_aliases={0: 0, 1: 1},
        compiler_params=pltpu.CompilerParams(has_side_effects=DATAFLOW_EFFECT),
    )(src_thru, land_thru, send_sem, recv_sem, after)


def ordered_sum(parts, name, tr=512):
    n, R, C = parts.shape
    tr = next((t for t in range(min(tr, R), 15, -16) if R % t == 0), R)

    def body(p_ref, o_ref):
        acc = p_ref[0].astype(F32)
        for j in range(1, n):
            acc = acc + p_ref[j].astype(F32)
        o_ref[...] = acc

    return pl.pallas_call(
        body, name=name, grid=(R // tr,),
        in_specs=[pl.BlockSpec((n, tr, C), lambda i: (0, i, 0))],
        out_specs=pl.BlockSpec((tr, C), lambda i: (i, 0)),
        out_shape=jax.ShapeDtypeStruct((R, C), F32),
        compiler_params=_params(1),
    )(parts)


def adamw(w, g, m, v, name, tr=256):
    R, C = w.shape
    tr = next((t for t in range(tr, 7, -8) if R % t == 0), R)
    c1 = 1.0 - ADAM_B1 ** ADAM_STEP
    c2 = 1.0 - ADAM_B2 ** ADAM_STEP

    def body(w_ref, g_ref, m_ref, v_ref, d_ref, mo_ref, vo_ref):
        grad = g_ref[...]
        m2 = ADAM_B1 * m_ref[...] + (1.0 - ADAM_B1) * grad
        v2 = ADAM_B2 * v_ref[...] + (1.0 - ADAM_B2) * (grad * grad)
        mo_ref[...] = m2
        vo_ref[...] = v2
        d_ref[...] = -ADAM_LR * ((m2 / c1) / (jnp.sqrt(v2 / c2) + ADAM_EPS) + ADAM_WD * w_ref[...])

    spec = pl.BlockSpec((tr, C), lambda i: (i, 0))
    return pl.pallas_call(
        body, name=name, grid=(R // tr,),
        in_specs=[spec] * 4, out_specs=[spec] * 3,
        out_shape=[jax.ShapeDtypeStruct((R, C), F32)] * 3,
        compiler_params=_params(1),
    )(w, g, m, v)


WEIGHT_NAMES = ("ffn1_norm", "ffn1_w_gu", "ffn1_w_down", "mix_norm", "even_w_in", "conv_a_w", "conv_a_b", "conv_a_ln_g",
                "conv_a_ln_b", "swa_sinks", "even_w_out", "odd_w_in", "sc_conv_w", "odd_w_out", "xa_norm", "xa_mem_norm",
                "xa_wq", "xa_wkv", "xa_wo", "ffn2_norm", "ffn2_w_gu", "ffn2_w_down", "final_norm")
BLOB_COLS = 1024
SMALL_ROWS = (("ffn1_norm", 0, 2), ("mix_norm", 2, 2), ("xa_norm", 4, 2), ("xa_mem_norm", 6, 2), ("ffn2_norm", 8, 2),
              ("final_norm", 10, 1))
ROW_CONV_B_LNG = 11
ROW_LNB_SINKS_LOSS = 12
LOSS_COL = 512 + SWA_HEADS
ROW_SC_CONV = 13
ROW_CONV_W = 16
SMALL_BLOB_ROWS = 32
SMALL_ADAM_ROWS = 16


def _small_blob(v):
    rows = [v[n].reshape(-1, D_MODEL) for n, _, _ in SMALL_ROWS]
    rows.append(jnp.concatenate([v["conv_a_b"].reshape(-1), v["conv_a_ln_g"].reshape(-1)]).reshape(1, D_MODEL))
    tail = jnp.zeros((D_MODEL - 512 - SWA_HEADS,), F32)
    if "loss" in v:
        tail = tail.at[0].set(v["loss"])
    rows.append(jnp.concatenate([v["conv_a_ln_b"].reshape(-1), v["swa_sinks"].reshape(-1), tail]).reshape(1, D_MODEL))
    rows.append(jnp.zeros((SMALL_ADAM_ROWS - ROW_SC_CONV, D_MODEL), F32))
    return jnp.concatenate(rows, axis=0)


def _small_unblob(b, shapes):
    out = {n: b[r:r + k].reshape(shapes[n]) for n, r, k in SMALL_ROWS}
    out["conv_a_b"] = b[ROW_CONV_B_LNG, :512].reshape(shapes["conv_a_b"])
    out["conv_a_ln_g"] = b[ROW_CONV_B_LNG, 512:].reshape(shapes["conv_a_ln_g"])
    out["conv_a_ln_b"] = b[ROW_LNB_SINKS_LOSS, :512].reshape(shapes["conv_a_ln_b"])
    out["swa_sinks"] = b[ROW_LNB_SINKS_LOSS, 512:512 + SWA_HEADS].reshape(shapes["swa_sinks"])
    return out


def kernel(x, mem, ffn1_norm, ffn1_w_gu, ffn1_w_down, mix_norm, even_w_in, conv_a_w, conv_a_b, conv_a_ln_g, conv_a_ln_b, swa_sinks, even_w_out, odd_w_in, sc_conv_w, odd_w_out, xa_norm, xa_mem_norm, xa_wq, xa_wkv, xa_wo, ffn2_norm, ffn2_w_gu, ffn2_w_down, final_norm, loss_target, m_ffn1_norm, m_ffn1_w_gu, m_ffn1_w_down, m_mix_norm, m_even_w_in, m_conv_a_w, m_conv_a_b, m_conv_a_ln_g, m_conv_a_ln_b, m_swa_sinks, m_even_w_out, m_odd_w_in, m_sc_conv_w, m_odd_w_out, m_xa_norm, m_xa_mem_norm, m_xa_wq, m_xa_wkv, m_xa_wo, m_ffn2_norm, m_ffn2_w_gu, m_ffn2_w_down, m_final_norm, v_ffn1_norm, v_ffn1_w_gu, v_ffn1_w_down, v_mix_norm, v_even_w_in, v_conv_a_w, v_conv_a_b, v_conv_a_ln_g, v_conv_a_ln_b, v_swa_sinks, v_even_w_out, v_odd_w_in, v_sc_conv_w, v_odd_w_out, v_xa_norm, v_xa_mem_norm, v_xa_wq, v_xa_wkv, v_xa_wo, v_ffn2_norm, v_ffn2_w_gu, v_ffn2_w_down, v_final_norm):
    w = dict(ffn1_norm=ffn1_norm, ffn1_w_gu=ffn1_w_gu, ffn1_w_down=ffn1_w_down, mix_norm=mix_norm, even_w_in=even_w_in,
             conv_a_w=conv_a_w, conv_a_b=conv_a_b, conv_a_ln_g=conv_a_ln_g, conv_a_ln_b=conv_a_ln_b, swa_sinks=swa_sinks,
             even_w_out=even_w_out, odd_w_in=odd_w_in, sc_conv_w=sc_conv_w, odd_w_out=odd_w_out, xa_norm=xa_norm,
             xa_mem_norm=xa_mem_norm, xa_wq=xa_wq, xa_wkv=xa_wkv, xa_wo=xa_wo, ffn2_norm=ffn2_norm, ffn2_w_gu=ffn2_w_gu,
             ffn2_w_down=ffn2_w_down, final_norm=final_norm)
    m = dict(ffn1_norm=m_ffn1_norm, ffn1_w_gu=m_ffn1_w_gu, ffn1_w_down=m_ffn1_w_down, mix_norm=m_mix_norm,
             even_w_in=m_even_w_in, conv_a_w=m_conv_a_w, conv_a_b=m_conv_a_b, conv_a_ln_g=m_conv_a_ln_g,
             conv_a_ln_b=m_conv_a_ln_b, swa_sinks=m_swa_sinks, even_w_out=m_even_w_out, odd_w_in=m_odd_w_in,
             sc_conv_w=m_sc_conv_w, odd_w_out=m_odd_w_out, xa_norm=m_xa_norm, xa_mem_norm=m_xa_mem_norm, xa_wq=m_xa_wq,
             xa_wkv=m_xa_wkv, xa_wo=m_xa_wo, ffn2_norm=m_ffn2_norm, ffn2_w_gu=m_ffn2_w_gu, ffn2_w_down=m_ffn2_w_down,
             final_norm=m_final_norm)
    v = dict(ffn1_norm=v_ffn1_norm, ffn1_w_gu=v_ffn1_w_gu, ffn1_w_down=v_ffn1_w_down, mix_norm=v_mix_norm,
             even_w_in=v_even_w_in, conv_a_w=v_conv_a_w, conv_a_b=v_conv_a_b, conv_a_ln_g=v_conv_a_ln_g,
             conv_a_ln_b=v_conv_a_ln_b, swa_sinks=v_swa_sinks, even_w_out=v_even_w_out, odd_w_in=v_odd_w_in,
             sc_conv_w=v_sc_conv_w, odd_w_out=v_odd_w_out, xa_norm=v_xa_norm, xa_mem_norm=v_xa_mem_norm, xa_wq=v_xa_wq,
             xa_wkv=v_xa_wkv, xa_wo=v_xa_wo, ffn2_norm=v_ffn2_norm, ffn2_w_gu=v_ffn2_w_gu, ffn2_w_down=v_ffn2_w_down,
             final_norm=v_final_norm)
    me = _flat_index(*_mesh_pos())

    conv_blob = jnp.concatenate([w["conv_a_w"].reshape(-1), w["sc_conv_w"].reshape(-1),
                                 jnp.zeros((8 * 1024 - 31 * 64 - 3 * 128,), F32)]).reshape(8, 1024)
    conv_all = all_gather(conv_blob, "gather_conv_weights").reshape(N_DEV, 8 * 1024)
    conv_a_full = jnp.transpose(conv_all[:, :31 * 64].reshape(N_DEV, 31, 64), (1, 0, 2)).reshape(31, 512)
    sc_full = jnp.transpose(conv_all[:, 31 * 64:31 * 64 + 3 * 128].reshape(N_DEV, 3, 128), (1, 0, 2)).reshape(3, 1024)

    def stage_blob(keys):
        shards = [w[n][l].T if SPLIT_AXIS[n] == 1 else w[n][l] for n, l in keys]
        return jnp.concatenate([s.astype(BF16) for s in shards], axis=0)

    def stage_rows(keys):
        return [(n, l, w[n].shape[2] if SPLIT_AXIS[n] == 1 else w[n].shape[1]) for n, l in keys]

    def unpack_weights(gathered, keys):
        out, off = {}, 0
        for n, l, rows in stage_rows(keys):
            out[(n, l)] = gathered[:, off:off + rows, :].reshape(N_DEV * rows, BLOB_COLS)
            off += rows
        return out

    def with_own(land, own):
        return lax.dynamic_update_slice(land, own[None], (me, 0, 0))

    gathered_a = all_gather(stage_blob(STAGE_KEYS["A"]), "gather_weights_a")
    later = ("B", "C", "D")
    blobs = [stage_blob(STAGE_KEYS[s]) for s in later]
    lands = [lax.empty((N_DEV,) + b.shape, BF16) for b in blobs]
    weight_handles, weight_token = exchange_start(blobs, lands, False, [gathered_a, conv_all], "gather_start")

    def get_weights(stage, after):
        if stage == "A":
            return unpack_weights(gathered_a, STAGE_KEYS["A"]), weight_token
        own, land = exchange_wait(weight_handles[later.index(stage)], False, after, "gather_wait_" + stage.lower())
        return unpack_weights(with_own(land, own), STAGE_KEYS[stage]), None

    grad_handles = {}

    def put_grads(stage, dws):
        packed = jnp.concatenate([dw.reshape(N_DEV, -1, BLOB_COLS) for dw in dws.values()], axis=1)
        land = lax.empty(packed.shape, BF16)
        (handle,), token = exchange_start([packed], [land], True, [], "scatter_start_" + stage.lower())
        grad_handles[stage] = (handle, tuple(dws))
        return token

    P = dict(ffn1_norm=ffn1_norm, mix_norm=mix_norm, xa_norm=xa_norm, xa_mem_norm=xa_mem_norm, ffn2_norm=ffn2_norm,
             final_norm=final_norm, conv_a_w=conv_a_full, conv_a_b=conv_a_b[0], conv_a_ln_g=conv_a_ln_g[0],
             conv_a_ln_b=conv_a_ln_b[0], swa_sinks=swa_sinks[0], sc_conv_w=sc_full)

    loss_part, grad_x, dP = local_step(x[0], mem[0], loss_target[0], P, get_weights, put_grads)

    def finish_grads(stage, after):
        handle, keys = grad_handles[stage]
        packed, land = exchange_wait(handle, True, after, "scatter_wait_" + stage.lower())
        own = lax.dynamic_slice(packed, (me, 0, 0), (1,) + packed.shape[1:])[0]
        rows_f32 = ordered_sum(with_own(land, own), "sum_grads_" + stage.lower())
        out, off = {}, 0
        for n, l, rows in stage_rows(keys):
            part = rows_f32[off:off + rows]
            out[(n, l)] = part.T if SPLIT_AXIS[n] == 1 else part
            off += rows
        return out

    layer_grads = {**finish_grads("D", grad_x), **finish_grads("BC", grad_x)}

    dP = dict(dP, loss=loss_part[0, 0])
    small = jnp.concatenate([
        _small_blob(dP)[:ROW_SC_CONV], dP["sc_conv_w"],
        jnp.concatenate([dP["conv_a_w"].reshape(-1), jnp.zeros((512,), F32)]).reshape(16, D_MODEL)], axis=0)
    small_sum = ordered_sum(all_gather(small, "gather_small_grads"), "sum_small_grads", tr=SMALL_BLOB_ROWS)
    loss = small_sum[ROW_LNB_SINKS_LOSS, LOSS_COL]
    grads = _small_unblob(small_sum, {n: w[n].shape for n in WEIGHT_NAMES})
    sc_g = small_sum[ROW_SC_CONV:ROW_SC_CONV + 3]
    grads["sc_conv_w"] = lax.dynamic_slice(sc_g, (0, me * 128), (3, 128)).reshape(w["sc_conv_w"].shape)
    cw_g = small_sum[ROW_CONV_W:].reshape(-1)[:31 * 512].reshape(31, 512)
    grads["conv_a_w"] = lax.dynamic_slice(cw_g, (0, me * 64), (31, 64)).reshape(w["conv_a_w"].shape)

    delta, new_m, new_v = {}, {}, {}

    def update(n):
        shp = w[n].shape
        two_d = (shp[0] * shp[1], shp[2])
        d_, m_, v_ = adamw(w[n].reshape(two_d), grads[n].reshape(two_d), m[n].reshape(two_d), v[n].reshape(two_d),
                           "adamw_" + n)
        delta[n], new_m[n], new_v[n] = d_.reshape(shp), m_.reshape(shp), v_.reshape(shp)

    first_stage = tuple(n for n, _ in STAGE_KEYS["A"])
    for n in SPLIT_AXIS:
        if n not in first_stage:
            grads[n] = jnp.stack([layer_grads[(n, l)] for l in range(w[n].shape[0])], axis=0)
            update(n)
    update("conv_a_w")
    update("sc_conv_w")
    layer_grads.update(finish_grads("A", delta["ffn2_w_gu"]))
    for n in first_stage:
        grads[n] = jnp.stack([layer_grads[(n, l)] for l in range(w[n].shape[0])], axis=0)
        update(n)
    d_, m_, v_ = adamw(_small_blob(w), small_sum[:SMALL_ADAM_ROWS], _small_blob(m), _small_blob(v), "adamw_small",
                       tr=SMALL_ADAM_ROWS)
    shapes = {n: w[n].shape for n in WEIGHT_NAMES}
    delta.update(_small_unblob(d_, shapes))
    new_m.update(_small_unblob(m_, shapes))
    new_v.update(_small_unblob(v_, shapes))

    return (loss, grad_x[None], *[grads[n] for n in WEIGHT_NAMES], *[delta[n] for n in WEIGHT_NAMES],
            *[new_m[n] for n in WEIGHT_NAMES], *[new_v[n] for n in WEIGHT_NAMES])
```

```python
import functools

import jax
import jax.numpy as jnp
from jax import lax
from jax.experimental import pallas as pl
from jax.experimental.pallas import tpu as pltpu

F32 = jnp.float32
BF16 = jnp.bfloat16

D_MODEL = 1024
D_FF = 2816
CONV_A_CH = 512
CONV_A_WIDTH = 31
SWA_HEADS = 8
SWA_KV_HEADS = 2
SWA_GROUP = SWA_HEADS // SWA_KV_HEADS
HEAD_DIM = 64
WINDOW = 128
SC_CH = 1024
XA_HEADS = 4
XA_HEAD_DIM = D_MODEL // XA_HEADS
RMS_EPS = 1e-6
LN_EPS = 1e-5
ADAM_LR = 0.001
ADAM_B1 = 0.9
ADAM_B2 = 0.999
ADAM_EPS = 1e-08
ADAM_WD = 0.01
ADAM_STEP = 10
N_DEV = 8

V7X_VMEM_BYTES = 64 * 1024 * 1024
VMEM_LIMIT = V7X_VMEM_BYTES - 8 * 1024 * 1024
CONV_HALO = 32
SC_HALO = 8
NEG_BIG = -1e30

SPLIT_AXIS = dict(ffn1_w_gu=1, ffn1_w_down=0, even_w_in=1, even_w_out=0, odd_w_in=1, odd_w_out=0, xa_wq=0, xa_wkv=1, xa_wo=0,
                  ffn2_w_gu=1, ffn2_w_down=0)
STAGE_KEYS = dict(
    A=(("ffn1_w_gu", 0), ("ffn1_w_down", 0)),
    B=(("even_w_in", 0), ("even_w_out", 0), ("xa_wq", 0), ("xa_wkv", 0), ("xa_wo", 0)),
    C=(("ffn2_w_gu", 0), ("ffn2_w_down", 0)),
    D=(("ffn1_w_gu", 1), ("ffn1_w_down", 1), ("odd_w_in", 0), ("odd_w_out", 0), ("xa_wq", 1), ("xa_wkv", 1), ("xa_wo", 1),
       ("ffn2_w_gu", 1), ("ffn2_w_down", 1)))


def _params(n_axes):
    return pltpu.CompilerParams(dimension_semantics=("arbitrary",) * n_axes, vmem_limit_bytes=VMEM_LIMIT)


def _tile(n, pref):
    t = min(n, pref)
    assert n % t == 0, (n, pref)
    return t


def _dot(a, b):
    return jnp.dot(a, b, preferred_element_type=F32)


def _dot_nt(a, b):
    return lax.dot_general(a, b, (((1,), (1,)), ((), ())), preferred_element_type=F32)


def _dot_tn(a, b):
    return lax.dot_general(a, b, (((0,), (0,)), ((), ())), preferred_element_type=F32)


def _sigmoid(x):
    return 0.5 * jnp.tanh(0.5 * x) + 0.5


ANY_SPEC = pl.BlockSpec(memory_space=pl.ANY)


def _with_dep(body, n_in, dep):
    if dep is None:
        return body, [], []
    return (lambda *refs: body(*refs[:n_in], *refs[n_in + 1:])), [ANY_SPEC], [dep]


def rmsnorm(h, g, name, tm=1024, dep=None):
    T, K = h.shape
    tm = _tile(T, tm)

    def kern(h_ref, g_ref, u_ref):
        x = h_ref[...]
        r = lax.rsqrt(jnp.mean(x * x, axis=-1, keepdims=True) + RMS_EPS)
        u_ref[...] = ((x * r) * g_ref[...]).astype(BF16)

    body, dep_spec, dep_arg = _with_dep(kern, 2, dep)
    return pl.pallas_call(
        body, name=name, grid=(T // tm,),
        in_specs=[pl.BlockSpec((tm, K), lambda i: (i, 0)), pl.BlockSpec((1, K), lambda i: (0, 0))] + dep_spec,
        out_specs=pl.BlockSpec((tm, K), lambda i: (i, 0)),
        out_shape=jax.ShapeDtypeStruct((T, K), BF16),
        compiler_params=_params(1),
    )(h, g, *dep_arg)


def matmul(a, w, out_dtype, name, tn, tm=2048, transposed=True, n_tiles=None, first_tile=0):
    T, K = a.shape
    N = w.shape[0] if transposed else w.shape[1]
    n_tiles = N // tn if n_tiles is None else n_tiles
    tm = _tile(T, tm)
    mm = _dot_nt if transposed else _dot

    def body(a_ref, w_ref, z_ref):
        z_ref[...] = mm(a_ref[...], w_ref[...]).astype(z_ref.dtype)

    w_spec = (pl.BlockSpec((tn, K), lambda i, j: (first_tile + j, 0)) if transposed
              else pl.BlockSpec((K, tn), lambda i, j: (0, first_tile + j)))
    return pl.pallas_call(
        body, name=name, grid=(T // tm, n_tiles),
        in_specs=[pl.BlockSpec((tm, K), lambda i, j: (i, 0)), w_spec],
        out_specs=pl.BlockSpec((tm, tn), lambda i, j: (i, j)),
        out_shape=jax.ShapeDtypeStruct((T, n_tiles * tn), out_dtype),
        compiler_params=_params(2),
    )(a, w)


def norm_matmul(h, g, w, out_dtype, name, tn, dep=None, transposed=True):
    u = rmsnorm(h, g, name + "_norm", dep=dep)
    return matmul(u, w, out_dtype, name, tn, transposed=transposed), u


def matmul_residual(a, w, res, name, tm=1024):
    T, K = a.shape
    N = w.shape[1]
    tm = _tile(T, tm)

    def body(a_ref, w_ref, r_ref, o_ref):
        o_ref[...] = r_ref[...] + _dot(a_ref[...], w_ref[...])

    return pl.pallas_call(
        body, name=name, grid=(T // tm,),
        in_specs=[pl.BlockSpec((tm, K), lambda i: (i, 0)),
                  pl.BlockSpec((K, N), lambda i: (0, 0)),
                  pl.BlockSpec((tm, N), lambda i: (i, 0))],
        out_specs=pl.BlockSpec((tm, N), lambda i: (i, 0)),
        out_shape=jax.ShapeDtypeStruct((T, N), F32),
        compiler_params=_params(1),
    )(a, w, res)


def matmul_nt(dy, w, out_dtype, name, tm=1024):
    T, N = dy.shape
    K = w.shape[0]
    tm = _tile(T, tm)

    def body(dy_ref, w_ref, o_ref):
        o_ref[...] = _dot_nt(dy_ref[...].astype(BF16), w_ref[...]).astype(o_ref.dtype)

    return pl.pallas_call(
        body, name=name, grid=(T // tm,),
        in_specs=[pl.BlockSpec((tm, N), lambda i: (i, 0)),
                  pl.BlockSpec((K, N), lambda i: (0, 0))],
        out_specs=pl.BlockSpec((tm, K), lambda i: (i, 0)),
        out_shape=jax.ShapeDtypeStruct((T, K), out_dtype),
        compiler_params=_params(1),
    )(dy, w)


def matmul_norm_bwd(dz, w, h, g, dh_in, name, tm=512, transposed=True, dep=None):
    T, N = dz.shape
    K = h.shape[1]
    tm = _tile(T, tm)

    def kern(dz_ref, w_ref, h_ref, g_ref, dhin_ref, dh_ref, dg_ref):
        @pl.when(pl.program_id(0) == 0)
        def _():
            dg_ref[...] = jnp.zeros_like(dg_ref)

        mm = _dot if transposed else _dot_nt
        du = mm(dz_ref[...], w_ref[...])
        x = h_ref[...]
        r = lax.rsqrt(jnp.mean(x * x, axis=-1, keepdims=True) + RMS_EPS)
        xh = x * r
        dg_ref[...] += jnp.sum(du * xh, axis=0, keepdims=True)
        dxh = du * g_ref[...]
        dh_ref[...] = dhin_ref[...] + r * (dxh - xh * jnp.mean(dxh * xh, axis=-1, keepdims=True))

    body, dep_spec, dep_arg = _with_dep(kern, 5, dep)
    return pl.pallas_call(
        body, name=name, grid=(T // tm,),
        in_specs=[pl.BlockSpec((tm, N), lambda i: (i, 0)),
                  pl.BlockSpec(w.shape, lambda i: (0, 0)),
                  pl.BlockSpec((tm, K), lambda i: (i, 0)),
                  pl.BlockSpec((1, K), lambda i: (0, 0)),
                  pl.BlockSpec((tm, K), lambda i: (i, 0))] + dep_spec,
        out_specs=[pl.BlockSpec((tm, K), lambda i: (i, 0)),
                   pl.BlockSpec((1, K), lambda i: (0, 0))],
        out_shape=[jax.ShapeDtypeStruct((T, K), F32), jax.ShapeDtypeStruct((1, K), F32)],
        compiler_params=_params(1),
    )(dz, w, h, g, dh_in, *dep_arg)


def matmul_tn(x, dy, name, scale=1.0, tk=None, tn=None, tt=1024):
    T, K = x.shape
    N = dy.shape[1]
    tk = K if tk is None else tk
    tn = N if tn is None else tn
    tt = _tile(T, tt)
    nt = T // tt

    def body(x_ref, dy_ref, o_ref, acc_ref):
        t = pl.program_id(2)

        @pl.when(t == 0)
        def _():
            acc_ref[...] = jnp.zeros_like(acc_ref)

        acc_ref[...] += _dot_tn(x_ref[...].astype(BF16), dy_ref[...].astype(BF16))

        @pl.when(t == nt - 1)
        def _():
            o_ref[...] = (acc_ref[...] * scale).astype(o_ref.dtype)

    return pl.pallas_call(
        body, name=name, grid=(K // tk, N // tn, nt),
        in_specs=[pl.BlockSpec((tt, tk), lambda a, b, t: (t, a)),
                  pl.BlockSpec((tt, tn), lambda a, b, t: (t, b))],
        out_specs=pl.BlockSpec((tk, tn), lambda a, b, t: (a, b)),
        out_shape=jax.ShapeDtypeStruct((K, N), BF16),
        scratch_shapes=[pltpu.VMEM((tk, tn), F32)],
        compiler_params=_params(3),
    )(x, dy)


def ffn_down(gu, wd, res, name, tm=512):
    T = gu.shape[0]
    F = gu.shape[1] // 2
    N = wd.shape[1]
    tm = _tile(T, tm)

    def body(g_ref, up_ref, w_ref, r_ref, o_ref, a_ref):
        g = g_ref[...].astype(F32)
        a_ref[...] = ((g * _sigmoid(g)) * up_ref[...].astype(F32)).astype(BF16)
        o_ref[...] = r_ref[...] + 0.5 * _dot(a_ref[...], w_ref[...])

    return pl.pallas_call(
        body, name=name, grid=(T // tm,),
        in_specs=[pl.BlockSpec((tm, F), lambda i: (i, 0)),
                  pl.BlockSpec((tm, F), lambda i: (i, 1)),
                  pl.BlockSpec((F, N), lambda i: (0, 0)),
                  pl.BlockSpec((tm, N), lambda i: (i, 0))],
        out_specs=[pl.BlockSpec((tm, N), lambda i: (i, 0)),
                   pl.BlockSpec((tm, F), lambda i: (i, 0))],
        out_shape=[jax.ShapeDtypeStruct((T, N), F32), jax.ShapeDtypeStruct((T, F), BF16)],
        compiler_params=_params(1),
    )(gu, gu, wd, res)


def ffn_down_bwd(dy, wd, gu, name, tm=512, dep=None):
    T, N = dy.shape
    F = wd.shape[0]
    tm = _tile(T, tm)

    def kern(dy_ref, w_ref, g_ref, up_ref, o_ref):
        da = 0.5 * _dot_nt(dy_ref[...].astype(BF16), w_ref[...])
        g = g_ref[...].astype(F32)
        up = up_ref[...].astype(F32)
        s = _sigmoid(g)
        o_ref[:, :F] = (da * up * (s * (1.0 + g * (1.0 - s)))).astype(BF16)
        o_ref[:, F:] = (da * (g * s)).astype(BF16)

    body, dep_spec, dep_arg = _with_dep(kern, 4, dep)
    return pl.pallas_call(
        body, name=name, grid=(T // tm,),
        in_specs=[pl.BlockSpec((tm, N), lambda i: (i, 0)),
                  pl.BlockSpec((F, N), lambda i: (0, 0)),
                  pl.BlockSpec((tm, F), lambda i: (i, 0)),
                  pl.BlockSpec((tm, F), lambda i: (i, 1))] + dep_spec,
        out_specs=pl.BlockSpec((tm, 2 * F), lambda i: (i, 0)),
        out_shape=jax.ShapeDtypeStruct((T, 2 * F), BF16),
        compiler_params=_params(1),
    )(dy, wd, gu, gu, *dep_arg)


def ffn_forward(h, g, w_gu, w_down, name, dep=None):
    gu, u = norm_matmul(h, g, w_gu, BF16, name + "_gu", D_FF // 2, dep=dep)
    h_out, a = ffn_down(gu, w_down, h, name + "_down")
    return h_out, (h, u, gu, a)


def ffn_backward(dy, saved, g, w_gu, w_down, name, dep=None, emit=None):
    h, u, gu, a = saved
    dgu = ffn_down_bwd(dy, w_down, gu, name + "_ddown", dep=dep)
    d_w_down = matmul_tn(a, dy, name + "_dwd", scale=0.5, tk=D_FF // 2)
    d_w_gu = matmul_tn(dgu, u, name + "_dwgu", tk=D_FF)
    dh, dg = matmul_norm_bwd(dgu, w_gu, h, g, dy, name + "_dx", dep=emit(d_w_gu, d_w_down))
    return dh, dg


CONV_ROW_CHUNK = 32
CONV_LANES = 128
CONV_X_OFFSETS = tuple(CONV_HALO - (CONV_A_WIDTH - 1) + k for k in range(CONV_A_WIDTH))
CONV_D_OFFSETS = tuple(CONV_A_WIDTH - 1 - k for k in range(CONV_A_WIDTH))


def _build_phases(ref, phase_ref, n_rows):
    for r in range(1, 8):
        phase_ref[r - 1] = ref[pl.ds(r, n_rows - 8), :]


def _tap_values(ref, phase_ref, offsets, n, base, lanes):
    out = {}
    for r in range(8):
        qs = sorted(o // 8 for o in offsets if o % 8 == r)
        if qs:
            lo, hi = qs[0], qs[-1]
            rows = pl.ds(base + 8 * lo, n + 8 * (hi - lo))
            span = ref[rows, lanes] if r == 0 else phase_ref[r - 1, rows, lanes]
            for q in qs:
                out[8 * q + r] = span[8 * (q - lo):8 * (q - lo) + n]
    return out


def conformer_conv_fwd(z, cw, cb, lg, lb, name, tm=512):
    T = z.shape[0]
    C = CONV_A_CH
    tm = _tile(T, tm)
    hb = tm // CONV_HALO
    CH = CONV_ROW_CHUNK
    KW = CONV_A_WIDTH

    def body(v_ref, gt_ref, pv_ref, pg_ref, cw_ref, cb_ref, lg_ref, lb_ref, o_ref, conv_ref, xs_ref, xph_ref):
        i = pl.program_id(0)
        prev = pv_ref[...] * _sigmoid(pg_ref[...])
        xs_ref[0:CONV_HALO, :] = jnp.where(i > 0, prev, 0.0)
        xs_ref[CONV_HALO:, :] = v_ref[...] * _sigmoid(gt_ref[...])
        _build_phases(xs_ref, xph_ref, tm + CONV_HALO)

        def chunk(c, carry):
            off = pl.multiple_of(c * CH, CH)
            for l0 in range(0, C, CONV_LANES):
                lanes = slice(l0, l0 + CONV_LANES)
                taps = _tap_values(xs_ref, xph_ref, CONV_X_OFFSETS, CH, off, lanes)
                acc = jnp.zeros((CH, CONV_LANES), F32) + cb_ref[:, lanes]
                for k in range(KW):
                    acc = acc + cw_ref[k:k + 1, lanes] * taps[CONV_X_OFFSETS[k]]
                conv_ref[pl.ds(off, CH), lanes] = acc
            return carry

        lax.fori_loop(0, tm // CH, chunk, 0)
        acc = conv_ref[...]
        mu = jnp.mean(acc, axis=-1, keepdims=True)
        xc = acc - mu
        var = jnp.mean(xc * xc, axis=-1, keepdims=True)
        y = (xc * lax.rsqrt(var + LN_EPS)) * lg_ref[...] + lb_ref[...]
        o_ref[...] = (y * _sigmoid(y)).astype(BF16)

    return pl.pallas_call(
        body, name=name, grid=(T // tm,),
        in_specs=[pl.BlockSpec((tm, C), lambda i: (i, 0)),
                  pl.BlockSpec((tm, C), lambda i: (i, 1)),
                  pl.BlockSpec((CONV_HALO, C), lambda i: (jnp.maximum(i * hb - 1, 0), 0)),
                  pl.BlockSpec((CONV_HALO, C), lambda i: (jnp.maximum(i * hb - 1, 0), 1)),
                  pl.BlockSpec((32, C), lambda i: (0, 0)),
                  pl.BlockSpec((1, C), lambda i: (0, 0)),
                  pl.BlockSpec((1, C), lambda i: (0, 0)),
                  pl.BlockSpec((1, C), lambda i: (0, 0))],
        out_specs=[pl.BlockSpec((tm, C), lambda i: (i, 0)), pl.BlockSpec((tm, C), lambda i: (i, 0))],
        out_shape=[jax.ShapeDtypeStruct((T, C), BF16), jax.ShapeDtypeStruct((T, C), F32)],
        scratch_shapes=[pltpu.VMEM((tm + CONV_HALO, C), F32), pltpu.VMEM((7, tm + CONV_HALO - 8, C), F32)],
        compiler_params=_params(1),
    )(z, z, z, z, cw, cb, lg, lb)


def conformer_conv_bwd(z, conv_out, dm, cw, lg, lb, name, tm=512):
    T = z.shape[0]
    C = CONV_A_CH
    tm = _tile(T, tm)
    hb = tm // CONV_HALO
    n_tiles = T // tm
    last_halo = T // CONV_HALO - 1
    R = tm + CONV_HALO
    KW = CONV_A_WIDTH
    CH = CONV_ROW_CHUNK

    def body(v_ref, gt_ref, pv_ref, pg_ref, cv_ref, ncv_ref, do_ref, ndo_ref, cw_ref, lg_ref, lb_ref,
             dz_ref, dcw_ref, dcb_ref, dlg_ref, dlb_ref, xs_ref, xph_ref, ds_ref, dph_ref):
        i = pl.program_id(0)

        @pl.when(i == 0)
        def _():
            dcw_ref[...] = jnp.zeros_like(dcw_ref)
            dcb_ref[...] = jnp.zeros_like(dcb_ref)
            dlg_ref[...] = jnp.zeros_like(dlg_ref)
            dlb_ref[...] = jnp.zeros_like(dlb_ref)

        prev = pv_ref[...] * _sigmoid(pg_ref[...])
        xs_ref[0:CONV_HALO, :] = jnp.where(i > 0, prev, 0.0)
        xs_ref[CONV_HALO:, :] = v_ref[...] * _sigmoid(gt_ref[...])
        _build_phases(xs_ref, xph_ref, tm + CONV_HALO)

        acc = jnp.concatenate([cv_ref[...], ncv_ref[...]], axis=0)
        mu = jnp.mean(acc, axis=-1, keepdims=True)
        xc = acc - mu
        rstd = lax.rsqrt(jnp.mean(xc * xc, axis=-1, keepdims=True) + LN_EPS)
        xh = xc * rstd
        y = xh * lg_ref[...] + lb_ref[...]
        s = _sigmoid(y)
        dout = jnp.concatenate([do_ref[...], jnp.where(i < n_tiles - 1, ndo_ref[...], 0.0)], axis=0)
        dy = dout * (s * (1.0 + y * (1.0 - s)))
        dxh = dy * lg_ref[...]
        dconv = rstd * (dxh - jnp.mean(dxh, axis=-1, keepdims=True) - xh * jnp.mean(dxh * xh, axis=-1, keepdims=True))
        ds_ref[...] = dconv
        dlg_ref[...] += jnp.sum(dy[:tm] * xh[:tm], axis=0, keepdims=True)
        dlb_ref[...] += jnp.sum(dy[:tm], axis=0, keepdims=True)
        dcb_ref[...] += jnp.sum(dconv[:tm], axis=0, keepdims=True)
        _build_phases(ds_ref, dph_ref, R)

        for l0 in range(0, C, CONV_LANES):
            lanes = slice(l0, l0 + CONV_LANES)

            def taps_bwd(c, wacc, l0=l0, lanes=lanes):
                off = pl.multiple_of(c * CH, CH)
                x_taps = _tap_values(xs_ref, xph_ref, CONV_X_OFFSETS, CH, off, lanes)
                d_taps = _tap_values(ds_ref, dph_ref, CONV_D_OFFSETS, CH, off, lanes)
                dc = ds_ref[pl.ds(off, CH), lanes]
                dglu = jnp.zeros((CH, CONV_LANES), F32)
                new = []
                for k in range(KW):
                    dglu = dglu + cw_ref[k:k + 1, lanes] * d_taps[CONV_D_OFFSETS[k]]
                    prod = dc * x_taps[CONV_X_OFFSETS[k]]
                    new.append(wacc[k] + ((prod[0:8] + prod[8:16]) + (prod[16:24] + prod[24:32])))
                val = v_ref[pl.ds(off, CH), lanes]
                sg = _sigmoid(gt_ref[pl.ds(off, CH), lanes])
                dz_ref[pl.ds(off, CH), lanes] = (dglu * sg).astype(BF16)
                dz_ref[pl.ds(off, CH), C + l0:C + l0 + CONV_LANES] = (dglu * val * sg * (1.0 - sg)).astype(BF16)
                return tuple(new)

            wacc = lax.fori_loop(0, tm // CH, taps_bwd, tuple(jnp.zeros((8, CONV_LANES), F32) for _ in range(KW)))
            for k in range(KW):
                dcw_ref[k:k + 1, lanes] += jnp.sum(wacc[k], axis=0, keepdims=True)

    prev_map = lambda i: jnp.maximum(i * hb - 1, 0)
    next_map = lambda i: jnp.minimum((i + 1) * hb, last_halo)
    return pl.pallas_call(
        body, name=name, grid=(n_tiles,),
        in_specs=[pl.BlockSpec((tm, C), lambda i: (i, 0)),
                  pl.BlockSpec((tm, C), lambda i: (i, 1)),
                  pl.BlockSpec((CONV_HALO, C), lambda i: (prev_map(i), 0)),
                  pl.BlockSpec((CONV_HALO, C), lambda i: (prev_map(i), 1)),
                  pl.BlockSpec((tm, C), lambda i: (i, 0)),
                  pl.BlockSpec((CONV_HALO, C), lambda i: (next_map(i), 0)),
                  pl.BlockSpec((tm, C), lambda i: (i, 0)),
                  pl.BlockSpec((CONV_HALO, C), lambda i: (next_map(i), 0)),
                  pl.BlockSpec((32, C), lambda i: (0, 0)),
                  pl.BlockSpec((1, C), lambda i: (0, 0)),
                  pl.BlockSpec((1, C), lambda i: (0, 0))],
        out_specs=[pl.BlockSpec((tm, 2 * C), lambda i: (i, 0)),
                   pl.BlockSpec((32, C), lambda i: (0, 0)),
                   pl.BlockSpec((1, C), lambda i: (0, 0)),
                   pl.BlockSpec((1, C), lambda i: (0, 0)),
                   pl.BlockSpec((1, C), lambda i: (0, 0))],
        out_shape=[jax.ShapeDtypeStruct((T, 2 * C), BF16),
                   jax.ShapeDtypeStruct((32, C), F32),
                   jax.ShapeDtypeStruct((1, C), F32),
                   jax.ShapeDtypeStruct((1, C), F32),
                   jax.ShapeDtypeStruct((1, C), F32)],
        scratch_shapes=[pltpu.VMEM((tm + CONV_HALO, C), F32), pltpu.VMEM((7, tm + CONV_HALO - 8, C), F32),
                        pltpu.VMEM((R, C), F32), pltpu.VMEM((7, R - 8, C), F32)],
        compiler_params=_params(1),
    )(z, z, z, z, conv_out, conv_out, dm, dm, cw, lg, lb)


def _swa_scores(q_h, kk_h, slope, bias_dist, valid, sink):
    s = _dot_nt(q_h, kk_h) * (HEAD_DIM ** -0.5) - slope * bias_dist
    s = jnp.where(valid, s, NEG_BIG)
    m = jnp.maximum(jnp.max(s, axis=-1, keepdims=True), sink)
    p = jnp.exp(s - m)
    e_sink = jnp.exp(sink - m)
    inv = 1.0 / (jnp.sum(p, axis=-1, keepdims=True) + e_sink)
    return p * inv, e_sink * inv


def _swa_mask(r0):
    qi = lax.broadcasted_iota(jnp.int32, (WINDOW, 2 * WINDOW), 0)
    kj = lax.broadcasted_iota(jnp.int32, (WINDOW, 2 * WINDOW), 1)
    dist = qi + WINDOW - kj
    valid = (dist >= 0) & (dist < WINDOW) & (r0 - WINDOW + kj >= 0)
    return dist.astype(F32), valid


def swa_fwd(z, kpad, vpad, sinks, name, tq=512):
    T = z.shape[0]
    tq = _tile(T, tq)
    HQ = SWA_HEADS * HEAD_DIM

    def body(sink_ref, q_ref, k_ref, v_ref, o_ref):
        i = pl.program_id(0)
        for sub in range(tq // WINDOW):
            r0 = pl.multiple_of(i * tq + sub * WINDOW, WINDOW)
            kk = k_ref[pl.ds(r0, 2 * WINDOW), :]
            vv = v_ref[pl.ds(r0, 2 * WINDOW), :]
            qb = q_ref[sub * WINDOW:(sub + 1) * WINDOW, :].astype(BF16)
            dist, valid = _swa_mask(r0)
            outs = []
            for h in range(SWA_HEADS):
                kh = h // SWA_GROUP
                ks = slice(kh * HEAD_DIM, (kh + 1) * HEAD_DIM)
                pn, _ = _swa_scores(qb[:, h * HEAD_DIM:(h + 1) * HEAD_DIM], kk[:, ks], 2.0 ** (-(h + 1)), dist, valid,
                                    sink_ref[h])
                outs.append(_dot(pn.astype(BF16), vv[:, ks]))
            o_ref[sub * WINDOW:(sub + 1) * WINDOW, :] = jnp.concatenate(outs, axis=-1).astype(BF16)

    return pl.pallas_call(
        body, name=name, grid=(T // tq,),
        in_specs=[pl.BlockSpec(memory_space=pltpu.SMEM),
                  pl.BlockSpec((tq, HQ), lambda i: (i, 2)),
                  pl.BlockSpec((T + WINDOW, 2 * HEAD_DIM), lambda i: (0, 0)),
                  pl.BlockSpec((T + WINDOW, 2 * HEAD_DIM), lambda i: (0, 0))],
        out_specs=pl.BlockSpec((tq, HQ), lambda i: (i, 0)),
        out_shape=jax.ShapeDtypeStruct((T, HQ), BF16),
        compiler_params=_params(1),
    )(sinks, z, kpad, vpad)


def swa_bwd(z, kpad, vpad, sinks, dm, name, tq=512):
    T = z.shape[0]
    tq = _tile(T, tq)
    HQ = SWA_HEADS * HEAD_DIM
    scale = HEAD_DIM ** -0.5

    def body(sink_ref, q_ref, k_ref, v_ref, do_ref, dq_ref, dk_ref, dv_ref, dsink_ref):
        i = pl.program_id(0)

        @pl.when(i == 0)
        def _():
            dk_ref[...] = jnp.zeros_like(dk_ref)
            dv_ref[...] = jnp.zeros_like(dv_ref)
            dsink_ref[...] = jnp.zeros_like(dsink_ref)

        for sub in range(tq // WINDOW):
            r0 = pl.multiple_of(i * tq + sub * WINDOW, WINDOW)
            kk = k_ref[pl.ds(r0, 2 * WINDOW), :]
            vv = v_ref[pl.ds(r0, 2 * WINDOW), :]
            rows = slice(sub * WINDOW, (sub + 1) * WINDOW)
            qb = q_ref[rows, :].astype(BF16)
            dob = do_ref[rows, :].astype(BF16)
            dist, valid = _swa_mask(r0)
            dqs, dks, dvs = [], [], []
            for kh in range(SWA_KV_HEADS):
                ks = slice(kh * HEAD_DIM, (kh + 1) * HEAD_DIM)
                dk_acc = jnp.zeros((2 * WINDOW, HEAD_DIM), F32)
                dv_acc = jnp.zeros((2 * WINDOW, HEAD_DIM), F32)
                for g in range(SWA_GROUP):
                    h = kh * SWA_GROUP + g
                    hs = slice(h * HEAD_DIM, (h + 1) * HEAD_DIM)
                    pn, p_sink = _swa_scores(qb[:, hs], kk[:, ks], 2.0 ** (-(h + 1)), dist, valid, sink_ref[h])
                    dp = _dot_nt(dob[:, hs], vv[:, ks])
                    delta = jnp.sum(pn * dp, axis=-1, keepdims=True)
                    ds = (pn * (dp - delta)).astype(BF16)
                    dqs.append(_dot(ds, kk[:, ks]) * scale)
                    dk_acc = dk_acc + _dot_tn(ds, qb[:, hs]) * scale
                    dv_acc = dv_acc + _dot_tn(pn.astype(BF16), dob[:, hs])
                    dsink_ref[h:h + 1, :] += jnp.zeros((1, 128), F32) - jnp.sum(p_sink * delta)
                dks.append(dk_acc)
                dvs.append(dv_acc)
            dq_ref[rows, :] = jnp.concatenate(dqs, axis=-1).astype(BF16)
            dk_ref[pl.ds(r0, 2 * WINDOW), :] += jnp.concatenate(dks, axis=-1)
            dv_ref[pl.ds(r0, 2 * WINDOW), :] += jnp.concatenate(dvs, axis=-1)

    kv_spec = pl.BlockSpec((T + WINDOW, 2 * HEAD_DIM), lambda i: (0, 0))
    return pl.pallas_call(
        body, name=name, grid=(T // tq,),
        in_specs=[pl.BlockSpec(memory_space=pltpu.SMEM),
                  pl.BlockSpec((tq, HQ), lambda i: (i, 2)),
                  kv_spec, kv_spec,
                  pl.BlockSpec((tq, HQ), lambda i: (i, 1))],
        out_specs=[pl.BlockSpec((tq, HQ), lambda i: (i, 0)),
                   kv_spec, kv_spec,
                   pl.BlockSpec((SWA_HEADS, 128), lambda i: (0, 0))],
        out_shape=[jax.ShapeDtypeStruct((T, HQ), BF16),
                   jax.ShapeDtypeStruct((T + WINDOW, 2 * HEAD_DIM), F32),
                   jax.ShapeDtypeStruct((T + WINDOW, 2 * HEAD_DIM), F32),
                   jax.ShapeDtypeStruct((SWA_HEADS, 128), F32)],
        compiler_params=_params(1),
    )(sinks, z, kpad, vpad, dm)


def short_conv_fwd(z, w, name, tm=512):
    T = z.shape[0]
    C = SC_CH
    tm = _tile(T, tm)
    hb = tm // SC_HALO

    def body(b_ref, c_ref, v_ref, pc_ref, pv_ref, w_ref, o_ref, xs_ref):
        i = pl.program_id(0)
        xs_ref[0:SC_HALO, :] = jnp.where(i > 0, pc_ref[...] * pv_ref[...], 0.0)
        xs_ref[SC_HALO:, :] = c_ref[...] * v_ref[...]
        conv = jnp.zeros((tm, C), F32)
        for k in range(3):
            conv = conv + w_ref[k:k + 1, :] * xs_ref[pl.ds(SC_HALO - 2 + k, tm), :]
        o_ref[...] = (b_ref[...] * conv).astype(BF16)

    prev_map = lambda i: jnp.maximum(i * hb - 1, 0)
    return pl.pallas_call(
        body, name=name, grid=(T // tm,),
        in_specs=[pl.BlockSpec((tm, C), lambda i: (i, 0)),
                  pl.BlockSpec((tm, C), lambda i: (i, 1)),
                  pl.BlockSpec((tm, C), lambda i: (i, 2)),
                  pl.BlockSpec((SC_HALO, C), lambda i: (prev_map(i), 1)),
                  pl.BlockSpec((SC_HALO, C), lambda i: (prev_map(i), 2)),
                  pl.BlockSpec((8, C), lambda i: (0, 0))],
        out_specs=pl.BlockSpec((tm, C), lambda i: (i, 0)),
        out_shape=jax.ShapeDtypeStruct((T, C), BF16),
        scratch_shapes=[pltpu.VMEM((tm + SC_HALO, C), F32)],
        compiler_params=_params(1),
    )(z, z, z, z, z, w)


def short_conv_bwd(z, dm, w, name, tm=512):
    T = z.shape[0]
    C = SC_CH
    tm = _tile(T, tm)
    hb = tm // SC_HALO
    n_tiles = T // tm
    last_halo = T // SC_HALO - 1
    R = tm + SC_HALO

    def body(b_ref, c_ref, v_ref, pc_ref, pv_ref, nb_ref, do_ref, ndo_ref, w_ref, dz_ref, dw_ref, xs_ref, ds_ref):
        i = pl.program_id(0)

        @pl.when(i == 0)
        def _():
            dw_ref[...] = jnp.zeros_like(dw_ref)

        c = c_ref[...]
        val = v_ref[...]
        dout = do_ref[...]
        xs_ref[0:SC_HALO, :] = jnp.where(i > 0, pc_ref[...] * pv_ref[...], 0.0)
        xs_ref[SC_HALO:, :] = c * val
        dconv = dout * b_ref[...]
        ds_ref[0:tm, :] = dconv
        ds_ref[tm:, :] = jnp.where(i < n_tiles - 1, ndo_ref[...] * nb_ref[...], 0.0)
        conv = jnp.zeros((tm, C), F32)
        dcv = jnp.zeros((tm, C), F32)
        for k in range(3):
            xk = xs_ref[pl.ds(SC_HALO - 2 + k, tm), :]
            conv = conv + w_ref[k:k + 1, :] * xk
            dw_ref[k:k + 1, :] += jnp.sum(dconv * xk, axis=0, keepdims=True)
            dcv = dcv + w_ref[k:k + 1, :] * ds_ref[pl.ds(2 - k, tm), :]
        dz_ref[:, 0:C] = (dout * conv).astype(BF16)
        dz_ref[:, C:2 * C] = (dcv * val).astype(BF16)
        dz_ref[:, 2 * C:] = (dcv * c).astype(BF16)

    prev_map = lambda i: jnp.maximum(i * hb - 1, 0)
    next_map = lambda i: jnp.minimum((i + 1) * hb, last_halo)
    return pl.pallas_call(
        body, name=name, grid=(n_tiles,),
        in_specs=[pl.BlockSpec((tm, C), lambda i: (i, 0)),
                  pl.BlockSpec((tm, C), lambda i: (i, 1)),
                  pl.BlockSpec((tm, C), lambda i: (i, 2)),
                  pl.BlockSpec((SC_HALO, C), lambda i: (prev_map(i), 1)),
                  pl.BlockSpec((SC_HALO, C), lambda i: (prev_map(i), 2)),
                  pl.BlockSpec((SC_HALO, C), lambda i: (next_map(i), 0)),
                  pl.BlockSpec((tm, C), lambda i: (i, 0)),
                  pl.BlockSpec((SC_HALO, C), lambda i: (next_map(i), 0)),
                  pl.BlockSpec((8, C), lambda i: (0, 0))],
        out_specs=[pl.BlockSpec((tm, 3 * C), lambda i: (i, 0)),
                   pl.BlockSpec((8, C), lambda i: (0, 0))],
        out_shape=[jax.ShapeDtypeStruct((T, 3 * C), BF16), jax.ShapeDtypeStruct((8, C), F32)],
        scratch_shapes=[pltpu.VMEM((tm + SC_HALO, C), F32), pltpu.VMEM((R, C), F32)],
        compiler_params=_params(1),
    )(z, z, z, z, z, z, dm, dm, w)


def _xa_probs(q_h, k_h):
    s = _dot_nt(q_h, k_h) * (XA_HEAD_DIM ** -0.5)
    p = jnp.exp(s - jnp.max(s, axis=-1, keepdims=True))
    return p * (1.0 / jnp.sum(p, axis=-1, keepdims=True))


def xattn_fwd(q, kv, name, tm=512):
    T = q.shape[0]
    M = kv.shape[0]
    tm = _tile(T, tm)

    def body(q_ref, k_ref, v_ref, o_ref):
        for h in range(XA_HEADS):
            hs = slice(h * XA_HEAD_DIM, (h + 1) * XA_HEAD_DIM)
            p = _xa_probs(q_ref[:, hs], k_ref[:, hs])
            o_ref[:, hs] = _dot(p.astype(BF16), v_ref[:, hs]).astype(BF16)

    return pl.pallas_call(
        body, name=name, grid=(T // tm,),
        in_specs=[pl.BlockSpec((tm, D_MODEL), lambda i: (i, 0)),
                  pl.BlockSpec((M, D_MODEL), lambda i: (0, 0)),
                  pl.BlockSpec((M, D_MODEL), lambda i: (0, 1))],
        out_specs=pl.BlockSpec((tm, D_MODEL), lambda i: (i, 0)),
        out_shape=jax.ShapeDtypeStruct((T, D_MODEL), BF16),
        compiler_params=_params(1),
    )(q, kv, kv)


def xattn_bwd(q, kv, do, name, tm=512):
    T = q.shape[0]
    M = kv.shape[0]
    tm = _tile(T, tm)
    scale = XA_HEAD_DIM ** -0.5

    def body(q_ref, k_ref, v_ref, do_ref, dq_ref, dkv_ref):
        @pl.when(pl.program_id(0) == 0)
        def _():
            dkv_ref[...] = jnp.zeros_like(dkv_ref)

        for h in range(XA_HEADS):
            hs = slice(h * XA_HEAD_DIM, (h + 1) * XA_HEAD_DIM)
            vs = slice(D_MODEL + h * XA_HEAD_DIM, D_MODEL + (h + 1) * XA_HEAD_DIM)
            q_h = q_ref[:, hs]
            do_h = do_ref[:, hs]
            p = _xa_probs(q_h, k_ref[:, hs])
            dp = _dot_nt(do_h, v_ref[:, hs])
            ds = (p * (dp - jnp.sum(p * dp, axis=-1, keepdims=True))).astype(BF16)
            dq_ref[:, hs] = (_dot(ds, k_ref[:, hs]) * scale).astype(BF16)
            dkv_ref[:, hs] += _dot_tn(ds, q_h) * scale
            dkv_ref[:, vs] += _dot_tn(p.astype(BF16), do_h)

    return pl.pallas_call(
        body, name=name, grid=(T // tm,),
        in_specs=[pl.BlockSpec((tm, D_MODEL), lambda i: (i, 0)),
                  pl.BlockSpec((M, D_MODEL), lambda i: (0, 0)),
                  pl.BlockSpec((M, D_MODEL), lambda i: (0, 1)),
                  pl.BlockSpec((tm, D_MODEL), lambda i: (i, 0))],
        out_specs=[pl.BlockSpec((tm, D_MODEL), lambda i: (i, 0)),
                   pl.BlockSpec((M, 2 * D_MODEL), lambda i: (0, 0))],
        out_shape=[jax.ShapeDtypeStruct((T, D_MODEL), BF16), jax.ShapeDtypeStruct((M, 2 * D_MODEL), F32)],
        compiler_params=_params(1),
    )(q, kv, kv, do)


def final_loss(h, g, target, name, tm=512):
    T, K = h.shape
    tm = _tile(T, tm)

    def body(h_ref, g_ref, t_ref, dh_ref, dg_ref, loss_ref):
        @pl.when(pl.program_id(0) == 0)
        def _():
            dg_ref[...] = jnp.zeros_like(dg_ref)
            loss_ref[...] = jnp.zeros_like(loss_ref)

        x = h_ref[...]
        r = lax.rsqrt(jnp.mean(x * x, axis=-1, keepdims=True) + RMS_EPS)
        xh = x * r
        e = xh * g_ref[...] - t_ref[...]
        loss_ref[...] += jnp.zeros((1, 128), F32) + 0.5 * jnp.sum(jnp.mean(e * e, axis=-1, keepdims=True))
        dy = e * (1.0 / K)
        dg_ref[...] += jnp.sum(dy * xh, axis=0, keepdims=True)
        dxh = dy * g_ref[...]
        dh_ref[...] = r * (dxh - xh * jnp.mean(dxh * xh, axis=-1, keepdims=True))

    return pl.pallas_call(
        body, name=name, grid=(T // tm,),
        in_specs=[pl.BlockSpec((tm, K), lambda i: (i, 0)),
                  pl.BlockSpec((1, K), lambda i: (0, 0)),
                  pl.BlockSpec((tm, K), lambda i: (i, 0))],
        out_specs=[pl.BlockSpec((tm, K), lambda i: (i, 0)),
                   pl.BlockSpec((1, K), lambda i: (0, 0)),
                   pl.BlockSpec((1, 128), lambda i: (0, 0))],
        out_shape=[jax.ShapeDtypeStruct((T, K), F32), jax.ShapeDtypeStruct((1, K), F32),
                   jax.ShapeDtypeStruct((1, 128), F32)],
        compiler_params=_params(1),
    )(h, g, target)


def _row(v):
    return v.reshape(1, -1)


def _pad_rows(a, rows):
    return jnp.pad(a, ((0, rows - a.shape[0]), (0, 0)))


def local_step(x, mem, target, P, get_weights, put_grads):
    cw = _pad_rows(P["conv_a_w"], 32)
    scw = _pad_rows(P["sc_conv_w"], 8)
    cb, lg, lb = _row(P["conv_a_b"]), _row(P["conv_a_ln_g"]), _row(P["conv_a_ln_b"])
    sinks = P["swa_sinks"]

    class _Layered:
        def __init__(self, store, name=None):
            self.store, self.name = store, name

        def __getitem__(self, key):
            if self.name is None:
                return self.store[(key, 0)] if key in ("even_w_in", "even_w_out", "odd_w_in", "odd_w_out") \
                    else _Layered(self.store, key)
            return self.store[(self.name, key)]

    store = {}
    W = _Layered(store)
    saved = []
    h = x
    for i in range(2):
        L = f"l{i}"
        new, dep = get_weights("A" if i == 0 else "D", h)
        store.update(new)
        h, s_ffn1 = ffn_forward(h, P["ffn1_norm"][i:i + 1], W["ffn1_w_gu"][i], W["ffn1_w_down"][i], L + "_ffn1", dep=dep)
        h1 = h
        if i == 0:
            new, _ = get_weights("B", h)
            store.update(new)
            u2 = rmsnorm(h1, P["mix_norm"][i:i + 1], L + "_mix_in_norm")
            z = matmul(u2, W["even_w_in"], F32, L + "_mix_in", 768, n_tiles=2)
            kv = matmul(u2, W["even_w_in"], BF16, L + "_mix_kv", 256, n_tiles=1, first_tile=6)
            a, conv_out = conformer_conv_fwd(z, cw, cb, lg, lb, L + "_conv")
            kpad = jnp.pad(kv[:, :2 * HEAD_DIM], ((WINDOW, 0), (0, 0)))
            vpad = jnp.pad(kv[:, 2 * HEAD_DIM:], ((WINDOW, 0), (0, 0)))
            o = swa_fwd(z, kpad, vpad, sinks, L + "_swa")
            m = jnp.concatenate([a, o], axis=-1)
            h = matmul_residual(m, W["even_w_out"], h1, L + "_mix_out")
            s_mix = (h1, u2, z, m, kpad, vpad, conv_out)
        else:
            z, u2 = norm_matmul(h1, P["mix_norm"][i:i + 1], W["odd_w_in"], F32, L + "_mix_in", 1024)
            m = short_conv_fwd(z, scw, L + "_sconv")
            h = matmul_residual(m, W["odd_w_out"], h1, L + "_mix_out")
            s_mix = (h1, u2, z, m)
        h2 = h
        kv, umem = norm_matmul(mem, P["xa_mem_norm"][i:i + 1], W["xa_wkv"][i], BF16, L + "_xa_kv", 2 * D_MODEL)
        q, u3 = norm_matmul(h2, P["xa_norm"][i:i + 1], W["xa_wq"][i], BF16, L + "_xa_q", D_MODEL, transposed=False)
        o = xattn_fwd(q, kv, L + "_xa")
        h = matmul_residual(o, W["xa_wo"][i], h2, L + "_xa_out")
        s_xa = (h2, u3, q, o, kv, umem)
        if i == 0:
            new, _ = get_weights("C", h)
            store.update(new)
        h, s_ffn2 = ffn_forward(h, P["ffn2_norm"][i:i + 1], W["ffn2_w_gu"][i], W["ffn2_w_down"][i], L + "_ffn2")
        saved.append((s_ffn1, s_mix, s_xa, s_ffn2))

    dh, d_final, loss = final_loss(h, _row(P["final_norm"]), target, "final_loss")

    names = ("ffn1_w_gu", "ffn1_w_down", "ffn2_w_gu", "ffn2_w_down", "xa_wq", "xa_wkv", "xa_wo", "even_w_in", "even_w_out",
             "odd_w_in", "odd_w_out")
    dW = {k: [None, None] for k in names}
    dP = {k: [None, None] for k in ("ffn1_norm", "mix_norm", "xa_norm", "xa_mem_norm", "ffn2_norm")}
    dP["final_norm"] = d_final.reshape(-1)
    for i in (1, 0):
        L = f"l{i}b"
        s_ffn1, s_mix, s_xa, s_ffn2 = saved[i]

        def keep_ffn2(d_w_gu, d_w_down, i=i):
            dW["ffn2_w_gu"][i], dW["ffn2_w_down"][i] = d_w_gu, d_w_down

        def send_ffn1(d_w_gu, d_w_down, i=i):
            dW["ffn1_w_gu"][i], dW["ffn1_w_down"][i] = d_w_gu, d_w_down
            stage = "D" if i == 1 else "A"
            return put_grads(stage, {k: dW[k[0]][k[1]] for k in STAGE_KEYS[stage]})

        dh, dP["ffn2_norm"][i] = ffn_backward(
            dh, s_ffn2, P["ffn2_norm"][i:i + 1], W["ffn2_w_gu"][i], W["ffn2_w_down"][i], L + "_ffn2", emit=keep_ffn2)
        h2, u3, q, o, kv, umem = s_xa
        dW["xa_wo"][i] = matmul_tn(o, dh, L + "_xa_dwo")
        do = matmul_nt(dh, W["xa_wo"][i], BF16, L + "_xa_do")
        dq, dkv = xattn_bwd(q, kv, do, L + "_xa")
        dW["xa_wq"][i] = matmul_tn(u3, dq, L + "_xa_dwq")
        dW["xa_wkv"][i] = matmul_tn(dkv, umem, L + "_xa_dwkv", tk=1024)
        dkv_b = dkv.astype(BF16)
        _, dP["xa_mem_norm"][i] = matmul_norm_bwd(dkv_b, W["xa_wkv"][i], mem, P["xa_mem_norm"][i:i + 1],
                                                  jnp.zeros_like(mem), L + "_xa_dmem")
        dh, dP["xa_norm"][i] = matmul_norm_bwd(dq, W["xa_wq"][i], h2, P["xa_norm"][i:i + 1], dh, L + "_xa_dx",
                                               transposed=False)
        if i == 0:
            h1, u2, z, m, kpad, vpad, conv_out = s_mix
            dW["even_w_out"][0] = matmul_tn(m, dh, L + "_mix_dwo")
            dm = matmul_nt(dh, W["even_w_out"], F32, L + "_mix_dm")
            dz_conv, dcw, dcb, dlg, dlb = conformer_conv_bwd(z, conv_out, dm, cw, lg, lb, L + "_conv")
            dq_s, dkp, dvp, dsk = swa_bwd(z, kpad, vpad, sinks, dm, L + "_swa")
            dz = jnp.concatenate([dz_conv, dq_s, dkp[WINDOW:].astype(BF16), dvp[WINDOW:].astype(BF16)], axis=-1)
            dW["even_w_in"][0] = matmul_tn(dz, u2, L + "_mix_dwi", tk=896)
            dh, dP["mix_norm"][i] = matmul_norm_bwd(dz, W["even_w_in"], h1, P["mix_norm"][i:i + 1], dh, L + "_mix_dx")
            dP["conv_a_w"] = dcw[:CONV_A_WIDTH]
            dP["conv_a_b"], dP["conv_a_ln_g"], dP["conv_a_ln_b"] = dcb.reshape(-1), dlg.reshape(-1), dlb.reshape(-1)
            dP["swa_sinks"] = dsk[:, 0]
        else:
            h1, u2, z, m = s_mix
            dW["odd_w_out"][0] = matmul_tn(m, dh, L + "_mix_dwo")
            dm = matmul_nt(dh, W["odd_w_out"], F32, L + "_mix_dm")
            dz, dscw = short_conv_bwd(z, dm, scw, L + "_sconv")
            dW["odd_w_in"][0] = matmul_tn(dz, u2, L + "_mix_dwi", tk=1024)
            dh, dP["mix_norm"][i] = matmul_norm_bwd(dz, W["odd_w_in"], h1, P["mix_norm"][i:i + 1], dh, L + "_mix_dx")
            dP["sc_conv_w"] = dscw[:3]
        dep = put_grads("BC", {k: dW[k[0]][k[1]] for k in STAGE_KEYS["B"] + STAGE_KEYS["C"]}) if i == 0 else None
        dh, dP["ffn1_norm"][i] = ffn_backward(
            dh, s_ffn1, P["ffn1_norm"][i:i + 1], W["ffn1_w_gu"][i], W["ffn1_w_down"][i], L + "_ffn1", dep=dep,
            emit=send_ffn1)
    for k in ("ffn1_norm", "mix_norm", "xa_norm", "xa_mem_norm", "ffn2_norm"):
        dP[k] = jnp.concatenate(dP[k], axis=0)
    return loss, dh, dP


def _mesh_pos():
    return lax.axis_index("x"), lax.axis_index("y"), lax.axis_index("c")


def _flat_index(px, py, pc):
    return 4 * px + 2 * py + pc


def all_gather(blob, name):
    R, C = blob.shape

    def body(x_ref, out_ref, send_sems, recv_sems, local_sem):
        x, y, c = _mesh_pos()
        me, sibling = (x, y, c), (x, y, 1 - c)
        chips = [(1 - x, y), (x, 1 - y), (1 - x, 1 - y)]

        def slot(px, py, pc):
            return out_ref.at[_flat_index(px, py, pc)]

        def copy(k, block, to, src=None):
            return pltpu.make_async_remote_copy(
                src_ref=slot(*block) if src is None else src, dst_ref=slot(*block),
                send_sem=send_sems.at[k], recv_sem=recv_sems.at[k],
                device_id=to, device_id_type=pl.DeviceIdType.MESH)

        mine = pltpu.make_async_copy(x_ref, slot(*me), local_sem)
        mine.start()
        first = [copy(0, me, sibling, src=x_ref)]
        first += [copy(1 + j, me, (*chip, c), src=x_ref) for j, chip in enumerate(chips)]
        for cp in first:
            cp.start()
        passed = [copy(4 + j, (*chip, c), sibling) for j, chip in enumerate(chips)]
        for j, chip in enumerate(chips):
            copy(1 + j, (*chip, c), me).wait_recv()
            passed[j].start()
        copy(0, sibling, me).wait_recv()
        for j, chip in enumerate(chips):
            copy(4 + j, (*chip, 1 - c), me).wait_recv()
        for cp in first + passed:
            cp.wait_send()
        mine.wait()

    return pl.pallas_call(
        body, name=name,
        out_shape=jax.ShapeDtypeStruct((N_DEV, R, C), blob.dtype),
        in_specs=[pl.BlockSpec(memory_space=pl.ANY)],
        out_specs=pl.BlockSpec(memory_space=pl.ANY),
        scratch_shapes=[pltpu.SemaphoreType.DMA((7,)), pltpu.SemaphoreType.DMA((7,)), pltpu.SemaphoreType.DMA],
    )(blob)


def scatter_exchange(g, name):
    _, R, C = g.shape

    def body(g_ref, out_ref, send_sems, recv_sems, local_sem):
        x, y, c = _mesh_pos()
        me_idx = _flat_index(x, y, c)
        mine = pltpu.make_async_copy(g_ref.at[me_idx], out_ref.at[me_idx], local_sem)
        mine.start()
        sends, peers = [], []
        for k in range(1, N_DEV):
            px = 1 - x if k & 4 else x
            py = 1 - y if k & 2 else y
            pc = 1 - c if k & 1 else c
            peer_idx = _flat_index(px, py, pc)
            cp = pltpu.make_async_remote_copy(
                src_ref=g_ref.at[peer_idx], dst_ref=out_ref.at[me_idx],
                send_sem=send_sems.at[k - 1], recv_sem=recv_sems.at[k - 1],
                device_id=(px, py, pc), device_id_type=pl.DeviceIdType.MESH)
            cp.start()
            sends.append(cp)
            peers.append((peer_idx, (px, py, pc)))
        for k in range(1, N_DEV):
            peer_idx, peer = peers[k - 1]
            pltpu.make_async_remote_copy(
                src_ref=g_ref.at[me_idx], dst_ref=out_ref.at[peer_idx],
                send_sem=send_sems.at[k - 1], recv_sem=recv_sems.at[k - 1],
                device_id=peer, device_id_type=pl.DeviceIdType.MESH).wait_recv()
        for cp in sends:
            cp.wait_send()
        mine.wait()

    return pl.pallas_call(
        body, name=name,
        out_shape=jax.ShapeDtypeStruct(g.shape, g.dtype),
        in_specs=[pl.BlockSpec(memory_space=pl.ANY)],
        out_specs=pl.BlockSpec(memory_space=pl.ANY),
        scratch_shapes=[pltpu.SemaphoreType.DMA((7,)), pltpu.SemaphoreType.DMA((7,)), pltpu.SemaphoreType.DMA],
    )(g)


HBM_SPEC = pl.BlockSpec(memory_space=pltpu.HBM)
SEM_SPEC = pl.BlockSpec(memory_space=pltpu.SEMAPHORE)
DATAFLOW_EFFECT = pltpu.SideEffectType.DATAFLOW_SIDE_EFFECTING


def _peers(x, y, c):
    out = []
    for k in range(1, N_DEV):
        pos = (1 - x if k & 4 else x, 1 - y if k & 2 else y, 1 - c if k & 1 else c)
        out.append((_flat_index(*pos), pos))
    return out


def _exchange_copy(src_ref, land_ref, send_sems, recv_sems, j, me, peer_idx, peer, scatter):
    return pltpu.make_async_remote_copy(
        src_ref=src_ref.at[peer_idx] if scatter else src_ref, dst_ref=land_ref.at[me],
        send_sem=send_sems.at[j], recv_sem=recv_sems.at[j], device_id=peer, device_id_type=pl.DeviceIdType.MESH)


def exchange_start(srcs, lands, scatter, after, name):
    n = len(srcs)
    n_after = len(after)

    def body(*refs):
        src_refs, land_refs = refs[:n], refs[n:2 * n]
        outs = refs[2 * n + n_after:]
        send_sems, recv_sems, token = outs[:n], outs[n:2 * n], outs[4 * n]
        x, y, c = _mesh_pos()
        me = _flat_index(x, y, c)
        for g in range(n):
            for j, (peer_idx, peer) in enumerate(_peers(x, y, c)):
                _exchange_copy(src_refs[g], land_refs[g], send_sems[g], recv_sems[g], j, me, peer_idx, peer, scatter).start()
        token[...] = jnp.zeros_like(token)

    hbm = lambda a: pltpu.with_memory_space_constraint(a, pltpu.HBM)
    res = pl.pallas_call(
        body, name=name,
        out_shape=(*[pltpu.SemaphoreType.DMA((N_DEV - 1,))] * (2 * n),
                   *[pltpu.HBM(a.shape, a.dtype) for a in srcs], *[pltpu.HBM(a.shape, a.dtype) for a in lands],
                   jax.ShapeDtypeStruct((8, 128), F32)),
        in_specs=[HBM_SPEC] * (2 * n) + [ANY_SPEC] * n_after,
        out_specs=(*[SEM_SPEC] * (2 * n), *[HBM_SPEC] * (2 * n), pl.BlockSpec(memory_space=pltpu.VMEM)),
        input_output_aliases={i: 2 * n + i for i in range(2 * n)},
        compiler_params=pltpu.CompilerParams(has_side_effects=DATAFLOW_EFFECT),
    )(*[hbm(a) for a in srcs], *[hbm(a) for a in lands], *after)
    handles = [(res[g], res[n + g], res[2 * n + g], res[3 * n + g]) for g in range(n)]
    return handles, res[4 * n]


def exchange_wait(handle, scatter, after, name):
    send_sem, recv_sem, src_thru, land_thru = handle

    def body(src_ref, land_ref, send_sems, recv_sems, after_ref, src_dead, got_ref):
        x, y, c = _mesh_pos()
        me = _flat_index(x, y, c)
        for j, (peer_idx, peer) in enumerate(_peers(x, y, c)):
            mine = _exchange_copy(src_ref, land_ref, send_sems, recv_sems, j, me, peer_idx, peer, scatter)
            mine.wait_send()
            theirs = pltpu.make_async_remote_copy(
                src_ref=src_ref.at[me] if scatter else src_ref, dst_ref=land_ref.at[peer_idx],
                send_sem=send_sems.at[j], recv_sem=recv_sems.at[j], device_id=peer, device_id_type=pl.DeviceIdType.MESH)
            theirs.wait_recv()

    return pl.pallas_call(
        body, name=name,
        out_shape=(pltpu.HBM(src_thru.shape, src_thru.dtype), pltpu.HBM(land_thru.shape, land_thru.dtype)),
        in_specs=(HBM_SPEC, HBM_SPEC, SEM_SPEC, SEM_SPEC, ANY_SPEC), out_specs=(HBM_SPEC, HBM_SPEC),
        input_output_aliases={0: 0, 1: 1},
        compiler_params=pltpu.CompilerParams(has_side_effects=DATAFLOW_EFFECT),
    )(src_thru, land_thru, send_sem, recv_sem, after)


def ordered_sum(parts, name, tr=512):
    n, R, C = parts.shape
    tr = next((t for t in range(min(tr, R), 15, -16) if R % t == 0), R)

    def body(p_ref, o_ref):
        acc = p_ref[0].astype(F32)
        for j in range(1, n):
            acc = acc + p_ref[j].astype(F32)
        o_ref[...] = acc

    return pl.pallas_call(
        body, name=name, grid=(R // tr,),
        in_specs=[pl.BlockSpec((n, tr, C), lambda i: (0, i, 0))],
        out_specs=pl.BlockSpec((tr, C), lambda i: (i, 0)),
        out_shape=jax.ShapeDtypeStruct((R, C), F32),
        compiler_params=_params(1),
    )(parts)


def adamw(w, g, m, v, name, tr=256):
    R, C = w.shape
    tr = next((t for t in range(tr, 7, -8) if R % t == 0), R)
    c1 = 1.0 - ADAM_B1 ** ADAM_STEP
    c2 = 1.0 - ADAM_B2 ** ADAM_STEP

    def body(w_ref, g_ref, m_ref, v_ref, d_ref, mo_ref, vo_ref):
        grad = g_ref[...]
        m2 = ADAM_B1 * m_ref[...] + (1.0 - ADAM_B1) * grad
        v2 = ADAM_B2 * v_ref[...] + (1.0 - ADAM_B2) * (grad * grad)
        mo_ref[...] = m2
        vo_ref[...] = v2
        d_ref[...] = -ADAM_LR * ((m2 / c1) / (jnp.sqrt(v2 / c2) + ADAM_EPS) + ADAM_WD * w_ref[...])

    spec = pl.BlockSpec((tr, C), lambda i: (i, 0))
    return pl.pallas_call(
        body, name=name, grid=(R // tr,),
        in_specs=[spec] * 4, out_specs=[spec] * 3,
        out_shape=[jax.ShapeDtypeStruct((R, C), F32)] * 3,
        compiler_params=_params(1),
    )(w, g, m, v)


WEIGHT_NAMES = ("ffn1_norm", "ffn1_w_gu", "ffn1_w_down", "mix_norm", "even_w_in", "conv_a_w", "conv_a_b", "conv_a_ln_g",
                "conv_a_ln_b", "swa_sinks", "even_w_out", "odd_w_in", "sc_conv_w", "odd_w_out", "xa_norm", "xa_mem_norm",
                "xa_wq", "xa_wkv", "xa_wo", "ffn2_norm", "ffn2_w_gu", "ffn2_w_down", "final_norm")
BLOB_COLS = 1024
SMALL_ROWS = (("ffn1_norm", 0, 2), ("mix_norm", 2, 2), ("xa_norm", 4, 2), ("xa_mem_norm", 6, 2), ("ffn2_norm", 8, 2),
              ("final_norm", 10, 1))
ROW_CONV_B_LNG = 11
ROW_LNB_SINKS_LOSS = 12
LOSS_COL = 512 + SWA_HEADS
ROW_SC_CONV = 13
ROW_CONV_W = 16
SMALL_BLOB_ROWS = 32
SMALL_ADAM_ROWS = 16


def _small_blob(v):
    rows = [v[n].reshape(-1, D_MODEL) for n, _, _ in SMALL_ROWS]
    rows.append(jnp.concatenate([v["conv_a_b"].reshape(-1), v["conv_a_ln_g"].reshape(-1)]).reshape(1, D_MODEL))
    tail = jnp.zeros((D_MODEL - 512 - SWA_HEADS,), F32)
    if "loss" in v:
        tail = tail.at[0].set(v["loss"])
    rows.append(jnp.concatenate([v["conv_a_ln_b"].reshape(-1), v["swa_sinks"].reshape(-1), tail]).reshape(1, D_MODEL))
    rows.append(jnp.zeros((SMALL_ADAM_ROWS - ROW_SC_CONV, D_MODEL), F32))
    return jnp.concatenate(rows, axis=0)


def _small_unblob(b, shapes):
    out = {n: b[r:r + k].reshape(shapes[n]) for n, r, k in SMALL_ROWS}
    out["conv_a_b"] = b[ROW_CONV_B_LNG, :512].reshape(shapes["conv_a_b"])
    out["conv_a_ln_g"] = b[ROW_CONV_B_LNG, 512:].reshape(shapes["conv_a_ln_g"])
    out["conv_a_ln_b"] = b[ROW_LNB_SINKS_LOSS, :512].reshape(shapes["conv_a_ln_b"])
    out["swa_sinks"] = b[ROW_LNB_SINKS_LOSS, 512:512 + SWA_HEADS].reshape(shapes["swa_sinks"])
    return out


def kernel(x, mem, ffn1_norm, ffn1_w_gu, ffn1_w_down, mix_norm, even_w_in, conv_a_w, conv_a_b, conv_a_ln_g, conv_a_ln_b, swa_sinks, even_w_out, odd_w_in, sc_conv_w, odd_w_out, xa_norm, xa_mem_norm, xa_wq, xa_wkv, xa_wo, ffn2_norm, ffn2_w_gu, ffn2_w_down, final_norm, loss_target, m_ffn1_norm, m_ffn1_w_gu, m_ffn1_w_down, m_mix_norm, m_even_w_in, m_conv_a_w, m_conv_a_b, m_conv_a_ln_g, m_conv_a_ln_b, m_swa_sinks, m_even_w_out, m_odd_w_in, m_sc_conv_w, m_odd_w_out, m_xa_norm, m_xa_mem_norm, m_xa_wq, m_xa_wkv, m_xa_wo, m_ffn2_norm, m_ffn2_w_gu, m_ffn2_w_down, m_final_norm, v_ffn1_norm, v_ffn1_w_gu, v_ffn1_w_down, v_mix_norm, v_even_w_in, v_conv_a_w, v_conv_a_b, v_conv_a_ln_g, v_conv_a_ln_b, v_swa_sinks, v_even_w_out, v_odd_w_in, v_sc_conv_w, v_odd_w_out, v_xa_norm, v_xa_mem_norm, v_xa_wq, v_xa_wkv, v_xa_wo, v_ffn2_norm, v_ffn2_w_gu, v_ffn2_w_down, v_final_norm):
    w = dict(ffn1_norm=ffn1_norm, ffn1_w_gu=ffn1_w_gu, ffn1_w_down=ffn1_w_down, mix_norm=mix_norm, even_w_in=even_w_in,
             conv_a_w=conv_a_w, conv_a_b=conv_a_b, conv_a_ln_g=conv_a_ln_g, conv_a_ln_b=conv_a_ln_b, swa_sinks=swa_sinks,
             even_w_out=even_w_out, odd_w_in=odd_w_in, sc_conv_w=sc_conv_w, odd_w_out=odd_w_out, xa_norm=xa_norm,
             xa_mem_norm=xa_mem_norm, xa_wq=xa_wq, xa_wkv=xa_wkv, xa_wo=xa_wo, ffn2_norm=ffn2_norm, ffn2_w_gu=ffn2_w_gu,
             ffn2_w_down=ffn2_w_down, final_norm=final_norm)
    m = dict(ffn1_norm=m_ffn1_norm, ffn1_w_gu=m_ffn1_w_gu, ffn1_w_down=m_ffn1_w_down, mix_norm=m_mix_norm,
             even_w_in=m_even_w_in, conv_a_w=m_conv_a_w, conv_a_b=m_conv_a_b, conv_a_ln_g=m_conv_a_ln_g,
             conv_a_ln_b=m_conv_a_ln_b, swa_sinks=m_swa_sinks, even_w_out=m_even_w_out, odd_w_in=m_odd_w_in,
             sc_conv_w=m_sc_conv_w, odd_w_out=m_odd_w_out, xa_norm=m_xa_norm, xa_mem_norm=m_xa_mem_norm, xa_wq=m_xa_wq,
             xa_wkv=m_xa_wkv, xa_wo=m_xa_wo, ffn2_norm=m_ffn2_norm, ffn2_w_gu=m_ffn2_w_gu, ffn2_w_down=m_ffn2_w_down,
             final_norm=m_final_norm)
    v = dict(ffn1_norm=v_ffn1_norm, ffn1_w_gu=v_ffn1_w_gu, ffn1_w_down=v_ffn1_w_down, mix_norm=v_mix_norm,
             even_w_in=v_even_w_in, conv_a_w=v_conv_a_w, conv_a_b=v_conv_a_b, conv_a_ln_g=v_conv_a_ln_g,
             conv_a_ln_b=v_conv_a_ln_b, swa_sinks=v_swa_sinks, even_w_out=v_even_w_out, odd_w_in=v_odd_w_in,
             sc_conv_w=v_sc_conv_w, odd_w_out=v_odd_w_out, xa_norm=v_xa_norm, xa_mem_norm=v_xa_mem_norm, xa_wq=v_xa_wq,
             xa_wkv=v_xa_wkv, xa_wo=v_xa_wo, ffn2_norm=v_ffn2_norm, ffn2_w_gu=v_ffn2_w_gu, ffn2_w_down=v_ffn2_w_down,
             final_norm=v_final_norm)
    me = _flat_index(*_mesh_pos())

    conv_blob = jnp.concatenate([w["conv_a_w"].reshape(-1), w["sc_conv_w"].reshape(-1),
                                 jnp.zeros((8 * 1024 - 31 * 64 - 3 * 128,), F32)]).reshape(8, 1024)
    conv_all = all_gather(conv_blob, "gather_conv_weights").reshape(N_DEV, 8 * 1024)
    conv_a_full = jnp.transpose(conv_all[:, :31 * 64].reshape(N_DEV, 31, 64), (1, 0, 2)).reshape(31, 512)
    sc_full = jnp.transpose(conv_all[:, 31 * 64:31 * 64 + 3 * 128].reshape(N_DEV, 3, 128), (1, 0, 2)).reshape(3, 1024)

    def stage_blob(keys):
        shards = [w[n][l].T if SPLIT_AXIS[n] == 1 else w[n][l] for n, l in keys]
        return jnp.concatenate([s.astype(BF16) for s in shards], axis=0)

    def stage_rows(keys):
        return [(n, l, w[n].shape[2] if SPLIT_AXIS[n] == 1 else w[n].shape[1]) for n, l in keys]

    def unpack_weights(gathered, keys):
        out, off = {}, 0
        for n, l, rows in stage_rows(keys):
            out[(n, l)] = gathered[:, off:off + rows, :].reshape(N_DEV * rows, BLOB_COLS)
            off += rows
        return out

    def with_own(land, own):
        return lax.dynamic_update_slice(land, own[None], (me, 0, 0))

    gathered_a = all_gather(stage_blob(STAGE_KEYS["A"]), "gather_weights_a")
    later = ("B", "C", "D")
    blobs = [stage_blob(STAGE_KEYS[s]) for s in later]
    lands = [lax.empty((N_DEV,) + b.shape, BF16) for b in blobs]
    weight_handles, weight_token = exchange_start(blobs, lands, False, [gathered_a, conv_all], "gather_start")

    def get_weights(stage, after):
        if stage == "A":
            return unpack_weights(gathered_a, STAGE_KEYS["A"]), weight_token
        own, land = exchange_wait(weight_handles[later.index(stage)], False, after, "gather_wait_" + stage.lower())
        return unpack_weights(with_own(land, own), STAGE_KEYS[stage]), None

    grad_handles = {}

    def put_grads(stage, dws):
        packed = jnp.concatenate([dw.reshape(N_DEV, -1, BLOB_COLS) for dw in dws.values()], axis=1)
        land = lax.empty(packed.shape, BF16)
        (handle,), token = exchange_start([packed], [land], True, [], "scatter_start_" + stage.lower())
        grad_handles[stage] = (handle, tuple(dws))
        return token

    P = dict(ffn1_norm=ffn1_norm, mix_norm=mix_norm, xa_norm=xa_norm, xa_mem_norm=xa_mem_norm, ffn2_norm=ffn2_norm,
             final_norm=final_norm, conv_a_w=conv_a_full, conv_a_b=conv_a_b[0], conv_a_ln_g=conv_a_ln_g[0],
             conv_a_ln_b=conv_a_ln_b[0], swa_sinks=swa_sinks[0], sc_conv_w=sc_full)

    loss_part, grad_x, dP = local_step(x[0], mem[0], loss_target[0], P, get_weights, put_grads)

    def finish_grads(stage, after):
        handle, keys = grad_handles[stage]
        packed, land = exchange_wait(handle, True, after, "scatter_wait_" + stage.lower())
        own = lax.dynamic_slice(packed, (me, 0, 0), (1,) + packed.shape[1:])[0]
        rows_f32 = ordered_sum(with_own(land, own), "sum_grads_" + stage.lower())
        out, off = {}, 0
        for n, l, rows in stage_rows(keys):
            part = rows_f32[off:off + rows]
            out[(n, l)] = part.T if SPLIT_AXIS[n] == 1 else part
            off += rows
        return out

    layer_grads = {**finish_grads("D", grad_x), **finish_grads("BC", grad_x)}

    dP = dict(dP, loss=loss_part[0, 0])
    small = jnp.concatenate([
        _small_blob(dP)[:ROW_SC_CONV], dP["sc_conv_w"],
        jnp.concatenate([dP["conv_a_w"].reshape(-1), jnp.zeros((512,), F32)]).reshape(16, D_MODEL)], axis=0)
    small_sum = ordered_sum(all_gather(small, "gather_small_grads"), "sum_small_grads", tr=SMALL_BLOB_ROWS)
    loss = small_sum[ROW_LNB_SINKS_LOSS, LOSS_COL]
    grads = _small_unblob(small_sum, {n: w[n].shape for n in WEIGHT_NAMES})
    sc_g = small_sum[ROW_SC_CONV:ROW_SC_CONV + 3]
    grads["sc_conv_w"] = lax.dynamic_slice(sc_g, (0, me * 128), (3, 128)).reshape(w["sc_conv_w"].shape)
    cw_g = small_sum[ROW_CONV_W:].reshape(-1)[:31 * 512].reshape(31, 512)
    grads["conv_a_w"] = lax.dynamic_slice(cw_g, (0, me * 64), (31, 64)).reshape(w["conv_a_w"].shape)

    delta, new_m, new_v = {}, {}, {}

    def update(n):
        shp = w[n].shape
        two_d = (shp[0] * shp[1], shp[2])
        d_, m_, v_ = adamw(w[n].reshape(two_d), grads[n].reshape(two_d), m[n].reshape(two_d), v[n].reshape(two_d),
                           "adamw_" + n)
        delta[n], new_m[n], new_v[n] = d_.reshape(shp), m_.reshape(shp), v_.reshape(shp)

    first_stage = tuple(n for n, _ in STAGE_KEYS["A"])
    for n in SPLIT_AXIS:
        if n not in first_stage:
            grads[n] = jnp.stack([layer_grads[(n, l)] for l in range(w[n].shape[0])], axis=0)
            update(n)
    update("conv_a_w")
    update("sc_conv_w")
    layer_grads.update(finish_grads("A", delta["ffn2_w_gu"]))
    for n in first_stage:
        grads[n] = jnp.stack([layer_grads[(n, l)] for l in range(w[n].shape[0])], axis=0)
        update(n)
    d_, m_, v_ = adamw(_small_blob(w), small_sum[:SMALL_ADAM_ROWS], _small_blob(m), _small_blob(v), "adamw_small",
                       tr=SMALL_ADAM_ROWS)
    shapes = {n: w[n].shape for n in WEIGHT_NAMES}
    delta.update(_small_unblob(d_, shapes))
    new_m.update(_small_unblob(m_, shapes))
    new_v.update(_small_unblob(v_, shapes))

    return (loss, grad_x[None], *[grads[n] for n in WEIGHT_NAMES], *[delta[n] for n in WEIGHT_NAMES],
            *[new_m[n] for n in WEIGHT_NAMES], *[new_v[n] for n in WEIGHT_NAMES])
```

```python
import functools

import jax
import jax.numpy as jnp
from jax import lax
from jax.experimental import pallas as pl
from jax.experimental.pallas import tpu as pltpu

F32 = jnp.float32
BF16 = jnp.bfloat16

D_MODEL = 1024
D_FF = 2816
CONV_A_CH = 512
CONV_A_WIDTH = 31
SWA_HEADS = 8
SWA_KV_HEADS = 2
SWA_GROUP = SWA_HEADS // SWA_KV_HEADS
HEAD_DIM = 64
WINDOW = 128
SC_CH = 1024
XA_HEADS = 4
XA_HEAD_DIM = D_MODEL // XA_HEADS
RMS_EPS = 1e-6
LN_EPS = 1e-5
ADAM_LR = 0.001
ADAM_B1 = 0.9
ADAM_B2 = 0.999
ADAM_EPS = 1e-08
ADAM_WD = 0.01
ADAM_STEP = 10
N_DEV = 8

V7X_VMEM_BYTES = 64 * 1024 * 1024
VMEM_LIMIT = V7X_VMEM_BYTES - 8 * 1024 * 1024
CONV_HALO = 32
SC_HALO = 8
NEG_BIG = -1e30

SPLIT_AXIS = dict(ffn1_w_gu=1, ffn1_w_down=0, even_w_in=1, even_w_out=0, odd_w_in=1, odd_w_out=0, xa_wq=0, xa_wkv=1, xa_wo=0,
                  ffn2_w_gu=1, ffn2_w_down=0)
STAGE_KEYS = dict(
    A=(("ffn1_w_gu", 0), ("ffn1_w_down", 0)),
    B=(("even_w_in", 0), ("even_w_out", 0), ("xa_wq", 0), ("xa_wkv", 0), ("xa_wo", 0)),
    C=(("ffn2_w_gu", 0), ("ffn2_w_down", 0)),
    D=(("ffn1_w_gu", 1), ("ffn1_w_down", 1), ("odd_w_in", 0), ("odd_w_out", 0), ("xa_wq", 1), ("xa_wkv", 1), ("xa_wo", 1),
       ("ffn2_w_gu", 1), ("ffn2_w_down", 1)))


def _params(n_axes):
    return pltpu.CompilerParams(dimension_semantics=("arbitrary",) * n_axes, vmem_limit_bytes=VMEM_LIMIT)


def _tile(n, pref):
    t = min(n, pref)
    assert n % t == 0, (n, pref)
    return t


def _dot(a, b):
    return jnp.dot(a, b, preferred_element_type=F32)


def _dot_nt(a, b):
    return lax.dot_general(a, b, (((1,), (1,)), ((), ())), preferred_element_type=F32)


def _dot_tn(a, b):
    return lax.dot_general(a, b, (((0,), (0,)), ((), ())), preferred_element_type=F32)


def _sigmoid(x):
    return 0.5 * jnp.tanh(0.5 * x) + 0.5


ANY_SPEC = pl.BlockSpec(memory_space=pl.ANY)


def _with_dep(body, n_in, dep):
    if dep is None:
        return body, [], []
    return (lambda *refs: body(*refs[:n_in], *refs[n_in + 1:])), [ANY_SPEC], [dep]


def rmsnorm(h, g, name, tm=1024, dep=None):
    T, K = h.shape
    tm = _tile(T, tm)

    def kern(h_ref, g_ref, u_ref):
        x = h_ref[...]
        r = lax.rsqrt(jnp.mean(x * x, axis=-1, keepdims=True) + RMS_EPS)
        u_ref[...] = ((x * r) * g_ref[...]).astype(BF16)

    body, dep_spec, dep_arg = _with_dep(kern, 2, dep)
    return pl.pallas_call(
        body, name=name, grid=(T // tm,),
        in_specs=[pl.BlockSpec((tm, K), lambda i: (i, 0)), pl.BlockSpec((1, K), lambda i: (0, 0))] + dep_spec,
        out_specs=pl.BlockSpec((tm, K), lambda i: (i, 0)),
        out_shape=jax.ShapeDtypeStruct((T, K), BF16),
        compiler_params=_params(1),
    )(h, g, *dep_arg)


def matmul(a, w, out_dtype, name, tn, tm=2048, transposed=True, n_tiles=None, first_tile=0):
    T, K = a.shape
    N = w.shape[0] if transposed else w.shape[1]
    n_tiles = N // tn if n_tiles is None else n_tiles
    tm = _tile(T, tm)
    mm = _dot_nt if transposed else _dot

    def body(a_ref, w_ref, z_ref):
        z_ref[...] = mm(a_ref[...], w_ref[...]).astype(z_ref.dtype)

    w_spec = (pl.BlockSpec((tn, K), lambda i, j: (first_tile + j, 0)) if transposed
              else pl.BlockSpec((K, tn), lambda i, j: (0, first_tile + j)))
    return pl.pallas_call(
        body, name=name, grid=(T // tm, n_tiles),
        in_specs=[pl.BlockSpec((tm, K), lambda i, j: (i, 0)), w_spec],
        out_specs=pl.BlockSpec((tm, tn), lambda i, j: (i, j)),
        out_shape=jax.ShapeDtypeStruct((T, n_tiles * tn), out_dtype),
        compiler_params=_params(2),
    )(a, w)


def norm_matmul(h, g, w, out_dtype, name, tn, dep=None, transposed=True):
    u = rmsnorm(h, g, name + "_norm", dep=dep)
    return matmul(u, w, out_dtype, name, tn, transposed=transposed), u


def matmul_residual(a, w, res, name, tm=1024):
    T, K = a.shape
    N = w.shape[1]
    tm = _tile(T, tm)

    def body(a_ref, w_ref, r_ref, o_ref):
        o_ref[...] = r_ref[...] + _dot(a_ref[...], w_ref[...])

    return pl.pallas_call(
        body, name=name, grid=(T // tm,),
        in_specs=[pl.BlockSpec((tm, K), lambda i: (i, 0)),
                  pl.BlockSpec((K, N), lambda i: (0, 0)),
                  pl.BlockSpec((tm, N), lambda i: (i, 0))],
        out_specs=pl.BlockSpec((tm, N), lambda i: (i, 0)),
        out_shape=jax.ShapeDtypeStruct((T, N), F32),
        compiler_params=_params(1),
    )(a, w, res)


def matmul_nt(dy, w, out_dtype, name, tm=1024):
    T, N = dy.shape
    K = w.shape[0]
    tm = _tile(T, tm)

    def body(dy_ref, w_ref, o_ref):
        o_ref[...] = _dot_nt(dy_ref[...].astype(BF16), w_ref[...]).astype(o_ref.dtype)

    return pl.pallas_call(
        body, name=name, grid=(T // tm,),
        in_specs=[pl.BlockSpec((tm, N), lambda i: (i, 0)),
                  pl.BlockSpec((K, N), lambda i: (0, 0))],
        out_specs=pl.BlockSpec((tm, K), lambda i: (i, 0)),
        out_shape=jax.ShapeDtypeStruct((T, K), out_dtype),
        compiler_params=_params(1),
    )(dy, w)


def matmul_norm_bwd(dz, w, h, g, dh_in, name, tm=512, transposed=True, dep=None):
    T, N = dz.shape
    K = h.shape[1]
    tm = _tile(T, tm)

    def kern(dz_ref, w_ref, h_ref, g_ref, dhin_ref, dh_ref, dg_ref):
        @pl.when(pl.program_id(0) == 0)
        def _():
            dg_ref[...] = jnp.zeros_like(dg_ref)

        mm = _dot if transposed else _dot_nt
        du = mm(dz_ref[...], w_ref[...])
        x = h_ref[...]
        r = lax.rsqrt(jnp.mean(x * x, axis=-1, keepdims=True) + RMS_EPS)
        xh = x * r
        dg_ref[...] += jnp.sum(du * xh, axis=0, keepdims=True)
        dxh = du * g_ref[...]
        dh_ref[...] = dhin_ref[...] + r * (dxh - xh * jnp.mean(dxh * xh, axis=-1, keepdims=True))

    body, dep_spec, dep_arg = _with_dep(kern, 5, dep)
    return pl.pallas_call(
        body, name=name, grid=(T // tm,),
        in_specs=[pl.BlockSpec((tm, N), lambda i: (i, 0)),
                  pl.BlockSpec(w.shape, lambda i: (0, 0)),
                  pl.BlockSpec((tm, K), lambda i: (i, 0)),
                  pl.BlockSpec((1, K), lambda i: (0, 0)),
                  pl.BlockSpec((tm, K), lambda i: (i, 0))] + dep_spec,
        out_specs=[pl.BlockSpec((tm, K), lambda i: (i, 0)),
                   pl.BlockSpec((1, K), lambda i: (0, 0))],
        out_shape=[jax.ShapeDtypeStruct((T, K), F32), jax.ShapeDtypeStruct((1, K), F32)],
        compiler_params=_params(1),
    )(dz, w, h, g, dh_in, *dep_arg)


def matmul_tn(x, dy, name, scale=1.0, tk=None, tn=None, tt=1024):
    T, K = x.shape
    N = dy.shape[1]
    tk = K if tk is None else tk
    tn = N if tn is None else tn
    tt = _tile(T, tt)
    nt = T // tt

    def body(x_ref, dy_ref, o_ref, acc_ref):
        t = pl.program_id(2)

        @pl.when(t == 0)
        def _():
            acc_ref[...] = jnp.zeros_like(acc_ref)

        acc_ref[...] += _dot_tn(x_ref[...].astype(BF16), dy_ref[...].astype(BF16))

        @pl.when(t == nt - 1)
        def _():
            o_ref[...] = (acc_ref[...] * scale).astype(o_ref.dtype)

    return pl.pallas_call(
        body, name=name, grid=(K // tk, N // tn, nt),
        in_specs=[pl.BlockSpec((tt, tk), lambda a, b, t: (t, a)),
                  pl.BlockSpec((tt, tn), lambda a, b, t: (t, b))],
        out_specs=pl.BlockSpec((tk, tn), lambda a, b, t: (a, b)),
        out_shape=jax.ShapeDtypeStruct((K, N), BF16),
        scratch_shapes=[pltpu.VMEM((tk, tn), F32)],
        compiler_params=_params(3),
    )(x, dy)


def ffn_down(gu, wd, res, name, tm=512):
    T = gu.shape[0]
    F = gu.shape[1] // 2
    N = wd.shape[1]
    tm = _tile(T, tm)

    def body(g_ref, up_ref, w_ref, r_ref, o_ref, a_ref):
        g = g_ref[...].astype(F32)
        a_ref[...] = ((g * _sigmoid(g)) * up_ref[...].astype(F32)).astype(BF16)
        o_ref[...] = r_ref[...] + 0.5 * _dot(a_ref[...], w_ref[...])

    return pl.pallas_call(
        body, name=name, grid=(T // tm,),
        in_specs=[pl.BlockSpec((tm, F), lambda i: (i, 0)),
                  pl.BlockSpec((tm, F), lambda i: (i, 1)),
                  pl.BlockSpec((F, N), lambda i: (0, 0)),
                  pl.BlockSpec((tm, N), lambda i: (i, 0))],
        out_specs=[pl.BlockSpec((tm, N), lambda i: (i, 0)),
                   pl.BlockSpec((tm, F), lambda i: (i, 0))],
        out_shape=[jax.ShapeDtypeStruct((T, N), F32), jax.ShapeDtypeStruct((T, F), BF16)],
        compiler_params=_params(1),
    )(gu, gu, wd, res)


def ffn_down_bwd(dy, wd, gu, name, tm=512, dep=None):
    T, N = dy.shape
    F = wd.shape[0]
    tm = _tile(T, tm)

    def kern(dy_ref, w_ref, g_ref, up_ref, o_ref):
        da = 0.5 * _dot_nt(dy_ref[...].astype(BF16), w_ref[...])
        g = g_ref[...].astype(F32)
        up = up_ref[...].astype(F32)
        s = _sigmoid(g)
        o_ref[:, :F] = (da * up * (s * (1.0 + g * (1.0 - s)))).astype(BF16)
        o_ref[:, F:] = (da * (g * s)).astype(BF16)

    body, dep_spec, dep_arg = _with_dep(kern, 4, dep)
    return pl.pallas_call(
        body, name=name, grid=(T // tm,),
        in_specs=[pl.BlockSpec((tm, N), lambda i: (i, 0)),
                  pl.BlockSpec((F, N), lambda i: (0, 0)),
                  pl.BlockSpec((tm, F), lambda i: (i, 0)),
                  pl.BlockSpec((tm, F), lambda i: (i, 1))] + dep_spec,
        out_specs=pl.BlockSpec((tm, 2 * F), lambda i: (i, 0)),
        out_shape=jax.ShapeDtypeStruct((T, 2 * F), BF16),
        compiler_params=_params(1),
    )(dy, wd, gu, gu, *dep_arg)


def ffn_forward(h, g, w_gu, w_down, name, dep=None):
    gu, u = norm_matmul(h, g, w_gu, BF16, name + "_gu", D_FF // 2, dep=dep)
    h_out, a = ffn_down(gu, w_down, h, name + "_down")
    return h_out, (h, u, gu, a)


def ffn_backward(dy, saved, g, w_gu, w_down, name, dep=None, emit=None):
    h, u, gu, a = saved
    dgu = ffn_down_bwd(dy, w_down, gu, name + "_ddown", dep=dep)
    d_w_down = matmul_tn(a, dy, name + "_dwd", scale=0.5, tk=D_FF // 2)
    d_w_gu = matmul_tn(dgu, u, name + "_dwgu", tk=D_FF)
    dh, dg = matmul_norm_bwd(dgu, w_gu, h, g, dy, name + "_dx", dep=emit(d_w_gu, d_w_down))
    return dh, dg


CONV_ROW_CHUNK = 32
CONV_LANES = 128
CONV_X_OFFSETS = tuple(CONV_HALO - (CONV_A_WIDTH - 1) + k for k in range(CONV_A_WIDTH))
CONV_D_OFFSETS = tuple(CONV_A_WIDTH - 1 - k for k in range(CONV_A_WIDTH))


def _build_phases(ref, phase_ref, n_rows):
    for r in range(1, 8):
        phase_ref[r - 1] = ref[pl.ds(r, n_rows - 8), :]


def _tap_values(ref, phase_ref, offsets, n, base, lanes):
    out = {}
    for r in range(8):
        qs = sorted(o // 8 for o in offsets if o % 8 == r)
        if qs:
            lo, hi = qs[0], qs[-1]
            rows = pl.ds(base + 8 * lo, n + 8 * (hi - lo))
            span = ref[rows, lanes] if r == 0 else phase_ref[r - 1, rows, lanes]
            for q in qs:
                out[8 * q + r] = span[8 * (q - lo):8 * (q - lo) + n]
    return out


def conformer_conv_fwd(z, cw, cb, lg, lb, name, tm=512):
    T = z.shape[0]
    C = CONV_A_CH
    tm = _tile(T, tm)
    hb = tm // CONV_HALO
    CH = CONV_ROW_CHUNK
    KW = CONV_A_WIDTH

    def body(v_ref, gt_ref, pv_ref, pg_ref, cw_ref, cb_ref, lg_ref, lb_ref, o_ref, conv_ref, xs_ref, xph_ref):
        i = pl.program_id(0)
        prev = pv_ref[...] * _sigmoid(pg_ref[...])
        xs_ref[0:CONV_HALO, :] = jnp.where(i > 0, prev, 0.0)
        xs_ref[CONV_HALO:, :] = v_ref[...] * _sigmoid(gt_ref[...])
        _build_phases(xs_ref, xph_ref, tm + CONV_HALO)

        def chunk(c, carry):
            off = pl.multiple_of(c * CH, CH)
            for l0 in range(0, C, CONV_LANES):
                lanes = slice(l0, l0 + CONV_LANES)
                taps = _tap_values(xs_ref, xph_ref, CONV_X_OFFSETS, CH, off, lanes)
                acc = jnp.zeros((CH, CONV_LANES), F32) + cb_ref[:, lanes]
                for k in range(KW):
                    acc = acc + cw_ref[k:k + 1, lanes] * taps[CONV_X_OFFSETS[k]]
                conv_ref[pl.ds(off, CH), lanes] = acc
            return carry

        lax.fori_loop(0, tm // CH, chunk, 0)
        acc = conv_ref[...]
        mu = jnp.mean(acc, axis=-1, keepdims=True)
        xc = acc - mu
        var = jnp.mean(xc * xc, axis=-1, keepdims=True)
        y = (xc * lax.rsqrt(var + LN_EPS)) * lg_ref[...] + lb_ref[...]
        o_ref[...] = (y * _sigmoid(y)).astype(BF16)

    return pl.pallas_call(
        body, name=name, grid=(T // tm,),
        in_specs=[pl.BlockSpec((tm, C), lambda i: (i, 0)),
                  pl.BlockSpec((tm, C), lambda i: (i, 1)),
                  pl.BlockSpec((CONV_HALO, C), lambda i: (jnp.maximum(i * hb - 1, 0), 0)),
                  pl.BlockSpec((CONV_HALO, C), lambda i: (jnp.maximum(i * hb - 1, 0), 1)),
                  pl.BlockSpec((32, C), lambda i: (0, 0)),
                  pl.BlockSpec((1, C), lambda i: (0, 0)),
                  pl.BlockSpec((1, C), lambda i: (0, 0)),
                  pl.BlockSpec((1, C), lambda i: (0, 0))],
        out_specs=[pl.BlockSpec((tm, C), lambda i: (i, 0)), pl.BlockSpec((tm, C), lambda i: (i, 0))],
        out_shape=[jax.ShapeDtypeStruct((T, C), BF16), jax.ShapeDtypeStruct((T, C), F32)],
        scratch_shapes=[pltpu.VMEM((tm + CONV_HALO, C), F32), pltpu.VMEM((7, tm + CONV_HALO - 8, C), F32)],
        compiler_params=_params(1),
    )(z, z, z, z, cw, cb, lg, lb)


def conformer_conv_bwd(z, conv_out, dm, cw, lg, lb, name, tm=512):
    T = z.shape[0]
    C = CONV_A_CH
    tm = _tile(T, tm)
    hb = tm // CONV_HALO
    n_tiles = T // tm
    last_halo = T // CONV_HALO - 1
    R = tm + CONV_HALO
    KW = CONV_A_WIDTH
    CH = CONV_ROW_CHUNK

    def body(v_ref, gt_ref, pv_ref, pg_ref, cv_ref, ncv_ref, do_ref, ndo_ref, cw_ref, lg_ref, lb_ref,
             dz_ref, dcw_ref, dcb_ref, dlg_ref, dlb_ref, xs_ref, xph_ref, ds_ref, dph_ref):
        i = pl.program_id(0)

        @pl.when(i == 0)
        def _():
            dcw_ref[...] = jnp.zeros_like(dcw_ref)
            dcb_ref[...] = jnp.zeros_like(dcb_ref)
            dlg_ref[...] = jnp.zeros_like(dlg_ref)
            dlb_ref[...] = jnp.zeros_like(dlb_ref)

        prev = pv_ref[...] * _sigmoid(pg_ref[...])
        xs_ref[0:CONV_HALO, :] = jnp.where(i > 0, prev, 0.0)
        xs_ref[CONV_HALO:, :] = v_ref[...] * _sigmoid(gt_ref[...])
        _build_phases(xs_ref, xph_ref, tm + CONV_HALO)

        acc = jnp.concatenate([cv_ref[...], ncv_ref[...]], axis=0)
        mu = jnp.mean(acc, axis=-1, keepdims=True)
        xc = acc - mu
        rstd = lax.rsqrt(jnp.mean(xc * xc, axis=-1, keepdims=True) + LN_EPS)
        xh = xc * rstd
        y = xh * lg_ref[...] + lb_ref[...]
        s = _sigmoid(y)
        dout = jnp.concatenate([do_ref[...], jnp.where(i < n_tiles - 1, ndo_ref[...], 0.0)], axis=0)
        dy = dout * (s * (1.0 + y * (1.0 - s)))
        dxh = dy * lg_ref[...]
        dconv = rstd * (dxh - jnp.mean(dxh, axis=-1, keepdims=True) - xh * jnp.mean(dxh * xh, axis=-1, keepdims=True))
        ds_ref[...] = dconv
        dlg_ref[...] += jnp.sum(dy[:tm] * xh[:tm], axis=0, keepdims=True)
        dlb_ref[...] += jnp.sum(dy[:tm], axis=0, keepdims=True)
        dcb_ref[...] += jnp.sum(dconv[:tm], axis=0, keepdims=True)
        _build_phases(ds_ref, dph_ref, R)

        for l0 in range(0, C, CONV_LANES):
            lanes = slice(l0, l0 + CONV_LANES)

            def taps_bwd(c, wacc, l0=l0, lanes=lanes):
                off = pl.multiple_of(c * CH, CH)
                x_taps = _tap_values(xs_ref, xph_ref, CONV_X_OFFSETS, CH, off, lanes)
                d_taps = _tap_values(ds_ref, dph_ref, CONV_D_OFFSETS, CH, off, lanes)
                dc = ds_ref[pl.ds(off, CH), lanes]
                dglu = jnp.zeros((CH, CONV_LANES), F32)
                new = []
                for k in range(KW):
                    dglu = dglu + cw_ref[k:k + 1, lanes] * d_taps[CONV_D_OFFSETS[k]]
                    prod = dc * x_taps[CONV_X_OFFSETS[k]]
                    new.append(wacc[k] + ((prod[0:8] + prod[8:16]) + (prod[16:24] + prod[24:32])))
                val = v_ref[pl.ds(off, CH), lanes]
                sg = _sigmoid(gt_ref[pl.ds(off, CH), lanes])
                dz_ref[pl.ds(off, CH), lanes] = (dglu * sg).astype(BF16)
                dz_ref[pl.ds(off, CH), C + l0:C + l0 + CONV_LANES] = (dglu * val * sg * (1.0 - sg)).astype(BF16)
                return tuple(new)

            wacc = lax.fori_loop(0, tm // CH, taps_bwd, tuple(jnp.zeros((8, CONV_LANES), F32) for _ in range(KW)))
            for k in range(KW):
                dcw_ref[k:k + 1, lanes] += jnp.sum(wacc[k], axis=0, keepdims=True)

    prev_map = lambda i: jnp.maximum(i * hb - 1, 0)
    next_map = lambda i: jnp.minimum((i + 1) * hb, last_halo)
    return pl.pallas_call(
        body, name=name, grid=(n_tiles,),
        in_specs=[pl.BlockSpec((tm, C), lambda i: (i, 0)),
                  pl.BlockSpec((tm, C), lambda i: (i, 1)),
                  pl.BlockSpec((CONV_HALO, C), lambda i: (prev_map(i), 0)),
                  pl.BlockSpec((CONV_HALO, C), lambda i: (prev_map(i), 1)),
                  pl.BlockSpec((tm, C), lambda i: (i, 0)),
                  pl.BlockSpec((CONV_HALO, C), lambda i: (next_map(i), 0)),
                  pl.BlockSpec((tm, C), lambda i: (i, 0)),
                  pl.BlockSpec((CONV_HALO, C), lambda i: (next_map(i), 0)),
                  pl.BlockSpec((32, C), lambda i: (0, 0)),
                  pl.BlockSpec((1, C), lambda i: (0, 0)),
                  pl.BlockSpec((1, C), lambda i: (0, 0))],
        out_specs=[pl.BlockSpec((tm, 2 * C), lambda i: (i, 0)),
                   pl.BlockSpec((32, C), lambda i: (0, 0)),
                   pl.BlockSpec((1, C), lambda i: (0, 0)),
                   pl.BlockSpec((1, C), lambda i: (0, 0)),
                   pl.BlockSpec((1, C), lambda i: (0, 0))],
        out_shape=[jax.ShapeDtypeStruct((T, 2 * C), BF16),
                   jax.ShapeDtypeStruct((32, C), F32),
                   jax.ShapeDtypeStruct((1, C), F32),
                   jax.ShapeDtypeStruct((1, C), F32),
                   jax.ShapeDtypeStruct((1, C), F32)],
        scratch_shapes=[pltpu.VMEM((tm + CONV_HALO, C), F32), pltpu.VMEM((7, tm + CONV_HALO - 8, C), F32),
                        pltpu.VMEM((R, C), F32), pltpu.VMEM((7, R - 8, C), F32)],
        compiler_params=_params(1),
    )(z, z, z, z, conv_out, conv_out, dm, dm, cw, lg, lb)


def _swa_scores(q_h, kk_h, slope, bias_dist, valid, sink):
    s = _dot_nt(q_h, kk_h) * (HEAD_DIM ** -0.5) - slope * bias_dist
    s = jnp.where(valid, s, NEG_BIG)
    m = jnp.maximum(jnp.max(s, axis=-1, keepdims=True), sink)
    p = jnp.exp(s - m)
    e_sink = jnp.exp(sink - m)
    inv = 1.0 / (jnp.sum(p, axis=-1, keepdims=True) + e_sink)
    return p * inv, e_sink * inv


def _swa_mask(r0):
    qi = lax.broadcasted_iota(jnp.int32, (WINDOW, 2 * WINDOW), 0)
    kj = lax.broadcasted_iota(jnp.int32, (WINDOW, 2 * WINDOW), 1)
    dist = qi + WINDOW - kj
    valid = (dist >= 0) & (dist < WINDOW) & (r0 - WINDOW + kj >= 0)
    return dist.astype(F32), valid


def swa_fwd(z, kpad, vpad, sinks, name, tq=512):
    T = z.shape[0]
    tq = _tile(T, tq)
    HQ = SWA_HEADS * HEAD_DIM

    def body(sink_ref, q_ref, k_ref, v_ref, o_ref):
        i = pl.program_id(0)
        for sub in range(tq // WINDOW):
            r0 = pl.multiple_of(i * tq + sub * WINDOW, WINDOW)
            kk = k_ref[pl.ds(r0, 2 * WINDOW), :]
            vv = v_ref[pl.ds(r0, 2 * WINDOW), :]
            qb = q_ref[sub * WINDOW:(sub + 1) * WINDOW, :].astype(BF16)
            dist, valid = _swa_mask(r0)
            outs = []
            for h in range(SWA_HEADS):
                kh = h // SWA_GROUP
                ks = slice(kh * HEAD_DIM, (kh + 1) * HEAD_DIM)
                pn, _ = _swa_scores(qb[:, h * HEAD_DIM:(h + 1) * HEAD_DIM], kk[:, ks], 2.0 ** (-(h + 1)), dist, valid,
                                    sink_ref[h])
                outs.append(_dot(pn.astype(BF16), vv[:, ks]))
            o_ref[sub * WINDOW:(sub + 1) * WINDOW, :] = jnp.concatenate(outs, axis=-1).astype(BF16)

    return pl.pallas_call(
        body, name=name, grid=(T // tq,),
        in_specs=[pl.BlockSpec(memory_space=pltpu.SMEM),
                  pl.BlockSpec((tq, HQ), lambda i: (i, 2)),
                  pl.BlockSpec((T + WINDOW, 2 * HEAD_DIM), lambda i: (0, 0)),
                  pl.BlockSpec((T + WINDOW, 2 * HEAD_DIM), lambda i: (0, 0))],
        out_specs=pl.BlockSpec((tq, HQ), lambda i: (i, 0)),
        out_shape=jax.ShapeDtypeStruct((T, HQ), BF16),
        compiler_params=_params(1),
    )(sinks, z, kpad, vpad)


def swa_bwd(z, kpad, vpad, sinks, dm, name, tq=512):
    T = z.shape[0]
    tq = _tile(T, tq)
    HQ = SWA_HEADS * HEAD_DIM
    scale = HEAD_DIM ** -0.5

    def body(sink_ref, q_ref, k_ref, v_ref, do_ref, dq_ref, dk_ref, dv_ref, dsink_ref):
        i = pl.program_id(0)

        @pl.when(i == 0)
        def _():
            dk_ref[...] = jnp.zeros_like(dk_ref)
            dv_ref[...] = jnp.zeros_like(dv_ref)
            dsink_ref[...] = jnp.zeros_like(dsink_ref)

        for sub in range(tq // WINDOW):
            r0 = pl.multiple_of(i * tq + sub * WINDOW, WINDOW)
            kk = k_ref[pl.ds(r0, 2 * WINDOW), :]
            vv = v_ref[pl.ds(r0, 2 * WINDOW), :]
            rows = slice(sub * WINDOW, (sub + 1) * WINDOW)
            qb = q_ref[rows, :].astype(BF16)
            dob = do_ref[rows, :].astype(BF16)
            dist, valid = _swa_mask(r0)
            dqs, dks, dvs = [], [], []
            for kh in range(SWA_KV_HEADS):
                ks = slice(kh * HEAD_DIM, (kh + 1) * HEAD_DIM)
                dk_acc = jnp.zeros((2 * WINDOW, HEAD_DIM), F32)
                dv_acc = jnp.zeros((2 * WINDOW, HEAD_DIM), F32)
                for g in range(SWA_GROUP):
                    h = kh * SWA_GROUP + g
                    hs = slice(h * HEAD_DIM, (h + 1) * HEAD_DIM)
                    pn, p_sink = _swa_scores(qb[:, hs], kk[:, ks], 2.0 ** (-(h + 1)), dist, valid, sink_ref[h])
                    dp = _dot_nt(dob[:, hs], vv[:, ks])
                    delta = jnp.sum(pn * dp, axis=-1, keepdims=True)
                    ds = (pn * (dp - delta)).astype(BF16)
                    dqs.append(_dot(ds, kk[:, ks]) * scale)
                    dk_acc = dk_acc + _dot_tn(ds, qb[:, hs]) * scale
                    dv_acc = dv_acc + _dot_tn(pn.astype(BF16), dob[:, hs])
                    dsink_ref[h:h + 1, :] += jnp.zeros((1, 128), F32) - jnp.sum(p_sink * delta)
                dks.append(dk_acc)
                dvs.append(dv_acc)
            dq_ref[rows, :] = jnp.concatenate(dqs, axis=-1).astype(BF16)
            dk_ref[pl.ds(r0, 2 * WINDOW), :] += jnp.concatenate(dks, axis=-1)
            dv_ref[pl.ds(r0, 2 * WINDOW), :] += jnp.concatenate(dvs, axis=-1)

    kv_spec = pl.BlockSpec((T + WINDOW, 2 * HEAD_DIM), lambda i: (0, 0))
    return pl.pallas_call(
        body, name=name, grid=(T // tq,),
        in_specs=[pl.BlockSpec(memory_space=pltpu.SMEM),
                  pl.BlockSpec((tq, HQ), lambda i: (i, 2)),
                  kv_spec, kv_spec,
                  pl.BlockSpec((tq, HQ), lambda i: (i, 1))],
        out_specs=[pl.BlockSpec((tq, HQ), lambda i: (i, 0)),
                   kv_spec, kv_spec,
                   pl.BlockSpec((SWA_HEADS, 128), lambda i: (0, 0))],
        out_shape=[jax.ShapeDtypeStruct((T, HQ), BF16),
                   jax.ShapeDtypeStruct((T + WINDOW, 2 * HEAD_DIM), F32),
                   jax.ShapeDtypeStruct((T + WINDOW, 2 * HEAD_DIM), F32),
                   jax.ShapeDtypeStruct((SWA_HEADS, 128), F32)],
        compiler_params=_params(1),
    )(sinks, z, kpad, vpad, dm)


def short_conv_fwd(z, w, name, tm=512):
    T = z.shape[0]
    C = SC_CH
    tm = _tile(T, tm)
    hb = tm // SC_HALO

    def body(b_ref, c_ref, v_ref, pc_ref, pv_ref, w_ref, o_ref, xs_ref):
        i = pl.program_id(0)
        xs_ref[0:SC_HALO, :] = jnp.where(i > 0, pc_ref[...] * pv_ref[...], 0.0)
        xs_ref[SC_HALO:, :] = c_ref[...] * v_ref[...]
        conv = jnp.zeros((tm, C), F32)
        for k in range(3):
            conv = conv + w_ref[k:k + 1, :] * xs_ref[pl.ds(SC_HALO - 2 + k, tm), :]
        o_ref[...] = (b_ref[...] * conv).astype(BF16)

    prev_map = lambda i: jnp.maximum(i * hb - 1, 0)
    return pl.pallas_call(
        body, name=name, grid=(T // tm,),
        in_specs=[pl.BlockSpec((tm, C), lambda i: (i, 0)),
                  pl.BlockSpec((tm, C), lambda i: (i, 1)),
                  pl.BlockSpec((tm, C), lambda i: (i, 2)),
                  pl.BlockSpec((SC_HALO, C), lambda i: (prev_map(i), 1)),
                  pl.BlockSpec((SC_HALO, C), lambda i: (prev_map(i), 2)),
                  pl.BlockSpec((8, C), lambda i: (0, 0))],
        out_specs=pl.BlockSpec((tm, C), lambda i: (i, 0)),
        out_shape=jax.ShapeDtypeStruct((T, C), BF16),
        scratch_shapes=[pltpu.VMEM((tm + SC_HALO, C), F32)],
        compiler_params=_params(1),
    )(z, z, z, z, z, w)


def short_conv_bwd(z, dm, w, name, tm=512):
    T = z.shape[0]
    C = SC_CH
    tm = _tile(T, tm)
    hb = tm // SC_HALO
    n_tiles = T // tm
    last_halo = T // SC_HALO - 1
    R = tm + SC_HALO

    def body(b_ref, c_ref, v_ref, pc_ref, pv_ref, nb_ref, do_ref, ndo_ref, w_ref, dz_ref, dw_ref, xs_ref, ds_ref):
        i = pl.program_id(0)

        @pl.when(i == 0)
        def _():
            dw_ref[...] = jnp.zeros_like(dw_ref)

        c = c_ref[...]
        val = v_ref[...]
        dout = do_ref[...]
        xs_ref[0:SC_HALO, :] = jnp.where(i > 0, pc_ref[...] * pv_ref[...], 0.0)
        xs_ref[SC_HALO:, :] = c * val
        dconv = dout * b_ref[...]
        ds_ref[0:tm, :] = dconv
        ds_ref[tm:, :] = jnp.where(i < n_tiles - 1, ndo_ref[...] * nb_ref[...], 0.0)
        conv = jnp.zeros((tm, C), F32)
        dcv = jnp.zeros((tm, C), F32)
        for k in range(3):
            xk = xs_ref[pl.ds(SC_HALO - 2 + k, tm), :]
            conv = conv + w_ref[k:k + 1, :] * xk
            dw_ref[k:k + 1, :] += jnp.sum(dconv * xk, axis=0, keepdims=True)
            dcv = dcv + w_ref[k:k + 1, :] * ds_ref[pl.ds(2 - k, tm), :]
        dz_ref[:, 0:C] = (dout * conv).astype(BF16)
        dz_ref[:, C:2 * C] = (dcv * val).astype(BF16)
        dz_ref[:, 2 * C:] = (dcv * c).astype(BF16)

    prev_map = lambda i: jnp.maximum(i * hb - 1, 0)
    next_map = lambda i: jnp.minimum((i + 1) * hb, last_halo)
    return pl.pallas_call(
        body, name=name, grid=(n_tiles,),
        in_specs=[pl.BlockSpec((tm, C), lambda i: (i, 0)),
                  pl.BlockSpec((tm, C), lambda i: (i, 1)),
                  pl.BlockSpec((tm, C), lambda i: (i, 2)),
                  pl.BlockSpec((SC_HALO, C), lambda i: (prev_map(i), 1)),
                  pl.BlockSpec((SC_HALO, C), lambda i: (prev_map(i), 2)),
                  pl.BlockSpec((SC_HALO, C), lambda i: (next_map(i), 0)),
                  pl.BlockSpec((tm, C), lambda i: (i, 0)),
                  pl.BlockSpec((SC_HALO, C), lambda i: (next_map(i), 0)),
                  pl.BlockSpec((8, C), lambda i: (0, 0))],
        out_specs=[pl.BlockSpec((tm, 3 * C), lambda i: (i, 0)),
                   pl.BlockSpec((8, C), lambda i: (0, 0))],
        out_shape=[jax.ShapeDtypeStruct((T, 3 * C), BF16), jax.ShapeDtypeStruct((8, C), F32)],
        scratch_shapes=[pltpu.VMEM((tm + SC_HALO, C), F32), pltpu.VMEM((R, C), F32)],
        compiler_params=_params(1),
    )(z, z, z, z, z, z, dm, dm, w)


def _xa_probs(q_h, k_h):
    s = _dot_nt(q_h, k_h) * (XA_HEAD_DIM ** -0.5)
    p = jnp.exp(s - jnp.max(s, axis=-1, keepdims=True))
    return p * (1.0 / jnp.sum(p, axis=-1, keepdims=True))


def xattn_fwd(q, kv, name, tm=512):
    T = q.shape[0]
    M = kv.shape[0]
    tm = _tile(T, tm)

    def body(q_ref, k_ref, v_ref, o_ref):
        for h in range(XA_HEADS):
            hs = slice(h * XA_HEAD_DIM, (h + 1) * XA_HEAD_DIM)
            p = _xa_probs(q_ref[:, hs], k_ref[:, hs])
            o_ref[:, hs] = _dot(p.astype(BF16), v_ref[:, hs]).astype(BF16)

    return pl.pallas_call(
        body, name=name, grid=(T // tm,),
        in_specs=[pl.BlockSpec((tm, D_MODEL), lambda i: (i, 0)),
                  pl.BlockSpec((M, D_MODEL), lambda i: (0, 0)),
                  pl.BlockSpec((M, D_MODEL), lambda i: (0, 1))],
        out_specs=pl.BlockSpec((tm, D_MODEL), lambda i: (i, 0)),
        out_shape=jax.ShapeDtypeStruct((T, D_MODEL), BF16),
        compiler_params=_params(1),
    )(q, kv, kv)


def xattn_bwd(q, kv, do, name, tm=512):
    T = q.shape[0]
    M = kv.shape[0]
    tm = _tile(T, tm)
    scale = XA_HEAD_DIM ** -0.5

    def body(q_ref, k_ref, v_ref, do_ref, dq_ref, dkv_ref):
        @pl.when(pl.program_id(0) == 0)
        def _():
            dkv_ref[...] = jnp.zeros_like(dkv_ref)

        for h in range(XA_HEADS):
            hs = slice(h * XA_HEAD_DIM, (h + 1) * XA_HEAD_DIM)
            vs = slice(D_MODEL + h * XA_HEAD_DIM, D_MODEL + (h + 1) * XA_HEAD_DIM)
            q_h = q_ref[:, hs]
            do_h = do_ref[:, hs]
            p = _xa_probs(q_h, k_ref[:, hs])
            dp = _dot_nt(do_h, v_ref[:, hs])
            ds = (p * (dp - jnp.sum(p * dp, axis=-1, keepdims=True))).astype(BF16)
            dq_ref[:, hs] = (_dot(ds, k_ref[:, hs]) * scale).astype(BF16)
            dkv_ref[:, hs] += _dot_tn(ds, q_h) * scale
            dkv_ref[:, vs] += _dot_tn(p.astype(BF16), do_h)

    return pl.pallas_call(
        body, name=name, grid=(T // tm,),
        in_specs=[pl.BlockSpec((tm, D_MODEL), lambda i: (i, 0)),
                  pl.BlockSpec((M, D_MODEL), lambda i: (0, 0)),
                  pl.BlockSpec((M, D_MODEL), lambda i: (0, 1)),
                  pl.BlockSpec((tm, D_MODEL), lambda i: (i, 0))],
        out_specs=[pl.BlockSpec((tm, D_MODEL), lambda i: (i, 0)),
                   pl.BlockSpec((M, 2 * D_MODEL), lambda i: (0, 0))],
        out_shape=[jax.ShapeDtypeStruct((T, D_MODEL), BF16), jax.ShapeDtypeStruct((M, 2 * D_MODEL), F32)],
        compiler_params=_params(1),
    )(q, kv, kv, do)


def final_loss(h, g, target, name, tm=512):
    T, K = h.shape
    tm = _tile(T, tm)

    def body(h_ref, g_ref, t_ref, dh_ref, dg_ref, loss_ref):
        @pl.when(pl.program_id(0) == 0)
        def _():
            dg_ref[...] = jnp.zeros_like(dg_ref)
            loss_ref[...] = jnp.zeros_like(loss_ref)

        x = h_ref[...]
        r = lax.rsqrt(jnp.mean(x * x, axis=-1, keepdims=True) + RMS_EPS)
        xh = x * r
        e = xh * g_ref[...] - t_ref[...]
        loss_ref[...] += jnp.zeros((1, 128), F32) + 0.5 * jnp.sum(jnp.mean(e * e, axis=-1, keepdims=True))
        dy = e * (1.0 / K)
        dg_ref[...] += jnp.sum(dy * xh, axis=0, keepdims=True)
        dxh = dy * g_ref[...]
        dh_ref[...] = r * (dxh - xh * jnp.mean(dxh * xh, axis=-1, keepdims=True))

    return pl.pallas_call(
        body, name=name, grid=(T // tm,),
        in_specs=[pl.BlockSpec((tm, K), lambda i: (i, 0)),
                  pl.BlockSpec((1, K), lambda i: (0, 0)),
                  pl.BlockSpec((tm, K), lambda i: (i, 0))],
        out_specs=[pl.BlockSpec((tm, K), lambda i: (i, 0)),
                   pl.BlockSpec((1, K), lambda i: (0, 0)),
                   pl.BlockSpec((1, 128), lambda i: (0, 0))],
        out_shape=[jax.ShapeDtypeStruct((T, K), F32), jax.ShapeDtypeStruct((1, K), F32),
                   jax.ShapeDtypeStruct((1, 128), F32)],
        compiler_params=_params(1),
    )(h, g, target)


def _row(v):
    return v.reshape(1, -1)


def _pad_rows(a, rows):
    return jnp.pad(a, ((0, rows - a.shape[0]), (0, 0)))


def local_step(x, mem, target, P, get_weights, put_grads):
    cw = _pad_rows(P["conv_a_w"], 32)
    scw = _pad_rows(P["sc_conv_w"], 8)
    cb, lg, lb = _row(P["conv_a_b"]), _row(P["conv_a_ln_g"]), _row(P["conv_a_ln_b"])
    sinks = P["swa_sinks"]

    class _Layered:
        def __init__(self, store, name=None):
            self.store, self.name = store, name

        def __getitem__(self, key):
            if self.name is None:
                return self.store[(key, 0)] if key in ("even_w_in", "even_w_out", "odd_w_in", "odd_w_out") \
                    else _Layered(self.store, key)
            return self.store[(self.name, key)]

    store = {}
    W = _Layered(store)
    saved = []
    h = x
    for i in range(2):
        L = f"l{i}"
        new, dep = get_weights("A" if i == 0 else "D", h)
        store.update(new)
        h, s_ffn1 = ffn_forward(h, P["ffn1_norm"][i:i + 1], W["ffn1_w_gu"][i], W["ffn1_w_down"][i], L + "_ffn1", dep=dep)
        h1 = h
        if i == 0:
            new, _ = get_weights("B", h)
            store.update(new)
            u2 = rmsnorm(h1, P["mix_norm"][i:i + 1], L + "_mix_in_norm")
            z = matmul(u2, W["even_w_in"], F32, L + "_mix_in", 768, n_tiles=2)
            kv = matmul(u2, W["even_w_in"], BF16, L + "_mix_kv", 256, n_tiles=1, first_tile=6)
            a, conv_out = conformer_conv_fwd(z, cw, cb, lg, lb, L + "_conv")
            kpad = jnp.pad(kv[:, :2 * HEAD_DIM], ((WINDOW, 0), (0, 0)))
            vpad = jnp.pad(kv[:, 2 * HEAD_DIM:], ((WINDOW, 0), (0, 0)))
            o = swa_fwd(z, kpad, vpad, sinks, L + "_swa")
            m = jnp.concatenate([a, o], axis=-1)
            h = matmul_residual(m, W["even_w_out"], h1, L + "_mix_out")
            s_mix = (h1, u2, z, m, kpad, vpad, conv_out)
        else:
            z, u2 = norm_matmul(h1, P["mix_norm"][i:i + 1], W["odd_w_in"], F32, L + "_mix_in", 1024)
            m = short_conv_fwd(z, scw, L + "_sconv")
            h = matmul_residual(m, W["odd_w_out"], h1, L + "_mix_out")
            s_mix = (h1, u2, z, m)
        h2 = h
        kv, umem = norm_matmul(mem, P["xa_mem_norm"][i:i + 1], W["xa_wkv"][i], BF16, L + "_xa_kv", 2 * D_MODEL)
        q, u3 = norm_matmul(h2, P["xa_norm"][i:i + 1], W["xa_wq"][i], BF16, L + "_xa_q", D_MODEL, transposed=False)
        o = xattn_fwd(q, kv, L + "_xa")
        h = matmul_residual(o, W["xa_wo"][i], h2, L + "_xa_out")
        s_xa = (h2, u3, q, o, kv, umem)
        if i == 0:
            new, _ = get_weights("C", h)
            store.update(new)
        h, s_ffn2 = ffn_forward(h, P["ffn2_norm"][i:i + 1], W["ffn2_w_gu"][i], W["ffn2_w_down"][i], L + "_ffn2")
        saved.append((s_ffn1, s_mix, s_xa, s_ffn2))

    dh, d_final, loss = final_loss(h, _row(P["final_norm"]), target, "final_loss")

    names = ("ffn1_w_gu", "ffn1_w_down", "ffn2_w_gu", "ffn2_w_down", "xa_wq", "xa_wkv", "xa_wo", "even_w_in", "even_w_out",
             "odd_w_in", "odd_w_out")
    dW = {k: [None, None] for k in names}
    dP = {k: [None, None] for k in ("ffn1_norm", "mix_norm", "xa_norm", "xa_mem_norm", "ffn2_norm")}
    dP["final_norm"] = d_final.reshape(-1)
    for i in (1, 0):
        L = f"l{i}b"
        s_ffn1, s_mix, s_xa, s_ffn2 = saved[i]

        def keep_ffn2(d_w_gu, d_w_down, i=i):
            dW["ffn2_w_gu"][i], dW["ffn2_w_down"][i] = d_w_gu, d_w_down

        def send_ffn1(d_w_gu, d_w_down, i=i):
            dW["ffn1_w_gu"][i], dW["ffn1_w_down"][i] = d_w_gu, d_w_down
            stage = "D" if i == 1 else "A"
            return put_grads(stage, {k: dW[k[0]][k[1]] for k in STAGE_KEYS[stage]})

        dh, dP["ffn2_norm"][i] = ffn_backward(
            dh, s_ffn2, P["ffn2_norm"][i:i + 1], W["ffn2_w_gu"][i], W["ffn2_w_down"][i], L + "_ffn2", emit=keep_ffn2)
        h2, u3, q, o, kv, umem = s_xa
        dW["xa_wo"][i] = matmul_tn(o, dh, L + "_xa_dwo")
        do = matmul_nt(dh, W["xa_wo"][i], BF16, L + "_xa_do")
        dq, dkv = xattn_bwd(q, kv, do, L + "_xa")
        dW["xa_wq"][i] = matmul_tn(u3, dq, L + "_xa_dwq")
        dW["xa_wkv"][i] = matmul_tn(dkv, umem, L + "_xa_dwkv", tk=1024)
        dkv_b = dkv.astype(BF16)
        _, dP["xa_mem_norm"][i] = matmul_norm_bwd(dkv_b, W["xa_wkv"][i], mem, P["xa_mem_norm"][i:i + 1],
                                                  jnp.zeros_like(mem), L + "_xa_dmem")
        dh, dP["xa_norm"][i] = matmul_norm_bwd(dq, W["xa_wq"][i], h2, P["xa_norm"][i:i + 1], dh, L + "_xa_dx",
                                               transposed=False)
        if i == 0:
            h1, u2, z, m, kpad, vpad, conv_out = s_mix
            dW["even_w_out"][0] = matmul_tn(m, dh, L + "_mix_dwo")
            dm = matmul_nt(dh, W["even_w_out"], F32, L + "_mix_dm")
            dz_conv, dcw, dcb, dlg, dlb = conformer_conv_bwd(z, conv_out, dm, cw, lg, lb, L + "_conv")
            dq_s, dkp, dvp, dsk = swa_bwd(z, kpad, vpad, sinks, dm, L + "_swa")
            dz = jnp.concatenate([dz_conv, dq_s, dkp[WINDOW:].astype(BF16), dvp[WINDOW:].astype(BF16)], axis=-1)
            dW["even_w_in"][0] = matmul_tn(dz, u2, L + "_mix_dwi", tk=896)
            dh, dP["mix_norm"][i] = matmul_norm_bwd(dz, W["even_w_in"], h1, P["mix_norm"][i:i + 1], dh, L + "_mix_dx")
            dP["conv_a_w"] = dcw[:CONV_A_WIDTH]
            dP["conv_a_b"], dP["conv_a_ln_g"], dP["conv_a_ln_b"] = dcb.reshape(-1), dlg.reshape(-1), dlb.reshape(-1)
            dP["swa_sinks"] = dsk[:, 0]
        else:
            h1, u2, z, m = s_mix
            dW["odd_w_out"][0] = matmul_tn(m, dh, L + "_mix_dwo")
            dm = matmul_nt(dh, W["odd_w_out"], F32, L + "_mix_dm")
            dz, dscw = short_conv_bwd(z, dm, scw, L + "_sconv")
            dW["odd_w_in"][0] = matmul_tn(dz, u2, L + "_mix_dwi", tk=1024)
            dh, dP["mix_norm"][i] = matmul_norm_bwd(dz, W["odd_w_in"], h1, P["mix_norm"][i:i + 1], dh, L + "_mix_dx")
            dP["sc_conv_w"] = dscw[:3]
        dep = put_grads("BC", {k: dW[k[0]][k[1]] for k in STAGE_KEYS["B"] + STAGE_KEYS["C"]}) if i == 0 else None
        dh, dP["ffn1_norm"][i] = ffn_backward(
            dh, s_ffn1, P["ffn1_norm"][i:i + 1], W["ffn1_w_gu"][i], W["ffn1_w_down"][i], L + "_ffn1", dep=dep,
            emit=send_ffn1)
    for k in ("ffn1_norm", "mix_norm", "xa_norm", "xa_mem_norm", "ffn2_norm"):
        dP[k] = jnp.concatenate(dP[k], axis=0)
    return loss, dh, dP


def _mesh_pos():
    return lax.axis_index("x"), lax.axis_index("y"), lax.axis_index("c")


def _flat_index(px, py, pc):
    return 4 * px + 2 * py + pc


def all_gather(blob, name, dep=None):
    R, C = blob.shape

    def kern(x_ref, out_ref, send_sems, recv_sems, local_sem):
        x, y, c = _mesh_pos()
        me, sibling = (x, y, c), (x, y, 1 - c)
        chips = [(1 - x, y), (x, 1 - y), (1 - x, 1 - y)]

        def slot(px, py, pc):
            return out_ref.at[_flat_index(px, py, pc)]

        def copy(k, block, to, src=None):
            return pltpu.make_async_remote_copy(
                src_ref=slot(*block) if src is None else src, dst_ref=slot(*block),
                send_sem=send_sems.at[k], recv_sem=recv_sems.at[k],
                device_id=to, device_id_type=pl.DeviceIdType.MESH)

        mine = pltpu.make_async_copy(x_ref, slot(*me), local_sem)
        mine.start()
        first = [copy(0, me, sibling, src=x_ref)]
        first += [copy(1 + j, me, (*chip, c), src=x_ref) for j, chip in enumerate(chips)]
        for cp in first:
            cp.start()
        passed = [copy(4 + j, (*chip, c), sibling) for j, chip in enumerate(chips)]
        for j, chip in enumerate(chips):
            copy(1 + j, (*chip, c), me).wait_recv()
            passed[j].start()
        copy(0, sibling, me).wait_recv()
        for j, chip in enumerate(chips):
            copy(4 + j, (*chip, 1 - c), me).wait_recv()
        for cp in first + passed:
            cp.wait_send()
        mine.wait()

    body, dep_spec, dep_arg = _with_dep(kern, 1, dep)
    return pl.pallas_call(
        body, name=name,
        out_shape=jax.ShapeDtypeStruct((N_DEV, R, C), blob.dtype),
        in_specs=[ANY_SPEC] + dep_spec,
        out_specs=ANY_SPEC,
        scratch_shapes=[pltpu.SemaphoreType.DMA((7,)), pltpu.SemaphoreType.DMA((7,)), pltpu.SemaphoreType.DMA],
    )(blob, *dep_arg)


def scatter_exchange(g, name):
    _, R, C = g.shape

    def body(g_ref, out_ref, send_sems, recv_sems, local_sem):
        x, y, c = _mesh_pos()
        me_idx = _flat_index(x, y, c)
        mine = pltpu.make_async_copy(g_ref.at[me_idx], out_ref.at[me_idx], local_sem)
        mine.start()
        sends, peers = [], []
        for k in range(1, N_DEV):
            px = 1 - x if k & 4 else x
            py = 1 - y if k & 2 else y
            pc = 1 - c if k & 1 else c
            peer_idx = _flat_index(px, py, pc)
            cp = pltpu.make_async_remote_copy(
                src_ref=g_ref.at[peer_idx], dst_ref=out_ref.at[me_idx],
                send_sem=send_sems.at[k - 1], recv_sem=recv_sems.at[k - 1],
                device_id=(px, py, pc), device_id_type=pl.DeviceIdType.MESH)
            cp.start()
            sends.append(cp)
            peers.append((peer_idx, (px, py, pc)))
        for k in range(1, N_DEV):
            peer_idx, peer = peers[k - 1]
            pltpu.make_async_remote_copy(
                src_ref=g_ref.at[me_idx], dst_ref=out_ref.at[peer_idx],
                send_sem=send_sems.at[k - 1], recv_sem=recv_sems.at[k - 1],
                device_id=peer, device_id_type=pl.DeviceIdType.MESH).wait_recv()
        for cp in sends:
            cp.wait_send()
        mine.wait()

    return pl.pallas_call(
        body, name=name,
        out_shape=jax.ShapeDtypeStruct(g.shape, g.dtype),
        in_specs=[pl.BlockSpec(memory_space=pl.ANY)],
        out_specs=pl.BlockSpec(memory_space=pl.ANY),
        scratch_shapes=[pltpu.SemaphoreType.DMA((7,)), pltpu.SemaphoreType.DMA((7,)), pltpu.SemaphoreType.DMA],
    )(g)


HBM_SPEC = pl.BlockSpec(memory_space=pltpu.HBM)
SEM_SPEC = pl.BlockSpec(memory_space=pltpu.SEMAPHORE)
DATAFLOW_EFFECT = pltpu.SideEffectType.DATAFLOW_SIDE_EFFECTING


def _peers(x, y, c):
    out = []
    for k in range(1, N_DEV):
        pos = (1 - x if k & 4 else x, 1 - y if k & 2 else y, 1 - c if k & 1 else c)
        out.append((_flat_index(*pos), pos))
    return out


def _exchange_copy(src_ref, land_ref, send_sems, recv_sems, j, me, peer_idx, peer, scatter):
    return pltpu.make_async_remote_copy(
        src_ref=src_ref.at[peer_idx] if scatter else src_ref, dst_ref=land_ref.at[me],
        send_sem=send_sems.at[j], recv_sem=recv_sems.at[j], device_id=peer, device_id_type=pl.DeviceIdType.MESH)


def exchange_start(srcs, lands, scatter, after, name):
    n = len(srcs)
    n_after = len(after)

    def body(*refs):
        src_refs, land_refs = refs[:n], refs[n:2 * n]
        outs = refs[2 * n + n_after:]
        send_sems, recv_sems, token = outs[:n], outs[n:2 * n], outs[4 * n]
        x, y, c = _mesh_pos()
        me = _flat_index(x, y, c)
        for g in range(n):
            for j, (peer_idx, peer) in enumerate(_peers(x, y, c)):
                _exchange_copy(src_refs[g], land_refs[g], send_sems[g], recv_sems[g], j, me, peer_idx, peer, scatter).start()
        token[...] = jnp.zeros_like(token)

    hbm = lambda a: pltpu.with_memory_space_constraint(a, pltpu.HBM)
    res = pl.pallas_call(
        body, name=name,
        out_shape=(*[pltpu.SemaphoreType.DMA((N_DEV - 1,))] * (2 * n),
                   *[pltpu.HBM(a.shape, a.dtype) for a in srcs], *[pltpu.HBM(a.shape, a.dtype) for a in lands],
                   jax.ShapeDtypeStruct((8, 128), F32)),
        in_specs=[HBM_SPEC] * (2 * n) + [ANY_SPEC] * n_after,
        out_specs=(*[SEM_SPEC] * (2 * n), *[HBM_SPEC] * (2 * n), pl.BlockSpec(memory_space=pltpu.VMEM)),
        input_output_aliases={i: 2 * n + i for i in range(2 * n)},
        compiler_params=pltpu.CompilerParams(has_side_effects=DATAFLOW_EFFECT),
    )(*[hbm(a) for a in srcs], *[hbm(a) for a in lands], *after)
    handles = [(res[g], res[n + g], res[2 * n + g], res[3 * n + g]) for g in range(n)]
    return handles, res[4 * n]


def exchange_wait(handles, scatter, after, name):
    n = len(handles)

    def body(*refs):
        src_refs, land_refs = refs[:n], refs[n:2 * n]
        send_sems, recv_sems = refs[2 * n:3 * n], refs[3 * n:4 * n]
        x, y, c = _mesh_pos()
        me = _flat_index(x, y, c)
        for g in range(n):
            for j, (peer_idx, peer) in enumerate(_peers(x, y, c)):
                mine = _exchange_copy(src_refs[g], land_refs[g], send_sems[g], recv_sems[g], j, me, peer_idx, peer,
                                      scatter)
                mine.wait_send()
                theirs = pltpu.make_async_remote_copy(
                    src_ref=src_refs[g].at[me] if scatter else src_refs[g], dst_ref=land_refs[g].at[peer_idx],
                    send_sem=send_sems[g].at[j], recv_sem=recv_sems[g].at[j], device_id=peer,
                    device_id_type=pl.DeviceIdType.MESH)
                theirs.wait_recv()

    srcs = [h[2] for h in handles]
    lands = [h[3] for h in handles]
    res = pl.pallas_call(
        body, name=name,
        out_shape=tuple(pltpu.HBM(a.shape, a.dtype) for a in srcs + lands),
        in_specs=[HBM_SPEC] * (2 * n) + [SEM_SPEC] * (2 * n) + [ANY_SPEC],
        out_specs=tuple([HBM_SPEC] * (2 * n)),
        input_output_aliases={i: i for i in range(2 * n)},
        compiler_params=pltpu.CompilerParams(has_side_effects=DATAFLOW_EFFECT),
    )(*srcs, *lands, *[h[0] for h in handles], *[h[1] for h in handles], after)
    return [(res[g], res[n + g]) for g in range(n)]


def ordered_sum(parts, name, tr=512):
    n, R, C = parts.shape
    tr = next((t for t in range(min(tr, R), 15, -16) if R % t == 0), R)

    def body(p_ref, o_ref):
        acc = p_ref[0].astype(F32)
        for j in range(1, n):
            acc = acc + p_ref[j].astype(F32)
        o_ref[...] = acc

    return pl.pallas_call(
        body, name=name, grid=(R // tr,),
        in_specs=[pl.BlockSpec((n, tr, C), lambda i: (0, i, 0))],
        out_specs=pl.BlockSpec((tr, C), lambda i: (i, 0)),
        out_shape=jax.ShapeDtypeStruct((R, C), F32),
        compiler_params=_params(1),
    )(parts)


def adamw(w, g, m, v, name, tr=256):
    R, C = w.shape
    tr = next((t for t in range(tr, 7, -8) if R % t == 0), R)
    c1 = 1.0 - ADAM_B1 ** ADAM_STEP
    c2 = 1.0 - ADAM_B2 ** ADAM_STEP

    def body(w_ref, g_ref, m_ref, v_ref, d_ref, mo_ref, vo_ref):
        grad = g_ref[...]
        m2 = ADAM_B1 * m_ref[...] + (1.0 - ADAM_B1) * grad
        v2 = ADAM_B2 * v_ref[...] + (1.0 - ADAM_B2) * (grad * grad)
        mo_ref[...] = m2
        vo_ref[...] = v2
        d_ref[...] = -ADAM_LR * ((m2 / c1) / (jnp.sqrt(v2 / c2) + ADAM_EPS) + ADAM_WD * w_ref[...])

    spec = pl.BlockSpec((tr, C), lambda i: (i, 0))
    return pl.pallas_call(
        body, name=name, grid=(R // tr,),
        in_specs=[spec] * 4, out_specs=[spec] * 3,
        out_shape=[jax.ShapeDtypeStruct((R, C), F32)] * 3,
        compiler_params=_params(1),
    )(w, g, m, v)


WEIGHT_NAMES = ("ffn1_norm", "ffn1_w_gu", "ffn1_w_down", "mix_norm", "even_w_in", "conv_a_w", "conv_a_b", "conv_a_ln_g",
                "conv_a_ln_b", "swa_sinks", "even_w_out", "odd_w_in", "sc_conv_w", "odd_w_out", "xa_norm", "xa_mem_norm",
                "xa_wq", "xa_wkv", "xa_wo", "ffn2_norm", "ffn2_w_gu", "ffn2_w_down", "final_norm")
BLOB_COLS = 1024
SMALL_ROWS = (("ffn1_norm", 0, 2), ("mix_norm", 2, 2), ("xa_norm", 4, 2), ("xa_mem_norm", 6, 2), ("ffn2_norm", 8, 2),
              ("final_norm", 10, 1))
ROW_CONV_B_LNG = 11
ROW_LNB_SINKS_LOSS = 12
LOSS_COL = 512 + SWA_HEADS
ROW_SC_CONV = 13
ROW_CONV_W = 16
SMALL_BLOB_ROWS = 32
SMALL_ADAM_ROWS = 16


def _small_blob(v):
    rows = [v[n].reshape(-1, D_MODEL) for n, _, _ in SMALL_ROWS]
    rows.append(jnp.concatenate([v["conv_a_b"].reshape(-1), v["conv_a_ln_g"].reshape(-1)]).reshape(1, D_MODEL))
    tail = jnp.zeros((D_MODEL - 512 - SWA_HEADS,), F32)
    if "loss" in v:
        tail = tail.at[0].set(v["loss"])
    rows.append(jnp.concatenate([v["conv_a_ln_b"].reshape(-1), v["swa_sinks"].reshape(-1), tail]).reshape(1, D_MODEL))
    rows.append(jnp.zeros((SMALL_ADAM_ROWS - ROW_SC_CONV, D_MODEL), F32))
    return jnp.concatenate(rows, axis=0)


def _small_unblob(b, shapes):
    out = {n: b[r:r + k].reshape(shapes[n]) for n, r, k in SMALL_ROWS}
    out["conv_a_b"] = b[ROW_CONV_B_LNG, :512].reshape(shapes["conv_a_b"])
    out["conv_a_ln_g"] = b[ROW_CONV_B_LNG, 512:].reshape(shapes["conv_a_ln_g"])
    out["conv_a_ln_b"] = b[ROW_LNB_SINKS_LOSS, :512].reshape(shapes["conv_a_ln_b"])
    out["swa_sinks"] = b[ROW_LNB_SINKS_LOSS, 512:512 + SWA_HEADS].reshape(shapes["swa_sinks"])
    return out


def kernel(x, mem, ffn1_norm, ffn1_w_gu, ffn1_w_down, mix_norm, even_w_in, conv_a_w, conv_a_b, conv_a_ln_g, conv_a_ln_b, swa_sinks, even_w_out, odd_w_in, sc_conv_w, odd_w_out, xa_norm, xa_mem_norm, xa_wq, xa_wkv, xa_wo, ffn2_norm, ffn2_w_gu, ffn2_w_down, final_norm, loss_target, m_ffn1_norm, m_ffn1_w_gu, m_ffn1_w_down, m_mix_norm, m_even_w_in, m_conv_a_w, m_conv_a_b, m_conv_a_ln_g, m_conv_a_ln_b, m_swa_sinks, m_even_w_out, m_odd_w_in, m_sc_conv_w, m_odd_w_out, m_xa_norm, m_xa_mem_norm, m_xa_wq, m_xa_wkv, m_xa_wo, m_ffn2_norm, m_ffn2_w_gu, m_ffn2_w_down, m_final_norm, v_ffn1_norm, v_ffn1_w_gu, v_ffn1_w_down, v_mix_norm, v_even_w_in, v_conv_a_w, v_conv_a_b, v_conv_a_ln_g, v_conv_a_ln_b, v_swa_sinks, v_even_w_out, v_odd_w_in, v_sc_conv_w, v_odd_w_out, v_xa_norm, v_xa_mem_norm, v_xa_wq, v_xa_wkv, v_xa_wo, v_ffn2_norm, v_ffn2_w_gu, v_ffn2_w_down, v_final_norm):
    w = dict(ffn1_norm=ffn1_norm, ffn1_w_gu=ffn1_w_gu, ffn1_w_down=ffn1_w_down, mix_norm=mix_norm, even_w_in=even_w_in,
             conv_a_w=conv_a_w, conv_a_b=conv_a_b, conv_a_ln_g=conv_a_ln_g, conv_a_ln_b=conv_a_ln_b, swa_sinks=swa_sinks,
             even_w_out=even_w_out, odd_w_in=odd_w_in, sc_conv_w=sc_conv_w, odd_w_out=odd_w_out, xa_norm=xa_norm,
             xa_mem_norm=xa_mem_norm, xa_wq=xa_wq, xa_wkv=xa_wkv, xa_wo=xa_wo, ffn2_norm=ffn2_norm, ffn2_w_gu=ffn2_w_gu,
             ffn2_w_down=ffn2_w_down, final_norm=final_norm)
    m = dict(ffn1_norm=m_ffn1_norm, ffn1_w_gu=m_ffn1_w_gu, ffn1_w_down=m_ffn1_w_down, mix_norm=m_mix_norm,
             even_w_in=m_even_w_in, conv_a_w=m_conv_a_w, conv_a_b=m_conv_a_b, conv_a_ln_g=m_conv_a_ln_g,
             conv_a_ln_b=m_conv_a_ln_b, swa_sinks=m_swa_sinks, even_w_out=m_even_w_out, odd_w_in=m_odd_w_in,
             sc_conv_w=m_sc_conv_w, odd_w_out=m_odd_w_out, xa_norm=m_xa_norm, xa_mem_norm=m_xa_mem_norm, xa_wq=m_xa_wq,
             xa_wkv=m_xa_wkv, xa_wo=m_xa_wo, ffn2_norm=m_ffn2_norm, ffn2_w_gu=m_ffn2_w_gu, ffn2_w_down=m_ffn2_w_down,
             final_norm=m_final_norm)
    v = dict(ffn1_norm=v_ffn1_norm, ffn1_w_gu=v_ffn1_w_gu, ffn1_w_down=v_ffn1_w_down, mix_norm=v_mix_norm,
             even_w_in=v_even_w_in, conv_a_w=v_conv_a_w, conv_a_b=v_conv_a_b, conv_a_ln_g=v_conv_a_ln_g,
             conv_a_ln_b=v_conv_a_ln_b, swa_sinks=v_swa_sinks, even_w_out=v_even_w_out, odd_w_in=v_odd_w_in,
             sc_conv_w=v_sc_conv_w, odd_w_out=v_odd_w_out, xa_norm=v_xa_norm, xa_mem_norm=v_xa_mem_norm, xa_wq=v_xa_wq,
             xa_wkv=v_xa_wkv, xa_wo=v_xa_wo, ffn2_norm=v_ffn2_norm, ffn2_w_gu=v_ffn2_w_gu, ffn2_w_down=v_ffn2_w_down,
             final_norm=v_final_norm)
    me = _flat_index(*_mesh_pos())

    conv_blob = jnp.concatenate([w["conv_a_w"].reshape(-1), w["sc_conv_w"].reshape(-1),
                                 jnp.zeros((8 * 1024 - 31 * 64 - 3 * 128,), F32)]).reshape(8, 1024)
    conv_all = all_gather(conv_blob, "gather_conv_weights").reshape(N_DEV, 8 * 1024)
    conv_a_full = jnp.transpose(conv_all[:, :31 * 64].reshape(N_DEV, 31, 64), (1, 0, 2)).reshape(31, 512)
    sc_full = jnp.transpose(conv_all[:, 31 * 64:31 * 64 + 3 * 128].reshape(N_DEV, 3, 128), (1, 0, 2)).reshape(3, 1024)

    def shard_rows(n, l):
        return (w[n][l].T if SPLIT_AXIS[n] == 1 else w[n][l]).astype(BF16)

    def with_own(land, own):
        return lax.dynamic_update_slice(land, own[None], (me, 0, 0))

    a_keys = STAGE_KEYS["A"]
    gathered_a = all_gather(jnp.concatenate([shard_rows(n, l) for n, l in a_keys], axis=0), "gather_weights_a")
    later = ("B", "C", "D")
    later_keys = [k for s in later for k in STAGE_KEYS[s]]
    shards = [shard_rows(n, l) for n, l in later_keys]
    lands = [lax.empty((N_DEV,) + s.shape, BF16) for s in shards]
    handles, weight_token = exchange_start(shards, lands, False, [gathered_a, conv_all], "gather_start")
    weight_handles = dict(zip(later_keys, handles))

    def get_weights(stage, after):
        keys = STAGE_KEYS[stage]
        if stage == "A":
            out, off = {}, 0
            for n, l in keys:
                rows = w[n].shape[2] if SPLIT_AXIS[n] == 1 else w[n].shape[1]
                out[(n, l)] = gathered_a[:, off:off + rows, :].reshape(N_DEV * rows, BLOB_COLS)
                off += rows
            return out, weight_token
        got = exchange_wait([weight_handles[k] for k in keys], False, after, "gather_wait_" + stage.lower())
        return {k: with_own(land, own).reshape(-1, BLOB_COLS) for k, (own, land) in zip(keys, got)}, None

    grad_handles = {}

    def put_grads(stage, dws):
        srcs = [dw.reshape(N_DEV, -1, BLOB_COLS) for dw in dws.values()]
        handles, token = exchange_start(srcs, [lax.empty(s.shape, BF16) for s in srcs], True, [],
                                        "scatter_start_" + stage.lower())
        grad_handles[stage] = (handles, tuple(dws))
        return token

    P = dict(ffn1_norm=ffn1_norm, mix_norm=mix_norm, xa_norm=xa_norm, xa_mem_norm=xa_mem_norm, ffn2_norm=ffn2_norm,
             final_norm=final_norm, conv_a_w=conv_a_full, conv_a_b=conv_a_b[0], conv_a_ln_g=conv_a_ln_g[0],
             conv_a_ln_b=conv_a_ln_b[0], swa_sinks=swa_sinks[0], sc_conv_w=sc_full)

    loss_part, grad_x, dP = local_step(x[0], mem[0], loss_target[0], P, get_weights, put_grads)

    def finish_grads(stage, after):
        handles, keys = grad_handles[stage]
        got = exchange_wait(handles, True, after, "scatter_wait_" + stage.lower())
        out = {}
        for (n, l), (src, land) in zip(keys, got):
            own = lax.dynamic_slice(src, (me, 0, 0), (1,) + src.shape[1:])[0]
            part = ordered_sum(with_own(land, own), f"sum_grads_{n}_{l}")
            out[(n, l)] = part.T if SPLIT_AXIS[n] == 1 else part
        return out

    layer_grads = {**finish_grads("D", grad_x), **finish_grads("BC", grad_x)}

    grads, delta, new_m, new_v = {}, {}, {}, {}

    def update(n):
        shp = w[n].shape
        two_d = (shp[0] * shp[1], shp[2])
        d_, m_, v_ = adamw(w[n].reshape(two_d), grads[n].reshape(two_d), m[n].reshape(two_d), v[n].reshape(two_d),
                           "adamw_" + n)
        delta[n], new_m[n], new_v[n] = d_.reshape(shp), m_.reshape(shp), v_.reshape(shp)

    first_stage = tuple(n for n, _ in STAGE_KEYS["A"])
    for n in SPLIT_AXIS:
        if n not in first_stage:
            grads[n] = jnp.stack([layer_grads[(n, l)] for l in range(w[n].shape[0])], axis=0)
            update(n)
    layer_grads.update(finish_grads("A", delta["ffn2_w_gu"]))

    dP = dict(dP, loss=loss_part[0, 0])
    small = jnp.concatenate([
        _small_blob(dP)[:ROW_SC_CONV], dP["sc_conv_w"],
        jnp.concatenate([dP["conv_a_w"].reshape(-1), jnp.zeros((512,), F32)]).reshape(16, D_MODEL)], axis=0)
    small_all = all_gather(small, "gather_small_grads", dep=layer_grads[STAGE_KEYS["A"][-1]])
    small_sum = ordered_sum(small_all, "sum_small_grads", tr=SMALL_BLOB_ROWS)
    loss = small_sum[ROW_LNB_SINKS_LOSS, LOSS_COL]
    grads.update(_small_unblob(small_sum, {n: w[n].shape for n in WEIGHT_NAMES}))
    sc_g = small_sum[ROW_SC_CONV:ROW_SC_CONV + 3]
    grads["sc_conv_w"] = lax.dynamic_slice(sc_g, (0, me * 128), (3, 128)).reshape(w["sc_conv_w"].shape)
    cw_g = small_sum[ROW_CONV_W:].reshape(-1)[:31 * 512].reshape(31, 512)
    grads["conv_a_w"] = lax.dynamic_slice(cw_g, (0, me * 64), (31, 64)).reshape(w["conv_a_w"].shape)

    update("conv_a_w")
    update("sc_conv_w")
    for n in first_stage:
        grads[n] = jnp.stack([layer_grads[(n, l)] for l in range(w[n].shape[0])], axis=0)
        update(n)
    d_, m_, v_ = adamw(_small_blob(w), small_sum[:SMALL_ADAM_ROWS], _small_blob(m), _small_blob(v), "adamw_small",
                       tr=SMALL_ADAM_ROWS)
    shapes = {n: w[n].shape for n in WEIGHT_NAMES}
    delta.update(_small_unblob(d_, shapes))
    new_m.update(_small_unblob(m_, shapes))
    new_v.update(_small_unblob(v_, shapes))

    return (loss, grad_x[None], *[grads[n] for n in WEIGHT_NAMES], *[delta[n] for n in WEIGHT_NAMES],
            *[new_m[n] for n in WEIGHT_NAMES], *[new_v[n] for n in WEIGHT_NAMES])
```

```python
import functools

import jax
import jax.numpy as jnp
from jax import lax
from jax.experimental import pallas as pl
from jax.experimental.pallas import tpu as pltpu

F32 = jnp.float32
BF16 = jnp.bfloat16

D_MODEL = 1024
D_FF = 2816
CONV_A_CH = 512
CONV_A_WIDTH = 31
SWA_HEADS = 8
SWA_KV_HEADS = 2
SWA_GROUP = SWA_HEADS // SWA_KV_HEADS
HEAD_DIM = 64
WINDOW = 128
SC_CH = 1024
XA_HEADS = 4
XA_HEAD_DIM = D_MODEL // XA_HEADS
RMS_EPS = 1e-6
LN_EPS = 1e-5
ADAM_LR = 0.001
ADAM_B1 = 0.9
ADAM_B2 = 0.999
ADAM_EPS = 1e-08
ADAM_WD = 0.01
ADAM_STEP = 10
N_DEV = 8

V7X_VMEM_BYTES = 64 * 1024 * 1024
VMEM_LIMIT = V7X_VMEM_BYTES - 8 * 1024 * 1024
CONV_HALO = 32
SC_HALO = 8
NEG_BIG = -1e30

SPLIT_AXIS = dict(ffn1_w_gu=1, ffn1_w_down=0, even_w_in=1, even_w_out=0, odd_w_in=1, odd_w_out=0, xa_wq=0, xa_wkv=1, xa_wo=0,
                  ffn2_w_gu=1, ffn2_w_down=0)
STAGE_KEYS = dict(
    A=(("ffn1_w_gu", 0), ("ffn1_w_down", 0)),
    B=(("even_w_in", 0), ("even_w_out", 0), ("xa_wq", 0), ("xa_wkv", 0), ("xa_wo", 0)),
    C=(("ffn2_w_gu", 0), ("ffn2_w_down", 0)),
    D=(("ffn1_w_gu", 1), ("ffn1_w_down", 1), ("odd_w_in", 0), ("odd_w_out", 0), ("xa_wq", 1), ("xa_wkv", 1), ("xa_wo", 1)),
    E=(("ffn2_w_gu", 1), ("ffn2_w_down", 1)))


def _params(n_axes):
    return pltpu.CompilerParams(dimension_semantics=("arbitrary",) * n_axes, vmem_limit_bytes=VMEM_LIMIT)


def _tile(n, pref):
    t = min(n, pref)
    assert n % t == 0, (n, pref)
    return t


def _dot(a, b):
    return jnp.dot(a, b, preferred_element_type=F32)


def _dot_nt(a, b):
    return lax.dot_general(a, b, (((1,), (1,)), ((), ())), preferred_element_type=F32)


def _dot_tn(a, b):
    return lax.dot_general(a, b, (((0,), (0,)), ((), ())), preferred_element_type=F32)


def _sigmoid(x):
    return 0.5 * jnp.tanh(0.5 * x) + 0.5


ANY_SPEC = pl.BlockSpec(memory_space=pl.ANY)


def _with_dep(body, n_in, dep):
    if dep is None:
        return body, [], []
    return (lambda *refs: body(*refs[:n_in], *refs[n_in + 1:])), [ANY_SPEC], [dep]


def rmsnorm(h, g, name, tm=1024, dep=None):
    T, K = h.shape
    tm = _tile(T, tm)

    def kern(h_ref, g_ref, u_ref):
        x = h_ref[...]
        r = lax.rsqrt(jnp.mean(x * x, axis=-1, keepdims=True) + RMS_EPS)
        u_ref[...] = ((x * r) * g_ref[...]).astype(BF16)

    body, dep_spec, dep_arg = _with_dep(kern, 2, dep)
    return pl.pallas_call(
        body, name=name, grid=(T // tm,),
        in_specs=[pl.BlockSpec((tm, K), lambda i: (i, 0)), pl.BlockSpec((1, K), lambda i: (0, 0))] + dep_spec,
        out_specs=pl.BlockSpec((tm, K), lambda i: (i, 0)),
        out_shape=jax.ShapeDtypeStruct((T, K), BF16),
        compiler_params=_params(1),
    )(h, g, *dep_arg)


def matmul(a, w, out_dtype, name, tn, tm=2048, transposed=True, n_tiles=None, first_tile=0):
    T, K = a.shape
    N = w.shape[0] if transposed else w.shape[1]
    n_tiles = N // tn if n_tiles is None else n_tiles
    tm = _tile(T, tm)
    mm = _dot_nt if transposed else _dot

    def body(a_ref, w_ref, z_ref):
        z_ref[...] = mm(a_ref[...], w_ref[...]).astype(z_ref.dtype)

    w_spec = (pl.BlockSpec((tn, K), lambda i, j: (first_tile + j, 0)) if transposed
              else pl.BlockSpec((K, tn), lambda i, j: (0, first_tile + j)))
    return pl.pallas_call(
        body, name=name, grid=(T // tm, n_tiles),
        in_specs=[pl.BlockSpec((tm, K), lambda i, j: (i, 0)), w_spec],
        out_specs=pl.BlockSpec((tm, tn), lambda i, j: (i, j)),
        out_shape=jax.ShapeDtypeStruct((T, n_tiles * tn), out_dtype),
        compiler_params=_params(2),
    )(a, w)


def norm_matmul(h, g, w, out_dtype, name, tn, dep=None, transposed=True):
    u = rmsnorm(h, g, name + "_norm", dep=dep)
    return matmul(u, w, out_dtype, name, tn, transposed=transposed), u


def matmul_residual(a, w, res, name, tm=1024):
    T, K = a.shape
    N = w.shape[1]
    tm = _tile(T, tm)

    def body(a_ref, w_ref, r_ref, o_ref):
        o_ref[...] = r_ref[...] + _dot(a_ref[...], w_ref[...])

    return pl.pallas_call(
        body, name=name, grid=(T // tm,),
        in_specs=[pl.BlockSpec((tm, K), lambda i: (i, 0)),
                  pl.BlockSpec((K, N), lambda i: (0, 0)),
                  pl.BlockSpec((tm, N), lambda i: (i, 0))],
        out_specs=pl.BlockSpec((tm, N), lambda i: (i, 0)),
        out_shape=jax.ShapeDtypeStruct((T, N), F32),
        compiler_params=_params(1),
    )(a, w, res)


def matmul_nt(dy, w, out_dtype, name, tm=1024):
    T, N = dy.shape
    K = w.shape[0]
    tm = _tile(T, tm)

    def body(dy_ref, w_ref, o_ref):
        o_ref[...] = _dot_nt(dy_ref[...].astype(BF16), w_ref[...]).astype(o_ref.dtype)

    return pl.pallas_call(
        body, name=name, grid=(T // tm,),
        in_specs=[pl.BlockSpec((tm, N), lambda i: (i, 0)),
                  pl.BlockSpec((K, N), lambda i: (0, 0))],
        out_specs=pl.BlockSpec((tm, K), lambda i: (i, 0)),
        out_shape=jax.ShapeDtypeStruct((T, K), out_dtype),
        compiler_params=_params(1),
    )(dy, w)


def matmul_norm_bwd(dz, w, h, g, dh_in, name, tm=512, transposed=True, dep=None):
    T, N = dz.shape
    K = h.shape[1]
    tm = _tile(T, tm)

    def kern(dz_ref, w_ref, h_ref, g_ref, dhin_ref, dh_ref, dg_ref):
        @pl.when(pl.program_id(0) == 0)
        def _():
            dg_ref[...] = jnp.zeros_like(dg_ref)

        mm = _dot if transposed else _dot_nt
        du = mm(dz_ref[...], w_ref[...])
        x = h_ref[...]
        r = lax.rsqrt(jnp.mean(x * x, axis=-1, keepdims=True) + RMS_EPS)
        xh = x * r
        dg_ref[...] += jnp.sum(du * xh, axis=0, keepdims=True)
        dxh = du * g_ref[...]
        dh_ref[...] = dhin_ref[...] + r * (dxh - xh * jnp.mean(dxh * xh, axis=-1, keepdims=True))

    body, dep_spec, dep_arg = _with_dep(kern, 5, dep)
    return pl.pallas_call(
        body, name=name, grid=(T // tm,),
        in_specs=[pl.BlockSpec((tm, N), lambda i: (i, 0)),
                  pl.BlockSpec(w.shape, lambda i: (0, 0)),
                  pl.BlockSpec((tm, K), lambda i: (i, 0)),
                  pl.BlockSpec((1, K), lambda i: (0, 0)),
                  pl.BlockSpec((tm, K), lambda i: (i, 0))] + dep_spec,
        out_specs=[pl.BlockSpec((tm, K), lambda i: (i, 0)),
                   pl.BlockSpec((1, K), lambda i: (0, 0))],
        out_shape=[jax.ShapeDtypeStruct((T, K), F32), jax.ShapeDtypeStruct((1, K), F32)],
        compiler_params=_params(1),
    )(dz, w, h, g, dh_in, *dep_arg)


def matmul_tn(x, dy, name, scale=1.0, tk=None, tn=None, tt=1024):
    T, K = x.shape
    N = dy.shape[1]
    tk = K if tk is None else tk
    tn = N if tn is None else tn
    tt = _tile(T, tt)
    nt = T // tt

    def body(x_ref, dy_ref, o_ref, acc_ref):
        t = pl.program_id(2)

        @pl.when(t == 0)
        def _():
            acc_ref[...] = jnp.zeros_like(acc_ref)

        acc_ref[...] += _dot_tn(x_ref[...].astype(BF16), dy_ref[...].astype(BF16))

        @pl.when(t == nt - 1)
        def _():
            o_ref[...] = (acc_ref[...] * scale).astype(o_ref.dtype)

    return pl.pallas_call(
        body, name=name, grid=(K // tk, N // tn, nt),
        in_specs=[pl.BlockSpec((tt, tk), lambda a, b, t: (t, a)),
                  pl.BlockSpec((tt, tn), lambda a, b, t: (t, b))],
        out_specs=pl.BlockSpec((tk, tn), lambda a, b, t: (a, b)),
        out_shape=jax.ShapeDtypeStruct((K, N), BF16),
        scratch_shapes=[pltpu.VMEM((tk, tn), F32)],
        compiler_params=_params(3),
    )(x, dy)


def ffn_down(gu, wd, res, name, tm=512):
    T = gu.shape[0]
    F = gu.shape[1] // 2
    N = wd.shape[1]
    tm = _tile(T, tm)

    def body(g_ref, up_ref, w_ref, r_ref, o_ref, a_ref):
        g = g_ref[...].astype(F32)
        a_ref[...] = ((g * _sigmoid(g)) * up_ref[...].astype(F32)).astype(BF16)
        o_ref[...] = r_ref[...] + 0.5 * _dot(a_ref[...], w_ref[...])

    return pl.pallas_call(
        body, name=name, grid=(T // tm,),
        in_specs=[pl.BlockSpec((tm, F), lambda i: (i, 0)),
                  pl.BlockSpec((tm, F), lambda i: (i, 1)),
                  pl.BlockSpec((F, N), lambda i: (0, 0)),
                  pl.BlockSpec((tm, N), lambda i: (i, 0))],
        out_specs=[pl.BlockSpec((tm, N), lambda i: (i, 0)),
                   pl.BlockSpec((tm, F), lambda i: (i, 0))],
        out_shape=[jax.ShapeDtypeStruct((T, N), F32), jax.ShapeDtypeStruct((T, F), BF16)],
        compiler_params=_params(1),
    )(gu, gu, wd, res)


def ffn_down_bwd(dy, wd, gu, name, tm=512, dep=None):
    T, N = dy.shape
    F = wd.shape[0]
    tm = _tile(T, tm)

    def kern(dy_ref, w_ref, g_ref, up_ref, o_ref):
        da = 0.5 * _dot_nt(dy_ref[...].astype(BF16), w_ref[...])
        g = g_ref[...].astype(F32)
        up = up_ref[...].astype(F32)
        s = _sigmoid(g)
        o_ref[:, :F] = (da * up * (s * (1.0 + g * (1.0 - s)))).astype(BF16)
        o_ref[:, F:] = (da * (g * s)).astype(BF16)

    body, dep_spec, dep_arg = _with_dep(kern, 4, dep)
    return pl.pallas_call(
        body, name=name, grid=(T // tm,),
        in_specs=[pl.BlockSpec((tm, N), lambda i: (i, 0)),
                  pl.BlockSpec((F, N), lambda i: (0, 0)),
                  pl.BlockSpec((tm, F), lambda i: (i, 0)),
                  pl.BlockSpec((tm, F), lambda i: (i, 1))] + dep_spec,
        out_specs=pl.BlockSpec((tm, 2 * F), lambda i: (i, 0)),
        out_shape=jax.ShapeDtypeStruct((T, 2 * F), BF16),
        compiler_params=_params(1),
    )(dy, wd, gu, gu, *dep_arg)


def ffn_forward(h, g, w_gu, w_down, name, dep=None):
    gu, u = norm_matmul(h, g, w_gu, BF16, name + "_gu", D_FF // 2, dep=dep)
    h_out, a = ffn_down(gu, w_down, h, name + "_down")
    return h_out, (h, u, gu, a)


def ffn_backward(dy, saved, g, w_gu, w_down, name, dep=None, emit=None):
    h, u, gu, a = saved
    dgu = ffn_down_bwd(dy, w_down, gu, name + "_ddown", dep=dep)
    d_w_down = matmul_tn(a, dy, name + "_dwd", scale=0.5, tk=D_FF // 2)
    d_w_gu = matmul_tn(dgu, u, name + "_dwgu", tk=D_FF)
    dh, dg = matmul_norm_bwd(dgu, w_gu, h, g, dy, name + "_dx", dep=emit(d_w_gu, d_w_down))
    return dh, dg


CONV_ROW_CHUNK = 32
CONV_LANES = 128
CONV_X_OFFSETS = tuple(CONV_HALO - (CONV_A_WIDTH - 1) + k for k in range(CONV_A_WIDTH))
CONV_D_OFFSETS = tuple(CONV_A_WIDTH - 1 - k for k in range(CONV_A_WIDTH))


def _build_phases(ref, phase_ref, n_rows):
    for r in range(1, 8):
        phase_ref[r - 1] = ref[pl.ds(r, n_rows - 8), :]


def _tap_values(ref, phase_ref, offsets, n, base, lanes):
    out = {}
    for r in range(8):
        qs = sorted(o // 8 for o in offsets if o % 8 == r)
        if qs:
            lo, hi = qs[0], qs[-1]
            rows = pl.ds(base + 8 * lo, n + 8 * (hi - lo))
            span = ref[rows, lanes] if r == 0 else phase_ref[r - 1, rows, lanes]
            for q in qs:
                out[8 * q + r] = span[8 * (q - lo):8 * (q - lo) + n]
    return out


def conformer_conv_fwd(z, cw, cb, lg, lb, name, tm=512):
    T = z.shape[0]
    C = CONV_A_CH
    tm = _tile(T, tm)
    hb = tm // CONV_HALO
    CH = CONV_ROW_CHUNK
    KW = CONV_A_WIDTH

    def body(v_ref, gt_ref, pv_ref, pg_ref, cw_ref, cb_ref, lg_ref, lb_ref, o_ref, conv_ref, xs_ref, xph_ref):
        i = pl.program_id(0)
        prev = pv_ref[...] * _sigmoid(pg_ref[...])
        xs_ref[0:CONV_HALO, :] = jnp.where(i > 0, prev, 0.0)
        xs_ref[CONV_HALO:, :] = v_ref[...] * _sigmoid(gt_ref[...])
        _build_phases(xs_ref, xph_ref, tm + CONV_HALO)

        def chunk(c, carry):
            off = pl.multiple_of(c * CH, CH)
            for l0 in range(0, C, CONV_LANES):
                lanes = slice(l0, l0 + CONV_LANES)
                taps = _tap_values(xs_ref, xph_ref, CONV_X_OFFSETS, CH, off, lanes)
                acc = jnp.zeros((CH, CONV_LANES), F32) + cb_ref[:, lanes]
                for k in range(KW):
                    acc = acc + cw_ref[k:k + 1, lanes] * taps[CONV_X_OFFSETS[k]]
                conv_ref[pl.ds(off, CH), lanes] = acc
            return carry

        lax.fori_loop(0, tm // CH, chunk, 0)
        acc = conv_ref[...]
        mu = jnp.mean(acc, axis=-1, keepdims=True)
        xc = acc - mu
        var = jnp.mean(xc * xc, axis=-1, keepdims=True)
        y = (xc * lax.rsqrt(var + LN_EPS)) * lg_ref[...] + lb_ref[...]
        o_ref[...] = (y * _sigmoid(y)).astype(BF16)

    return pl.pallas_call(
        body, name=name, grid=(T // tm,),
        in_specs=[pl.BlockSpec((tm, C), lambda i: (i, 0)),
                  pl.BlockSpec((tm, C), lambda i: (i, 1)),
                  pl.BlockSpec((CONV_HALO, C), lambda i: (jnp.maximum(i * hb - 1, 0), 0)),
                  pl.BlockSpec((CONV_HALO, C), lambda i: (jnp.maximum(i * hb - 1, 0), 1)),
                  pl.BlockSpec((32, C), lambda i: (0, 0)),
                  pl.BlockSpec((1, C), lambda i: (0, 0)),
                  pl.BlockSpec((1, C), lambda i: (0, 0)),
                  pl.BlockSpec((1, C), lambda i: (0, 0))],
        out_specs=[pl.BlockSpec((tm, C), lambda i: (i, 0)), pl.BlockSpec((tm, C), lambda i: (i, 0))],
        out_shape=[jax.ShapeDtypeStruct((T, C), BF16), jax.ShapeDtypeStruct((T, C), F32)],
        scratch_shapes=[pltpu.VMEM((tm + CONV_HALO, C), F32), pltpu.VMEM((7, tm + CONV_HALO - 8, C), F32)],
        compiler_params=_params(1),
    )(z, z, z, z, cw, cb, lg, lb)


def conformer_conv_bwd(z, conv_out, dm, cw, lg, lb, name, tm=512):
    T = z.shape[0]
    C = CONV_A_CH
    tm = _tile(T, tm)
    hb = tm // CONV_HALO
    n_tiles = T // tm
    last_halo = T // CONV_HALO - 1
    R = tm + CONV_HALO
    KW = CONV_A_WIDTH
    CH = CONV_ROW_CHUNK

    def body(v_ref, gt_ref, pv_ref, pg_ref, cv_ref, ncv_ref, do_ref, ndo_ref, cw_ref, lg_ref, lb_ref,
             dz_ref, dcw_ref, dcb_ref, dlg_ref, dlb_ref, xs_ref, xph_ref, ds_ref, dph_ref):
        i = pl.program_id(0)

        @pl.when(i == 0)
        def _():
            dcw_ref[...] = jnp.zeros_like(dcw_ref)
            dcb_ref[...] = jnp.zeros_like(dcb_ref)
            dlg_ref[...] = jnp.zeros_like(dlg_ref)
            dlb_ref[...] = jnp.zeros_like(dlb_ref)

        prev = pv_ref[...] * _sigmoid(pg_ref[...])
        xs_ref[0:CONV_HALO, :] = jnp.where(i > 0, prev, 0.0)
        xs_ref[CONV_HALO:, :] = v_ref[...] * _sigmoid(gt_ref[...])
        _build_phases(xs_ref, xph_ref, tm + CONV_HALO)

        acc = jnp.concatenate([cv_ref[...], ncv_ref[...]], axis=0)
        mu = jnp.mean(acc, axis=-1, keepdims=True)
        xc = acc - mu
        rstd = lax.rsqrt(jnp.mean(xc * xc, axis=-1, keepdims=True) + LN_EPS)
        xh = xc * rstd
        y = xh * lg_ref[...] + lb_ref[...]
        s = _sigmoid(y)
        dout = jnp.concatenate([do_ref[...], jnp.where(i < n_tiles - 1, ndo_ref[...], 0.0)], axis=0)
        dy = dout * (s * (1.0 + y * (1.0 - s)))
        dxh = dy * lg_ref[...]
        dconv = rstd * (dxh - jnp.mean(dxh, axis=-1, keepdims=True) - xh * jnp.mean(dxh * xh, axis=-1, keepdims=True))
        ds_ref[...] = dconv
        dlg_ref[...] += jnp.sum(dy[:tm] * xh[:tm], axis=0, keepdims=True)
        dlb_ref[...] += jnp.sum(dy[:tm], axis=0, keepdims=True)
        dcb_ref[...] += jnp.sum(dconv[:tm], axis=0, keepdims=True)
        _build_phases(ds_ref, dph_ref, R)

        for l0 in range(0, C, CONV_LANES):
            lanes = slice(l0, l0 + CONV_LANES)

            def taps_bwd(c, wacc, l0=l0, lanes=lanes):
                off = pl.multiple_of(c * CH, CH)
                x_taps = _tap_values(xs_ref, xph_ref, CONV_X_OFFSETS, CH, off, lanes)
                d_taps = _tap_values(ds_ref, dph_ref, CONV_D_OFFSETS, CH, off, lanes)
                dc = ds_ref[pl.ds(off, CH), lanes]
                dglu = jnp.zeros((CH, CONV_LANES), F32)
                new = []
                for k in range(KW):
                    dglu = dglu + cw_ref[k:k + 1, lanes] * d_taps[CONV_D_OFFSETS[k]]
                    prod = dc * x_taps[CONV_X_OFFSETS[k]]
                    new.append(wacc[k] + ((prod[0:8] + prod[8:16]) + (prod[16:24] + prod[24:32])))
                val = v_ref[pl.ds(off, CH), lanes]
                sg = _sigmoid(gt_ref[pl.ds(off, CH), lanes])
                dz_ref[pl.ds(off, CH), lanes] = (dglu * sg).astype(BF16)
                dz_ref[pl.ds(off, CH), C + l0:C + l0 + CONV_LANES] = (dglu * val * sg * (1.0 - sg)).astype(BF16)
                return tuple(new)

            wacc = lax.fori_loop(0, tm // CH, taps_bwd, tuple(jnp.zeros((8, CONV_LANES), F32) for _ in range(KW)))
            for k in range(KW):
                dcw_ref[k:k + 1, lanes] += jnp.sum(wacc[k], axis=0, keepdims=True)

    prev_map = lambda i: jnp.maximum(i * hb - 1, 0)
    next_map = lambda i: jnp.minimum((i + 1) * hb, last_halo)
    return pl.pallas_call(
        body, name=name, grid=(n_tiles,),
        in_specs=[pl.BlockSpec((tm, C), lambda i: (i, 0)),
                  pl.BlockSpec((tm, C), lambda i: (i, 1)),
                  pl.BlockSpec((CONV_HALO, C), lambda i: (prev_map(i), 0)),
                  pl.BlockSpec((CONV_HALO, C), lambda i: (prev_map(i), 1)),
                  pl.BlockSpec((tm, C), lambda i: (i, 0)),
                  pl.BlockSpec((CONV_HALO, C), lambda i: (next_map(i), 0)),
                  pl.BlockSpec((tm, C), lambda i: (i, 0)),
                  pl.BlockSpec((CONV_HALO, C), lambda i: (next_map(i), 0)),
                  pl.BlockSpec((32, C), lambda i: (0, 0)),
                  pl.BlockSpec((1, C), lambda i: (0, 0)),
                  pl.BlockSpec((1, C), lambda i: (0, 0))],
        out_specs=[pl.BlockSpec((tm, 2 * C), lambda i: (i, 0)),
                   pl.BlockSpec((32, C), lambda i: (0, 0)),
                   pl.BlockSpec((1, C), lambda i: (0, 0)),
                   pl.BlockSpec((1, C), lambda i: (0, 0)),
                   pl.BlockSpec((1, C), lambda i: (0, 0))],
        out_shape=[jax.ShapeDtypeStruct((T, 2 * C), BF16),
                   jax.ShapeDtypeStruct((32, C), F32),
                   jax.ShapeDtypeStruct((1, C), F32),
                   jax.ShapeDtypeStruct((1, C), F32),
                   jax.ShapeDtypeStruct((1, C), F32)],
        scratch_shapes=[pltpu.VMEM((tm + CONV_HALO, C), F32), pltpu.VMEM((7, tm + CONV_HALO - 8, C), F32),
                        pltpu.VMEM((R, C), F32), pltpu.VMEM((7, R - 8, C), F32)],
        compiler_params=_params(1),
    )(z, z, z, z, conv_out, conv_out, dm, dm, cw, lg, lb)


def _swa_scores(q_h, kk_h, slope, bias_dist, valid, sink):
    s = _dot_nt(q_h, kk_h) * (HEAD_DIM ** -0.5) - slope * bias_dist
    s = jnp.where(valid, s, NEG_BIG)
    m = jnp.maximum(jnp.max(s, axis=-1, keepdims=True), sink)
    p = jnp.exp(s - m)
    e_sink = jnp.exp(sink - m)
    inv = 1.0 / (jnp.sum(p, axis=-1, keepdims=True) + e_sink)
    return p * inv, e_sink * inv


def _swa_mask(r0):
    qi = lax.broadcasted_iota(jnp.int32, (WINDOW, 2 * WINDOW), 0)
    kj = lax.broadcasted_iota(jnp.int32, (WINDOW, 2 * WINDOW), 1)
    dist = qi + WINDOW - kj
    valid = (dist >= 0) & (dist < WINDOW) & (r0 - WINDOW + kj >= 0)
    return dist.astype(F32), valid


def swa_fwd(z, kpad, vpad, sinks, name, tq=512):
    T = z.shape[0]
    tq = _tile(T, tq)
    HQ = SWA_HEADS * HEAD_DIM

    def body(sink_ref, q_ref, k_ref, v_ref, o_ref):
        i = pl.program_id(0)
        for sub in range(tq // WINDOW):
            r0 = pl.multiple_of(i * tq + sub * WINDOW, WINDOW)
            kk = k_ref[pl.ds(r0, 2 * WINDOW), :]
            vv = v_ref[pl.ds(r0, 2 * WINDOW), :]
            qb = q_ref[sub * WINDOW:(sub + 1) * WINDOW, :].astype(BF16)
            dist, valid = _swa_mask(r0)
            outs = []
            for h in range(SWA_HEADS):
                kh = h // SWA_GROUP
                ks = slice(kh * HEAD_DIM, (kh + 1) * HEAD_DIM)
                pn, _ = _swa_scores(qb[:, h * HEAD_DIM:(h + 1) * HEAD_DIM], kk[:, ks], 2.0 ** (-(h + 1)), dist, valid,
                                    sink_ref[h])
                outs.append(_dot(pn.astype(BF16), vv[:, ks]))
            o_ref[sub * WINDOW:(sub + 1) * WINDOW, :] = jnp.concatenate(outs, axis=-1).astype(BF16)

    return pl.pallas_call(
        body, name=name, grid=(T // tq,),
        in_specs=[pl.BlockSpec(memory_space=pltpu.SMEM),
                  pl.BlockSpec((tq, HQ), lambda i: (i, 2)),
                  pl.BlockSpec((T + WINDOW, 2 * HEAD_DIM), lambda i: (0, 0)),
                  pl.BlockSpec((T + WINDOW, 2 * HEAD_DIM), lambda i: (0, 0))],
        out_specs=pl.BlockSpec((tq, HQ), lambda i: (i, 0)),
        out_shape=jax.ShapeDtypeStruct((T, HQ), BF16),
        compiler_params=_params(1),
    )(sinks, z, kpad, vpad)


def swa_bwd(z, kpad, vpad, sinks, dm, name, tq=512):
    T = z.shape[0]
    tq = _tile(T, tq)
    HQ = SWA_HEADS * HEAD_DIM
    scale = HEAD_DIM ** -0.5

    def body(sink_ref, q_ref, k_ref, v_ref, do_ref, dq_ref, dk_ref, dv_ref, dsink_ref):
        i = pl.program_id(0)

        @pl.when(i == 0)
        def _():
            dk_ref[...] = jnp.zeros_like(dk_ref)
            dv_ref[...] = jnp.zeros_like(dv_ref)
            dsink_ref[...] = jnp.zeros_like(dsink_ref)

        for sub in range(tq // WINDOW):
            r0 = pl.multiple_of(i * tq + sub * WINDOW, WINDOW)
            kk = k_ref[pl.ds(r0, 2 * WINDOW), :]
            vv = v_ref[pl.ds(r0, 2 * WINDOW), :]
            rows = slice(sub * WINDOW, (sub + 1) * WINDOW)
            qb = q_ref[rows, :].astype(BF16)
            dob = do_ref[rows, :].astype(BF16)
            dist, valid = _swa_mask(r0)
            dqs, dks, dvs = [], [], []
            for kh in range(SWA_KV_HEADS):
                ks = slice(kh * HEAD_DIM, (kh + 1) * HEAD_DIM)
                dk_acc = jnp.zeros((2 * WINDOW, HEAD_DIM), F32)
                dv_acc = jnp.zeros((2 * WINDOW, HEAD_DIM), F32)
                for g in range(SWA_GROUP):
                    h = kh * SWA_GROUP + g
                    hs = slice(h * HEAD_DIM, (h + 1) * HEAD_DIM)
                    pn, p_sink = _swa_scores(qb[:, hs], kk[:, ks], 2.0 ** (-(h + 1)), dist, valid, sink_ref[h])
                    dp = _dot_nt(dob[:, hs], vv[:, ks])
                    delta = jnp.sum(pn * dp, axis=-1, keepdims=True)
                    ds = (pn * (dp - delta)).astype(BF16)
                    dqs.append(_dot(ds, kk[:, ks]) * scale)
                    dk_acc = dk_acc + _dot_tn(ds, qb[:, hs]) * scale
                    dv_acc = dv_acc + _dot_tn(pn.astype(BF16), dob[:, hs])
                    dsink_ref[h:h + 1, :] += jnp.zeros((1, 128), F32) - jnp.sum(p_sink * delta)
                dks.append(dk_acc)
                dvs.append(dv_acc)
            dq_ref[rows, :] = jnp.concatenate(dqs, axis=-1).astype(BF16)
            dk_ref[pl.ds(r0, 2 * WINDOW), :] += jnp.concatenate(dks, axis=-1)
            dv_ref[pl.ds(r0, 2 * WINDOW), :] += jnp.concatenate(dvs, axis=-1)

    kv_spec = pl.BlockSpec((T + WINDOW, 2 * HEAD_DIM), lambda i: (0, 0))
    return pl.pallas_call(
        body, name=name, grid=(T // tq,),
        in_specs=[pl.BlockSpec(memory_space=pltpu.SMEM),
                  pl.BlockSpec((tq, HQ), lambda i: (i, 2)),
                  kv_spec, kv_spec,
                  pl.BlockSpec((tq, HQ), lambda i: (i, 1))],
        out_specs=[pl.BlockSpec((tq, HQ), lambda i: (i, 0)),
                   kv_spec, kv_spec,
                   pl.BlockSpec((SWA_HEADS, 128), lambda i: (0, 0))],
        out_shape=[jax.ShapeDtypeStruct((T, HQ), BF16),
                   jax.ShapeDtypeStruct((T + WINDOW, 2 * HEAD_DIM), F32),
                   jax.ShapeDtypeStruct((T + WINDOW, 2 * HEAD_DIM), F32),
                   jax.ShapeDtypeStruct((SWA_HEADS, 128), F32)],
        compiler_params=_params(1),
    )(sinks, z, kpad, vpad, dm)


def short_conv_fwd(z, w, name, tm=512):
    T = z.shape[0]
    C = SC_CH
    tm = _tile(T, tm)
    hb = tm // SC_HALO

    def body(b_ref, c_ref, v_ref, pc_ref, pv_ref, w_ref, o_ref, xs_ref):
        i = pl.program_id(0)
        xs_ref[0:SC_HALO, :] = jnp.where(i > 0, pc_ref[...] * pv_ref[...], 0.0)
        xs_ref[SC_HALO:, :] = c_ref[...] * v_ref[...]
        conv = jnp.zeros((tm, C), F32)
        for k in range(3):
            conv = conv + w_ref[k:k + 1, :] * xs_ref[pl.ds(SC_HALO - 2 + k, tm), :]
        o_ref[...] = (b_ref[...] * conv).astype(BF16)

    prev_map = lambda i: jnp.maximum(i * hb - 1, 0)
    return pl.pallas_call(
        body, name=name, grid=(T // tm,),
        in_specs=[pl.BlockSpec((tm, C), lambda i: (i, 0)),
                  pl.BlockSpec((tm, C), lambda i: (i, 1)),
                  pl.BlockSpec((tm, C), lambda i: (i, 2)),
                  pl.BlockSpec((SC_HALO, C), lambda i: (prev_map(i), 1)),
                  pl.BlockSpec((SC_HALO, C), lambda i: (prev_map(i), 2)),
                  pl.BlockSpec((8, C), lambda i: (0, 0))],
        out_specs=pl.BlockSpec((tm, C), lambda i: (i, 0)),
        out_shape=jax.ShapeDtypeStruct((T, C), BF16),
        scratch_shapes=[pltpu.VMEM((tm + SC_HALO, C), F32)],
        compiler_params=_params(1),
    )(z, z, z, z, z, w)


def short_conv_bwd(z, dm, w, name, tm=512):
    T = z.shape[0]
    C = SC_CH
    tm = _tile(T, tm)
    hb = tm // SC_HALO
    n_tiles = T // tm
    last_halo = T // SC_HALO - 1
    R = tm + SC_HALO

    def body(b_ref, c_ref, v_ref, pc_ref, pv_ref, nb_ref, do_ref, ndo_ref, w_ref, dz_ref, dw_ref, xs_ref, ds_ref):
        i = pl.program_id(0)

        @pl.when(i == 0)
        def _():
            dw_ref[...] = jnp.zeros_like(dw_ref)

        c = c_ref[...]
        val = v_ref[...]
        dout = do_ref[...]
        xs_ref[0:SC_HALO, :] = jnp.where(i > 0, pc_ref[...] * pv_ref[...], 0.0)
        xs_ref[SC_HALO:, :] = c * val
        dconv = dout * b_ref[...]
        ds_ref[0:tm, :] = dconv
        ds_ref[tm:, :] = jnp.where(i < n_tiles - 1, ndo_ref[...] * nb_ref[...], 0.0)
        conv = jnp.zeros((tm, C), F32)
        dcv = jnp.zeros((tm, C), F32)
        for k in range(3):
            xk = xs_ref[pl.ds(SC_HALO - 2 + k, tm), :]
            conv = conv + w_ref[k:k + 1, :] * xk
            dw_ref[k:k + 1, :] += jnp.sum(dconv * xk, axis=0, keepdims=True)
            dcv = dcv + w_ref[k:k + 1, :] * ds_ref[pl.ds(2 - k, tm), :]
        dz_ref[:, 0:C] = (dout * conv).astype(BF16)
        dz_ref[:, C:2 * C] = (dcv * val).astype(BF16)
        dz_ref[:, 2 * C:] = (dcv * c).astype(BF16)

    prev_map = lambda i: jnp.maximum(i * hb - 1, 0)
    next_map = lambda i: jnp.minimum((i + 1) * hb, last_halo)
    return pl.pallas_call(
        body, name=name, grid=(n_tiles,),
        in_specs=[pl.BlockSpec((tm, C), lambda i: (i, 0)),
                  pl.BlockSpec((tm, C), lambda i: (i, 1)),
                  pl.BlockSpec((tm, C), lambda i: (i, 2)),
                  pl.BlockSpec((SC_HALO, C), lambda i: (prev_map(i), 1)),
                  pl.BlockSpec((SC_HALO, C), lambda i: (prev_map(i), 2)),
                  pl.BlockSpec((SC_HALO, C), lambda i: (next_map(i), 0)),
                  pl.BlockSpec((tm, C), lambda i: (i, 0)),
                  pl.BlockSpec((SC_HALO, C), lambda i: (next_map(i), 0)),
                  pl.BlockSpec((8, C), lambda i: (0, 0))],
        out_specs=[pl.BlockSpec((tm, 3 * C), lambda i: (i, 0)),
                   pl.BlockSpec((8, C), lambda i: (0, 0))],
        out_shape=[jax.ShapeDtypeStruct((T, 3 * C), BF16), jax.ShapeDtypeStruct((8, C), F32)],
        scratch_shapes=[pltpu.VMEM((tm + SC_HALO, C), F32), pltpu.VMEM((R, C), F32)],
        compiler_params=_params(1),
    )(z, z, z, z, z, z, dm, dm, w)


def _xa_probs(q_h, k_h):
    s = _dot_nt(q_h, k_h) * (XA_HEAD_DIM ** -0.5)
    p = jnp.exp(s - jnp.max(s, axis=-1, keepdims=True))
    return p * (1.0 / jnp.sum(p, axis=-1, keepdims=True))


def xattn_fwd(q, kv, name, tm=512):
    T = q.shape[0]
    M = kv.shape[0]
    tm = _tile(T, tm)

    def body(q_ref, k_ref, v_ref, o_ref):
        for h in range(XA_HEADS):
            hs = slice(h * XA_HEAD_DIM, (h + 1) * XA_HEAD_DIM)
            p = _xa_probs(q_ref[:, hs], k_ref[:, hs])
            o_ref[:, hs] = _dot(p.astype(BF16), v_ref[:, hs]).astype(BF16)

    return pl.pallas_call(
        body, name=name, grid=(T // tm,),
        in_specs=[pl.BlockSpec((tm, D_MODEL), lambda i: (i, 0)),
                  pl.BlockSpec((M, D_MODEL), lambda i: (0, 0)),
                  pl.BlockSpec((M, D_MODEL), lambda i: (0, 1))],
        out_specs=pl.BlockSpec((tm, D_MODEL), lambda i: (i, 0)),
        out_shape=jax.ShapeDtypeStruct((T, D_MODEL), BF16),
        compiler_params=_params(1),
    )(q, kv, kv)


def xattn_bwd(q, kv, do, name, tm=512):
    T = q.shape[0]
    M = kv.shape[0]
    tm = _tile(T, tm)
    scale = XA_HEAD_DIM ** -0.5

    def body(q_ref, k_ref, v_ref, do_ref, dq_ref, dkv_ref):
        @pl.when(pl.program_id(0) == 0)
        def _():
            dkv_ref[...] = jnp.zeros_like(dkv_ref)

        for h in range(XA_HEADS):
            hs = slice(h * XA_HEAD_DIM, (h + 1) * XA_HEAD_DIM)
            vs = slice(D_MODEL + h * XA_HEAD_DIM, D_MODEL + (h + 1) * XA_HEAD_DIM)
            q_h = q_ref[:, hs]
            do_h = do_ref[:, hs]
            p = _xa_probs(q_h, k_ref[:, hs])
            dp = _dot_nt(do_h, v_ref[:, hs])
            ds = (p * (dp - jnp.sum(p * dp, axis=-1, keepdims=True))).astype(BF16)
            dq_ref[:, hs] = (_dot(ds, k_ref[:, hs]) * scale).astype(BF16)
            dkv_ref[:, hs] += _dot_tn(ds, q_h) * scale
            dkv_ref[:, vs] += _dot_tn(p.astype(BF16), do_h)

    return pl.pallas_call(
        body, name=name, grid=(T // tm,),
        in_specs=[pl.BlockSpec((tm, D_MODEL), lambda i: (i, 0)),
                  pl.BlockSpec((M, D_MODEL), lambda i: (0, 0)),
                  pl.BlockSpec((M, D_MODEL), lambda i: (0, 1)),
                  pl.BlockSpec((tm, D_MODEL), lambda i: (i, 0))],
        out_specs=[pl.BlockSpec((tm, D_MODEL), lambda i: (i, 0)),
                   pl.BlockSpec((M, 2 * D_MODEL), lambda i: (0, 0))],
        out_shape=[jax.ShapeDtypeStruct((T, D_MODEL), BF16), jax.ShapeDtypeStruct((M, 2 * D_MODEL), F32)],
        compiler_params=_params(1),
    )(q, kv, kv, do)


def final_loss(h, g, target, name, tm=512):
    T, K = h.shape
    tm = _tile(T, tm)

    def body(h_ref, g_ref, t_ref, dh_ref, dg_ref, loss_ref):
        @pl.when(pl.program_id(0) == 0)
        def _():
            dg_ref[...] = jnp.zeros_like(dg_ref)
            loss_ref[...] = jnp.zeros_like(loss_ref)

        x = h_ref[...]
        r = lax.rsqrt(jnp.mean(x * x, axis=-1, keepdims=True) + RMS_EPS)
        xh = x * r
        e = xh * g_ref[...] - t_ref[...]
        loss_ref[...] += jnp.zeros((1, 128), F32) + 0.5 * jnp.sum(jnp.mean(e * e, axis=-1, keepdims=True))
        dy = e * (1.0 / K)
        dg_ref[...] += jnp.sum(dy * xh, axis=0, keepdims=True)
        dxh = dy * g_ref[...]
        dh_ref[...] = r * (dxh - xh * jnp.mean(dxh * xh, axis=-1, keepdims=True))

    return pl.pallas_call(
        body, name=name, grid=(T // tm,),
        in_specs=[pl.BlockSpec((tm, K), lambda i: (i, 0)),
                  pl.BlockSpec((1, K), lambda i: (0, 0)),
                  pl.BlockSpec((tm, K), lambda i: (i, 0))],
        out_specs=[pl.BlockSpec((tm, K), lambda i: (i, 0)),
                   pl.BlockSpec((1, K), lambda i: (0, 0)),
                   pl.BlockSpec((1, 128), lambda i: (0, 0))],
        out_shape=[jax.ShapeDtypeStruct((T, K), F32), jax.ShapeDtypeStruct((1, K), F32),
                   jax.ShapeDtypeStruct((1, 128), F32)],
        compiler_params=_params(1),
    )(h, g, target)


def _row(v):
    return v.reshape(1, -1)


def _pad_rows(a, rows):
    return jnp.pad(a, ((0, rows - a.shape[0]), (0, 0)))


def local_step(x, mem, target, P, get_weights, put_grads):
    cw = _pad_rows(P["conv_a_w"], 32)
    scw = _pad_rows(P["sc_conv_w"], 8)
    cb, lg, lb = _row(P["conv_a_b"]), _row(P["conv_a_ln_g"]), _row(P["conv_a_ln_b"])
    sinks = P["swa_sinks"]

    class _Layered:
        def __init__(self, store, name=None):
            self.store, self.name = store, name

        def __getitem__(self, key):
            if self.name is None:
                return self.store[(key, 0)] if key in ("even_w_in", "even_w_out", "odd_w_in", "odd_w_out") \
                    else _Layered(self.store, key)
            return self.store[(self.name, key)]

    store = {}
    W = _Layered(store)
    saved = []
    h = x
    for i in range(2):
        L = f"l{i}"
        new, dep = get_weights("A" if i == 0 else "D", h)
        store.update(new)

        h, s_ffn1 = ffn_forward(h, P["ffn1_norm"][i:i + 1], W["ffn1_w_gu"][i], W["ffn1_w_down"][i], L + "_ffn1", dep=dep)
        h1 = h
        if i == 0:
            new, _ = get_weights("B", h)
            store.update(new)
            u2 = rmsnorm(h1, P["mix_norm"][i:i + 1], L + "_mix_in_norm")
            z = matmul(u2, W["even_w_in"], F32, L + "_mix_in", 768, n_tiles=2)
            kv = matmul(u2, W["even_w_in"], BF16, L + "_mix_kv", 256, n_tiles=1, first_tile=6)
            a, conv_out = conformer_conv_fwd(z, cw, cb, lg, lb, L + "_conv")
            kpad = jnp.pad(kv[:, :2 * HEAD_DIM], ((WINDOW, 0), (0, 0)))
            vpad = jnp.pad(kv[:, 2 * HEAD_DIM:], ((WINDOW, 0), (0, 0)))
            o = swa_fwd(z, kpad, vpad, sinks, L + "_swa")
            m = jnp.concatenate([a, o], axis=-1)
            h = matmul_residual(m, W["even_w_out"], h1, L + "_mix_out")
            s_mix = (h1, u2, z, m, kpad, vpad, conv_out)
        else:
            z, u2 = norm_matmul(h1, P["mix_norm"][i:i + 1], W["odd_w_in"], F32, L + "_mix_in", 1024)
            m = short_conv_fwd(z, scw, L + "_sconv")
            h = matmul_residual(m, W["odd_w_out"], h1, L + "_mix_out")
            s_mix = (h1, u2, z, m)
        h2 = h
        kv, umem = norm_matmul(mem, P["xa_mem_norm"][i:i + 1], W["xa_wkv"][i], BF16, L + "_xa_kv", 2 * D_MODEL)
        q, u3 = norm_matmul(h2, P["xa_norm"][i:i + 1], W["xa_wq"][i], BF16, L + "_xa_q", D_MODEL, transposed=False)
        o = xattn_fwd(q, kv, L + "_xa")
        h = matmul_residual(o, W["xa_wo"][i], h2, L + "_xa_out")
        s_xa = (h2, u3, q, o, kv, umem)
        new, _ = get_weights("C" if i == 0 else "E", h)
        store.update(new)
        h, s_ffn2 = ffn_forward(h, P["ffn2_norm"][i:i + 1], W["ffn2_w_gu"][i], W["ffn2_w_down"][i], L + "_ffn2")
        saved.append((s_ffn1, s_mix, s_xa, s_ffn2))

    dh, d_final, loss = final_loss(h, _row(P["final_norm"]), target, "final_loss")

    names = ("ffn1_w_gu", "ffn1_w_down", "ffn2_w_gu", "ffn2_w_down", "xa_wq", "xa_wkv", "xa_wo", "even_w_in", "even_w_out",
             "odd_w_in", "odd_w_out")
    dW = {k: [None, None] for k in names}
    dP = {k: [None, None] for k in ("ffn1_norm", "mix_norm", "xa_norm", "xa_mem_norm", "ffn2_norm")}
    dP["final_norm"] = d_final.reshape(-1)
    for i in (1, 0):
        L = f"l{i}b"
        s_ffn1, s_mix, s_xa, s_ffn2 = saved[i]

        def keep_ffn2(d_w_gu, d_w_down, i=i):
            dW["ffn2_w_gu"][i], dW["ffn2_w_down"][i] = d_w_gu, d_w_down

        def send_ffn1(d_w_gu, d_w_down, i=i):
            dW["ffn1_w_gu"][i], dW["ffn1_w_down"][i] = d_w_gu, d_w_down
            keys = STAGE_KEYS["D"] + STAGE_KEYS["E"] if i == 1 else STAGE_KEYS["A"]
            return put_grads("D" if i == 1 else "A", {k: dW[k[0]][k[1]] for k in keys})

        dh, dP["ffn2_norm"][i] = ffn_backward(
            dh, s_ffn2, P["ffn2_norm"][i:i + 1], W["ffn2_w_gu"][i], W["ffn2_w_down"][i], L + "_ffn2", emit=keep_ffn2)
        h2, u3, q, o, kv, umem = s_xa
        dW["xa_wo"][i] = matmul_tn(o, dh, L + "_xa_dwo")
        do = matmul_nt(dh, W["xa_wo"][i], BF16, L + "_xa_do")
        dq, dkv = xattn_bwd(q, kv, do, L + "_xa")
        dW["xa_wq"][i] = matmul_tn(u3, dq, L + "_xa_dwq")
        dW["xa_wkv"][i] = matmul_tn(dkv, umem, L + "_xa_dwkv", tk=1024)
        dkv_b = dkv.astype(BF16)
        _, dP["xa_mem_norm"][i] = matmul_norm_bwd(dkv_b, W["xa_wkv"][i], mem, P["xa_mem_norm"][i:i + 1],
                                                  jnp.zeros_like(mem), L + "_xa_dmem")
        dh, dP["xa_norm"][i] = matmul_norm_bwd(dq, W["xa_wq"][i], h2, P["xa_norm"][i:i + 1], dh, L + "_xa_dx",
                                               transposed=False)
        if i == 0:
            h1, u2, z, m, kpad, vpad, conv_out = s_mix
            dW["even_w_out"][0] = matmul_tn(m, dh, L + "_mix_dwo")
            dm = matmul_nt(dh, W["even_w_out"], F32, L + "_mix_dm")
            dz_conv, dcw, dcb, dlg, dlb = conformer_conv_bwd(z, conv_out, dm, cw, lg, lb, L + "_conv")
            dq_s, dkp, dvp, dsk = swa_bwd(z, kpad, vpad, sinks, dm, L + "_swa")
            dz = jnp.concatenate([dz_conv, dq_s, dkp[WINDOW:].astype(BF16), dvp[WINDOW:].astype(BF16)], axis=-1)
            dW["even_w_in"][0] = matmul_tn(dz, u2, L + "_mix_dwi", tk=896)
            dh, dP["mix_norm"][i] = matmul_norm_bwd(dz, W["even_w_in"], h1, P["mix_norm"][i:i + 1], dh, L + "_mix_dx")
            dP["conv_a_w"] = dcw[:CONV_A_WIDTH]
            dP["conv_a_b"], dP["conv_a_ln_g"], dP["conv_a_ln_b"] = dcb.reshape(-1), dlg.reshape(-1), dlb.reshape(-1)
            dP["swa_sinks"] = dsk[:, 0]
        else:
            h1, u2, z, m = s_mix
            dW["odd_w_out"][0] = matmul_tn(m, dh, L + "_mix_dwo")
            dm = matmul_nt(dh, W["odd_w_out"], F32, L + "_mix_dm")
            dz, dscw = short_conv_bwd(z, dm, scw, L + "_sconv")
            dW["odd_w_in"][0] = matmul_tn(dz, u2, L + "_mix_dwi", tk=1024)
            dh, dP["mix_norm"][i] = matmul_norm_bwd(dz, W["odd_w_in"], h1, P["mix_norm"][i:i + 1], dh, L + "_mix_dx")
            dP["sc_conv_w"] = dscw[:3]
        dep = put_grads("BC", {k: dW[k[0]][k[1]] for k in STAGE_KEYS["B"] + STAGE_KEYS["C"]}) if i == 0 else None
        dh, dP["ffn1_norm"][i] = ffn_backward(
            dh, s_ffn1, P["ffn1_norm"][i:i + 1], W["ffn1_w_gu"][i], W["ffn1_w_down"][i], L + "_ffn1", dep=dep,
            emit=send_ffn1)
    for k in ("ffn1_norm", "mix_norm", "xa_norm", "xa_mem_norm", "ffn2_norm"):
        dP[k] = jnp.concatenate(dP[k], axis=0)
    return loss, dh, dP


def _mesh_pos():
    return lax.axis_index("x"), lax.axis_index("y"), lax.axis_index("c")


def _flat_index(px, py, pc):
    return 4 * px + 2 * py + pc


def all_gather(blob, name, dep=None):
    R, C = blob.shape

    def kern(x_ref, out_ref, send_sems, recv_sems, local_sem):
        x, y, c = _mesh_pos()
        me, sibling = (x, y, c), (x, y, 1 - c)
        chips = [(1 - x, y), (x, 1 - y), (1 - x, 1 - y)]

        def slot(px, py, pc):
            return out_ref.at[_flat_index(px, py, pc)]

        def copy(k, block, to, src=None):
            return pltpu.make_async_remote_copy(
                src_ref=slot(*block) if src is None else src, dst_ref=slot(*block),
                send_sem=send_sems.at[k], recv_sem=recv_sems.at[k],
                device_id=to, device_id_type=pl.DeviceIdType.MESH)

        mine = pltpu.make_async_copy(x_ref, slot(*me), local_sem)
        mine.start()
        first = [copy(0, me, sibling, src=x_ref)]
        first += [copy(1 + j, me, (*chip, c), src=x_ref) for j, chip in enumerate(chips)]
        for cp in first:
            cp.start()
        passed = [copy(4 + j, (*chip, c), sibling) for j, chip in enumerate(chips)]
        for j, chip in enumerate(chips):
            copy(1 + j, (*chip, c), me).wait_recv()
            passed[j].start()
        copy(0, sibling, me).wait_recv()
        for j, chip in enumerate(chips):
            copy(4 + j, (*chip, 1 - c), me).wait_recv()
        for cp in first + passed:
            cp.wait_send()
        mine.wait()

    body, dep_spec, dep_arg = _with_dep(kern, 1, dep)
    return pl.pallas_call(
        body, name=name,
        out_shape=jax.ShapeDtypeStruct((N_DEV, R, C), blob.dtype),
        in_specs=[ANY_SPEC] + dep_spec,
        out_specs=ANY_SPEC,
        scratch_shapes=[pltpu.SemaphoreType.DMA((7,)), pltpu.SemaphoreType.DMA((7,)), pltpu.SemaphoreType.DMA],
    )(blob, *dep_arg)


def scatter_exchange(g, name):
    _, R, C = g.shape

    def body(g_ref, out_ref, send_sems, recv_sems, local_sem):
        x, y, c = _mesh_pos()
        me_idx = _flat_index(x, y, c)
        mine = pltpu.make_async_copy(g_ref.at[me_idx], out_ref.at[me_idx], local_sem)
        mine.start()
        sends, peers = [], []
        for k in range(1, N_DEV):
            px = 1 - x if k & 4 else x
            py = 1 - y if k & 2 else y
            pc = 1 - c if k & 1 else c
            peer_idx = _flat_index(px, py, pc)
            cp = pltpu.make_async_remote_copy(
                src_ref=g_ref.at[peer_idx], dst_ref=out_ref.at[me_idx],
                send_sem=send_sems.at[k - 1], recv_sem=recv_sems.at[k - 1],
                device_id=(px, py, pc), device_id_type=pl.DeviceIdType.MESH)
            cp.start()
            sends.append(cp)
            peers.append((peer_idx, (px, py, pc)))
        for k in range(1, N_DEV):
            peer_idx, peer = peers[k - 1]
            pltpu.make_async_remote_copy(
                src_ref=g_ref.at[me_idx], dst_ref=out_ref.at[peer_idx],
                send_sem=send_sems.at[k - 1], recv_sem=recv_sems.at[k - 1],
                device_id=peer, device_id_type=pl.DeviceIdType.MESH).wait_recv()
        for cp in sends:
            cp.wait_send()
        mine.wait()

    return pl.pallas_call(
        body, name=name,
        out_shape=jax.ShapeDtypeStruct(g.shape, g.dtype),
        in_specs=[pl.BlockSpec(memory_space=pl.ANY)],
        out_specs=pl.BlockSpec(memory_space=pl.ANY),
        scratch_shapes=[pltpu.SemaphoreType.DMA((7,)), pltpu.SemaphoreType.DMA((7,)), pltpu.SemaphoreType.DMA],
    )(g)


HBM_SPEC = pl.BlockSpec(memory_space=pltpu.HBM)
SEM_SPEC = pl.BlockSpec(memory_space=pltpu.SEMAPHORE)
DATAFLOW_EFFECT = pltpu.SideEffectType.DATAFLOW_SIDE_EFFECTING


def _peers(x, y, c):
    out = []
    for k in range(1, N_DEV):
        pos = (1 - x if k & 4 else x, 1 - y if k & 2 else y, 1 - c if k & 1 else c)
        out.append((_flat_index(*pos), pos))
    return out


def _exchange_copy(src_ref, land_ref, send_sems, recv_sems, j, me, peer_idx, peer, scatter):
    return pltpu.make_async_remote_copy(
        src_ref=src_ref.at[peer_idx] if scatter else src_ref, dst_ref=land_ref.at[me],
        send_sem=send_sems.at[j], recv_sem=recv_sems.at[j], device_id=peer, device_id_type=pl.DeviceIdType.MESH)


def exchange_start(srcs, lands, scatter, after, name):
    n = len(srcs)
    n_after = len(after)

    def body(*refs):
        src_refs, land_refs = refs[:n], refs[n:2 * n]
        outs = refs[2 * n + n_after:]
        send_sems, recv_sems, token = outs[:n], outs[n:2 * n], outs[4 * n]
        x, y, c = _mesh_pos()
        me = _flat_index(x, y, c)
        for g in range(n):
            for j, (peer_idx, peer) in enumerate(_peers(x, y, c)):
                _exchange_copy(src_refs[g], land_refs[g], send_sems[g], recv_sems[g], j, me, peer_idx, peer, scatter).start()
        token[...] = jnp.zeros_like(token)

    hbm = lambda a: pltpu.with_memory_space_constraint(a, pltpu.HBM)
    res = pl.pallas_call(
        body, name=name,
        out_shape=(*[pltpu.SemaphoreType.DMA((N_DEV - 1,))] * (2 * n),
                   *[pltpu.HBM(a.shape, a.dtype) for a in srcs], *[pltpu.HBM(a.shape, a.dtype) for a in lands],
                   jax.ShapeDtypeStruct((8, 128), F32)),
        in_specs=[HBM_SPEC] * (2 * n) + [ANY_SPEC] * n_after,
        out_specs=(*[SEM_SPEC] * (2 * n), *[HBM_SPEC] * (2 * n), pl.BlockSpec(memory_space=pltpu.VMEM)),
        input_output_aliases={i: 2 * n + i for i in range(2 * n)},
        compiler_params=pltpu.CompilerParams(has_side_effects=DATAFLOW_EFFECT),
    )(*[hbm(a) for a in srcs], *[hbm(a) for a in lands], *after)
    handles = [(res[g], res[n + g], res[2 * n + g], res[3 * n + g]) for g in range(n)]
    return handles, res[4 * n]


def exchange_wait(handles, scatter, after, name):
    n = len(handles)

    def body(*refs):
        src_refs, land_refs = refs[:n], refs[n:2 * n]
        send_sems, recv_sems = refs[2 * n:3 * n], refs[3 * n:4 * n]
        x, y, c = _mesh_pos()
        me = _flat_index(x, y, c)
        for g in range(n):
            for j, (peer_idx, peer) in enumerate(_peers(x, y, c)):
                mine = _exchange_copy(src_refs[g], land_refs[g], send_sems[g], recv_sems[g], j, me, peer_idx, peer,
                                      scatter)
                mine.wait_send()
                theirs = pltpu.make_async_remote_copy(
                    src_ref=src_refs[g].at[me] if scatter else src_refs[g], dst_ref=land_refs[g].at[peer_idx],
                    send_sem=send_sems[g].at[j], recv_sem=recv_sems[g].at[j], device_id=peer,
                    device_id_type=pl.DeviceIdType.MESH)
                theirs.wait_recv()

    srcs = [h[2] for h in handles]
    lands = [h[3] for h in handles]
    res = pl.pallas_call(
        body, name=name,
        out_shape=tuple(pltpu.HBM(a.shape, a.dtype) for a in srcs + lands),
        in_specs=[HBM_SPEC] * (2 * n) + [SEM_SPEC] * (2 * n) + [ANY_SPEC],
        out_specs=tuple([HBM_SPEC] * (2 * n)),
        input_output_aliases={i: i for i in range(2 * n)},
        compiler_params=pltpu.CompilerParams(has_side_effects=DATAFLOW_EFFECT),
    )(*srcs, *lands, *[h[0] for h in handles], *[h[1] for h in handles], after)
    return [(res[g], res[n + g]) for g in range(n)]


def ordered_sum(parts, name, tr=512):
    n, R, C = parts.shape
    tr = next((t for t in range(min(tr, R), 15, -16) if R % t == 0), R)

    def body(p_ref, o_ref):
        acc = p_ref[0].astype(F32)
        for j in range(1, n):
            acc = acc + p_ref[j].astype(F32)
        o_ref[...] = acc

    return pl.pallas_call(
        body, name=name, grid=(R // tr,),
        in_specs=[pl.BlockSpec((n, tr, C), lambda i: (0, i, 0))],
        out_specs=pl.BlockSpec((tr, C), lambda i: (i, 0)),
        out_shape=jax.ShapeDtypeStruct((R, C), F32),
        compiler_params=_params(1),
    )(parts)


def adamw(w, g, m, v, name, tr=256):
    R, C = w.shape
    tr = next((t for t in range(tr, 7, -8) if R % t == 0), R)
    c1 = 1.0 - ADAM_B1 ** ADAM_STEP
    c2 = 1.0 - ADAM_B2 ** ADAM_STEP

    def body(w_ref, g_ref, m_ref, v_ref, d_ref, mo_ref, vo_ref):
        grad = g_ref[...]
        m2 = ADAM_B1 * m_ref[...] + (1.0 - ADAM_B1) * grad
        v2 = ADAM_B2 * v_ref[...] + (1.0 - ADAM_B2) * (grad * grad)
        mo_ref[...] = m2
        vo_ref[...] = v2
        d_ref[...] = -ADAM_LR * ((m2 / c1) / (jnp.sqrt(v2 / c2) + ADAM_EPS) + ADAM_WD * w_ref[...])

    spec = pl.BlockSpec((tr, C), lambda i: (i, 0))
    return pl.pallas_call(
        body, name=name, grid=(R // tr,),
        in_specs=[spec] * 4, out_specs=[spec] * 3,
        out_shape=[jax.ShapeDtypeStruct((R, C), F32)] * 3,
        compiler_params=_params(1),
    )(w, g, m, v)


WEIGHT_NAMES = ("ffn1_norm", "ffn1_w_gu", "ffn1_w_down", "mix_norm", "even_w_in", "conv_a_w", "conv_a_b", "conv_a_ln_g",
                "conv_a_ln_b", "swa_sinks", "even_w_out", "odd_w_in", "sc_conv_w", "odd_w_out", "xa_norm", "xa_mem_norm",
                "xa_wq", "xa_wkv", "xa_wo", "ffn2_norm", "ffn2_w_gu", "ffn2_w_down", "final_norm")
BLOB_COLS = 1024
SMALL_ROWS = (("ffn1_norm", 0, 2), ("mix_norm", 2, 2), ("xa_norm", 4, 2), ("xa_mem_norm", 6, 2), ("ffn2_norm", 8, 2),
              ("final_norm", 10, 1))
ROW_CONV_B_LNG = 11
ROW_LNB_SINKS_LOSS = 12
LOSS_COL = 512 + SWA_HEADS
ROW_SC_CONV = 13
ROW_CONV_W = 16
SMALL_BLOB_ROWS = 32
SMALL_ADAM_ROWS = 16


def _small_blob(v):
    rows = [v[n].reshape(-1, D_MODEL) for n, _, _ in SMALL_ROWS]
    rows.append(jnp.concatenate([v["conv_a_b"].reshape(-1), v["conv_a_ln_g"].reshape(-1)]).reshape(1, D_MODEL))
    tail = jnp.zeros((D_MODEL - 512 - SWA_HEADS,), F32)
    if "loss" in v:
        tail = tail.at[0].set(v["loss"])
    rows.append(jnp.concatenate([v["conv_a_ln_b"].reshape(-1), v["swa_sinks"].reshape(-1), tail]).reshape(1, D_MODEL))
    rows.append(jnp.zeros((SMALL_ADAM_ROWS - ROW_SC_CONV, D_MODEL), F32))
    return jnp.concatenate(rows, axis=0)


def _small_unblob(b, shapes):
    out = {n: b[r:r + k].reshape(shapes[n]) for n, r, k in SMALL_ROWS}
    out["conv_a_b"] = b[ROW_CONV_B_LNG, :512].reshape(shapes["conv_a_b"])
    out["conv_a_ln_g"] = b[ROW_CONV_B_LNG, 512:].reshape(shapes["conv_a_ln_g"])
    out["conv_a_ln_b"] = b[ROW_LNB_SINKS_LOSS, :512].reshape(shapes["conv_a_ln_b"])
    out["swa_sinks"] = b[ROW_LNB_SINKS_LOSS, 512:512 + SWA_HEADS].reshape(shapes["swa_sinks"])
    return out


def kernel(x, mem, ffn1_norm, ffn1_w_gu, ffn1_w_down, mix_norm, even_w_in, conv_a_w, conv_a_b, conv_a_ln_g, conv_a_ln_b, swa_sinks, even_w_out, odd_w_in, sc_conv_w, odd_w_out, xa_norm, xa_mem_norm, xa_wq, xa_wkv, xa_wo, ffn2_norm, ffn2_w_gu, ffn2_w_down, final_norm, loss_target, m_ffn1_norm, m_ffn1_w_gu, m_ffn1_w_down, m_mix_norm, m_even_w_in, m_conv_a_w, m_conv_a_b, m_conv_a_ln_g, m_conv_a_ln_b, m_swa_sinks, m_even_w_out, m_odd_w_in, m_sc_conv_w, m_odd_w_out, m_xa_norm, m_xa_mem_norm, m_xa_wq, m_xa_wkv, m_xa_wo, m_ffn2_norm, m_ffn2_w_gu, m_ffn2_w_down, m_final_norm, v_ffn1_norm, v_ffn1_w_gu, v_ffn1_w_down, v_mix_norm, v_even_w_in, v_conv_a_w, v_conv_a_b, v_conv_a_ln_g, v_conv_a_ln_b, v_swa_sinks, v_even_w_out, v_odd_w_in, v_sc_conv_w, v_odd_w_out, v_xa_norm, v_xa_mem_norm, v_xa_wq, v_xa_wkv, v_xa_wo, v_ffn2_norm, v_ffn2_w_gu, v_ffn2_w_down, v_final_norm):
    w = dict(ffn1_norm=ffn1_norm, ffn1_w_gu=ffn1_w_gu, ffn1_w_down=ffn1_w_down, mix_norm=mix_norm, even_w_in=even_w_in,
             conv_a_w=conv_a_w, conv_a_b=conv_a_b, conv_a_ln_g=conv_a_ln_g, conv_a_ln_b=conv_a_ln_b, swa_sinks=swa_sinks,
             even_w_out=even_w_out, odd_w_in=odd_w_in, sc_conv_w=sc_conv_w, odd_w_out=odd_w_out, xa_norm=xa_norm,
             xa_mem_norm=xa_mem_norm, xa_wq=xa_wq, xa_wkv=xa_wkv, xa_wo=xa_wo, ffn2_norm=ffn2_norm, ffn2_w_gu=ffn2_w_gu,
             ffn2_w_down=ffn2_w_down, final_norm=final_norm)
    m = dict(ffn1_norm=m_ffn1_norm, ffn1_w_gu=m_ffn1_w_gu, ffn1_w_down=m_ffn1_w_down, mix_norm=m_mix_norm,
             even_w_in=m_even_w_in, conv_a_w=m_conv_a_w, conv_a_b=m_conv_a_b, conv_a_ln_g=m_conv_a_ln_g,
             conv_a_ln_b=m_conv_a_ln_b, swa_sinks=m_swa_sinks, even_w_out=m_even_w_out, odd_w_in=m_odd_w_in,
             sc_conv_w=m_sc_conv_w, odd_w_out=m_odd_w_out, xa_norm=m_xa_norm, xa_mem_norm=m_xa_mem_norm, xa_wq=m_xa_wq,
             xa_wkv=m_xa_wkv, xa_wo=m_xa_wo, ffn2_norm=m_ffn2_norm, ffn2_w_gu=m_ffn2_w_gu, ffn2_w_down=m_ffn2_w_down,
             final_norm=m_final_norm)
    v = dict(ffn1_norm=v_ffn1_norm, ffn1_w_gu=v_ffn1_w_gu, ffn1_w_down=v_ffn1_w_down, mix_norm=v_mix_norm,
             even_w_in=v_even_w_in, conv_a_w=v_conv_a_w, conv_a_b=v_conv_a_b, conv_a_ln_g=v_conv_a_ln_g,
             conv_a_ln_b=v_conv_a_ln_b, swa_sinks=v_swa_sinks, even_w_out=v_even_w_out, odd_w_in=v_odd_w_in,
             sc_conv_w=v_sc_conv_w, odd_w_out=v_odd_w_out, xa_norm=v_xa_norm, xa_mem_norm=v_xa_mem_norm, xa_wq=v_xa_wq,
             xa_wkv=v_xa_wkv, xa_wo=v_xa_wo, ffn2_norm=v_ffn2_norm, ffn2_w_gu=v_ffn2_w_gu, ffn2_w_down=v_ffn2_w_down,
             final_norm=v_final_norm)
    me = _flat_index(*_mesh_pos())

    conv_blob = jnp.concatenate([w["conv_a_w"].reshape(-1), w["sc_conv_w"].reshape(-1),
                                 jnp.zeros((8 * 1024 - 31 * 64 - 3 * 128,), F32)]).reshape(8, 1024)
    conv_all = all_gather(conv_blob, "gather_conv_weights").reshape(N_DEV, 8 * 1024)
    conv_a_full = jnp.transpose(conv_all[:, :31 * 64].reshape(N_DEV, 31, 64), (1, 0, 2)).reshape(31, 512)
    sc_full = jnp.transpose(conv_all[:, 31 * 64:31 * 64 + 3 * 128].reshape(N_DEV, 3, 128), (1, 0, 2)).reshape(3, 1024)

    def shard_rows(n, l):
        return (w[n][l].T if SPLIT_AXIS[n] == 1 else w[n][l]).astype(BF16)

    def with_own(land, own):
        return lax.dynamic_update_slice(land, own[None], (me, 0, 0))

    a_keys = STAGE_KEYS["A"]
    gathered_a = all_gather(jnp.concatenate([shard_rows(n, l) for n, l in a_keys], axis=0), "gather_weights_a")
    later = ("B", "C", "D", "E")
    later_keys = [k for s in later for k in STAGE_KEYS[s]]
    shards = [shard_rows(n, l) for n, l in later_keys]
    lands = [lax.empty((N_DEV,) + s.shape, BF16) for s in shards]
    handles, weight_token = exchange_start(shards, lands, False, [gathered_a, conv_all], "gather_start")
    weight_handles = dict(zip(later_keys, handles))

    def get_weights(stage, after):
        keys = STAGE_KEYS[stage]
        if stage == "A":
            out, off = {}, 0
            for n, l in keys:
                rows = w[n].shape[2] if SPLIT_AXIS[n] == 1 else w[n].shape[1]
                out[(n, l)] = gathered_a[:, off:off + rows, :].reshape(N_DEV * rows, BLOB_COLS)
                off += rows
            return out, weight_token
        got = exchange_wait([weight_handles[k] for k in keys], False, after, "gather_wait_" + stage.lower())
        return {k: with_own(land, own).reshape(-1, BLOB_COLS) for k, (own, land) in zip(keys, got)}, None

    grad_handles = {}

    def put_grads(stage, dws):
        srcs = [dw.reshape(N_DEV, -1, BLOB_COLS) for dw in dws.values()]
        handles, token = exchange_start(srcs, [lax.empty(s.shape, BF16) for s in srcs], True, [],
                                        "scatter_start_" + stage.lower())
        grad_handles[stage] = (handles, tuple(dws))
        return token

    P = dict(ffn1_norm=ffn1_norm, mix_norm=mix_norm, xa_norm=xa_norm, xa_mem_norm=xa_mem_norm, ffn2_norm=ffn2_norm,
             final_norm=final_norm, conv_a_w=conv_a_full, conv_a_b=conv_a_b[0], conv_a_ln_g=conv_a_ln_g[0],
             conv_a_ln_b=conv_a_ln_b[0], swa_sinks=swa_sinks[0], sc_conv_w=sc_full)

    loss_part, grad_x, dP = local_step(x[0], mem[0], loss_target[0], P, get_weights, put_grads)

    def finish_grads(stage, after):
        handles, keys = grad_handles[stage]
        got = exchange_wait(handles, True, after, "scatter_wait_" + stage.lower())
        out = {}
        for (n, l), (src, land) in zip(keys, got):
            own = lax.dynamic_slice(src, (me, 0, 0), (1,) + src.shape[1:])[0]
            part = ordered_sum(with_own(land, own), f"sum_grads_{n}_{l}")
            out[(n, l)] = part.T if SPLIT_AXIS[n] == 1 else part
        return out

    layer_grads = {**finish_grads("D", grad_x), **finish_grads("BC", grad_x)}

    grads, delta, new_m, new_v = {}, {}, {}, {}

    def update(n):
        shp = w[n].shape
        two_d = (shp[0] * shp[1], shp[2])
        d_, m_, v_ = adamw(w[n].reshape(two_d), grads[n].reshape(two_d), m[n].reshape(two_d), v[n].reshape(two_d),
                           "adamw_" + n)
        delta[n], new_m[n], new_v[n] = d_.reshape(shp), m_.reshape(shp), v_.reshape(shp)

    first_stage = tuple(n for n, _ in STAGE_KEYS["A"])
    for n in SPLIT_AXIS:
        if n not in first_stage:
            grads[n] = jnp.stack([layer_grads[(n, l)] for l in range(w[n].shape[0])], axis=0)
            update(n)
    layer_grads.update(finish_grads("A", delta["ffn2_w_gu"]))

    dP = dict(dP, loss=loss_part[0, 0])
    small = jnp.concatenate([
        _small_blob(dP)[:ROW_SC_CONV], dP["sc_conv_w"],
        jnp.concatenate([dP["conv_a_w"].reshape(-1), jnp.zeros((512,), F32)]).reshape(16, D_MODEL)], axis=0)
    small_all = all_gather(small, "gather_small_grads", dep=layer_grads[STAGE_KEYS["A"][-1]])
    small_sum = ordered_sum(small_all, "sum_small_grads", tr=SMALL_BLOB_ROWS)
    loss = small_sum[ROW_LNB_SINKS_LOSS, LOSS_COL]
    grads.update(_small_unblob(small_sum, {n: w[n].shape for n in WEIGHT_NAMES}))
    sc_g = small_sum[ROW_SC_CONV:ROW_SC_CONV + 3]
    grads["sc_conv_w"] = lax.dynamic_slice(sc_g, (0, me * 128), (3, 128)).reshape(w["sc_conv_w"].shape)
    cw_g = small_sum[ROW_CONV_W:].reshape(-1)[:31 * 512].reshape(31, 512)
    grads["conv_a_w"] = lax.dynamic_slice(cw_g, (0, me * 64), (31, 64)).reshape(w["conv_a_w"].shape)

    update("conv_a_w")
    update("sc_conv_w")
    for n in first_stage:
        grads[n] = jnp.stack([layer_grads[(n, l)] for l in range(w[n].shape[0])], axis=0)
        update(n)
    d_, m_, v_ = adamw(_small_blob(w), small_sum[:SMALL_ADAM_ROWS], _small_blob(m), _small_blob(v), "adamw_small",
                       tr=SMALL_ADAM_ROWS)
    shapes = {n: w[n].shape for n in WEIGHT_NAMES}
    delta.update(_small_unblob(d_, shapes))
    new_m.update(_small_unblob(m_, shapes))
    new_v.update(_small_unblob(v_, shapes))

    return (loss, grad_x[None], *[grads[n] for n in WEIGHT_NAMES], *[delta[n] for n in WEIGHT_NAMES],
            *[new_m[n] for n in WEIGHT_NAMES], *[new_v[n] for n in WEIGHT_NAMES])
```

```python
import functools

import jax
import jax.numpy as jnp
from jax import lax
from jax.experimental import pallas as pl
from jax.experimental.pallas import tpu as pltpu

F32 = jnp.float32
BF16 = jnp.bfloat16

D_MODEL = 1024
D_FF = 2816
CONV_A_CH = 512
CONV_A_WIDTH = 31
SWA_HEADS = 8
SWA_KV_HEADS = 2
SWA_GROUP = SWA_HEADS // SWA_KV_HEADS
HEAD_DIM = 64
WINDOW = 128
SC_CH = 1024
XA_HEADS = 4
XA_HEAD_DIM = D_MODEL // XA_HEADS
RMS_EPS = 1e-6
LN_EPS = 1e-5
ADAM_LR = 0.001
ADAM_B1 = 0.9
ADAM_B2 = 0.999
ADAM_EPS = 1e-08
ADAM_WD = 0.01
ADAM_STEP = 10
N_DEV = 8

V7X_VMEM_BYTES = 64 * 1024 * 1024
VMEM_LIMIT = V7X_VMEM_BYTES - 8 * 1024 * 1024
CONV_HALO = 32
SC_HALO = 8
NEG_BIG = -1e30

SPLIT_AXIS = dict(ffn1_w_gu=1, ffn1_w_down=0, even_w_in=1, even_w_out=0, odd_w_in=1, odd_w_out=0, xa_wq=0, xa_wkv=1, xa_wo=0,
                  ffn2_w_gu=1, ffn2_w_down=0)
STAGE_KEYS = dict(
    A=(("ffn1_w_gu", 0), ("ffn1_w_down", 0)),
    B=(("even_w_in", 0), ("even_w_out", 0), ("xa_wq", 0), ("xa_wkv", 0), ("xa_wo", 0)),
    C=(("ffn2_w_gu", 0), ("ffn2_w_down", 0)),
    D=(("ffn1_w_gu", 1), ("ffn1_w_down", 1), ("odd_w_in", 0), ("odd_w_out", 0), ("xa_wq", 1), ("xa_wkv", 1), ("xa_wo", 1)),
    E=(("ffn2_w_gu", 1), ("ffn2_w_down", 1)))


def _params(n_axes):
    return pltpu.CompilerParams(dimension_semantics=("arbitrary",) * n_axes, vmem_limit_bytes=VMEM_LIMIT)


def _tile(n, pref):
    t = min(n, pref)
    assert n % t == 0, (n, pref)
    return t


def _dot(a, b):
    return jnp.dot(a, b, preferred_element_type=F32)


def _dot_nt(a, b):
    return lax.dot_general(a, b, (((1,), (1,)), ((), ())), preferred_element_type=F32)


def _dot_tn(a, b):
    return lax.dot_general(a, b, (((0,), (0,)), ((), ())), preferred_element_type=F32)


def _sigmoid(x):
    return 0.5 * jnp.tanh(0.5 * x) + 0.5


ANY_SPEC = pl.BlockSpec(memory_space=pl.ANY)


def _with_dep(body, n_in, dep):
    if dep is None:
        return body, [], []
    return (lambda *refs: body(*refs[:n_in], *refs[n_in + 1:])), [ANY_SPEC], [dep]


def rmsnorm(h, g, name, tm=1024, dep=None):
    T, K = h.shape
    tm = _tile(T, tm)

    def kern(h_ref, g_ref, u_ref):
        x = h_ref[...]
        r = lax.rsqrt(jnp.mean(x * x, axis=-1, keepdims=True) + RMS_EPS)
        u_ref[...] = ((x * r) * g_ref[...]).astype(BF16)

    body, dep_spec, dep_arg = _with_dep(kern, 2, dep)
    return pl.pallas_call(
        body, name=name, grid=(T // tm,),
        in_specs=[pl.BlockSpec((tm, K), lambda i: (i, 0)), pl.BlockSpec((1, K), lambda i: (0, 0))] + dep_spec,
        out_specs=pl.BlockSpec((tm, K), lambda i: (i, 0)),
        out_shape=jax.ShapeDtypeStruct((T, K), BF16),
        compiler_params=_params(1),
    )(h, g, *dep_arg)


def matmul(a, w, out_dtype, name, tn, tm=2048, transposed=True, n_tiles=None, first_tile=0):
    T, K = a.shape
    N = w.shape[0] if transposed else w.shape[1]
    n_tiles = N // tn if n_tiles is None else n_tiles
    tm = _tile(T, tm)
    mm = _dot_nt if transposed else _dot

    def body(a_ref, w_ref, z_ref):
        z_ref[...] = mm(a_ref[...], w_ref[...]).astype(z_ref.dtype)

    w_spec = (pl.BlockSpec((tn, K), lambda i, j: (first_tile + j, 0)) if transposed
              else pl.BlockSpec((K, tn), lambda i, j: (0, first_tile + j)))
    return pl.pallas_call(
        body, name=name, grid=(T // tm, n_tiles),
        in_specs=[pl.BlockSpec((tm, K), lambda i, j: (i, 0)), w_spec],
        out_specs=pl.BlockSpec((tm, tn), lambda i, j: (i, j)),
        out_shape=jax.ShapeDtypeStruct((T, n_tiles * tn), out_dtype),
        compiler_params=_params(2),
    )(a, w)


def norm_matmul(h, g, w, out_dtype, name, tn, dep=None, transposed=True, u=None):
    u = rmsnorm(h, g, name + "_norm", dep=dep) if u is None else u
    return matmul(u, w, out_dtype, name, tn, transposed=transposed), u


def matmul_residual(a, w, res, g_next, name, tm=1024):
    T, K = a.shape
    N = w.shape[1]
    tm = _tile(T, tm)

    def body(a_ref, w_ref, r_ref, g_ref, o_ref, u_ref):
        x = r_ref[...] + _dot(a_ref[...], w_ref[...])
        o_ref[...] = x
        r = lax.rsqrt(jnp.mean(x * x, axis=-1, keepdims=True) + RMS_EPS)
        u_ref[...] = ((x * r) * g_ref[...]).astype(BF16)

    return pl.pallas_call(
        body, name=name, grid=(T // tm,),
        in_specs=[pl.BlockSpec((tm, K), lambda i: (i, 0)),
                  pl.BlockSpec((K, N), lambda i: (0, 0)),
                  pl.BlockSpec((tm, N), lambda i: (i, 0)),
                  pl.BlockSpec((1, N), lambda i: (0, 0))],
        out_specs=[pl.BlockSpec((tm, N), lambda i: (i, 0)), pl.BlockSpec((tm, N), lambda i: (i, 0))],
        out_shape=[jax.ShapeDtypeStruct((T, N), F32), jax.ShapeDtypeStruct((T, N), BF16)],
        compiler_params=_params(1),
    )(a, w, res, g_next)


def matmul_nt(dy, w, out_dtype, name, tm=1024):
    T, N = dy.shape
    K = w.shape[0]
    tm = _tile(T, tm)

    def body(dy_ref, w_ref, o_ref):
        o_ref[...] = _dot_nt(dy_ref[...].astype(BF16), w_ref[...]).astype(o_ref.dtype)

    return pl.pallas_call(
        body, name=name, grid=(T // tm,),
        in_specs=[pl.BlockSpec((tm, N), lambda i: (i, 0)),
                  pl.BlockSpec((K, N), lambda i: (0, 0))],
        out_specs=pl.BlockSpec((tm, K), lambda i: (i, 0)),
        out_shape=jax.ShapeDtypeStruct((T, K), out_dtype),
        compiler_params=_params(1),
    )(dy, w)


def matmul_norm_bwd(dz, w, h, g, dh_in, name, tm=512, transposed=True, dep=None):
    T, N = dz.shape
    K = h.shape[1]
    tm = _tile(T, tm)

    def kern(dz_ref, w_ref, h_ref, g_ref, dhin_ref, dh_ref, dg_ref):
        @pl.when(pl.program_id(0) == 0)
        def _():
            dg_ref[...] = jnp.zeros_like(dg_ref)

        mm = _dot if transposed else _dot_nt
        du = mm(dz_ref[...], w_ref[...])
        x = h_ref[...]
        r = lax.rsqrt(jnp.mean(x * x, axis=-1, keepdims=True) + RMS_EPS)
        xh = x * r
        dg_ref[...] += jnp.sum(du * xh, axis=0, keepdims=True)
        dxh = du * g_ref[...]
        dh_ref[...] = dhin_ref[...] + r * (dxh - xh * jnp.mean(dxh * xh, axis=-1, keepdims=True))

    body, dep_spec, dep_arg = _with_dep(kern, 5, dep)
    return pl.pallas_call(
        body, name=name, grid=(T // tm,),
        in_specs=[pl.BlockSpec((tm, N), lambda i: (i, 0)),
                  pl.BlockSpec(w.shape, lambda i: (0, 0)),
                  pl.BlockSpec((tm, K), lambda i: (i, 0)),
                  pl.BlockSpec((1, K), lambda i: (0, 0)),
                  pl.BlockSpec((tm, K), lambda i: (i, 0))] + dep_spec,
        out_specs=[pl.BlockSpec((tm, K), lambda i: (i, 0)),
                   pl.BlockSpec((1, K), lambda i: (0, 0))],
        out_shape=[jax.ShapeDtypeStruct((T, K), F32), jax.ShapeDtypeStruct((1, K), F32)],
        compiler_params=_params(1),
    )(dz, w, h, g, dh_in, *dep_arg)


def matmul_tn(x, dy, name, scale=1.0, tk=None, tn=None, tt=1024):
    T, K = x.shape
    N = dy.shape[1]
    tk = K if tk is None else tk
    tn = N if tn is None else tn
    tt = _tile(T, tt)
    nt = T // tt

    def body(x_ref, dy_ref, o_ref, acc_ref):
        t = pl.program_id(2)

        @pl.when(t == 0)
        def _():
            acc_ref[...] = jnp.zeros_like(acc_ref)

        acc_ref[...] += _dot_tn(x_ref[...].astype(BF16), dy_ref[...].astype(BF16))

        @pl.when(t == nt - 1)
        def _():
            o_ref[...] = (acc_ref[...] * scale).astype(o_ref.dtype)

    return pl.pallas_call(
        body, name=name, grid=(K // tk, N // tn, nt),
        in_specs=[pl.BlockSpec((tt, tk), lambda a, b, t: (t, a)),
                  pl.BlockSpec((tt, tn), lambda a, b, t: (t, b))],
        out_specs=pl.BlockSpec((tk, tn), lambda a, b, t: (a, b)),
        out_shape=jax.ShapeDtypeStruct((K, N), BF16),
        scratch_shapes=[pltpu.VMEM((tk, tn), F32)],
        compiler_params=_params(3),
    )(x, dy)


def ffn_down(gu, wd, res, g_next, name, tm=512):
    T = gu.shape[0]
    F = gu.shape[1] // 2
    N = wd.shape[1]
    tm = _tile(T, tm)
    with_next = g_next is not None

    def body(g_ref, up_ref, w_ref, r_ref, *rest):
        o_ref, a_ref = rest[-3:-1] if with_next else rest[-2:]
        g = g_ref[...].astype(F32)
        a_ref[...] = ((g * _sigmoid(g)) * up_ref[...].astype(F32)).astype(BF16)
        x = r_ref[...] + 0.5 * _dot(a_ref[...], w_ref[...])
        o_ref[...] = x
        if with_next:
            r = lax.rsqrt(jnp.mean(x * x, axis=-1, keepdims=True) + RMS_EPS)
            rest[-1][...] = ((x * r) * rest[0][...]).astype(BF16)

    row = lambda width: pl.BlockSpec((tm, width), lambda i: (i, 0))
    res = pl.pallas_call(
        body, name=name, grid=(T // tm,),
        in_specs=[row(F), pl.BlockSpec((tm, F), lambda i: (i, 1)), pl.BlockSpec((F, N), lambda i: (0, 0)), row(N)]
        + ([pl.BlockSpec((1, N), lambda i: (0, 0))] if with_next else []),
        out_specs=[row(N), row(F)] + ([row(N)] if with_next else []),
        out_shape=[jax.ShapeDtypeStruct((T, N), F32), jax.ShapeDtypeStruct((T, F), BF16)]
        + ([jax.ShapeDtypeStruct((T, N), BF16)] if with_next else []),
        compiler_params=_params(1),
    )(gu, gu, wd, res, *([g_next] if with_next else []))
    return (res[0], res[1], res[2]) if with_next else (res[0], res[1], None)


def ffn_down_bwd(dy, wd, gu, name, tm=512, dep=None):
    T, N = dy.shape
    F = wd.shape[0]
    tm = _tile(T, tm)

    def kern(dy_ref, w_ref, g_ref, up_ref, o_ref):
        da = 0.5 * _dot_nt(dy_ref[...].astype(BF16), w_ref[...])
        g = g_ref[...].astype(F32)
        up = up_ref[...].astype(F32)
        s = _sigmoid(g)
        o_ref[:, :F] = (da * up * (s * (1.0 + g * (1.0 - s)))).astype(BF16)
        o_ref[:, F:] = (da * (g * s)).astype(BF16)

    body, dep_spec, dep_arg = _with_dep(kern, 4, dep)
    return pl.pallas_call(
        body, name=name, grid=(T // tm,),
        in_specs=[pl.BlockSpec((tm, N), lambda i: (i, 0)),
                  pl.BlockSpec((F, N), lambda i: (0, 0)),
                  pl.BlockSpec((tm, F), lambda i: (i, 0)),
                  pl.BlockSpec((tm, F), lambda i: (i, 1))] + dep_spec,
        out_specs=pl.BlockSpec((tm, 2 * F), lambda i: (i, 0)),
        out_shape=jax.ShapeDtypeStruct((T, 2 * F), BF16),
        compiler_params=_params(1),
    )(dy, wd, gu, gu, *dep_arg)


def ffn_forward(h, g, w_gu, w_down, name, dep=None, u=None, g_next=None):
    gu, u = norm_matmul(h, g, w_gu, BF16, name + "_gu", D_FF // 2, dep=dep, u=u)
    h_out, a, u_next = ffn_down(gu, w_down, h, g_next, name + "_down")
    return h_out, u_next, (h, u, gu, a)


def ffn_backward(dy, saved, g, w_gu, w_down, name, dep=None, emit=None):
    h, u, gu, a = saved
    dgu = ffn_down_bwd(dy, w_down, gu, name + "_ddown", dep=dep)
    d_w_down = matmul_tn(a, dy, name + "_dwd", scale=0.5, tk=D_FF // 2)
    d_w_gu = matmul_tn(dgu, u, name + "_dwgu", tk=D_FF)
    dh, dg = matmul_norm_bwd(dgu, w_gu, h, g, dy, name + "_dx", dep=emit(d_w_gu, d_w_down))
    return dh, dg


CONV_ROW_CHUNK = 32
CONV_LANES = 128
CONV_X_OFFSETS = tuple(CONV_HALO - (CONV_A_WIDTH - 1) + k for k in range(CONV_A_WIDTH))
CONV_D_OFFSETS = tuple(CONV_A_WIDTH - 1 - k for k in range(CONV_A_WIDTH))


def _build_phases(ref, phase_ref, n_rows):
    for r in range(1, 8):
        phase_ref[r - 1] = ref[pl.ds(r, n_rows - 8), :]


def _tap_values(ref, phase_ref, offsets, n, base, lanes):
    out = {}
    for r in range(8):
        qs = sorted(o // 8 for o in offsets if o % 8 == r)
        if qs:
            lo, hi = qs[0], qs[-1]
            rows = pl.ds(base + 8 * lo, n + 8 * (hi - lo))
            span = ref[rows, lanes] if r == 0 else phase_ref[r - 1, rows, lanes]
            for q in qs:
                out[8 * q + r] = span[8 * (q - lo):8 * (q - lo) + n]
    return out


def conformer_conv_fwd(z, cw, cb, lg, lb, name, tm=512):
    T = z.shape[0]
    C = CONV_A_CH
    tm = _tile(T, tm)
    hb = tm // CONV_HALO
    CH = CONV_ROW_CHUNK
    KW = CONV_A_WIDTH

    def body(v_ref, gt_ref, pv_ref, pg_ref, cw_ref, cb_ref, lg_ref, lb_ref, o_ref, conv_ref, xs_ref, xph_ref):
        i = pl.program_id(0)
        prev = pv_ref[...] * _sigmoid(pg_ref[...])
        xs_ref[0:CONV_HALO, :] = jnp.where(i > 0, prev, 0.0)
        xs_ref[CONV_HALO:, :] = v_ref[...] * _sigmoid(gt_ref[...])
        _build_phases(xs_ref, xph_ref, tm + CONV_HALO)

        def chunk(c, carry):
            off = pl.multiple_of(c * CH, CH)
            for l0 in range(0, C, CONV_LANES):
                lanes = slice(l0, l0 + CONV_LANES)
                taps = _tap_values(xs_ref, xph_ref, CONV_X_OFFSETS, CH, off, lanes)
                acc = jnp.zeros((CH, CONV_LANES), F32) + cb_ref[:, lanes]
                for k in range(KW):
                    acc = acc + cw_ref[k:k + 1, lanes] * taps[CONV_X_OFFSETS[k]]
                conv_ref[pl.ds(off, CH), lanes] = acc
            return carry

        lax.fori_loop(0, tm // CH, chunk, 0)
        acc = conv_ref[...]
        mu = jnp.mean(acc, axis=-1, keepdims=True)
        xc = acc - mu
        var = jnp.mean(xc * xc, axis=-1, keepdims=True)
        y = (xc * lax.rsqrt(var + LN_EPS)) * lg_ref[...] + lb_ref[...]
        o_ref[...] = (y * _sigmoid(y)).astype(BF16)

    return pl.pallas_call(
        body, name=name, grid=(T // tm,),
        in_specs=[pl.BlockSpec((tm, C), lambda i: (i, 0)),
                  pl.BlockSpec((tm, C), lambda i: (i, 1)),
                  pl.BlockSpec((CONV_HALO, C), lambda i: (jnp.maximum(i * hb - 1, 0), 0)),
                  pl.BlockSpec((CONV_HALO, C), lambda i: (jnp.maximum(i * hb - 1, 0), 1)),
                  pl.BlockSpec((32, C), lambda i: (0, 0)),
                  pl.BlockSpec((1, C), lambda i: (0, 0)),
                  pl.BlockSpec((1, C), lambda i: (0, 0)),
                  pl.BlockSpec((1, C), lambda i: (0, 0))],
        out_specs=[pl.BlockSpec((tm, C), lambda i: (i, 0)), pl.BlockSpec((tm, C), lambda i: (i, 0))],
        out_shape=[jax.ShapeDtypeStruct((T, C), BF16), jax.ShapeDtypeStruct((T, C), F32)],
        scratch_shapes=[pltpu.VMEM((tm + CONV_HALO, C), F32), pltpu.VMEM((7, tm + CONV_HALO - 8, C), F32)],
        compiler_params=_params(1),
    )(z, z, z, z, cw, cb, lg, lb)


def conformer_conv_bwd(z, conv_out, dm, cw, lg, lb, name, tm=512):
    T = z.shape[0]
    C = CONV_A_CH
    tm = _tile(T, tm)
    hb = tm // CONV_HALO
    n_tiles = T // tm
    last_halo = T // CONV_HALO - 1
    R = tm + CONV_HALO
    KW = CONV_A_WIDTH
    CH = CONV_ROW_CHUNK

    def body(v_ref, gt_ref, pv_ref, pg_ref, cv_ref, ncv_ref, do_ref, ndo_ref, cw_ref, lg_ref, lb_ref,
             dz_ref, dcw_ref, dcb_ref, dlg_ref, dlb_ref, xs_ref, xph_ref, ds_ref, dph_ref):
        i = pl.program_id(0)

        @pl.when(i == 0)
        def _():
            dcw_ref[...] = jnp.zeros_like(dcw_ref)
            dcb_ref[...] = jnp.zeros_like(dcb_ref)
            dlg_ref[...] = jnp.zeros_like(dlg_ref)
            dlb_ref[...] = jnp.zeros_like(dlb_ref)

        prev = pv_ref[...] * _sigmoid(pg_ref[...])
        xs_ref[0:CONV_HALO, :] = jnp.where(i > 0, prev, 0.0)
        xs_ref[CONV_HALO:, :] = v_ref[...] * _sigmoid(gt_ref[...])
        _build_phases(xs_ref, xph_ref, tm + CONV_HALO)

        acc = jnp.concatenate([cv_ref[...], ncv_ref[...]], axis=0)
        mu = jnp.mean(acc, axis=-1, keepdims=True)
        xc = acc - mu
        rstd = lax.rsqrt(jnp.mean(xc * xc, axis=-1, keepdims=True) + LN_EPS)
        xh = xc * rstd
        y = xh * lg_ref[...] + lb_ref[...]
        s = _sigmoid(y)
        dout = jnp.concatenate([do_ref[...], jnp.where(i < n_tiles - 1, ndo_ref[...], 0.0)], axis=0)
        dy = dout * (s * (1.0 + y * (1.0 - s)))
        dxh = dy * lg_ref[...]
        dconv = rstd * (dxh - jnp.mean(dxh, axis=-1, keepdims=True) - xh * jnp.mean(dxh * xh, axis=-1, keepdims=True))
        ds_ref[...] = dconv
        dlg_ref[...] += jnp.sum(dy[:tm] * xh[:tm], axis=0, keepdims=True)
        dlb_ref[...] += jnp.sum(dy[:tm], axis=0, keepdims=True)
        dcb_ref[...] += jnp.sum(dconv[:tm], axis=0, keepdims=True)
        _build_phases(ds_ref, dph_ref, R)

        for l0 in range(0, C, CONV_LANES):
            lanes = slice(l0, l0 + CONV_LANES)

            def taps_bwd(c, wacc, l0=l0, lanes=lanes):
                off = pl.multiple_of(c * CH, CH)
                x_taps = _tap_values(xs_ref, xph_ref, CONV_X_OFFSETS, CH, off, lanes)
                d_taps = _tap_values(ds_ref, dph_ref, CONV_D_OFFSETS, CH, off, lanes)
                dc = ds_ref[pl.ds(off, CH), lanes]
                dglu = jnp.zeros((CH, CONV_LANES), F32)
                new = []
                for k in range(KW):
                    dglu = dglu + cw_ref[k:k + 1, lanes] * d_taps[CONV_D_OFFSETS[k]]
                    prod = dc * x_taps[CONV_X_OFFSETS[k]]
                    new.append(wacc[k] + ((prod[0:8] + prod[8:16]) + (prod[16:24] + prod[24:32])))
                val = v_ref[pl.ds(off, CH), lanes]
                sg = _sigmoid(gt_ref[pl.ds(off, CH), lanes])
                dz_ref[pl.ds(off, CH), lanes] = (dglu * sg).astype(BF16)
                dz_ref[pl.ds(off, CH), C + l0:C + l0 + CONV_LANES] = (dglu * val * sg * (1.0 - sg)).astype(BF16)
                return tuple(new)

            wacc = lax.fori_loop(0, tm // CH, taps_bwd, tuple(jnp.zeros((8, CONV_LANES), F32) for _ in range(KW)))
            for k in range(KW):
                dcw_ref[k:k + 1, lanes] += jnp.sum(wacc[k], axis=0, keepdims=True)

    prev_map = lambda i: jnp.maximum(i * hb - 1, 0)
    next_map = lambda i: jnp.minimum((i + 1) * hb, last_halo)
    return pl.pallas_call(
        body, name=name, grid=(n_tiles,),
        in_specs=[pl.BlockSpec((tm, C), lambda i: (i, 0)),
                  pl.BlockSpec((tm, C), lambda i: (i, 1)),
                  pl.BlockSpec((CONV_HALO, C), lambda i: (prev_map(i), 0)),
                  pl.BlockSpec((CONV_HALO, C), lambda i: (prev_map(i), 1)),
                  pl.BlockSpec((tm, C), lambda i: (i, 0)),
                  pl.BlockSpec((CONV_HALO, C), lambda i: (next_map(i), 0)),
                  pl.BlockSpec((tm, C), lambda i: (i, 0)),
                  pl.BlockSpec((CONV_HALO, C), lambda i: (next_map(i), 0)),
                  pl.BlockSpec((32, C), lambda i: (0, 0)),
                  pl.BlockSpec((1, C), lambda i: (0, 0)),
                  pl.BlockSpec((1, C), lambda i: (0, 0))],
        out_specs=[pl.BlockSpec((tm, 2 * C), lambda i: (i, 0)),
                   pl.BlockSpec((32, C), lambda i: (0, 0)),
                   pl.BlockSpec((1, C), lambda i: (0, 0)),
                   pl.BlockSpec((1, C), lambda i: (0, 0)),
                   pl.BlockSpec((1, C), lambda i: (0, 0))],
        out_shape=[jax.ShapeDtypeStruct((T, 2 * C), BF16),
                   jax.ShapeDtypeStruct((32, C), F32),
                   jax.ShapeDtypeStruct((1, C), F32),
                   jax.ShapeDtypeStruct((1, C), F32),
                   jax.ShapeDtypeStruct((1, C), F32)],
        scratch_shapes=[pltpu.VMEM((tm + CONV_HALO, C), F32), pltpu.VMEM((7, tm + CONV_HALO - 8, C), F32),
                        pltpu.VMEM((R, C), F32), pltpu.VMEM((7, R - 8, C), F32)],
        compiler_params=_params(1),
    )(z, z, z, z, conv_out, conv_out, dm, dm, cw, lg, lb)


def _swa_scores(q_h, kk_h, slope, bias_dist, valid, sink):
    s = _dot_nt(q_h, kk_h) * (HEAD_DIM ** -0.5) - slope * bias_dist
    s = jnp.where(valid, s, NEG_BIG)
    m = jnp.maximum(jnp.max(s, axis=-1, keepdims=True), sink)
    p = jnp.exp(s - m)
    e_sink = jnp.exp(sink - m)
    inv = 1.0 / (jnp.sum(p, axis=-1, keepdims=True) + e_sink)
    return p * inv, e_sink * inv


def _swa_mask(r0):
    qi = lax.broadcasted_iota(jnp.int32, (WINDOW, 2 * WINDOW), 0)
    kj = lax.broadcasted_iota(jnp.int32, (WINDOW, 2 * WINDOW), 1)
    dist = qi + WINDOW - kj
    valid = (dist >= 0) & (dist < WINDOW) & (r0 - WINDOW + kj >= 0)
    return dist.astype(F32), valid


def swa_fwd(z, kpad, vpad, sinks, name, tq=512):
    T = z.shape[0]
    tq = _tile(T, tq)
    HQ = SWA_HEADS * HEAD_DIM

    def body(sink_ref, q_ref, k_ref, v_ref, o_ref):
        i = pl.program_id(0)
        for sub in range(tq // WINDOW):
            r0 = pl.multiple_of(i * tq + sub * WINDOW, WINDOW)
            kk = k_ref[pl.ds(r0, 2 * WINDOW), :]
            vv = v_ref[pl.ds(r0, 2 * WINDOW), :]
            qb = q_ref[sub * WINDOW:(sub + 1) * WINDOW, :].astype(BF16)
            dist, valid = _swa_mask(r0)
            outs = []
            for h in range(SWA_HEADS):
                kh = h // SWA_GROUP
                ks = slice(kh * HEAD_DIM, (kh + 1) * HEAD_DIM)
                pn, _ = _swa_scores(qb[:, h * HEAD_DIM:(h + 1) * HEAD_DIM], kk[:, ks], 2.0 ** (-(h + 1)), dist, valid,
                                    sink_ref[h])
                outs.append(_dot(pn.astype(BF16), vv[:, ks]))
            o_ref[sub * WINDOW:(sub + 1) * WINDOW, :] = jnp.concatenate(outs, axis=-1).astype(BF16)

    return pl.pallas_call(
        body, name=name, grid=(T // tq,),
        in_specs=[pl.BlockSpec(memory_space=pltpu.SMEM),
                  pl.BlockSpec((tq, HQ), lambda i: (i, 2)),
                  pl.BlockSpec((T + WINDOW, 2 * HEAD_DIM), lambda i: (0, 0)),
                  pl.BlockSpec((T + WINDOW, 2 * HEAD_DIM), lambda i: (0, 0))],
        out_specs=pl.BlockSpec((tq, HQ), lambda i: (i, 0)),
        out_shape=jax.ShapeDtypeStruct((T, HQ), BF16),
        compiler_params=_params(1),
    )(sinks, z, kpad, vpad)


def swa_bwd(z, kpad, vpad, sinks, dm, name, tq=512):
    T = z.shape[0]
    tq = _tile(T, tq)
    HQ = SWA_HEADS * HEAD_DIM
    scale = HEAD_DIM ** -0.5

    def body(sink_ref, q_ref, k_ref, v_ref, do_ref, dq_ref, dk_ref, dv_ref, dsink_ref):
        i = pl.program_id(0)

        @pl.when(i == 0)
        def _():
            dk_ref[...] = jnp.zeros_like(dk_ref)
            dv_ref[...] = jnp.zeros_like(dv_ref)
            dsink_ref[...] = jnp.zeros_like(dsink_ref)

        for sub in range(tq // WINDOW):
            r0 = pl.multiple_of(i * tq + sub * WINDOW, WINDOW)
            kk = k_ref[pl.ds(r0, 2 * WINDOW), :]
            vv = v_ref[pl.ds(r0, 2 * WINDOW), :]
            rows = slice(sub * WINDOW, (sub + 1) * WINDOW)
            qb = q_ref[rows, :].astype(BF16)
            dob = do_ref[rows, :].astype(BF16)
            dist, valid = _swa_mask(r0)
            dqs, dks, dvs = [], [], []
            for kh in range(SWA_KV_HEADS):
                ks = slice(kh * HEAD_DIM, (kh + 1) * HEAD_DIM)
                dk_acc = jnp.zeros((2 * WINDOW, HEAD_DIM), F32)
                dv_acc = jnp.zeros((2 * WINDOW, HEAD_DIM), F32)
                for g in range(SWA_GROUP):
                    h = kh * SWA_GROUP + g
                    hs = slice(h * HEAD_DIM, (h + 1) * HEAD_DIM)
                    pn, p_sink = _swa_scores(qb[:, hs], kk[:, ks], 2.0 ** (-(h + 1)), dist, valid, sink_ref[h])
                    dp = _dot_nt(dob[:, hs], vv[:, ks])
                    delta = jnp.sum(pn * dp, axis=-1, keepdims=True)
                    ds = (pn * (dp - delta)).astype(BF16)
                    dqs.append(_dot(ds, kk[:, ks]) * scale)
                    dk_acc = dk_acc + _dot_tn(ds, qb[:, hs]) * scale
                    dv_acc = dv_acc + _dot_tn(pn.astype(BF16), dob[:, hs])
                    dsink_ref[h:h + 1, :] += jnp.zeros((1, 128), F32) - jnp.sum(p_sink * delta)
                dks.append(dk_acc)
                dvs.append(dv_acc)
            dq_ref[rows, :] = jnp.concatenate(dqs, axis=-1).astype(BF16)
            dk_ref[pl.ds(r0, 2 * WINDOW), :] += jnp.concatenate(dks, axis=-1)
            dv_ref[pl.ds(r0, 2 * WINDOW), :] += jnp.concatenate(dvs, axis=-1)

    kv_spec = pl.BlockSpec((T + WINDOW, 2 * HEAD_DIM), lambda i: (0, 0))
    return pl.pallas_call(
        body, name=name, grid=(T // tq,),
        in_specs=[pl.BlockSpec(memory_space=pltpu.SMEM),
                  pl.BlockSpec((tq, HQ), lambda i: (i, 2)),
                  kv_spec, kv_spec,
                  pl.BlockSpec((tq, HQ), lambda i: (i, 1))],
        out_specs=[pl.BlockSpec((tq, HQ), lambda i: (i, 0)),
                   kv_spec, kv_spec,
                   pl.BlockSpec((SWA_HEADS, 128), lambda i: (0, 0))],
        out_shape=[jax.ShapeDtypeStruct((T, HQ), BF16),
                   jax.ShapeDtypeStruct((T + WINDOW, 2 * HEAD_DIM), F32),
                   jax.ShapeDtypeStruct((T + WINDOW, 2 * HEAD_DIM), F32),
                   jax.ShapeDtypeStruct((SWA_HEADS, 128), F32)],
        compiler_params=_params(1),
    )(sinks, z, kpad, vpad, dm)


def short_conv_fwd(z, w, name, tm=512):
    T = z.shape[0]
    C = SC_CH
    tm = _tile(T, tm)
    hb = tm // SC_HALO

    def body(b_ref, c_ref, v_ref, pc_ref, pv_ref, w_ref, o_ref, xs_ref):
        i = pl.program_id(0)
        xs_ref[0:SC_HALO, :] = jnp.where(i > 0, pc_ref[...] * pv_ref[...], 0.0)
        xs_ref[SC_HALO:, :] = c_ref[...] * v_ref[...]
        conv = jnp.zeros((tm, C), F32)
        for k in range(3):
            conv = conv + w_ref[k:k + 1, :] * xs_ref[pl.ds(SC_HALO - 2 + k, tm), :]
        o_ref[...] = (b_ref[...] * conv).astype(BF16)

    prev_map = lambda i: jnp.maximum(i * hb - 1, 0)
    return pl.pallas_call(
        body, name=name, grid=(T // tm,),
        in_specs=[pl.BlockSpec((tm, C), lambda i: (i, 0)),
                  pl.BlockSpec((tm, C), lambda i: (i, 1)),
                  pl.BlockSpec((tm, C), lambda i: (i, 2)),
                  pl.BlockSpec((SC_HALO, C), lambda i: (prev_map(i), 1)),
                  pl.BlockSpec((SC_HALO, C), lambda i: (prev_map(i), 2)),
                  pl.BlockSpec((8, C), lambda i: (0, 0))],
        out_specs=pl.BlockSpec((tm, C), lambda i: (i, 0)),
        out_shape=jax.ShapeDtypeStruct((T, C), BF16),
        scratch_shapes=[pltpu.VMEM((tm + SC_HALO, C), F32)],
        compiler_params=_params(1),
    )(z, z, z, z, z, w)


def short_conv_bwd(z, dm, w, name, tm=512):
    T = z.shape[0]
    C = SC_CH
    tm = _tile(T, tm)
    hb = tm // SC_HALO
    n_tiles = T // tm
    last_halo = T // SC_HALO - 1
    R = tm + SC_HALO

    def body(b_ref, c_ref, v_ref, pc_ref, pv_ref, nb_ref, do_ref, ndo_ref, w_ref, dz_ref, dw_ref, xs_ref, ds_ref):
        i = pl.program_id(0)

        @pl.when(i == 0)
        def _():
            dw_ref[...] = jnp.zeros_like(dw_ref)

        c = c_ref[...]
        val = v_ref[...]
        dout = do_ref[...]
        xs_ref[0:SC_HALO, :] = jnp.where(i > 0, pc_ref[...] * pv_ref[...], 0.0)
        xs_ref[SC_HALO:, :] = c * val
        dconv = dout * b_ref[...]
        ds_ref[0:tm, :] = dconv
        ds_ref[tm:, :] = jnp.where(i < n_tiles - 1, ndo_ref[...] * nb_ref[...], 0.0)
        conv = jnp.zeros((tm, C), F32)
        dcv = jnp.zeros((tm, C), F32)
        for k in range(3):
            xk = xs_ref[pl.ds(SC_HALO - 2 + k, tm), :]
            conv = conv + w_ref[k:k + 1, :] * xk
            dw_ref[k:k + 1, :] += jnp.sum(dconv * xk, axis=0, keepdims=True)
            dcv = dcv + w_ref[k:k + 1, :] * ds_ref[pl.ds(2 - k, tm), :]
        dz_ref[:, 0:C] = (dout * conv).astype(BF16)
        dz_ref[:, C:2 * C] = (dcv * val).astype(BF16)
        dz_ref[:, 2 * C:] = (dcv * c).astype(BF16)

    prev_map = lambda i: jnp.maximum(i * hb - 1, 0)
    next_map = lambda i: jnp.minimum((i + 1) * hb, last_halo)
    return pl.pallas_call(
        body, name=name, grid=(n_tiles,),
        in_specs=[pl.BlockSpec((tm, C), lambda i: (i, 0)),
                  pl.BlockSpec((tm, C), lambda i: (i, 1)),
                  pl.BlockSpec((tm, C), lambda i: (i, 2)),
                  pl.BlockSpec((SC_HALO, C), lambda i: (prev_map(i), 1)),
                  pl.BlockSpec((SC_HALO, C), lambda i: (prev_map(i), 2)),
                  pl.BlockSpec((SC_HALO, C), lambda i: (next_map(i), 0)),
                  pl.BlockSpec((tm, C), lambda i: (i, 0)),
                  pl.BlockSpec((SC_HALO, C), lambda i: (next_map(i), 0)),
                  pl.BlockSpec((8, C), lambda i: (0, 0))],
        out_specs=[pl.BlockSpec((tm, 3 * C), lambda i: (i, 0)),
                   pl.BlockSpec((8, C), lambda i: (0, 0))],
        out_shape=[jax.ShapeDtypeStruct((T, 3 * C), BF16), jax.ShapeDtypeStruct((8, C), F32)],
        scratch_shapes=[pltpu.VMEM((tm + SC_HALO, C), F32), pltpu.VMEM((R, C), F32)],
        compiler_params=_params(1),
    )(z, z, z, z, z, z, dm, dm, w)


def _xa_probs(q_h, k_h):
    s = _dot_nt(q_h, k_h) * (XA_HEAD_DIM ** -0.5)
    p = jnp.exp(s - jnp.max(s, axis=-1, keepdims=True))
    return p * (1.0 / jnp.sum(p, axis=-1, keepdims=True))


def xattn_fwd(q, kv, name, tm=512):
    T = q.shape[0]
    M = kv.shape[0]
    tm = _tile(T, tm)

    def body(q_ref, k_ref, v_ref, o_ref):
        for h in range(XA_HEADS):
            hs = slice(h * XA_HEAD_DIM, (h + 1) * XA_HEAD_DIM)
            p = _xa_probs(q_ref[:, hs], k_ref[:, hs])
            o_ref[:, hs] = _dot(p.astype(BF16), v_ref[:, hs]).astype(BF16)

    return pl.pallas_call(
        body, name=name, grid=(T // tm,),
        in_specs=[pl.BlockSpec((tm, D_MODEL), lambda i: (i, 0)),
                  pl.BlockSpec((M, D_MODEL), lambda i: (0, 0)),
                  pl.BlockSpec((M, D_MODEL), lambda i: (0, 1))],
        out_specs=pl.BlockSpec((tm, D_MODEL), lambda i: (i, 0)),
        out_shape=jax.ShapeDtypeStruct((T, D_MODEL), BF16),
        compiler_params=_params(1),
    )(q, kv, kv)


def xattn_bwd(q, kv, do, name, tm=512):
    T = q.shape[0]
    M = kv.shape[0]
    tm = _tile(T, tm)
    scale = XA_HEAD_DIM ** -0.5

    def body(q_ref, k_ref, v_ref, do_ref, dq_ref, dkv_ref):
        @pl.when(pl.program_id(0) == 0)
        def _():
            dkv_ref[...] = jnp.zeros_like(dkv_ref)

        for h in range(XA_HEADS):
            hs = slice(h * XA_HEAD_DIM, (h + 1) * XA_HEAD_DIM)
            vs = slice(D_MODEL + h * XA_HEAD_DIM, D_MODEL + (h + 1) * XA_HEAD_DIM)
            q_h = q_ref[:, hs]
            do_h = do_ref[:, hs]
            p = _xa_probs(q_h, k_ref[:, hs])
            dp = _dot_nt(do_h, v_ref[:, hs])
            ds = (p * (dp - jnp.sum(p * dp, axis=-1, keepdims=True))).astype(BF16)
            dq_ref[:, hs] = (_dot(ds, k_ref[:, hs]) * scale).astype(BF16)
            dkv_ref[:, hs] += _dot_tn(ds, q_h) * scale
            dkv_ref[:, vs] += _dot_tn(p.astype(BF16), do_h)

    return pl.pallas_call(
        body, name=name, grid=(T // tm,),
        in_specs=[pl.BlockSpec((tm, D_MODEL), lambda i: (i, 0)),
                  pl.BlockSpec((M, D_MODEL), lambda i: (0, 0)),
                  pl.BlockSpec((M, D_MODEL), lambda i: (0, 1)),
                  pl.BlockSpec((tm, D_MODEL), lambda i: (i, 0))],
        out_specs=[pl.BlockSpec((tm, D_MODEL), lambda i: (i, 0)),
                   pl.BlockSpec((M, 2 * D_MODEL), lambda i: (0, 0))],
        out_shape=[jax.ShapeDtypeStruct((T, D_MODEL), BF16), jax.ShapeDtypeStruct((M, 2 * D_MODEL), F32)],
        compiler_params=_params(1),
    )(q, kv, kv, do)


def final_loss(h, g, target, name, tm=512):
    T, K = h.shape
    tm = _tile(T, tm)

    def body(h_ref, g_ref, t_ref, dh_ref, dg_ref, loss_ref):
        @pl.when(pl.program_id(0) == 0)
        def _():
            dg_ref[...] = jnp.zeros_like(dg_ref)
            loss_ref[...] = jnp.zeros_like(loss_ref)

        x = h_ref[...]
        r = lax.rsqrt(jnp.mean(x * x, axis=-1, keepdims=True) + RMS_EPS)
        xh = x * r
        e = xh * g_ref[...] - t_ref[...]
        loss_ref[...] += jnp.zeros((1, 128), F32) + 0.5 * jnp.sum(jnp.mean(e * e, axis=-1, keepdims=True))
        dy = e * (1.0 / K)
        dg_ref[...] += jnp.sum(dy * xh, axis=0, keepdims=True)
        dxh = dy * g_ref[...]
        dh_ref[...] = r * (dxh - xh * jnp.mean(dxh * xh, axis=-1, keepdims=True))

    return pl.pallas_call(
        body, name=name, grid=(T // tm,),
        in_specs=[pl.BlockSpec((tm, K), lambda i: (i, 0)),
                  pl.BlockSpec((1, K), lambda i: (0, 0)),
                  pl.BlockSpec((tm, K), lambda i: (i, 0))],
        out_specs=[pl.BlockSpec((tm, K), lambda i: (i, 0)),
                   pl.BlockSpec((1, K), lambda i: (0, 0)),
                   pl.BlockSpec((1, 128), lambda i: (0, 0))],
        out_shape=[jax.ShapeDtypeStruct((T, K), F32), jax.ShapeDtypeStruct((1, K), F32),
                   jax.ShapeDtypeStruct((1, 128), F32)],
        compiler_params=_params(1),
    )(h, g, target)


def _row(v):
    return v.reshape(1, -1)


def _pad_rows(a, rows):
    return jnp.pad(a, ((0, rows - a.shape[0]), (0, 0)))


def local_step(x, mem, target, P, get_weights, put_grads):
    cw = _pad_rows(P["conv_a_w"], 32)
    scw = _pad_rows(P["sc_conv_w"], 8)
    cb, lg, lb = _row(P["conv_a_b"]), _row(P["conv_a_ln_g"]), _row(P["conv_a_ln_b"])
    sinks = P["swa_sinks"]

    class _Layered:
        def __init__(self, store, name=None):
            self.store, self.name = store, name

        def __getitem__(self, key):
            if self.name is None:
                return self.store[(key, 0)] if key in ("even_w_in", "even_w_out", "odd_w_in", "odd_w_out") \
                    else _Layered(self.store, key)
            return self.store[(self.name, key)]

    store = {}
    W = _Layered(store)
    saved = []
    h = x
    u1 = None
    for i in range(2):
        L = f"l{i}"
        new, dep = get_weights("A" if i == 0 else "D", h)
        store.update(new)

        h, u2, s_ffn1 = ffn_forward(h, P["ffn1_norm"][i:i + 1], W["ffn1_w_gu"][i], W["ffn1_w_down"][i], L + "_ffn1",
                                    dep=dep, u=u1, g_next=P["mix_norm"][i:i + 1])
        h1 = h
        if i == 0:
            new, _ = get_weights("B", h)
            store.update(new)
            z = matmul(u2, W["even_w_in"], F32, L + "_mix_in", 768, n_tiles=2)
            kv = matmul(u2, W["even_w_in"], BF16, L + "_mix_kv", 256, n_tiles=1, first_tile=6)
            a, conv_out = conformer_conv_fwd(z, cw, cb, lg, lb, L + "_conv")
            kpad = jnp.pad(kv[:, :2 * HEAD_DIM], ((WINDOW, 0), (0, 0)))
            vpad = jnp.pad(kv[:, 2 * HEAD_DIM:], ((WINDOW, 0), (0, 0)))
            o = swa_fwd(z, kpad, vpad, sinks, L + "_swa")
            m = jnp.concatenate([a, o], axis=-1)
            h, u3 = matmul_residual(m, W["even_w_out"], h1, P["xa_norm"][i:i + 1], L + "_mix_out")
            s_mix = (h1, u2, z, m, kpad, vpad, conv_out)
        else:
            z = matmul(u2, W["odd_w_in"], F32, L + "_mix_in", 1024)
            m = short_conv_fwd(z, scw, L + "_sconv")
            h, u3 = matmul_residual(m, W["odd_w_out"], h1, P["xa_norm"][i:i + 1], L + "_mix_out")
            s_mix = (h1, u2, z, m)
        h2 = h
        kv, umem = norm_matmul(mem, P["xa_mem_norm"][i:i + 1], W["xa_wkv"][i], BF16, L + "_xa_kv", 2 * D_MODEL)
        q, u3 = norm_matmul(h2, P["xa_norm"][i:i + 1], W["xa_wq"][i], BF16, L + "_xa_q", D_MODEL, transposed=False, u=u3)
        o = xattn_fwd(q, kv, L + "_xa")
        h, u4 = matmul_residual(o, W["xa_wo"][i], h2, P["ffn2_norm"][i:i + 1], L + "_xa_out")
        s_xa = (h2, u3, q, o, kv, umem)
        new, _ = get_weights("C" if i == 0 else "E", h)
        store.update(new)
        h, u1, s_ffn2 = ffn_forward(h, P["ffn2_norm"][i:i + 1], W["ffn2_w_gu"][i], W["ffn2_w_down"][i], L + "_ffn2", u=u4,
                                    g_next=P["ffn1_norm"][1:2] if i == 0 else None)
        saved.append((s_ffn1, s_mix, s_xa, s_ffn2))

    dh, d_final, loss = final_loss(h, _row(P["final_norm"]), target, "final_loss")

    names = ("ffn1_w_gu", "ffn1_w_down", "ffn2_w_gu", "ffn2_w_down", "xa_wq", "xa_wkv", "xa_wo", "even_w_in", "even_w_out",
             "odd_w_in", "odd_w_out")
    dW = {k: [None, None] for k in names}
    dP = {k: [None, None] for k in ("ffn1_norm", "mix_norm", "xa_norm", "xa_mem_norm", "ffn2_norm")}
    dP["final_norm"] = d_final.reshape(-1)
    for i in (1, 0):
        L = f"l{i}b"
        s_ffn1, s_mix, s_xa, s_ffn2 = saved[i]

        def keep_ffn2(d_w_gu, d_w_down, i=i):
            dW["ffn2_w_gu"][i], dW["ffn2_w_down"][i] = d_w_gu, d_w_down

        def send_ffn1(d_w_gu, d_w_down, i=i):
            dW["ffn1_w_gu"][i], dW["ffn1_w_down"][i] = d_w_gu, d_w_down
            keys = STAGE_KEYS["D"] + STAGE_KEYS["E"] if i == 1 else STAGE_KEYS["A"]
            return put_grads("D" if i == 1 else "A", {k: dW[k[0]][k[1]] for k in keys})

        dh, dP["ffn2_norm"][i] = ffn_backward(
            dh, s_ffn2, P["ffn2_norm"][i:i + 1], W["ffn2_w_gu"][i], W["ffn2_w_down"][i], L + "_ffn2", emit=keep_ffn2)
        h2, u3, q, o, kv, umem = s_xa
        dW["xa_wo"][i] = matmul_tn(o, dh, L + "_xa_dwo")
        do = matmul_nt(dh, W["xa_wo"][i], BF16, L + "_xa_do")
        dq, dkv = xattn_bwd(q, kv, do, L + "_xa")
        dW["xa_wq"][i] = matmul_tn(u3, dq, L + "_xa_dwq")
        dW["xa_wkv"][i] = matmul_tn(dkv, umem, L + "_xa_dwkv", tk=1024)
        dkv_b = dkv.astype(BF16)
        _, dP["xa_mem_norm"][i] = matmul_norm_bwd(dkv_b, W["xa_wkv"][i], mem, P["xa_mem_norm"][i:i + 1],
                                                  jnp.zeros_like(mem), L + "_xa_dmem")
        dh, dP["xa_norm"][i] = matmul_norm_bwd(dq, W["xa_wq"][i], h2, P["xa_norm"][i:i + 1], dh, L + "_xa_dx",
                                               transposed=False)
        if i == 0:
            h1, u2, z, m, kpad, vpad, conv_out = s_mix
            dW["even_w_out"][0] = matmul_tn(m, dh, L + "_mix_dwo")
            dm = matmul_nt(dh, W["even_w_out"], F32, L + "_mix_dm")
            dz_conv, dcw, dcb, dlg, dlb = conformer_conv_bwd(z, conv_out, dm, cw, lg, lb, L + "_conv")
            dq_s, dkp, dvp, dsk = swa_bwd(z, kpad, vpad, sinks, dm, L + "_swa")
            dz = jnp.concatenate([dz_conv, dq_s, dkp[WINDOW:].astype(BF16), dvp[WINDOW:].astype(BF16)], axis=-1)
            dW["even_w_in"][0] = matmul_tn(dz, u2, L + "_mix_dwi", tk=896)
            dh, dP["mix_norm"][i] = matmul_norm_bwd(dz, W["even_w_in"], h1, P["mix_norm"][i:i + 1], dh, L + "_mix_dx")
            dP["conv_a_w"] = dcw[:CONV_A_WIDTH]
            dP["conv_a_b"], dP["conv_a_ln_g"], dP["conv_a_ln_b"] = dcb.reshape(-1), dlg.reshape(-1), dlb.reshape(-1)
            dP["swa_sinks"] = dsk[:, 0]
        else:
            h1, u2, z, m = s_mix
            dW["odd_w_out"][0] = matmul_tn(m, dh, L + "_mix_dwo")
            dm = matmul_nt(dh, W["odd_w_out"], F32, L + "_mix_dm")
            dz, dscw = short_conv_bwd(z, dm, scw, L + "_sconv")
            dW["odd_w_in"][0] = matmul_tn(dz, u2, L + "_mix_dwi", tk=1024)
            dh, dP["mix_norm"][i] = matmul_norm_bwd(dz, W["odd_w_in"], h1, P["mix_norm"][i:i + 1], dh, L + "_mix_dx")
            dP["sc_conv_w"] = dscw[:3]
        dep = put_grads("BC", {k: dW[k[0]][k[1]] for k in STAGE_KEYS["B"] + STAGE_KEYS["C"]}) if i == 0 else None
        dh, dP["ffn1_norm"][i] = ffn_backward(
            dh, s_ffn1, P["ffn1_norm"][i:i + 1], W["ffn1_w_gu"][i], W["ffn1_w_down"][i], L + "_ffn1", dep=dep,
            emit=send_ffn1)
    for k in ("ffn1_norm", "mix_norm", "xa_norm", "xa_mem_norm", "ffn2_norm"):
        dP[k] = jnp.concatenate(dP[k], axis=0)
    return loss, dh, dP


def _mesh_pos():
    return lax.axis_index("x"), lax.axis_index("y"), lax.axis_index("c")


def _flat_index(px, py, pc):
    return 4 * px + 2 * py + pc


def all_gather(blob, name, dep=None):
    R, C = blob.shape

    def kern(x_ref, out_ref, send_sems, recv_sems, local_sem):
        x, y, c = _mesh_pos()
        me, sibling = (x, y, c), (x, y, 1 - c)
        chips = [(1 - x, y), (x, 1 - y), (1 - x, 1 - y)]

        def slot(px, py, pc):
            return out_ref.at[_flat_index(px, py, pc)]

        def copy(k, block, to, src=None):
            return pltpu.make_async_remote_copy(
                src_ref=slot(*block) if src is None else src, dst_ref=slot(*block),
                send_sem=send_sems.at[k], recv_sem=recv_sems.at[k],
                device_id=to, device_id_type=pl.DeviceIdType.MESH)

        mine = pltpu.make_async_copy(x_ref, slot(*me), local_sem)
        mine.start()
        first = [copy(0, me, sibling, src=x_ref)]
        first += [copy(1 + j, me, (*chip, c), src=x_ref) for j, chip in enumerate(chips)]
        for cp in first:
            cp.start()
        passed = [copy(4 + j, (*chip, c), sibling) for j, chip in enumerate(chips)]
        for j, chip in enumerate(chips):
            copy(1 + j, (*chip, c), me).wait_recv()
            passed[j].start()
        copy(0, sibling, me).wait_recv()
        for j, chip in enumerate(chips):
            copy(4 + j, (*chip, 1 - c), me).wait_recv()
        for cp in first + passed:
            cp.wait_send()
        mine.wait()

    body, dep_spec, dep_arg = _with_dep(kern, 1, dep)
    return pl.pallas_call(
        body, name=name,
        out_shape=jax.ShapeDtypeStruct((N_DEV, R, C), blob.dtype),
        in_specs=[ANY_SPEC] + dep_spec,
        out_specs=ANY_SPEC,
        scratch_shapes=[pltpu.SemaphoreType.DMA((7,)), pltpu.SemaphoreType.DMA((7,)), pltpu.SemaphoreType.DMA],
    )(blob, *dep_arg)


HBM_SPEC = pl.BlockSpec(memory_space=pltpu.HBM)
SEM_SPEC = pl.BlockSpec(memory_space=pltpu.SEMAPHORE)
DATAFLOW_EFFECT = pltpu.SideEffectType.DATAFLOW_SIDE_EFFECTING


def _peers(x, y, c):
    out = []
    for k in range(1, N_DEV):
        pos = (1 - x if k & 4 else x, 1 - y if k & 2 else y, 1 - c if k & 1 else c)
        out.append((_flat_index(*pos), pos))
    return out


def _exchange_copy(src_ref, land_ref, send_sems, recv_sems, j, me, peer_idx, peer, scatter):
    return pltpu.make_async_remote_copy(
        src_ref=src_ref.at[peer_idx] if scatter else src_ref, dst_ref=land_ref.at[me],
        send_sem=send_sems.at[j], recv_sem=recv_sems.at[j], device_id=peer, device_id_type=pl.DeviceIdType.MESH)


def exchange_start(srcs, lands, scatter, after, name):
    n = len(srcs)
    n_after = len(after)

    def body(*refs):
        src_refs, land_refs = refs[:n], refs[n:2 * n]
        outs = refs[2 * n + n_after:]
        send_sems, recv_sems, token = outs[:n], outs[n:2 * n], outs[4 * n]
        x, y, c = _mesh_pos()
        me = _flat_index(x, y, c)
        for g in range(n):
            for j, (peer_idx, peer) in enumerate(_peers(x, y, c)):
                _exchange_copy(src_refs[g], land_refs[g], send_sems[g], recv_sems[g], j, me, peer_idx, peer, scatter).start()
        token[...] = jnp.zeros_like(token)

    hbm = lambda a: pltpu.with_memory_space_constraint(a, pltpu.HBM)
    res = pl.pallas_call(
        body, name=name,
        out_shape=(*[pltpu.SemaphoreType.DMA((N_DEV - 1,))] * (2 * n),
                   *[pltpu.HBM(a.shape, a.dtype) for a in srcs], *[pltpu.HBM(a.shape, a.dtype) for a in lands],
                   jax.ShapeDtypeStruct((8, 128), F32)),
        in_specs=[HBM_SPEC] * (2 * n) + [ANY_SPEC] * n_after,
        out_specs=(*[SEM_SPEC] * (2 * n), *[HBM_SPEC] * (2 * n), pl.BlockSpec(memory_space=pltpu.VMEM)),
        input_output_aliases={i: 2 * n + i for i in range(2 * n)},
        compiler_params=pltpu.CompilerParams(has_side_effects=DATAFLOW_EFFECT),
    )(*[hbm(a) for a in srcs], *[hbm(a) for a in lands], *after)
    handles = [(res[g], res[n + g], res[2 * n + g], res[3 * n + g]) for g in range(n)]
    return handles, res[4 * n]


def exchange_wait(handles, scatter, after, name):
    n = len(handles)

    def body(*refs):
        src_refs, land_refs = refs[:n], refs[n:2 * n]
        send_sems, recv_sems = refs[2 * n:3 * n], refs[3 * n:4 * n]
        x, y, c = _mesh_pos()
        me = _flat_index(x, y, c)
        for g in range(n):
            for j, (peer_idx, peer) in enumerate(_peers(x, y, c)):
                mine = _exchange_copy(src_refs[g], land_refs[g], send_sems[g], recv_sems[g], j, me, peer_idx, peer,
                                      scatter)
                mine.wait_send()
                theirs = pltpu.make_async_remote_copy(
                    src_ref=src_refs[g].at[me] if scatter else src_refs[g], dst_ref=land_refs[g].at[peer_idx],
                    send_sem=send_sems[g].at[j], recv_sem=recv_sems[g].at[j], device_id=peer,
                    device_id_type=pl.DeviceIdType.MESH)
                theirs.wait_recv()

    srcs = [h[2] for h in handles]
    lands = [h[3] for h in handles]
    res = pl.pallas_call(
        body, name=name,
        out_shape=tuple(pltpu.HBM(a.shape, a.dtype) for a in srcs + lands),
        in_specs=[HBM_SPEC] * (2 * n) + [SEM_SPEC] * (2 * n) + [ANY_SPEC],
        out_specs=tuple([HBM_SPEC] * (2 * n)),
        input_output_aliases={i: i for i in range(2 * n)},
        compiler_params=pltpu.CompilerParams(has_side_effects=DATAFLOW_EFFECT),
    )(*srcs, *lands, *[h[0] for h in handles], *[h[1] for h in handles], after)
    return [(res[g], res[n + g]) for g in range(n)]


def ordered_sum(parts, name, tr=512):
    n, R, C = parts.shape
    tr = next((t for t in range(min(tr, R), 15, -16) if R % t == 0), R)

    def body(p_ref, o_ref):
        acc = p_ref[0].astype(F32)
        for j in range(1, n):
            acc = acc + p_ref[j].astype(F32)
        o_ref[...] = acc

    return pl.pallas_call(
        body, name=name, grid=(R // tr,),
        in_specs=[pl.BlockSpec((n, tr, C), lambda i: (0, i, 0))],
        out_specs=pl.BlockSpec((tr, C), lambda i: (i, 0)),
        out_shape=jax.ShapeDtypeStruct((R, C), F32),
        compiler_params=_params(1),
    )(parts)


def adamw(w, g, m, v, name, tr=256):
    R, C = w.shape
    tr = next((t for t in range(tr, 7, -8) if R % t == 0), R)
    c1 = 1.0 - ADAM_B1 ** ADAM_STEP
    c2 = 1.0 - ADAM_B2 ** ADAM_STEP

    def body(w_ref, g_ref, m_ref, v_ref, d_ref, mo_ref, vo_ref):
        grad = g_ref[...]
        m2 = ADAM_B1 * m_ref[...] + (1.0 - ADAM_B1) * grad
        v2 = ADAM_B2 * v_ref[...] + (1.0 - ADAM_B2) * (grad * grad)
        mo_ref[...] = m2
        vo_ref[...] = v2
        d_ref[...] = -ADAM_LR * ((m2 / c1) / (jnp.sqrt(v2 / c2) + ADAM_EPS) + ADAM_WD * w_ref[...])

    spec = pl.BlockSpec((tr, C), lambda i: (i, 0))
    return pl.pallas_call(
        body, name=name, grid=(R // tr,),
        in_specs=[spec] * 4, out_specs=[spec] * 3,
        out_shape=[jax.ShapeDtypeStruct((R, C), F32)] * 3,
        compiler_params=_params(1),
    )(w, g, m, v)


WEIGHT_NAMES = ("ffn1_norm", "ffn1_w_gu", "ffn1_w_down", "mix_norm", "even_w_in", "conv_a_w", "conv_a_b", "conv_a_ln_g",
                "conv_a_ln_b", "swa_sinks", "even_w_out", "odd_w_in", "sc_conv_w", "odd_w_out", "xa_norm", "xa_mem_norm",
                "xa_wq", "xa_wkv", "xa_wo", "ffn2_norm", "ffn2_w_gu", "ffn2_w_down", "final_norm")
BLOB_COLS = 1024
SMALL_ROWS = (("ffn1_norm", 0, 2), ("mix_norm", 2, 2), ("xa_norm", 4, 2), ("xa_mem_norm", 6, 2), ("ffn2_norm", 8, 2),
              ("final_norm", 10, 1))
ROW_CONV_B_LNG = 11
ROW_LNB_SINKS_LOSS = 12
LOSS_COL = 512 + SWA_HEADS
ROW_SC_CONV = 13
ROW_CONV_W = 16
SMALL_BLOB_ROWS = 32
SMALL_ADAM_ROWS = 16


def _small_blob(v):
    rows = [v[n].reshape(-1, D_MODEL) for n, _, _ in SMALL_ROWS]
    rows.append(jnp.concatenate([v["conv_a_b"].reshape(-1), v["conv_a_ln_g"].reshape(-1)]).reshape(1, D_MODEL))
    tail = jnp.zeros((D_MODEL - 512 - SWA_HEADS,), F32)
    if "loss" in v:
        tail = tail.at[0].set(v["loss"])
    rows.append(jnp.concatenate([v["conv_a_ln_b"].reshape(-1), v["swa_sinks"].reshape(-1), tail]).reshape(1, D_MODEL))
    rows.append(jnp.zeros((SMALL_ADAM_ROWS - ROW_SC_CONV, D_MODEL), F32))
    return jnp.concatenate(rows, axis=0)


def _small_unblob(b, shapes):
    out = {n: b[r:r + k].reshape(shapes[n]) for n, r, k in SMALL_ROWS}
    out["conv_a_b"] = b[ROW_CONV_B_LNG, :512].reshape(shapes["conv_a_b"])
    out["conv_a_ln_g"] = b[ROW_CONV_B_LNG, 512:].reshape(shapes["conv_a_ln_g"])
    out["conv_a_ln_b"] = b[ROW_LNB_SINKS_LOSS, :512].reshape(shapes["conv_a_ln_b"])
    out["swa_sinks"] = b[ROW_LNB_SINKS_LOSS, 512:512 + SWA_HEADS].reshape(shapes["swa_sinks"])
    return out


def kernel(x, mem, ffn1_norm, ffn1_w_gu, ffn1_w_down, mix_norm, even_w_in, conv_a_w, conv_a_b, conv_a_ln_g, conv_a_ln_b, swa_sinks, even_w_out, odd_w_in, sc_conv_w, odd_w_out, xa_norm, xa_mem_norm, xa_wq, xa_wkv, xa_wo, ffn2_norm, ffn2_w_gu, ffn2_w_down, final_norm, loss_target, m_ffn1_norm, m_ffn1_w_gu, m_ffn1_w_down, m_mix_norm, m_even_w_in, m_conv_a_w, m_conv_a_b, m_conv_a_ln_g, m_conv_a_ln_b, m_swa_sinks, m_even_w_out, m_odd_w_in, m_sc_conv_w, m_odd_w_out, m_xa_norm, m_xa_mem_norm, m_xa_wq, m_xa_wkv, m_xa_wo, m_ffn2_norm, m_ffn2_w_gu, m_ffn2_w_down, m_final_norm, v_ffn1_norm, v_ffn1_w_gu, v_ffn1_w_down, v_mix_norm, v_even_w_in, v_conv_a_w, v_conv_a_b, v_conv_a_ln_g, v_conv_a_ln_b, v_swa_sinks, v_even_w_out, v_odd_w_in, v_sc_conv_w, v_odd_w_out, v_xa_norm, v_xa_mem_norm, v_xa_wq, v_xa_wkv, v_xa_wo, v_ffn2_norm, v_ffn2_w_gu, v_ffn2_w_down, v_final_norm):
    w = dict(ffn1_norm=ffn1_norm, ffn1_w_gu=ffn1_w_gu, ffn1_w_down=ffn1_w_down, mix_norm=mix_norm, even_w_in=even_w_in,
             conv_a_w=conv_a_w, conv_a_b=conv_a_b, conv_a_ln_g=conv_a_ln_g, conv_a_ln_b=conv_a_ln_b, swa_sinks=swa_sinks,
             even_w_out=even_w_out, odd_w_in=odd_w_in, sc_conv_w=sc_conv_w, odd_w_out=odd_w_out, xa_norm=xa_norm,
             xa_mem_norm=xa_mem_norm, xa_wq=xa_wq, xa_wkv=xa_wkv, xa_wo=xa_wo, ffn2_norm=ffn2_norm, ffn2_w_gu=ffn2_w_gu,
             ffn2_w_down=ffn2_w_down, final_norm=final_norm)
    m = dict(ffn1_norm=m_ffn1_norm, ffn1_w_gu=m_ffn1_w_gu, ffn1_w_down=m_ffn1_w_down, mix_norm=m_mix_norm,
             even_w_in=m_even_w_in, conv_a_w=m_conv_a_w, conv_a_b=m_conv_a_b, conv_a_ln_g=m_conv_a_ln_g,
             conv_a_ln_b=m_conv_a_ln_b, swa_sinks=m_swa_sinks, even_w_out=m_even_w_out, odd_w_in=m_odd_w_in,
             sc_conv_w=m_sc_conv_w, odd_w_out=m_odd_w_out, xa_norm=m_xa_norm, xa_mem_norm=m_xa_mem_norm, xa_wq=m_xa_wq,
             xa_wkv=m_xa_wkv, xa_wo=m_xa_wo, ffn2_norm=m_ffn2_norm, ffn2_w_gu=m_ffn2_w_gu, ffn2_w_down=m_ffn2_w_down,
             final_norm=m_final_norm)
    v = dict(ffn1_norm=v_ffn1_norm, ffn1_w_gu=v_ffn1_w_gu, ffn1_w_down=v_ffn1_w_down, mix_norm=v_mix_norm,
             even_w_in=v_even_w_in, conv_a_w=v_conv_a_w, conv_a_b=v_conv_a_b, conv_a_ln_g=v_conv_a_ln_g,
             conv_a_ln_b=v_conv_a_ln_b, swa_sinks=v_swa_sinks, even_w_out=v_even_w_out, odd_w_in=v_odd_w_in,
             sc_conv_w=v_sc_conv_w, odd_w_out=v_odd_w_out, xa_norm=v_xa_norm, xa_mem_norm=v_xa_mem_norm, xa_wq=v_xa_wq,
             xa_wkv=v_xa_wkv, xa_wo=v_xa_wo, ffn2_norm=v_ffn2_norm, ffn2_w_gu=v_ffn2_w_gu, ffn2_w_down=v_ffn2_w_down,
             final_norm=v_final_norm)
    me = _flat_index(*_mesh_pos())

    conv_blob = jnp.concatenate([w["conv_a_w"].reshape(-1), w["sc_conv_w"].reshape(-1),
                                 jnp.zeros((8 * 1024 - 31 * 64 - 3 * 128,), F32)]).reshape(8, 1024)
    conv_all = all_gather(conv_blob, "gather_conv_weights").reshape(N_DEV, 8 * 1024)
    conv_a_full = jnp.transpose(conv_all[:, :31 * 64].reshape(N_DEV, 31, 64), (1, 0, 2)).reshape(31, 512)
    sc_full = jnp.transpose(conv_all[:, 31 * 64:31 * 64 + 3 * 128].reshape(N_DEV, 3, 128), (1, 0, 2)).reshape(3, 1024)

    def shard_rows(n, l):
        return (w[n][l].T if SPLIT_AXIS[n] == 1 else w[n][l]).astype(BF16)

    def with_own(land, own):
        return lax.dynamic_update_slice(land, own[None], (me, 0, 0))

    a_keys = STAGE_KEYS["A"]
    gathered_a = all_gather(jnp.concatenate([shard_rows(n, l) for n, l in a_keys], axis=0), "gather_weights_a")
    later = ("B", "C", "D", "E")
    later_keys = [k for s in later for k in STAGE_KEYS[s]]
    shards = [shard_rows(n, l) for n, l in later_keys]
    lands = [lax.empty((N_DEV,) + s.shape, BF16) for s in shards]
    handles, weight_token = exchange_start(shards, lands, False, [gathered_a, conv_all], "gather_start")
    weight_handles = dict(zip(later_keys, handles))

    def get_weights(stage, after):
        keys = STAGE_KEYS[stage]
        if stage == "A":
            out, off = {}, 0
            for n, l in keys:
                rows = w[n].shape[2] if SPLIT_AXIS[n] == 1 else w[n].shape[1]
                out[(n, l)] = gathered_a[:, off:off + rows, :].reshape(N_DEV * rows, BLOB_COLS)
                off += rows
            return out, weight_token
        got = exchange_wait([weight_handles[k] for k in keys], False, after, "gather_wait_" + stage.lower())
        return {k: with_own(land, own).reshape(-1, BLOB_COLS) for k, (own, land) in zip(keys, got)}, None

    grad_handles = {}

    def put_grads(stage, dws):
        srcs = [dw.reshape(N_DEV, -1, BLOB_COLS) for dw in dws.values()]
        handles, token = exchange_start(srcs, [lax.empty(s.shape, BF16) for s in srcs], True, [],
                                        "scatter_start_" + stage.lower())
        grad_handles[stage] = (handles, tuple(dws))
        return token

    P = dict(ffn1_norm=ffn1_norm, mix_norm=mix_norm, xa_norm=xa_norm, xa_mem_norm=xa_mem_norm, ffn2_norm=ffn2_norm,
             final_norm=final_norm, conv_a_w=conv_a_full, conv_a_b=conv_a_b[0], conv_a_ln_g=conv_a_ln_g[0],
             conv_a_ln_b=conv_a_ln_b[0], swa_sinks=swa_sinks[0], sc_conv_w=sc_full)

    loss_part, grad_x, dP = local_step(x[0], mem[0], loss_target[0], P, get_weights, put_grads)

    def finish_grads(stage, after):
        handles, keys = grad_handles[stage]
        got = exchange_wait(handles, True, after, "scatter_wait_" + stage.lower())
        out = {}
        for (n, l), (src, land) in zip(keys, got):
            own = lax.dynamic_slice(src, (me, 0, 0), (1,) + src.shape[1:])[0]
            part = ordered_sum(with_own(land, own), f"sum_grads_{n}_{l}")
            out[(n, l)] = part.T if SPLIT_AXIS[n] == 1 else part
        return out

    layer_grads = {**finish_grads("D", grad_x), **finish_grads("BC", grad_x)}

    grads, delta, new_m, new_v = {}, {}, {}, {}

    def update(n):
        shp = w[n].shape
        two_d = (shp[0] * shp[1], shp[2])
        d_, m_, v_ = adamw(w[n].reshape(two_d), grads[n].reshape(two_d), m[n].reshape(two_d), v[n].reshape(two_d),
                           "adamw_" + n)
        delta[n], new_m[n], new_v[n] = d_.reshape(shp), m_.reshape(shp), v_.reshape(shp)

    first_stage = tuple(n for n, _ in STAGE_KEYS["A"])
    for n in SPLIT_AXIS:
        if n not in first_stage:
            grads[n] = jnp.stack([layer_grads[(n, l)] for l in range(w[n].shape[0])], axis=0)
            update(n)
    layer_grads.update(finish_grads("A", delta["ffn2_w_gu"]))

    dP = dict(dP, loss=loss_part[0, 0])
    small = jnp.concatenate([
        _small_blob(dP)[:ROW_SC_CONV], dP["sc_conv_w"],
        jnp.concatenate([dP["conv_a_w"].reshape(-1), jnp.zeros((512,), F32)]).reshape(16, D_MODEL)], axis=0)
    small_all = all_gather(small, "gather_small_grads", dep=layer_grads[STAGE_KEYS["A"][-1]])
    small_sum = ordered_sum(small_all, "sum_small_grads", tr=SMALL_BLOB_ROWS)
    loss = small_sum[ROW_LNB_SINKS_LOSS, LOSS_COL]
    grads.update(_small_unblob(small_sum, {n: w[n].shape for n in WEIGHT_NAMES}))
    sc_g = small_sum[ROW_SC_CONV:ROW_SC_CONV + 3]
    grads["sc_conv_w"] = lax.dynamic_slice(sc_g, (0, me * 128), (3, 128)).reshape(w["sc_conv_w"].shape)
    cw_g = small_sum[ROW_CONV_W:].reshape(-1)[:31 * 512].reshape(31, 512)
    grads["conv_a_w"] = lax.dynamic_slice(cw_g, (0, me * 64), (31, 64)).reshape(w["conv_a_w"].shape)

    update("conv_a_w")
    update("sc_conv_w")
    for n in first_stage:
        grads[n] = jnp.stack([layer_grads[(n, l)] for l in range(w[n].shape[0])], axis=0)
        update(n)
    d_, m_, v_ = adamw(_small_blob(w), small_sum[:SMALL_ADAM_ROWS], _small_blob(m), _small_blob(v), "adamw_small",
                       tr=SMALL_ADAM_ROWS)
    shapes = {n: w[n].shape for n in WEIGHT_NAMES}
    delta.update(_small_unblob(d_, shapes))
    new_m.update(_small_unblob(m_, shapes))
    new_v.update(_small_unblob(v_, shapes))

    return (loss, grad_x[None], *[grads[n] for n in WEIGHT_NAMES], *[delta[n] for n in WEIGHT_NAMES],
            *[new_m[n] for n in WEIGHT_NAMES], *[new_v[n] for n in WEIGHT_NAMES])
```

```python
import functools

import jax
import jax.numpy as jnp
from jax import lax
from jax.experimental import pallas as pl
from jax.experimental.pallas import tpu as pltpu

F32 = jnp.float32
BF16 = jnp.bfloat16

D_MODEL = 1024
D_FF = 2816
CONV_A_CH = 512
CONV_A_WIDTH = 31
SWA_HEADS = 8
SWA_KV_HEADS = 2
SWA_GROUP = SWA_HEADS // SWA_KV_HEADS
HEAD_DIM = 64
WINDOW = 128
SC_CH = 1024
XA_HEADS = 4
XA_HEAD_DIM = D_MODEL // XA_HEADS
RMS_EPS = 1e-6
LN_EPS = 1e-5
ADAM_LR = 0.001
ADAM_B1 = 0.9
ADAM_B2 = 0.999
ADAM_EPS = 1e-08
ADAM_WD = 0.01
ADAM_STEP = 10
N_DEV = 8

V7X_VMEM_BYTES = 64 * 1024 * 1024
VMEM_LIMIT = V7X_VMEM_BYTES - 8 * 1024 * 1024
CONV_HALO = 32
SC_HALO = 8
NEG_BIG = -1e30

SPLIT_AXIS = dict(ffn1_w_gu=1, ffn1_w_down=0, even_w_in=1, even_w_out=0, odd_w_in=1, odd_w_out=0, xa_wq=0, xa_wkv=1, xa_wo=0,
                  ffn2_w_gu=1, ffn2_w_down=0)
STAGE_KEYS = dict(
    A=(("ffn1_w_gu", 0), ("ffn1_w_down", 0)),
    B=(("even_w_in", 0), ("even_w_out", 0), ("xa_wq", 0), ("xa_wkv", 0), ("xa_wo", 0)),
    C=(("ffn2_w_gu", 0), ("ffn2_w_down", 0)),
    D=(("ffn1_w_gu", 1), ("ffn1_w_down", 1), ("odd_w_in", 0), ("odd_w_out", 0), ("xa_wq", 1), ("xa_wkv", 1), ("xa_wo", 1)),
    E=(("ffn2_w_gu", 1), ("ffn2_w_down", 1)))


def _params(n_axes):
    return pltpu.CompilerParams(dimension_semantics=("arbitrary",) * n_axes, vmem_limit_bytes=VMEM_LIMIT)


def _tile(n, pref):
    t = min(n, pref)
    assert n % t == 0, (n, pref)
    return t


def _dot(a, b):
    return jnp.dot(a, b, preferred_element_type=F32)


def _dot_nt(a, b):
    return lax.dot_general(a, b, (((1,), (1,)), ((), ())), preferred_element_type=F32)


def _dot_tn(a, b):
    return lax.dot_general(a, b, (((0,), (0,)), ((), ())), preferred_element_type=F32)


def _sigmoid(x):
    return 0.5 * jnp.tanh(0.5 * x) + 0.5


ANY_SPEC = pl.BlockSpec(memory_space=pl.ANY)


def _with_dep(body, n_in, dep):
    if dep is None:
        return body, [], []
    return (lambda *refs: body(*refs[:n_in], *refs[n_in + 1:])), [ANY_SPEC], [dep]


def rmsnorm(h, g, name, tm=1024, dep=None):
    T, K = h.shape
    tm = _tile(T, tm)

    def kern(h_ref, g_ref, u_ref):
        x = h_ref[...]
        r = lax.rsqrt(jnp.mean(x * x, axis=-1, keepdims=True) + RMS_EPS)
        u_ref[...] = ((x * r) * g_ref[...]).astype(BF16)

    body, dep_spec, dep_arg = _with_dep(kern, 2, dep)
    return pl.pallas_call(
        body, name=name, grid=(T // tm,),
        in_specs=[pl.BlockSpec((tm, K), lambda i: (i, 0)), pl.BlockSpec((1, K), lambda i: (0, 0))] + dep_spec,
        out_specs=pl.BlockSpec((tm, K), lambda i: (i, 0)),
        out_shape=jax.ShapeDtypeStruct((T, K), BF16),
        compiler_params=_params(1),
    )(h, g, *dep_arg)


def matmul(a, w, out_dtype, name, tn, tm=2048, transposed=True, n_tiles=None, first_tile=0):
    T, K = a.shape
    N = w.shape[0] if transposed else w.shape[1]
    n_tiles = N // tn if n_tiles is None else n_tiles
    tm = _tile(T, tm)
    mm = _dot_nt if transposed else _dot

    def body(a_ref, w_ref, z_ref):
        z_ref[...] = mm(a_ref[...], w_ref[...]).astype(z_ref.dtype)

    w_spec = (pl.BlockSpec((tn, K), lambda i, j: (first_tile + j, 0)) if transposed
              else pl.BlockSpec((K, tn), lambda i, j: (0, first_tile + j)))
    return pl.pallas_call(
        body, name=name, grid=(T // tm, n_tiles),
        in_specs=[pl.BlockSpec((tm, K), lambda i, j: (i, 0)), w_spec],
        out_specs=pl.BlockSpec((tm, tn), lambda i, j: (i, j)),
        out_shape=jax.ShapeDtypeStruct((T, n_tiles * tn), out_dtype),
        compiler_params=_params(2),
    )(a, w)


def norm_matmul(h, g, w, out_dtype, name, tn, dep=None, transposed=True, u=None, tm=2048):
    u = rmsnorm(h, g, name + "_norm", dep=dep) if u is None else u
    return matmul(u, w, out_dtype, name, tn, tm=tm, transposed=transposed), u


def matmul_residual(a, w, res, g_next, name, tm=1024):
    T, K = a.shape
    N = w.shape[1]
    tm = _tile(T, tm)

    def body(a_ref, w_ref, r_ref, g_ref, o_ref, u_ref):
        x = r_ref[...] + _dot(a_ref[...], w_ref[...])
        o_ref[...] = x
        r = lax.rsqrt(jnp.mean(x * x, axis=-1, keepdims=True) + RMS_EPS)
        u_ref[...] = ((x * r) * g_ref[...]).astype(BF16)

    return pl.pallas_call(
        body, name=name, grid=(T // tm,),
        in_specs=[pl.BlockSpec((tm, K), lambda i: (i, 0)),
                  pl.BlockSpec((K, N), lambda i: (0, 0)),
                  pl.BlockSpec((tm, N), lambda i: (i, 0)),
                  pl.BlockSpec((1, N), lambda i: (0, 0))],
        out_specs=[pl.BlockSpec((tm, N), lambda i: (i, 0)), pl.BlockSpec((tm, N), lambda i: (i, 0))],
        out_shape=[jax.ShapeDtypeStruct((T, N), F32), jax.ShapeDtypeStruct((T, N), BF16)],
        compiler_params=_params(1),
    )(a, w, res, g_next)


def matmul_nt(dy, w, out_dtype, name, tm=1024):
    T, N = dy.shape
    K = w.shape[0]
    tm = _tile(T, tm)

    def body(dy_ref, w_ref, o_ref):
        o_ref[...] = _dot_nt(dy_ref[...].astype(BF16), w_ref[...]).astype(o_ref.dtype)

    return pl.pallas_call(
        body, name=name, grid=(T // tm,),
        in_specs=[pl.BlockSpec((tm, N), lambda i: (i, 0)),
                  pl.BlockSpec((K, N), lambda i: (0, 0))],
        out_specs=pl.BlockSpec((tm, K), lambda i: (i, 0)),
        out_shape=jax.ShapeDtypeStruct((T, K), out_dtype),
        compiler_params=_params(1),
    )(dy, w)


def matmul_norm_bwd(dz, w, h, g, dh_in, name, tm=512, transposed=True, dep=None):
    T, N = dz.shape
    K = h.shape[1]
    tm = _tile(T, tm)

    def kern(dz_ref, w_ref, h_ref, g_ref, dhin_ref, dh_ref, dg_ref):
        @pl.when(pl.program_id(0) == 0)
        def _():
            dg_ref[...] = jnp.zeros_like(dg_ref)

        mm = _dot if transposed else _dot_nt
        du = mm(dz_ref[...], w_ref[...])
        x = h_ref[...]
        r = lax.rsqrt(jnp.mean(x * x, axis=-1, keepdims=True) + RMS_EPS)
        xh = x * r
        dg_ref[...] += jnp.sum(du * xh, axis=0, keepdims=True)
        dxh = du * g_ref[...]
        dh_ref[...] = dhin_ref[...] + r * (dxh - xh * jnp.mean(dxh * xh, axis=-1, keepdims=True))

    body, dep_spec, dep_arg = _with_dep(kern, 5, dep)
    return pl.pallas_call(
        body, name=name, grid=(T // tm,),
        in_specs=[pl.BlockSpec((tm, N), lambda i: (i, 0)),
                  pl.BlockSpec(w.shape, lambda i: (0, 0)),
                  pl.BlockSpec((tm, K), lambda i: (i, 0)),
                  pl.BlockSpec((1, K), lambda i: (0, 0)),
                  pl.BlockSpec((tm, K), lambda i: (i, 0))] + dep_spec,
        out_specs=[pl.BlockSpec((tm, K), lambda i: (i, 0)),
                   pl.BlockSpec((1, K), lambda i: (0, 0))],
        out_shape=[jax.ShapeDtypeStruct((T, K), F32), jax.ShapeDtypeStruct((1, K), F32)],
        compiler_params=_params(1),
    )(dz, w, h, g, dh_in, *dep_arg)


def matmul_tn(x, dy, name, scale=1.0, tk=None, tn=None, tt=1024):
    T, K = x.shape
    N = dy.shape[1]
    tk = K if tk is None else tk
    tn = N if tn is None else tn
    tt = _tile(T, tt)
    nt = T // tt

    def body(x_ref, dy_ref, o_ref, acc_ref):
        t = pl.program_id(2)

        @pl.when(t == 0)
        def _():
            acc_ref[...] = jnp.zeros_like(acc_ref)

        acc_ref[...] += _dot_tn(x_ref[...].astype(BF16), dy_ref[...].astype(BF16))

        @pl.when(t == nt - 1)
        def _():
            o_ref[...] = (acc_ref[...] * scale).astype(o_ref.dtype)

    return pl.pallas_call(
        body, name=name, grid=(K // tk, N // tn, nt),
        in_specs=[pl.BlockSpec((tt, tk), lambda a, b, t: (t, a)),
                  pl.BlockSpec((tt, tn), lambda a, b, t: (t, b))],
        out_specs=pl.BlockSpec((tk, tn), lambda a, b, t: (a, b)),
        out_shape=jax.ShapeDtypeStruct((K, N), BF16),
        scratch_shapes=[pltpu.VMEM((tk, tn), F32)],
        compiler_params=_params(3),
    )(x, dy)


def ffn_down(gu, wd, res, g_next, name, tm=512):
    T = gu.shape[0]
    F = gu.shape[1] // 2
    N = wd.shape[1]
    tm = _tile(T, tm)
    with_next = g_next is not None

    def body(g_ref, up_ref, w_ref, r_ref, *rest):
        o_ref, a_ref = rest[-3:-1] if with_next else rest[-2:]
        g = g_ref[...].astype(F32)
        a_ref[...] = ((g * _sigmoid(g)) * up_ref[...].astype(F32)).astype(BF16)
        x = r_ref[...] + 0.5 * _dot(a_ref[...], w_ref[...])
        o_ref[...] = x
        if with_next:
            r = lax.rsqrt(jnp.mean(x * x, axis=-1, keepdims=True) + RMS_EPS)
            rest[-1][...] = ((x * r) * rest[0][...]).astype(BF16)

    row = lambda width: pl.BlockSpec((tm, width), lambda i: (i, 0))
    res = pl.pallas_call(
        body, name=name, grid=(T // tm,),
        in_specs=[row(F), pl.BlockSpec((tm, F), lambda i: (i, 1)), pl.BlockSpec((F, N), lambda i: (0, 0)), row(N)]
        + ([pl.BlockSpec((1, N), lambda i: (0, 0))] if with_next else []),
        out_specs=[row(N), row(F)] + ([row(N)] if with_next else []),
        out_shape=[jax.ShapeDtypeStruct((T, N), F32), jax.ShapeDtypeStruct((T, F), BF16)]
        + ([jax.ShapeDtypeStruct((T, N), BF16)] if with_next else []),
        compiler_params=_params(1),
    )(gu, gu, wd, res, *([g_next] if with_next else []))
    return (res[0], res[1], res[2]) if with_next else (res[0], res[1], None)


def ffn_down_bwd(dy, wd, gu, name, tm=512, dep=None):
    T, N = dy.shape
    F = wd.shape[0]
    tm = _tile(T, tm)

    def kern(dy_ref, w_ref, g_ref, up_ref, o_ref):
        da = 0.5 * _dot_nt(dy_ref[...].astype(BF16), w_ref[...])
        g = g_ref[...].astype(F32)
        up = up_ref[...].astype(F32)
        s = _sigmoid(g)
        o_ref[:, :F] = (da * up * (s * (1.0 + g * (1.0 - s)))).astype(BF16)
        o_ref[:, F:] = (da * (g * s)).astype(BF16)

    body, dep_spec, dep_arg = _with_dep(kern, 4, dep)
    return pl.pallas_call(
        body, name=name, grid=(T // tm,),
        in_specs=[pl.BlockSpec((tm, N), lambda i: (i, 0)),
                  pl.BlockSpec((F, N), lambda i: (0, 0)),
                  pl.BlockSpec((tm, F), lambda i: (i, 0)),
                  pl.BlockSpec((tm, F), lambda i: (i, 1))] + dep_spec,
        out_specs=pl.BlockSpec((tm, 2 * F), lambda i: (i, 0)),
        out_shape=jax.ShapeDtypeStruct((T, 2 * F), BF16),
        compiler_params=_params(1),
    )(dy, wd, gu, gu, *dep_arg)


def ffn_forward(h, g, w_gu, w_down, name, dep=None, u=None, g_next=None):
    gu, u = norm_matmul(h, g, w_gu, BF16, name + "_gu", D_FF, dep=dep, u=u, tm=1024)
    h_out, a, u_next = ffn_down(gu, w_down, h, g_next, name + "_down")
    return h_out, u_next, (h, u, gu, a)


def ffn_backward(dy, saved, g, w_gu, w_down, name, dep=None, emit=None):
    h, u, gu, a = saved
    dgu = ffn_down_bwd(dy, w_down, gu, name + "_ddown", dep=dep)
    d_w_down = matmul_tn(a, dy, name + "_dwd", scale=0.5, tk=D_FF // 2)
    d_w_gu = matmul_tn(dgu, u, name + "_dwgu", tk=D_FF)
    dh, dg = matmul_norm_bwd(dgu, w_gu, h, g, dy, name + "_dx", dep=emit(d_w_gu, d_w_down))
    return dh, dg


CONV_ROW_CHUNK = 32
CONV_LANES = 128
CONV_X_OFFSETS = tuple(CONV_HALO - (CONV_A_WIDTH - 1) + k for k in range(CONV_A_WIDTH))
CONV_D_OFFSETS = tuple(CONV_A_WIDTH - 1 - k for k in range(CONV_A_WIDTH))


def _build_phases(ref, phase_ref, n_rows):
    for r in range(1, 8):
        phase_ref[r - 1] = ref[pl.ds(r, n_rows - 8), :]


def _tap_values(ref, phase_ref, offsets, n, base, lanes):
    out = {}
    for r in range(8):
        qs = sorted(o // 8 for o in offsets if o % 8 == r)
        if qs:
            lo, hi = qs[0], qs[-1]
            rows = pl.ds(base + 8 * lo, n + 8 * (hi - lo))
            span = ref[rows, lanes] if r == 0 else phase_ref[r - 1, rows, lanes]
            for q in qs:
                out[8 * q + r] = span[8 * (q - lo):8 * (q - lo) + n]
    return out


def conformer_conv_fwd(z, cw, cb, lg, lb, name, tm=512):
    T = z.shape[0]
    C = CONV_A_CH
    tm = _tile(T, tm)
    hb = tm // CONV_HALO
    CH = CONV_ROW_CHUNK
    KW = CONV_A_WIDTH

    def body(v_ref, gt_ref, pv_ref, pg_ref, cw_ref, cb_ref, lg_ref, lb_ref, o_ref, conv_ref, xs_ref, xph_ref):
        i = pl.program_id(0)
        prev = pv_ref[...] * _sigmoid(pg_ref[...])
        xs_ref[0:CONV_HALO, :] = jnp.where(i > 0, prev, 0.0)
        xs_ref[CONV_HALO:, :] = v_ref[...] * _sigmoid(gt_ref[...])
        _build_phases(xs_ref, xph_ref, tm + CONV_HALO)

        def chunk(c, carry):
            off = pl.multiple_of(c * CH, CH)
            for l0 in range(0, C, CONV_LANES):
                lanes = slice(l0, l0 + CONV_LANES)
                taps = _tap_values(xs_ref, xph_ref, CONV_X_OFFSETS, CH, off, lanes)
                acc = jnp.zeros((CH, CONV_LANES), F32) + cb_ref[:, lanes]
                for k in range(KW):
                    acc = acc + cw_ref[k:k + 1, lanes] * taps[CONV_X_OFFSETS[k]]
                conv_ref[pl.ds(off, CH), lanes] = acc
            return carry

        lax.fori_loop(0, tm // CH, chunk, 0)
        acc = conv_ref[...]
        mu = jnp.mean(acc, axis=-1, keepdims=True)
        xc = acc - mu
        var = jnp.mean(xc * xc, axis=-1, keepdims=True)
        y = (xc * lax.rsqrt(var + LN_EPS)) * lg_ref[...] + lb_ref[...]
        o_ref[...] = (y * _sigmoid(y)).astype(BF16)

    return pl.pallas_call(
        body, name=name, grid=(T // tm,),
        in_specs=[pl.BlockSpec((tm, C), lambda i: (i, 0)),
                  pl.BlockSpec((tm, C), lambda i: (i, 1)),
                  pl.BlockSpec((CONV_HALO, C), lambda i: (jnp.maximum(i * hb - 1, 0), 0)),
                  pl.BlockSpec((CONV_HALO, C), lambda i: (jnp.maximum(i * hb - 1, 0), 1)),
                  pl.BlockSpec((32, C), lambda i: (0, 0)),
                  pl.BlockSpec((1, C), lambda i: (0, 0)),
                  pl.BlockSpec((1, C), lambda i: (0, 0)),
                  pl.BlockSpec((1, C), lambda i: (0, 0))],
        out_specs=[pl.BlockSpec((tm, C), lambda i: (i, 0)), pl.BlockSpec((tm, C), lambda i: (i, 0))],
        out_shape=[jax.ShapeDtypeStruct((T, C), BF16), jax.ShapeDtypeStruct((T, C), F32)],
        scratch_shapes=[pltpu.VMEM((tm + CONV_HALO, C), F32), pltpu.VMEM((7, tm + CONV_HALO - 8, C), F32)],
        compiler_params=_params(1),
    )(z, z, z, z, cw, cb, lg, lb)


def conformer_conv_bwd(z, conv_out, dm, cw, lg, lb, name, tm=512):
    T = z.shape[0]
    C = CONV_A_CH
    tm = _tile(T, tm)
    hb = tm // CONV_HALO
    n_tiles = T // tm
    last_halo = T // CONV_HALO - 1
    R = tm + CONV_HALO
    KW = CONV_A_WIDTH
    CH = CONV_ROW_CHUNK

    def body(v_ref, gt_ref, pv_ref, pg_ref, cv_ref, ncv_ref, do_ref, ndo_ref, cw_ref, lg_ref, lb_ref,
             dz_ref, dcw_ref, dcb_ref, dlg_ref, dlb_ref, xs_ref, xph_ref, ds_ref, dph_ref):
        i = pl.program_id(0)

        @pl.when(i == 0)
        def _():
            dcw_ref[...] = jnp.zeros_like(dcw_ref)
            dcb_ref[...] = jnp.zeros_like(dcb_ref)
            dlg_ref[...] = jnp.zeros_like(dlg_ref)
            dlb_ref[...] = jnp.zeros_like(dlb_ref)

        prev = pv_ref[...] * _sigmoid(pg_ref[...])
        xs_ref[0:CONV_HALO, :] = jnp.where(i > 0, prev, 0.0)
        xs_ref[CONV_HALO:, :] = v_ref[...] * _sigmoid(gt_ref[...])
        _build_phases(xs_ref, xph_ref, tm + CONV_HALO)

        acc = jnp.concatenate([cv_ref[...], ncv_ref[...]], axis=0)
        mu = jnp.mean(acc, axis=-1, keepdims=True)
        xc = acc - mu
        rstd = lax.rsqrt(jnp.mean(xc * xc, axis=-1, keepdims=True) + LN_EPS)
        xh = xc * rstd
        y = xh * lg_ref[...] + lb_ref[...]
        s = _sigmoid(y)
        dout = jnp.concatenate([do_ref[...], jnp.where(i < n_tiles - 1, ndo_ref[...], 0.0)], axis=0)
        dy = dout * (s * (1.0 + y * (1.0 - s)))
        dxh = dy * lg_ref[...]
        dconv = rstd * (dxh - jnp.mean(dxh, axis=-1, keepdims=True) - xh * jnp.mean(dxh * xh, axis=-1, keepdims=True))
        ds_ref[...] = dconv
        dlg_ref[...] += jnp.sum(dy[:tm] * xh[:tm], axis=0, keepdims=True)
        dlb_ref[...] += jnp.sum(dy[:tm], axis=0, keepdims=True)
        dcb_ref[...] += jnp.sum(dconv[:tm], axis=0, keepdims=True)
        _build_phases(ds_ref, dph_ref, R)

        for l0 in range(0, C, CONV_LANES):
            lanes = slice(l0, l0 + CONV_LANES)

            def taps_bwd(c, wacc, l0=l0, lanes=lanes):
                off = pl.multiple_of(c * CH, CH)
                x_taps = _tap_values(xs_ref, xph_ref, CONV_X_OFFSETS, CH, off, lanes)
                d_taps = _tap_values(ds_ref, dph_ref, CONV_D_OFFSETS, CH, off, lanes)
                dc = ds_ref[pl.ds(off, CH), lanes]
                dglu = jnp.zeros((CH, CONV_LANES), F32)
                new = []
                for k in range(KW):
                    dglu = dglu + cw_ref[k:k + 1, lanes] * d_taps[CONV_D_OFFSETS[k]]
                    prod = dc * x_taps[CONV_X_OFFSETS[k]]
                    new.append(wacc[k] + ((prod[0:8] + prod[8:16]) + (prod[16:24] + prod[24:32])))
                val = v_ref[pl.ds(off, CH), lanes]
                sg = _sigmoid(gt_ref[pl.ds(off, CH), lanes])
                dz_ref[pl.ds(off, CH), lanes] = (dglu * sg).astype(BF16)
                dz_ref[pl.ds(off, CH), C + l0:C + l0 + CONV_LANES] = (dglu * val * sg * (1.0 - sg)).astype(BF16)
                return tuple(new)

            wacc = lax.fori_loop(0, tm // CH, taps_bwd, tuple(jnp.zeros((8, CONV_LANES), F32) for _ in range(KW)))
            for k in range(KW):
                dcw_ref[k:k + 1, lanes] += jnp.sum(wacc[k], axis=0, keepdims=True)

    prev_map = lambda i: jnp.maximum(i * hb - 1, 0)
    next_map = lambda i: jnp.minimum((i + 1) * hb, last_halo)
    return pl.pallas_call(
        body, name=name, grid=(n_tiles,),
        in_specs=[pl.BlockSpec((tm, C), lambda i: (i, 0)),
                  pl.BlockSpec((tm, C), lambda i: (i, 1)),
                  pl.BlockSpec((CONV_HALO, C), lambda i: (prev_map(i), 0)),
                  pl.BlockSpec((CONV_HALO, C), lambda i: (prev_map(i), 1)),
                  pl.BlockSpec((tm, C), lambda i: (i, 0)),
                  pl.BlockSpec((CONV_HALO, C), lambda i: (next_map(i), 0)),
                  pl.BlockSpec((tm, C), lambda i: (i, 0)),
                  pl.BlockSpec((CONV_HALO, C), lambda i: (next_map(i), 0)),
                  pl.BlockSpec((32, C), lambda i: (0, 0)),
                  pl.BlockSpec((1, C), lambda i: (0, 0)),
                  pl.BlockSpec((1, C), lambda i: (0, 0))],
        out_specs=[pl.BlockSpec((tm, 2 * C), lambda i: (i, 0)),
                   pl.BlockSpec((32, C), lambda i: (0, 0)),
                   pl.BlockSpec((1, C), lambda i: (0, 0)),
                   pl.BlockSpec((1, C), lambda i: (0, 0)),
                   pl.BlockSpec((1, C), lambda i: (0, 0))],
        out_shape=[jax.ShapeDtypeStruct((T, 2 * C), BF16),
                   jax.ShapeDtypeStruct((32, C), F32),
                   jax.ShapeDtypeStruct((1, C), F32),
                   jax.ShapeDtypeStruct((1, C), F32),
                   jax.ShapeDtypeStruct((1, C), F32)],
        scratch_shapes=[pltpu.VMEM((tm + CONV_HALO, C), F32), pltpu.VMEM((7, tm + CONV_HALO - 8, C), F32),
                        pltpu.VMEM((R, C), F32), pltpu.VMEM((7, R - 8, C), F32)],
        compiler_params=_params(1),
    )(z, z, z, z, conv_out, conv_out, dm, dm, cw, lg, lb)


def _swa_scores(q_h, kk_h, slope, bias_dist, valid, sink):
    s = _dot_nt(q_h, kk_h) * (HEAD_DIM ** -0.5) - slope * bias_dist
    s = jnp.where(valid, s, NEG_BIG)
    m = jnp.maximum(jnp.max(s, axis=-1, keepdims=True), sink)
    p = jnp.exp(s - m)
    e_sink = jnp.exp(sink - m)
    inv = 1.0 / (jnp.sum(p, axis=-1, keepdims=True) + e_sink)
    return p * inv, e_sink * inv


def _swa_mask(r0):
    qi = lax.broadcasted_iota(jnp.int32, (WINDOW, 2 * WINDOW), 0)
    kj = lax.broadcasted_iota(jnp.int32, (WINDOW, 2 * WINDOW), 1)
    dist = qi + WINDOW - kj
    valid = (dist >= 0) & (dist < WINDOW) & (r0 - WINDOW + kj >= 0)
    return dist.astype(F32), valid


def swa_fwd(z, kpad, vpad, sinks, name, tq=512):
    T = z.shape[0]
    tq = _tile(T, tq)
    HQ = SWA_HEADS * HEAD_DIM

    def body(sink_ref, q_ref, k_ref, v_ref, o_ref):
        i = pl.program_id(0)
        for sub in range(tq // WINDOW):
            r0 = pl.multiple_of(i * tq + sub * WINDOW, WINDOW)
            kk = k_ref[pl.ds(r0, 2 * WINDOW), :]
            vv = v_ref[pl.ds(r0, 2 * WINDOW), :]
            qb = q_ref[sub * WINDOW:(sub + 1) * WINDOW, :].astype(BF16)
            dist, valid = _swa_mask(r0)
            outs = []
            for h in range(SWA_HEADS):
                kh = h // SWA_GROUP
                ks = slice(kh * HEAD_DIM, (kh + 1) * HEAD_DIM)
                pn, _ = _swa_scores(qb[:, h * HEAD_DIM:(h + 1) * HEAD_DIM], kk[:, ks], 2.0 ** (-(h + 1)), dist, valid,
                                    sink_ref[h])
                outs.append(_dot(pn.astype(BF16), vv[:, ks]))
            o_ref[sub * WINDOW:(sub + 1) * WINDOW, :] = jnp.concatenate(outs, axis=-1).astype(BF16)

    return pl.pallas_call(
        body, name=name, grid=(T // tq,),
        in_specs=[pl.BlockSpec(memory_space=pltpu.SMEM),
                  pl.BlockSpec((tq, HQ), lambda i: (i, 2)),
                  pl.BlockSpec((T + WINDOW, 2 * HEAD_DIM), lambda i: (0, 0)),
                  pl.BlockSpec((T + WINDOW, 2 * HEAD_DIM), lambda i: (0, 0))],
        out_specs=pl.BlockSpec((tq, HQ), lambda i: (i, 0)),
        out_shape=jax.ShapeDtypeStruct((T, HQ), BF16),
        compiler_params=_params(1),
    )(sinks, z, kpad, vpad)


def swa_bwd(z, kpad, vpad, sinks, dm, name, tq=512):
    T = z.shape[0]
    tq = _tile(T, tq)
    HQ = SWA_HEADS * HEAD_DIM
    scale = HEAD_DIM ** -0.5

    def body(sink_ref, q_ref, k_ref, v_ref, do_ref, dq_ref, dk_ref, dv_ref, dsink_ref):
        i = pl.program_id(0)

        @pl.when(i == 0)
        def _():
            dk_ref[...] = jnp.zeros_like(dk_ref)
            dv_ref[...] = jnp.zeros_like(dv_ref)
            dsink_ref[...] = jnp.zeros_like(dsink_ref)

        for sub in range(tq // WINDOW):
            r0 = pl.multiple_of(i * tq + sub * WINDOW, WINDOW)
            kk = k_ref[pl.ds(r0, 2 * WINDOW), :]
            vv = v_ref[pl.ds(r0, 2 * WINDOW), :]
            rows = slice(sub * WINDOW, (sub + 1) * WINDOW)
            qb = q_ref[rows, :].astype(BF16)
            dob = do_ref[rows, :].astype(BF16)
            dist, valid = _swa_mask(r0)
            dqs, dks, dvs = [], [], []
            for kh in range(SWA_KV_HEADS):
                ks = slice(kh * HEAD_DIM, (kh + 1) * HEAD_DIM)
                dk_acc = jnp.zeros((2 * WINDOW, HEAD_DIM), F32)
                dv_acc = jnp.zeros((2 * WINDOW, HEAD_DIM), F32)
                for g in range(SWA_GROUP):
                    h = kh * SWA_GROUP + g
                    hs = slice(h * HEAD_DIM, (h + 1) * HEAD_DIM)
                    pn, p_sink = _swa_scores(qb[:, hs], kk[:, ks], 2.0 ** (-(h + 1)), dist, valid, sink_ref[h])
                    dp = _dot_nt(dob[:, hs], vv[:, ks])
                    delta = jnp.sum(pn * dp, axis=-1, keepdims=True)
                    ds = (pn * (dp - delta)).astype(BF16)
                    dqs.append(_dot(ds, kk[:, ks]) * scale)
                    dk_acc = dk_acc + _dot_tn(ds, qb[:, hs]) * scale
                    dv_acc = dv_acc + _dot_tn(pn.astype(BF16), dob[:, hs])
                    dsink_ref[h:h + 1, :] += jnp.zeros((1, 128), F32) - jnp.sum(p_sink * delta)
                dks.append(dk_acc)
                dvs.append(dv_acc)
            dq_ref[rows, :] = jnp.concatenate(dqs, axis=-1).astype(BF16)
            dk_ref[pl.ds(r0, 2 * WINDOW), :] += jnp.concatenate(dks, axis=-1)
            dv_ref[pl.ds(r0, 2 * WINDOW), :] += jnp.concatenate(dvs, axis=-1)

    kv_spec = pl.BlockSpec((T + WINDOW, 2 * HEAD_DIM), lambda i: (0, 0))
    return pl.pallas_call(
        body, name=name, grid=(T // tq,),
        in_specs=[pl.BlockSpec(memory_space=pltpu.SMEM),
                  pl.BlockSpec((tq, HQ), lambda i: (i, 2)),
                  kv_spec, kv_spec,
                  pl.BlockSpec((tq, HQ), lambda i: (i, 1))],
        out_specs=[pl.BlockSpec((tq, HQ), lambda i: (i, 0)),
                   kv_spec, kv_spec,
                   pl.BlockSpec((SWA_HEADS, 128), lambda i: (0, 0))],
        out_shape=[jax.ShapeDtypeStruct((T, HQ), BF16),
                   jax.ShapeDtypeStruct((T + WINDOW, 2 * HEAD_DIM), F32),
                   jax.ShapeDtypeStruct((T + WINDOW, 2 * HEAD_DIM), F32),
                   jax.ShapeDtypeStruct((SWA_HEADS, 128), F32)],
        compiler_params=_params(1),
    )(sinks, z, kpad, vpad, dm)


def short_conv_fwd(z, w, name, tm=512):
    T = z.shape[0]
    C = SC_CH
    tm = _tile(T, tm)
    hb = tm // SC_HALO

    def body(b_ref, c_ref, v_ref, pc_ref, pv_ref, w_ref, o_ref, xs_ref):
        i = pl.program_id(0)
        xs_ref[0:SC_HALO, :] = jnp.where(i > 0, pc_ref[...] * pv_ref[...], 0.0)
        xs_ref[SC_HALO:, :] = c_ref[...] * v_ref[...]
        conv = jnp.zeros((tm, C), F32)
        for k in range(3):
            conv = conv + w_ref[k:k + 1, :] * xs_ref[pl.ds(SC_HALO - 2 + k, tm), :]
        o_ref[...] = (b_ref[...] * conv).astype(BF16)

    prev_map = lambda i: jnp.maximum(i * hb - 1, 0)
    return pl.pallas_call(
        body, name=name, grid=(T // tm,),
        in_specs=[pl.BlockSpec((tm, C), lambda i: (i, 0)),
                  pl.BlockSpec((tm, C), lambda i: (i, 1)),
                  pl.BlockSpec((tm, C), lambda i: (i, 2)),
                  pl.BlockSpec((SC_HALO, C), lambda i: (prev_map(i), 1)),
                  pl.BlockSpec((SC_HALO, C), lambda i: (prev_map(i), 2)),
                  pl.BlockSpec((8, C), lambda i: (0, 0))],
        out_specs=pl.BlockSpec((tm, C), lambda i: (i, 0)),
        out_shape=jax.ShapeDtypeStruct((T, C), BF16),
        scratch_shapes=[pltpu.VMEM((tm + SC_HALO, C), F32)],
        compiler_params=_params(1),
    )(z, z, z, z, z, w)


def short_conv_bwd(z, dm, w, name, tm=512):
    T = z.shape[0]
    C = SC_CH
    tm = _tile(T, tm)
    hb = tm // SC_HALO
    n_tiles = T // tm
    last_halo = T // SC_HALO - 1
    R = tm + SC_HALO

    def body(b_ref, c_ref, v_ref, pc_ref, pv_ref, nb_ref, do_ref, ndo_ref, w_ref, dz_ref, dw_ref, xs_ref, ds_ref):
        i = pl.program_id(0)

        @pl.when(i == 0)
        def _():
            dw_ref[...] = jnp.zeros_like(dw_ref)

        c = c_ref[...]
        val = v_ref[...]
        dout = do_ref[...]
        xs_ref[0:SC_HALO, :] = jnp.where(i > 0, pc_ref[...] * pv_ref[...], 0.0)
        xs_ref[SC_HALO:, :] = c * val
        dconv = dout * b_ref[...]
        ds_ref[0:tm, :] = dconv
        ds_ref[tm:, :] = jnp.where(i < n_tiles - 1, ndo_ref[...] * nb_ref[...], 0.0)
        conv = jnp.zeros((tm, C), F32)
        dcv = jnp.zeros((tm, C), F32)
        for k in range(3):
            xk = xs_ref[pl.ds(SC_HALO - 2 + k, tm), :]
            conv = conv + w_ref[k:k + 1, :] * xk
            dw_ref[k:k + 1, :] += jnp.sum(dconv * xk, axis=0, keepdims=True)
            dcv = dcv + w_ref[k:k + 1, :] * ds_ref[pl.ds(2 - k, tm), :]
        dz_ref[:, 0:C] = (dout * conv).astype(BF16)
        dz_ref[:, C:2 * C] = (dcv * val).astype(BF16)
        dz_ref[:, 2 * C:] = (dcv * c).astype(BF16)

    prev_map = lambda i: jnp.maximum(i * hb - 1, 0)
    next_map = lambda i: jnp.minimum((i + 1) * hb, last_halo)
    return pl.pallas_call(
        body, name=name, grid=(n_tiles,),
        in_specs=[pl.BlockSpec((tm, C), lambda i: (i, 0)),
                  pl.BlockSpec((tm, C), lambda i: (i, 1)),
                  pl.BlockSpec((tm, C), lambda i: (i, 2)),
                  pl.BlockSpec((SC_HALO, C), lambda i: (prev_map(i), 1)),
                  pl.BlockSpec((SC_HALO, C), lambda i: (prev_map(i), 2)),
                  pl.BlockSpec((SC_HALO, C), lambda i: (next_map(i), 0)),
                  pl.BlockSpec((tm, C), lambda i: (i, 0)),
                  pl.BlockSpec((SC_HALO, C), lambda i: (next_map(i), 0)),
                  pl.BlockSpec((8, C), lambda i: (0, 0))],
        out_specs=[pl.BlockSpec((tm, 3 * C), lambda i: (i, 0)),
                   pl.BlockSpec((8, C), lambda i: (0, 0))],
        out_shape=[jax.ShapeDtypeStruct((T, 3 * C), BF16), jax.ShapeDtypeStruct((8, C), F32)],
        scratch_shapes=[pltpu.VMEM((tm + SC_HALO, C), F32), pltpu.VMEM((R, C), F32)],
        compiler_params=_params(1),
    )(z, z, z, z, z, z, dm, dm, w)


def _xa_probs(q_h, k_h):
    s = _dot_nt(q_h, k_h) * (XA_HEAD_DIM ** -0.5)
    p = jnp.exp(s - jnp.max(s, axis=-1, keepdims=True))
    return p * (1.0 / jnp.sum(p, axis=-1, keepdims=True))


def xattn_fwd(q, kv, name, tm=512):
    T = q.shape[0]
    M = kv.shape[0]
    tm = _tile(T, tm)

    def body(q_ref, k_ref, v_ref, o_ref):
        for h in range(XA_HEADS):
            hs = slice(h * XA_HEAD_DIM, (h + 1) * XA_HEAD_DIM)
            p = _xa_probs(q_ref[:, hs], k_ref[:, hs])
            o_ref[:, hs] = _dot(p.astype(BF16), v_ref[:, hs]).astype(BF16)

    return pl.pallas_call(
        body, name=name, grid=(T // tm,),
        in_specs=[pl.BlockSpec((tm, D_MODEL), lambda i: (i, 0)),
                  pl.BlockSpec((M, D_MODEL), lambda i: (0, 0)),
                  pl.BlockSpec((M, D_MODEL), lambda i: (0, 1))],
        out_specs=pl.BlockSpec((tm, D_MODEL), lambda i: (i, 0)),
        out_shape=jax.ShapeDtypeStruct((T, D_MODEL), BF16),
        compiler_params=_params(1),
    )(q, kv, kv)


def xattn_bwd(q, kv, do, name, tm=512):
    T = q.shape[0]
    M = kv.shape[0]
    tm = _tile(T, tm)
    scale = XA_HEAD_DIM ** -0.5

    def body(q_ref, k_ref, v_ref, do_ref, dq_ref, dkv_ref):
        @pl.when(pl.program_id(0) == 0)
        def _():
            dkv_ref[...] = jnp.zeros_like(dkv_ref)

        for h in range(XA_HEADS):
            hs = slice(h * XA_HEAD_DIM, (h + 1) * XA_HEAD_DIM)
            vs = slice(D_MODEL + h * XA_HEAD_DIM, D_MODEL + (h + 1) * XA_HEAD_DIM)
            q_h = q_ref[:, hs]
            do_h = do_ref[:, hs]
            p = _xa_probs(q_h, k_ref[:, hs])
            dp = _dot_nt(do_h, v_ref[:, hs])
            ds = (p * (dp - jnp.sum(p * dp, axis=-1, keepdims=True))).astype(BF16)
            dq_ref[:, hs] = (_dot(ds, k_ref[:, hs]) * scale).astype(BF16)
            dkv_ref[:, hs] += _dot_tn(ds, q_h) * scale
            dkv_ref[:, vs] += _dot_tn(p.astype(BF16), do_h)

    return pl.pallas_call(
        body, name=name, grid=(T // tm,),
        in_specs=[pl.BlockSpec((tm, D_MODEL), lambda i: (i, 0)),
                  pl.BlockSpec((M, D_MODEL), lambda i: (0, 0)),
                  pl.BlockSpec((M, D_MODEL), lambda i: (0, 1)),
                  pl.BlockSpec((tm, D_MODEL), lambda i: (i, 0))],
        out_specs=[pl.BlockSpec((tm, D_MODEL), lambda i: (i, 0)),
                   pl.BlockSpec((M, 2 * D_MODEL), lambda i: (0, 0))],
        out_shape=[jax.ShapeDtypeStruct((T, D_MODEL), BF16), jax.ShapeDtypeStruct((M, 2 * D_MODEL), F32)],
        compiler_params=_params(1),
    )(q, kv, kv, do)


def final_loss(h, g, target, name, tm=512):
    T, K = h.shape
    tm = _tile(T, tm)

    def body(h_ref, g_ref, t_ref, dh_ref, dg_ref, loss_ref):
        @pl.when(pl.program_id(0) == 0)
        def _():
            dg_ref[...] = jnp.zeros_like(dg_ref)
            loss_ref[...] = jnp.zeros_like(loss_ref)

        x = h_ref[...]
        r = lax.rsqrt(jnp.mean(x * x, axis=-1, keepdims=True) + RMS_EPS)
        xh = x * r
        e = xh * g_ref[...] - t_ref[...]
        loss_ref[...] += jnp.zeros((1, 128), F32) + 0.5 * jnp.sum(jnp.mean(e * e, axis=-1, keepdims=True))
        dy = e * (1.0 / K)
        dg_ref[...] += jnp.sum(dy * xh, axis=0, keepdims=True)
        dxh = dy * g_ref[...]
        dh_ref[...] = r * (dxh - xh * jnp.mean(dxh * xh, axis=-1, keepdims=True))

    return pl.pallas_call(
        body, name=name, grid=(T // tm,),
        in_specs=[pl.BlockSpec((tm, K), lambda i: (i, 0)),
                  pl.BlockSpec((1, K), lambda i: (0, 0)),
                  pl.BlockSpec((tm, K), lambda i: (i, 0))],
        out_specs=[pl.BlockSpec((tm, K), lambda i: (i, 0)),
                   pl.BlockSpec((1, K), lambda i: (0, 0)),
                   pl.BlockSpec((1, 128), lambda i: (0, 0))],
        out_shape=[jax.ShapeDtypeStruct((T, K), F32), jax.ShapeDtypeStruct((1, K), F32),
                   jax.ShapeDtypeStruct((1, 128), F32)],
        compiler_params=_params(1),
    )(h, g, target)


def _row(v):
    return v.reshape(1, -1)


def _pad_rows(a, rows):
    return jnp.pad(a, ((0, rows - a.shape[0]), (0, 0)))


def local_step(x, mem, target, P, get_weights, put_grads):
    cw = _pad_rows(P["conv_a_w"], 32)
    scw = _pad_rows(P["sc_conv_w"], 8)
    cb, lg, lb = _row(P["conv_a_b"]), _row(P["conv_a_ln_g"]), _row(P["conv_a_ln_b"])
    sinks = P["swa_sinks"]

    class _Layered:
        def __init__(self, store, name=None):
            self.store, self.name = store, name

        def __getitem__(self, key):
            if self.name is None:
                return self.store[(key, 0)] if key in ("even_w_in", "even_w_out", "odd_w_in", "odd_w_out") \
                    else _Layered(self.store, key)
            return self.store[(self.name, key)]

    store = {}
    W = _Layered(store)
    saved = []
    h = x
    u1 = None
    for i in range(2):
        L = f"l{i}"
        new, dep = get_weights("A" if i == 0 else "D", h)
        store.update(new)

        h, u2, s_ffn1 = ffn_forward(h, P["ffn1_norm"][i:i + 1], W["ffn1_w_gu"][i], W["ffn1_w_down"][i], L + "_ffn1",
                                    dep=dep, u=u1, g_next=P["mix_norm"][i:i + 1])
        h1 = h
        if i == 0:
            new, _ = get_weights("B", h)
            store.update(new)
            z = matmul(u2, W["even_w_in"], F32, L + "_mix_in", 768, n_tiles=2)
            kv = matmul(u2, W["even_w_in"], BF16, L + "_mix_kv", 256, n_tiles=1, first_tile=6)
            a, conv_out = conformer_conv_fwd(z, cw, cb, lg, lb, L + "_conv")
            kpad = jnp.pad(kv[:, :2 * HEAD_DIM], ((WINDOW, 0), (0, 0)))
            vpad = jnp.pad(kv[:, 2 * HEAD_DIM:], ((WINDOW, 0), (0, 0)))
            o = swa_fwd(z, kpad, vpad, sinks, L + "_swa")
            m = jnp.concatenate([a, o], axis=-1)
            h, u3 = matmul_residual(m, W["even_w_out"], h1, P["xa_norm"][i:i + 1], L + "_mix_out")
            s_mix = (h1, u2, z, m, kpad, vpad, conv_out)
        else:
            z = matmul(u2, W["odd_w_in"], F32, L + "_mix_in", 1024)
            m = short_conv_fwd(z, scw, L + "_sconv")
            h, u3 = matmul_residual(m, W["odd_w_out"], h1, P["xa_norm"][i:i + 1], L + "_mix_out")
            s_mix = (h1, u2, z, m)
        h2 = h
        kv, umem = norm_matmul(mem, P["xa_mem_norm"][i:i + 1], W["xa_wkv"][i], BF16, L + "_xa_kv", 2 * D_MODEL)
        q, u3 = norm_matmul(h2, P["xa_norm"][i:i + 1], W["xa_wq"][i], BF16, L + "_xa_q", D_MODEL, transposed=False, u=u3)
        o = xattn_fwd(q, kv, L + "_xa")
        h, u4 = matmul_residual(o, W["xa_wo"][i], h2, P["ffn2_norm"][i:i + 1], L + "_xa_out")
        s_xa = (h2, u3, q, o, kv, umem)
        new, _ = get_weights("C" if i == 0 else "E", h)
        store.update(new)
        h, u1, s_ffn2 = ffn_forward(h, P["ffn2_norm"][i:i + 1], W["ffn2_w_gu"][i], W["ffn2_w_down"][i], L + "_ffn2", u=u4,
                                    g_next=P["ffn1_norm"][1:2] if i == 0 else None)
        saved.append((s_ffn1, s_mix, s_xa, s_ffn2))

    dh, d_final, loss = final_loss(h, _row(P["final_norm"]), target, "final_loss")

    names = ("ffn1_w_gu", "ffn1_w_down", "ffn2_w_gu", "ffn2_w_down", "xa_wq", "xa_wkv", "xa_wo", "even_w_in", "even_w_out",
             "odd_w_in", "odd_w_out")
    dW = {k: [None, None] for k in names}
    dP = {k: [None, None] for k in ("ffn1_norm", "mix_norm", "xa_norm", "xa_mem_norm", "ffn2_norm")}
    dP["final_norm"] = d_final.reshape(-1)
    for i in (1, 0):
        L = f"l{i}b"
        s_ffn1, s_mix, s_xa, s_ffn2 = saved[i]

        def keep_ffn2(d_w_gu, d_w_down, i=i):
            dW["ffn2_w_gu"][i], dW["ffn2_w_down"][i] = d_w_gu, d_w_down

        def send_ffn1(d_w_gu, d_w_down, i=i):
            dW["ffn1_w_gu"][i], dW["ffn1_w_down"][i] = d_w_gu, d_w_down
            keys = STAGE_KEYS["D"] + STAGE_KEYS["E"] if i == 1 else STAGE_KEYS["A"]
            return put_grads("D" if i == 1 else "A", {k: dW[k[0]][k[1]] for k in keys})

        dh, dP["ffn2_norm"][i] = ffn_backward(
            dh, s_ffn2, P["ffn2_norm"][i:i + 1], W["ffn2_w_gu"][i], W["ffn2_w_down"][i], L + "_ffn2", emit=keep_ffn2)
        h2, u3, q, o, kv, umem = s_xa
        dW["xa_wo"][i] = matmul_tn(o, dh, L + "_xa_dwo")
        do = matmul_nt(dh, W["xa_wo"][i], BF16, L + "_xa_do")
        dq, dkv = xattn_bwd(q, kv, do, L + "_xa")
        dW["xa_wq"][i] = matmul_tn(u3, dq, L + "_xa_dwq")
        dW["xa_wkv"][i] = matmul_tn(dkv, umem, L + "_xa_dwkv", tk=1024)
        dkv_b = dkv.astype(BF16)
        _, dP["xa_mem_norm"][i] = matmul_norm_bwd(dkv_b, W["xa_wkv"][i], mem, P["xa_mem_norm"][i:i + 1],
                                                  jnp.zeros_like(mem), L + "_xa_dmem")
        dh, dP["xa_norm"][i] = matmul_norm_bwd(dq, W["xa_wq"][i], h2, P["xa_norm"][i:i + 1], dh, L + "_xa_dx",
                                               transposed=False)
        if i == 0:
            h1, u2, z, m, kpad, vpad, conv_out = s_mix
            dW["even_w_out"][0] = matmul_tn(m, dh, L + "_mix_dwo")
            dm = matmul_nt(dh, W["even_w_out"], F32, L + "_mix_dm")
            dz_conv, dcw, dcb, dlg, dlb = conformer_conv_bwd(z, conv_out, dm, cw, lg, lb, L + "_conv")
            dq_s, dkp, dvp, dsk = swa_bwd(z, kpad, vpad, sinks, dm, L + "_swa")
            dz = jnp.concatenate([dz_conv, dq_s, dkp[WINDOW:].astype(BF16), dvp[WINDOW:].astype(BF16)], axis=-1)
            dW["even_w_in"][0] = matmul_tn(dz, u2, L + "_mix_dwi", tk=896)
            dh, dP["mix_norm"][i] = matmul_norm_bwd(dz, W["even_w_in"], h1, P["mix_norm"][i:i + 1], dh, L + "_mix_dx")
            dP["conv_a_w"] = dcw[:CONV_A_WIDTH]
            dP["conv_a_b"], dP["conv_a_ln_g"], dP["conv_a_ln_b"] = dcb.reshape(-1), dlg.reshape(-1), dlb.reshape(-1)
            dP["swa_sinks"] = dsk[:, 0]
        else:
            h1, u2, z, m = s_mix
            dW["odd_w_out"][0] = matmul_tn(m, dh, L + "_mix_dwo")
            dm = matmul_nt(dh, W["odd_w_out"], F32, L + "_mix_dm")
            dz, dscw = short_conv_bwd(z, dm, scw, L + "_sconv")
            dW["odd_w_in"][0] = matmul_tn(dz, u2, L + "_mix_dwi", tk=1024)
            dh, dP["mix_norm"][i] = matmul_norm_bwd(dz, W["odd_w_in"], h1, P["mix_norm"][i:i + 1], dh, L + "_mix_dx")
            dP["sc_conv_w"] = dscw[:3]
        dep = put_grads("BC", {k: dW[k[0]][k[1]] for k in STAGE_KEYS["B"] + STAGE_KEYS["C"]}) if i == 0 else None
        dh, dP["ffn1_norm"][i] = ffn_backward(
            dh, s_ffn1, P["ffn1_norm"][i:i + 1], W["ffn1_w_gu"][i], W["ffn1_w_down"][i], L + "_ffn1", dep=dep,
            emit=send_ffn1)
    for k in ("ffn1_norm", "mix_norm", "xa_norm", "xa_mem_norm", "ffn2_norm"):
        dP[k] = jnp.concatenate(dP[k], axis=0)
    return loss, dh, dP


def _mesh_pos():
    return lax.axis_index("x"), lax.axis_index("y"), lax.axis_index("c")


def _flat_index(px, py, pc):
    return 4 * px + 2 * py + pc


def all_gather(blob, name, dep=None):
    R, C = blob.shape

    def kern(x_ref, out_ref, send_sems, recv_sems, local_sem):
        x, y, c = _mesh_pos()
        me, sibling = (x, y, c), (x, y, 1 - c)
        chips = [(1 - x, y), (x, 1 - y), (1 - x, 1 - y)]

        def slot(px, py, pc):
            return out_ref.at[_flat_index(px, py, pc)]

        def copy(k, block, to, src=None):
            return pltpu.make_async_remote_copy(
                src_ref=slot(*block) if src is None else src, dst_ref=slot(*block),
                send_sem=send_sems.at[k], recv_sem=recv_sems.at[k],
                device_id=to, device_id_type=pl.DeviceIdType.MESH)

        mine = pltpu.make_async_copy(x_ref, slot(*me), local_sem)
        mine.start()
        first = [copy(0, me, sibling, src=x_ref)]
        first += [copy(1 + j, me, (*chip, c), src=x_ref) for j, chip in enumerate(chips)]
        for cp in first:
            cp.start()
        passed = [copy(4 + j, (*chip, c), sibling) for j, chip in enumerate(chips)]
        for j, chip in enumerate(chips):
            copy(1 + j, (*chip, c), me).wait_recv()
            passed[j].start()
        copy(0, sibling, me).wait_recv()
        for j, chip in enumerate(chips):
            copy(4 + j, (*chip, 1 - c), me).wait_recv()
        for cp in first + passed:
            cp.wait_send()
        mine.wait()

    body, dep_spec, dep_arg = _with_dep(kern, 1, dep)
    return pl.pallas_call(
        body, name=name,
        out_shape=jax.ShapeDtypeStruct((N_DEV, R, C), blob.dtype),
        in_specs=[ANY_SPEC] + dep_spec,
        out_specs=ANY_SPEC,
        scratch_shapes=[pltpu.SemaphoreType.DMA((7,)), pltpu.SemaphoreType.DMA((7,)), pltpu.SemaphoreType.DMA],
    )(blob, *dep_arg)


HBM_SPEC = pl.BlockSpec(memory_space=pltpu.HBM)
SEM_SPEC = pl.BlockSpec(memory_space=pltpu.SEMAPHORE)
DATAFLOW_EFFECT = pltpu.SideEffectType.DATAFLOW_SIDE_EFFECTING


def _peers(x, y, c):
    out = []
    for k in range(1, N_DEV):
        pos = (1 - x if k & 4 else x, 1 - y if k & 2 else y, 1 - c if k & 1 else c)
        out.append((_flat_index(*pos), pos))
    return out


def _exchange_copy(src_ref, land_ref, send_sems, recv_sems, j, me, peer_idx, peer, scatter):
    return pltpu.make_async_remote_copy(
        src_ref=src_ref.at[peer_idx] if scatter else src_ref, dst_ref=land_ref.at[me],
        send_sem=send_sems.at[j], recv_sem=recv_sems.at[j], device_id=peer, device_id_type=pl.DeviceIdType.MESH)


def exchange_start(srcs, lands, scatter, after, name):
    n = len(srcs)
    n_after = len(after)

    def body(*refs):
        src_refs, land_refs = refs[:n], refs[n:2 * n]
        outs = refs[2 * n + n_after:]
        send_sems, recv_sems, token = outs[:n], outs[n:2 * n], outs[4 * n]
        x, y, c = _mesh_pos()
        me = _flat_index(x, y, c)
        for g in range(n):
            for j, (peer_idx, peer) in enumerate(_peers(x, y, c)):
                _exchange_copy(src_refs[g], land_refs[g], send_sems[g], recv_sems[g], j, me, peer_idx, peer, scatter).start()
        token[...] = jnp.zeros_like(token)

    hbm = lambda a: pltpu.with_memory_space_constraint(a, pltpu.HBM)
    res = pl.pallas_call(
        body, name=name,
        out_shape=(*[pltpu.SemaphoreType.DMA((N_DEV - 1,))] * (2 * n),
                   *[pltpu.HBM(a.shape, a.dtype) for a in srcs], *[pltpu.HBM(a.shape, a.dtype) for a in lands],
                   jax.ShapeDtypeStruct((8, 128), F32)),
        in_specs=[HBM_SPEC] * (2 * n) + [ANY_SPEC] * n_after,
        out_specs=(*[SEM_SPEC] * (2 * n), *[HBM_SPEC] * (2 * n), pl.BlockSpec(memory_space=pltpu.VMEM)),
        input_output_aliases={i: 2 * n + i for i in range(2 * n)},
        compiler_params=pltpu.CompilerParams(has_side_effects=DATAFLOW_EFFECT),
    )(*[hbm(a) for a in srcs], *[hbm(a) for a in lands], *after)
    handles = [(res[g], res[n + g], res[2 * n + g], res[3 * n + g]) for g in range(n)]
    return handles, res[4 * n]


def exchange_wait(handles, scatter, after, name):
    n = len(handles)

    def body(*refs):
        src_refs, land_refs = refs[:n], refs[n:2 * n]
        send_sems, recv_sems = refs[2 * n:3 * n], refs[3 * n:4 * n]
        x, y, c = _mesh_pos()
        me = _flat_index(x, y, c)
        for g in range(n):
            for j, (peer_idx, peer) in enumerate(_peers(x, y, c)):
                mine = _exchange_copy(src_refs[g], land_refs[g], send_sems[g], recv_sems[g], j, me, peer_idx, peer,
                                      scatter)
                mine.wait_send()
                theirs = pltpu.make_async_remote_copy(
                    src_ref=src_refs[g].at[me] if scatter else src_refs[g], dst_ref=land_refs[g].at[peer_idx],
                    send_sem=send_sems[g].at[j], recv_sem=recv_sems[g].at[j], device_id=peer,
                    device_id_type=pl.DeviceIdType.MESH)
                theirs.wait_recv()

    srcs = [h[2] for h in handles]
    lands = [h[3] for h in handles]
    res = pl.pallas_call(
        body, name=name,
        out_shape=tuple(pltpu.HBM(a.shape, a.dtype) for a in srcs + lands),
        in_specs=[HBM_SPEC] * (2 * n) + [SEM_SPEC] * (2 * n) + [ANY_SPEC],
        out_specs=tuple([HBM_SPEC] * (2 * n)),
        input_output_aliases={i: i for i in range(2 * n)},
        compiler_params=pltpu.CompilerParams(has_side_effects=DATAFLOW_EFFECT),
    )(*srcs, *lands, *[h[0] for h in handles], *[h[1] for h in handles], after)
    return [(res[g], res[n + g]) for g in range(n)]


def ordered_sum(parts, name, tr=512):
    n, R, C = parts.shape
    tr = next((t for t in range(min(tr, R), 15, -16) if R % t == 0), R)

    def body(p_ref, o_ref):
        acc = p_ref[0].astype(F32)
        for j in range(1, n):
            acc = acc + p_ref[j].astype(F32)
        o_ref[...] = acc

    return pl.pallas_call(
        body, name=name, grid=(R // tr,),
        in_specs=[pl.BlockSpec((n, tr, C), lambda i: (0, i, 0))],
        out_specs=pl.BlockSpec((tr, C), lambda i: (i, 0)),
        out_shape=jax.ShapeDtypeStruct((R, C), F32),
        compiler_params=_params(1),
    )(parts)


def adamw(w, g, m, v, name, tr=256):
    R, C = w.shape
    tr = next((t for t in range(tr, 7, -8) if R % t == 0), R)
    c1 = 1.0 - ADAM_B1 ** ADAM_STEP
    c2 = 1.0 - ADAM_B2 ** ADAM_STEP

    def body(w_ref, g_ref, m_ref, v_ref, d_ref, mo_ref, vo_ref):
        grad = g_ref[...]
        m2 = ADAM_B1 * m_ref[...] + (1.0 - ADAM_B1) * grad
        v2 = ADAM_B2 * v_ref[...] + (1.0 - ADAM_B2) * (grad * grad)
        mo_ref[...] = m2
        vo_ref[...] = v2
        d_ref[...] = -ADAM_LR * ((m2 / c1) / (jnp.sqrt(v2 / c2) + ADAM_EPS) + ADAM_WD * w_ref[...])

    spec = pl.BlockSpec((tr, C), lambda i: (i, 0))
    return pl.pallas_call(
        body, name=name, grid=(R // tr,),
        in_specs=[spec] * 4, out_specs=[spec] * 3,
        out_shape=[jax.ShapeDtypeStruct((R, C), F32)] * 3,
        compiler_params=_params(1),
    )(w, g, m, v)


WEIGHT_NAMES = ("ffn1_norm", "ffn1_w_gu", "ffn1_w_down", "mix_norm", "even_w_in", "conv_a_w", "conv_a_b", "conv_a_ln_g",
                "conv_a_ln_b", "swa_sinks", "even_w_out", "odd_w_in", "sc_conv_w", "odd_w_out", "xa_norm", "xa_mem_norm",
                "xa_wq", "xa_wkv", "xa_wo", "ffn2_norm", "ffn2_w_gu", "ffn2_w_down", "final_norm")
BLOB_COLS = 1024
SMALL_ROWS = (("ffn1_norm", 0, 2), ("mix_norm", 2, 2), ("xa_norm", 4, 2), ("xa_mem_norm", 6, 2), ("ffn2_norm", 8, 2),
              ("final_norm", 10, 1))
ROW_CONV_B_LNG = 11
ROW_LNB_SINKS_LOSS = 12
LOSS_COL = 512 + SWA_HEADS
ROW_SC_CONV = 13
ROW_CONV_W = 16
SMALL_BLOB_ROWS = 32
SMALL_ADAM_ROWS = 16


def _small_blob(v):
    rows = [v[n].reshape(-1, D_MODEL) for n, _, _ in SMALL_ROWS]
    rows.append(jnp.concatenate([v["conv_a_b"].reshape(-1), v["conv_a_ln_g"].reshape(-1)]).reshape(1, D_MODEL))
    tail = jnp.zeros((D_MODEL - 512 - SWA_HEADS,), F32)
    if "loss" in v:
        tail = tail.at[0].set(v["loss"])
    rows.append(jnp.concatenate([v["conv_a_ln_b"].reshape(-1), v["swa_sinks"].reshape(-1), tail]).reshape(1, D_MODEL))
    rows.append(jnp.zeros((SMALL_ADAM_ROWS - ROW_SC_CONV, D_MODEL), F32))
    return jnp.concatenate(rows, axis=0)


def _small_unblob(b, shapes):
    out = {n: b[r:r + k].reshape(shapes[n]) for n, r, k in SMALL_ROWS}
    out["conv_a_b"] = b[ROW_CONV_B_LNG, :512].reshape(shapes["conv_a_b"])
    out["conv_a_ln_g"] = b[ROW_CONV_B_LNG, 512:].reshape(shapes["conv_a_ln_g"])
    out["conv_a_ln_b"] = b[ROW_LNB_SINKS_LOSS, :512].reshape(shapes["conv_a_ln_b"])
    out["swa_sinks"] = b[ROW_LNB_SINKS_LOSS, 512:512 + SWA_HEADS].reshape(shapes["swa_sinks"])
    return out


def kernel(x, mem, ffn1_norm, ffn1_w_gu, ffn1_w_down, mix_norm, even_w_in, conv_a_w, conv_a_b, conv_a_ln_g, conv_a_ln_b, swa_sinks, even_w_out, odd_w_in, sc_conv_w, odd_w_out, xa_norm, xa_mem_norm, xa_wq, xa_wkv, xa_wo, ffn2_norm, ffn2_w_gu, ffn2_w_down, final_norm, loss_target, m_ffn1_norm, m_ffn1_w_gu, m_ffn1_w_down, m_mix_norm, m_even_w_in, m_conv_a_w, m_conv_a_b, m_conv_a_ln_g, m_conv_a_ln_b, m_swa_sinks, m_even_w_out, m_odd_w_in, m_sc_conv_w, m_odd_w_out, m_xa_norm, m_xa_mem_norm, m_xa_wq, m_xa_wkv, m_xa_wo, m_ffn2_norm, m_ffn2_w_gu, m_ffn2_w_down, m_final_norm, v_ffn1_norm, v_ffn1_w_gu, v_ffn1_w_down, v_mix_norm, v_even_w_in, v_conv_a_w, v_conv_a_b, v_conv_a_ln_g, v_conv_a_ln_b, v_swa_sinks, v_even_w_out, v_odd_w_in, v_sc_conv_w, v_odd_w_out, v_xa_norm, v_xa_mem_norm, v_xa_wq, v_xa_wkv, v_xa_wo, v_ffn2_norm, v_ffn2_w_gu, v_ffn2_w_down, v_final_norm):
    w = dict(ffn1_norm=ffn1_norm, ffn1_w_gu=ffn1_w_gu, ffn1_w_down=ffn1_w_down, mix_norm=mix_norm, even_w_in=even_w_in,
             conv_a_w=conv_a_w, conv_a_b=conv_a_b, conv_a_ln_g=conv_a_ln_g, conv_a_ln_b=conv_a_ln_b, swa_sinks=swa_sinks,
             even_w_out=even_w_out, odd_w_in=odd_w_in, sc_conv_w=sc_conv_w, odd_w_out=odd_w_out, xa_norm=xa_norm,
             xa_mem_norm=xa_mem_norm, xa_wq=xa_wq, xa_wkv=xa_wkv, xa_wo=xa_wo, ffn2_norm=ffn2_norm, ffn2_w_gu=ffn2_w_gu,
             ffn2_w_down=ffn2_w_down, final_norm=final_norm)
    m = dict(ffn1_norm=m_ffn1_norm, ffn1_w_gu=m_ffn1_w_gu, ffn1_w_down=m_ffn1_w_down, mix_norm=m_mix_norm,
             even_w_in=m_even_w_in, conv_a_w=m_conv_a_w, conv_a_b=m_conv_a_b, conv_a_ln_g=m_conv_a_ln_g,
             conv_a_ln_b=m_conv_a_ln_b, swa_sinks=m_swa_sinks, even_w_out=m_even_w_out, odd_w_in=m_odd_w_in,
             sc_conv_w=m_sc_conv_w, odd_w_out=m_odd_w_out, xa_norm=m_xa_norm, xa_mem_norm=m_xa_mem_norm, xa_wq=m_xa_wq,
             xa_wkv=m_xa_wkv, xa_wo=m_xa_wo, ffn2_norm=m_ffn2_norm, ffn2_w_gu=m_ffn2_w_gu, ffn2_w_down=m_ffn2_w_down,
             final_norm=m_final_norm)
    v = dict(ffn1_norm=v_ffn1_norm, ffn1_w_gu=v_ffn1_w_gu, ffn1_w_down=v_ffn1_w_down, mix_norm=v_mix_norm,
             even_w_in=v_even_w_in, conv_a_w=v_conv_a_w, conv_a_b=v_conv_a_b, conv_a_ln_g=v_conv_a_ln_g,
             conv_a_ln_b=v_conv_a_ln_b, swa_sinks=v_swa_sinks, even_w_out=v_even_w_out, odd_w_in=v_odd_w_in,
             sc_conv_w=v_sc_conv_w, odd_w_out=v_odd_w_out, xa_norm=v_xa_norm, xa_mem_norm=v_xa_mem_norm, xa_wq=v_xa_wq,
             xa_wkv=v_xa_wkv, xa_wo=v_xa_wo, ffn2_norm=v_ffn2_norm, ffn2_w_gu=v_ffn2_w_gu, ffn2_w_down=v_ffn2_w_down,
             final_norm=v_final_norm)
    me = _flat_index(*_mesh_pos())

    conv_blob = jnp.concatenate([w["conv_a_w"].reshape(-1), w["sc_conv_w"].reshape(-1),
                                 jnp.zeros((8 * 1024 - 31 * 64 - 3 * 128,), F32)]).reshape(8, 1024)
    conv_all = all_gather(conv_blob, "gather_conv_weights").reshape(N_DEV, 8 * 1024)
    conv_a_full = jnp.transpose(conv_all[:, :31 * 64].reshape(N_DEV, 31, 64), (1, 0, 2)).reshape(31, 512)
    sc_full = jnp.transpose(conv_all[:, 31 * 64:31 * 64 + 3 * 128].reshape(N_DEV, 3, 128), (1, 0, 2)).reshape(3, 1024)

    def shard_rows(n, l):
        return (w[n][l].T if SPLIT_AXIS[n] == 1 else w[n][l]).astype(BF16)

    def with_own(land, own):
        return lax.dynamic_update_slice(land, own[None], (me, 0, 0))

    a_keys = STAGE_KEYS["A"]
    gathered_a = all_gather(jnp.concatenate([shard_rows(n, l) for n, l in a_keys], axis=0), "gather_weights_a")
    later = ("B", "C", "D", "E")
    later_keys = [k for s in later for k in STAGE_KEYS[s]]
    shards = [shard_rows(n, l) for n, l in later_keys]
    lands = [lax.empty((N_DEV,) + s.shape, BF16) for s in shards]
    handles, weight_token = exchange_start(shards, lands, False, [gathered_a, conv_all], "gather_start")
    weight_handles = dict(zip(later_keys, handles))

    def get_weights(stage, after):
        keys = STAGE_KEYS[stage]
        if stage == "A":
            out, off = {}, 0
            for n, l in keys:
                rows = w[n].shape[2] if SPLIT_AXIS[n] == 1 else w[n].shape[1]
                out[(n, l)] = gathered_a[:, off:off + rows, :].reshape(N_DEV * rows, BLOB_COLS)
                off += rows
            return out, weight_token
        got = exchange_wait([weight_handles[k] for k in keys], False, after, "gather_wait_" + stage.lower())
        return {k: with_own(land, own).reshape(-1, BLOB_COLS) for k, (own, land) in zip(keys, got)}, None

    grad_handles = {}

    def put_grads(stage, dws):
        srcs = [dw.reshape(N_DEV, -1, BLOB_COLS) for dw in dws.values()]
        handles, token = exchange_start(srcs, [lax.empty(s.shape, BF16) for s in srcs], True, [],
                                        "scatter_start_" + stage.lower())
        grad_handles[stage] = (handles, tuple(dws))
        return token

    P = dict(ffn1_norm=ffn1_norm, mix_norm=mix_norm, xa_norm=xa_norm, xa_mem_norm=xa_mem_norm, ffn2_norm=ffn2_norm,
             final_norm=final_norm, conv_a_w=conv_a_full, conv_a_b=conv_a_b[0], conv_a_ln_g=conv_a_ln_g[0],
             conv_a_ln_b=conv_a_ln_b[0], swa_sinks=swa_sinks[0], sc_conv_w=sc_full)

    loss_part, grad_x, dP = local_step(x[0], mem[0], loss_target[0], P, get_weights, put_grads)

    def finish_grads(stage, after):
        handles, keys = grad_handles[stage]
        got = exchange_wait(handles, True, after, "scatter_wait_" + stage.lower())
        out = {}
        for (n, l), (src, land) in zip(keys, got):
            own = lax.dynamic_slice(src, (me, 0, 0), (1,) + src.shape[1:])[0]
            part = ordered_sum(with_own(land, own), f"sum_grads_{n}_{l}")
            out[(n, l)] = part.T if SPLIT_AXIS[n] == 1 else part
        return out

    layer_grads = {**finish_grads("D", grad_x), **finish_grads("BC", grad_x)}

    grads, delta, new_m, new_v = {}, {}, {}, {}

    def update(n):
        shp = w[n].shape
        two_d = (shp[0] * shp[1], shp[2])
        d_, m_, v_ = adamw(w[n].reshape(two_d), grads[n].reshape(two_d), m[n].reshape(two_d), v[n].reshape(two_d),
                           "adamw_" + n)
        delta[n], new_m[n], new_v[n] = d_.reshape(shp), m_.reshape(shp), v_.reshape(shp)

    first_stage = tuple(n for n, _ in STAGE_KEYS["A"])
    for n in SPLIT_AXIS:
        if n not in first_stage:
            grads[n] = jnp.stack([layer_grads[(n, l)] for l in range(w[n].shape[0])], axis=0)
            update(n)
    layer_grads.update(finish_grads("A", delta["ffn2_w_gu"]))

    dP = dict(dP, loss=loss_part[0, 0])
    small = jnp.concatenate([
        _small_blob(dP)[:ROW_SC_CONV], dP["sc_conv_w"],
        jnp.concatenate([dP["conv_a_w"].reshape(-1), jnp.zeros((512,), F32)]).reshape(16, D_MODEL)], axis=0)
    small_all = all_gather(small, "gather_small_grads", dep=layer_grads[STAGE_KEYS["A"][-1]])
    small_sum = ordered_sum(small_all, "sum_small_grads", tr=SMALL_BLOB_ROWS)
    loss = small_sum[ROW_LNB_SINKS_LOSS, LOSS_COL]
    grads.update(_small_unblob(small_sum, {n: w[n].shape for n in WEIGHT_NAMES}))
    sc_g = small_sum[ROW_SC_CONV:ROW_SC_CONV + 3]
    grads["sc_conv_w"] = lax.dynamic_slice(sc_g, (0, me * 128), (3, 128)).reshape(w["sc_conv_w"].shape)
    cw_g = small_sum[ROW_CONV_W:].reshape(-1)[:31 * 512].reshape(31, 512)
    grads["conv_a_w"] = lax.dynamic_slice(cw_g, (0, me * 64), (31, 64)).reshape(w["conv_a_w"].shape)

    update("conv_a_w")
    update("sc_conv_w")
    for n in first_stage:
        grads[n] = jnp.stack([layer_grads[(n, l)] for l in range(w[n].shape[0])], axis=0)
        update(n)
    d_, m_, v_ = adamw(_small_blob(w), small_sum[:SMALL_ADAM_ROWS], _small_blob(m), _small_blob(v), "adamw_small",
                       tr=SMALL_ADAM_ROWS)
    shapes = {n: w[n].shape for n in WEIGHT_NAMES}
    delta.update(_small_unblob(d_, shapes))
    new_m.update(_small_unblob(m_, shapes))
    new_v.update(_small_unblob(v_, shapes))

    return (loss, grad_x[None], *[grads[n] for n in WEIGHT_NAMES], *[delta[n] for n in WEIGHT_NAMES],
            *[new_m[n] for n in WEIGHT_NAMES], *[new_v[n] for n in WEIGHT_NAMES])
```

```python
import functools

import jax
import jax.numpy as jnp
from jax import lax
from jax.experimental import pallas as pl
from jax.experimental.pallas import tpu as pltpu

F32 = jnp.float32
BF16 = jnp.bfloat16

D_MODEL = 1024
D_FF = 2816
CONV_A_CH = 512
CONV_A_WIDTH = 31
SWA_HEADS = 8
SWA_KV_HEADS = 2
SWA_GROUP = SWA_HEADS // SWA_KV_HEADS
HEAD_DIM = 64
WINDOW = 128
SC_CH = 1024
XA_HEADS = 4
XA_HEAD_DIM = D_MODEL // XA_HEADS
RMS_EPS = 1e-6
LN_EPS = 1e-5
ADAM_LR = 0.001
ADAM_B1 = 0.9
ADAM_B2 = 0.999
ADAM_EPS = 1e-08
ADAM_WD = 0.01
ADAM_STEP = 10
N_DEV = 8

V7X_VMEM_BYTES = 64 * 1024 * 1024
VMEM_LIMIT = V7X_VMEM_BYTES - 8 * 1024 * 1024
CONV_HALO = 32
SC_HALO = 8
NEG_BIG = -1e30

SPLIT_AXIS = dict(ffn1_w_gu=1, ffn1_w_down=0, even_w_in=1, even_w_out=0, odd_w_in=1, odd_w_out=0, xa_wq=0, xa_wkv=1, xa_wo=0,
                  ffn2_w_gu=1, ffn2_w_down=0)
STAGE_KEYS = dict(
    A=(("ffn1_w_gu", 0), ("ffn1_w_down", 0)),
    B=(("even_w_in", 0), ("even_w_out", 0), ("xa_wq", 0), ("xa_wkv", 0), ("xa_wo", 0)),
    C=(("ffn2_w_gu", 0), ("ffn2_w_down", 0)),
    D=(("ffn1_w_gu", 1), ("ffn1_w_down", 1), ("odd_w_in", 0), ("odd_w_out", 0), ("xa_wq", 1), ("xa_wkv", 1), ("xa_wo", 1)),
    E=(("ffn2_w_gu", 1), ("ffn2_w_down", 1)))


def _params(n_axes):
    return pltpu.CompilerParams(dimension_semantics=("arbitrary",) * n_axes, vmem_limit_bytes=VMEM_LIMIT)


def _tile(n, pref):
    t = min(n, pref)
    assert n % t == 0, (n, pref)
    return t


def _dot(a, b):
    return jnp.dot(a, b, preferred_element_type=F32)


def _dot_nt(a, b):
    return lax.dot_general(a, b, (((1,), (1,)), ((), ())), preferred_element_type=F32)


def _dot_tn(a, b):
    return lax.dot_general(a, b, (((0,), (0,)), ((), ())), preferred_element_type=F32)


def _sigmoid(x):
    return 0.5 * jnp.tanh(0.5 * x) + 0.5


ANY_SPEC = pl.BlockSpec(memory_space=pl.ANY)


def _with_dep(body, n_in, dep):
    if dep is None:
        return body, [], []
    return (lambda *refs: body(*refs[:n_in], *refs[n_in + 1:])), [ANY_SPEC], [dep]


def rmsnorm(h, g, name, tm=1024, dep=None):
    T, K = h.shape
    tm = _tile(T, tm)

    def kern(h_ref, g_ref, u_ref):
        x = h_ref[...]
        r = lax.rsqrt(jnp.mean(x * x, axis=-1, keepdims=True) + RMS_EPS)
        u_ref[...] = ((x * r) * g_ref[...]).astype(BF16)

    body, dep_spec, dep_arg = _with_dep(kern, 2, dep)
    return pl.pallas_call(
        body, name=name, grid=(T // tm,),
        in_specs=[pl.BlockSpec((tm, K), lambda i: (i, 0)), pl.BlockSpec((1, K), lambda i: (0, 0))] + dep_spec,
        out_specs=pl.BlockSpec((tm, K), lambda i: (i, 0)),
        out_shape=jax.ShapeDtypeStruct((T, K), BF16),
        compiler_params=_params(1),
    )(h, g, *dep_arg)


def matmul(a, w, out_dtype, name, tn, tm=2048, transposed=True, n_tiles=None, first_tile=0):
    T, K = a.shape
    N = w.shape[0] if transposed else w.shape[1]
    n_tiles = N // tn if n_tiles is None else n_tiles
    tm = _tile(T, tm)
    mm = _dot_nt if transposed else _dot

    def body(a_ref, w_ref, z_ref):
        z_ref[...] = mm(a_ref[...], w_ref[...]).astype(z_ref.dtype)

    w_spec = (pl.BlockSpec((tn, K), lambda i, j: (first_tile + j, 0)) if transposed
              else pl.BlockSpec((K, tn), lambda i, j: (0, first_tile + j)))
    return pl.pallas_call(
        body, name=name, grid=(T // tm, n_tiles),
        in_specs=[pl.BlockSpec((tm, K), lambda i, j: (i, 0)), w_spec],
        out_specs=pl.BlockSpec((tm, tn), lambda i, j: (i, j)),
        out_shape=jax.ShapeDtypeStruct((T, n_tiles * tn), out_dtype),
        compiler_params=_params(2),
    )(a, w)


def norm_matmul(h, g, w, out_dtype, name, tn, dep=None, transposed=True, u=None, tm=2048):
    u = rmsnorm(h, g, name + "_norm", dep=dep) if u is None else u
    return matmul(u, w, out_dtype, name, tn, tm=tm, transposed=transposed), u


def matmul_residual(a, w, res, g_next, name, tm=1024):
    T, K = a.shape
    N = w.shape[1]
    tm = _tile(T, tm)

    def body(a_ref, w_ref, r_ref, g_ref, o_ref, u_ref):
        x = r_ref[...] + _dot(a_ref[...], w_ref[...])
        o_ref[...] = x
        r = lax.rsqrt(jnp.mean(x * x, axis=-1, keepdims=True) + RMS_EPS)
        u_ref[...] = ((x * r) * g_ref[...]).astype(BF16)

    return pl.pallas_call(
        body, name=name, grid=(T // tm,),
        in_specs=[pl.BlockSpec((tm, K), lambda i: (i, 0)),
                  pl.BlockSpec((K, N), lambda i: (0, 0)),
                  pl.BlockSpec((tm, N), lambda i: (i, 0)),
                  pl.BlockSpec((1, N), lambda i: (0, 0))],
        out_specs=[pl.BlockSpec((tm, N), lambda i: (i, 0)), pl.BlockSpec((tm, N), lambda i: (i, 0))],
        out_shape=[jax.ShapeDtypeStruct((T, N), F32), jax.ShapeDtypeStruct((T, N), BF16)],
        compiler_params=_params(1),
    )(a, w, res, g_next)


def matmul_nt(dy, w, out_dtype, name, tm=2048):
    T, N = dy.shape
    K = w.shape[0]
    tm = _tile(T, tm)

    def body(dy_ref, w_ref, o_ref):
        o_ref[...] = _dot_nt(dy_ref[...].astype(BF16), w_ref[...]).astype(o_ref.dtype)

    return pl.pallas_call(
        body, name=name, grid=(T // tm,),
        in_specs=[pl.BlockSpec((tm, N), lambda i: (i, 0)),
                  pl.BlockSpec((K, N), lambda i: (0, 0))],
        out_specs=pl.BlockSpec((tm, K), lambda i: (i, 0)),
        out_shape=jax.ShapeDtypeStruct((T, K), out_dtype),
        compiler_params=_params(1),
    )(dy, w)


def matmul_norm_bwd(dz, w, h, g, dh_in, name, tm=512, transposed=True, dep=None):
    T, N = dz.shape
    K = h.shape[1]
    tm = _tile(T, tm)

    def kern(dz_ref, w_ref, h_ref, g_ref, dhin_ref, dh_ref, dg_ref):
        @pl.when(pl.program_id(0) == 0)
        def _():
            dg_ref[...] = jnp.zeros_like(dg_ref)

        mm = _dot if transposed else _dot_nt
        du = mm(dz_ref[...], w_ref[...])
        x = h_ref[...]
        r = lax.rsqrt(jnp.mean(x * x, axis=-1, keepdims=True) + RMS_EPS)
        xh = x * r
        dg_ref[...] += jnp.sum(du * xh, axis=0, keepdims=True)
        dxh = du * g_ref[...]
        dh_ref[...] = dhin_ref[...] + r * (dxh - xh * jnp.mean(dxh * xh, axis=-1, keepdims=True))

    body, dep_spec, dep_arg = _with_dep(kern, 5, dep)
    return pl.pallas_call(
        body, name=name, grid=(T // tm,),
        in_specs=[pl.BlockSpec((tm, N), lambda i: (i, 0)),
                  pl.BlockSpec(w.shape, lambda i: (0, 0)),
                  pl.BlockSpec((tm, K), lambda i: (i, 0)),
                  pl.BlockSpec((1, K), lambda i: (0, 0)),
                  pl.BlockSpec((tm, K), lambda i: (i, 0))] + dep_spec,
        out_specs=[pl.BlockSpec((tm, K), lambda i: (i, 0)),
                   pl.BlockSpec((1, K), lambda i: (0, 0))],
        out_shape=[jax.ShapeDtypeStruct((T, K), F32), jax.ShapeDtypeStruct((1, K), F32)],
        compiler_params=_params(1),
    )(dz, w, h, g, dh_in, *dep_arg)


def matmul_tn(x, dy, name, scale=1.0, tk=None, tn=None, tt=1024):
    T, K = x.shape
    N = dy.shape[1]
    tk = K if tk is None else tk
    tn = N if tn is None else tn
    tt = _tile(T, tt)
    nt = T // tt

    def body(x_ref, dy_ref, o_ref, acc_ref):
        t = pl.program_id(2)

        @pl.when(t == 0)
        def _():
            acc_ref[...] = jnp.zeros_like(acc_ref)

        acc_ref[...] += _dot_tn(x_ref[...].astype(BF16), dy_ref[...].astype(BF16))

        @pl.when(t == nt - 1)
        def _():
            o_ref[...] = (acc_ref[...] * scale).astype(o_ref.dtype)

    return pl.pallas_call(
        body, name=name, grid=(K // tk, N // tn, nt),
        in_specs=[pl.BlockSpec((tt, tk), lambda a, b, t: (t, a)),
                  pl.BlockSpec((tt, tn), lambda a, b, t: (t, b))],
        out_specs=pl.BlockSpec((tk, tn), lambda a, b, t: (a, b)),
        out_shape=jax.ShapeDtypeStruct((K, N), BF16),
        scratch_shapes=[pltpu.VMEM((tk, tn), F32)],
        compiler_params=_params(3),
    )(x, dy)


def ffn_down(gu, wd, res, g_next, name, tm=512):
    T = gu.shape[0]
    F = gu.shape[1] // 2
    N = wd.shape[1]
    tm = _tile(T, tm)
    with_next = g_next is not None

    def body(g_ref, up_ref, w_ref, r_ref, *rest):
        o_ref, a_ref = rest[-3:-1] if with_next else rest[-2:]
        g = g_ref[...].astype(F32)
        a_ref[...] = ((g * _sigmoid(g)) * up_ref[...].astype(F32)).astype(BF16)
        x = r_ref[...] + 0.5 * _dot(a_ref[...], w_ref[...])
        o_ref[...] = x
        if with_next:
            r = lax.rsqrt(jnp.mean(x * x, axis=-1, keepdims=True) + RMS_EPS)
            rest[-1][...] = ((x * r) * rest[0][...]).astype(BF16)

    row = lambda width: pl.BlockSpec((tm, width), lambda i: (i, 0))
    res = pl.pallas_call(
        body, name=name, grid=(T // tm,),
        in_specs=[row(F), pl.BlockSpec((tm, F), lambda i: (i, 1)), pl.BlockSpec((F, N), lambda i: (0, 0)), row(N)]
        + ([pl.BlockSpec((1, N), lambda i: (0, 0))] if with_next else []),
        out_specs=[row(N), row(F)] + ([row(N)] if with_next else []),
        out_shape=[jax.ShapeDtypeStruct((T, N), F32), jax.ShapeDtypeStruct((T, F), BF16)]
        + ([jax.ShapeDtypeStruct((T, N), BF16)] if with_next else []),
        compiler_params=_params(1),
    )(gu, gu, wd, res, *([g_next] if with_next else []))
    return (res[0], res[1], res[2]) if with_next else (res[0], res[1], None)


def ffn_down_bwd(dy, wd, gu, name, tm=512, dep=None):
    T, N = dy.shape
    F = wd.shape[0]
    tm = _tile(T, tm)

    def kern(dy_ref, w_ref, g_ref, up_ref, o_ref):
        da = 0.5 * _dot_nt(dy_ref[...].astype(BF16), w_ref[...])
        g = g_ref[...].astype(F32)
        up = up_ref[...].astype(F32)
        s = _sigmoid(g)
        o_ref[:, :F] = (da * up * (s * (1.0 + g * (1.0 - s)))).astype(BF16)
        o_ref[:, F:] = (da * (g * s)).astype(BF16)

    body, dep_spec, dep_arg = _with_dep(kern, 4, dep)
    return pl.pallas_call(
        body, name=name, grid=(T // tm,),
        in_specs=[pl.BlockSpec((tm, N), lambda i: (i, 0)),
                  pl.BlockSpec((F, N), lambda i: (0, 0)),
                  pl.BlockSpec((tm, F), lambda i: (i, 0)),
                  pl.BlockSpec((tm, F), lambda i: (i, 1))] + dep_spec,
        out_specs=pl.BlockSpec((tm, 2 * F), lambda i: (i, 0)),
        out_shape=jax.ShapeDtypeStruct((T, 2 * F), BF16),
        compiler_params=_params(1),
    )(dy, wd, gu, gu, *dep_arg)


def ffn_forward(h, g, w_gu, w_down, name, dep=None, u=None, g_next=None):
    gu, u = norm_matmul(h, g, w_gu, BF16, name + "_gu", D_FF, dep=dep, u=u, tm=1024)
    h_out, a, u_next = ffn_down(gu, w_down, h, g_next, name + "_down")
    return h_out, u_next, (h, u, gu, a)


def ffn_backward(dy, saved, g, w_gu, w_down, name, dep=None, emit=None):
    h, u, gu, a = saved
    dgu = ffn_down_bwd(dy, w_down, gu, name + "_ddown", dep=dep)
    d_w_down = matmul_tn(a, dy, name + "_dwd", scale=0.5, tk=D_FF // 2, tt=2048)
    d_w_gu = matmul_tn(dgu, u, name + "_dwgu", tk=D_FF)
    dh, dg = matmul_norm_bwd(dgu, w_gu, h, g, dy, name + "_dx", dep=emit(d_w_gu, d_w_down))
    return dh, dg


CONV_ROW_CHUNK = 32
CONV_LANES = 128
CONV_X_OFFSETS = tuple(CONV_HALO - (CONV_A_WIDTH - 1) + k for k in range(CONV_A_WIDTH))
CONV_D_OFFSETS = tuple(CONV_A_WIDTH - 1 - k for k in range(CONV_A_WIDTH))


def _build_phases(ref, phase_ref, n_rows):
    for r in range(1, 8):
        phase_ref[r - 1] = ref[pl.ds(r, n_rows - 8), :]


def _tap_values(ref, phase_ref, offsets, n, base, lanes):
    out = {}
    for r in range(8):
        qs = sorted(o // 8 for o in offsets if o % 8 == r)
        if qs:
            lo, hi = qs[0], qs[-1]
            rows = pl.ds(base + 8 * lo, n + 8 * (hi - lo))
            span = ref[rows, lanes] if r == 0 else phase_ref[r - 1, rows, lanes]
            for q in qs:
                out[8 * q + r] = span[8 * (q - lo):8 * (q - lo) + n]
    return out


def conformer_conv_fwd(z, cw, cb, lg, lb, name, tm=512):
    T = z.shape[0]
    C = CONV_A_CH
    tm = _tile(T, tm)
    hb = tm // CONV_HALO
    CH = CONV_ROW_CHUNK
    KW = CONV_A_WIDTH

    def body(v_ref, gt_ref, pv_ref, pg_ref, cw_ref, cb_ref, lg_ref, lb_ref, o_ref, conv_ref, xs_ref, xph_ref):
        i = pl.program_id(0)
        prev = pv_ref[...] * _sigmoid(pg_ref[...])
        xs_ref[0:CONV_HALO, :] = jnp.where(i > 0, prev, 0.0)
        xs_ref[CONV_HALO:, :] = v_ref[...] * _sigmoid(gt_ref[...])
        _build_phases(xs_ref, xph_ref, tm + CONV_HALO)

        def chunk(c, carry):
            off = pl.multiple_of(c * CH, CH)
            for l0 in range(0, C, CONV_LANES):
                lanes = slice(l0, l0 + CONV_LANES)
                taps = _tap_values(xs_ref, xph_ref, CONV_X_OFFSETS, CH, off, lanes)
                acc = jnp.zeros((CH, CONV_LANES), F32) + cb_ref[:, lanes]
                for k in range(KW):
                    acc = acc + cw_ref[k:k + 1, lanes] * taps[CONV_X_OFFSETS[k]]
                conv_ref[pl.ds(off, CH), lanes] = acc
            return carry

        lax.fori_loop(0, tm // CH, chunk, 0)
        acc = conv_ref[...]
        mu = jnp.mean(acc, axis=-1, keepdims=True)
        xc = acc - mu
        var = jnp.mean(xc * xc, axis=-1, keepdims=True)
        y = (xc * lax.rsqrt(var + LN_EPS)) * lg_ref[...] + lb_ref[...]
        o_ref[...] = (y * _sigmoid(y)).astype(BF16)

    return pl.pallas_call(
        body, name=name, grid=(T // tm,),
        in_specs=[pl.BlockSpec((tm, C), lambda i: (i, 0)),
                  pl.BlockSpec((tm, C), lambda i: (i, 1)),
                  pl.BlockSpec((CONV_HALO, C), lambda i: (jnp.maximum(i * hb - 1, 0), 0)),
                  pl.BlockSpec((CONV_HALO, C), lambda i: (jnp.maximum(i * hb - 1, 0), 1)),
                  pl.BlockSpec((32, C), lambda i: (0, 0)),
                  pl.BlockSpec((1, C), lambda i: (0, 0)),
                  pl.BlockSpec((1, C), lambda i: (0, 0)),
                  pl.BlockSpec((1, C), lambda i: (0, 0))],
        out_specs=[pl.BlockSpec((tm, C), lambda i: (i, 0)), pl.BlockSpec((tm, C), lambda i: (i, 0))],
        out_shape=[jax.ShapeDtypeStruct((T, C), BF16), jax.ShapeDtypeStruct((T, C), F32)],
        scratch_shapes=[pltpu.VMEM((tm + CONV_HALO, C), F32), pltpu.VMEM((7, tm + CONV_HALO - 8, C), F32)],
        compiler_params=_params(1),
    )(z, z, z, z, cw, cb, lg, lb)


def conformer_conv_bwd(z, conv_out, dm, cw, lg, lb, name, tm=512):
    T = z.shape[0]
    C = CONV_A_CH
    tm = _tile(T, tm)
    hb = tm // CONV_HALO
    n_tiles = T // tm
    last_halo = T // CONV_HALO - 1
    R = tm + CONV_HALO
    KW = CONV_A_WIDTH
    CH = CONV_ROW_CHUNK

    def body(v_ref, gt_ref, pv_ref, pg_ref, cv_ref, ncv_ref, do_ref, ndo_ref, cw_ref, lg_ref, lb_ref,
             dz_ref, dcw_ref, dcb_ref, dlg_ref, dlb_ref, xs_ref, xph_ref, ds_ref, dph_ref):
        i = pl.program_id(0)

        @pl.when(i == 0)
        def _():
            dcw_ref[...] = jnp.zeros_like(dcw_ref)
            dcb_ref[...] = jnp.zeros_like(dcb_ref)
            dlg_ref[...] = jnp.zeros_like(dlg_ref)
            dlb_ref[...] = jnp.zeros_like(dlb_ref)

        prev = pv_ref[...] * _sigmoid(pg_ref[...])
        xs_ref[0:CONV_HALO, :] = jnp.where(i > 0, prev, 0.0)
        xs_ref[CONV_HALO:, :] = v_ref[...] * _sigmoid(gt_ref[...])
        _build_phases(xs_ref, xph_ref, tm + CONV_HALO)

        acc = jnp.concatenate([cv_ref[...], ncv_ref[...]], axis=0)
        mu = jnp.mean(acc, axis=-1, keepdims=True)
        xc = acc - mu
        rstd = lax.rsqrt(jnp.mean(xc * xc, axis=-1, keepdims=True) + LN_EPS)
        xh = xc * rstd
        y = xh * lg_ref[...] + lb_ref[...]
        s = _sigmoid(y)
        dout = jnp.concatenate([do_ref[...], jnp.where(i < n_tiles - 1, ndo_ref[...], 0.0)], axis=0)
        dy = dout * (s * (1.0 + y * (1.0 - s)))
        dxh = dy * lg_ref[...]
        dconv = rstd * (dxh - jnp.mean(dxh, axis=-1, keepdims=True) - xh * jnp.mean(dxh * xh, axis=-1, keepdims=True))
        ds_ref[...] = dconv
        dlg_ref[...] += jnp.sum(dy[:tm] * xh[:tm], axis=0, keepdims=True)
        dlb_ref[...] += jnp.sum(dy[:tm], axis=0, keepdims=True)
        dcb_ref[...] += jnp.sum(dconv[:tm], axis=0, keepdims=True)
        _build_phases(ds_ref, dph_ref, R)

        for l0 in range(0, C, CONV_LANES):
            lanes = slice(l0, l0 + CONV_LANES)

            def taps_bwd(c, wacc, l0=l0, lanes=lanes):
                off = pl.multiple_of(c * CH, CH)
                x_taps = _tap_values(xs_ref, xph_ref, CONV_X_OFFSETS, CH, off, lanes)
                d_taps = _tap_values(ds_ref, dph_ref, CONV_D_OFFSETS, CH, off, lanes)
                dc = ds_ref[pl.ds(off, CH), lanes]
                dglu = jnp.zeros((CH, CONV_LANES), F32)
                new = []
                for k in range(KW):
                    dglu = dglu + cw_ref[k:k + 1, lanes] * d_taps[CONV_D_OFFSETS[k]]
                    prod = dc * x_taps[CONV_X_OFFSETS[k]]
                    new.append(wacc[k] + ((prod[0:8] + prod[8:16]) + (prod[16:24] + prod[24:32])))
                val = v_ref[pl.ds(off, CH), lanes]
                sg = _sigmoid(gt_ref[pl.ds(off, CH), lanes])
                dz_ref[pl.ds(off, CH), lanes] = (dglu * sg).astype(BF16)
                dz_ref[pl.ds(off, CH), C + l0:C + l0 + CONV_LANES] = (dglu * val * sg * (1.0 - sg)).astype(BF16)
                return tuple(new)

            wacc = lax.fori_loop(0, tm // CH, taps_bwd, tuple(jnp.zeros((8, CONV_LANES), F32) for _ in range(KW)))
            for k in range(KW):
                dcw_ref[k:k + 1, lanes] += jnp.sum(wacc[k], axis=0, keepdims=True)

    prev_map = lambda i: jnp.maximum(i * hb - 1, 0)
    next_map = lambda i: jnp.minimum((i + 1) * hb, last_halo)
    return pl.pallas_call(
        body, name=name, grid=(n_tiles,),
        in_specs=[pl.BlockSpec((tm, C), lambda i: (i, 0)),
                  pl.BlockSpec((tm, C), lambda i: (i, 1)),
                  pl.BlockSpec((CONV_HALO, C), lambda i: (prev_map(i), 0)),
                  pl.BlockSpec((CONV_HALO, C), lambda i: (prev_map(i), 1)),
                  pl.BlockSpec((tm, C), lambda i: (i, 0)),
                  pl.BlockSpec((CONV_HALO, C), lambda i: (next_map(i), 0)),
                  pl.BlockSpec((tm, C), lambda i: (i, 0)),
                  pl.BlockSpec((CONV_HALO, C), lambda i: (next_map(i), 0)),
                  pl.BlockSpec((32, C), lambda i: (0, 0)),
                  pl.BlockSpec((1, C), lambda i: (0, 0)),
                  pl.BlockSpec((1, C), lambda i: (0, 0))],
        out_specs=[pl.BlockSpec((tm, 2 * C), lambda i: (i, 0)),
                   pl.BlockSpec((32, C), lambda i: (0, 0)),
                   pl.BlockSpec((1, C), lambda i: (0, 0)),
                   pl.BlockSpec((1, C), lambda i: (0, 0)),
                   pl.BlockSpec((1, C), lambda i: (0, 0))],
        out_shape=[jax.ShapeDtypeStruct((T, 2 * C), BF16),
                   jax.ShapeDtypeStruct((32, C), F32),
                   jax.ShapeDtypeStruct((1, C), F32),
                   jax.ShapeDtypeStruct((1, C), F32),
                   jax.ShapeDtypeStruct((1, C), F32)],
        scratch_shapes=[pltpu.VMEM((tm + CONV_HALO, C), F32), pltpu.VMEM((7, tm + CONV_HALO - 8, C), F32),
                        pltpu.VMEM((R, C), F32), pltpu.VMEM((7, R - 8, C), F32)],
        compiler_params=_params(1),
    )(z, z, z, z, conv_out, conv_out, dm, dm, cw, lg, lb)


def _swa_scores(q_h, kk_h, slope, bias_dist, valid, sink):
    s = _dot_nt(q_h, kk_h) * (HEAD_DIM ** -0.5) - slope * bias_dist
    s = jnp.where(valid, s, NEG_BIG)
    m = jnp.maximum(jnp.max(s, axis=-1, keepdims=True), sink)
    p = jnp.exp(s - m)
    e_sink = jnp.exp(sink - m)
    inv = 1.0 / (jnp.sum(p, axis=-1, keepdims=True) + e_sink)
    return p * inv, e_sink * inv


def _swa_mask(r0):
    qi = lax.broadcasted_iota(jnp.int32, (WINDOW, 2 * WINDOW), 0)
    kj = lax.broadcasted_iota(jnp.int32, (WINDOW, 2 * WINDOW), 1)
    dist = qi + WINDOW - kj
    valid = (dist >= 0) & (dist < WINDOW) & (r0 - WINDOW + kj >= 0)
    return dist.astype(F32), valid


def swa_fwd(z, kpad, vpad, sinks, name, tq=512):
    T = z.shape[0]
    tq = _tile(T, tq)
    HQ = SWA_HEADS * HEAD_DIM

    def body(sink_ref, q_ref, k_ref, v_ref, o_ref):
        i = pl.program_id(0)
        for sub in range(tq // WINDOW):
            r0 = pl.multiple_of(i * tq + sub * WINDOW, WINDOW)
            kk = k_ref[pl.ds(r0, 2 * WINDOW), :]
            vv = v_ref[pl.ds(r0, 2 * WINDOW), :]
            qb = q_ref[sub * WINDOW:(sub + 1) * WINDOW, :].astype(BF16)
            dist, valid = _swa_mask(r0)
            outs = []
            for h in range(SWA_HEADS):
                kh = h // SWA_GROUP
                ks = slice(kh * HEAD_DIM, (kh + 1) * HEAD_DIM)
                pn, _ = _swa_scores(qb[:, h * HEAD_DIM:(h + 1) * HEAD_DIM], kk[:, ks], 2.0 ** (-(h + 1)), dist, valid,
                                    sink_ref[h])
                outs.append(_dot(pn.astype(BF16), vv[:, ks]))
            o_ref[sub * WINDOW:(sub + 1) * WINDOW, :] = jnp.concatenate(outs, axis=-1).astype(BF16)

    return pl.pallas_call(
        body, name=name, grid=(T // tq,),
        in_specs=[pl.BlockSpec(memory_space=pltpu.SMEM),
                  pl.BlockSpec((tq, HQ), lambda i: (i, 2)),
                  pl.BlockSpec((T + WINDOW, 2 * HEAD_DIM), lambda i: (0, 0)),
                  pl.BlockSpec((T + WINDOW, 2 * HEAD_DIM), lambda i: (0, 0))],
        out_specs=pl.BlockSpec((tq, HQ), lambda i: (i, 0)),
        out_shape=jax.ShapeDtypeStruct((T, HQ), BF16),
        compiler_params=_params(1),
    )(sinks, z, kpad, vpad)


def swa_bwd(z, kpad, vpad, sinks, dm, name, tq=512):
    T = z.shape[0]
    tq = _tile(T, tq)
    HQ = SWA_HEADS * HEAD_DIM
    scale = HEAD_DIM ** -0.5

    def body(sink_ref, q_ref, k_ref, v_ref, do_ref, dq_ref, dk_ref, dv_ref, dsink_ref):
        i = pl.program_id(0)

        @pl.when(i == 0)
        def _():
            dk_ref[...] = jnp.zeros_like(dk_ref)
            dv_ref[...] = jnp.zeros_like(dv_ref)
            dsink_ref[...] = jnp.zeros_like(dsink_ref)

        for sub in range(tq // WINDOW):
            r0 = pl.multiple_of(i * tq + sub * WINDOW, WINDOW)
            kk = k_ref[pl.ds(r0, 2 * WINDOW), :]
            vv = v_ref[pl.ds(r0, 2 * WINDOW), :]
            rows = slice(sub * WINDOW, (sub + 1) * WINDOW)
            qb = q_ref[rows, :].astype(BF16)
            dob = do_ref[rows, :].astype(BF16)
            dist, valid = _swa_mask(r0)
            dqs, dks, dvs = [], [], []
            for kh in range(SWA_KV_HEADS):
                ks = slice(kh * HEAD_DIM, (kh + 1) * HEAD_DIM)
                dk_acc = jnp.zeros((2 * WINDOW, HEAD_DIM), F32)
                dv_acc = jnp.zeros((2 * WINDOW, HEAD_DIM), F32)
                for g in range(SWA_GROUP):
                    h = kh * SWA_GROUP + g
                    hs = slice(h * HEAD_DIM, (h + 1) * HEAD_DIM)
                    pn, p_sink = _swa_scores(qb[:, hs], kk[:, ks], 2.0 ** (-(h + 1)), dist, valid, sink_ref[h])
                    dp = _dot_nt(dob[:, hs], vv[:, ks])
                    delta = jnp.sum(pn * dp, axis=-1, keepdims=True)
                    ds = (pn * (dp - delta)).astype(BF16)
                    dqs.append(_dot(ds, kk[:, ks]) * scale)
                    dk_acc = dk_acc + _dot_tn(ds, qb[:, hs]) * scale
                    dv_acc = dv_acc + _dot_tn(pn.astype(BF16), dob[:, hs])
                    dsink_ref[h:h + 1, :] += jnp.zeros((1, 128), F32) - jnp.sum(p_sink * delta)
                dks.append(dk_acc)
                dvs.append(dv_acc)
            dq_ref[rows, :] = jnp.concatenate(dqs, axis=-1).astype(BF16)
            dk_ref[pl.ds(r0, 2 * WINDOW), :] += jnp.concatenate(dks, axis=-1)
            dv_ref[pl.ds(r0, 2 * WINDOW), :] += jnp.concatenate(dvs, axis=-1)

    kv_spec = pl.BlockSpec((T + WINDOW, 2 * HEAD_DIM), lambda i: (0, 0))
    return pl.pallas_call(
        body, name=name, grid=(T // tq,),
        in_specs=[pl.BlockSpec(memory_space=pltpu.SMEM),
                  pl.BlockSpec((tq, HQ), lambda i: (i, 2)),
                  kv_spec, kv_spec,
                  pl.BlockSpec((tq, HQ), lambda i: (i, 1))],
        out_specs=[pl.BlockSpec((tq, HQ), lambda i: (i, 0)),
                   kv_spec, kv_spec,
                   pl.BlockSpec((SWA_HEADS, 128), lambda i: (0, 0))],
        out_shape=[jax.ShapeDtypeStruct((T, HQ), BF16),
                   jax.ShapeDtypeStruct((T + WINDOW, 2 * HEAD_DIM), F32),
                   jax.ShapeDtypeStruct((T + WINDOW, 2 * HEAD_DIM), F32),
                   jax.ShapeDtypeStruct((SWA_HEADS, 128), F32)],
        compiler_params=_params(1),
    )(sinks, z, kpad, vpad, dm)


def short_conv_fwd(z, w, name, tm=1024):
    T = z.shape[0]
    C = SC_CH
    tm = _tile(T, tm)
    hb = tm // SC_HALO

    def body(b_ref, c_ref, v_ref, pc_ref, pv_ref, w_ref, o_ref, xs_ref):
        i = pl.program_id(0)
        xs_ref[0:SC_HALO, :] = jnp.where(i > 0, pc_ref[...] * pv_ref[...], 0.0)
        xs_ref[SC_HALO:, :] = c_ref[...] * v_ref[...]
        conv = jnp.zeros((tm, C), F32)
        for k in range(3):
            conv = conv + w_ref[k:k + 1, :] * xs_ref[pl.ds(SC_HALO - 2 + k, tm), :]
        o_ref[...] = (b_ref[...] * conv).astype(BF16)

    prev_map = lambda i: jnp.maximum(i * hb - 1, 0)
    return pl.pallas_call(
        body, name=name, grid=(T // tm,),
        in_specs=[pl.BlockSpec((tm, C), lambda i: (i, 0)),
                  pl.BlockSpec((tm, C), lambda i: (i, 1)),
                  pl.BlockSpec((tm, C), lambda i: (i, 2)),
                  pl.BlockSpec((SC_HALO, C), lambda i: (prev_map(i), 1)),
                  pl.BlockSpec((SC_HALO, C), lambda i: (prev_map(i), 2)),
                  pl.BlockSpec((8, C), lambda i: (0, 0))],
        out_specs=pl.BlockSpec((tm, C), lambda i: (i, 0)),
        out_shape=jax.ShapeDtypeStruct((T, C), BF16),
        scratch_shapes=[pltpu.VMEM((tm + SC_HALO, C), F32)],
        compiler_params=_params(1),
    )(z, z, z, z, z, w)


def short_conv_bwd(z, dm, w, name, tm=512):
    T = z.shape[0]
    C = SC_CH
    tm = _tile(T, tm)
    hb = tm // SC_HALO
    n_tiles = T // tm
    last_halo = T // SC_HALO - 1
    R = tm + SC_HALO

    def body(b_ref, c_ref, v_ref, pc_ref, pv_ref, nb_ref, do_ref, ndo_ref, w_ref, dz_ref, dw_ref, xs_ref, ds_ref):
        i = pl.program_id(0)

        @pl.when(i == 0)
        def _():
            dw_ref[...] = jnp.zeros_like(dw_ref)

        c = c_ref[...]
        val = v_ref[...]
        dout = do_ref[...]
        xs_ref[0:SC_HALO, :] = jnp.where(i > 0, pc_ref[...] * pv_ref[...], 0.0)
        xs_ref[SC_HALO:, :] = c * val
        dconv = dout * b_ref[...]
        ds_ref[0:tm, :] = dconv
        ds_ref[tm:, :] = jnp.where(i < n_tiles - 1, ndo_ref[...] * nb_ref[...], 0.0)
        conv = jnp.zeros((tm, C), F32)
        dcv = jnp.zeros((tm, C), F32)
        for k in range(3):
            xk = xs_ref[pl.ds(SC_HALO - 2 + k, tm), :]
            conv = conv + w_ref[k:k + 1, :] * xk
            dw_ref[k:k + 1, :] += jnp.sum(dconv * xk, axis=0, keepdims=True)
            dcv = dcv + w_ref[k:k + 1, :] * ds_ref[pl.ds(2 - k, tm), :]
        dz_ref[:, 0:C] = (dout * conv).astype(BF16)
        dz_ref[:, C:2 * C] = (dcv * val).astype(BF16)
        dz_ref[:, 2 * C:] = (dcv * c).astype(BF16)

    prev_map = lambda i: jnp.maximum(i * hb - 1, 0)
    next_map = lambda i: jnp.minimum((i + 1) * hb, last_halo)
    return pl.pallas_call(
        body, name=name, grid=(n_tiles,),
        in_specs=[pl.BlockSpec((tm, C), lambda i: (i, 0)),
                  pl.BlockSpec((tm, C), lambda i: (i, 1)),
                  pl.BlockSpec((tm, C), lambda i: (i, 2)),
                  pl.BlockSpec((SC_HALO, C), lambda i: (prev_map(i), 1)),
                  pl.BlockSpec((SC_HALO, C), lambda i: (prev_map(i), 2)),
                  pl.BlockSpec((SC_HALO, C), lambda i: (next_map(i), 0)),
                  pl.BlockSpec((tm, C), lambda i: (i, 0)),
                  pl.BlockSpec((SC_HALO, C), lambda i: (next_map(i), 0)),
                  pl.BlockSpec((8, C), lambda i: (0, 0))],
        out_specs=[pl.BlockSpec((tm, 3 * C), lambda i: (i, 0)),
                   pl.BlockSpec((8, C), lambda i: (0, 0))],
        out_shape=[jax.ShapeDtypeStruct((T, 3 * C), BF16), jax.ShapeDtypeStruct((8, C), F32)],
        scratch_shapes=[pltpu.VMEM((tm + SC_HALO, C), F32), pltpu.VMEM((R, C), F32)],
        compiler_params=_params(1),
    )(z, z, z, z, z, z, dm, dm, w)


def _xa_probs(q_h, k_h):
    s = _dot_nt(q_h, k_h) * (XA_HEAD_DIM ** -0.5)
    p = jnp.exp(s - jnp.max(s, axis=-1, keepdims=True))
    return p * (1.0 / jnp.sum(p, axis=-1, keepdims=True))


def xattn_fwd(q, kv, name, tm=2048):
    T = q.shape[0]
    M = kv.shape[0]
    tm = _tile(T, tm)

    def body(q_ref, k_ref, v_ref, o_ref):
        for h in range(XA_HEADS):
            hs = slice(h * XA_HEAD_DIM, (h + 1) * XA_HEAD_DIM)
            p = _xa_probs(q_ref[:, hs], k_ref[:, hs])
            o_ref[:, hs] = _dot(p.astype(BF16), v_ref[:, hs]).astype(BF16)

    return pl.pallas_call(
        body, name=name, grid=(T // tm,),
        in_specs=[pl.BlockSpec((tm, D_MODEL), lambda i: (i, 0)),
                  pl.BlockSpec((M, D_MODEL), lambda i: (0, 0)),
                  pl.BlockSpec((M, D_MODEL), lambda i: (0, 1))],
        out_specs=pl.BlockSpec((tm, D_MODEL), lambda i: (i, 0)),
        out_shape=jax.ShapeDtypeStruct((T, D_MODEL), BF16),
        compiler_params=_params(1),
    )(q, kv, kv)


def xattn_bwd(q, kv, do, name, tm=2048):
    T = q.shape[0]
    M = kv.shape[0]
    tm = _tile(T, tm)
    scale = XA_HEAD_DIM ** -0.5

    def body(q_ref, k_ref, v_ref, do_ref, dq_ref, dkv_ref):
        @pl.when(pl.program_id(0) == 0)
        def _():
            dkv_ref[...] = jnp.zeros_like(dkv_ref)

        for h in range(XA_HEADS):
            hs = slice(h * XA_HEAD_DIM, (h + 1) * XA_HEAD_DIM)
            vs = slice(D_MODEL + h * XA_HEAD_DIM, D_MODEL + (h + 1) * XA_HEAD_DIM)
            q_h = q_ref[:, hs]
            do_h = do_ref[:, hs]
            p = _xa_probs(q_h, k_ref[:, hs])
            dp = _dot_nt(do_h, v_ref[:, hs])
            ds = (p * (dp - jnp.sum(p * dp, axis=-1, keepdims=True))).astype(BF16)
            dq_ref[:, hs] = (_dot(ds, k_ref[:, hs]) * scale).astype(BF16)
            dkv_ref[:, hs] += _dot_tn(ds, q_h) * scale
            dkv_ref[:, vs] += _dot_tn(p.astype(BF16), do_h)

    return pl.pallas_call(
        body, name=name, grid=(T // tm,),
        in_specs=[pl.BlockSpec((tm, D_MODEL), lambda i: (i, 0)),
                  pl.BlockSpec((M, D_MODEL), lambda i: (0, 0)),
                  pl.BlockSpec((M, D_MODEL), lambda i: (0, 1)),
                  pl.BlockSpec((tm, D_MODEL), lambda i: (i, 0))],
        out_specs=[pl.BlockSpec((tm, D_MODEL), lambda i: (i, 0)),
                   pl.BlockSpec((M, 2 * D_MODEL), lambda i: (0, 0))],
        out_shape=[jax.ShapeDtypeStruct((T, D_MODEL), BF16), jax.ShapeDtypeStruct((M, 2 * D_MODEL), F32)],
        compiler_params=_params(1),
    )(q, kv, kv, do)


def final_loss(h, g, target, name, tm=1024):
    T, K = h.shape
    tm = _tile(T, tm)

    def body(h_ref, g_ref, t_ref, dh_ref, dg_ref, loss_ref):
        @pl.when(pl.program_id(0) == 0)
        def _():
            dg_ref[...] = jnp.zeros_like(dg_ref)
            loss_ref[...] = jnp.zeros_like(loss_ref)

        x = h_ref[...]
        r = lax.rsqrt(jnp.mean(x * x, axis=-1, keepdims=True) + RMS_EPS)
        xh = x * r
        e = xh * g_ref[...] - t_ref[...]
        loss_ref[...] += jnp.zeros((1, 128), F32) + 0.5 * jnp.sum(jnp.mean(e * e, axis=-1, keepdims=True))
        dy = e * (1.0 / K)
        dg_ref[...] += jnp.sum(dy * xh, axis=0, keepdims=True)
        dxh = dy * g_ref[...]
        dh_ref[...] = r * (dxh - xh * jnp.mean(dxh * xh, axis=-1, keepdims=True))

    return pl.pallas_call(
        body, name=name, grid=(T // tm,),
        in_specs=[pl.BlockSpec((tm, K), lambda i: (i, 0)),
                  pl.BlockSpec((1, K), lambda i: (0, 0)),
                  pl.BlockSpec((tm, K), lambda i: (i, 0))],
        out_specs=[pl.BlockSpec((tm, K), lambda i: (i, 0)),
                   pl.BlockSpec((1, K), lambda i: (0, 0)),
                   pl.BlockSpec((1, 128), lambda i: (0, 0))],
        out_shape=[jax.ShapeDtypeStruct((T, K), F32), jax.ShapeDtypeStruct((1, K), F32),
                   jax.ShapeDtypeStruct((1, 128), F32)],
        compiler_params=_params(1),
    )(h, g, target)


def _row(v):
    return v.reshape(1, -1)


def _pad_rows(a, rows):
    return jnp.pad(a, ((0, rows - a.shape[0]), (0, 0)))


def local_step(x, mem, target, P, get_weights, put_grads):
    cw = _pad_rows(P["conv_a_w"], 32)
    scw = _pad_rows(P["sc_conv_w"], 8)
    cb, lg, lb = _row(P["conv_a_b"]), _row(P["conv_a_ln_g"]), _row(P["conv_a_ln_b"])
    sinks = P["swa_sinks"]

    class _Layered:
        def __init__(self, store, name=None):
            self.store, self.name = store, name

        def __getitem__(self, key):
            if self.name is None:
                return self.store[(key, 0)] if key in ("even_w_in", "even_w_out", "odd_w_in", "odd_w_out") \
                    else _Layered(self.store, key)
            return self.store[(self.name, key)]

    store = {}
    W = _Layered(store)
    saved = []
    h = x
    u1 = None
    for i in range(2):
        L = f"l{i}"
        new, dep = get_weights("A" if i == 0 else "D", h)
        store.update(new)

        h, u2, s_ffn1 = ffn_forward(h, P["ffn1_norm"][i:i + 1], W["ffn1_w_gu"][i], W["ffn1_w_down"][i], L + "_ffn1",
                                    dep=dep, u=u1, g_next=P["mix_norm"][i:i + 1])
        h1 = h
        if i == 0:
            new, _ = get_weights("B", h)
            store.update(new)
            z = matmul(u2, W["even_w_in"], F32, L + "_mix_in", 768, n_tiles=2)
            kv = matmul(u2, W["even_w_in"], BF16, L + "_mix_kv", 256, n_tiles=1, first_tile=6)
            a, conv_out = conformer_conv_fwd(z, cw, cb, lg, lb, L + "_conv")
            kpad = jnp.pad(kv[:, :2 * HEAD_DIM], ((WINDOW, 0), (0, 0)))
            vpad = jnp.pad(kv[:, 2 * HEAD_DIM:], ((WINDOW, 0), (0, 0)))
            o = swa_fwd(z, kpad, vpad, sinks, L + "_swa")
            m = jnp.concatenate([a, o], axis=-1)
            h, u3 = matmul_residual(m, W["even_w_out"], h1, P["xa_norm"][i:i + 1], L + "_mix_out")
            s_mix = (h1, u2, z, m, kpad, vpad, conv_out)
        else:
            z = matmul(u2, W["odd_w_in"], F32, L + "_mix_in", 1024)
            m = short_conv_fwd(z, scw, L + "_sconv")
            h, u3 = matmul_residual(m, W["odd_w_out"], h1, P["xa_norm"][i:i + 1], L + "_mix_out")
            s_mix = (h1, u2, z, m)
        h2 = h
        kv, umem = norm_matmul(mem, P["xa_mem_norm"][i:i + 1], W["xa_wkv"][i], BF16, L + "_xa_kv", 2 * D_MODEL)
        q, u3 = norm_matmul(h2, P["xa_norm"][i:i + 1], W["xa_wq"][i], BF16, L + "_xa_q", D_MODEL, transposed=False, u=u3)
        o = xattn_fwd(q, kv, L + "_xa")
        h, u4 = matmul_residual(o, W["xa_wo"][i], h2, P["ffn2_norm"][i:i + 1], L + "_xa_out")
        s_xa = (h2, u3, q, o, kv, umem)
        new, _ = get_weights("C" if i == 0 else "E", h)
        store.update(new)
        h, u1, s_ffn2 = ffn_forward(h, P["ffn2_norm"][i:i + 1], W["ffn2_w_gu"][i], W["ffn2_w_down"][i], L + "_ffn2", u=u4,
                                    g_next=P["ffn1_norm"][1:2] if i == 0 else None)
        saved.append((s_ffn1, s_mix, s_xa, s_ffn2))

    dh, d_final, loss = final_loss(h, _row(P["final_norm"]), target, "final_loss")

    names = ("ffn1_w_gu", "ffn1_w_down", "ffn2_w_gu", "ffn2_w_down", "xa_wq", "xa_wkv", "xa_wo", "even_w_in", "even_w_out",
             "odd_w_in", "odd_w_out")
    dW = {k: [None, None] for k in names}
    dP = {k: [None, None] for k in ("ffn1_norm", "mix_norm", "xa_norm", "xa_mem_norm", "ffn2_norm")}
    dP["final_norm"] = d_final.reshape(-1)
    for i in (1, 0):
        L = f"l{i}b"
        s_ffn1, s_mix, s_xa, s_ffn2 = saved[i]

        def keep_ffn2(d_w_gu, d_w_down, i=i):
            dW["ffn2_w_gu"][i], dW["ffn2_w_down"][i] = d_w_gu, d_w_down

        def send_ffn1(d_w_gu, d_w_down, i=i):
            dW["ffn1_w_gu"][i], dW["ffn1_w_down"][i] = d_w_gu, d_w_down
            keys = STAGE_KEYS["D"] + STAGE_KEYS["E"] if i == 1 else STAGE_KEYS["A"]
            return put_grads("D" if i == 1 else "A", {k: dW[k[0]][k[1]] for k in keys})

        dh, dP["ffn2_norm"][i] = ffn_backward(
            dh, s_ffn2, P["ffn2_norm"][i:i + 1], W["ffn2_w_gu"][i], W["ffn2_w_down"][i], L + "_ffn2", emit=keep_ffn2)
        h2, u3, q, o, kv, umem = s_xa
        dW["xa_wo"][i] = matmul_tn(o, dh, L + "_xa_dwo")
        do = matmul_nt(dh, W["xa_wo"][i], BF16, L + "_xa_do")
        dq, dkv = xattn_bwd(q, kv, do, L + "_xa")
        dW["xa_wq"][i] = matmul_tn(u3, dq, L + "_xa_dwq")
        dW["xa_wkv"][i] = matmul_tn(dkv, umem, L + "_xa_dwkv", tk=1024)
        dkv_b = dkv.astype(BF16)
        _, dP["xa_mem_norm"][i] = matmul_norm_bwd(dkv_b, W["xa_wkv"][i], mem, P["xa_mem_norm"][i:i + 1],
                                                  jnp.zeros_like(mem), L + "_xa_dmem")
        dh, dP["xa_norm"][i] = matmul_norm_bwd(dq, W["xa_wq"][i], h2, P["xa_norm"][i:i + 1], dh, L + "_xa_dx",
                                               tm=1024, transposed=False)
        if i == 0:
            h1, u2, z, m, kpad, vpad, conv_out = s_mix
            dW["even_w_out"][0] = matmul_tn(m, dh, L + "_mix_dwo")
            dm = matmul_nt(dh, W["even_w_out"], F32, L + "_mix_dm")
            dz_conv, dcw, dcb, dlg, dlb = conformer_conv_bwd(z, conv_out, dm, cw, lg, lb, L + "_conv")
            dq_s, dkp, dvp, dsk = swa_bwd(z, kpad, vpad, sinks, dm, L + "_swa")
            dz = jnp.concatenate([dz_conv, dq_s, dkp[WINDOW:].astype(BF16), dvp[WINDOW:].astype(BF16)], axis=-1)
            dW["even_w_in"][0] = matmul_tn(dz, u2, L + "_mix_dwi", tk=896)
            dh, dP["mix_norm"][i] = matmul_norm_bwd(dz, W["even_w_in"], h1, P["mix_norm"][i:i + 1], dh, L + "_mix_dx",
                                                    tm=1024)
            dP["conv_a_w"] = dcw[:CONV_A_WIDTH]
            dP["conv_a_b"], dP["conv_a_ln_g"], dP["conv_a_ln_b"] = dcb.reshape(-1), dlg.reshape(-1), dlb.reshape(-1)
            dP["swa_sinks"] = dsk[:, 0]
        else:
            h1, u2, z, m = s_mix
            dW["odd_w_out"][0] = matmul_tn(m, dh, L + "_mix_dwo")
            dm = matmul_nt(dh, W["odd_w_out"], F32, L + "_mix_dm")
            dz, dscw = short_conv_bwd(z, dm, scw, L + "_sconv")
            dW["odd_w_in"][0] = matmul_tn(dz, u2, L + "_mix_dwi", tk=1024)
            dh, dP["mix_norm"][i] = matmul_norm_bwd(dz, W["odd_w_in"], h1, P["mix_norm"][i:i + 1], dh, L + "_mix_dx",
                                                    tm=1024)
            dP["sc_conv_w"] = dscw[:3]
        dep = put_grads("BC", {k: dW[k[0]][k[1]] for k in STAGE_KEYS["B"] + STAGE_KEYS["C"]}) if i == 0 else None
        dh, dP["ffn1_norm"][i] = ffn_backward(
            dh, s_ffn1, P["ffn1_norm"][i:i + 1], W["ffn1_w_gu"][i], W["ffn1_w_down"][i], L + "_ffn1", dep=dep,
            emit=send_ffn1)
    for k in ("ffn1_norm", "mix_norm", "xa_norm", "xa_mem_norm", "ffn2_norm"):
        dP[k] = jnp.concatenate(dP[k], axis=0)
    return loss, dh, dP


def _mesh_pos():
    return lax.axis_index("x"), lax.axis_index("y"), lax.axis_index("c")


def _flat_index(px, py, pc):
    return 4 * px + 2 * py + pc


def all_gather(blob, name, dep=None):
    R, C = blob.shape

    def kern(x_ref, out_ref, send_sems, recv_sems, local_sem):
        x, y, c = _mesh_pos()
        me, sibling = (x, y, c), (x, y, 1 - c)
        chips = [(1 - x, y), (x, 1 - y), (1 - x, 1 - y)]

        def slot(px, py, pc):
            return out_ref.at[_flat_index(px, py, pc)]

        def copy(k, block, to, src=None):
            return pltpu.make_async_remote_copy(
                src_ref=slot(*block) if src is None else src, dst_ref=slot(*block),
                send_sem=send_sems.at[k], recv_sem=recv_sems.at[k],
                device_id=to, device_id_type=pl.DeviceIdType.MESH)

        mine = pltpu.make_async_copy(x_ref, slot(*me), local_sem)
        mine.start()
        first = [copy(0, me, sibling, src=x_ref)]
        first += [copy(1 + j, me, (*chip, c), src=x_ref) for j, chip in enumerate(chips)]
        for cp in first:
            cp.start()
        passed = [copy(4 + j, (*chip, c), sibling) for j, chip in enumerate(chips)]
        for j, chip in enumerate(chips):
            copy(1 + j, (*chip, c), me).wait_recv()
            passed[j].start()
        copy(0, sibling, me).wait_recv()
        for j, chip in enumerate(chips):
            copy(4 + j, (*chip, 1 - c), me).wait_recv()
        for cp in first + passed:
            cp.wait_send()
        mine.wait()

    body, dep_spec, dep_arg = _with_dep(kern, 1, dep)
    return pl.pallas_call(
        body, name=name,
        out_shape=jax.ShapeDtypeStruct((N_DEV, R, C), blob.dtype),
        in_specs=[ANY_SPEC] + dep_spec,
        out_specs=ANY_SPEC,
        scratch_shapes=[pltpu.SemaphoreType.DMA((7,)), pltpu.SemaphoreType.DMA((7,)), pltpu.SemaphoreType.DMA],
    )(blob, *dep_arg)


HBM_SPEC = pl.BlockSpec(memory_space=pltpu.HBM)
SEM_SPEC = pl.BlockSpec(memory_space=pltpu.SEMAPHORE)
DATAFLOW_EFFECT = pltpu.SideEffectType.DATAFLOW_SIDE_EFFECTING


def _peers(x, y, c):
    out = []
    for k in range(1, N_DEV):
        pos = (1 - x if k & 4 else x, 1 - y if k & 2 else y, 1 - c if k & 1 else c)
        out.append((_flat_index(*pos), pos))
    return out


def _exchange_copy(src_ref, land_ref, send_sems, recv_sems, j, me, peer_idx, peer, scatter):
    return pltpu.make_async_remote_copy(
        src_ref=src_ref.at[peer_idx] if scatter else src_ref, dst_ref=land_ref.at[me],
        send_sem=send_sems.at[j], recv_sem=recv_sems.at[j], device_id=peer, device_id_type=pl.DeviceIdType.MESH)


def exchange_start(srcs, lands, scatter, after, name):
    n = len(srcs)
    n_after = len(after)

    def body(*refs):
        src_refs, land_refs = refs[:n], refs[n:2 * n]
        outs = refs[2 * n + n_after:]
        send_sems, recv_sems, token = outs[:n], outs[n:2 * n], outs[4 * n]
        x, y, c = _mesh_pos()
        me = _flat_index(x, y, c)
        for g in range(n):
            for j, (peer_idx, peer) in enumerate(_peers(x, y, c)):
                _exchange_copy(src_refs[g], land_refs[g], send_sems[g], recv_sems[g], j, me, peer_idx, peer, scatter).start()
        token[...] = jnp.zeros_like(token)

    hbm = lambda a: pltpu.with_memory_space_constraint(a, pltpu.HBM)
    res = pl.pallas_call(
        body, name=name,
        out_shape=(*[pltpu.SemaphoreType.DMA((N_DEV - 1,))] * (2 * n),
                   *[pltpu.HBM(a.shape, a.dtype) for a in srcs], *[pltpu.HBM(a.shape, a.dtype) for a in lands],
                   jax.ShapeDtypeStruct((8, 128), F32)),
        in_specs=[HBM_SPEC] * (2 * n) + [ANY_SPEC] * n_after,
        out_specs=(*[SEM_SPEC] * (2 * n), *[HBM_SPEC] * (2 * n), pl.BlockSpec(memory_space=pltpu.VMEM)),
        input_output_aliases={i: 2 * n + i for i in range(2 * n)},
        compiler_params=pltpu.CompilerParams(has_side_effects=DATAFLOW_EFFECT),
    )(*[hbm(a) for a in srcs], *[hbm(a) for a in lands], *after)
    handles = [(res[g], res[n + g], res[2 * n + g], res[3 * n + g]) for g in range(n)]
    return handles, res[4 * n]


def exchange_wait(handles, scatter, after, name):
    n = len(handles)

    def body(*refs):
        src_refs, land_refs = refs[:n], refs[n:2 * n]
        send_sems, recv_sems = refs[2 * n:3 * n], refs[3 * n:4 * n]
        x, y, c = _mesh_pos()
        me = _flat_index(x, y, c)
        for g in range(n):
            for j, (peer_idx, peer) in enumerate(_peers(x, y, c)):
                mine = _exchange_copy(src_refs[g], land_refs[g], send_sems[g], recv_sems[g], j, me, peer_idx, peer,
                                      scatter)
                mine.wait_send()
                theirs = pltpu.make_async_remote_copy(
                    src_ref=src_refs[g].at[me] if scatter else src_refs[g], dst_ref=land_refs[g].at[peer_idx],
                    send_sem=send_sems[g].at[j], recv_sem=recv_sems[g].at[j], device_id=peer,
                    device_id_type=pl.DeviceIdType.MESH)
                theirs.wait_recv()

    srcs = [h[2] for h in handles]
    lands = [h[3] for h in handles]
    res = pl.pallas_call(
        body, name=name,
        out_shape=tuple(pltpu.HBM(a.shape, a.dtype) for a in srcs + lands),
        in_specs=[HBM_SPEC] * (2 * n) + [SEM_SPEC] * (2 * n) + [ANY_SPEC],
        out_specs=tuple([HBM_SPEC] * (2 * n)),
        input_output_aliases={i: i for i in range(2 * n)},
        compiler_params=pltpu.CompilerParams(has_side_effects=DATAFLOW_EFFECT),
    )(*srcs, *lands, *[h[0] for h in handles], *[h[1] for h in handles], after)
    return [(res[g], res[n + g]) for g in range(n)]


def ordered_sum(parts, name, tr=512):
    n, R, C = parts.shape
    tr = next((t for t in range(min(tr, R), 15, -16) if R % t == 0), R)

    def body(p_ref, o_ref):
        acc = p_ref[0].astype(F32)
        for j in range(1, n):
            acc = acc + p_ref[j].astype(F32)
        o_ref[...] = acc

    return pl.pallas_call(
        body, name=name, grid=(R // tr,),
        in_specs=[pl.BlockSpec((n, tr, C), lambda i: (0, i, 0))],
        out_specs=pl.BlockSpec((tr, C), lambda i: (i, 0)),
        out_shape=jax.ShapeDtypeStruct((R, C), F32),
        compiler_params=_params(1),
    )(parts)


def adamw(w, g, m, v, name, tr=256):
    R, C = w.shape
    tr = next((t for t in range(tr, 7, -8) if R % t == 0), R)
    c1 = 1.0 - ADAM_B1 ** ADAM_STEP
    c2 = 1.0 - ADAM_B2 ** ADAM_STEP

    def body(w_ref, g_ref, m_ref, v_ref, d_ref, mo_ref, vo_ref):
        grad = g_ref[...]
        m2 = ADAM_B1 * m_ref[...] + (1.0 - ADAM_B1) * grad
        v2 = ADAM_B2 * v_ref[...] + (1.0 - ADAM_B2) * (grad * grad)
        mo_ref[...] = m2
        vo_ref[...] = v2
        d_ref[...] = -ADAM_LR * ((m2 / c1) / (jnp.sqrt(v2 / c2) + ADAM_EPS) + ADAM_WD * w_ref[...])

    spec = pl.BlockSpec((tr, C), lambda i: (i, 0))
    return pl.pallas_call(
        body, name=name, grid=(R // tr,),
        in_specs=[spec] * 4, out_specs=[spec] * 3,
        out_shape=[jax.ShapeDtypeStruct((R, C), F32)] * 3,
        compiler_params=_params(1),
    )(w, g, m, v)


WEIGHT_NAMES = ("ffn1_norm", "ffn1_w_gu", "ffn1_w_down", "mix_norm", "even_w_in", "conv_a_w", "conv_a_b", "conv_a_ln_g",
                "conv_a_ln_b", "swa_sinks", "even_w_out", "odd_w_in", "sc_conv_w", "odd_w_out", "xa_norm", "xa_mem_norm",
                "xa_wq", "xa_wkv", "xa_wo", "ffn2_norm", "ffn2_w_gu", "ffn2_w_down", "final_norm")
BLOB_COLS = 1024
SMALL_ROWS = (("ffn1_norm", 0, 2), ("mix_norm", 2, 2), ("xa_norm", 4, 2), ("xa_mem_norm", 6, 2), ("ffn2_norm", 8, 2),
              ("final_norm", 10, 1))
ROW_CONV_B_LNG = 11
ROW_LNB_SINKS_LOSS = 12
LOSS_COL = 512 + SWA_HEADS
ROW_SC_CONV = 13
ROW_CONV_W = 16
SMALL_BLOB_ROWS = 32
SMALL_ADAM_ROWS = 16


def _small_blob(v):
    rows = [v[n].reshape(-1, D_MODEL) for n, _, _ in SMALL_ROWS]
    rows.append(jnp.concatenate([v["conv_a_b"].reshape(-1), v["conv_a_ln_g"].reshape(-1)]).reshape(1, D_MODEL))
    tail = jnp.zeros((D_MODEL - 512 - SWA_HEADS,), F32)
    if "loss" in v:
        tail = tail.at[0].set(v["loss"])
    rows.append(jnp.concatenate([v["conv_a_ln_b"].reshape(-1), v["swa_sinks"].reshape(-1), tail]).reshape(1, D_MODEL))
    rows.append(jnp.zeros((SMALL_ADAM_ROWS - ROW_SC_CONV, D_MODEL), F32))
    return jnp.concatenate(rows, axis=0)


def _small_unblob(b, shapes):
    out = {n: b[r:r + k].reshape(shapes[n]) for n, r, k in SMALL_ROWS}
    out["conv_a_b"] = b[ROW_CONV_B_LNG, :512].reshape(shapes["conv_a_b"])
    out["conv_a_ln_g"] = b[ROW_CONV_B_LNG, 512:].reshape(shapes["conv_a_ln_g"])
    out["conv_a_ln_b"] = b[ROW_LNB_SINKS_LOSS, :512].reshape(shapes["conv_a_ln_b"])
    out["swa_sinks"] = b[ROW_LNB_SINKS_LOSS, 512:512 + SWA_HEADS].reshape(shapes["swa_sinks"])
    return out


def kernel(x, mem, ffn1_norm, ffn1_w_gu, ffn1_w_down, mix_norm, even_w_in, conv_a_w, conv_a_b, conv_a_ln_g, conv_a_ln_b, swa_sinks, even_w_out, odd_w_in, sc_conv_w, odd_w_out, xa_norm, xa_mem_norm, xa_wq, xa_wkv, xa_wo, ffn2_norm, ffn2_w_gu, ffn2_w_down, final_norm, loss_target, m_ffn1_norm, m_ffn1_w_gu, m_ffn1_w_down, m_mix_norm, m_even_w_in, m_conv_a_w, m_conv_a_b, m_conv_a_ln_g, m_conv_a_ln_b, m_swa_sinks, m_even_w_out, m_odd_w_in, m_sc_conv_w, m_odd_w_out, m_xa_norm, m_xa_mem_norm, m_xa_wq, m_xa_wkv, m_xa_wo, m_ffn2_norm, m_ffn2_w_gu, m_ffn2_w_down, m_final_norm, v_ffn1_norm, v_ffn1_w_gu, v_ffn1_w_down, v_mix_norm, v_even_w_in, v_conv_a_w, v_conv_a_b, v_conv_a_ln_g, v_conv_a_ln_b, v_swa_sinks, v_even_w_out, v_odd_w_in, v_sc_conv_w, v_odd_w_out, v_xa_norm, v_xa_mem_norm, v_xa_wq, v_xa_wkv, v_xa_wo, v_ffn2_norm, v_ffn2_w_gu, v_ffn2_w_down, v_final_norm):
    w = dict(ffn1_norm=ffn1_norm, ffn1_w_gu=ffn1_w_gu, ffn1_w_down=ffn1_w_down, mix_norm=mix_norm, even_w_in=even_w_in,
             conv_a_w=conv_a_w, conv_a_b=conv_a_b, conv_a_ln_g=conv_a_ln_g, conv_a_ln_b=conv_a_ln_b, swa_sinks=swa_sinks,
             even_w_out=even_w_out, odd_w_in=odd_w_in, sc_conv_w=sc_conv_w, odd_w_out=odd_w_out, xa_norm=xa_norm,
             xa_mem_norm=xa_mem_norm, xa_wq=xa_wq, xa_wkv=xa_wkv, xa_wo=xa_wo, ffn2_norm=ffn2_norm, ffn2_w_gu=ffn2_w_gu,
             ffn2_w_down=ffn2_w_down, final_norm=final_norm)
    m = dict(ffn1_norm=m_ffn1_norm, ffn1_w_gu=m_ffn1_w_gu, ffn1_w_down=m_ffn1_w_down, mix_norm=m_mix_norm,
             even_w_in=m_even_w_in, conv_a_w=m_conv_a_w, conv_a_b=m_conv_a_b, conv_a_ln_g=m_conv_a_ln_g,
             conv_a_ln_b=m_conv_a_ln_b, swa_sinks=m_swa_sinks, even_w_out=m_even_w_out, odd_w_in=m_odd_w_in,
             sc_conv_w=m_sc_conv_w, odd_w_out=m_odd_w_out, xa_norm=m_xa_norm, xa_mem_norm=m_xa_mem_norm, xa_wq=m_xa_wq,
             xa_wkv=m_xa_wkv, xa_wo=m_xa_wo, ffn2_norm=m_ffn2_norm, ffn2_w_gu=m_ffn2_w_gu, ffn2_w_down=m_ffn2_w_down,
             final_norm=m_final_norm)
    v = dict(ffn1_norm=v_ffn1_norm, ffn1_w_gu=v_ffn1_w_gu, ffn1_w_down=v_ffn1_w_down, mix_norm=v_mix_norm,
             even_w_in=v_even_w_in, conv_a_w=v_conv_a_w, conv_a_b=v_conv_a_b, conv_a_ln_g=v_conv_a_ln_g,
             conv_a_ln_b=v_conv_a_ln_b, swa_sinks=v_swa_sinks, even_w_out=v_even_w_out, odd_w_in=v_odd_w_in,
             sc_conv_w=v_sc_conv_w, odd_w_out=v_odd_w_out, xa_norm=v_xa_norm, xa_mem_norm=v_xa_mem_norm, xa_wq=v_xa_wq,
             xa_wkv=v_xa_wkv, xa_wo=v_xa_wo, ffn2_norm=v_ffn2_norm, ffn2_w_gu=v_ffn2_w_gu, ffn2_w_down=v_ffn2_w_down,
             final_norm=v_final_norm)
    me = _flat_index(*_mesh_pos())

    conv_blob = jnp.concatenate([w["conv_a_w"].reshape(-1), w["sc_conv_w"].reshape(-1),
                                 jnp.zeros((8 * 1024 - 31 * 64 - 3 * 128,), F32)]).reshape(8, 1024)
    conv_all = all_gather(conv_blob, "gather_conv_weights").reshape(N_DEV, 8 * 1024)
    conv_a_full = jnp.transpose(conv_all[:, :31 * 64].reshape(N_DEV, 31, 64), (1, 0, 2)).reshape(31, 512)
    sc_full = jnp.transpose(conv_all[:, 31 * 64:31 * 64 + 3 * 128].reshape(N_DEV, 3, 128), (1, 0, 2)).reshape(3, 1024)

    def shard_rows(n, l):
        return (w[n][l].T if SPLIT_AXIS[n] == 1 else w[n][l]).astype(BF16)

    def with_own(land, own):
        return lax.dynamic_update_slice(land, own[None], (me, 0, 0))

    a_keys = STAGE_KEYS["A"]
    gathered_a = all_gather(jnp.concatenate([shard_rows(n, l) for n, l in a_keys], axis=0), "gather_weights_a")
    later = ("B", "C", "D", "E")
    later_keys = [k for s in later for k in STAGE_KEYS[s]]
    shards = [shard_rows(n, l) for n, l in later_keys]
    lands = [lax.empty((N_DEV,) + s.shape, BF16) for s in shards]
    handles, weight_token = exchange_start(shards, lands, False, [gathered_a, conv_all], "gather_start")
    weight_handles = dict(zip(later_keys, handles))

    def get_weights(stage, after):
        keys = STAGE_KEYS[stage]
        if stage == "A":
            out, off = {}, 0
            for n, l in keys:
                rows = w[n].shape[2] if SPLIT_AXIS[n] == 1 else w[n].shape[1]
                out[(n, l)] = gathered_a[:, off:off + rows, :].reshape(N_DEV * rows, BLOB_COLS)
                off += rows
            return out, weight_token
        got = exchange_wait([weight_handles[k] for k in keys], False, after, "gather_wait_" + stage.lower())
        return {k: with_own(land, own).reshape(-1, BLOB_COLS) for k, (own, land) in zip(keys, got)}, None

    grad_handles = {}

    def put_grads(stage, dws):
        srcs = [dw.reshape(N_DEV, -1, BLOB_COLS) for dw in dws.values()]
        handles, token = exchange_start(srcs, [lax.empty(s.shape, BF16) for s in srcs], True, [],
                                        "scatter_start_" + stage.lower())
        grad_handles[stage] = (handles, tuple(dws))
        return token

    P = dict(ffn1_norm=ffn1_norm, mix_norm=mix_norm, xa_norm=xa_norm, xa_mem_norm=xa_mem_norm, ffn2_norm=ffn2_norm,
             final_norm=final_norm, conv_a_w=conv_a_full, conv_a_b=conv_a_b[0], conv_a_ln_g=conv_a_ln_g[0],
             conv_a_ln_b=conv_a_ln_b[0], swa_sinks=swa_sinks[0], sc_conv_w=sc_full)

    loss_part, grad_x, dP = local_step(x[0], mem[0], loss_target[0], P, get_weights, put_grads)

    def finish_grads(stage, after):
        handles, keys = grad_handles[stage]
        got = exchange_wait(handles, True, after, "scatter_wait_" + stage.lower())
        out = {}
        for (n, l), (src, land) in zip(keys, got):
            own = lax.dynamic_slice(src, (me, 0, 0), (1,) + src.shape[1:])[0]
            part = ordered_sum(with_own(land, own), f"sum_grads_{n}_{l}")
            out[(n, l)] = part.T if SPLIT_AXIS[n] == 1 else part
        return out

    layer_grads = {**finish_grads("D", grad_x), **finish_grads("BC", grad_x)}

    grads, delta, new_m, new_v = {}, {}, {}, {}

    def update(n):
        shp = w[n].shape
        two_d = (shp[0] * shp[1], shp[2])
        d_, m_, v_ = adamw(w[n].reshape(two_d), grads[n].reshape(two_d), m[n].reshape(two_d), v[n].reshape(two_d),
                           "adamw_" + n)
        delta[n], new_m[n], new_v[n] = d_.reshape(shp), m_.reshape(shp), v_.reshape(shp)

    first_stage = tuple(n for n, _ in STAGE_KEYS["A"])
    for n in SPLIT_AXIS:
        if n not in first_stage:
            grads[n] = jnp.stack([layer_grads[(n, l)] for l in range(w[n].shape[0])], axis=0)
            update(n)
    layer_grads.update(finish_grads("A", delta["ffn2_w_gu"]))

    dP = dict(dP, loss=loss_part[0, 0])
    small = jnp.concatenate([
        _small_blob(dP)[:ROW_SC_CONV], dP["sc_conv_w"],
        jnp.concatenate([dP["conv_a_w"].reshape(-1), jnp.zeros((512,), F32)]).reshape(16, D_MODEL)], axis=0)
    small_all = all_gather(small, "gather_small_grads", dep=layer_grads[STAGE_KEYS["A"][-1]])
    small_sum = ordered_sum(small_all, "sum_small_grads", tr=SMALL_BLOB_ROWS)
    loss = small_sum[ROW_LNB_SINKS_LOSS, LOSS_COL]
    grads.update(_small_unblob(small_sum, {n: w[n].shape for n in WEIGHT_NAMES}))
    sc_g = small_sum[ROW_SC_CONV:ROW_SC_CONV + 3]
    grads["sc_conv_w"] = lax.dynamic_slice(sc_g, (0, me * 128), (3, 128)).reshape(w["sc_conv_w"].shape)
    cw_g = small_sum[ROW_CONV_W:].reshape(-1)[:31 * 512].reshape(31, 512)
    grads["conv_a_w"] = lax.dynamic_slice(cw_g, (0, me * 64), (31, 64)).reshape(w["conv_a_w"].shape)

    update("conv_a_w")
    update("sc_conv_w")
    for n in first_stage:
        grads[n] = jnp.stack([layer_grads[(n, l)] for l in range(w[n].shape[0])], axis=0)
        update(n)
    d_, m_, v_ = adamw(_small_blob(w), small_sum[:SMALL_ADAM_ROWS], _small_blob(m), _small_blob(v), "adamw_small",
                       tr=SMALL_ADAM_ROWS)
    shapes = {n: w[n].shape for n in WEIGHT_NAMES}
    delta.update(_small_unblob(d_, shapes))
    new_m.update(_small_unblob(m_, shapes))
    new_v.update(_small_unblob(v_, shapes))

    return (loss, grad_x[None], *[grads[n] for n in WEIGHT_NAMES], *[delta[n] for n in WEIGHT_NAMES],
            *[new_m[n] for n in WEIGHT_NAMES], *[new_v[n] for n in WEIGHT_NAMES])
```

```python
import functools

import jax
import jax.numpy as jnp
from jax import lax
from jax.experimental import pallas as pl
from jax.experimental.pallas import tpu as pltpu

F32 = jnp.float32
BF16 = jnp.bfloat16

D_MODEL = 1024
D_FF = 2816
CONV_A_CH = 512
CONV_A_WIDTH = 31
SWA_HEADS = 8
SWA_KV_HEADS = 2
SWA_GROUP = SWA_HEADS // SWA_KV_HEADS
HEAD_DIM = 64
WINDOW = 128
SC_CH = 1024
XA_HEADS = 4
XA_HEAD_DIM = D_MODEL // XA_HEADS
RMS_EPS = 1e-6
LN_EPS = 1e-5
ADAM_LR = 0.001
ADAM_B1 = 0.9
ADAM_B2 = 0.999
ADAM_EPS = 1e-08
ADAM_WD = 0.01
ADAM_STEP = 10
N_DEV = 8

V7X_VMEM_BYTES = 64 * 1024 * 1024
VMEM_LIMIT = V7X_VMEM_BYTES - 8 * 1024 * 1024
CONV_HALO = 32
SC_HALO = 8
NEG_BIG = -1e30

SPLIT_AXIS = dict(ffn1_w_gu=1, ffn1_w_down=0, even_w_in=1, even_w_out=0, odd_w_in=1, odd_w_out=0, xa_wq=0, xa_wkv=1, xa_wo=0,
                  ffn2_w_gu=1, ffn2_w_down=0)
STAGE_KEYS = dict(
    A=(("ffn1_w_gu", 0), ("ffn1_w_down", 0)),
    B=(("even_w_in", 0), ("even_w_out", 0), ("xa_wq", 0), ("xa_wkv", 0), ("xa_wo", 0)),
    C=(("ffn2_w_gu", 0), ("ffn2_w_down", 0)),
    D=(("ffn1_w_gu", 1), ("ffn1_w_down", 1), ("odd_w_in", 0), ("odd_w_out", 0), ("xa_wq", 1), ("xa_wkv", 1), ("xa_wo", 1)),
    E=(("ffn2_w_gu", 1), ("ffn2_w_down", 1)))


def _params(n_axes):
    return pltpu.CompilerParams(dimension_semantics=("arbitrary",) * n_axes, vmem_limit_bytes=VMEM_LIMIT)


def _tile(n, pref):
    t = min(n, pref)
    assert n % t == 0, (n, pref)
    return t


def _dot(a, b):
    return jnp.dot(a, b, preferred_element_type=F32)


def _dot_nt(a, b):
    return lax.dot_general(a, b, (((1,), (1,)), ((), ())), preferred_element_type=F32)


def _dot_tn(a, b):
    return lax.dot_general(a, b, (((0,), (0,)), ((), ())), preferred_element_type=F32)


def _sigmoid(x):
    return 0.5 * jnp.tanh(0.5 * x) + 0.5


ANY_SPEC = pl.BlockSpec(memory_space=pl.ANY)


def _with_dep(body, n_in, dep):
    if dep is None:
        return body, [], []
    return (lambda *refs: body(*refs[:n_in], *refs[n_in + 1:])), [ANY_SPEC], [dep]


def rmsnorm(h, g, name, tm=1024, dep=None):
    T, K = h.shape
    tm = _tile(T, tm)

    def kern(h_ref, g_ref, u_ref):
        x = h_ref[...]
        r = lax.rsqrt(jnp.mean(x * x, axis=-1, keepdims=True) + RMS_EPS)
        u_ref[...] = ((x * r) * g_ref[...]).astype(BF16)

    body, dep_spec, dep_arg = _with_dep(kern, 2, dep)
    return pl.pallas_call(
        body, name=name, grid=(T // tm,),
        in_specs=[pl.BlockSpec((tm, K), lambda i: (i, 0)), pl.BlockSpec((1, K), lambda i: (0, 0))] + dep_spec,
        out_specs=pl.BlockSpec((tm, K), lambda i: (i, 0)),
        out_shape=jax.ShapeDtypeStruct((T, K), BF16),
        compiler_params=_params(1),
    )(h, g, *dep_arg)


def matmul(a, w, out_dtype, name, tn, tm=2048, transposed=True, n_tiles=None, first_tile=0, dep=None):
    T, K = a.shape
    N = w.shape[0] if transposed else w.shape[1]
    n_tiles = N // tn if n_tiles is None else n_tiles
    tm = _tile(T, tm)
    mm = _dot_nt if transposed else _dot

    def kern(a_ref, w_ref, z_ref):
        z_ref[...] = mm(a_ref[...], w_ref[...]).astype(z_ref.dtype)

    body, dep_spec, dep_arg = _with_dep(kern, 2, dep)
    w_spec = (pl.BlockSpec((tn, K), lambda i, j: (first_tile + j, 0)) if transposed
              else pl.BlockSpec((K, tn), lambda i, j: (0, first_tile + j)))
    return pl.pallas_call(
        body, name=name, grid=(T // tm, n_tiles),
        in_specs=[pl.BlockSpec((tm, K), lambda i, j: (i, 0)), w_spec] + dep_spec,
        out_specs=pl.BlockSpec((tm, tn), lambda i, j: (i, j)),
        out_shape=jax.ShapeDtypeStruct((T, n_tiles * tn), out_dtype),
        compiler_params=_params(2),
    )(a, w, *dep_arg)


def norm_matmul(h, g, w, out_dtype, name, tn, dep=None, transposed=True, u=None, tm=2048):
    if u is None:
        u, dep = rmsnorm(h, g, name + "_norm", dep=dep), None
    return matmul(u, w, out_dtype, name, tn, tm=tm, transposed=transposed, dep=dep), u


def matmul_residual(a, w, res, g_next, name, tm=1024):
    T, K = a.shape
    N = w.shape[1]
    tm = _tile(T, tm)

    def body(a_ref, w_ref, r_ref, g_ref, o_ref, u_ref):
        x = r_ref[...] + _dot(a_ref[...], w_ref[...])
        o_ref[...] = x
        r = lax.rsqrt(jnp.mean(x * x, axis=-1, keepdims=True) + RMS_EPS)
        u_ref[...] = ((x * r) * g_ref[...]).astype(BF16)

    return pl.pallas_call(
        body, name=name, grid=(T // tm,),
        in_specs=[pl.BlockSpec((tm, K), lambda i: (i, 0)),
                  pl.BlockSpec((K, N), lambda i: (0, 0)),
                  pl.BlockSpec((tm, N), lambda i: (i, 0)),
                  pl.BlockSpec((1, N), lambda i: (0, 0))],
        out_specs=[pl.BlockSpec((tm, N), lambda i: (i, 0)), pl.BlockSpec((tm, N), lambda i: (i, 0))],
        out_shape=[jax.ShapeDtypeStruct((T, N), F32), jax.ShapeDtypeStruct((T, N), BF16)],
        compiler_params=_params(1),
    )(a, w, res, g_next)


def matmul_nt(dy, w, out_dtype, name, tm=2048):
    T, N = dy.shape
    K = w.shape[0]
    tm = _tile(T, tm)

    def body(dy_ref, w_ref, o_ref):
        o_ref[...] = _dot_nt(dy_ref[...].astype(BF16), w_ref[...]).astype(o_ref.dtype)

    return pl.pallas_call(
        body, name=name, grid=(T // tm,),
        in_specs=[pl.BlockSpec((tm, N), lambda i: (i, 0)),
                  pl.BlockSpec((K, N), lambda i: (0, 0))],
        out_specs=pl.BlockSpec((tm, K), lambda i: (i, 0)),
        out_shape=jax.ShapeDtypeStruct((T, K), out_dtype),
        compiler_params=_params(1),
    )(dy, w)


def matmul_norm_bwd(dz, w, h, g, dh_in, name, tm=512, transposed=True, dep=None):
    T, N = dz.shape
    K = h.shape[1]
    tm = _tile(T, tm)

    def kern(dz_ref, w_ref, h_ref, g_ref, dhin_ref, dh_ref, dg_ref):
        @pl.when(pl.program_id(0) == 0)
        def _():
            dg_ref[...] = jnp.zeros_like(dg_ref)

        mm = _dot if transposed else _dot_nt
        du = mm(dz_ref[...], w_ref[...])
        x = h_ref[...]
        r = lax.rsqrt(jnp.mean(x * x, axis=-1, keepdims=True) + RMS_EPS)
        xh = x * r
        dg_ref[...] += jnp.sum(du * xh, axis=0, keepdims=True)
        dxh = du * g_ref[...]
        dh_ref[...] = dhin_ref[...] + r * (dxh - xh * jnp.mean(dxh * xh, axis=-1, keepdims=True))

    body, dep_spec, dep_arg = _with_dep(kern, 5, dep)
    return pl.pallas_call(
        body, name=name, grid=(T // tm,),
        in_specs=[pl.BlockSpec((tm, N), lambda i: (i, 0)),
                  pl.BlockSpec(w.shape, lambda i: (0, 0)),
                  pl.BlockSpec((tm, K), lambda i: (i, 0)),
                  pl.BlockSpec((1, K), lambda i: (0, 0)),
                  pl.BlockSpec((tm, K), lambda i: (i, 0))] + dep_spec,
        out_specs=[pl.BlockSpec((tm, K), lambda i: (i, 0)),
                   pl.BlockSpec((1, K), lambda i: (0, 0))],
        out_shape=[jax.ShapeDtypeStruct((T, K), F32), jax.ShapeDtypeStruct((1, K), F32)],
        compiler_params=_params(1),
    )(dz, w, h, g, dh_in, *dep_arg)


def matmul_tn(x, dy, name, scale=1.0, tk=None, tn=None, tt=1024):
    T, K = x.shape
    N = dy.shape[1]
    tk = K if tk is None else tk
    tn = N if tn is None else tn
    tt = _tile(T, tt)
    nt = T // tt

    def body(x_ref, dy_ref, o_ref, acc_ref):
        t = pl.program_id(2)

        @pl.when(t == 0)
        def _():
            acc_ref[...] = jnp.zeros_like(acc_ref)

        acc_ref[...] += _dot_tn(x_ref[...].astype(BF16), dy_ref[...].astype(BF16))

        @pl.when(t == nt - 1)
        def _():
            o_ref[...] = (acc_ref[...] * scale).astype(o_ref.dtype)

    return pl.pallas_call(
        body, name=name, grid=(K // tk, N // tn, nt),
        in_specs=[pl.BlockSpec((tt, tk), lambda a, b, t: (t, a)),
                  pl.BlockSpec((tt, tn), lambda a, b, t: (t, b))],
        out_specs=pl.BlockSpec((tk, tn), lambda a, b, t: (a, b)),
        out_shape=jax.ShapeDtypeStruct((K, N), BF16),
        scratch_shapes=[pltpu.VMEM((tk, tn), F32)],
        compiler_params=_params(3),
    )(x, dy)


def ffn_down(gu, wd, res, g_next, name, tm=512):
    T = gu.shape[0]
    F = gu.shape[1] // 2
    N = wd.shape[1]
    tm = _tile(T, tm)
    with_next = g_next is not None

    def body(g_ref, up_ref, w_ref, r_ref, *rest):
        o_ref, a_ref = rest[-3:-1] if with_next else rest[-2:]
        g = g_ref[...].astype(F32)
        a_ref[...] = ((g * _sigmoid(g)) * up_ref[...].astype(F32)).astype(BF16)
        x = r_ref[...] + 0.5 * _dot(a_ref[...], w_ref[...])
        o_ref[...] = x
        if with_next:
            r = lax.rsqrt(jnp.mean(x * x, axis=-1, keepdims=True) + RMS_EPS)
            rest[-1][...] = ((x * r) * rest[0][...]).astype(BF16)

    row = lambda width: pl.BlockSpec((tm, width), lambda i: (i, 0))
    res = pl.pallas_call(
        body, name=name, grid=(T // tm,),
        in_specs=[row(F), pl.BlockSpec((tm, F), lambda i: (i, 1)), pl.BlockSpec((F, N), lambda i: (0, 0)), row(N)]
        + ([pl.BlockSpec((1, N), lambda i: (0, 0))] if with_next else []),
        out_specs=[row(N), row(F)] + ([row(N)] if with_next else []),
        out_shape=[jax.ShapeDtypeStruct((T, N), F32), jax.ShapeDtypeStruct((T, F), BF16)]
        + ([jax.ShapeDtypeStruct((T, N), BF16)] if with_next else []),
        compiler_params=_params(1),
    )(gu, gu, wd, res, *([g_next] if with_next else []))
    return (res[0], res[1], res[2]) if with_next else (res[0], res[1], None)


def ffn_down_bwd(dy, wd, gu, name, tm=512, dep=None):
    T, N = dy.shape
    F = wd.shape[0]
    tm = _tile(T, tm)

    def kern(dy_ref, w_ref, g_ref, up_ref, o_ref):
        da = 0.5 * _dot_nt(dy_ref[...].astype(BF16), w_ref[...])
        g = g_ref[...].astype(F32)
        up = up_ref[...].astype(F32)
        s = _sigmoid(g)
        o_ref[:, :F] = (da * up * (s * (1.0 + g * (1.0 - s)))).astype(BF16)
        o_ref[:, F:] = (da * (g * s)).astype(BF16)

    body, dep_spec, dep_arg = _with_dep(kern, 4, dep)
    return pl.pallas_call(
        body, name=name, grid=(T // tm,),
        in_specs=[pl.BlockSpec((tm, N), lambda i: (i, 0)),
                  pl.BlockSpec((F, N), lambda i: (0, 0)),
                  pl.BlockSpec((tm, F), lambda i: (i, 0)),
                  pl.BlockSpec((tm, F), lambda i: (i, 1))] + dep_spec,
        out_specs=pl.BlockSpec((tm, 2 * F), lambda i: (i, 0)),
        out_shape=jax.ShapeDtypeStruct((T, 2 * F), BF16),
        compiler_params=_params(1),
    )(dy, wd, gu, gu, *dep_arg)


def ffn_forward(h, g, w_gu, w_down, name, dep=None, u=None, g_next=None):
    gu, u = norm_matmul(h, g, w_gu, BF16, name + "_gu", D_FF, dep=dep, u=u, tm=1024)
    h_out, a, u_next = ffn_down(gu, w_down, h, g_next, name + "_down")
    return h_out, u_next, (h, u, gu, a)


def ffn_backward(dy, saved, g, w_gu, w_down, name, dep=None, emit=None):
    h, u, gu, a = saved
    dgu = ffn_down_bwd(dy, w_down, gu, name + "_ddown", dep=dep)
    d_w_down = matmul_tn(a, dy, name + "_dwd", scale=0.5, tk=D_FF // 2, tt=2048)
    d_w_gu = matmul_tn(dgu, u, name + "_dwgu", tk=D_FF)
    dh, dg = matmul_norm_bwd(dgu, w_gu, h, g, dy, name + "_dx", dep=emit(d_w_gu, d_w_down))
    return dh, dg


CONV_ROW_CHUNK = 32
CONV_LANES = 128
CONV_X_OFFSETS = tuple(CONV_HALO - (CONV_A_WIDTH - 1) + k for k in range(CONV_A_WIDTH))
CONV_D_OFFSETS = tuple(CONV_A_WIDTH - 1 - k for k in range(CONV_A_WIDTH))


def _build_phases(ref, phase_ref, n_rows):
    for r in range(1, 8):
        phase_ref[r - 1] = ref[pl.ds(r, n_rows - 8), :]


def _tap_values(ref, phase_ref, offsets, n, base, lanes):
    out = {}
    for r in range(8):
        qs = sorted(o // 8 for o in offsets if o % 8 == r)
        if qs:
            lo, hi = qs[0], qs[-1]
            rows = pl.ds(base + 8 * lo, n + 8 * (hi - lo))
            span = ref[rows, lanes] if r == 0 else phase_ref[r - 1, rows, lanes]
            for q in qs:
                out[8 * q + r] = span[8 * (q - lo):8 * (q - lo) + n]
    return out


def conformer_conv_fwd(z, cw, cb, lg, lb, name, tm=512):
    T = z.shape[0]
    C = CONV_A_CH
    tm = _tile(T, tm)
    hb = tm // CONV_HALO
    CH = CONV_ROW_CHUNK
    KW = CONV_A_WIDTH

    def body(v_ref, gt_ref, pv_ref, pg_ref, cw_ref, cb_ref, lg_ref, lb_ref, o_ref, conv_ref, xs_ref, xph_ref):
        i = pl.program_id(0)
        prev = pv_ref[...] * _sigmoid(pg_ref[...])
        xs_ref[0:CONV_HALO, :] = jnp.where(i > 0, prev, 0.0)
        xs_ref[CONV_HALO:, :] = v_ref[...] * _sigmoid(gt_ref[...])
        _build_phases(xs_ref, xph_ref, tm + CONV_HALO)

        def chunk(c, carry):
            off = pl.multiple_of(c * CH, CH)
            for l0 in range(0, C, CONV_LANES):
                lanes = slice(l0, l0 + CONV_LANES)
                taps = _tap_values(xs_ref, xph_ref, CONV_X_OFFSETS, CH, off, lanes)
                acc = jnp.zeros((CH, CONV_LANES), F32) + cb_ref[:, lanes]
                for k in range(KW):
                    acc = acc + cw_ref[k:k + 1, lanes] * taps[CONV_X_OFFSETS[k]]
                conv_ref[pl.ds(off, CH), lanes] = acc
            return carry

        lax.fori_loop(0, tm // CH, chunk, 0)
        acc = conv_ref[...]
        mu = jnp.mean(acc, axis=-1, keepdims=True)
        xc = acc - mu
        var = jnp.mean(xc * xc, axis=-1, keepdims=True)
        y = (xc * lax.rsqrt(var + LN_EPS)) * lg_ref[...] + lb_ref[...]
        o_ref[...] = (y * _sigmoid(y)).astype(BF16)

    return pl.pallas_call(
        body, name=name, grid=(T // tm,),
        in_specs=[pl.BlockSpec((tm, C), lambda i: (i, 0)),
                  pl.BlockSpec((tm, C), lambda i: (i, 1)),
                  pl.BlockSpec((CONV_HALO, C), lambda i: (jnp.maximum(i * hb - 1, 0), 0)),
                  pl.BlockSpec((CONV_HALO, C), lambda i: (jnp.maximum(i * hb - 1, 0), 1)),
                  pl.BlockSpec((32, C), lambda i: (0, 0)),
                  pl.BlockSpec((1, C), lambda i: (0, 0)),
                  pl.BlockSpec((1, C), lambda i: (0, 0)),
                  pl.BlockSpec((1, C), lambda i: (0, 0))],
        out_specs=[pl.BlockSpec((tm, C), lambda i: (i, 0)), pl.BlockSpec((tm, C), lambda i: (i, 0))],
        out_shape=[jax.ShapeDtypeStruct((T, C), BF16), jax.ShapeDtypeStruct((T, C), F32)],
        scratch_shapes=[pltpu.VMEM((tm + CONV_HALO, C), F32), pltpu.VMEM((7, tm + CONV_HALO - 8, C), F32)],
        compiler_params=_params(1),
    )(z, z, z, z, cw, cb, lg, lb)


def conformer_conv_bwd(z, conv_out, dm, cw, lg, lb, name, tm=512):
    T = z.shape[0]
    C = CONV_A_CH
    tm = _tile(T, tm)
    hb = tm // CONV_HALO
    n_tiles = T // tm
    last_halo = T // CONV_HALO - 1
    R = tm + CONV_HALO
    KW = CONV_A_WIDTH
    CH = CONV_ROW_CHUNK

    def body(v_ref, gt_ref, pv_ref, pg_ref, cv_ref, ncv_ref, do_ref, ndo_ref, cw_ref, lg_ref, lb_ref,
             dz_ref, dcw_ref, dcb_ref, dlg_ref, dlb_ref, xs_ref, xph_ref, ds_ref, dph_ref):
        i = pl.program_id(0)

        @pl.when(i == 0)
        def _():
            dcw_ref[...] = jnp.zeros_like(dcw_ref)
            dcb_ref[...] = jnp.zeros_like(dcb_ref)
            dlg_ref[...] = jnp.zeros_like(dlg_ref)
            dlb_ref[...] = jnp.zeros_like(dlb_ref)

        prev = pv_ref[...] * _sigmoid(pg_ref[...])
        xs_ref[0:CONV_HALO, :] = jnp.where(i > 0, prev, 0.0)
        xs_ref[CONV_HALO:, :] = v_ref[...] * _sigmoid(gt_ref[...])
        _build_phases(xs_ref, xph_ref, tm + CONV_HALO)

        acc = jnp.concatenate([cv_ref[...], ncv_ref[...]], axis=0)
        mu = jnp.mean(acc, axis=-1, keepdims=True)
        xc = acc - mu
        rstd = lax.rsqrt(jnp.mean(xc * xc, axis=-1, keepdims=True) + LN_EPS)
        xh = xc * rstd
        y = xh * lg_ref[...] + lb_ref[...]
        s = _sigmoid(y)
        dout = jnp.concatenate([do_ref[...], jnp.where(i < n_tiles - 1, ndo_ref[...], 0.0)], axis=0)
        dy = dout * (s * (1.0 + y * (1.0 - s)))
        dxh = dy * lg_ref[...]
        dconv = rstd * (dxh - jnp.mean(dxh, axis=-1, keepdims=True) - xh * jnp.mean(dxh * xh, axis=-1, keepdims=True))
        ds_ref[...] = dconv
        dlg_ref[...] += jnp.sum(dy[:tm] * xh[:tm], axis=0, keepdims=True)
        dlb_ref[...] += jnp.sum(dy[:tm], axis=0, keepdims=True)
        dcb_ref[...] += jnp.sum(dconv[:tm], axis=0, keepdims=True)
        _build_phases(ds_ref, dph_ref, R)

        for l0 in range(0, C, CONV_LANES):
            lanes = slice(l0, l0 + CONV_LANES)

            def taps_bwd(c, wacc, l0=l0, lanes=lanes):
                off = pl.multiple_of(c * CH, CH)
                x_taps = _tap_values(xs_ref, xph_ref, CONV_X_OFFSETS, CH, off, lanes)
                d_taps = _tap_values(ds_ref, dph_ref, CONV_D_OFFSETS, CH, off, lanes)
                dc = ds_ref[pl.ds(off, CH), lanes]
                dglu = jnp.zeros((CH, CONV_LANES), F32)
                new = []
                for k in range(KW):
                    dglu = dglu + cw_ref[k:k + 1, lanes] * d_taps[CONV_D_OFFSETS[k]]
                    prod = dc * x_taps[CONV_X_OFFSETS[k]]
                    new.append(wacc[k] + ((prod[0:8] + prod[8:16]) + (prod[16:24] + prod[24:32])))
                val = v_ref[pl.ds(off, CH), lanes]
                sg = _sigmoid(gt_ref[pl.ds(off, CH), lanes])
                dz_ref[pl.ds(off, CH), lanes] = (dglu * sg).astype(BF16)
                dz_ref[pl.ds(off, CH), C + l0:C + l0 + CONV_LANES] = (dglu * val * sg * (1.0 - sg)).astype(BF16)
                return tuple(new)

            wacc = lax.fori_loop(0, tm // CH, taps_bwd, tuple(jnp.zeros((8, CONV_LANES), F32) for _ in range(KW)))
            for k in range(KW):
                dcw_ref[k:k + 1, lanes] += jnp.sum(wacc[k], axis=0, keepdims=True)

    prev_map = lambda i: jnp.maximum(i * hb - 1, 0)
    next_map = lambda i: jnp.minimum((i + 1) * hb, last_halo)
    return pl.pallas_call(
        body, name=name, grid=(n_tiles,),
        in_specs=[pl.BlockSpec((tm, C), lambda i: (i, 0)),
                  pl.BlockSpec((tm, C), lambda i: (i, 1)),
                  pl.BlockSpec((CONV_HALO, C), lambda i: (prev_map(i), 0)),
                  pl.BlockSpec((CONV_HALO, C), lambda i: (prev_map(i), 1)),
                  pl.BlockSpec((tm, C), lambda i: (i, 0)),
                  pl.BlockSpec((CONV_HALO, C), lambda i: (next_map(i), 0)),
                  pl.BlockSpec((tm, C), lambda i: (i, 0)),
                  pl.BlockSpec((CONV_HALO, C), lambda i: (next_map(i), 0)),
                  pl.BlockSpec((32, C), lambda i: (0, 0)),
                  pl.BlockSpec((1, C), lambda i: (0, 0)),
                  pl.BlockSpec((1, C), lambda i: (0, 0))],
        out_specs=[pl.BlockSpec((tm, 2 * C), lambda i: (i, 0)),
                   pl.BlockSpec((32, C), lambda i: (0, 0)),
                   pl.BlockSpec((1, C), lambda i: (0, 0)),
                   pl.BlockSpec((1, C), lambda i: (0, 0)),
                   pl.BlockSpec((1, C), lambda i: (0, 0))],
        out_shape=[jax.ShapeDtypeStruct((T, 2 * C), BF16),
                   jax.ShapeDtypeStruct((32, C), F32),
                   jax.ShapeDtypeStruct((1, C), F32),
                   jax.ShapeDtypeStruct((1, C), F32),
                   jax.ShapeDtypeStruct((1, C), F32)],
        scratch_shapes=[pltpu.VMEM((tm + CONV_HALO, C), F32), pltpu.VMEM((7, tm + CONV_HALO - 8, C), F32),
                        pltpu.VMEM((R, C), F32), pltpu.VMEM((7, R - 8, C), F32)],
        compiler_params=_params(1),
    )(z, z, z, z, conv_out, conv_out, dm, dm, cw, lg, lb)


def _swa_scores(q_h, kk_h, slope, bias_dist, valid, sink):
    s = _dot_nt(q_h, kk_h) * (HEAD_DIM ** -0.5) - slope * bias_dist
    s = jnp.where(valid, s, NEG_BIG)
    m = jnp.maximum(jnp.max(s, axis=-1, keepdims=True), sink)
    p = jnp.exp(s - m)
    e_sink = jnp.exp(sink - m)
    inv = 1.0 / (jnp.sum(p, axis=-1, keepdims=True) + e_sink)
    return p * inv, e_sink * inv


def _swa_mask(r0):
    qi = lax.broadcasted_iota(jnp.int32, (WINDOW, 2 * WINDOW), 0)
    kj = lax.broadcasted_iota(jnp.int32, (WINDOW, 2 * WINDOW), 1)
    dist = qi + WINDOW - kj
    valid = (dist >= 0) & (dist < WINDOW) & (r0 - WINDOW + kj >= 0)
    return dist.astype(F32), valid


def swa_fwd(z, kpad, vpad, sinks, name, tq=512):
    T = z.shape[0]
    tq = _tile(T, tq)
    HQ = SWA_HEADS * HEAD_DIM

    def body(sink_ref, q_ref, k_ref, v_ref, o_ref):
        i = pl.program_id(0)
        for sub in range(tq // WINDOW):
            r0 = pl.multiple_of(i * tq + sub * WINDOW, WINDOW)
            kk = k_ref[pl.ds(r0, 2 * WINDOW), :]
            vv = v_ref[pl.ds(r0, 2 * WINDOW), :]
            qb = q_ref[sub * WINDOW:(sub + 1) * WINDOW, :].astype(BF16)
            dist, valid = _swa_mask(r0)
            outs = []
            for h in range(SWA_HEADS):
                kh = h // SWA_GROUP
                ks = slice(kh * HEAD_DIM, (kh + 1) * HEAD_DIM)
                pn, _ = _swa_scores(qb[:, h * HEAD_DIM:(h + 1) * HEAD_DIM], kk[:, ks], 2.0 ** (-(h + 1)), dist, valid,
                                    sink_ref[h])
                outs.append(_dot(pn.astype(BF16), vv[:, ks]))
            o_ref[sub * WINDOW:(sub + 1) * WINDOW, :] = jnp.concatenate(outs, axis=-1).astype(BF16)

    return pl.pallas_call(
        body, name=name, grid=(T // tq,),
        in_specs=[pl.BlockSpec(memory_space=pltpu.SMEM),
                  pl.BlockSpec((tq, HQ), lambda i: (i, 2)),
                  pl.BlockSpec((T + WINDOW, 2 * HEAD_DIM), lambda i: (0, 0)),
                  pl.BlockSpec((T + WINDOW, 2 * HEAD_DIM), lambda i: (0, 0))],
        out_specs=pl.BlockSpec((tq, HQ), lambda i: (i, 0)),
        out_shape=jax.ShapeDtypeStruct((T, HQ), BF16),
        compiler_params=_params(1),
    )(sinks, z, kpad, vpad)


def swa_bwd(z, kpad, vpad, sinks, dm, name, tq=512):
    T = z.shape[0]
    tq = _tile(T, tq)
    HQ = SWA_HEADS * HEAD_DIM
    scale = HEAD_DIM ** -0.5

    def body(sink_ref, q_ref, k_ref, v_ref, do_ref, dq_ref, dk_ref, dv_ref, dsink_ref):
        i = pl.program_id(0)

        @pl.when(i == 0)
        def _():
            dk_ref[...] = jnp.zeros_like(dk_ref)
            dv_ref[...] = jnp.zeros_like(dv_ref)
            dsink_ref[...] = jnp.zeros_like(dsink_ref)

        for sub in range(tq // WINDOW):
            r0 = pl.multiple_of(i * tq + sub * WINDOW, WINDOW)
            kk = k_ref[pl.ds(r0, 2 * WINDOW), :]
            vv = v_ref[pl.ds(r0, 2 * WINDOW), :]
            rows = slice(sub * WINDOW, (sub + 1) * WINDOW)
            qb = q_ref[rows, :].astype(BF16)
            dob = do_ref[rows, :].astype(BF16)
            dist, valid = _swa_mask(r0)
            dqs, dks, dvs = [], [], []
            for kh in range(SWA_KV_HEADS):
                ks = slice(kh * HEAD_DIM, (kh + 1) * HEAD_DIM)
                dk_acc = jnp.zeros((2 * WINDOW, HEAD_DIM), F32)
                dv_acc = jnp.zeros((2 * WINDOW, HEAD_DIM), F32)
                for g in range(SWA_GROUP):
                    h = kh * SWA_GROUP + g
                    hs = slice(h * HEAD_DIM, (h + 1) * HEAD_DIM)
                    pn, p_sink = _swa_scores(qb[:, hs], kk[:, ks], 2.0 ** (-(h + 1)), dist, valid, sink_ref[h])
                    dp = _dot_nt(dob[:, hs], vv[:, ks])
                    delta = jnp.sum(pn * dp, axis=-1, keepdims=True)
                    ds = (pn * (dp - delta)).astype(BF16)
                    dqs.append(_dot(ds, kk[:, ks]) * scale)
                    dk_acc = dk_acc + _dot_tn(ds, qb[:, hs]) * scale
                    dv_acc = dv_acc + _dot_tn(pn.astype(BF16), dob[:, hs])
                    dsink_ref[h:h + 1, :] += jnp.zeros((1, 128), F32) - jnp.sum(p_sink * delta)
                dks.append(dk_acc)
                dvs.append(dv_acc)
            dq_ref[rows, :] = jnp.concatenate(dqs, axis=-1).astype(BF16)
            dk_ref[pl.ds(r0, 2 * WINDOW), :] += jnp.concatenate(dks, axis=-1)
            dv_ref[pl.ds(r0, 2 * WINDOW), :] += jnp.concatenate(dvs, axis=-1)

    kv_spec = pl.BlockSpec((T + WINDOW, 2 * HEAD_DIM), lambda i: (0, 0))
    return pl.pallas_call(
        body, name=name, grid=(T // tq,),
        in_specs=[pl.BlockSpec(memory_space=pltpu.SMEM),
                  pl.BlockSpec((tq, HQ), lambda i: (i, 2)),
                  kv_spec, kv_spec,
                  pl.BlockSpec((tq, HQ), lambda i: (i, 1))],
        out_specs=[pl.BlockSpec((tq, HQ), lambda i: (i, 0)),
                   kv_spec, kv_spec,
                   pl.BlockSpec((SWA_HEADS, 128), lambda i: (0, 0))],
        out_shape=[jax.ShapeDtypeStruct((T, HQ), BF16),
                   jax.ShapeDtypeStruct((T + WINDOW, 2 * HEAD_DIM), F32),
                   jax.ShapeDtypeStruct((T + WINDOW, 2 * HEAD_DIM), F32),
                   jax.ShapeDtypeStruct((SWA_HEADS, 128), F32)],
        compiler_params=_params(1),
    )(sinks, z, kpad, vpad, dm)


def short_conv_fwd(z, w, name, tm=1024):
    T = z.shape[0]
    C = SC_CH
    tm = _tile(T, tm)
    hb = tm // SC_HALO

    def body(b_ref, c_ref, v_ref, pc_ref, pv_ref, w_ref, o_ref, xs_ref):
        i = pl.program_id(0)
        xs_ref[0:SC_HALO, :] = jnp.where(i > 0, pc_ref[...] * pv_ref[...], 0.0)
        xs_ref[SC_HALO:, :] = c_ref[...] * v_ref[...]
        conv = jnp.zeros((tm, C), F32)
        for k in range(3):
            conv = conv + w_ref[k:k + 1, :] * xs_ref[pl.ds(SC_HALO - 2 + k, tm), :]
        o_ref[...] = (b_ref[...] * conv).astype(BF16)

    prev_map = lambda i: jnp.maximum(i * hb - 1, 0)
    return pl.pallas_call(
        body, name=name, grid=(T // tm,),
        in_specs=[pl.BlockSpec((tm, C), lambda i: (i, 0)),
                  pl.BlockSpec((tm, C), lambda i: (i, 1)),
                  pl.BlockSpec((tm, C), lambda i: (i, 2)),
                  pl.BlockSpec((SC_HALO, C), lambda i: (prev_map(i), 1)),
                  pl.BlockSpec((SC_HALO, C), lambda i: (prev_map(i), 2)),
                  pl.BlockSpec((8, C), lambda i: (0, 0))],
        out_specs=pl.BlockSpec((tm, C), lambda i: (i, 0)),
        out_shape=jax.ShapeDtypeStruct((T, C), BF16),
        scratch_shapes=[pltpu.VMEM((tm + SC_HALO, C), F32)],
        compiler_params=_params(1),
    )(z, z, z, z, z, w)


def short_conv_bwd(z, dm, w, name, tm=512):
    T = z.shape[0]
    C = SC_CH
    tm = _tile(T, tm)
    hb = tm // SC_HALO
    n_tiles = T // tm
    last_halo = T // SC_HALO - 1
    R = tm + SC_HALO

    def body(b_ref, c_ref, v_ref, pc_ref, pv_ref, nb_ref, do_ref, ndo_ref, w_ref, dz_ref, dw_ref, xs_ref, ds_ref):
        i = pl.program_id(0)

        @pl.when(i == 0)
        def _():
            dw_ref[...] = jnp.zeros_like(dw_ref)

        c = c_ref[...]
        val = v_ref[...]
        dout = do_ref[...]
        xs_ref[0:SC_HALO, :] = jnp.where(i > 0, pc_ref[...] * pv_ref[...], 0.0)
        xs_ref[SC_HALO:, :] = c * val
        dconv = dout * b_ref[...]
        ds_ref[0:tm, :] = dconv
        ds_ref[tm:, :] = jnp.where(i < n_tiles - 1, ndo_ref[...] * nb_ref[...], 0.0)
        conv = jnp.zeros((tm, C), F32)
        dcv = jnp.zeros((tm, C), F32)
        for k in range(3):
            xk = xs_ref[pl.ds(SC_HALO - 2 + k, tm), :]
            conv = conv + w_ref[k:k + 1, :] * xk
            dw_ref[k:k + 1, :] += jnp.sum(dconv * xk, axis=0, keepdims=True)
            dcv = dcv + w_ref[k:k + 1, :] * ds_ref[pl.ds(2 - k, tm), :]
        dz_ref[:, 0:C] = (dout * conv).astype(BF16)
        dz_ref[:, C:2 * C] = (dcv * val).astype(BF16)
        dz_ref[:, 2 * C:] = (dcv * c).astype(BF16)

    prev_map = lambda i: jnp.maximum(i * hb - 1, 0)
    next_map = lambda i: jnp.minimum((i + 1) * hb, last_halo)
    return pl.pallas_call(
        body, name=name, grid=(n_tiles,),
        in_specs=[pl.BlockSpec((tm, C), lambda i: (i, 0)),
                  pl.BlockSpec((tm, C), lambda i: (i, 1)),
                  pl.BlockSpec((tm, C), lambda i: (i, 2)),
                  pl.BlockSpec((SC_HALO, C), lambda i: (prev_map(i), 1)),
                  pl.BlockSpec((SC_HALO, C), lambda i: (prev_map(i), 2)),
                  pl.BlockSpec((SC_HALO, C), lambda i: (next_map(i), 0)),
                  pl.BlockSpec((tm, C), lambda i: (i, 0)),
                  pl.BlockSpec((SC_HALO, C), lambda i: (next_map(i), 0)),
                  pl.BlockSpec((8, C), lambda i: (0, 0))],
        out_specs=[pl.BlockSpec((tm, 3 * C), lambda i: (i, 0)),
                   pl.BlockSpec((8, C), lambda i: (0, 0))],
        out_shape=[jax.ShapeDtypeStruct((T, 3 * C), BF16), jax.ShapeDtypeStruct((8, C), F32)],
        scratch_shapes=[pltpu.VMEM((tm + SC_HALO, C), F32), pltpu.VMEM((R, C), F32)],
        compiler_params=_params(1),
    )(z, z, z, z, z, z, dm, dm, w)


def _xa_probs(q_h, k_h):
    s = _dot_nt(q_h, k_h) * (XA_HEAD_DIM ** -0.5)
    p = jnp.exp(s - jnp.max(s, axis=-1, keepdims=True))
    return p * (1.0 / jnp.sum(p, axis=-1, keepdims=True))


def xattn_fwd(q, kv, name, tm=2048):
    T = q.shape[0]
    M = kv.shape[0]
    tm = _tile(T, tm)

    def body(q_ref, k_ref, v_ref, o_ref):
        for h in range(XA_HEADS):
            hs = slice(h * XA_HEAD_DIM, (h + 1) * XA_HEAD_DIM)
            p = _xa_probs(q_ref[:, hs], k_ref[:, hs])
            o_ref[:, hs] = _dot(p.astype(BF16), v_ref[:, hs]).astype(BF16)

    return pl.pallas_call(
        body, name=name, grid=(T // tm,),
        in_specs=[pl.BlockSpec((tm, D_MODEL), lambda i: (i, 0)),
                  pl.BlockSpec((M, D_MODEL), lambda i: (0, 0)),
                  pl.BlockSpec((M, D_MODEL), lambda i: (0, 1))],
        out_specs=pl.BlockSpec((tm, D_MODEL), lambda i: (i, 0)),
        out_shape=jax.ShapeDtypeStruct((T, D_MODEL), BF16),
        compiler_params=_params(1),
    )(q, kv, kv)


def xattn_bwd(q, kv, do, name, tm=2048):
    T = q.shape[0]
    M = kv.shape[0]
    tm = _tile(T, tm)
    scale = XA_HEAD_DIM ** -0.5

    def body(q_ref, k_ref, v_ref, do_ref, dq_ref, dkv_ref):
        @pl.when(pl.program_id(0) == 0)
        def _():
            dkv_ref[...] = jnp.zeros_like(dkv_ref)

        for h in range(XA_HEADS):
            hs = slice(h * XA_HEAD_DIM, (h + 1) * XA_HEAD_DIM)
            vs = slice(D_MODEL + h * XA_HEAD_DIM, D_MODEL + (h + 1) * XA_HEAD_DIM)
            q_h = q_ref[:, hs]
            do_h = do_ref[:, hs]
            p = _xa_probs(q_h, k_ref[:, hs])
            dp = _dot_nt(do_h, v_ref[:, hs])
            ds = (p * (dp - jnp.sum(p * dp, axis=-1, keepdims=True))).astype(BF16)
            dq_ref[:, hs] = (_dot(ds, k_ref[:, hs]) * scale).astype(BF16)
            dkv_ref[:, hs] += _dot_tn(ds, q_h) * scale
            dkv_ref[:, vs] += _dot_tn(p.astype(BF16), do_h)

    return pl.pallas_call(
        body, name=name, grid=(T // tm,),
        in_specs=[pl.BlockSpec((tm, D_MODEL), lambda i: (i, 0)),
                  pl.BlockSpec((M, D_MODEL), lambda i: (0, 0)),
                  pl.BlockSpec((M, D_MODEL), lambda i: (0, 1)),
                  pl.BlockSpec((tm, D_MODEL), lambda i: (i, 0))],
        out_specs=[pl.BlockSpec((tm, D_MODEL), lambda i: (i, 0)),
                   pl.BlockSpec((M, 2 * D_MODEL), lambda i: (0, 0))],
        out_shape=[jax.ShapeDtypeStruct((T, D_MODEL), BF16), jax.ShapeDtypeStruct((M, 2 * D_MODEL), F32)],
        compiler_params=_params(1),
    )(q, kv, kv, do)


def final_loss(h, g, target, name, tm=1024):
    T, K = h.shape
    tm = _tile(T, tm)

    def body(h_ref, g_ref, t_ref, dh_ref, dg_ref, loss_ref):
        @pl.when(pl.program_id(0) == 0)
        def _():
            dg_ref[...] = jnp.zeros_like(dg_ref)
            loss_ref[...] = jnp.zeros_like(loss_ref)

        x = h_ref[...]
        r = lax.rsqrt(jnp.mean(x * x, axis=-1, keepdims=True) + RMS_EPS)
        xh = x * r
        e = xh * g_ref[...] - t_ref[...]
        loss_ref[...] += jnp.zeros((1, 128), F32) + 0.5 * jnp.sum(jnp.mean(e * e, axis=-1, keepdims=True))
        dy = e * (1.0 / K)
        dg_ref[...] += jnp.sum(dy * xh, axis=0, keepdims=True)
        dxh = dy * g_ref[...]
        dh_ref[...] = r * (dxh - xh * jnp.mean(dxh * xh, axis=-1, keepdims=True))

    return pl.pallas_call(
        body, name=name, grid=(T // tm,),
        in_specs=[pl.BlockSpec((tm, K), lambda i: (i, 0)),
                  pl.BlockSpec((1, K), lambda i: (0, 0)),
                  pl.BlockSpec((tm, K), lambda i: (i, 0))],
        out_specs=[pl.BlockSpec((tm, K), lambda i: (i, 0)),
                   pl.BlockSpec((1, K), lambda i: (0, 0)),
                   pl.BlockSpec((1, 128), lambda i: (0, 0))],
        out_shape=[jax.ShapeDtypeStruct((T, K), F32), jax.ShapeDtypeStruct((1, K), F32),
                   jax.ShapeDtypeStruct((1, 128), F32)],
        compiler_params=_params(1),
    )(h, g, target)


def _row(v):
    return v.reshape(1, -1)


def _pad_rows(a, rows):
    return jnp.pad(a, ((0, rows - a.shape[0]), (0, 0)))


def local_step(x, mem, target, P, get_weights, put_grads, u_first=None):
    cw = _pad_rows(P["conv_a_w"], 32)
    scw = _pad_rows(P["sc_conv_w"], 8)
    cb, lg, lb = _row(P["conv_a_b"]), _row(P["conv_a_ln_g"]), _row(P["conv_a_ln_b"])
    sinks = P["swa_sinks"]

    class _Layered:
        def __init__(self, store, name=None):
            self.store, self.name = store, name

        def __getitem__(self, key):
            if self.name is None:
                return self.store[(key, 0)] if key in ("even_w_in", "even_w_out", "odd_w_in", "odd_w_out") \
                    else _Layered(self.store, key)
            return self.store[(self.name, key)]

    store = {}
    W = _Layered(store)
    saved = []
    h = x
    u1 = u_first
    for i in range(2):
        L = f"l{i}"
        new, dep = get_weights("A" if i == 0 else "D", h)
        store.update(new)

        h, u2, s_ffn1 = ffn_forward(h, P["ffn1_norm"][i:i + 1], W["ffn1_w_gu"][i], W["ffn1_w_down"][i], L + "_ffn1",
                                    dep=dep, u=u1, g_next=P["mix_norm"][i:i + 1])
        h1 = h
        if i == 0:
            new, _ = get_weights("B", h)
            store.update(new)
            z = matmul(u2, W["even_w_in"], F32, L + "_mix_in", 768, n_tiles=2)
            kv = matmul(u2, W["even_w_in"], BF16, L + "_mix_kv", 256, n_tiles=1, first_tile=6)
            a, conv_out = conformer_conv_fwd(z, cw, cb, lg, lb, L + "_conv")
            kpad = jnp.pad(kv[:, :2 * HEAD_DIM], ((WINDOW, 0), (0, 0)))
            vpad = jnp.pad(kv[:, 2 * HEAD_DIM:], ((WINDOW, 0), (0, 0)))
            o = swa_fwd(z, kpad, vpad, sinks, L + "_swa")
            m = jnp.concatenate([a, o], axis=-1)
            h, u3 = matmul_residual(m, W["even_w_out"], h1, P["xa_norm"][i:i + 1], L + "_mix_out")
            s_mix = (h1, u2, z, m, kpad, vpad, conv_out)
        else:
            z = matmul(u2, W["odd_w_in"], F32, L + "_mix_in", 1024)
            m = short_conv_fwd(z, scw, L + "_sconv")
            h, u3 = matmul_residual(m, W["odd_w_out"], h1, P["xa_norm"][i:i + 1], L + "_mix_out")
            s_mix = (h1, u2, z, m)
        h2 = h
        kv, umem = norm_matmul(mem, P["xa_mem_norm"][i:i + 1], W["xa_wkv"][i], BF16, L + "_xa_kv", 2 * D_MODEL)
        q, u3 = norm_matmul(h2, P["xa_norm"][i:i + 1], W["xa_wq"][i], BF16, L + "_xa_q", D_MODEL, transposed=False, u=u3)
        o = xattn_fwd(q, kv, L + "_xa")
        h, u4 = matmul_residual(o, W["xa_wo"][i], h2, P["ffn2_norm"][i:i + 1], L + "_xa_out")
        s_xa = (h2, u3, q, o, kv, umem)
        new, _ = get_weights("C" if i == 0 else "E", h)
        store.update(new)
        h, u1, s_ffn2 = ffn_forward(h, P["ffn2_norm"][i:i + 1], W["ffn2_w_gu"][i], W["ffn2_w_down"][i], L + "_ffn2", u=u4,
                                    g_next=P["ffn1_norm"][1:2] if i == 0 else None)
        saved.append((s_ffn1, s_mix, s_xa, s_ffn2))

    dh, d_final, loss = final_loss(h, _row(P["final_norm"]), target, "final_loss")

    names = ("ffn1_w_gu", "ffn1_w_down", "ffn2_w_gu", "ffn2_w_down", "xa_wq", "xa_wkv", "xa_wo", "even_w_in", "even_w_out",
             "odd_w_in", "odd_w_out")
    dW = {k: [None, None] for k in names}
    dP = {k: [None, None] for k in ("ffn1_norm", "mix_norm", "xa_norm", "xa_mem_norm", "ffn2_norm")}
    dP["final_norm"] = d_final.reshape(-1)
    for i in (1, 0):
        L = f"l{i}b"
        s_ffn1, s_mix, s_xa, s_ffn2 = saved[i]

        def keep_ffn2(d_w_gu, d_w_down, i=i):
            dW["ffn2_w_gu"][i], dW["ffn2_w_down"][i] = d_w_gu, d_w_down

        def send_ffn1(d_w_gu, d_w_down, i=i):
            dW["ffn1_w_gu"][i], dW["ffn1_w_down"][i] = d_w_gu, d_w_down
            keys = STAGE_KEYS["D"] + STAGE_KEYS["E"] if i == 1 else STAGE_KEYS["A"]
            return put_grads("D" if i == 1 else "A", {k: dW[k[0]][k[1]] for k in keys})

        dh, dP["ffn2_norm"][i] = ffn_backward(
            dh, s_ffn2, P["ffn2_norm"][i:i + 1], W["ffn2_w_gu"][i], W["ffn2_w_down"][i], L + "_ffn2", emit=keep_ffn2)
        h2, u3, q, o, kv, umem = s_xa
        dW["xa_wo"][i] = matmul_tn(o, dh, L + "_xa_dwo")
        do = matmul_nt(dh, W["xa_wo"][i], BF16, L + "_xa_do")
        dq, dkv = xattn_bwd(q, kv, do, L + "_xa")
        dW["xa_wq"][i] = matmul_tn(u3, dq, L + "_xa_dwq")
        dW["xa_wkv"][i] = matmul_tn(dkv, umem, L + "_xa_dwkv", tk=1024)
        dkv_b = dkv.astype(BF16)
        _, dP["xa_mem_norm"][i] = matmul_norm_bwd(dkv_b, W["xa_wkv"][i], mem, P["xa_mem_norm"][i:i + 1],
                                                  jnp.zeros_like(mem), L + "_xa_dmem")
        dh, dP["xa_norm"][i] = matmul_norm_bwd(dq, W["xa_wq"][i], h2, P["xa_norm"][i:i + 1], dh, L + "_xa_dx",
                                               tm=1024, transposed=False)
        if i == 0:
            h1, u2, z, m, kpad, vpad, conv_out = s_mix
            dW["even_w_out"][0] = matmul_tn(m, dh, L + "_mix_dwo")
            dm = matmul_nt(dh, W["even_w_out"], F32, L + "_mix_dm")
            dz_conv, dcw, dcb, dlg, dlb = conformer_conv_bwd(z, conv_out, dm, cw, lg, lb, L + "_conv")
            dq_s, dkp, dvp, dsk = swa_bwd(z, kpad, vpad, sinks, dm, L + "_swa")
            dz = jnp.concatenate([dz_conv, dq_s, dkp[WINDOW:].astype(BF16), dvp[WINDOW:].astype(BF16)], axis=-1)
            dW["even_w_in"][0] = matmul_tn(dz, u2, L + "_mix_dwi", tk=896)
            dh, dP["mix_norm"][i] = matmul_norm_bwd(dz, W["even_w_in"], h1, P["mix_norm"][i:i + 1], dh, L + "_mix_dx",
                                                    tm=1024)
            dP["conv_a_w"] = dcw[:CONV_A_WIDTH]
            dP["conv_a_b"], dP["conv_a_ln_g"], dP["conv_a_ln_b"] = dcb.reshape(-1), dlg.reshape(-1), dlb.reshape(-1)
            dP["swa_sinks"] = dsk[:, 0]
        else:
            h1, u2, z, m = s_mix
            dW["odd_w_out"][0] = matmul_tn(m, dh, L + "_mix_dwo")
            dm = matmul_nt(dh, W["odd_w_out"], F32, L + "_mix_dm")
            dz, dscw = short_conv_bwd(z, dm, scw, L + "_sconv")
            dW["odd_w_in"][0] = matmul_tn(dz, u2, L + "_mix_dwi", tk=1024)
            dh, dP["mix_norm"][i] = matmul_norm_bwd(dz, W["odd_w_in"], h1, P["mix_norm"][i:i + 1], dh, L + "_mix_dx",
                                                    tm=1024)
            dP["sc_conv_w"] = dscw[:3]
        dep = put_grads("BC", {k: dW[k[0]][k[1]] for k in STAGE_KEYS["B"] + STAGE_KEYS["C"]}) if i == 0 else None
        dh, dP["ffn1_norm"][i] = ffn_backward(
            dh, s_ffn1, P["ffn1_norm"][i:i + 1], W["ffn1_w_gu"][i], W["ffn1_w_down"][i], L + "_ffn1", dep=dep,
            emit=send_ffn1)
    for k in ("ffn1_norm", "mix_norm", "xa_norm", "xa_mem_norm", "ffn2_norm"):
        dP[k] = jnp.concatenate(dP[k], axis=0)
    return loss, dh, dP


def _mesh_pos():
    return lax.axis_index("x"), lax.axis_index("y"), lax.axis_index("c")


def _flat_index(px, py, pc):
    return 4 * px + 2 * py + pc


def all_gather(blob, name, dep=None):
    R, C = blob.shape

    def kern(x_ref, out_ref, send_sems, recv_sems, local_sem):
        x, y, c = _mesh_pos()
        me, sibling = (x, y, c), (x, y, 1 - c)
        chips = [(1 - x, y), (x, 1 - y), (1 - x, 1 - y)]

        def slot(px, py, pc):
            return out_ref.at[_flat_index(px, py, pc)]

        def copy(k, block, to, src=None):
            return pltpu.make_async_remote_copy(
                src_ref=slot(*block) if src is None else src, dst_ref=slot(*block),
                send_sem=send_sems.at[k], recv_sem=recv_sems.at[k],
                device_id=to, device_id_type=pl.DeviceIdType.MESH)

        mine = pltpu.make_async_copy(x_ref, slot(*me), local_sem)
        mine.start()
        first = [copy(0, me, sibling, src=x_ref)]
        first += [copy(1 + j, me, (*chip, c), src=x_ref) for j, chip in enumerate(chips)]
        for cp in first:
            cp.start()
        passed = [copy(4 + j, (*chip, c), sibling) for j, chip in enumerate(chips)]
        for j, chip in enumerate(chips):
            copy(1 + j, (*chip, c), me).wait_recv()
            passed[j].start()
        copy(0, sibling, me).wait_recv()
        for j, chip in enumerate(chips):
            copy(4 + j, (*chip, 1 - c), me).wait_recv()
        for cp in first + passed:
            cp.wait_send()
        mine.wait()

    body, dep_spec, dep_arg = _with_dep(kern, 1, dep)
    return pl.pallas_call(
        body, name=name,
        out_shape=jax.ShapeDtypeStruct((N_DEV, R, C), blob.dtype),
        in_specs=[ANY_SPEC] + dep_spec,
        out_specs=ANY_SPEC,
        scratch_shapes=[pltpu.SemaphoreType.DMA((7,)), pltpu.SemaphoreType.DMA((7,)), pltpu.SemaphoreType.DMA],
    )(blob, *dep_arg)


def all_gather_with_norm(blob, h, g, name, tm=1024):
    R, C = blob.shape
    T, K = h.shape
    tm = _tile(T, tm)
    n_steps = T // tm

    def body(x_ref, h_ref, g_ref, out_ref, u_ref, send_sems, recv_sems, local_sem):
        i = pl.program_id(0)
        x, y, c = _mesh_pos()
        me, sibling = (x, y, c), (x, y, 1 - c)
        chips = [(1 - x, y), (x, 1 - y), (1 - x, 1 - y)]

        def slot(px, py, pc):
            return out_ref.at[_flat_index(px, py, pc)]

        def copy(k, block, to, src=None):
            return pltpu.make_async_remote_copy(
                src_ref=slot(*block) if src is None else src, dst_ref=slot(*block),
                send_sem=send_sems.at[k], recv_sem=recv_sems.at[k],
                device_id=to, device_id_type=pl.DeviceIdType.MESH)

        mine = pltpu.make_async_copy(x_ref, slot(*me), local_sem)
        first = [copy(0, me, sibling, src=x_ref)]
        first += [copy(1 + j, me, (*chip, c), src=x_ref) for j, chip in enumerate(chips)]
        passed = [copy(4 + j, (*chip, c), sibling) for j, chip in enumerate(chips)]

        @pl.when(i == 0)
        def _():
            mine.start()
            for cp in first:
                cp.start()

        xt = h_ref[...]
        r = lax.rsqrt(jnp.mean(xt * xt, axis=-1, keepdims=True) + RMS_EPS)
        u_ref[...] = ((xt * r) * g_ref[...]).astype(BF16)

        @pl.when(i == n_steps - 1)
        def _():
            for j, chip in enumerate(chips):
                copy(1 + j, (*chip, c), me).wait_recv()
                passed[j].start()
            copy(0, sibling, me).wait_recv()
            for j, chip in enumerate(chips):
                copy(4 + j, (*chip, 1 - c), me).wait_recv()
            for cp in first + passed:
                cp.wait_send()
            mine.wait()

    return pl.pallas_call(
        body, name=name, grid=(n_steps,),
        out_shape=[jax.ShapeDtypeStruct((N_DEV, R, C), blob.dtype), jax.ShapeDtypeStruct((T, K), BF16)],
        in_specs=[ANY_SPEC, pl.BlockSpec((tm, K), lambda i: (i, 0)), pl.BlockSpec((1, K), lambda i: (0, 0))],
        out_specs=[ANY_SPEC, pl.BlockSpec((tm, K), lambda i: (i, 0))],
        scratch_shapes=[pltpu.SemaphoreType.DMA((7,)), pltpu.SemaphoreType.DMA((7,)), pltpu.SemaphoreType.DMA],
        compiler_params=_params(1),
    )(blob, h, g)


HBM_SPEC = pl.BlockSpec(memory_space=pltpu.HBM)
SEM_SPEC = pl.BlockSpec(memory_space=pltpu.SEMAPHORE)
DATAFLOW_EFFECT = pltpu.SideEffectType.DATAFLOW_SIDE_EFFECTING


def _peers(x, y, c):
    out = []
    for k in range(1, N_DEV):
        pos = (1 - x if k & 4 else x, 1 - y if k & 2 else y, 1 - c if k & 1 else c)
        out.append((_flat_index(*pos), pos))
    return out


def _exchange_copy(src_ref, land_ref, send_sems, recv_sems, j, me, peer_idx, peer, scatter):
    return pltpu.make_async_remote_copy(
        src_ref=src_ref.at[peer_idx] if scatter else src_ref, dst_ref=land_ref.at[me],
        send_sem=send_sems.at[j], recv_sem=recv_sems.at[j], device_id=peer, device_id_type=pl.DeviceIdType.MESH)


def exchange_start(srcs, lands, scatter, after, name):
    n = len(srcs)
    n_after = len(after)

    def body(*refs):
        src_refs, land_refs = refs[:n], refs[n:2 * n]
        outs = refs[2 * n + n_after:]
        send_sems, recv_sems, token = outs[:n], outs[n:2 * n], outs[4 * n]
        x, y, c = _mesh_pos()
        me = _flat_index(x, y, c)
        for g in range(n):
            for j, (peer_idx, peer) in enumerate(_peers(x, y, c)):
                _exchange_copy(src_refs[g], land_refs[g], send_sems[g], recv_sems[g], j, me, peer_idx, peer, scatter).start()
        token[...] = jnp.zeros_like(token)

    hbm = lambda a: pltpu.with_memory_space_constraint(a, pltpu.HBM)
    res = pl.pallas_call(
        body, name=name,
        out_shape=(*[pltpu.SemaphoreType.DMA((N_DEV - 1,))] * (2 * n),
                   *[pltpu.HBM(a.shape, a.dtype) for a in srcs], *[pltpu.HBM(a.shape, a.dtype) for a in lands],
                   jax.ShapeDtypeStruct((8, 128), F32)),
        in_specs=[HBM_SPEC] * (2 * n) + [ANY_SPEC] * n_after,
        out_specs=(*[SEM_SPEC] * (2 * n), *[HBM_SPEC] * (2 * n), pl.BlockSpec(memory_space=pltpu.VMEM)),
        input_output_aliases={i: 2 * n + i for i in range(2 * n)},
        compiler_params=pltpu.CompilerParams(has_side_effects=DATAFLOW_EFFECT),
    )(*[hbm(a) for a in srcs], *[hbm(a) for a in lands], *after)
    handles = [(res[g], res[n + g], res[2 * n + g], res[3 * n + g]) for g in range(n)]
    return handles, res[4 * n]


def exchange_wait(handles, scatter, after, name):
    n = len(handles)

    def body(*refs):
        src_refs, land_refs = refs[:n], refs[n:2 * n]
        send_sems, recv_sems = refs[2 * n:3 * n], refs[3 * n:4 * n]
        x, y, c = _mesh_pos()
        me = _flat_index(x, y, c)
        for g in range(n):
            for j, (peer_idx, peer) in enumerate(_peers(x, y, c)):
                mine = _exchange_copy(src_refs[g], land_refs[g], send_sems[g], recv_sems[g], j, me, peer_idx, peer,
                                      scatter)
                mine.wait_send()
                theirs = pltpu.make_async_remote_copy(
                    src_ref=src_refs[g].at[me] if scatter else src_refs[g], dst_ref=land_refs[g].at[peer_idx],
                    send_sem=send_sems[g].at[j], recv_sem=recv_sems[g].at[j], device_id=peer,
                    device_id_type=pl.DeviceIdType.MESH)
                theirs.wait_recv()

    srcs = [h[2] for h in handles]
    lands = [h[3] for h in handles]
    res = pl.pallas_call(
        body, name=name,
        out_shape=tuple(pltpu.HBM(a.shape, a.dtype) for a in srcs + lands),
        in_specs=[HBM_SPEC] * (2 * n) + [SEM_SPEC] * (2 * n) + [ANY_SPEC],
        out_specs=tuple([HBM_SPEC] * (2 * n)),
        input_output_aliases={i: i for i in range(2 * n)},
        compiler_params=pltpu.CompilerParams(has_side_effects=DATAFLOW_EFFECT),
    )(*srcs, *lands, *[h[0] for h in handles], *[h[1] for h in handles], after)
    return [(res[g], res[n + g]) for g in range(n)]


def ordered_sum(parts, name, tr=512):
    n, R, C = parts.shape
    tr = next((t for t in range(min(tr, R), 15, -16) if R % t == 0), R)

    def body(p_ref, o_ref):
        acc = p_ref[0].astype(F32)
        for j in range(1, n):
            acc = acc + p_ref[j].astype(F32)
        o_ref[...] = acc

    return pl.pallas_call(
        body, name=name, grid=(R // tr,),
        in_specs=[pl.BlockSpec((n, tr, C), lambda i: (0, i, 0))],
        out_specs=pl.BlockSpec((tr, C), lambda i: (i, 0)),
        out_shape=jax.ShapeDtypeStruct((R, C), F32),
        compiler_params=_params(1),
    )(parts)


def adamw(w, g, m, v, name, tr=256):
    R, C = w.shape
    tr = next((t for t in range(tr, 7, -8) if R % t == 0), R)
    c1 = 1.0 - ADAM_B1 ** ADAM_STEP
    c2 = 1.0 - ADAM_B2 ** ADAM_STEP

    def body(w_ref, g_ref, m_ref, v_ref, d_ref, mo_ref, vo_ref):
        grad = g_ref[...]
        m2 = ADAM_B1 * m_ref[...] + (1.0 - ADAM_B1) * grad
        v2 = ADAM_B2 * v_ref[...] + (1.0 - ADAM_B2) * (grad * grad)
        mo_ref[...] = m2
        vo_ref[...] = v2
        d_ref[...] = -ADAM_LR * ((m2 / c1) / (jnp.sqrt(v2 / c2) + ADAM_EPS) + ADAM_WD * w_ref[...])

    spec = pl.BlockSpec((tr, C), lambda i: (i, 0))
    return pl.pallas_call(
        body, name=name, grid=(R // tr,),
        in_specs=[spec] * 4, out_specs=[spec] * 3,
        out_shape=[jax.ShapeDtypeStruct((R, C), F32)] * 3,
        compiler_params=_params(1),
    )(w, g, m, v)


WEIGHT_NAMES = ("ffn1_norm", "ffn1_w_gu", "ffn1_w_down", "mix_norm", "even_w_in", "conv_a_w", "conv_a_b", "conv_a_ln_g",
                "conv_a_ln_b", "swa_sinks", "even_w_out", "odd_w_in", "sc_conv_w", "odd_w_out", "xa_norm", "xa_mem_norm",
                "xa_wq", "xa_wkv", "xa_wo", "ffn2_norm", "ffn2_w_gu", "ffn2_w_down", "final_norm")
BLOB_COLS = 1024
SMALL_ROWS = (("ffn1_norm", 0, 2), ("mix_norm", 2, 2), ("xa_norm", 4, 2), ("xa_mem_norm", 6, 2), ("ffn2_norm", 8, 2),
              ("final_norm", 10, 1))
ROW_CONV_B_LNG = 11
ROW_LNB_SINKS_LOSS = 12
LOSS_COL = 512 + SWA_HEADS
ROW_SC_CONV = 13
ROW_CONV_W = 16
SMALL_BLOB_ROWS = 32
SMALL_ADAM_ROWS = 16


def _small_blob(v):
    rows = [v[n].reshape(-1, D_MODEL) for n, _, _ in SMALL_ROWS]
    rows.append(jnp.concatenate([v["conv_a_b"].reshape(-1), v["conv_a_ln_g"].reshape(-1)]).reshape(1, D_MODEL))
    tail = jnp.zeros((D_MODEL - 512 - SWA_HEADS,), F32)
    if "loss" in v:
        tail = tail.at[0].set(v["loss"])
    rows.append(jnp.concatenate([v["conv_a_ln_b"].reshape(-1), v["swa_sinks"].reshape(-1), tail]).reshape(1, D_MODEL))
    rows.append(jnp.zeros((SMALL_ADAM_ROWS - ROW_SC_CONV, D_MODEL), F32))
    return jnp.concatenate(rows, axis=0)


def _small_unblob(b, shapes):
    out = {n: b[r:r + k].reshape(shapes[n]) for n, r, k in SMALL_ROWS}
    out["conv_a_b"] = b[ROW_CONV_B_LNG, :512].reshape(shapes["conv_a_b"])
    out["conv_a_ln_g"] = b[ROW_CONV_B_LNG, 512:].reshape(shapes["conv_a_ln_g"])
    out["conv_a_ln_b"] = b[ROW_LNB_SINKS_LOSS, :512].reshape(shapes["conv_a_ln_b"])
    out["swa_sinks"] = b[ROW_LNB_SINKS_LOSS, 512:512 + SWA_HEADS].reshape(shapes["swa_sinks"])
    return out


def kernel(x, mem, ffn1_norm, ffn1_w_gu, ffn1_w_down, mix_norm, even_w_in, conv_a_w, conv_a_b, conv_a_ln_g, conv_a_ln_b, swa_sinks, even_w_out, odd_w_in, sc_conv_w, odd_w_out, xa_norm, xa_mem_norm, xa_wq, xa_wkv, xa_wo, ffn2_norm, ffn2_w_gu, ffn2_w_down, final_norm, loss_target, m_ffn1_norm, m_ffn1_w_gu, m_ffn1_w_down, m_mix_norm, m_even_w_in, m_conv_a_w, m_conv_a_b, m_conv_a_ln_g, m_conv_a_ln_b, m_swa_sinks, m_even_w_out, m_odd_w_in, m_sc_conv_w, m_odd_w_out, m_xa_norm, m_xa_mem_norm, m_xa_wq, m_xa_wkv, m_xa_wo, m_ffn2_norm, m_ffn2_w_gu, m_ffn2_w_down, m_final_norm, v_ffn1_norm, v_ffn1_w_gu, v_ffn1_w_down, v_mix_norm, v_even_w_in, v_conv_a_w, v_conv_a_b, v_conv_a_ln_g, v_conv_a_ln_b, v_swa_sinks, v_even_w_out, v_odd_w_in, v_sc_conv_w, v_odd_w_out, v_xa_norm, v_xa_mem_norm, v_xa_wq, v_xa_wkv, v_xa_wo, v_ffn2_norm, v_ffn2_w_gu, v_ffn2_w_down, v_final_norm):
    w = dict(ffn1_norm=ffn1_norm, ffn1_w_gu=ffn1_w_gu, ffn1_w_down=ffn1_w_down, mix_norm=mix_norm, even_w_in=even_w_in,
             conv_a_w=conv_a_w, conv_a_b=conv_a_b, conv_a_ln_g=conv_a_ln_g, conv_a_ln_b=conv_a_ln_b, swa_sinks=swa_sinks,
             even_w_out=even_w_out, odd_w_in=odd_w_in, sc_conv_w=sc_conv_w, odd_w_out=odd_w_out, xa_norm=xa_norm,
             xa_mem_norm=xa_mem_norm, xa_wq=xa_wq, xa_wkv=xa_wkv, xa_wo=xa_wo, ffn2_norm=ffn2_norm, ffn2_w_gu=ffn2_w_gu,
             ffn2_w_down=ffn2_w_down, final_norm=final_norm)
    m = dict(ffn1_norm=m_ffn1_norm, ffn1_w_gu=m_ffn1_w_gu, ffn1_w_down=m_ffn1_w_down, mix_norm=m_mix_norm,
             even_w_in=m_even_w_in, conv_a_w=m_conv_a_w, conv_a_b=m_conv_a_b, conv_a_ln_g=m_conv_a_ln_g,
             conv_a_ln_b=m_conv_a_ln_b, swa_sinks=m_swa_sinks, even_w_out=m_even_w_out, odd_w_in=m_odd_w_in,
             sc_conv_w=m_sc_conv_w, odd_w_out=m_odd_w_out, xa_norm=m_xa_norm, xa_mem_norm=m_xa_mem_norm, xa_wq=m_xa_wq,
             xa_wkv=m_xa_wkv, xa_wo=m_xa_wo, ffn2_norm=m_ffn2_norm, ffn2_w_gu=m_ffn2_w_gu, ffn2_w_down=m_ffn2_w_down,
             final_norm=m_final_norm)
    v = dict(ffn1_norm=v_ffn1_norm, ffn1_w_gu=v_ffn1_w_gu, ffn1_w_down=v_ffn1_w_down, mix_norm=v_mix_norm,
             even_w_in=v_even_w_in, conv_a_w=v_conv_a_w, conv_a_b=v_conv_a_b, conv_a_ln_g=v_conv_a_ln_g,
             conv_a_ln_b=v_conv_a_ln_b, swa_sinks=v_swa_sinks, even_w_out=v_even_w_out, odd_w_in=v_odd_w_in,
             sc_conv_w=v_sc_conv_w, odd_w_out=v_odd_w_out, xa_norm=v_xa_norm, xa_mem_norm=v_xa_mem_norm, xa_wq=v_xa_wq,
             xa_wkv=v_xa_wkv, xa_wo=v_xa_wo, ffn2_norm=v_ffn2_norm, ffn2_w_gu=v_ffn2_w_gu, ffn2_w_down=v_ffn2_w_down,
             final_norm=v_final_norm)
    me = _flat_index(*_mesh_pos())

    conv_blob = jnp.concatenate([w["conv_a_w"].reshape(-1), w["sc_conv_w"].reshape(-1),
                                 jnp.zeros((8 * 1024 - 31 * 64 - 3 * 128,), F32)]).reshape(8, 1024)
    conv_all = all_gather(conv_blob, "gather_conv_weights").reshape(N_DEV, 8 * 1024)
    conv_a_full = jnp.transpose(conv_all[:, :31 * 64].reshape(N_DEV, 31, 64), (1, 0, 2)).reshape(31, 512)
    sc_full = jnp.transpose(conv_all[:, 31 * 64:31 * 64 + 3 * 128].reshape(N_DEV, 3, 128), (1, 0, 2)).reshape(3, 1024)

    def shard_rows(n, l):
        return (w[n][l].T if SPLIT_AXIS[n] == 1 else w[n][l]).astype(BF16)

    def with_own(land, own):
        return lax.dynamic_update_slice(land, own[None], (me, 0, 0))

    a_keys = STAGE_KEYS["A"]
    gathered_a, u_first = all_gather_with_norm(jnp.concatenate([shard_rows(n, l) for n, l in a_keys], axis=0), x[0],
                                               ffn1_norm[0:1], "gather_weights_a")
    later = ("B", "C", "D", "E")
    later_keys = [k for s in later for k in STAGE_KEYS[s]]
    shards = [shard_rows(n, l) for n, l in later_keys]
    lands = [lax.empty((N_DEV,) + s.shape, BF16) for s in shards]
    handles, weight_token = exchange_start(shards, lands, False, [gathered_a, conv_all], "gather_start")
    weight_handles = dict(zip(later_keys, handles))

    def get_weights(stage, after):
        keys = STAGE_KEYS[stage]
        if stage == "A":
            out, off = {}, 0
            for n, l in keys:
                rows = w[n].shape[2] if SPLIT_AXIS[n] == 1 else w[n].shape[1]
                out[(n, l)] = gathered_a[:, off:off + rows, :].reshape(N_DEV * rows, BLOB_COLS)
                off += rows
            return out, weight_token
        got = exchange_wait([weight_handles[k] for k in keys], False, after, "gather_wait_" + stage.lower())
        return {k: with_own(land, own).reshape(-1, BLOB_COLS) for k, (own, land) in zip(keys, got)}, None

    grad_handles = {}

    def put_grads(stage, dws):
        srcs = [dw.reshape(N_DEV, -1, BLOB_COLS) for dw in dws.values()]
        handles, token = exchange_start(srcs, [lax.empty(s.shape, BF16) for s in srcs], True, [],
                                        "scatter_start_" + stage.lower())
        grad_handles[stage] = (handles, tuple(dws))
        return token

    P = dict(ffn1_norm=ffn1_norm, mix_norm=mix_norm, xa_norm=xa_norm, xa_mem_norm=xa_mem_norm, ffn2_norm=ffn2_norm,
             final_norm=final_norm, conv_a_w=conv_a_full, conv_a_b=conv_a_b[0], conv_a_ln_g=conv_a_ln_g[0],
             conv_a_ln_b=conv_a_ln_b[0], swa_sinks=swa_sinks[0], sc_conv_w=sc_full)

    loss_part, grad_x, dP = local_step(x[0], mem[0], loss_target[0], P, get_weights, put_grads, u_first=u_first)

    def finish_grads(stage, after):
        handles, keys = grad_handles[stage]
        got = exchange_wait(handles, True, after, "scatter_wait_" + stage.lower())
        out = {}
        for (n, l), (src, land) in zip(keys, got):
            own = lax.dynamic_slice(src, (me, 0, 0), (1,) + src.shape[1:])[0]
            part = ordered_sum(with_own(land, own), f"sum_grads_{n}_{l}")
            out[(n, l)] = part.T if SPLIT_AXIS[n] == 1 else part
        return out

    layer_grads = {**finish_grads("D", grad_x), **finish_grads("BC", grad_x)}

    grads, delta, new_m, new_v = {}, {}, {}, {}

    def update(n):
        shp = w[n].shape
        two_d = (shp[0] * shp[1], shp[2])
        d_, m_, v_ = adamw(w[n].reshape(two_d), grads[n].reshape(two_d), m[n].reshape(two_d), v[n].reshape(two_d),
                           "adamw_" + n)
        delta[n], new_m[n], new_v[n] = d_.reshape(shp), m_.reshape(shp), v_.reshape(shp)

    first_stage = tuple(n for n, _ in STAGE_KEYS["A"])
    for n in SPLIT_AXIS:
        if n not in first_stage:
            grads[n] = jnp.stack([layer_grads[(n, l)] for l in range(w[n].shape[0])], axis=0)
            update(n)
    layer_grads.update(finish_grads("A", delta["ffn2_w_gu"]))

    dP = dict(dP, loss=loss_part[0, 0])
    small = jnp.concatenate([
        _small_blob(dP)[:ROW_SC_CONV], dP["sc_conv_w"],
        jnp.concatenate([dP["conv_a_w"].reshape(-1), jnp.zeros((512,), F32)]).reshape(16, D_MODEL)], axis=0)
    small_all = all_gather(small, "gather_small_grads", dep=layer_grads[STAGE_KEYS["A"][-1]])
    small_sum = ordered_sum(small_all, "sum_small_grads", tr=SMALL_BLOB_ROWS)
    loss = small_sum[ROW_LNB_SINKS_LOSS, LOSS_COL]
    grads.update(_small_unblob(small_sum, {n: w[n].shape for n in WEIGHT_NAMES}))
    sc_g = small_sum[ROW_SC_CONV:ROW_SC_CONV + 3]
    grads["sc_conv_w"] = lax.dynamic_slice(sc_g, (0, me * 128), (3, 128)).reshape(w["sc_conv_w"].shape)
    cw_g = small_sum[ROW_CONV_W:].reshape(-1)[:31 * 512].reshape(31, 512)
    grads["conv_a_w"] = lax.dynamic_slice(cw_g, (0, me * 64), (31, 64)).reshape(w["conv_a_w"].shape)

    update("conv_a_w")
    update("sc_conv_w")
    for n in first_stage:
        grads[n] = jnp.stack([layer_grads[(n, l)] for l in range(w[n].shape[0])], axis=0)
        update(n)
    d_, m_, v_ = adamw(_small_blob(w), small_sum[:SMALL_ADAM_ROWS], _small_blob(m), _small_blob(v), "adamw_small",
                       tr=SMALL_ADAM_ROWS)
    shapes = {n: w[n].shape for n in WEIGHT_NAMES}
    delta.update(_small_unblob(d_, shapes))
    new_m.update(_small_unblob(m_, shapes))
    new_v.update(_small_unblob(v_, shapes))

    return (loss, grad_x[None], *[grads[n] for n in WEIGHT_NAMES], *[delta[n] for n in WEIGHT_NAMES],
            *[new_m[n] for n in WEIGHT_NAMES], *[new_v[n] for n in WEIGHT_NAMES])
```

```python
import functools

import jax
import jax.numpy as jnp
from jax import lax
from jax.experimental import pallas as pl
from jax.experimental.pallas import tpu as pltpu

F32 = jnp.float32
BF16 = jnp.bfloat16

D_MODEL = 1024
D_FF = 2816
CONV_A_CH = 512
CONV_A_WIDTH = 31
SWA_HEADS = 8
SWA_KV_HEADS = 2
SWA_GROUP = SWA_HEADS // SWA_KV_HEADS
HEAD_DIM = 64
WINDOW = 128
SC_CH = 1024
XA_HEADS = 4
XA_HEAD_DIM = D_MODEL // XA_HEADS
RMS_EPS = 1e-6
LN_EPS = 1e-5
ADAM_LR = 0.001
ADAM_B1 = 0.9
ADAM_B2 = 0.999
ADAM_EPS = 1e-08
ADAM_WD = 0.01
ADAM_STEP = 10
N_DEV = 8

V7X_VMEM_BYTES = 64 * 1024 * 1024
VMEM_LIMIT = V7X_VMEM_BYTES - 8 * 1024 * 1024
CONV_HALO = 32
SC_HALO = 8
NEG_BIG = -1e30

SPLIT_AXIS = dict(ffn1_w_gu=1, ffn1_w_down=0, even_w_in=1, even_w_out=0, odd_w_in=1, odd_w_out=0, xa_wq=0, xa_wkv=1, xa_wo=0,
                  ffn2_w_gu=1, ffn2_w_down=0)
STAGE_KEYS = dict(
    A=(("ffn1_w_gu", 0), ("ffn1_w_down", 0)),
    B=(("even_w_in", 0), ("even_w_out", 0), ("xa_wq", 0), ("xa_wkv", 0), ("xa_wo", 0)),
    C=(("ffn2_w_gu", 0), ("ffn2_w_down", 0)),
    D=(("ffn1_w_gu", 1), ("ffn1_w_down", 1), ("odd_w_in", 0), ("odd_w_out", 0), ("xa_wq", 1), ("xa_wkv", 1), ("xa_wo", 1)),
    E=(("ffn2_w_gu", 1), ("ffn2_w_down", 1)))


def _params(n_axes):
    return pltpu.CompilerParams(dimension_semantics=("arbitrary",) * n_axes, vmem_limit_bytes=VMEM_LIMIT)


def _tile(n, pref):
    t = min(n, pref)
    assert n % t == 0, (n, pref)
    return t


def _dot(a, b):
    return jnp.dot(a, b, preferred_element_type=F32)


def _dot_nt(a, b):
    return lax.dot_general(a, b, (((1,), (1,)), ((), ())), preferred_element_type=F32)


def _dot_tn(a, b):
    return lax.dot_general(a, b, (((0,), (0,)), ((), ())), preferred_element_type=F32)


def _sigmoid(x):
    return 0.5 * jnp.tanh(0.5 * x) + 0.5


ANY_SPEC = pl.BlockSpec(memory_space=pl.ANY)


def _with_dep(body, n_in, dep):
    if dep is None:
        return body, [], []
    return (lambda *refs: body(*refs[:n_in], *refs[n_in + 1:])), [ANY_SPEC], [dep]


def rmsnorm(h, g, name, tm=1024, dep=None):
    T, K = h.shape
    tm = _tile(T, tm)

    def kern(h_ref, g_ref, u_ref):
        x = h_ref[...]
        r = lax.rsqrt(jnp.mean(x * x, axis=-1, keepdims=True) + RMS_EPS)
        u_ref[...] = ((x * r) * g_ref[...]).astype(BF16)

    body, dep_spec, dep_arg = _with_dep(kern, 2, dep)
    return pl.pallas_call(
        body, name=name, grid=(T // tm,),
        in_specs=[pl.BlockSpec((tm, K), lambda i: (i, 0)), pl.BlockSpec((1, K), lambda i: (0, 0))] + dep_spec,
        out_specs=pl.BlockSpec((tm, K), lambda i: (i, 0)),
        out_shape=jax.ShapeDtypeStruct((T, K), BF16),
        compiler_params=_params(1),
    )(h, g, *dep_arg)


def matmul(a, w, out_dtype, name, tn, tm=2048, transposed=True, n_tiles=None, first_tile=0, dep=None):
    T, K = a.shape
    N = w.shape[0] if transposed else w.shape[1]
    n_tiles = N // tn if n_tiles is None else n_tiles
    tm = _tile(T, tm)
    mm = _dot_nt if transposed else _dot

    def kern(a_ref, w_ref, z_ref):
        z_ref[...] = mm(a_ref[...], w_ref[...]).astype(z_ref.dtype)

    body, dep_spec, dep_arg = _with_dep(kern, 2, dep)
    w_spec = (pl.BlockSpec((tn, K), lambda i, j: (first_tile + j, 0)) if transposed
              else pl.BlockSpec((K, tn), lambda i, j: (0, first_tile + j)))
    return pl.pallas_call(
        body, name=name, grid=(T // tm, n_tiles),
        in_specs=[pl.BlockSpec((tm, K), lambda i, j: (i, 0)), w_spec] + dep_spec,
        out_specs=pl.BlockSpec((tm, tn), lambda i, j: (i, j)),
        out_shape=jax.ShapeDtypeStruct((T, n_tiles * tn), out_dtype),
        compiler_params=_params(2),
    )(a, w, *dep_arg)


def norm_matmul(h, g, w, out_dtype, name, tn, dep=None, transposed=True, u=None, tm=2048):
    if u is None:
        u, dep = rmsnorm(h, g, name + "_norm", dep=dep), None
    return matmul(u, w, out_dtype, name, tn, tm=tm, transposed=transposed, dep=dep), u


def matmul_residual(a, w, res, g_next, name, tm=1024):
    T, K = a.shape
    N = w.shape[1]
    tm = _tile(T, tm)

    def body(a_ref, w_ref, r_ref, g_ref, o_ref, u_ref):
        x = r_ref[...] + _dot(a_ref[...], w_ref[...])
        o_ref[...] = x
        r = lax.rsqrt(jnp.mean(x * x, axis=-1, keepdims=True) + RMS_EPS)
        u_ref[...] = ((x * r) * g_ref[...]).astype(BF16)

    return pl.pallas_call(
        body, name=name, grid=(T // tm,),
        in_specs=[pl.BlockSpec((tm, K), lambda i: (i, 0)),
                  pl.BlockSpec((K, N), lambda i: (0, 0)),
                  pl.BlockSpec((tm, N), lambda i: (i, 0)),
                  pl.BlockSpec((1, N), lambda i: (0, 0))],
        out_specs=[pl.BlockSpec((tm, N), lambda i: (i, 0)), pl.BlockSpec((tm, N), lambda i: (i, 0))],
        out_shape=[jax.ShapeDtypeStruct((T, N), F32), jax.ShapeDtypeStruct((T, N), BF16)],
        compiler_params=_params(1),
    )(a, w, res, g_next)


def matmul_nt(dy, w, out_dtype, name, tm=2048):
    T, N = dy.shape
    K = w.shape[0]
    tm = _tile(T, tm)

    def body(dy_ref, w_ref, o_ref):
        o_ref[...] = _dot_nt(dy_ref[...].astype(BF16), w_ref[...]).astype(o_ref.dtype)

    return pl.pallas_call(
        body, name=name, grid=(T // tm,),
        in_specs=[pl.BlockSpec((tm, N), lambda i: (i, 0)),
                  pl.BlockSpec((K, N), lambda i: (0, 0))],
        out_specs=pl.BlockSpec((tm, K), lambda i: (i, 0)),
        out_shape=jax.ShapeDtypeStruct((T, K), out_dtype),
        compiler_params=_params(1),
    )(dy, w)


def matmul_norm_bwd(dz, w, h, g, dh_in, name, tm=512, transposed=True, dep=None):
    T, N = dz.shape
    K = h.shape[1]
    tm = _tile(T, tm)

    def kern(dz_ref, w_ref, h_ref, g_ref, dhin_ref, dh_ref, dg_ref):
        @pl.when(pl.program_id(0) == 0)
        def _():
            dg_ref[...] = jnp.zeros_like(dg_ref)

        mm = _dot if transposed else _dot_nt
        du = mm(dz_ref[...], w_ref[...])
        x = h_ref[...]
        r = lax.rsqrt(jnp.mean(x * x, axis=-1, keepdims=True) + RMS_EPS)
        xh = x * r
        dg_ref[...] += jnp.sum(du * xh, axis=0, keepdims=True)
        dxh = du * g_ref[...]
        dh_ref[...] = dhin_ref[...] + r * (dxh - xh * jnp.mean(dxh * xh, axis=-1, keepdims=True))

    body, dep_spec, dep_arg = _with_dep(kern, 5, dep)
    return pl.pallas_call(
        body, name=name, grid=(T // tm,),
        in_specs=[pl.BlockSpec((tm, N), lambda i: (i, 0)),
                  pl.BlockSpec(w.shape, lambda i: (0, 0)),
                  pl.BlockSpec((tm, K), lambda i: (i, 0)),
                  pl.BlockSpec((1, K), lambda i: (0, 0)),
                  pl.BlockSpec((tm, K), lambda i: (i, 0))] + dep_spec,
        out_specs=[pl.BlockSpec((tm, K), lambda i: (i, 0)),
                   pl.BlockSpec((1, K), lambda i: (0, 0))],
        out_shape=[jax.ShapeDtypeStruct((T, K), F32), jax.ShapeDtypeStruct((1, K), F32)],
        compiler_params=_params(1),
    )(dz, w, h, g, dh_in, *dep_arg)


def matmul_tn(x, dy, name, scale=1.0, tk=None, tn=None, tt=1024):
    T, K = x.shape
    N = dy.shape[1]
    tk = K if tk is None else tk
    tn = N if tn is None else tn
    tt = _tile(T, tt)
    nt = T // tt

    def body(x_ref, dy_ref, o_ref, acc_ref):
        t = pl.program_id(2)

        @pl.when(t == 0)
        def _():
            acc_ref[...] = jnp.zeros_like(acc_ref)

        acc_ref[...] += _dot_tn(x_ref[...].astype(BF16), dy_ref[...].astype(BF16))

        @pl.when(t == nt - 1)
        def _():
            o_ref[...] = (acc_ref[...] * scale).astype(o_ref.dtype)

    return pl.pallas_call(
        body, name=name, grid=(K // tk, N // tn, nt),
        in_specs=[pl.BlockSpec((tt, tk), lambda a, b, t: (t, a)),
                  pl.BlockSpec((tt, tn), lambda a, b, t: (t, b))],
        out_specs=pl.BlockSpec((tk, tn), lambda a, b, t: (a, b)),
        out_shape=jax.ShapeDtypeStruct((K, N), BF16),
        scratch_shapes=[pltpu.VMEM((tk, tn), F32)],
        compiler_params=_params(3),
    )(x, dy)


def ffn_down(gu, wd, res, g_next, name, tm=512):
    T = gu.shape[0]
    F = gu.shape[1] // 2
    N = wd.shape[1]
    tm = _tile(T, tm)
    with_next = g_next is not None

    def body(g_ref, up_ref, w_ref, r_ref, *rest):
        o_ref, a_ref = rest[-3:-1] if with_next else rest[-2:]
        g = g_ref[...].astype(F32)
        a_ref[...] = ((g * _sigmoid(g)) * up_ref[...].astype(F32)).astype(BF16)
        x = r_ref[...] + 0.5 * _dot(a_ref[...], w_ref[...])
        o_ref[...] = x
        if with_next:
            r = lax.rsqrt(jnp.mean(x * x, axis=-1, keepdims=True) + RMS_EPS)
            rest[-1][...] = ((x * r) * rest[0][...]).astype(BF16)

    row = lambda width: pl.BlockSpec((tm, width), lambda i: (i, 0))
    res = pl.pallas_call(
        body, name=name, grid=(T // tm,),
        in_specs=[row(F), pl.BlockSpec((tm, F), lambda i: (i, 1)), pl.BlockSpec((F, N), lambda i: (0, 0)), row(N)]
        + ([pl.BlockSpec((1, N), lambda i: (0, 0))] if with_next else []),
        out_specs=[row(N), row(F)] + ([row(N)] if with_next else []),
        out_shape=[jax.ShapeDtypeStruct((T, N), F32), jax.ShapeDtypeStruct((T, F), BF16)]
        + ([jax.ShapeDtypeStruct((T, N), BF16)] if with_next else []),
        compiler_params=_params(1),
    )(gu, gu, wd, res, *([g_next] if with_next else []))
    return (res[0], res[1], res[2]) if with_next else (res[0], res[1], None)


def ffn_down_bwd(dy, wd, gu, name, tm=512, dep=None):
    T, N = dy.shape
    F = wd.shape[0]
    tm = _tile(T, tm)

    def kern(dy_ref, w_ref, g_ref, up_ref, o_ref):
        da = 0.5 * _dot_nt(dy_ref[...].astype(BF16), w_ref[...])
        g = g_ref[...].astype(F32)
        up = up_ref[...].astype(F32)
        s = _sigmoid(g)
        o_ref[:, :F] = (da * up * (s * (1.0 + g * (1.0 - s)))).astype(BF16)
        o_ref[:, F:] = (da * (g * s)).astype(BF16)

    body, dep_spec, dep_arg = _with_dep(kern, 4, dep)
    return pl.pallas_call(
        body, name=name, grid=(T // tm,),
        in_specs=[pl.BlockSpec((tm, N), lambda i: (i, 0)),
                  pl.BlockSpec((F, N), lambda i: (0, 0)),
                  pl.BlockSpec((tm, F), lambda i: (i, 0)),
                  pl.BlockSpec((tm, F), lambda i: (i, 1))] + dep_spec,
        out_specs=pl.BlockSpec((tm, 2 * F), lambda i: (i, 0)),
        out_shape=jax.ShapeDtypeStruct((T, 2 * F), BF16),
        compiler_params=_params(1),
    )(dy, wd, gu, gu, *dep_arg)


def ffn_forward(h, g, w_gu, w_down, name, dep=None, u=None, g_next=None):
    gu, u = norm_matmul(h, g, w_gu, BF16, name + "_gu", D_FF, dep=dep, u=u, tm=1024)
    h_out, a, u_next = ffn_down(gu, w_down, h, g_next, name + "_down")
    return h_out, u_next, (h, u, gu, a)


def ffn_backward(dy, saved, g, w_gu, w_down, name, dep=None, emit=None):
    h, u, gu, a = saved
    dgu = ffn_down_bwd(dy, w_down, gu, name + "_ddown", dep=dep)
    d_w_down = matmul_tn(a, dy, name + "_dwd", scale=0.5, tk=D_FF // 2, tt=2048)
    d_w_gu = matmul_tn(dgu, u, name + "_dwgu", tk=D_FF)
    dh, dg = matmul_norm_bwd(dgu, w_gu, h, g, dy, name + "_dx", dep=emit(d_w_gu, d_w_down))
    return dh, dg


CONV_ROW_CHUNK = 32
CONV_LANES = 128
CONV_X_OFFSETS = tuple(CONV_HALO - (CONV_A_WIDTH - 1) + k for k in range(CONV_A_WIDTH))
CONV_D_OFFSETS = tuple(CONV_A_WIDTH - 1 - k for k in range(CONV_A_WIDTH))


def _build_phases(ref, phase_ref, n_rows):
    for r in range(1, 8):
        phase_ref[r - 1] = ref[pl.ds(r, n_rows - 8), :]


def _tap_values(ref, phase_ref, offsets, n, base, lanes):
    out = {}
    for r in range(8):
        qs = sorted(o // 8 for o in offsets if o % 8 == r)
        if qs:
            lo, hi = qs[0], qs[-1]
            rows = pl.ds(base + 8 * lo, n + 8 * (hi - lo))
            span = ref[rows, lanes] if r == 0 else phase_ref[r - 1, rows, lanes]
            for q in qs:
                out[8 * q + r] = span[8 * (q - lo):8 * (q - lo) + n]
    return out


def conformer_conv_fwd(z, cw, cb, lg, lb, name, tm=512):
    T = z.shape[0]
    C = CONV_A_CH
    tm = _tile(T, tm)
    hb = tm // CONV_HALO
    CH = CONV_ROW_CHUNK
    KW = CONV_A_WIDTH

    def body(v_ref, gt_ref, pv_ref, pg_ref, cw_ref, cb_ref, lg_ref, lb_ref, o_ref, conv_ref, xs_ref, xph_ref):
        i = pl.program_id(0)
        prev = pv_ref[...] * _sigmoid(pg_ref[...])
        xs_ref[0:CONV_HALO, :] = jnp.where(i > 0, prev, 0.0)
        xs_ref[CONV_HALO:, :] = v_ref[...] * _sigmoid(gt_ref[...])
        _build_phases(xs_ref, xph_ref, tm + CONV_HALO)

        def chunk(c, carry):
            off = pl.multiple_of(c * CH, CH)
            for l0 in range(0, C, CONV_LANES):
                lanes = slice(l0, l0 + CONV_LANES)
                taps = _tap_values(xs_ref, xph_ref, CONV_X_OFFSETS, CH, off, lanes)
                acc = jnp.zeros((CH, CONV_LANES), F32) + cb_ref[:, lanes]
                for k in range(KW):
                    acc = acc + cw_ref[k:k + 1, lanes] * taps[CONV_X_OFFSETS[k]]
                conv_ref[pl.ds(off, CH), lanes] = acc
            return carry

        lax.fori_loop(0, tm // CH, chunk, 0)
        acc = conv_ref[...]
        mu = jnp.mean(acc, axis=-1, keepdims=True)
        xc = acc - mu
        var = jnp.mean(xc * xc, axis=-1, keepdims=True)
        y = (xc * lax.rsqrt(var + LN_EPS)) * lg_ref[...] + lb_ref[...]
        o_ref[...] = (y * _sigmoid(y)).astype(BF16)

    return pl.pallas_call(
        body, name=name, grid=(T // tm,),
        in_specs=[pl.BlockSpec((tm, C), lambda i: (i, 0)),
                  pl.BlockSpec((tm, C), lambda i: (i, 1)),
                  pl.BlockSpec((CONV_HALO, C), lambda i: (jnp.maximum(i * hb - 1, 0), 0)),
                  pl.BlockSpec((CONV_HALO, C), lambda i: (jnp.maximum(i * hb - 1, 0), 1)),
                  pl.BlockSpec((32, C), lambda i: (0, 0)),
                  pl.BlockSpec((1, C), lambda i: (0, 0)),
                  pl.BlockSpec((1, C), lambda i: (0, 0)),
                  pl.BlockSpec((1, C), lambda i: (0, 0))],
        out_specs=[pl.BlockSpec((tm, C), lambda i: (i, 0)), pl.BlockSpec((tm, C), lambda i: (i, 0))],
        out_shape=[jax.ShapeDtypeStruct((T, C), BF16), jax.ShapeDtypeStruct((T, C), F32)],
        scratch_shapes=[pltpu.VMEM((tm + CONV_HALO, C), F32), pltpu.VMEM((7, tm + CONV_HALO - 8, C), F32)],
        compiler_params=_params(1),
    )(z, z, z, z, cw, cb, lg, lb)


def conformer_conv_bwd(z, conv_out, dm, cw, lg, lb, name, tm=512):
    T = z.shape[0]
    C = CONV_A_CH
    tm = _tile(T, tm)
    hb = tm // CONV_HALO
    n_tiles = T // tm
    last_halo = T // CONV_HALO - 1
    R = tm + CONV_HALO
    KW = CONV_A_WIDTH
    CH = CONV_ROW_CHUNK

    def body(v_ref, gt_ref, pv_ref, pg_ref, cv_ref, ncv_ref, do_ref, ndo_ref, cw_ref, lg_ref, lb_ref,
             dz_ref, dcw_ref, dcb_ref, dlg_ref, dlb_ref, xs_ref, xph_ref, ds_ref, dph_ref):
        i = pl.program_id(0)

        @pl.when(i == 0)
        def _():
            dcw_ref[...] = jnp.zeros_like(dcw_ref)
            dcb_ref[...] = jnp.zeros_like(dcb_ref)
            dlg_ref[...] = jnp.zeros_like(dlg_ref)
            dlb_ref[...] = jnp.zeros_like(dlb_ref)

        prev = pv_ref[...] * _sigmoid(pg_ref[...])
        xs_ref[0:CONV_HALO, :] = jnp.where(i > 0, prev, 0.0)
        xs_ref[CONV_HALO:, :] = v_ref[...] * _sigmoid(gt_ref[...])
        _build_phases(xs_ref, xph_ref, tm + CONV_HALO)

        acc = jnp.concatenate([cv_ref[...], ncv_ref[...]], axis=0)
        mu = jnp.mean(acc, axis=-1, keepdims=True)
        xc = acc - mu
        rstd = lax.rsqrt(jnp.mean(xc * xc, axis=-1, keepdims=True) + LN_EPS)
        xh = xc * rstd
        y = xh * lg_ref[...] + lb_ref[...]
        s = _sigmoid(y)
        dout = jnp.concatenate([do_ref[...], jnp.where(i < n_tiles - 1, ndo_ref[...], 0.0)], axis=0)
        dy = dout * (s * (1.0 + y * (1.0 - s)))
        dxh = dy * lg_ref[...]
        dconv = rstd * (dxh - jnp.mean(dxh, axis=-1, keepdims=True) - xh * jnp.mean(dxh * xh, axis=-1, keepdims=True))
        ds_ref[...] = dconv
        dlg_ref[...] += jnp.sum(dy[:tm] * xh[:tm], axis=0, keepdims=True)
        dlb_ref[...] += jnp.sum(dy[:tm], axis=0, keepdims=True)
        dcb_ref[...] += jnp.sum(dconv[:tm], axis=0, keepdims=True)
        _build_phases(ds_ref, dph_ref, R)

        for l0 in range(0, C, CONV_LANES):
            lanes = slice(l0, l0 + CONV_LANES)

            def taps_bwd(c, wacc, l0=l0, lanes=lanes):
                off = pl.multiple_of(c * CH, CH)
                x_taps = _tap_values(xs_ref, xph_ref, CONV_X_OFFSETS, CH, off, lanes)
                d_taps = _tap_values(ds_ref, dph_ref, CONV_D_OFFSETS, CH, off, lanes)
                dc = ds_ref[pl.ds(off, CH), lanes]
                dglu = jnp.zeros((CH, CONV_LANES), F32)
                new = []
                for k in range(KW):
                    dglu = dglu + cw_ref[k:k + 1, lanes] * d_taps[CONV_D_OFFSETS[k]]
                    prod = dc * x_taps[CONV_X_OFFSETS[k]]
                    new.append(wacc[k] + ((prod[0:8] + prod[8:16]) + (prod[16:24] + prod[24:32])))
                val = v_ref[pl.ds(off, CH), lanes]
                sg = _sigmoid(gt_ref[pl.ds(off, CH), lanes])
                dz_ref[pl.ds(off, CH), lanes] = (dglu * sg).astype(BF16)
                dz_ref[pl.ds(off, CH), C + l0:C + l0 + CONV_LANES] = (dglu * val * sg * (1.0 - sg)).astype(BF16)
                return tuple(new)

            wacc = lax.fori_loop(0, tm // CH, taps_bwd, tuple(jnp.zeros((8, CONV_LANES), F32) for _ in range(KW)))
            for k in range(KW):
                dcw_ref[k:k + 1, lanes] += jnp.sum(wacc[k], axis=0, keepdims=True)

    prev_map = lambda i: jnp.maximum(i * hb - 1, 0)
    next_map = lambda i: jnp.minimum((i + 1) * hb, last_halo)
    return pl.pallas_call(
        body, name=name, grid=(n_tiles,),
        in_specs=[pl.BlockSpec((tm, C), lambda i: (i, 0)),
                  pl.BlockSpec((tm, C), lambda i: (i, 1)),
                  pl.BlockSpec((CONV_HALO, C), lambda i: (prev_map(i), 0)),
                  pl.BlockSpec((CONV_HALO, C), lambda i: (prev_map(i), 1)),
                  pl.BlockSpec((tm, C), lambda i: (i, 0)),
                  pl.BlockSpec((CONV_HALO, C), lambda i: (next_map(i), 0)),
                  pl.BlockSpec((tm, C), lambda i: (i, 0)),
                  pl.BlockSpec((CONV_HALO, C), lambda i: (next_map(i), 0)),
                  pl.BlockSpec((32, C), lambda i: (0, 0)),
                  pl.BlockSpec((1, C), lambda i: (0, 0)),
                  pl.BlockSpec((1, C), lambda i: (0, 0))],
        out_specs=[pl.BlockSpec((tm, 2 * C), lambda i: (i, 0)),
                   pl.BlockSpec((32, C), lambda i: (0, 0)),
                   pl.BlockSpec((1, C), lambda i: (0, 0)),
                   pl.BlockSpec((1, C), lambda i: (0, 0)),
                   pl.BlockSpec((1, C), lambda i: (0, 0))],
        out_shape=[jax.ShapeDtypeStruct((T, 2 * C), BF16),
                   jax.ShapeDtypeStruct((32, C), F32),
                   jax.ShapeDtypeStruct((1, C), F32),
                   jax.ShapeDtypeStruct((1, C), F32),
                   jax.ShapeDtypeStruct((1, C), F32)],
        scratch_shapes=[pltpu.VMEM((tm + CONV_HALO, C), F32), pltpu.VMEM((7, tm + CONV_HALO - 8, C), F32),
                        pltpu.VMEM((R, C), F32), pltpu.VMEM((7, R - 8, C), F32)],
        compiler_params=_params(1),
    )(z, z, z, z, conv_out, conv_out, dm, dm, cw, lg, lb)


def _swa_scores(q_h, kk_h, slope, bias_dist, valid, sink):
    s = _dot_nt(q_h, kk_h) * (HEAD_DIM ** -0.5) - slope * bias_dist
    s = jnp.where(valid, s, NEG_BIG)
    m = jnp.maximum(jnp.max(s, axis=-1, keepdims=True), sink)
    p = jnp.exp(s - m)
    e_sink = jnp.exp(sink - m)
    inv = 1.0 / (jnp.sum(p, axis=-1, keepdims=True) + e_sink)
    return p * inv, e_sink * inv


def _swa_mask(r0):
    qi = lax.broadcasted_iota(jnp.int32, (WINDOW, 2 * WINDOW), 0)
    kj = lax.broadcasted_iota(jnp.int32, (WINDOW, 2 * WINDOW), 1)
    dist = qi + WINDOW - kj
    valid = (dist >= 0) & (dist < WINDOW) & (r0 - WINDOW + kj >= 0)
    return dist.astype(F32), valid


def swa_fwd(z, kpad, vpad, sinks, name, tq=512):
    T = z.shape[0]
    tq = _tile(T, tq)
    HQ = SWA_HEADS * HEAD_DIM

    def body(sink_ref, q_ref, k_ref, v_ref, o_ref):
        i = pl.program_id(0)
        for sub in range(tq // WINDOW):
            r0 = pl.multiple_of(i * tq + sub * WINDOW, WINDOW)
            kk = k_ref[pl.ds(r0, 2 * WINDOW), :]
            vv = v_ref[pl.ds(r0, 2 * WINDOW), :]
            qb = q_ref[sub * WINDOW:(sub + 1) * WINDOW, :].astype(BF16)
            dist, valid = _swa_mask(r0)
            outs = []
            for h in range(SWA_HEADS):
                kh = h // SWA_GROUP
                ks = slice(kh * HEAD_DIM, (kh + 1) * HEAD_DIM)
                pn, _ = _swa_scores(qb[:, h * HEAD_DIM:(h + 1) * HEAD_DIM], kk[:, ks], 2.0 ** (-(h + 1)), dist, valid,
                                    sink_ref[h])
                outs.append(_dot(pn.astype(BF16), vv[:, ks]))
            o_ref[sub * WINDOW:(sub + 1) * WINDOW, :] = jnp.concatenate(outs, axis=-1).astype(BF16)

    return pl.pallas_call(
        body, name=name, grid=(T // tq,),
        in_specs=[pl.BlockSpec(memory_space=pltpu.SMEM),
                  pl.BlockSpec((tq, HQ), lambda i: (i, 2)),
                  pl.BlockSpec((T + WINDOW, 2 * HEAD_DIM), lambda i: (0, 0)),
                  pl.BlockSpec((T + WINDOW, 2 * HEAD_DIM), lambda i: (0, 0))],
        out_specs=pl.BlockSpec((tq, HQ), lambda i: (i, 0)),
        out_shape=jax.ShapeDtypeStruct((T, HQ), BF16),
        compiler_params=_params(1),
    )(sinks, z, kpad, vpad)


def swa_bwd(z, kpad, vpad, sinks, dm, name, tq=512):
    T = z.shape[0]
    tq = _tile(T, tq)
    HQ = SWA_HEADS * HEAD_DIM
    scale = HEAD_DIM ** -0.5

    def body(sink_ref, q_ref, k_ref, v_ref, do_ref, dq_ref, dk_ref, dv_ref, dsink_ref):
        i = pl.program_id(0)

        @pl.when(i == 0)
        def _():
            dk_ref[...] = jnp.zeros_like(dk_ref)
            dv_ref[...] = jnp.zeros_like(dv_ref)
            dsink_ref[...] = jnp.zeros_like(dsink_ref)

        for sub in range(tq // WINDOW):
            r0 = pl.multiple_of(i * tq + sub * WINDOW, WINDOW)
            kk = k_ref[pl.ds(r0, 2 * WINDOW), :]
            vv = v_ref[pl.ds(r0, 2 * WINDOW), :]
            rows = slice(sub * WINDOW, (sub + 1) * WINDOW)
            qb = q_ref[rows, :].astype(BF16)
            dob = do_ref[rows, :].astype(BF16)
            dist, valid = _swa_mask(r0)
            dqs, dks, dvs = [], [], []
            for kh in range(SWA_KV_HEADS):
                ks = slice(kh * HEAD_DIM, (kh + 1) * HEAD_DIM)
                dk_acc = jnp.zeros((2 * WINDOW, HEAD_DIM), F32)
                dv_acc = jnp.zeros((2 * WINDOW, HEAD_DIM), F32)
                for g in range(SWA_GROUP):
                    h = kh * SWA_GROUP + g
                    hs = slice(h * HEAD_DIM, (h + 1) * HEAD_DIM)
                    pn, p_sink = _swa_scores(qb[:, hs], kk[:, ks], 2.0 ** (-(h + 1)), dist, valid, sink_ref[h])
                    dp = _dot_nt(dob[:, hs], vv[:, ks])
                    delta = jnp.sum(pn * dp, axis=-1, keepdims=True)
                    ds = (pn * (dp - delta)).astype(BF16)
                    dqs.append(_dot(ds, kk[:, ks]) * scale)
                    dk_acc = dk_acc + _dot_tn(ds, qb[:, hs]) * scale
                    dv_acc = dv_acc + _dot_tn(pn.astype(BF16), dob[:, hs])
                    dsink_ref[h:h + 1, :] += jnp.zeros((1, 128), F32) - jnp.sum(p_sink * delta)
                dks.append(dk_acc)
                dvs.append(dv_acc)
            dq_ref[rows, :] = jnp.concatenate(dqs, axis=-1).astype(BF16)
            dk_ref[pl.ds(r0, 2 * WINDOW), :] += jnp.concatenate(dks, axis=-1)
            dv_ref[pl.ds(r0, 2 * WINDOW), :] += jnp.concatenate(dvs, axis=-1)

    kv_spec = pl.BlockSpec((T + WINDOW, 2 * HEAD_DIM), lambda i: (0, 0))
    return pl.pallas_call(
        body, name=name, grid=(T // tq,),
        in_specs=[pl.BlockSpec(memory_space=pltpu.SMEM),
                  pl.BlockSpec((tq, HQ), lambda i: (i, 2)),
                  kv_spec, kv_spec,
                  pl.BlockSpec((tq, HQ), lambda i: (i, 1))],
        out_specs=[pl.BlockSpec((tq, HQ), lambda i: (i, 0)),
                   kv_spec, kv_spec,
                   pl.BlockSpec((SWA_HEADS, 128), lambda i: (0, 0))],
        out_shape=[jax.ShapeDtypeStruct((T, HQ), BF16),
                   jax.ShapeDtypeStruct((T + WINDOW, 2 * HEAD_DIM), F32),
                   jax.ShapeDtypeStruct((T + WINDOW, 2 * HEAD_DIM), F32),
                   jax.ShapeDtypeStruct((SWA_HEADS, 128), F32)],
        compiler_params=_params(1),
    )(sinks, z, kpad, vpad, dm)


def short_conv_fwd(z, w, name, tm=1024):
    T = z.shape[0]
    C = SC_CH
    tm = _tile(T, tm)
    hb = tm // SC_HALO

    def body(b_ref, c_ref, v_ref, pc_ref, pv_ref, w_ref, o_ref, xs_ref):
        i = pl.program_id(0)
        xs_ref[0:SC_HALO, :] = jnp.where(i > 0, pc_ref[...] * pv_ref[...], 0.0)
        xs_ref[SC_HALO:, :] = c_ref[...] * v_ref[...]
        conv = jnp.zeros((tm, C), F32)
        for k in range(3):
            conv = conv + w_ref[k:k + 1, :] * xs_ref[pl.ds(SC_HALO - 2 + k, tm), :]
        o_ref[...] = (b_ref[...] * conv).astype(BF16)

    prev_map = lambda i: jnp.maximum(i * hb - 1, 0)
    return pl.pallas_call(
        body, name=name, grid=(T // tm,),
        in_specs=[pl.BlockSpec((tm, C), lambda i: (i, 0)),
                  pl.BlockSpec((tm, C), lambda i: (i, 1)),
                  pl.BlockSpec((tm, C), lambda i: (i, 2)),
                  pl.BlockSpec((SC_HALO, C), lambda i: (prev_map(i), 1)),
                  pl.BlockSpec((SC_HALO, C), lambda i: (prev_map(i), 2)),
                  pl.BlockSpec((8, C), lambda i: (0, 0))],
        out_specs=pl.BlockSpec((tm, C), lambda i: (i, 0)),
        out_shape=jax.ShapeDtypeStruct((T, C), BF16),
        scratch_shapes=[pltpu.VMEM((tm + SC_HALO, C), F32)],
        compiler_params=_params(1),
    )(z, z, z, z, z, w)


def short_conv_bwd(z, dm, w, name, tm=512):
    T = z.shape[0]
    C = SC_CH
    tm = _tile(T, tm)
    hb = tm // SC_HALO
    n_tiles = T // tm
    last_halo = T // SC_HALO - 1
    R = tm + SC_HALO

    def body(b_ref, c_ref, v_ref, pc_ref, pv_ref, nb_ref, do_ref, ndo_ref, w_ref, dz_ref, dw_ref, xs_ref, ds_ref):
        i = pl.program_id(0)

        @pl.when(i == 0)
        def _():
            dw_ref[...] = jnp.zeros_like(dw_ref)

        c = c_ref[...]
        val = v_ref[...]
        dout = do_ref[...]
        xs_ref[0:SC_HALO, :] = jnp.where(i > 0, pc_ref[...] * pv_ref[...], 0.0)
        xs_ref[SC_HALO:, :] = c * val
        dconv = dout * b_ref[...]
        ds_ref[0:tm, :] = dconv
        ds_ref[tm:, :] = jnp.where(i < n_tiles - 1, ndo_ref[...] * nb_ref[...], 0.0)
        conv = jnp.zeros((tm, C), F32)
        dcv = jnp.zeros((tm, C), F32)
        for k in range(3):
            xk = xs_ref[pl.ds(SC_HALO - 2 + k, tm), :]
            conv = conv + w_ref[k:k + 1, :] * xk
            dw_ref[k:k + 1, :] += jnp.sum(dconv * xk, axis=0, keepdims=True)
            dcv = dcv + w_ref[k:k + 1, :] * ds_ref[pl.ds(2 - k, tm), :]
        dz_ref[:, 0:C] = (dout * conv).astype(BF16)
        dz_ref[:, C:2 * C] = (dcv * val).astype(BF16)
        dz_ref[:, 2 * C:] = (dcv * c).astype(BF16)

    prev_map = lambda i: jnp.maximum(i * hb - 1, 0)
    next_map = lambda i: jnp.minimum((i + 1) * hb, last_halo)
    return pl.pallas_call(
        body, name=name, grid=(n_tiles,),
        in_specs=[pl.BlockSpec((tm, C), lambda i: (i, 0)),
                  pl.BlockSpec((tm, C), lambda i: (i, 1)),
                  pl.BlockSpec((tm, C), lambda i: (i, 2)),
                  pl.BlockSpec((SC_HALO, C), lambda i: (prev_map(i), 1)),
                  pl.BlockSpec((SC_HALO, C), lambda i: (prev_map(i), 2)),
                  pl.BlockSpec((SC_HALO, C), lambda i: (next_map(i), 0)),
                  pl.BlockSpec((tm, C), lambda i: (i, 0)),
                  pl.BlockSpec((SC_HALO, C), lambda i: (next_map(i), 0)),
                  pl.BlockSpec((8, C), lambda i: (0, 0))],
        out_specs=[pl.BlockSpec((tm, 3 * C), lambda i: (i, 0)),
                   pl.BlockSpec((8, C), lambda i: (0, 0))],
        out_shape=[jax.ShapeDtypeStruct((T, 3 * C), BF16), jax.ShapeDtypeStruct((8, C), F32)],
        scratch_shapes=[pltpu.VMEM((tm + SC_HALO, C), F32), pltpu.VMEM((R, C), F32)],
        compiler_params=_params(1),
    )(z, z, z, z, z, z, dm, dm, w)


def _xa_probs(q_h, k_h):
    s = _dot_nt(q_h, k_h) * (XA_HEAD_DIM ** -0.5)
    p = jnp.exp(s - jnp.max(s, axis=-1, keepdims=True))
    return p * (1.0 / jnp.sum(p, axis=-1, keepdims=True))


def xattn_fwd(q, kv, name, tm=2048):
    T = q.shape[0]
    M = kv.shape[0]
    tm = _tile(T, tm)

    def body(q_ref, k_ref, v_ref, o_ref):
        for h in range(XA_HEADS):
            hs = slice(h * XA_HEAD_DIM, (h + 1) * XA_HEAD_DIM)
            p = _xa_probs(q_ref[:, hs], k_ref[:, hs])
            o_ref[:, hs] = _dot(p.astype(BF16), v_ref[:, hs]).astype(BF16)

    return pl.pallas_call(
        body, name=name, grid=(T // tm,),
        in_specs=[pl.BlockSpec((tm, D_MODEL), lambda i: (i, 0)),
                  pl.BlockSpec((M, D_MODEL), lambda i: (0, 0)),
                  pl.BlockSpec((M, D_MODEL), lambda i: (0, 1))],
        out_specs=pl.BlockSpec((tm, D_MODEL), lambda i: (i, 0)),
        out_shape=jax.ShapeDtypeStruct((T, D_MODEL), BF16),
        compiler_params=_params(1),
    )(q, kv, kv)


def xattn_bwd(q, kv, do, name, tm=2048):
    T = q.shape[0]
    M = kv.shape[0]
    tm = _tile(T, tm)
    scale = XA_HEAD_DIM ** -0.5

    def body(q_ref, k_ref, v_ref, do_ref, dq_ref, dkv_ref):
        @pl.when(pl.program_id(0) == 0)
        def _():
            dkv_ref[...] = jnp.zeros_like(dkv_ref)

        for h in range(XA_HEADS):
            hs = slice(h * XA_HEAD_DIM, (h + 1) * XA_HEAD_DIM)
            vs = slice(D_MODEL + h * XA_HEAD_DIM, D_MODEL + (h + 1) * XA_HEAD_DIM)
            q_h = q_ref[:, hs]
            do_h = do_ref[:, hs]
            p = _xa_probs(q_h, k_ref[:, hs])
            dp = _dot_nt(do_h, v_ref[:, hs])
            ds = (p * (dp - jnp.sum(p * dp, axis=-1, keepdims=True))).astype(BF16)
            dq_ref[:, hs] = (_dot(ds, k_ref[:, hs]) * scale).astype(BF16)
            dkv_ref[:, hs] += _dot_tn(ds, q_h) * scale
            dkv_ref[:, vs] += _dot_tn(p.astype(BF16), do_h)

    return pl.pallas_call(
        body, name=name, grid=(T // tm,),
        in_specs=[pl.BlockSpec((tm, D_MODEL), lambda i: (i, 0)),
                  pl.BlockSpec((M, D_MODEL), lambda i: (0, 0)),
                  pl.BlockSpec((M, D_MODEL), lambda i: (0, 1)),
                  pl.BlockSpec((tm, D_MODEL), lambda i: (i, 0))],
        out_specs=[pl.BlockSpec((tm, D_MODEL), lambda i: (i, 0)),
                   pl.BlockSpec((M, 2 * D_MODEL), lambda i: (0, 0))],
        out_shape=[jax.ShapeDtypeStruct((T, D_MODEL), BF16), jax.ShapeDtypeStruct((M, 2 * D_MODEL), F32)],
        compiler_params=_params(1),
    )(q, kv, kv, do)


def final_loss(h, g, target, name, tm=1024):
    T, K = h.shape
    tm = _tile(T, tm)

    def body(h_ref, g_ref, t_ref, dh_ref, dg_ref, loss_ref):
        @pl.when(pl.program_id(0) == 0)
        def _():
            dg_ref[...] = jnp.zeros_like(dg_ref)
            loss_ref[...] = jnp.zeros_like(loss_ref)

        x = h_ref[...]
        r = lax.rsqrt(jnp.mean(x * x, axis=-1, keepdims=True) + RMS_EPS)
        xh = x * r
        e = xh * g_ref[...] - t_ref[...]
        loss_ref[...] += jnp.zeros((1, 128), F32) + 0.5 * jnp.sum(jnp.mean(e * e, axis=-1, keepdims=True))
        dy = e * (1.0 / K)
        dg_ref[...] += jnp.sum(dy * xh, axis=0, keepdims=True)
        dxh = dy * g_ref[...]
        dh_ref[...] = r * (dxh - xh * jnp.mean(dxh * xh, axis=-1, keepdims=True))

    return pl.pallas_call(
        body, name=name, grid=(T // tm,),
        in_specs=[pl.BlockSpec((tm, K), lambda i: (i, 0)),
                  pl.BlockSpec((1, K), lambda i: (0, 0)),
                  pl.BlockSpec((tm, K), lambda i: (i, 0))],
        out_specs=[pl.BlockSpec((tm, K), lambda i: (i, 0)),
                   pl.BlockSpec((1, K), lambda i: (0, 0)),
                   pl.BlockSpec((1, 128), lambda i: (0, 0))],
        out_shape=[jax.ShapeDtypeStruct((T, K), F32), jax.ShapeDtypeStruct((1, K), F32),
                   jax.ShapeDtypeStruct((1, 128), F32)],
        compiler_params=_params(1),
    )(h, g, target)


def _row(v):
    return v.reshape(1, -1)


def _pad_rows(a, rows):
    return jnp.pad(a, ((0, rows - a.shape[0]), (0, 0)))


def local_step(x, mem, target, P, get_weights, put_grads, u_first=None):
    cw = _pad_rows(P["conv_a_w"], 32)
    scw = _pad_rows(P["sc_conv_w"], 8)
    cb, lg, lb = _row(P["conv_a_b"]), _row(P["conv_a_ln_g"]), _row(P["conv_a_ln_b"])
    sinks = P["swa_sinks"]

    class _Layered:
        def __init__(self, store, name=None):
            self.store, self.name = store, name

        def __getitem__(self, key):
            if self.name is None:
                return self.store[(key, 0)] if key in ("even_w_in", "even_w_out", "odd_w_in", "odd_w_out") \
                    else _Layered(self.store, key)
            return self.store[(self.name, key)]

    store = {}
    W = _Layered(store)
    saved = []
    h = x
    u1 = u_first
    for i in range(2):
        L = f"l{i}"
        new, dep = get_weights("A" if i == 0 else "D", h)
        store.update(new)

        h, u2, s_ffn1 = ffn_forward(h, P["ffn1_norm"][i:i + 1], W["ffn1_w_gu"][i], W["ffn1_w_down"][i], L + "_ffn1",
                                    dep=dep, u=u1, g_next=P["mix_norm"][i:i + 1])
        h1 = h
        if i == 0:
            new, _ = get_weights("B", h)
            store.update(new)
            z = matmul(u2, W["even_w_in"], F32, L + "_mix_in", 768, n_tiles=2)
            kv = matmul(u2, W["even_w_in"], BF16, L + "_mix_kv", 256, n_tiles=1, first_tile=6)
            a, conv_out = conformer_conv_fwd(z, cw, cb, lg, lb, L + "_conv")
            kpad = jnp.pad(kv[:, :2 * HEAD_DIM], ((WINDOW, 0), (0, 0)))
            vpad = jnp.pad(kv[:, 2 * HEAD_DIM:], ((WINDOW, 0), (0, 0)))
            o = swa_fwd(z, kpad, vpad, sinks, L + "_swa")
            m = jnp.concatenate([a, o], axis=-1)
            h, u3 = matmul_residual(m, W["even_w_out"], h1, P["xa_norm"][i:i + 1], L + "_mix_out")
            s_mix = (h1, u2, z, m, kpad, vpad, conv_out)
        else:
            z = matmul(u2, W["odd_w_in"], F32, L + "_mix_in", 1024)
            m = short_conv_fwd(z, scw, L + "_sconv")
            h, u3 = matmul_residual(m, W["odd_w_out"], h1, P["xa_norm"][i:i + 1], L + "_mix_out")
            s_mix = (h1, u2, z, m)
        h2 = h
        kv, umem = norm_matmul(mem, P["xa_mem_norm"][i:i + 1], W["xa_wkv"][i], BF16, L + "_xa_kv", 2 * D_MODEL)
        q, u3 = norm_matmul(h2, P["xa_norm"][i:i + 1], W["xa_wq"][i], BF16, L + "_xa_q", D_MODEL, transposed=False, u=u3)
        o = xattn_fwd(q, kv, L + "_xa")
        h, u4 = matmul_residual(o, W["xa_wo"][i], h2, P["ffn2_norm"][i:i + 1], L + "_xa_out")
        s_xa = (h2, u3, q, o, kv, umem)
        new, _ = get_weights("C" if i == 0 else "E", h)
        store.update(new)
        h, u1, s_ffn2 = ffn_forward(h, P["ffn2_norm"][i:i + 1], W["ffn2_w_gu"][i], W["ffn2_w_down"][i], L + "_ffn2", u=u4,
                                    g_next=P["ffn1_norm"][1:2] if i == 0 else None)
        saved.append((s_ffn1, s_mix, s_xa, s_ffn2))

    dh, d_final, loss = final_loss(h, _row(P["final_norm"]), target, "final_loss")

    names = ("ffn1_w_gu", "ffn1_w_down", "ffn2_w_gu", "ffn2_w_down", "xa_wq", "xa_wkv", "xa_wo", "even_w_in", "even_w_out",
             "odd_w_in", "odd_w_out")
    dW = {k: [None, None] for k in names}
    dP = {k: [None, None] for k in ("ffn1_norm", "mix_norm", "xa_norm", "xa_mem_norm", "ffn2_norm")}
    dP["final_norm"] = d_final.reshape(-1)
    for i in (1, 0):
        L = f"l{i}b"
        s_ffn1, s_mix, s_xa, s_ffn2 = saved[i]

        def keep_ffn2(d_w_gu, d_w_down, i=i):
            dW["ffn2_w_gu"][i], dW["ffn2_w_down"][i] = d_w_gu, d_w_down

        def send_ffn1(d_w_gu, d_w_down, i=i):
            dW["ffn1_w_gu"][i], dW["ffn1_w_down"][i] = d_w_gu, d_w_down
            keys = STAGE_KEYS["D"] + STAGE_KEYS["E"] if i == 1 else STAGE_KEYS["A"]
            return put_grads("D" if i == 1 else "A", {k: dW[k[0]][k[1]] for k in keys})

        dh, dP["ffn2_norm"][i] = ffn_backward(
            dh, s_ffn2, P["ffn2_norm"][i:i + 1], W["ffn2_w_gu"][i], W["ffn2_w_down"][i], L + "_ffn2", emit=keep_ffn2)
        h2, u3, q, o, kv, umem = s_xa
        dW["xa_wo"][i] = matmul_tn(o, dh, L + "_xa_dwo")
        do = matmul_nt(dh, W["xa_wo"][i], BF16, L + "_xa_do")
        dq, dkv = xattn_bwd(q, kv, do, L + "_xa")
        dW["xa_wq"][i] = matmul_tn(u3, dq, L + "_xa_dwq")
        dW["xa_wkv"][i] = matmul_tn(dkv, umem, L + "_xa_dwkv", tk=1024)
        dkv_b = dkv.astype(BF16)
        _, dP["xa_mem_norm"][i] = matmul_norm_bwd(dkv_b, W["xa_wkv"][i], mem, P["xa_mem_norm"][i:i + 1],
                                                  jnp.zeros_like(mem), L + "_xa_dmem")
        dh, dP["xa_norm"][i] = matmul_norm_bwd(dq, W["xa_wq"][i], h2, P["xa_norm"][i:i + 1], dh, L + "_xa_dx",
                                               tm=1024, transposed=False)
        if i == 0:
            h1, u2, z, m, kpad, vpad, conv_out = s_mix
            dW["even_w_out"][0] = matmul_tn(m, dh, L + "_mix_dwo")
            dm = matmul_nt(dh, W["even_w_out"], F32, L + "_mix_dm")
            dz_conv, dcw, dcb, dlg, dlb = conformer_conv_bwd(z, conv_out, dm, cw, lg, lb, L + "_conv")
            dq_s, dkp, dvp, dsk = swa_bwd(z, kpad, vpad, sinks, dm, L + "_swa")
            dz = jnp.concatenate([dz_conv, dq_s, dkp[WINDOW:].astype(BF16), dvp[WINDOW:].astype(BF16)], axis=-1)
            dW["even_w_in"][0] = matmul_tn(dz, u2, L + "_mix_dwi", tk=896)
            dh, dP["mix_norm"][i] = matmul_norm_bwd(dz, W["even_w_in"], h1, P["mix_norm"][i:i + 1], dh, L + "_mix_dx",
                                                    tm=1024)
            dP["conv_a_w"] = dcw[:CONV_A_WIDTH]
            dP["conv_a_b"], dP["conv_a_ln_g"], dP["conv_a_ln_b"] = dcb.reshape(-1), dlg.reshape(-1), dlb.reshape(-1)
            dP["swa_sinks"] = dsk[:, 0]
        else:
            h1, u2, z, m = s_mix
            dW["odd_w_out"][0] = matmul_tn(m, dh, L + "_mix_dwo")
            dm = matmul_nt(dh, W["odd_w_out"], F32, L + "_mix_dm")
            dz, dscw = short_conv_bwd(z, dm, scw, L + "_sconv")
            dW["odd_w_in"][0] = matmul_tn(dz, u2, L + "_mix_dwi", tk=1024)
            dh, dP["mix_norm"][i] = matmul_norm_bwd(dz, W["odd_w_in"], h1, P["mix_norm"][i:i + 1], dh, L + "_mix_dx",
                                                    tm=1024)
            dP["sc_conv_w"] = dscw[:3]
        dep = put_grads("BC", {k: dW[k[0]][k[1]] for k in STAGE_KEYS["B"] + STAGE_KEYS["C"]}) if i == 0 else None
        dh, dP["ffn1_norm"][i] = ffn_backward(
            dh, s_ffn1, P["ffn1_norm"][i:i + 1], W["ffn1_w_gu"][i], W["ffn1_w_down"][i], L + "_ffn1", dep=dep,
            emit=send_ffn1)
    for k in ("ffn1_norm", "mix_norm", "xa_norm", "xa_mem_norm", "ffn2_norm"):
        dP[k] = jnp.concatenate(dP[k], axis=0)
    return loss, dh, dP


def _mesh_pos():
    return lax.axis_index("x"), lax.axis_index("y"), lax.axis_index("c")


def _flat_index(px, py, pc):
    return 4 * px + 2 * py + pc


def all_gather(blob, name, dep=None):
    R, C = blob.shape

    def kern(x_ref, out_ref, send_sems, recv_sems, local_sem):
        x, y, c = _mesh_pos()
        me, sibling = (x, y, c), (x, y, 1 - c)
        chips = [(1 - x, y), (x, 1 - y), (1 - x, 1 - y)]

        def slot(px, py, pc):
            return out_ref.at[_flat_index(px, py, pc)]

        def copy(k, block, to, src=None):
            return pltpu.make_async_remote_copy(
                src_ref=slot(*block) if src is None else src, dst_ref=slot(*block),
                send_sem=send_sems.at[k], recv_sem=recv_sems.at[k],
                device_id=to, device_id_type=pl.DeviceIdType.MESH)

        mine = pltpu.make_async_copy(x_ref, slot(*me), local_sem)
        mine.start()
        first = [copy(0, me, sibling, src=x_ref)]
        first += [copy(1 + j, me, (*chip, c), src=x_ref) for j, chip in enumerate(chips)]
        for cp in first:
            cp.start()
        passed = [copy(4 + j, (*chip, c), sibling) for j, chip in enumerate(chips)]
        for j, chip in enumerate(chips):
            copy(1 + j, (*chip, c), me).wait_recv()
            passed[j].start()
        copy(0, sibling, me).wait_recv()
        for j, chip in enumerate(chips):
            copy(4 + j, (*chip, 1 - c), me).wait_recv()
        for cp in first + passed:
            cp.wait_send()
        mine.wait()

    body, dep_spec, dep_arg = _with_dep(kern, 1, dep)
    return pl.pallas_call(
        body, name=name,
        out_shape=jax.ShapeDtypeStruct((N_DEV, R, C), blob.dtype),
        in_specs=[ANY_SPEC] + dep_spec,
        out_specs=ANY_SPEC,
        scratch_shapes=[pltpu.SemaphoreType.DMA((7,)), pltpu.SemaphoreType.DMA((7,)), pltpu.SemaphoreType.DMA],
    )(blob, *dep_arg)


def all_gather_with_norm(blob, h, g, name, tm=1024):
    R, C = blob.shape
    T, K = h.shape
    tm = _tile(T, tm)
    n_steps = T // tm

    def body(x_ref, h_ref, g_ref, out_ref, u_ref, send_sems, recv_sems, local_sem):
        i = pl.program_id(0)
        x, y, c = _mesh_pos()
        me, sibling = (x, y, c), (x, y, 1 - c)
        chips = [(1 - x, y), (x, 1 - y), (1 - x, 1 - y)]

        def slot(px, py, pc):
            return out_ref.at[_flat_index(px, py, pc)]

        def copy(k, block, to, src=None):
            return pltpu.make_async_remote_copy(
                src_ref=slot(*block) if src is None else src, dst_ref=slot(*block),
                send_sem=send_sems.at[k], recv_sem=recv_sems.at[k],
                device_id=to, device_id_type=pl.DeviceIdType.MESH)

        mine = pltpu.make_async_copy(x_ref, slot(*me), local_sem)
        first = [copy(0, me, sibling, src=x_ref)]
        first += [copy(1 + j, me, (*chip, c), src=x_ref) for j, chip in enumerate(chips)]
        passed = [copy(4 + j, (*chip, c), sibling) for j, chip in enumerate(chips)]

        @pl.when(i == 0)
        def _():
            mine.start()
            for cp in first:
                cp.start()

        xt = h_ref[...]
        r = lax.rsqrt(jnp.mean(xt * xt, axis=-1, keepdims=True) + RMS_EPS)
        u_ref[...] = ((xt * r) * g_ref[...]).astype(BF16)

        @pl.when(i == n_steps - 1)
        def _():
            for j, chip in enumerate(chips):
                copy(1 + j, (*chip, c), me).wait_recv()
                passed[j].start()
            copy(0, sibling, me).wait_recv()
            for j, chip in enumerate(chips):
                copy(4 + j, (*chip, 1 - c), me).wait_recv()
            for cp in first + passed:
                cp.wait_send()
            mine.wait()

    return pl.pallas_call(
        body, name=name, grid=(n_steps,),
        out_shape=[jax.ShapeDtypeStruct((N_DEV, R, C), blob.dtype), jax.ShapeDtypeStruct((T, K), BF16)],
        in_specs=[ANY_SPEC, pl.BlockSpec((tm, K), lambda i: (i, 0)), pl.BlockSpec((1, K), lambda i: (0, 0))],
        out_specs=[ANY_SPEC, pl.BlockSpec((tm, K), lambda i: (i, 0))],
        scratch_shapes=[pltpu.SemaphoreType.DMA((7,)), pltpu.SemaphoreType.DMA((7,)), pltpu.SemaphoreType.DMA],
        compiler_params=_params(1),
    )(blob, h, g)


HBM_SPEC = pl.BlockSpec(memory_space=pltpu.HBM)
SEM_SPEC = pl.BlockSpec(memory_space=pltpu.SEMAPHORE)
DATAFLOW_EFFECT = pltpu.SideEffectType.DATAFLOW_SIDE_EFFECTING


def _peers(x, y, c):
    out = []
    for k in range(1, N_DEV):
        pos = (1 - x if k & 4 else x, 1 - y if k & 2 else y, 1 - c if k & 1 else c)
        out.append((_flat_index(*pos), pos))
    return out


def _exchange_copy(src_ref, land_ref, send_sems, recv_sems, j, me, peer_idx, peer, scatter):
    return pltpu.make_async_remote_copy(
        src_ref=src_ref.at[peer_idx] if scatter else src_ref, dst_ref=land_ref.at[me],
        send_sem=send_sems.at[j], recv_sem=recv_sems.at[j], device_id=peer, device_id_type=pl.DeviceIdType.MESH)


def exchange_start(srcs, lands, scatter, after, name):
    n = len(srcs)
    n_after = len(after)

    def body(*refs):
        src_refs, land_refs = refs[:n], refs[n:2 * n]
        outs = refs[2 * n + n_after:]
        send_sems, recv_sems, token = outs[:n], outs[n:2 * n], outs[4 * n]
        x, y, c = _mesh_pos()
        me = _flat_index(x, y, c)
        for g in range(n):
            for j, (peer_idx, peer) in enumerate(_peers(x, y, c)):
                _exchange_copy(src_refs[g], land_refs[g], send_sems[g], recv_sems[g], j, me, peer_idx, peer, scatter).start()
        token[...] = jnp.zeros_like(token)

    hbm = lambda a: pltpu.with_memory_space_constraint(a, pltpu.HBM)
    res = pl.pallas_call(
        body, name=name,
        out_shape=(*[pltpu.SemaphoreType.DMA((N_DEV - 1,))] * (2 * n),
                   *[pltpu.HBM(a.shape, a.dtype) for a in srcs], *[pltpu.HBM(a.shape, a.dtype) for a in lands],
                   jax.ShapeDtypeStruct((8, 128), F32)),
        in_specs=[HBM_SPEC] * (2 * n) + [ANY_SPEC] * n_after,
        out_specs=(*[SEM_SPEC] * (2 * n), *[HBM_SPEC] * (2 * n), pl.BlockSpec(memory_space=pltpu.VMEM)),
        input_output_aliases={i: 2 * n + i for i in range(2 * n)},
        compiler_params=pltpu.CompilerParams(has_side_effects=DATAFLOW_EFFECT),
    )(*[hbm(a) for a in srcs], *[hbm(a) for a in lands], *after)
    handles = [(res[g], res[n + g], res[2 * n + g], res[3 * n + g]) for g in range(n)]
    return handles, res[4 * n]


def exchange_wait(handles, scatter, after, name):
    n = len(handles)

    def body(*refs):
        src_refs, land_refs = refs[:n], refs[n:2 * n]
        send_sems, recv_sems = refs[2 * n:3 * n], refs[3 * n:4 * n]
        x, y, c = _mesh_pos()
        me = _flat_index(x, y, c)
        for g in range(n):
            for j, (peer_idx, peer) in enumerate(_peers(x, y, c)):
                mine = _exchange_copy(src_refs[g], land_refs[g], send_sems[g], recv_sems[g], j, me, peer_idx, peer,
                                      scatter)
                mine.wait_send()
                theirs = pltpu.make_async_remote_copy(
                    src_ref=src_refs[g].at[me] if scatter else src_refs[g], dst_ref=land_refs[g].at[peer_idx],
                    send_sem=send_sems[g].at[j], recv_sem=recv_sems[g].at[j], device_id=peer,
                    device_id_type=pl.DeviceIdType.MESH)
                theirs.wait_recv()

    srcs = [h[2] for h in handles]
    lands = [h[3] for h in handles]
    res = pl.pallas_call(
        body, name=name,
        out_shape=tuple(pltpu.HBM(a.shape, a.dtype) for a in srcs + lands),
        in_specs=[HBM_SPEC] * (2 * n) + [SEM_SPEC] * (2 * n) + [ANY_SPEC],
        out_specs=tuple([HBM_SPEC] * (2 * n)),
        input_output_aliases={i: i for i in range(2 * n)},
        compiler_params=pltpu.CompilerParams(has_side_effects=DATAFLOW_EFFECT),
    )(*srcs, *lands, *[h[0] for h in handles], *[h[1] for h in handles], after)
    return [(res[g], res[n + g]) for g in range(n)]


def ordered_sum(parts, name, tr=512):
    n, R, C = parts.shape
    tr = next((t for t in range(min(tr, R), 15, -16) if R % t == 0), R)

    def body(p_ref, o_ref):
        acc = p_ref[0].astype(F32)
        for j in range(1, n):
            acc = acc + p_ref[j].astype(F32)
        o_ref[...] = acc

    return pl.pallas_call(
        body, name=name, grid=(R // tr,),
        in_specs=[pl.BlockSpec((n, tr, C), lambda i: (0, i, 0))],
        out_specs=pl.BlockSpec((tr, C), lambda i: (i, 0)),
        out_shape=jax.ShapeDtypeStruct((R, C), F32),
        compiler_params=_params(1),
    )(parts)


def adamw(w, g, m, v, name, tr=256):
    R, C = w.shape
    tr = next((t for t in range(tr, 7, -8) if R % t == 0), R)
    c1 = 1.0 - ADAM_B1 ** ADAM_STEP
    c2 = 1.0 - ADAM_B2 ** ADAM_STEP

    def body(w_ref, g_ref, m_ref, v_ref, d_ref, mo_ref, vo_ref):
        grad = g_ref[...]
        m2 = ADAM_B1 * m_ref[...] + (1.0 - ADAM_B1) * grad
        v2 = ADAM_B2 * v_ref[...] + (1.0 - ADAM_B2) * (grad * grad)
        mo_ref[...] = m2
        vo_ref[...] = v2
        d_ref[...] = -ADAM_LR * ((m2 / c1) / (jnp.sqrt(v2 / c2) + ADAM_EPS) + ADAM_WD * w_ref[...])

    spec = pl.BlockSpec((tr, C), lambda i: (i, 0))
    return pl.pallas_call(
        body, name=name, grid=(R // tr,),
        in_specs=[spec] * 4, out_specs=[spec] * 3,
        out_shape=[jax.ShapeDtypeStruct((R, C), F32)] * 3,
        compiler_params=_params(1),
    )(w, g, m, v)


WEIGHT_NAMES = ("ffn1_norm", "ffn1_w_gu", "ffn1_w_down", "mix_norm", "even_w_in", "conv_a_w", "conv_a_b", "conv_a_ln_g",
                "conv_a_ln_b", "swa_sinks", "even_w_out", "odd_w_in", "sc_conv_w", "odd_w_out", "xa_norm", "xa_mem_norm",
                "xa_wq", "xa_wkv", "xa_wo", "ffn2_norm", "ffn2_w_gu", "ffn2_w_down", "final_norm")
BLOB_COLS = 1024
SMALL_ROWS = (("ffn1_norm", 0, 2), ("mix_norm", 2, 2), ("xa_norm", 4, 2), ("xa_mem_norm", 6, 2), ("ffn2_norm", 8, 2),
              ("final_norm", 10, 1))
ROW_CONV_B_LNG = 11
ROW_LNB_SINKS_LOSS = 12
LOSS_COL = 512 + SWA_HEADS
ROW_SC_CONV = 13
ROW_CONV_W = 16
SMALL_BLOB_ROWS = 32
SMALL_ADAM_ROWS = 16


def _small_blob(v):
    rows = [v[n].reshape(-1, D_MODEL) for n, _, _ in SMALL_ROWS]
    rows.append(jnp.concatenate([v["conv_a_b"].reshape(-1), v["conv_a_ln_g"].reshape(-1)]).reshape(1, D_MODEL))
    tail = jnp.zeros((D_MODEL - 512 - SWA_HEADS,), F32)
    if "loss" in v:
        tail = tail.at[0].set(v["loss"])
    rows.append(jnp.concatenate([v["conv_a_ln_b"].reshape(-1), v["swa_sinks"].reshape(-1), tail]).reshape(1, D_MODEL))
    rows.append(jnp.zeros((SMALL_ADAM_ROWS - ROW_SC_CONV, D_MODEL), F32))
    return jnp.concatenate(rows, axis=0)


def _small_unblob(b, shapes):
    out = {n: b[r:r + k].reshape(shapes[n]) for n, r, k in SMALL_ROWS}
    out["conv_a_b"] = b[ROW_CONV_B_LNG, :512].reshape(shapes["conv_a_b"])
    out["conv_a_ln_g"] = b[ROW_CONV_B_LNG, 512:].reshape(shapes["conv_a_ln_g"])
    out["conv_a_ln_b"] = b[ROW_LNB_SINKS_LOSS, :512].reshape(shapes["conv_a_ln_b"])
    out["swa_sinks"] = b[ROW_LNB_SINKS_LOSS, 512:512 + SWA_HEADS].reshape(shapes["swa_sinks"])
    return out


def kernel(x, mem, ffn1_norm, ffn1_w_gu, ffn1_w_down, mix_norm, even_w_in, conv_a_w, conv_a_b, conv_a_ln_g, conv_a_ln_b, swa_sinks, even_w_out, odd_w_in, sc_conv_w, odd_w_out, xa_norm, xa_mem_norm, xa_wq, xa_wkv, xa_wo, ffn2_norm, ffn2_w_gu, ffn2_w_down, final_norm, loss_target, m_ffn1_norm, m_ffn1_w_gu, m_ffn1_w_down, m_mix_norm, m_even_w_in, m_conv_a_w, m_conv_a_b, m_conv_a_ln_g, m_conv_a_ln_b, m_swa_sinks, m_even_w_out, m_odd_w_in, m_sc_conv_w, m_odd_w_out, m_xa_norm, m_xa_mem_norm, m_xa_wq, m_xa_wkv, m_xa_wo, m_ffn2_norm, m_ffn2_w_gu, m_ffn2_w_down, m_final_norm, v_ffn1_norm, v_ffn1_w_gu, v_ffn1_w_down, v_mix_norm, v_even_w_in, v_conv_a_w, v_conv_a_b, v_conv_a_ln_g, v_conv_a_ln_b, v_swa_sinks, v_even_w_out, v_odd_w_in, v_sc_conv_w, v_odd_w_out, v_xa_norm, v_xa_mem_norm, v_xa_wq, v_xa_wkv, v_xa_wo, v_ffn2_norm, v_ffn2_w_gu, v_ffn2_w_down, v_final_norm):
    w = dict(ffn1_norm=ffn1_norm, ffn1_w_gu=ffn1_w_gu, ffn1_w_down=ffn1_w_down, mix_norm=mix_norm, even_w_in=even_w_in,
             conv_a_w=conv_a_w, conv_a_b=conv_a_b, conv_a_ln_g=conv_a_ln_g, conv_a_ln_b=conv_a_ln_b, swa_sinks=swa_sinks,
             even_w_out=even_w_out, odd_w_in=odd_w_in, sc_conv_w=sc_conv_w, odd_w_out=odd_w_out, xa_norm=xa_norm,
             xa_mem_norm=xa_mem_norm, xa_wq=xa_wq, xa_wkv=xa_wkv, xa_wo=xa_wo, ffn2_norm=ffn2_norm, ffn2_w_gu=ffn2_w_gu,
             ffn2_w_down=ffn2_w_down, final_norm=final_norm)
    m = dict(ffn1_norm=m_ffn1_norm, ffn1_w_gu=m_ffn1_w_gu, ffn1_w_down=m_ffn1_w_down, mix_norm=m_mix_norm,
             even_w_in=m_even_w_in, conv_a_w=m_conv_a_w, conv_a_b=m_conv_a_b, conv_a_ln_g=m_conv_a_ln_g,
             conv_a_ln_b=m_conv_a_ln_b, swa_sinks=m_swa_sinks, even_w_out=m_even_w_out, odd_w_in=m_odd_w_in,
             sc_conv_w=m_sc_conv_w, odd_w_out=m_odd_w_out, xa_norm=m_xa_norm, xa_mem_norm=m_xa_mem_norm, xa_wq=m_xa_wq,
             xa_wkv=m_xa_wkv, xa_wo=m_xa_wo, ffn2_norm=m_ffn2_norm, ffn2_w_gu=m_ffn2_w_gu, ffn2_w_down=m_ffn2_w_down,
             final_norm=m_final_norm)
    v = dict(ffn1_norm=v_ffn1_norm, ffn1_w_gu=v_ffn1_w_gu, ffn1_w_down=v_ffn1_w_down, mix_norm=v_mix_norm,
             even_w_in=v_even_w_in, conv_a_w=v_conv_a_w, conv_a_b=v_conv_a_b, conv_a_ln_g=v_conv_a_ln_g,
             conv_a_ln_b=v_conv_a_ln_b, swa_sinks=v_swa_sinks, even_w_out=v_even_w_out, odd_w_in=v_odd_w_in,
             sc_conv_w=v_sc_conv_w, odd_w_out=v_odd_w_out, xa_norm=v_xa_norm, xa_mem_norm=v_xa_mem_norm, xa_wq=v_xa_wq,
             xa_wkv=v_xa_wkv, xa_wo=v_xa_wo, ffn2_norm=v_ffn2_norm, ffn2_w_gu=v_ffn2_w_gu, ffn2_w_down=v_ffn2_w_down,
             final_norm=v_final_norm)
    me = _flat_index(*_mesh_pos())

    conv_blob = jnp.concatenate([w["conv_a_w"].reshape(-1), w["sc_conv_w"].reshape(-1),
                                 jnp.zeros((8 * 1024 - 31 * 64 - 3 * 128,), F32)]).reshape(8, 1024)
    conv_bits = lax.bitcast_convert_type(conv_blob, BF16).reshape(16, BLOB_COLS)

    def shard_rows(n, l):
        return (w[n][l].T if SPLIT_AXIS[n] == 1 else w[n][l]).astype(BF16)

    def with_own(land, own):
        return lax.dynamic_update_slice(land, own[None], (me, 0, 0))

    a_keys = STAGE_KEYS["A"]
    gathered_a, u_first = all_gather_with_norm(
        jnp.concatenate([shard_rows(n, l) for n, l in a_keys] + [conv_bits], axis=0), x[0], ffn1_norm[0:1],
        "gather_weights_a")
    conv_all = lax.bitcast_convert_type(gathered_a[:, -16:, :].reshape(N_DEV, 8, BLOB_COLS, 2), F32).reshape(N_DEV, -1)
    conv_a_full = jnp.transpose(conv_all[:, :31 * 64].reshape(N_DEV, 31, 64), (1, 0, 2)).reshape(31, 512)
    sc_full = jnp.transpose(conv_all[:, 31 * 64:31 * 64 + 3 * 128].reshape(N_DEV, 3, 128), (1, 0, 2)).reshape(3, 1024)
    later = ("B", "C", "D", "E")
    later_keys = [k for s in later for k in STAGE_KEYS[s]]
    shards = [shard_rows(n, l) for n, l in later_keys]
    lands = [lax.empty((N_DEV,) + s.shape, BF16) for s in shards]
    handles, weight_token = exchange_start(shards, lands, False, [gathered_a], "gather_start")
    weight_handles = dict(zip(later_keys, handles))

    def get_weights(stage, after):
        keys = STAGE_KEYS[stage]
        if stage == "A":
            out, off = {}, 0
            for n, l in keys:
                rows = w[n].shape[2] if SPLIT_AXIS[n] == 1 else w[n].shape[1]
                out[(n, l)] = gathered_a[:, off:off + rows, :].reshape(N_DEV * rows, BLOB_COLS)
                off += rows
            return out, weight_token
        got = exchange_wait([weight_handles[k] for k in keys], False, after, "gather_wait_" + stage.lower())
        return {k: with_own(land, own).reshape(-1, BLOB_COLS) for k, (own, land) in zip(keys, got)}, None

    grad_handles = {}

    def put_grads(stage, dws):
        srcs = [dw.reshape(N_DEV, -1, BLOB_COLS) for dw in dws.values()]
        handles, token = exchange_start(srcs, [lax.empty(s.shape, BF16) for s in srcs], True, [],
                                        "scatter_start_" + stage.lower())
        grad_handles[stage] = (handles, tuple(dws))
        return token

    P = dict(ffn1_norm=ffn1_norm, mix_norm=mix_norm, xa_norm=xa_norm, xa_mem_norm=xa_mem_norm, ffn2_norm=ffn2_norm,
             final_norm=final_norm, conv_a_w=conv_a_full, conv_a_b=conv_a_b[0], conv_a_ln_g=conv_a_ln_g[0],
             conv_a_ln_b=conv_a_ln_b[0], swa_sinks=swa_sinks[0], sc_conv_w=sc_full)

    loss_part, grad_x, dP = local_step(x[0], mem[0], loss_target[0], P, get_weights, put_grads, u_first=u_first)

    def finish_grads(stage, after):
        handles, keys = grad_handles[stage]
        got = exchange_wait(handles, True, after, "scatter_wait_" + stage.lower())
        out = {}
        for (n, l), (src, land) in zip(keys, got):
            own = lax.dynamic_slice(src, (me, 0, 0), (1,) + src.shape[1:])[0]
            part = ordered_sum(with_own(land, own), f"sum_grads_{n}_{l}")
            out[(n, l)] = part.T if SPLIT_AXIS[n] == 1 else part
        return out

    layer_grads = {**finish_grads("D", grad_x), **finish_grads("BC", grad_x)}

    grads, delta, new_m, new_v = {}, {}, {}, {}

    def update(n):
        shp = w[n].shape
        two_d = (shp[0] * shp[1], shp[2])
        d_, m_, v_ = adamw(w[n].reshape(two_d), grads[n].reshape(two_d), m[n].reshape(two_d), v[n].reshape(two_d),
                           "adamw_" + n)
        delta[n], new_m[n], new_v[n] = d_.reshape(shp), m_.reshape(shp), v_.reshape(shp)

    first_stage = tuple(n for n, _ in STAGE_KEYS["A"])
    for n in SPLIT_AXIS:
        if n not in first_stage:
            grads[n] = jnp.stack([layer_grads[(n, l)] for l in range(w[n].shape[0])], axis=0)
            update(n)
    layer_grads.update(finish_grads("A", delta["ffn2_w_gu"]))

    dP = dict(dP, loss=loss_part[0, 0])
    small = jnp.concatenate([
        _small_blob(dP)[:ROW_SC_CONV], dP["sc_conv_w"],
        jnp.concatenate([dP["conv_a_w"].reshape(-1), jnp.zeros((512,), F32)]).reshape(16, D_MODEL)], axis=0)
    small_all = all_gather(small, "gather_small_grads", dep=layer_grads[STAGE_KEYS["A"][-1]])
    small_sum = ordered_sum(small_all, "sum_small_grads", tr=SMALL_BLOB_ROWS)
    loss = small_sum[ROW_LNB_SINKS_LOSS, LOSS_COL]
    grads.update(_small_unblob(small_sum, {n: w[n].shape for n in WEIGHT_NAMES}))
    sc_g = small_sum[ROW_SC_CONV:ROW_SC_CONV + 3]
    grads["sc_conv_w"] = lax.dynamic_slice(sc_g, (0, me * 128), (3, 128)).reshape(w["sc_conv_w"].shape)
    cw_g = small_sum[ROW_CONV_W:].reshape(-1)[:31 * 512].reshape(31, 512)
    grads["conv_a_w"] = lax.dynamic_slice(cw_g, (0, me * 64), (31, 64)).reshape(w["conv_a_w"].shape)

    update("conv_a_w")
    update("sc_conv_w")
    for n in first_stage:
        grads[n] = jnp.stack([layer_grads[(n, l)] for l in range(w[n].shape[0])], axis=0)
        update(n)
    d_, m_, v_ = adamw(_small_blob(w), small_sum[:SMALL_ADAM_ROWS], _small_blob(m), _small_blob(v), "adamw_small",
                       tr=SMALL_ADAM_ROWS)
    shapes = {n: w[n].shape for n in WEIGHT_NAMES}
    delta.update(_small_unblob(d_, shapes))
    new_m.update(_small_unblob(m_, shapes))
    new_v.update(_small_unblob(v_, shapes))

    return (loss, grad_x[None], *[grads[n] for n in WEIGHT_NAMES], *[delta[n] for n in WEIGHT_NAMES],
            *[new_m[n] for n in WEIGHT_NAMES], *[new_v[n] for n in WEIGHT_NAMES])
```
